```python
import math
import jax
import jax.numpy as jnp
from jax import lax

D_MODEL = 1024
BATCH = 16
SEQ = 4096
DEPTH = 1

D_FF = 2816
FFN_RESIDUAL_WEIGHT = 0.5
SSM_D_INNER = 2 * D_MODEL
SSM_HEAD_DIM = 64
SSM_HEADS = SSM_D_INNER // SSM_HEAD_DIM
SSM_GROUPS = 4
SSM_STATE = 128
SSM_CONV = 4
SSM_CHUNK = 128
SSM_CONV_DIM = SSM_D_INNER + 2 * SSM_GROUPS * SSM_STATE
ATTN_Q_HEADS = 16
ATTN_KV_HEADS = 4
ATTN_HEAD_DIM = 64
ATTN_WINDOW = 128
ATTN_BLOCK = 128
ATTN_Q_DIM = ATTN_Q_HEADS * ATTN_HEAD_DIM
ATTN_KV_DIM = ATTN_KV_HEADS * ATTN_HEAD_DIM
REL_BUCKETS = 32
REL_MAX_DISTANCE = 128
RMS_EPS = 1e-6
IN_SPLITS = (D_MODEL, D_MODEL, SSM_D_INNER, SSM_CONV_DIM, SSM_HEADS, ATTN_Q_DIM, ATTN_KV_DIM, ATTN_KV_DIM)
IN_COLS = D_MODEL + D_MODEL + SSM_D_INNER + SSM_CONV_DIM + SSM_HEADS + ATTN_Q_DIM + 2 * ATTN_KV_DIM

kernel_name = 'hybrid_ssd_swa_sink_macaron_block'


def rmsnorm(x, g):
    xf = x.astype(jnp.float32)
    y = xf * lax.rsqrt(jnp.mean(xf * xf, axis=-1, keepdims=True) + RMS_EPS)
    return (y * g.astype(jnp.float32)).astype(x.dtype)


def swiglu(x, w_gate, w_up, w_down):
    return (jax.nn.silu(x @ w_gate) * (x @ w_up)) @ w_down


def t5_causal_bucket(dist):
    max_exact = REL_BUCKETS // 2
    d = jnp.maximum(dist, 1).astype(jnp.float32)
    large = max_exact + (jnp.log(d / max_exact) / math.log(REL_MAX_DISTANCE / max_exact)
                         * (REL_BUCKETS - max_exact)).astype(jnp.int32)
    large = jnp.minimum(large, REL_BUCKETS - 1)
    return jnp.where(dist < max_exact, dist, large)


def causal_depthwise_conv(x, w, bias):
    y = lax.conv_general_dilated(
        x, w[:, None, :].astype(x.dtype), window_strides=(1,), padding=[(SSM_CONV - 1, 0)],
        dimension_numbers=('NWC', 'WIO', 'NWC'), feature_group_count=x.shape[-1])
    return y + bias.astype(x.dtype)


def ssd_chunked_scan(xs, dt, a, bm, cm):
    b, s = xs.shape[:2]
    nc = s // SSM_CHUNK
    q = SSM_CHUNK
    r = SSM_HEADS // SSM_GROUPS
    f32 = jnp.float32
    x = (xs.astype(f32) * dt[..., None]).reshape(b, nc, q, SSM_GROUPS, r, SSM_HEAD_DIM)
    a_cs = jnp.cumsum((dt * a).reshape(b, nc, q, SSM_GROUPS, r), axis=2)
    bc = bm.astype(f32).reshape(b, nc, q, SSM_GROUPS, SSM_STATE)
    cc = cm.astype(f32).reshape(b, nc, q, SSM_GROUPS, SSM_STATE)
    causal = jnp.tril(jnp.ones((q, q), dtype=bool))[:, :, None, None]
    seg = a_cs[:, :, :, None] - a_cs[:, :, None]
    decay = jnp.exp(jnp.where(causal, seg, -jnp.inf))
    scores = jnp.einsum('bcign,bcjgn->bcijg', cc, bc)
    y_diag = jnp.einsum('bcijgr,bcjgrp->bcigrp', scores[..., None] * decay, x)
    x_w = x * jnp.exp(a_cs[:, :, -1:] - a_cs)[..., None]
    states = jnp.einsum('bcjgn,bcjgrp->bcgrpn', bc, x_w)
    chunk_decay = jnp.exp(a_cs[:, :, -1])

    def step(h, inp):
        s_c, d_c = inp
        return h * d_c[..., None, None] + s_c, h

    h0 = jnp.zeros((b, SSM_GROUPS, r, SSM_HEAD_DIM, SSM_STATE), f32)
    _, prev = lax.scan(step, h0, (jnp.moveaxis(states, 1, 0), jnp.moveaxis(chunk_decay, 1, 0)))
    prev = jnp.moveaxis(prev, 0, 1)
    y_off = jnp.einsum('bcign,bcgrpn->bcigrp', cc, prev) * jnp.exp(a_cs)[..., None]
    return (y_diag + y_off).reshape(b, s, SSM_HEADS, SSM_HEAD_DIM)


def ssd_branch(z, xbc, dt_raw, conv_w, conv_b, dt_bias, a_log, d_skip, norm_g):
    b, s = z.shape[:2]
    f32 = jnp.float32
    xbc = jax.nn.silu(causal_depthwise_conv(xbc, conv_w, conv_b))
    xs = xbc[..., :SSM_D_INNER].reshape(b, s, SSM_HEADS, SSM_HEAD_DIM)
    bm = xbc[..., SSM_D_INNER:SSM_D_INNER + SSM_GROUPS * SSM_STATE].reshape(b, s, SSM_GROUPS, SSM_STATE)
    cm = xbc[..., SSM_D_INNER + SSM_GROUPS * SSM_STATE:].reshape(b, s, SSM_GROUPS, SSM_STATE)
    dt = jax.nn.softplus(dt_raw.astype(f32) + dt_bias.astype(f32))
    a = -jnp.exp(a_log.astype(f32))
    y = ssd_chunked_scan(xs, dt, a, bm, cm)
    y = y + d_skip.astype(f32)[:, None] * xs.astype(f32)
    yg = (y.reshape(b, s, SSM_D_INNER) * jax.nn.silu(z.astype(f32))).reshape(b, s, SSM_GROUPS, -1)
    yg = yg * lax.rsqrt(jnp.mean(yg * yg, axis=-1, keepdims=True) + RMS_EPS)
    return (yg.reshape(b, s, SSM_D_INNER) * norm_g.astype(f32)).astype(z.dtype)


def swa_sink_attention(q, k, v, sinks, rel_table):
    b, s = q.shape[:2]
    nb = s // ATTN_BLOCK
    r = ATTN_Q_HEADS // ATTN_KV_HEADS
    f32 = jnp.float32
    blk = ATTN_BLOCK
    qb = q.astype(f32).reshape(b, nb, blk, ATTN_KV_HEADS, r, ATTN_HEAD_DIM) * (ATTN_HEAD_DIM ** -0.5)
    kb = k.astype(f32).reshape(b, nb, blk, ATTN_KV_HEADS, ATTN_HEAD_DIM)
    vb = v.astype(f32).reshape(b, nb, blk, ATTN_KV_HEADS, ATTN_HEAD_DIM)

    def band(t):
        prev = jnp.concatenate([jnp.zeros_like(t[:, :1]), t[:, :-1]], axis=1)
        return jnp.concatenate([prev, t], axis=2)

    kk, vv = band(kb), band(vb)
    qi = jnp.arange(blk)[:, None]
    kj = jnp.arange(2 * blk)[None, :]
    dist = qi + blk - kj
    in_window = (dist >= 0) & (dist < ATTN_WINDOW)
    key_exists = (jnp.arange(nb)[:, None, None] > 0) | (kj >= blk)[None]
    mask = in_window[None] & key_exists
    bias = rel_table.astype(f32)[t5_causal_bucket(jnp.maximum(dist, 0))]
    bias = jnp.transpose(bias, (2, 0, 1)).reshape(ATTN_KV_HEADS, r, 1, blk, 2 * blk)
    logits = jnp.einsum('bnikrd,bnjkd->bkrnij', qb, kk) + bias
    logits = jnp.where(mask, logits, -jnp.inf)
    sink = sinks.astype(f32).reshape(ATTN_KV_HEADS, r, 1, 1)
    m = jnp.maximum(logits.max(axis=-1), sink)
    p = jnp.exp(logits - m[..., None])
    p = p / (p.sum(axis=-1) + jnp.exp(sink - m))[..., None]
    o = jnp.einsum('bkrnij,bnjkd->bnikrd', p, vv)
    return o.reshape(b, s, ATTN_Q_DIM).astype(q.dtype)


def hybrid_mixer(u, w_in, conv_w, conv_b, dt_bias, a_log, d_skip, ssm_norm_g, w_ssm_proj,
                 attn_sinks, rel_table, w_attn_proj, w_out):
    proj = u @ w_in
    parts = []
    start = 0
    for size in IN_SPLITS:
        parts.append(proj[..., start:start + size])
        start += size
    g_ssm, g_attn, z, xbc, dt_raw, q, k, v = parts
    y_ssm = ssd_branch(z, xbc, dt_raw, conv_w, conv_b, dt_bias, a_log, d_skip, ssm_norm_g) @ w_ssm_proj
    y_attn = swa_sink_attention(q, k, v, attn_sinks, rel_table) @ w_attn_proj
    merged = jax.nn.sigmoid(g_ssm) * y_ssm + jax.nn.sigmoid(g_attn) * y_attn
    return merged @ w_out


def _fwd_setup_inputs(seed: int = 0) -> dict:
    key = jax.random.key(seed)
    ks = jax.random.split(key, 32)
    f32 = jnp.float32

    def dense(k, fan_in, fan_out):
        return jax.random.normal(k, (DEPTH, fan_in, fan_out), f32) * fan_in ** -0.5

    def gain(k, n):
        return 1.0 + 0.05 * jax.random.normal(k, (DEPTH, n), f32)

    dt0 = jnp.exp(jax.random.uniform(ks[20], (DEPTH, SSM_HEADS), f32, math.log(1e-3), math.log(1e-1)))
    return {
        'x': jax.random.normal(ks[0], (BATCH, SEQ, D_MODEL), f32),
        'ffn1_pre_g': gain(ks[1], D_MODEL),
        'ffn1_w_gate': dense(ks[2], D_MODEL, D_FF),
        'ffn1_w_up': dense(ks[3], D_MODEL, D_FF),
        'ffn1_w_down': dense(ks[4], D_FF, D_MODEL),
        'ffn1_post_g': gain(ks[5], D_MODEL),
        'mix_pre_g': gain(ks[6], D_MODEL),
        'w_in': dense(ks[7], D_MODEL, IN_COLS),
        'conv_w': jax.random.normal(ks[8], (DEPTH, SSM_CONV, SSM_CONV_DIM), f32) * SSM_CONV ** -0.5,
        'conv_b': 0.02 * jax.random.normal(ks[9], (DEPTH, SSM_CONV_DIM), f32),
        'dt_bias': dt0 + jnp.log(-jnp.expm1(-dt0)),
        'a_log': jnp.log(jax.random.uniform(ks[10], (DEPTH, SSM_HEADS), f32, 1.0, 16.0)),
        'd_skip': 1.0 + 0.1 * jax.random.normal(ks[11], (DEPTH, SSM_HEADS), f32),
        'ssm_norm_g': gain(ks[12], SSM_D_INNER),
        'w_ssm_proj': dense(ks[13], SSM_D_INNER, D_MODEL),
        'attn_sinks': 0.5 * jax.random.normal(ks[14], (DEPTH, ATTN_Q_HEADS), f32),
        'rel_bias_table': 0.5 * jax.random.normal(ks[15], (REL_BUCKETS, ATTN_Q_HEADS), f32),
        'w_attn_proj': dense(ks[16], ATTN_Q_DIM, D_MODEL),
        'w_out': dense(ks[17], D_MODEL, D_MODEL),
        'mix_post_g': gain(ks[18], D_MODEL),
        'ffn2_pre_g': gain(ks[19], D_MODEL),
        'ffn2_w_gate': dense(ks[21], D_MODEL, D_FF),
        'ffn2_w_up': dense(ks[22], D_MODEL, D_FF),
        'ffn2_w_down': dense(ks[23], D_FF, D_MODEL),
        'ffn2_post_g': gain(ks[24], D_MODEL),
    }


def _fwd_reference(x, ffn1_pre_g, ffn1_w_gate, ffn1_w_up, ffn1_w_down, ffn1_post_g, mix_pre_g, w_in,
              conv_w, conv_b, dt_bias, a_log, d_skip, ssm_norm_g, w_ssm_proj, attn_sinks,
              rel_bias_table, w_attn_proj, w_out, mix_post_g, ffn2_pre_g, ffn2_w_gate, ffn2_w_up,
              ffn2_w_down, ffn2_post_g):
    h = x
    for l in range(DEPTH):
        f1 = swiglu(rmsnorm(h, ffn1_pre_g[l]), ffn1_w_gate[l], ffn1_w_up[l], ffn1_w_down[l])
        h = h + FFN_RESIDUAL_WEIGHT * rmsnorm(f1, ffn1_post_g[l])
        mix = hybrid_mixer(rmsnorm(h, mix_pre_g[l]), w_in[l], conv_w[l], conv_b[l], dt_bias[l],
                           a_log[l], d_skip[l], ssm_norm_g[l], w_ssm_proj[l], attn_sinks[l],
                           rel_bias_table, w_attn_proj[l], w_out[l])
        h = h + rmsnorm(mix, mix_post_g[l])
        f2 = swiglu(rmsnorm(h, ffn2_pre_g[l]), ffn2_w_gate[l], ffn2_w_up[l], ffn2_w_down[l])
        h = h + FFN_RESIDUAL_WEIGHT * rmsnorm(f2, ffn2_post_g[l])
    return h


import jax as _jax
import jax.numpy as _jnp

TWIN_FORMAT = 'train_step'
FWD_PARAMS = ['x', 'ffn1_pre_g', 'ffn1_w_gate', 'ffn1_w_up', 'ffn1_w_down', 'ffn1_post_g', 'mix_pre_g', 'w_in', 'conv_w', 'conv_b', 'dt_bias', 'a_log', 'd_skip', 'ssm_norm_g', 'w_ssm_proj', 'attn_sinks', 'rel_bias_table', 'w_attn_proj', 'w_out', 'mix_post_g', 'ffn2_pre_g', 'ffn2_w_gate', 'ffn2_w_up', 'ffn2_w_down', 'ffn2_post_g']
TWIN_WEIGHTS = ['ffn1_pre_g', 'ffn1_w_gate', 'ffn1_w_up', 'ffn1_w_down', 'ffn1_post_g', 'mix_pre_g', 'w_in', 'conv_w', 'conv_b', 'dt_bias', 'a_log', 'd_skip', 'ssm_norm_g', 'w_ssm_proj', 'attn_sinks', 'rel_bias_table', 'w_attn_proj', 'w_out', 'mix_post_g', 'ffn2_pre_g', 'ffn2_w_gate', 'ffn2_w_up', 'ffn2_w_down', 'ffn2_post_g']
TWIN_DIFF_INPUT = 'x'
TWIN_INPUTS = ['x', 'ffn1_pre_g', 'ffn1_w_gate', 'ffn1_w_up', 'ffn1_w_down', 'ffn1_post_g', 'mix_pre_g', 'w_in', 'conv_w', 'conv_b', 'dt_bias', 'a_log', 'd_skip', 'ssm_norm_g', 'w_ssm_proj', 'attn_sinks', 'rel_bias_table', 'w_attn_proj', 'w_out', 'mix_post_g', 'ffn2_pre_g', 'ffn2_w_gate', 'ffn2_w_up', 'ffn2_w_down', 'ffn2_post_g', 'loss_target', 'm_ffn1_pre_g', 'm_ffn1_w_gate', 'm_ffn1_w_up', 'm_ffn1_w_down', 'm_ffn1_post_g', 'm_mix_pre_g', 'm_w_in', 'm_conv_w', 'm_conv_b', 'm_dt_bias', 'm_a_log', 'm_d_skip', 'm_ssm_norm_g', 'm_w_ssm_proj', 'm_attn_sinks', 'm_rel_bias_table', 'm_w_attn_proj', 'm_w_out', 'm_mix_post_g', 'm_ffn2_pre_g', 'm_ffn2_w_gate', 'm_ffn2_w_up', 'm_ffn2_w_down', 'm_ffn2_post_g', 'v_ffn1_pre_g', 'v_ffn1_w_gate', 'v_ffn1_w_up', 'v_ffn1_w_down', 'v_ffn1_post_g', 'v_mix_pre_g', 'v_w_in', 'v_conv_w', 'v_conv_b', 'v_dt_bias', 'v_a_log', 'v_d_skip', 'v_ssm_norm_g', 'v_w_ssm_proj', 'v_attn_sinks', 'v_rel_bias_table', 'v_w_attn_proj', 'v_w_out', 'v_mix_post_g', 'v_ffn2_pre_g', 'v_ffn2_w_gate', 'v_ffn2_w_up', 'v_ffn2_w_down', 'v_ffn2_post_g']
TWIN_OUTPUTS = ['loss', 'grad_x', 'grad_ffn1_pre_g', 'grad_ffn1_w_gate', 'grad_ffn1_w_up', 'grad_ffn1_w_down', 'grad_ffn1_post_g', 'grad_mix_pre_g', 'grad_w_in', 'grad_conv_w', 'grad_conv_b', 'grad_dt_bias', 'grad_a_log', 'grad_d_skip', 'grad_ssm_norm_g', 'grad_w_ssm_proj', 'grad_attn_sinks', 'grad_rel_bias_table', 'grad_w_attn_proj', 'grad_w_out', 'grad_mix_post_g', 'grad_ffn2_pre_g', 'grad_ffn2_w_gate', 'grad_ffn2_w_up', 'grad_ffn2_w_down', 'grad_ffn2_post_g', 'delta_ffn1_pre_g', 'delta_ffn1_w_gate', 'delta_ffn1_w_up', 'delta_ffn1_w_down', 'delta_ffn1_post_g', 'delta_mix_pre_g', 'delta_w_in', 'delta_conv_w', 'delta_conv_b', 'delta_dt_bias', 'delta_a_log', 'delta_d_skip', 'delta_ssm_norm_g', 'delta_w_ssm_proj', 'delta_attn_sinks', 'delta_rel_bias_table', 'delta_w_attn_proj', 'delta_w_out', 'delta_mix_post_g', 'delta_ffn2_pre_g', 'delta_ffn2_w_gate', 'delta_ffn2_w_up', 'delta_ffn2_w_down', 'delta_ffn2_post_g', 'new_m_ffn1_pre_g', 'new_m_ffn1_w_gate', 'new_m_ffn1_w_up', 'new_m_ffn1_w_down', 'new_m_ffn1_post_g', 'new_m_mix_pre_g', 'new_m_w_in', 'new_m_conv_w', 'new_m_conv_b', 'new_m_dt_bias', 'new_m_a_log', 'new_m_d_skip', 'new_m_ssm_norm_g', 'new_m_w_ssm_proj', 'new_m_attn_sinks', 'new_m_rel_bias_table', 'new_m_w_attn_proj', 'new_m_w_out', 'new_m_mix_post_g', 'new_m_ffn2_pre_g', 'new_m_ffn2_w_gate', 'new_m_ffn2_w_up', 'new_m_ffn2_w_down', 'new_m_ffn2_post_g', 'new_v_ffn1_pre_g', 'new_v_ffn1_w_gate', 'new_v_ffn1_w_up', 'new_v_ffn1_w_down', 'new_v_ffn1_post_g', 'new_v_mix_pre_g', 'new_v_w_in', 'new_v_conv_w', 'new_v_conv_b', 'new_v_dt_bias', 'new_v_a_log', 'new_v_d_skip', 'new_v_ssm_norm_g', 'new_v_w_ssm_proj', 'new_v_attn_sinks', 'new_v_rel_bias_table', 'new_v_w_attn_proj', 'new_v_w_out', 'new_v_mix_post_g', 'new_v_ffn2_pre_g', 'new_v_ffn2_w_gate', 'new_v_ffn2_w_up', 'new_v_ffn2_w_down', 'new_v_ffn2_post_g']
TWIN_LEAF_KINDS = {'loss': 'loss', 'grad_x': 'grad_x', 'grad_ffn1_pre_g': 'grad_w', 'grad_ffn1_w_gate': 'grad_w', 'grad_ffn1_w_up': 'grad_w', 'grad_ffn1_w_down': 'grad_w', 'grad_ffn1_post_g': 'grad_w', 'grad_mix_pre_g': 'grad_w', 'grad_w_in': 'grad_w', 'grad_conv_w': 'grad_w', 'grad_conv_b': 'grad_w', 'grad_dt_bias': 'grad_w', 'grad_a_log': 'grad_w', 'grad_d_skip': 'grad_w', 'grad_ssm_norm_g': 'grad_w', 'grad_w_ssm_proj': 'grad_w', 'grad_attn_sinks': 'grad_w', 'grad_rel_bias_table': 'grad_w', 'grad_w_attn_proj': 'grad_w', 'grad_w_out': 'grad_w', 'grad_mix_post_g': 'grad_w', 'grad_ffn2_pre_g': 'grad_w', 'grad_ffn2_w_gate': 'grad_w', 'grad_ffn2_w_up': 'grad_w', 'grad_ffn2_w_down': 'grad_w', 'grad_ffn2_post_g': 'grad_w', 'delta_ffn1_pre_g': 'delta_w', 'delta_ffn1_w_gate': 'delta_w', 'delta_ffn1_w_up': 'delta_w', 'delta_ffn1_w_down': 'delta_w', 'delta_ffn1_post_g': 'delta_w', 'delta_mix_pre_g': 'delta_w', 'delta_w_in': 'delta_w', 'delta_conv_w': 'delta_w', 'delta_conv_b': 'delta_w', 'delta_dt_bias': 'delta_w', 'delta_a_log': 'delta_w', 'delta_d_skip': 'delta_w', 'delta_ssm_norm_g': 'delta_w', 'delta_w_ssm_proj': 'delta_w', 'delta_attn_sinks': 'delta_w', 'delta_rel_bias_table': 'delta_w', 'delta_w_attn_proj': 'delta_w', 'delta_w_out': 'delta_w', 'delta_mix_post_g': 'delta_w', 'delta_ffn2_pre_g': 'delta_w', 'delta_ffn2_w_gate': 'delta_w', 'delta_ffn2_w_up': 'delta_w', 'delta_ffn2_w_down': 'delta_w', 'delta_ffn2_post_g': 'delta_w', 'new_m_ffn1_pre_g': 'new_m', 'new_m_ffn1_w_gate': 'new_m', 'new_m_ffn1_w_up': 'new_m', 'new_m_ffn1_w_down': 'new_m', 'new_m_ffn1_post_g': 'new_m', 'new_m_mix_pre_g': 'new_m', 'new_m_w_in': 'new_m', 'new_m_conv_w': 'new_m', 'new_m_conv_b': 'new_m', 'new_m_dt_bias': 'new_m', 'new_m_a_log': 'new_m', 'new_m_d_skip': 'new_m', 'new_m_ssm_norm_g': 'new_m', 'new_m_w_ssm_proj': 'new_m', 'new_m_attn_sinks': 'new_m', 'new_m_rel_bias_table': 'new_m', 'new_m_w_attn_proj': 'new_m', 'new_m_w_out': 'new_m', 'new_m_mix_post_g': 'new_m', 'new_m_ffn2_pre_g': 'new_m', 'new_m_ffn2_w_gate': 'new_m', 'new_m_ffn2_w_up': 'new_m', 'new_m_ffn2_w_down': 'new_m', 'new_m_ffn2_post_g': 'new_m', 'new_v_ffn1_pre_g': 'new_v', 'new_v_ffn1_w_gate': 'new_v', 'new_v_ffn1_w_up': 'new_v', 'new_v_ffn1_w_down': 'new_v', 'new_v_ffn1_post_g': 'new_v', 'new_v_mix_pre_g': 'new_v', 'new_v_w_in': 'new_v', 'new_v_conv_w': 'new_v', 'new_v_conv_b': 'new_v', 'new_v_dt_bias': 'new_v', 'new_v_a_log': 'new_v', 'new_v_d_skip': 'new_v', 'new_v_ssm_norm_g': 'new_v', 'new_v_w_ssm_proj': 'new_v', 'new_v_attn_sinks': 'new_v', 'new_v_rel_bias_table': 'new_v', 'new_v_w_attn_proj': 'new_v', 'new_v_w_out': 'new_v', 'new_v_mix_post_g': 'new_v', 'new_v_ffn2_pre_g': 'new_v', 'new_v_ffn2_w_gate': 'new_v', 'new_v_ffn2_w_up': 'new_v', 'new_v_ffn2_w_down': 'new_v', 'new_v_ffn2_post_g': 'new_v'}


def _forward(args):
    return _fwd_reference(*[args[k] for k in FWD_PARAMS])


def _output_shape():
    out = _jax.eval_shape(lambda: _forward(_fwd_setup_inputs(0)))
    return out.shape, out.dtype

N_MICROBATCH = 1
ADAM_LR = 0.001
ADAM_B1 = 0.9
ADAM_B2 = 0.999
ADAM_EPS = 1e-08
ADAM_WD = 0.01
ADAM_STEP = 10
PER_EXAMPLE_BATCH_AXIS = {'x': 0, 'loss_target': 0}
SHARED_INPUTS = []
_WEIGHT_DTYPES = {'ffn1_pre_g': _jnp.float32, 'ffn1_w_gate': _jnp.float32, 'ffn1_w_up': _jnp.float32, 'ffn1_w_down': _jnp.float32, 'ffn1_post_g': _jnp.float32, 'mix_pre_g': _jnp.float32, 'w_in': _jnp.float32, 'conv_w': _jnp.float32, 'conv_b': _jnp.float32, 'dt_bias': _jnp.float32, 'a_log': _jnp.float32, 'd_skip': _jnp.float32, 'ssm_norm_g': _jnp.float32, 'w_ssm_proj': _jnp.float32, 'attn_sinks': _jnp.float32, 'rel_bias_table': _jnp.float32, 'w_attn_proj': _jnp.float32, 'w_out': _jnp.float32, 'mix_post_g': _jnp.float32, 'ffn2_pre_g': _jnp.float32, 'ffn2_w_gate': _jnp.float32, 'ffn2_w_up': _jnp.float32, 'ffn2_w_down': _jnp.float32, 'ffn2_post_g': _jnp.float32}
MOMENT_SCALE = {'ffn1_pre_g': 6.067712e-01, 'ffn1_w_gate': 2.564403e-01, 'ffn1_w_up': 2.675035e-01, 'ffn1_w_down': 4.488659e-01, 'ffn1_post_g': 1.583729e+01, 'mix_pre_g': 8.665837e-01, 'w_in': 2.856252e-01, 'conv_w': 3.638701e-01, 'conv_b': 8.618909e-01, 'dt_bias': 7.610886e-01, 'a_log': 1.817132e+00, 'd_skip': 2.236392e+00, 'ssm_norm_g': 4.861014e-01, 'w_ssm_proj': 7.044602e-01, 'attn_sinks': 6.408012e-02, 'rel_bias_table': 1.137494e-01, 'w_attn_proj': 9.912220e-02, 'w_out': 7.306106e-01, 'mix_post_g': 6.401351e+01, 'ffn2_pre_g': 5.313364e-01, 'ffn2_w_gate': 1.621203e-01, 'ffn2_w_up': 2.499535e-01, 'ffn2_w_down': 4.111881e-01, 'ffn2_post_g': 1.592585e+01}


def _to_microbatches(a, axis):
    t = _jnp.moveaxis(a, axis, 0)
    t = t.reshape((N_MICROBATCH, t.shape[0] // N_MICROBATCH) + t.shape[1:])
    return _jnp.moveaxis(t, 1, axis + 1)


def setup_inputs(seed: int = 0) -> dict:
    inp = _fwd_setup_inputs(seed)
    key = _jax.random.fold_in(_jax.random.key(seed), 7919)
    shape, _ = _output_shape()
    out = dict(inp)
    out["loss_target"] = _jax.random.normal(_jax.random.fold_in(key, 0), shape, _jnp.float32)
    for i, name in enumerate(TWIN_WEIGHTS):
        w = inp[name].astype(_jnp.float32)
        if MOMENT_SCALE is None:
            s = _jnp.sqrt(_jnp.mean(_jnp.square(w)) + 1e-30)
        else:
            s = MOMENT_SCALE[name]
        km, kv = _jax.random.split(_jax.random.fold_in(key, i + 1))
        out[name] = w
        out["m_" + name] = s * _jax.random.normal(km, w.shape, _jnp.float32)
        out["v_" + name] = (s * s) * _jax.random.uniform(kv, w.shape, _jnp.float32, 0.5, 1.5)
    if N_MICROBATCH > 1:
        for name, axis in PER_EXAMPLE_BATCH_AXIS.items():
            out[name] = _to_microbatches(out[name], axis)
    return {'x': out['x'], 'ffn1_pre_g': out['ffn1_pre_g'], 'ffn1_w_gate': out['ffn1_w_gate'], 'ffn1_w_up': out['ffn1_w_up'], 'ffn1_w_down': out['ffn1_w_down'], 'ffn1_post_g': out['ffn1_post_g'], 'mix_pre_g': out['mix_pre_g'], 'w_in': out['w_in'], 'conv_w': out['conv_w'], 'conv_b': out['conv_b'], 'dt_bias': out['dt_bias'], 'a_log': out['a_log'], 'd_skip': out['d_skip'], 'ssm_norm_g': out['ssm_norm_g'], 'w_ssm_proj': out['w_ssm_proj'], 'attn_sinks': out['attn_sinks'], 'rel_bias_table': out['rel_bias_table'], 'w_attn_proj': out['w_attn_proj'], 'w_out': out['w_out'], 'mix_post_g': out['mix_post_g'], 'ffn2_pre_g': out['ffn2_pre_g'], 'ffn2_w_gate': out['ffn2_w_gate'], 'ffn2_w_up': out['ffn2_w_up'], 'ffn2_w_down': out['ffn2_w_down'], 'ffn2_post_g': out['ffn2_post_g'], 'loss_target': out['loss_target'], 'm_ffn1_pre_g': out['m_ffn1_pre_g'], 'm_ffn1_w_gate': out['m_ffn1_w_gate'], 'm_ffn1_w_up': out['m_ffn1_w_up'], 'm_ffn1_w_down': out['m_ffn1_w_down'], 'm_ffn1_post_g': out['m_ffn1_post_g'], 'm_mix_pre_g': out['m_mix_pre_g'], 'm_w_in': out['m_w_in'], 'm_conv_w': out['m_conv_w'], 'm_conv_b': out['m_conv_b'], 'm_dt_bias': out['m_dt_bias'], 'm_a_log': out['m_a_log'], 'm_d_skip': out['m_d_skip'], 'm_ssm_norm_g': out['m_ssm_norm_g'], 'm_w_ssm_proj': out['m_w_ssm_proj'], 'm_attn_sinks': out['m_attn_sinks'], 'm_rel_bias_table': out['m_rel_bias_table'], 'm_w_attn_proj': out['m_w_attn_proj'], 'm_w_out': out['m_w_out'], 'm_mix_post_g': out['m_mix_post_g'], 'm_ffn2_pre_g': out['m_ffn2_pre_g'], 'm_ffn2_w_gate': out['m_ffn2_w_gate'], 'm_ffn2_w_up': out['m_ffn2_w_up'], 'm_ffn2_w_down': out['m_ffn2_w_down'], 'm_ffn2_post_g': out['m_ffn2_post_g'], 'v_ffn1_pre_g': out['v_ffn1_pre_g'], 'v_ffn1_w_gate': out['v_ffn1_w_gate'], 'v_ffn1_w_up': out['v_ffn1_w_up'], 'v_ffn1_w_down': out['v_ffn1_w_down'], 'v_ffn1_post_g': out['v_ffn1_post_g'], 'v_mix_pre_g': out['v_mix_pre_g'], 'v_w_in': out['v_w_in'], 'v_conv_w': out['v_conv_w'], 'v_conv_b': out['v_conv_b'], 'v_dt_bias': out['v_dt_bias'], 'v_a_log': out['v_a_log'], 'v_d_skip': out['v_d_skip'], 'v_ssm_norm_g': out['v_ssm_norm_g'], 'v_w_ssm_proj': out['v_w_ssm_proj'], 'v_attn_sinks': out['v_attn_sinks'], 'v_rel_bias_table': out['v_rel_bias_table'], 'v_w_attn_proj': out['v_w_attn_proj'], 'v_w_out': out['v_w_out'], 'v_mix_post_g': out['v_mix_post_g'], 'v_ffn2_pre_g': out['v_ffn2_pre_g'], 'v_ffn2_w_gate': out['v_ffn2_w_gate'], 'v_ffn2_w_up': out['v_ffn2_w_up'], 'v_ffn2_w_down': out['v_ffn2_w_down'], 'v_ffn2_post_g': out['v_ffn2_post_g']}


def _loss(weights, diff, rest, loss_target):
    with _jax.named_scope("forward"):
        args = {**rest, TWIN_DIFF_INPUT: diff, **{k: w.astype(_WEIGHT_DTYPES[k]) for k, w in weights.items()}}
        y = _forward(args)
    with _jax.named_scope("loss_head"):
        err = _jnp.square(y.astype(_jnp.float32) - loss_target)
        return 0.5 * _jnp.sum(_jnp.mean(err, axis=-1)) if err.ndim else 0.5 * err


def _adamw(w, g, m, v):
    m = ADAM_B1 * m + (1.0 - ADAM_B1) * g
    v = ADAM_B2 * v + (1.0 - ADAM_B2) * _jnp.square(g)
    m_hat = m / (1.0 - ADAM_B1 ** ADAM_STEP)
    v_hat = v / (1.0 - ADAM_B2 ** ADAM_STEP)
    delta = -ADAM_LR * (m_hat / (_jnp.sqrt(v_hat) + ADAM_EPS) + ADAM_WD * w)
    return delta, m, v


def reference(x, ffn1_pre_g, ffn1_w_gate, ffn1_w_up, ffn1_w_down, ffn1_post_g, mix_pre_g, w_in, conv_w, conv_b, dt_bias, a_log, d_skip, ssm_norm_g, w_ssm_proj, attn_sinks, rel_bias_table, w_attn_proj, w_out, mix_post_g, ffn2_pre_g, ffn2_w_gate, ffn2_w_up, ffn2_w_down, ffn2_post_g, loss_target, m_ffn1_pre_g, m_ffn1_w_gate, m_ffn1_w_up, m_ffn1_w_down, m_ffn1_post_g, m_mix_pre_g, m_w_in, m_conv_w, m_conv_b, m_dt_bias, m_a_log, m_d_skip, m_ssm_norm_g, m_w_ssm_proj, m_attn_sinks, m_rel_bias_table, m_w_attn_proj, m_w_out, m_mix_post_g, m_ffn2_pre_g, m_ffn2_w_gate, m_ffn2_w_up, m_ffn2_w_down, m_ffn2_post_g, v_ffn1_pre_g, v_ffn1_w_gate, v_ffn1_w_up, v_ffn1_w_down, v_ffn1_post_g, v_mix_pre_g, v_w_in, v_conv_w, v_conv_b, v_dt_bias, v_a_log, v_d_skip, v_ssm_norm_g, v_w_ssm_proj, v_attn_sinks, v_rel_bias_table, v_w_attn_proj, v_w_out, v_mix_post_g, v_ffn2_pre_g, v_ffn2_w_gate, v_ffn2_w_up, v_ffn2_w_down, v_ffn2_post_g):
    given = dict(x=x, ffn1_pre_g=ffn1_pre_g, ffn1_w_gate=ffn1_w_gate, ffn1_w_up=ffn1_w_up, ffn1_w_down=ffn1_w_down, ffn1_post_g=ffn1_post_g, mix_pre_g=mix_pre_g, w_in=w_in, conv_w=conv_w, conv_b=conv_b, dt_bias=dt_bias, a_log=a_log, d_skip=d_skip, ssm_norm_g=ssm_norm_g, w_ssm_proj=w_ssm_proj, attn_sinks=attn_sinks, rel_bias_table=rel_bias_table, w_attn_proj=w_attn_proj, w_out=w_out, mix_post_g=mix_post_g, ffn2_pre_g=ffn2_pre_g, ffn2_w_gate=ffn2_w_gate, ffn2_w_up=ffn2_w_up, ffn2_w_down=ffn2_w_down, ffn2_post_g=ffn2_post_g, loss_target=loss_target, m_ffn1_pre_g=m_ffn1_pre_g, m_ffn1_w_gate=m_ffn1_w_gate, m_ffn1_w_up=m_ffn1_w_up, m_ffn1_w_down=m_ffn1_w_down, m_ffn1_post_g=m_ffn1_post_g, m_mix_pre_g=m_mix_pre_g, m_w_in=m_w_in, m_conv_w=m_conv_w, m_conv_b=m_conv_b, m_dt_bias=m_dt_bias, m_a_log=m_a_log, m_d_skip=m_d_skip, m_ssm_norm_g=m_ssm_norm_g, m_w_ssm_proj=m_w_ssm_proj, m_attn_sinks=m_attn_sinks, m_rel_bias_table=m_rel_bias_table, m_w_attn_proj=m_w_attn_proj, m_w_out=m_w_out, m_mix_post_g=m_mix_post_g, m_ffn2_pre_g=m_ffn2_pre_g, m_ffn2_w_gate=m_ffn2_w_gate, m_ffn2_w_up=m_ffn2_w_up, m_ffn2_w_down=m_ffn2_w_down, m_ffn2_post_g=m_ffn2_post_g, v_ffn1_pre_g=v_ffn1_pre_g, v_ffn1_w_gate=v_ffn1_w_gate, v_ffn1_w_up=v_ffn1_w_up, v_ffn1_w_down=v_ffn1_w_down, v_ffn1_post_g=v_ffn1_post_g, v_mix_pre_g=v_mix_pre_g, v_w_in=v_w_in, v_conv_w=v_conv_w, v_conv_b=v_conv_b, v_dt_bias=v_dt_bias, v_a_log=v_a_log, v_d_skip=v_d_skip, v_ssm_norm_g=v_ssm_norm_g, v_w_ssm_proj=v_w_ssm_proj, v_attn_sinks=v_attn_sinks, v_rel_bias_table=v_rel_bias_table, v_w_attn_proj=v_w_attn_proj, v_w_out=v_w_out, v_mix_post_g=v_mix_post_g, v_ffn2_pre_g=v_ffn2_pre_g, v_ffn2_w_gate=v_ffn2_w_gate, v_ffn2_w_up=v_ffn2_w_up, v_ffn2_w_down=v_ffn2_w_down, v_ffn2_post_g=v_ffn2_post_g)
    weights = {n: given[n] for n in TWIN_WEIGHTS}
    shared = {n: given[n] for n in SHARED_INPUTS}
    per_example = {n: given[n] for n in ['x']}
    grad_fn = _jax.value_and_grad(_loss, argnums=(0, 1))

    def one_microbatch(ex, loss_target):
        ex = dict(ex)
        diff = ex.pop(TWIN_DIFF_INPUT)
        return grad_fn(weights, diff, {**shared, **ex}, loss_target)

    if N_MICROBATCH == 1:
        loss, (grad_w, grad_x) = one_microbatch(per_example, given["loss_target"])
    else:
        def body(carry, xs):
            loss_sum, grad_sum = carry
            l_k, (gw_k, gx_k) = one_microbatch(xs[0], xs[1])
            with _jax.named_scope("update"):
                return (loss_sum + l_k, _jax.tree.map(_jnp.add, grad_sum, gw_k)), gx_k

        init = (_jnp.zeros((), _jnp.float32), _jax.tree.map(_jnp.zeros_like, weights))
        (loss, grad_w), grad_x = _jax.lax.scan(body, init, (per_example, given["loss_target"]))
    with _jax.named_scope("update"):
        delta_w, new_m, new_v = {}, {}, {}
        for n in TWIN_WEIGHTS:
            delta_w[n], new_m[n], new_v[n] = _adamw(weights[n], grad_w[n], given["m_" + n], given["v_" + n])
    return (loss, grad_x, *[grad_w[n] for n in TWIN_WEIGHTS], *[delta_w[n] for n in TWIN_WEIGHTS],
            *[new_m[n] for n in TWIN_WEIGHTS], *[new_v[n] for n in TWIN_WEIGHTS])
```

```python
import functools
import math

import jax
import jax.numpy as jnp
import numpy as np
from jax import lax
from jax.experimental import pallas as pl
from jax.experimental.pallas import tpu as pltpu

F32 = jnp.float32
BF16 = jnp.bfloat16

D_MODEL = 1024
D_FF = 2816
N_SHARD = 4
SSM_D_INNER = 2048
SSM_HEAD_DIM = 64
SSM_HEADS = 32
SSM_GROUPS = 4
SSM_HPG = SSM_HEADS // SSM_GROUPS
SSM_GW = SSM_D_INNER // SSM_GROUPS
SSM_STATE = 128
SSM_CONV = 4
SSM_CHUNK = 128
SSM_CONV_DIM = SSM_D_INNER + 2 * SSM_GROUPS * SSM_STATE
ATTN_Q_HEADS = 16
ATTN_KV_HEADS = 4
ATTN_REP = ATTN_Q_HEADS // ATTN_KV_HEADS
ATTN_HEAD_DIM = 64
ATTN_BLOCK = 128
ATTN_Q_DIM = 1024
ATTN_KV_DIM = 256
REL_BUCKETS = 32
REL_MAX_DISTANCE = 128
RMS_EPS = 1e-6
IN_COLS = 8736
ADAM_LR = 0.001
ADAM_B1 = 0.9
ADAM_B2 = 0.999
ADAM_EPS = 1e-08
ADAM_WD = 0.01
ADAM_STEP = 10
HALO = 8

VMEM_LIMIT = 56 * 1024 * 1024


def _cparams(*sem):
    return pltpu.CompilerParams(dimension_semantics=tuple(sem) if sem else None, vmem_limit_bytes=VMEM_LIMIT)


def _dot(a, b):
    return jnp.dot(a, b, preferred_element_type=F32)


def _dot_nt(a, b):
    return lax.dot_general(a, b, (((1,), (1,)), ((), ())), preferred_element_type=F32)


def _dot_tn(a, b):
    return lax.dot_general(a, b, (((0,), (0,)), ((), ())), preferred_element_type=F32)


def _dot_hi(a, b):
    return jnp.dot(a, b, preferred_element_type=F32, precision=lax.Precision.HIGHEST)


def _sigmoid(x):
    return 1.0 / (1.0 + jnp.exp(-x))


def _resident(shape):
    n = len(shape)
    return pl.BlockSpec(shape, lambda *_: (0,) * n, pipeline_mode=pl.Buffered(1))


def _rows(tm, width):
    return pl.BlockSpec((tm, width), lambda i: (i, 0))


def ffn_fwd(h, g_pre, wg, wu, wd, g_post, target=None, *, name, tm=512):
    T, D = h.shape
    NS, _, FS = wg.shape
    with_loss = target is not None
    nt = T // tm

    def body(*refs):
        if with_loss:
            (h_ref, gpre_ref, wg_ref, wu_ref, wd_ref, gpost_ref, tgt_ref,
             hout_ref, n_ref, gate_ref, up_ref, f_ref, dy_ref, loss_ref) = refs
        else:
            (h_ref, gpre_ref, wg_ref, wu_ref, wd_ref, gpost_ref,
             hout_ref, n_ref, gate_ref, up_ref, f_ref) = refs
        hh = h_ref[...]
        r = lax.rsqrt(jnp.mean(hh * hh, axis=-1, keepdims=True) + RMS_EPS)
        n = (hh * r * gpre_ref[...]).astype(BF16)
        n_ref[...] = n
        acc = jnp.zeros((tm, D), F32)
        for s in range(NS):
            gate = _dot(n, wg_ref[s])
            up = _dot(n, wu_ref[s])
            gate_ref[s] = gate.astype(BF16)
            up_ref[s] = up.astype(BF16)
            a = (gate * _sigmoid(gate) * up).astype(BF16)
            acc = acc + _dot(a, wd_ref[s])
        f_ref[...] = acc
        r2 = lax.rsqrt(jnp.mean(acc * acc, axis=-1, keepdims=True) + RMS_EPS)
        out = hh + 0.5 * (acc * r2 * gpost_ref[...])
        hout_ref[...] = out
        if with_loss:
            e = out - tgt_ref[...]
            dy_ref[...] = e * (1.0 / D)
            loss_ref[...] = jnp.full((1, 8, 128), 0.5 / D, F32) * jnp.sum(e * e)

    in_specs = [_rows(tm, D), _resident((1, D)), _resident((NS, D, FS)), _resident((NS, D, FS)),
                _resident((NS, FS, D)), _resident((1, D))]
    args = [h, g_pre, wg, wu, wd, g_post]
    out_shape = [jax.ShapeDtypeStruct((T, D), F32), jax.ShapeDtypeStruct((T, D), BF16),
                 jax.ShapeDtypeStruct((NS, T, FS), BF16), jax.ShapeDtypeStruct((NS, T, FS), BF16),
                 jax.ShapeDtypeStruct((T, D), F32)]
    seg = pl.BlockSpec((NS, tm, FS), lambda i: (0, i, 0))
    out_specs = [_rows(tm, D), _rows(tm, D), seg, seg, _rows(tm, D)]
    if with_loss:
        in_specs.append(_rows(tm, D))
        args.append(target)
        out_shape += [jax.ShapeDtypeStruct((T, D), F32), jax.ShapeDtypeStruct((nt, 8, 128), F32)]
        out_specs += [_rows(tm, D), pl.BlockSpec((1, 8, 128), lambda i: (i, 0, 0))]
    return pl.pallas_call(body, grid=(nt,), in_specs=in_specs, out_specs=out_specs, out_shape=out_shape,
                          compiler_params=_cparams("parallel"), name=name)(*args)


def ffn_bwd(dout, h, f, gate, up, g_pre, g_post, wg, wu, wd, *, name, tm=256):
    T, D = h.shape
    NS, _, FS = wg.shape
    nt = T // tm

    def body(dout_ref, h_ref, f_ref, gate_ref, up_ref, gpre_ref, gpost_ref, wg_ref, wu_ref, wd_ref,
             dh_ref, df_ref, a_ref, dgate_ref, dup_ref, dgpre_ref, dgpost_ref):
        @pl.when(pl.program_id(0) == 0)
        def _():
            dgpre_ref[...] = jnp.zeros_like(dgpre_ref)
            dgpost_ref[...] = jnp.zeros_like(dgpost_ref)

        do = dout_ref[...]
        ff = f_ref[...]
        d_fn = 0.5 * do
        r2 = lax.rsqrt(jnp.mean(ff * ff, axis=-1, keepdims=True) + RMS_EPS)
        dgpost_ref[...] += jnp.sum(d_fn * ff * r2, axis=0, keepdims=True)
        t = d_fn * gpost_ref[...]
        df = r2 * t - ff * (r2 * r2 * r2 * jnp.mean(t * ff, axis=-1, keepdims=True))
        dfb = df.astype(BF16)
        df_ref[...] = dfb
        dn = jnp.zeros((tm, D), F32)
        for s in range(NS):
            da = _dot_nt(dfb, wd_ref[s])
            g = gate_ref[s].astype(F32)
            u = up_ref[s].astype(F32)
            sg = _sigmoid(g)
            silu = g * sg
            a_ref[s] = (silu * u).astype(BF16)
            dgt = (da * u * (sg * (1.0 + g * (1.0 - sg)))).astype(BF16)
            dupv = (da * silu).astype(BF16)
            dgate_ref[s] = dgt
            dup_ref[s] = dupv
            dn = dn + _dot_nt(dgt, wg_ref[s]) + _dot_nt(dupv, wu_ref[s])
        hh = h_ref[...]
        r1 = lax.rsqrt(jnp.mean(hh * hh, axis=-1, keepdims=True) + RMS_EPS)
        dgpre_ref[...] += jnp.sum(dn * hh * r1, axis=0, keepdims=True)
        t = dn * gpre_ref[...]
        dh_ref[...] = do + r1 * t - hh * (r1 * r1 * r1 * jnp.mean(t * hh, axis=-1, keepdims=True))

    seg = pl.BlockSpec((NS, tm, FS), lambda i: (0, i, 0))
    acc = pl.BlockSpec((1, D), lambda i: (0, 0))
    return pl.pallas_call(
        body, grid=(nt,),
        in_specs=[_rows(tm, D), _rows(tm, D), _rows(tm, D), seg, seg, _resident((1, D)), _resident((1, D)),
                  _resident((NS, D, FS)), _resident((NS, D, FS)), _resident((NS, FS, D))],
        out_specs=[_rows(tm, D), _rows(tm, D), seg, seg, seg, acc, acc],
        out_shape=[jax.ShapeDtypeStruct((T, D), F32), jax.ShapeDtypeStruct((T, D), BF16),
                   jax.ShapeDtypeStruct((NS, T, FS), BF16), jax.ShapeDtypeStruct((NS, T, FS), BF16),
                   jax.ShapeDtypeStruct((NS, T, FS), BF16),
                   jax.ShapeDtypeStruct((1, D), F32), jax.ShapeDtypeStruct((1, D), F32)],
        compiler_params=_cparams("arbitrary"), name=name)(dout, h, f, gate, up, g_pre, g_post, wg, wu, wd)


def mm_tn(a, g, *, name, tt=1024, tn=None):
    Ba, T, K = a.shape
    Bg, _, N = g.shape
    B = max(Ba, Bg)
    tn = N if tn is None else tn
    tt = min(tt, T)
    nsteps = T // tt

    def body(a_ref, g_ref, o_ref):
        @pl.when(pl.program_id(2) == 0)
        def _():
            o_ref[...] = jnp.zeros_like(o_ref)

        o_ref[0] += _dot_tn(a_ref[0], g_ref[0].astype(BF16))

    return pl.pallas_call(
        body, grid=(B, N // tn, nsteps),
        in_specs=[pl.BlockSpec((1, tt, K), (lambda b, j, t: (b, t, 0)) if Ba > 1 else (lambda b, j, t: (0, t, 0))),
                  pl.BlockSpec((1, tt, tn), (lambda b, j, t: (b, t, j)) if Bg > 1 else (lambda b, j, t: (0, t, j)))],
        out_specs=pl.BlockSpec((1, K, tn), lambda b, j, t: (b, 0, j)),
        out_shape=jax.ShapeDtypeStruct((B, K, N), F32),
        compiler_params=_cparams("parallel", "parallel", "arbitrary"), name=name)(a, g)


def mix_in_fwd(h, g, w_gz, w_xbc, w_dt, w_dtT, w_qkv, *, name, tm=256):
    T, D = h.shape
    nt = T // tm
    CB = 1024

    def body(h_ref, g_ref, wgz_ref, wxbc_ref, wdt_ref, wdtT_ref, wqkv_ref,
             u_ref, gates_ref, z_ref, xbc_ref, dt_ref, dtT_ref, q_ref, k_ref, v_ref):
        hh = h_ref[...]
        r = lax.rsqrt(jnp.mean(hh * hh, axis=-1, keepdims=True) + RMS_EPS)
        u = (hh * r * g_ref[...]).astype(BF16)
        u_ref[...] = u
        for cb in range(0, 2048, CB):
            gates_ref[:, cb:cb + CB] = _dot(u, wgz_ref[:, cb:cb + CB])
            z_ref[:, cb:cb + CB] = _dot(u, wgz_ref[:, 2048 + cb:2048 + cb + CB])
        for cb in range(0, SSM_CONV_DIM, CB):
            xbc_ref[:, cb:cb + CB] = _dot(u, wxbc_ref[:, cb:cb + CB])
        dt_ref[...] = _dot(u, wdt_ref[...])
        dtT_ref[...] = _dot_nt(wdtT_ref[...], u)
        q_ref[...] = _dot(u, wqkv_ref[:, 0:ATTN_Q_DIM]).astype(BF16)
        k_ref[...] = _dot(u, wqkv_ref[:, ATTN_Q_DIM:ATTN_Q_DIM + ATTN_KV_DIM]).astype(BF16)
        v_ref[...] = _dot(u, wqkv_ref[:, ATTN_Q_DIM + ATTN_KV_DIM:]).astype(BF16)

    sds = jax.ShapeDtypeStruct
    return pl.pallas_call(
        body, grid=(nt,),
        in_specs=[_rows(tm, D), _resident((1, D)), _resident(w_gz.shape), _resident(w_xbc.shape),
                  _resident(w_dt.shape), _resident(w_dtT.shape), _resident(w_qkv.shape)],
        out_specs=[_rows(tm, D), _rows(tm, 2048), _rows(tm, 2048), _rows(tm, SSM_CONV_DIM), _rows(tm, SSM_HEADS),
                   pl.BlockSpec((SSM_HEADS, tm), lambda i: (0, i)),
                   _rows(tm, ATTN_Q_DIM), _rows(tm, ATTN_KV_DIM), _rows(tm, ATTN_KV_DIM)],
        out_shape=[sds((T, D), BF16), sds((T, 2048), F32), sds((T, 2048), F32), sds((T, SSM_CONV_DIM), F32),
                   sds((T, SSM_HEADS), F32), sds((SSM_HEADS, T), F32),
                   sds((T, ATTN_Q_DIM), BF16), sds((T, ATTN_KV_DIM), BF16), sds((T, ATTN_KV_DIM), BF16)],
        compiler_params=_cparams("parallel"), name=name)(h, g, w_gz, w_xbc, w_dt, w_dtT, w_qkv)


def _softplus(x):
    return jnp.maximum(x, 0.0) + jnp.log(1.0 + jnp.exp(-jnp.abs(x)))


def _iota(shape, axis):
    return lax.broadcasted_iota(jnp.int32, shape, axis)


def _head_expand(g, per_head):
    shape = (SSM_HEADS, SSM_HPG * per_head)
    head = lax.shift_right_logical(_iota(shape, 1), int(math.log2(per_head)))
    return (_iota(shape, 0) == SSM_HPG * g + head).astype(F32)


def _conv_pre(x_ref, halo_ref, w_ref, b_ref, xp_ref, first):
    Q = SSM_CHUNK
    halo = jnp.where(first, 0.0, halo_ref[...])
    xp_ref[0:HALO, :] = halo
    xp_ref[HALO:HALO + Q, :] = x_ref[...]
    pre = b_ref[...] + w_ref[3:4, :] * xp_ref[HALO:HALO + Q, :]
    for k in range(SSM_CONV - 1):
        pre = pre + w_ref[k:k + 1, :] * xp_ref[pl.ds(HALO - 3 + k, Q), :]
    return pre


def _ssd_specs(nc):
    Q, GW, N = SSM_CHUNK, SSM_GW, SSM_STATE
    nb_xs = SSM_D_INNER // N
    nb_c = nb_xs + SSM_GROUPS

    def rb(cmap):
        def row(b, c, g):
            return b * nc + cmap(c)
        return row

    def specs(cmap):
        row = rb(cmap)
        hrow = lambda b, c, g: jnp.maximum(row(b, c, g) * (Q // HALO) - 1, 0)
        return dict(
            xs=pl.BlockSpec((Q, GW), lambda b, c, g: (row(b, c, g), g)),
            bm=pl.BlockSpec((Q, N), lambda b, c, g: (row(b, c, g), nb_xs + g)),
            cm=pl.BlockSpec((Q, N), lambda b, c, g: (row(b, c, g), nb_c + g)),
            xs_halo=pl.BlockSpec((HALO, GW), lambda b, c, g: (hrow(b, c, g), g)),
            bm_halo=pl.BlockSpec((HALO, N), lambda b, c, g: (hrow(b, c, g), nb_xs + g)),
            cm_halo=pl.BlockSpec((HALO, N), lambda b, c, g: (hrow(b, c, g), nb_c + g)),
            grp=pl.BlockSpec((Q, GW), lambda b, c, g: (row(b, c, g), g)),
            dt=pl.BlockSpec((Q, SSM_HEADS), lambda b, c, g: (row(b, c, g), 0)),
            dtT=pl.BlockSpec((SSM_HEADS, Q), lambda b, c, g: (0, row(b, c, g))),
            w_xs=pl.BlockSpec((SSM_CONV, GW), lambda b, c, g: (0, g)),
            w_bm=pl.BlockSpec((SSM_CONV, N), lambda b, c, g: (0, nb_xs + g)),
            w_cm=pl.BlockSpec((SSM_CONV, N), lambda b, c, g: (0, nb_c + g)),
            b_xs=pl.BlockSpec((1, GW), lambda b, c, g: (0, g)),
            b_bm=pl.BlockSpec((1, N), lambda b, c, g: (0, nb_xs + g)),
            b_cm=pl.BlockSpec((1, N), lambda b, c, g: (0, nb_c + g)),
            vec_g=pl.BlockSpec((1, GW), lambda b, c, g: (0, g)),
            row32=pl.BlockSpec((1, SSM_HEADS), lambda b, c, g: (0, 0)),
            col32=pl.BlockSpec((SSM_HEADS, 1), lambda b, c, g: (0, 0)),
            state=pl.BlockSpec((1, SSM_HPG, SSM_HEAD_DIM, N), lambda b, c, g: (row(b, c, g), g, 0, 0)),
        )
    return specs


def _ssd_chunk_common(first, g, xs_ref, bm_ref, cm_ref, xs_halo, bm_halo, cm_halo, w_xs, w_bm, w_cm, b_xs, b_bm, b_cm,
                      dt_ref, dtT_ref, dtb_ref, dtbT_ref, alog_ref, alogT_ref, xp_xs, xp_bm, xp_cm):
    Q = SSM_CHUNK
    pre_xs = _conv_pre(xs_ref, xs_halo, w_xs, b_xs, xp_xs, first)
    pre_bm = _conv_pre(bm_ref, bm_halo, w_bm, b_bm, xp_bm, first)
    pre_cm = _conv_pre(cm_ref, cm_halo, w_cm, b_cm, xp_cm, first)
    xs = pre_xs * _sigmoid(pre_xs)
    bm = pre_bm * _sigmoid(pre_bm)
    cm = pre_cm * _sigmoid(pre_cm)
    dtr = dt_ref[...] + dtb_ref[...]
    dtrT = dtT_ref[...] + dtbT_ref[...]
    dt = _softplus(dtr)
    dtT = _softplus(dtrT)
    a = -jnp.exp(alog_ref[...])
    aT = -jnp.exp(alogT_ref[...])
    tri = (_iota((Q, Q), 0) >= _iota((Q, Q), 1)).astype(F32)
    triT = (_iota((Q, Q), 0) <= _iota((Q, Q), 1)).astype(F32)
    acs = _dot_hi(tri, dt * a)
    acsT = _dot_hi(dtT * aT, triT)
    return dict(pre_xs=pre_xs, pre_bm=pre_bm, pre_cm=pre_cm, xs=xs, bm=bm, cm=cm, dtr=dtr, dt=dt, a=a,
                acs=acs, acsT=acsT, tri=tri, triT=triT)


def ssd_fwd(xbc, z, dt_raw, dt_rawT, conv_w, conv_b, dt_bias, a_log, d_skip, norm_g, *, batch, name):
    T = xbc.shape[0]
    Q, GW, N, P, HPG = SSM_CHUNK, SSM_GW, SSM_STATE, SSM_HEAD_DIM, SSM_HPG
    nc = T // batch // Q
    sp = _ssd_specs(nc)(lambda c: c)

    def body(xs_ref, bm_ref, cm_ref, xs_halo, bm_halo, cm_halo, z_ref, dt_ref, dtT_ref,
             w_xs, w_bm, w_cm, b_xs, b_bm, b_cm, dtb_ref, dtbT_ref, alog_ref, alogT_ref, dskip_ref, ng_ref,
             y_ref, ys_ref, st_ref, state, acsT_s, y_s, xp_xs, xp_bm, xp_cm):
        c = pl.program_id(1)
        g = pl.program_id(2)
        first = c == 0
        cc = _ssd_chunk_common(first, g, xs_ref, bm_ref, cm_ref, xs_halo, bm_halo, cm_halo, w_xs, w_bm, w_cm,
                               b_xs, b_bm, b_cm, dt_ref, dtT_ref, dtb_ref, dtbT_ref, alog_ref, alogT_ref,
                               xp_xs, xp_bm, xp_cm)
        xs, acs = cc["xs"], cc["acs"]
        acsT_s[...] = cc["acsT"]
        e64 = _head_expand(g, P)
        e128 = _head_expand(g, Q)
        acs_x = _dot_hi(acs, e64)
        acs_b = _dot_hi(acs, e128)
        x = xs * _dot_hi(cc["dt"], e64)
        last_x = acs_x[Q - 1:Q, :]
        xw = (x * jnp.exp(last_x - acs_x)).astype(BF16)
        ex = jnp.exp(acs_x)
        xb = x.astype(BF16)
        bb = cc["bm"].astype(BF16)
        cb = cc["cm"].astype(BF16)
        s = _dot_nt(cb, bb)
        causal = _iota((Q, Q), 0) >= _iota((Q, Q), 1)
        for r in range(HPG):
            hd = HPG * g + r

            @pl.when(first)
            def _():
                state[hd] = jnp.zeros((P, N), F32)

            seg = acs_b[:, Q * r:Q * (r + 1)] - acsT_s[pl.ds(hd, 1), :]
            m = (s * jnp.exp(jnp.where(causal, seg, -1e30))).astype(BF16)
            hp = state[hd]
            st_ref[0, r] = hp
            y_h = _dot(m, xb[:, P * r:P * (r + 1)]) + _dot_nt(cb, hp.astype(BF16)) * ex[:, P * r:P * (r + 1)]
            y_s[:, P * r:P * (r + 1)] = y_h
            decay = jnp.exp(acsT_s[pl.ds(hd, 1), pl.ds(Q - 1, 1)])
            state[hd] = hp * decay + _dot_tn(xw[:, P * r:P * (r + 1)], bb)
        dexp = _dot_hi(jnp.broadcast_to(dskip_ref[...], (8, SSM_HEADS)), e64)[0:1, :]
        y = y_s[...] + dexp * xs
        y_ref[...] = y
        zz = z_ref[...]
        yg = y * (zz * _sigmoid(zz))
        rr = lax.rsqrt(jnp.mean(yg * yg, axis=-1, keepdims=True) + RMS_EPS)
        ys_ref[...] = (yg * rr * ng_ref[...]).astype(BF16)

    col = lambda v: v.reshape(SSM_HEADS, 1)
    sds = jax.ShapeDtypeStruct
    return pl.pallas_call(
        body, grid=(batch, nc, SSM_GROUPS),
        in_specs=[sp["xs"], sp["bm"], sp["cm"], sp["xs_halo"], sp["bm_halo"], sp["cm_halo"], sp["grp"], sp["dt"],
                  sp["dtT"], sp["w_xs"], sp["w_bm"], sp["w_cm"], sp["b_xs"], sp["b_bm"], sp["b_cm"],
                  sp["row32"], sp["col32"], sp["row32"], sp["col32"], sp["row32"], sp["vec_g"]],
        out_specs=[sp["grp"], sp["grp"], sp["state"]],
        out_shape=[sds((T, SSM_D_INNER), F32), sds((T, SSM_D_INNER), BF16),
                   sds((T // Q, SSM_HEADS, P, N), F32)],
        scratch_shapes=[pltpu.VMEM((SSM_HEADS, P, N), F32), pltpu.VMEM((SSM_HEADS, Q), F32), pltpu.VMEM((Q, GW), F32),
                        pltpu.VMEM((HALO + Q, GW), F32), pltpu.VMEM((HALO + Q, N), F32), pltpu.VMEM((HALO + Q, N), F32)],
        compiler_params=_cparams("arbitrary", "arbitrary", "arbitrary"), name=name,
    )(xbc, xbc, xbc, xbc, xbc, xbc, z, dt_raw, dt_rawT, conv_w, conv_w, conv_w, conv_b, conv_b, conv_b,
      dt_bias, col(dt_bias), a_log, col(a_log), d_skip, norm_g)


def _attn_specs(nb):
    BLK = ATTN_BLOCK

    def specs(last):
        def cur(b, n):
            return b * nb + (n if last is None else jnp.minimum(n, nb - 1))

        def prev(b, n):
            return b * nb + jnp.maximum((n if last is None else jnp.minimum(n, nb - 1)) - 1, 0)
        return cur, prev
    return specs


def attn_bias(table_t, onehot, *, name):
    def body(t_ref, f_ref, o_ref):
        o_ref[...] = _dot_hi(t_ref[...], f_ref[...])
    return pl.pallas_call(body, out_shape=jax.ShapeDtypeStruct((ATTN_Q_HEADS, onehot.shape[1]), F32),
                          compiler_params=_cparams(), name=name)(table_t, onehot)


def attn_bias_bwd(dbias, onehot, *, name):
    def body(d_ref, f_ref, o_ref):
        o_ref[...] = lax.dot_general(d_ref[...], f_ref[...], (((1,), (1,)), ((), ())), preferred_element_type=F32,
                                     precision=lax.Precision.HIGHEST)
    return pl.pallas_call(body, out_shape=jax.ShapeDtypeStruct((ATTN_Q_HEADS, REL_BUCKETS), F32),
                          compiler_params=_cparams(), name=name)(dbias, onehot)


def attn_fwd(q, k, v, bias, sinks, *, batch, name):
    T = q.shape[0]
    BLK, HD = ATTN_BLOCK, ATTN_HEAD_DIM
    nb = T // batch // BLK
    cur, prev = _attn_specs(nb)(None)
    scale = HD ** -0.5

    def body(q_ref, kc_ref, kp_ref, vc_ref, vp_ref, bias_ref, sink_ref, o_ref, lse_ref):
        n = pl.program_id(1)
        ii = _iota((BLK, BLK), 0)
        jj = _iota((BLK, BLK), 1)
        m_prev = jnp.logical_and(jj > ii, n > 0)
        m_cur = jj <= ii
        for kk in range(ATTN_KV_HEADS):
            ks = slice(HD * kk, HD * (kk + 1))
            kc, kp, vc, vp = kc_ref[:, ks], kp_ref[:, ks], vc_ref[:, ks], vp_ref[:, ks]
            for r in range(ATTN_REP):
                hd = ATTN_REP * kk + r
                qh = q_ref[:, HD * hd:HD * (hd + 1)]
                lp = jnp.where(m_prev, _dot_nt(qh, kp) * scale + bias_ref[hd, :, 0:BLK], -1e30)
                lc = jnp.where(m_cur, _dot_nt(qh, kc) * scale + bias_ref[hd, :, BLK:2 * BLK], -1e30)
                sink = sink_ref[0:1, hd:hd + 1]
                mx = jnp.maximum(jnp.maximum(jnp.max(lp, axis=-1, keepdims=True), jnp.max(lc, axis=-1, keepdims=True)), sink)
                pp = jnp.exp(lp - mx)
                pc = jnp.exp(lc - mx)
                den = jnp.sum(pp, axis=-1, keepdims=True) + jnp.sum(pc, axis=-1, keepdims=True) + jnp.exp(sink - mx)
                inv = 1.0 / den
                o = _dot((pp * inv).astype(BF16), vp) + _dot((pc * inv).astype(BF16), vc)
                o_ref[:, HD * hd:HD * (hd + 1)] = o.astype(BF16)
                lse_ref[:, hd:hd + 1] = mx + jnp.log(den)

    sds = jax.ShapeDtypeStruct
    return pl.pallas_call(
        body, grid=(batch, nb),
        in_specs=[pl.BlockSpec((BLK, ATTN_Q_DIM), lambda b, n: (cur(b, n), 0)),
                  pl.BlockSpec((BLK, ATTN_KV_DIM), lambda b, n: (cur(b, n), 0)),
                  pl.BlockSpec((BLK, ATTN_KV_DIM), lambda b, n: (prev(b, n), 0)),
                  pl.BlockSpec((BLK, ATTN_KV_DIM), lambda b, n: (cur(b, n), 0)),
                  pl.BlockSpec((BLK, ATTN_KV_DIM), lambda b, n: (prev(b, n), 0)),
                  pl.BlockSpec((ATTN_Q_HEADS, BLK, 2 * BLK), lambda b, n: (0, 0, 0)),
                  pl.BlockSpec((1, ATTN_Q_HEADS), lambda b, n: (0, 0))],
        out_specs=[pl.BlockSpec((BLK, ATTN_Q_DIM), lambda b, n: (cur(b, n), 0)),
                   pl.BlockSpec((BLK, ATTN_Q_HEADS), lambda b, n: (cur(b, n), 0))],
        out_shape=[sds((T, ATTN_Q_DIM), BF16), sds((T, ATTN_Q_HEADS), F32)],
        compiler_params=_cparams("parallel", "parallel"), name=name)(q, k, k, v, v, bias, sinks)


def mix_out_fwd(ys, o, gates, h, w_ssm, w_attn, w_out, g_post, *, name, tm=512):
    T, D = h.shape
    nt = T // tm

    def body(ys_ref, o_ref, gates_ref, h_ref, wssm_ref, wattn_ref, wout_ref, g_ref,
             hout_ref, yssm_ref, yattn_ref, mix_ref, merged_ref):
        y_ssm = _dot(ys_ref[...], wssm_ref[...])
        y_attn = _dot(o_ref[...], wattn_ref[...])
        yssm_ref[...] = y_ssm
        yattn_ref[...] = y_attn
        merged = (_sigmoid(gates_ref[:, 0:D]) * y_ssm + _sigmoid(gates_ref[:, D:2 * D]) * y_attn).astype(BF16)
        merged_ref[...] = merged
        mix = _dot(merged, wout_ref[...])
        mix_ref[...] = mix
        r = lax.rsqrt(jnp.mean(mix * mix, axis=-1, keepdims=True) + RMS_EPS)
        hout_ref[...] = h_ref[...] + mix * r * g_ref[...]

    sds = jax.ShapeDtypeStruct
    return pl.pallas_call(
        body, grid=(nt,),
        in_specs=[_rows(tm, SSM_D_INNER), _rows(tm, ATTN_Q_DIM), _rows(tm, 2 * D), _rows(tm, D),
                  _resident(w_ssm.shape), _resident(w_attn.shape), _resident(w_out.shape), _resident((1, D))],
        out_specs=[_rows(tm, D)] * 5,
        out_shape=[sds((T, D), F32), sds((T, D), F32), sds((T, D), F32), sds((T, D), F32), sds((T, D), BF16)],
        compiler_params=_cparams("parallel"), name=name)(ys, o, gates, h, w_ssm, w_attn, w_out, g_post)


def mix_out_bwd(dh, mix, y_ssm, y_attn, gates, w_out, w_ssm, w_attn, g_post, *, name, tm=256):
    T, D = dh.shape
    nt = T // tm

    def body(dh_ref, mix_ref, yssm_ref, yattn_ref, gates_ref, wout_ref, wssm_ref, wattn_ref, g_ref,
             dmix_ref, dyssm_ref, dyattn_ref, dgates_ref, dys_ref, do_ref, dg_ref):
        @pl.when(pl.program_id(0) == 0)
        def _():
            dg_ref[...] = jnp.zeros_like(dg_ref)

        do = dh_ref[...]
        mix = mix_ref[...]
        r = lax.rsqrt(jnp.mean(mix * mix, axis=-1, keepdims=True) + RMS_EPS)
        dg_ref[...] += jnp.sum(do * mix * r, axis=0, keepdims=True)
        t = do * g_ref[...]
        dmix = (r * t - mix * (r * r * r * jnp.mean(t * mix, axis=-1, keepdims=True))).astype(BF16)
        dmix_ref[...] = dmix
        dmerged = _dot_nt(dmix, wout_ref[...])
        s1 = _sigmoid(gates_ref[:, 0:D])
        s2 = _sigmoid(gates_ref[:, D:2 * D])
        dyssm = (dmerged * s1).astype(BF16)
        dyattn = (dmerged * s2).astype(BF16)
        dyssm_ref[...] = dyssm
        dyattn_ref[...] = dyattn
        dgates_ref[:, 0:D] = (dmerged * yssm_ref[...] * (s1 * (1.0 - s1))).astype(BF16)
        dgates_ref[:, D:2 * D] = (dmerged * yattn_ref[...] * (s2 * (1.0 - s2))).astype(BF16)
        dys_ref[...] = _dot_nt(dyssm, wssm_ref[...])
        do_ref[...] = _dot_nt(dyattn, wattn_ref[...]).astype(BF16)

    sds = jax.ShapeDtypeStruct
    return pl.pallas_call(
        body, grid=(nt,),
        in_specs=[_rows(tm, D), _rows(tm, D), _rows(tm, D), _rows(tm, D), _rows(tm, 2 * D),
                  _resident(w_out.shape), _resident(w_ssm.shape), _resident(w_attn.shape), _resident((1, D))],
        out_specs=[_rows(tm, D), _rows(tm, D), _rows(tm, D), _rows(tm, 2 * D), _rows(tm, SSM_D_INNER),
                   _rows(tm, ATTN_Q_DIM), pl.BlockSpec((1, D), lambda i: (0, 0))],
        out_shape=[sds((T, D), BF16), sds((T, D), BF16), sds((T, D), BF16), sds((T, 2 * D), BF16),
                   sds((T, SSM_D_INNER), F32), sds((T, ATTN_Q_DIM), BF16), sds((1, D), F32)],
        compiler_params=_cparams("arbitrary"), name=name)(dh, mix, y_ssm, y_attn, gates, w_out, w_ssm, w_attn, g_post)


def attn_bwd(q, k, v, o, do, lse, bias, sinks, *, batch, name):
    T = q.shape[0]
    BLK, HD = ATTN_BLOCK, ATTN_HEAD_DIM
    nb = T // batch // BLK
    cur, prev = _attn_specs(nb)(nb)
    scale = HD ** -0.5

    def body(q_ref, kc_ref, kp_ref, vc_ref, vp_ref, o_ref, do_ref, lse_ref, bias_ref, sink_ref,
             dq_ref, dk_ref, dv_ref, dbias_ref, dsink_ref, ck, cv):
        b = pl.program_id(0)
        n = pl.program_id(1)

        @pl.when(jnp.logical_and(b == 0, n == 0))
        def _():
            dbias_ref[...] = jnp.zeros_like(dbias_ref)
            dsink_ref[...] = jnp.zeros_like(dsink_ref)

        @pl.when(n == 0)
        def _():
            ck[...] = jnp.zeros_like(ck)
            cv[...] = jnp.zeros_like(cv)

        @pl.when(n == nb)
        def _():
            dk_ref[...] = ck[...].astype(BF16)
            dv_ref[...] = cv[...].astype(BF16)

        @pl.when(n < nb)
        def _():
            ii = _iota((BLK, BLK), 0)
            jj = _iota((BLK, BLK), 1)
            m_prev = jnp.logical_and(jj > ii, n > 0)
            m_cur = jj <= ii
            lane16 = _iota((1, ATTN_Q_HEADS), 1)
            dsink = jnp.zeros((1, ATTN_Q_HEADS), F32)
            for kk in range(ATTN_KV_HEADS):
                ks = slice(HD * kk, HD * (kk + 1))
                kc, kp, vc, vp = kc_ref[:, ks], kp_ref[:, ks], vc_ref[:, ks], vp_ref[:, ks]
                dkp = jnp.zeros((BLK, HD), F32)
                dkc = jnp.zeros((BLK, HD), F32)
                dvp = jnp.zeros((BLK, HD), F32)
                dvc = jnp.zeros((BLK, HD), F32)
                for r in range(ATTN_REP):
                    hd = ATTN_REP * kk + r
                    hs = slice(HD * hd, HD * (hd + 1))
                    qh = q_ref[:, hs]
                    doh = do_ref[:, hs]
                    lse_h = lse_ref[:, hd:hd + 1]
                    lp = jnp.where(m_prev, _dot_nt(qh, kp) * scale + bias_ref[hd, :, 0:BLK], -1e30)
                    lc = jnp.where(m_cur, _dot_nt(qh, kc) * scale + bias_ref[hd, :, BLK:2 * BLK], -1e30)
                    pp = jnp.exp(lp - lse_h)
                    pc = jnp.exp(lc - lse_h)
                    delta = jnp.sum(doh.astype(F32) * o_ref[:, hs].astype(F32), axis=-1, keepdims=True)
                    dlp = pp * (_dot_nt(doh, vp) - delta)
                    dlc = pc * (_dot_nt(doh, vc) - delta)
                    ps = jnp.exp(sink_ref[0:1, hd:hd + 1] - lse_h)
                    dsink = dsink + jnp.where(lane16 == hd, -jnp.sum(ps * delta, axis=0, keepdims=True), 0.0)
                    dbias_ref[hd, :, 0:BLK] += dlp
                    dbias_ref[hd, :, BLK:2 * BLK] += dlc
                    dlpb = dlp.astype(BF16)
                    dlcb = dlc.astype(BF16)
                    dq_ref[:, hs] = ((_dot(dlpb, kp) + _dot(dlcb, kc)) * scale).astype(BF16)
                    dkp = dkp + _dot_tn(dlpb, qh)
                    dkc = dkc + _dot_tn(dlcb, qh)
                    dvp = dvp + _dot_tn(pp.astype(BF16), doh)
                    dvc = dvc + _dot_tn(pc.astype(BF16), doh)
                dk_ref[:, ks] = (ck[:, ks] + dkp * scale).astype(BF16)
                dv_ref[:, ks] = (cv[:, ks] + dvp).astype(BF16)
                ck[:, ks] = dkc * scale
                cv[:, ks] = dvc
            dsink_ref[...] += dsink

    sds = jax.ShapeDtypeStruct
    qspec = pl.BlockSpec((BLK, ATTN_Q_DIM), lambda b, n: (cur(b, n), 0))
    cspec = pl.BlockSpec((BLK, ATTN_KV_DIM), lambda b, n: (cur(b, n), 0))
    pspec = pl.BlockSpec((BLK, ATTN_KV_DIM), lambda b, n: (prev(b, n), 0))
    late = pl.BlockSpec((BLK, ATTN_KV_DIM), lambda b, n: (b * nb + jnp.maximum(n - 1, 0), 0))
    return pl.pallas_call(
        body, grid=(batch, nb + 1),
        in_specs=[qspec, cspec, pspec, cspec, pspec, qspec, qspec,
                  pl.BlockSpec((BLK, ATTN_Q_HEADS), lambda b, n: (cur(b, n), 0)),
                  pl.BlockSpec((ATTN_Q_HEADS, BLK, 2 * BLK), lambda b, n: (0, 0, 0)),
                  pl.BlockSpec((1, ATTN_Q_HEADS), lambda b, n: (0, 0))],
        out_specs=[qspec, late, late,
                   pl.BlockSpec((ATTN_Q_HEADS, BLK, 2 * BLK), lambda b, n: (0, 0, 0)),
                   pl.BlockSpec((1, ATTN_Q_HEADS), lambda b, n: (0, 0))],
        out_shape=[sds((T, ATTN_Q_DIM), BF16), sds((T, ATTN_KV_DIM), BF16), sds((T, ATTN_KV_DIM), BF16),
                   sds((ATTN_Q_HEADS, BLK, 2 * BLK), F32), sds((1, ATTN_Q_HEADS), F32)],
        scratch_shapes=[pltpu.VMEM((BLK, ATTN_KV_DIM), F32), pltpu.VMEM((BLK, ATTN_KV_DIM), F32)],
        compiler_params=_cparams("arbitrary", "arbitrary"), name=name)(q, k, k, v, v, o, do, lse, bias, sinks)


def _conv_bwd(dxc, pre, xp_ref, w_ref, carry_ref, acc_ref, dp_ref, g, last):
    Q = SSM_CHUNK
    sg = _sigmoid(pre)
    dpre = dxc * (sg * (1.0 + pre * (1.0 - sg)))

    @pl.when(last)
    def _():
        carry_ref[g] = jnp.zeros(carry_ref.shape[1:], F32)

    dp_ref[0:Q, :] = dpre
    dp_ref[Q:Q + HALO, :] = carry_ref[g]
    carry_ref[g] = dpre[0:HALO, :]
    rows = [jnp.sum(dpre * xp_ref[pl.ds(HALO - 3 + k, Q), :], axis=0, keepdims=True) for k in range(SSM_CONV)]
    rows.append(jnp.sum(dpre, axis=0, keepdims=True))
    rows.append(jnp.zeros((HALO - SSM_CONV - 1, dpre.shape[1]), F32))
    acc_ref[g] += jnp.concatenate(rows, axis=0)
    dx = w_ref[3:4, :] * dpre
    for k in range(SSM_CONV - 1):
        dx = dx + w_ref[k:k + 1, :] * dp_ref[pl.ds(3 - k, Q), :]
    return dx


def ssd_bwd(dys, y, xbc, z, dt_raw, dt_rawT, states, conv_w, conv_b, dt_bias, a_log, d_skip, norm_g, *, batch, name):
    T = xbc.shape[0]
    Q, GW, N, P, HPG, G, H = SSM_CHUNK, SSM_GW, SSM_STATE, SSM_HEAD_DIM, SSM_HPG, SSM_GROUPS, SSM_HEADS
    nc = T // batch // Q
    sp = _ssd_specs(nc)(lambda c: nc - 1 - c)

    def body(xs_ref, bm_ref, cm_ref, xs_halo, bm_halo, cm_halo, z_ref, y_ref, dys_ref, dt_ref, dtT_ref, st_ref,
             w_xs, w_bm, w_cm, b_xs, b_bm, b_cm, dtb_ref, dtbT_ref, alog_ref, alogT_ref, dskip_ref, ng_ref,
             dz_ref, dxs_ref, dbm_ref, dcm_ref, ddt_ref, acc_xs, acc_bm, acc_cm, acc_head,
             dstate, acsT_s, dacsT_s, yoff_s, dxw_s, dx_s, xp_xs, xp_bm, xp_cm, dp_xs, dp_bm, dp_cm,
             cy_xs, cy_bm, cy_cm):
        b = pl.program_id(0)
        cr = pl.program_id(1)
        g = pl.program_id(2)
        c = nc - 1 - cr
        first = c == 0
        last = cr == 0

        @pl.when(jnp.logical_and(jnp.logical_and(b == 0, cr == 0), g == 0))
        def _():
            acc_xs[...] = jnp.zeros_like(acc_xs)
            acc_bm[...] = jnp.zeros_like(acc_bm)
            acc_cm[...] = jnp.zeros_like(acc_cm)
            acc_head[...] = jnp.zeros_like(acc_head)

        cc = _ssd_chunk_common(first, g, xs_ref, bm_ref, cm_ref, xs_halo, bm_halo, cm_halo, w_xs, w_bm, w_cm,
                               b_xs, b_bm, b_cm, dt_ref, dtT_ref, dtb_ref, dtbT_ref, alog_ref, alogT_ref,
                               xp_xs, xp_bm, xp_cm)
        xs, acs, dt, a = cc["xs"], cc["acs"], cc["dt"], cc["a"]
        acsT_s[...] = cc["acsT"]
        dacsT_s[...] = jnp.zeros_like(dacsT_s)
        e64 = _head_expand(g, P)
        e128 = _head_expand(g, Q)
        acs_x = _dot_hi(acs, e64)
        acs_b = _dot_hi(acs, e128)
        dt_x = _dot_hi(dt, e64)
        x = xs * dt_x
        w_x = jnp.exp(acs_x[Q - 1:Q, :] - acs_x)
        ex = jnp.exp(acs_x)

        yv = y_ref[...]
        zz = z_ref[...]
        sz = _sigmoid(zz)
        silu_z = zz * sz
        yg = yv * silu_z
        rr = lax.rsqrt(jnp.mean(yg * yg, axis=-1, keepdims=True) + RMS_EPS)
        dys_v = dys_ref[...]
        d_ng = jnp.sum(dys_v * yg * rr, axis=0, keepdims=True)
        t = dys_v * ng_ref[...]
        dyg = rr * t - yg * (rr * rr * rr * jnp.mean(t * yg, axis=-1, keepdims=True))
        dy = dyg * silu_z
        dz_ref[...] = (dyg * yv * (sz * (1.0 + zz * (1.0 - sz)))).astype(BF16)

        dexp = _dot_hi(jnp.broadcast_to(dskip_ref[...], (8, H)), e64)[0:1, :]
        d_dskip = _dot_nt(jnp.broadcast_to(jnp.sum(dy * xs, axis=0, keepdims=True), (8, GW)), e64)[0:1, :]

        dyb = dy.astype(BF16)
        xb = x.astype(BF16)
        xwb = (x * w_x).astype(BF16)
        bb = cc["bm"].astype(BF16)
        cb = cc["cm"].astype(BF16)
        s = _dot_nt(cb, bb)
        causal = _iota((Q, Q), 0) >= _iota((Q, Q), 1)
        lane_h = _iota((1, H), 1)
        ds_acc = jnp.zeros((Q, Q), F32)
        d_c = jnp.zeros((Q, N), F32)
        d_b = jnp.zeros((Q, N), F32)
        dacs = jnp.zeros((Q, H), F32)
        last_terms = jnp.zeros((1, H), F32)
        for r in range(HPG):
            hd = HPG * g + r
            cols = slice(P * r, P * (r + 1))

            @pl.when(last)
            def _():
                dstate[hd] = jnp.zeros((P, N), F32)

            seg = acs_b[:, Q * r:Q * (r + 1)] - acsT_s[pl.ds(hd, 1), :]
            l = jnp.exp(jnp.where(causal, seg, -1e30))
            m = s * l
            mb = m.astype(BF16)
            dyh = dyb[:, cols]
            hp = st_ref[0, r]
            hpb = hp.astype(BF16)
            dh = dstate[hd]
            dhb = dh.astype(BF16)
            yoff_s[:, cols] = _dot_nt(cb, hpb) * ex[:, cols]
            dye = (dy[:, cols] * ex[:, cols]).astype(BF16)
            d_c = d_c + _dot(dye, hpb)
            dhp_off = _dot_tn(dye, cb)
            dm = _dot_nt(dyh, xb[:, cols])
            dx_s[:, cols] = _dot_tn(mb, dyh)
            gmat = dm * m
            onehot = (lane_h == hd).astype(F32)
            dacs = dacs + jnp.sum(gmat, axis=-1, keepdims=True) * onehot
            dacsT_s[pl.ds(hd, 1), :] = -jnp.sum(gmat, axis=0, keepdims=True)
            ds_acc = ds_acc + dm * l
            dxw_s[:, cols] = _dot_nt(bb, dhb)
            d_b = d_b + _dot(xwb[:, cols], dhb)
            decay = jnp.exp(acsT_s[pl.ds(hd, 1), pl.ds(Q - 1, 1)])
            ddecay = jnp.sum(jnp.sum(dh * hp, axis=-1, keepdims=True), axis=0, keepdims=True)
            last_terms = last_terms + (ddecay * decay) * onehot
            dstate[hd] = dh * decay + dhp_off
        dsb = ds_acc.astype(BF16)
        d_c = d_c + _dot(dsb, bb)
        d_b = d_b + _dot_tn(dsb, cb)
        dxw = dxw_s[...]
        dx_full = dx_s[...] + dxw * w_x
        tw = _dot_nt(dxw * x * w_x, e64)
        dacs = dacs + _dot_nt(dy * yoff_s[...], e64) - tw
        last_terms = last_terms + jnp.sum(tw, axis=0, keepdims=True)
        eye = (_iota((Q, Q), 0) == _iota((Q, Q), 1)).astype(F32)
        dacs = dacs + lax.dot_general(eye, dacsT_s[...], (((1,), (1,)), ((), ())), preferred_element_type=F32,
                                      precision=lax.Precision.HIGHEST)
        dacs = dacs + jnp.where(_iota((Q, 1), 0) == Q - 1, 1.0, 0.0) * last_terms
        d_dta = _dot_hi(cc["triT"], dacs)
        ddt = d_dta * a + _dot_nt(dx_full * xs, e64)
        d_alog = jnp.sum(d_dta * dt, axis=0, keepdims=True) * a
        ddt_raw = ddt * _sigmoid(cc["dtr"])
        d_dtb = jnp.sum(ddt_raw, axis=0, keepdims=True)

        @pl.when(g == 0)
        def _():
            ddt_ref[...] = ddt_raw

        @pl.when(g > 0)
        def _():
            ddt_ref[...] += ddt_raw

        acc_head[...] += jnp.concatenate([d_dtb, d_alog, d_dskip, jnp.zeros((5, H), F32)], axis=0)
        dxs = dexp * dy + dx_full * dt_x
        dxs_ref[...] = _conv_bwd(dxs, cc["pre_xs"], xp_xs, w_xs, cy_xs, acc_xs, dp_xs, g, last).astype(BF16)
        dbm_ref[...] = _conv_bwd(d_b, cc["pre_bm"], xp_bm, w_bm, cy_bm, acc_bm, dp_bm, g, last).astype(BF16)
        dcm_ref[...] = _conv_bwd(d_c, cc["pre_cm"], xp_cm, w_cm, cy_cm, acc_cm, dp_cm, g, last).astype(BF16)
        acc_xs[g, pl.ds(SSM_CONV + 1, 1), :] += d_ng

    col = lambda v: v.reshape(H, 1)
    sds = jax.ShapeDtypeStruct
    row = lambda b, c, g: b * nc + (nc - 1 - c)
    full = lambda shape: pl.BlockSpec(shape, lambda b, c, g: (0,) * len(shape))
    return pl.pallas_call(
        body, grid=(batch, nc, G),
        in_specs=[sp["xs"], sp["bm"], sp["cm"], sp["xs_halo"], sp["bm_halo"], sp["cm_halo"], sp["grp"], sp["grp"],
                  sp["grp"], sp["dt"], sp["dtT"], sp["state"],
                  sp["w_xs"], sp["w_bm"], sp["w_cm"], sp["b_xs"], sp["b_bm"], sp["b_cm"],
                  sp["row32"], sp["col32"], sp["row32"], sp["col32"], sp["row32"], sp["vec_g"]],
        out_specs=[sp["grp"], sp["grp"],
                   pl.BlockSpec((Q, N), lambda b, c, g: (row(b, c, g), g)),
                   pl.BlockSpec((Q, N), lambda b, c, g: (row(b, c, g), g)),
                   sp["dt"], full((G, HALO, GW)), full((G, HALO, N)), full((G, HALO, N)), full((8, H))],
        out_shape=[sds((T, SSM_D_INNER), BF16), sds((T, SSM_D_INNER), BF16), sds((T, G * N), BF16),
                   sds((T, G * N), BF16), sds((T, H), F32),
                   sds((G, HALO, GW), F32), sds((G, HALO, N), F32), sds((G, HALO, N), F32), sds((8, H), F32)],
        scratch_shapes=[pltpu.VMEM((H, P, N), F32), pltpu.VMEM((H, Q), F32), pltpu.VMEM((H, Q), F32),
                        pltpu.VMEM((Q, GW), F32), pltpu.VMEM((Q, GW), F32), pltpu.VMEM((Q, GW), F32),
                        pltpu.VMEM((HALO + Q, GW), F32), pltpu.VMEM((HALO + Q, N), F32), pltpu.VMEM((HALO + Q, N), F32),
                        pltpu.VMEM((Q + HALO, GW), F32), pltpu.VMEM((Q + HALO, N), F32), pltpu.VMEM((Q + HALO, N), F32),
                        pltpu.VMEM((G, HALO, GW), F32), pltpu.VMEM((G, HALO, N), F32), pltpu.VMEM((G, HALO, N), F32)],
        compiler_params=_cparams("arbitrary", "arbitrary", "arbitrary"), name=name,
    )(xbc, xbc, xbc, xbc, xbc, xbc, z, y, dys, dt_raw, dt_rawT, states, conv_w, conv_w, conv_w, conv_b, conv_b, conv_b,
      dt_bias, col(dt_bias), a_log, col(a_log), d_skip, norm_g)


def mix_in_bwd(dh, h, g, dgates, dz, dxs, dbm, dcm, ddt, dq, dk, dv, w_gz, w_xbc, w_dt, w_qkv, *, name, tm=512):
    T, D = h.shape
    nt = T // tm
    GN = SSM_GROUPS * SSM_STATE

    def body(dh_ref, h_ref, g_ref, dgates_ref, dz_ref, dxs_ref, dbm_ref, dcm_ref, ddt_ref, dq_ref, dk_ref, dv_ref,
             wgz_ref, wxbc_ref, wdt_ref, wqkv_ref, dhin_ref, dg_ref):
        @pl.when(pl.program_id(0) == 0)
        def _():
            dg_ref[...] = jnp.zeros_like(dg_ref)

        du = _dot_nt(dgates_ref[...], wgz_ref[:, 0:2048])
        du = du + _dot_nt(dz_ref[...], wgz_ref[:, 2048:4096])
        du = du + _dot_nt(dxs_ref[...], wxbc_ref[:, 0:SSM_D_INNER])
        du = du + _dot_nt(dbm_ref[...], wxbc_ref[:, SSM_D_INNER:SSM_D_INNER + GN])
        du = du + _dot_nt(dcm_ref[...], wxbc_ref[:, SSM_D_INNER + GN:])
        du = du + _dot_nt(ddt_ref[...].astype(BF16), wdt_ref[...])
        du = du + _dot_nt(dq_ref[...], wqkv_ref[:, 0:ATTN_Q_DIM])
        du = du + _dot_nt(dk_ref[...], wqkv_ref[:, ATTN_Q_DIM:ATTN_Q_DIM + ATTN_KV_DIM])
        du = du + _dot_nt(dv_ref[...], wqkv_ref[:, ATTN_Q_DIM + ATTN_KV_DIM:])
        hh = h_ref[...]
        r = lax.rsqrt(jnp.mean(hh * hh, axis=-1, keepdims=True) + RMS_EPS)
        dg_ref[...] += jnp.sum(du * hh * r, axis=0, keepdims=True)
        t = du * g_ref[...]
        dhin_ref[...] = dh_ref[...] + r * t - hh * (r * r * r * jnp.mean(t * hh, axis=-1, keepdims=True))

    sds = jax.ShapeDtypeStruct
    return pl.pallas_call(
        body, grid=(nt,),
        in_specs=[_rows(tm, D), _rows(tm, D), _resident((1, D)), _rows(tm, 2048), _rows(tm, 2048), _rows(tm, SSM_D_INNER),
                  _rows(tm, GN), _rows(tm, GN), _rows(tm, SSM_HEADS), _rows(tm, ATTN_Q_DIM), _rows(tm, ATTN_KV_DIM),
                  _rows(tm, ATTN_KV_DIM), _resident(w_gz.shape), _resident(w_xbc.shape), _resident(w_dt.shape),
                  _resident(w_qkv.shape)],
        out_specs=[_rows(tm, D), pl.BlockSpec((1, D), lambda i: (0, 0))],
        out_shape=[sds((T, D), F32), sds((1, D), F32)],
        compiler_params=_cparams("arbitrary"), name=name,
    )(dh, h, g, dgates, dz, dxs, dbm, dcm, ddt, dq, dk, dv, w_gz, w_xbc, w_dt, w_qkv)


MESH = pl.DeviceIdType.MESH
ANY = pl.BlockSpec(memory_space=pl.ANY)
ROW_ALIGN = 16


def _me():
    return lax.axis_index("x"), lax.axis_index("y"), lax.axis_index("c")


def _other_chips(x, y):
    return [(1 - x, y), (x, 1 - y), (1 - x, 1 - y)]


def _remote(src, dst, send_sem, recv_sem, to):
    return pltpu.make_async_remote_copy(src_ref=src, dst_ref=dst, send_sem=send_sem, recv_sem=recv_sem,
                                        device_id=to, device_id_type=MESH)


def _half(c, rows):
    return pl.ds(pl.multiple_of(c * (rows // 2), ROW_ALIGN), rows // 2)


def ag_weights(shards, *, name):
    n = len(shards)

    def body(*refs):
        ins, outs = refs[:n], refs[n:2 * n]
        loc_sem, ici_send, ici_recv, d2d_send, d2d_recv = refs[2 * n:]
        x, y, c = _me()
        s_me = 2 * x + y
        sib = (x, y, 1 - c)
        chips = _other_chips(x, y)
        local, sent = [], []
        for i in range(n):
            mine = _half(c, ins[i].shape[0])
            local.append(pltpu.make_async_copy(ins[i], outs[i].at[s_me], loc_sem.at[i]))
            local[-1].start()
            for j, chip in enumerate(chips):
                sent.append(_remote(ins[i].at[mine], outs[i].at[s_me, mine], ici_send.at[i, j], ici_recv.at[i, j],
                                    (*chip, c)))
                sent[-1].start()
        for i in range(n):
            mine = _half(c, ins[i].shape[0])
            for j, chip in enumerate(chips):
                landed = outs[i].at[2 * chip[0] + chip[1], mine]
                _remote(landed, landed, ici_send.at[i, j], ici_recv.at[i, j], (*chip, c)).wait_recv()
                sent.append(_remote(landed, landed, d2d_send.at[i, j], d2d_recv.at[i, j], sib))
                sent[-1].start()
        for i in range(n):
            other = _half(1 - c, ins[i].shape[0])
            for j, chip in enumerate(chips):
                lands = outs[i].at[2 * chip[0] + chip[1], other]
                _remote(lands, lands, d2d_send.at[i, j], d2d_recv.at[i, j], sib).wait_recv()
        for cp in sent:
            cp.wait_send()
        for cp in local:
            cp.wait()

    return pl.pallas_call(
        body, in_specs=[ANY] * n, out_specs=[ANY] * n,
        out_shape=[jax.ShapeDtypeStruct((N_SHARD,) + s.shape, s.dtype) for s in shards],
        scratch_shapes=[pltpu.SemaphoreType.DMA((n,)), pltpu.SemaphoreType.DMA((n, 3)), pltpu.SemaphoreType.DMA((n, 3)),
                        pltpu.SemaphoreType.DMA((n, 3)), pltpu.SemaphoreType.DMA((n, 3))],
        name=name)(*shards)


def rs_pair(grads, *, name):
    n = len(grads)

    def body(*refs):
        ins, outs = refs[:n], refs[n:2 * n]
        send, recv = refs[2 * n:]
        x, y, c = _me()
        sib = (x, y, 1 - c)
        sent = []
        for i in range(n):
            rows = ins[i].shape[1]
            sent.append(_remote(ins[i].at[:, _half(1 - c, rows), :], outs[i], send.at[i], recv.at[i], sib))
            sent[-1].start()
        for cp in sent:
            cp.wait()

    return pl.pallas_call(
        body, in_specs=[ANY] * n, out_specs=[ANY] * n,
        out_shape=[jax.ShapeDtypeStruct((N_SHARD, g.shape[1] // 2, g.shape[2]), g.dtype) for g in grads],
        scratch_shapes=[pltpu.SemaphoreType.DMA((n,)), pltpu.SemaphoreType.DMA((n,))], name=name)(*grads)


def rs_add(grad, part, c, *, rt, name):
    _, rows, cols = grad.shape
    r2 = rows // 2
    nrb = r2 // rt

    def body(c_ref, g_ref, p_ref, o_ref):
        o_ref[...] = (g_ref[...] + p_ref[...]).astype(BF16)

    return pl.pallas_call(
        body,
        grid_spec=pltpu.PrefetchScalarGridSpec(
            num_scalar_prefetch=1, grid=(N_SHARD, nrb),
            in_specs=[pl.BlockSpec((1, rt, cols), lambda k, i, c_ref: (k, c_ref[0] * nrb + i, 0)),
                      pl.BlockSpec((1, rt, cols), lambda k, i, c_ref: (k, i, 0))],
            out_specs=pl.BlockSpec((1, rt, cols), lambda k, i, c_ref: (k, i, 0))),
        out_shape=jax.ShapeDtypeStruct((N_SHARD, r2, cols), BF16),
        compiler_params=_cparams("parallel", "parallel"), name=name)(c, grad, part)


def rs_chips(sums, *, name):
    n = len(sums)

    def body(*refs):
        ins, outs = refs[:n], refs[n:2 * n]
        loc_sem, send, recv = refs[2 * n:]
        x, y, c = _me()
        s_me = 2 * x + y
        local, sent = [], []
        for i in range(n):
            local.append(pltpu.make_async_copy(ins[i].at[s_me], outs[i].at[s_me], loc_sem.at[i]))
            local[-1].start()
            for j, chip in enumerate(_other_chips(x, y)):
                sent.append(_remote(ins[i].at[2 * chip[0] + chip[1]], outs[i].at[s_me], send.at[i, j], recv.at[i, j],
                                    (*chip, c)))
                sent[-1].start()
        for cp in sent:
            cp.wait()
        for cp in local:
            cp.wait()

    return pl.pallas_call(
        body, in_specs=[ANY] * n, out_specs=[ANY] * n,
        out_shape=[jax.ShapeDtypeStruct(s.shape, s.dtype) for s in sums],
        scratch_shapes=[pltpu.SemaphoreType.DMA((n,)), pltpu.SemaphoreType.DMA((n, 3)), pltpu.SemaphoreType.DMA((n, 3))],
        name=name)(*sums)


def rs_total(parts, *, rt, name):
    _, r2, cols = parts.shape

    def body(p_ref, o_ref):
        acc = p_ref[0].astype(F32)
        for k in range(1, N_SHARD):
            acc = acc + p_ref[k].astype(F32)
        o_ref[...] = acc

    return pl.pallas_call(
        body, grid=(r2 // rt,),
        in_specs=[pl.BlockSpec((N_SHARD, rt, cols), lambda i: (0, i, 0))],
        out_specs=pl.BlockSpec((rt, cols), lambda i: (i, 0)),
        out_shape=jax.ShapeDtypeStruct((r2, cols), F32),
        compiler_params=_cparams("parallel"), name=name)(parts)


def rs_share(halves, *, name):
    n = len(halves)

    def body(*refs):
        ins, outs = refs[:n], refs[n:2 * n]
        loc_sem, send, recv = refs[2 * n:]
        x, y, c = _me()
        sib = (x, y, 1 - c)
        local, sent = [], []
        for i in range(n):
            mine = _half(c, outs[i].shape[0])
            local.append(pltpu.make_async_copy(ins[i], outs[i].at[mine], loc_sem.at[i]))
            local[-1].start()
            sent.append(_remote(ins[i], outs[i].at[mine], send.at[i], recv.at[i], sib))
            sent[-1].start()
        for i in range(n):
            other = _half(1 - c, outs[i].shape[0])
            _remote(ins[i], outs[i].at[other], send.at[i], recv.at[i], sib).wait_recv()
        for cp in sent:
            cp.wait_send()
        for cp in local:
            cp.wait()

    return pl.pallas_call(
        body, in_specs=[ANY] * n, out_specs=[ANY] * n,
        out_shape=[jax.ShapeDtypeStruct((2 * h.shape[0], h.shape[1]), h.dtype) for h in halves],
        scratch_shapes=[pltpu.SemaphoreType.DMA((n,)), pltpu.SemaphoreType.DMA((n,)), pltpu.SemaphoreType.DMA((n,))],
        name=name)(*halves)


def small_allreduce(buf, *, name):
    rows = buf.shape[0]

    def body(x_ref, o_ref, slots, send, recv):
        x, y, c = _me()
        me = 4 * x + 2 * y + c
        slots[me] = x_ref[...]
        sent = []
        for d in range(1, 8):
            peer = (1 - x if d & 4 else x, 1 - y if d & 2 else y, 1 - c if d & 1 else c)
            sent.append(_remote(x_ref, slots.at[me], send.at[d - 1], recv.at[d - 1], peer))
            sent[-1].start()
        for cp in sent:
            cp.wait()
        acc = slots[0]
        for k in range(1, 8):
            acc = acc + slots[k]
        o_ref[...] = acc

    return pl.pallas_call(
        body, out_shape=jax.ShapeDtypeStruct(buf.shape, F32),
        in_specs=[pl.BlockSpec(memory_space=pltpu.VMEM)], out_specs=pl.BlockSpec(memory_space=pltpu.VMEM),
        scratch_shapes=[pltpu.VMEM((8, rows, 128), F32), pltpu.SemaphoreType.DMA((7,)), pltpu.SemaphoreType.DMA((7,))],
        name=name)(buf)


def adamw(w, g, m, v, *, name, rt=None):
    rows, cols = w.shape
    rt = rows if rt is None else rt
    c1 = 1.0 - ADAM_B1 ** ADAM_STEP
    c2 = 1.0 - ADAM_B2 ** ADAM_STEP

    def body(w_ref, g_ref, m_ref, v_ref, d_ref, nm_ref, nv_ref):
        gg = g_ref[...]
        nm = ADAM_B1 * m_ref[...] + (1.0 - ADAM_B1) * gg
        nv = ADAM_B2 * v_ref[...] + (1.0 - ADAM_B2) * (gg * gg)
        nm_ref[...] = nm
        nv_ref[...] = nv
        d_ref[...] = -ADAM_LR * ((nm / c1) / (jnp.sqrt(nv / c2) + ADAM_EPS) + ADAM_WD * w_ref[...])

    spec = pl.BlockSpec((rt, cols), lambda i: (i, 0))
    return pl.pallas_call(
        body, grid=(rows // rt,), in_specs=[spec] * 4, out_specs=[spec] * 3,
        out_shape=[jax.ShapeDtypeStruct((rows, cols), F32)] * 3,
        compiler_params=_cparams("parallel"), name=name)(w, g, m, v)


WEIGHTS = ['ffn1_pre_g', 'ffn1_w_gate', 'ffn1_w_up', 'ffn1_w_down', 'ffn1_post_g', 'mix_pre_g', 'w_in', 'conv_w',
           'conv_b', 'dt_bias', 'a_log', 'd_skip', 'ssm_norm_g', 'w_ssm_proj', 'attn_sinks', 'rel_bias_table',
           'w_attn_proj', 'w_out', 'mix_post_g', 'ffn2_pre_g', 'ffn2_w_gate', 'ffn2_w_up', 'ffn2_w_down', 'ffn2_post_g']
BIG = ['ffn1_w_gate', 'ffn1_w_up', 'ffn1_w_down', 'w_in', 'w_ssm_proj', 'w_attn_proj', 'w_out',
       'ffn2_w_gate', 'ffn2_w_up', 'ffn2_w_down']
SMALL = [w for w in WEIGHTS if w not in BIG]
PAD_ROWS_1024 = 384


def _bucket_onehot():
    blk = ATTN_BLOCK
    dist = np.maximum(np.arange(blk)[:, None] + blk - np.arange(2 * blk)[None, :], 0)
    max_exact = REL_BUCKETS // 2
    d = np.maximum(dist, 1).astype(np.float32)
    large = max_exact + (np.log(d / np.float32(max_exact)) / np.float32(math.log(REL_MAX_DISTANCE / max_exact))
                         * np.float32(REL_BUCKETS - max_exact)).astype(np.int32)
    bucket = np.where(dist < max_exact, dist, np.minimum(large, REL_BUCKETS - 1)).reshape(-1)
    return jnp.asarray((bucket[None, :] == np.arange(REL_BUCKETS)[:, None]).astype(np.float32))


def _pack_rows(parts, mult=8):
    flat = jnp.concatenate([p.reshape(-1).astype(F32) for p in parts])
    rows = -(-flat.shape[0] // (128 * mult)) * mult
    return jnp.pad(flat, (0, rows * 128 - flat.shape[0])).reshape(rows, 128)


def _unpack_rows(buf, shapes):
    flat = buf.reshape(-1)
    out, at = [], 0
    for shp in shapes:
        size = int(np.prod(shp))
        out.append(flat[at:at + size].reshape(shp))
        at += size
    return out


def kernel(x, ffn1_pre_g, ffn1_w_gate, ffn1_w_up, ffn1_w_down, ffn1_post_g, mix_pre_g, w_in, conv_w, conv_b, dt_bias, a_log, d_skip, ssm_norm_g, w_ssm_proj, attn_sinks, rel_bias_table, w_attn_proj, w_out, mix_post_g, ffn2_pre_g, ffn2_w_gate, ffn2_w_up, ffn2_w_down, ffn2_post_g, loss_target, m_ffn1_pre_g, m_ffn1_w_gate, m_ffn1_w_up, m_ffn1_w_down, m_ffn1_post_g, m_mix_pre_g, m_w_in, m_conv_w, m_conv_b, m_dt_bias, m_a_log, m_d_skip, m_ssm_norm_g, m_w_ssm_proj, m_attn_sinks, m_rel_bias_table, m_w_attn_proj, m_w_out, m_mix_post_g, m_ffn2_pre_g, m_ffn2_w_gate, m_ffn2_w_up, m_ffn2_w_down, m_ffn2_post_g, v_ffn1_pre_g, v_ffn1_w_gate, v_ffn1_w_up, v_ffn1_w_down, v_ffn1_post_g, v_mix_pre_g, v_w_in, v_conv_w, v_conv_b, v_dt_bias, v_a_log, v_d_skip, v_ssm_norm_g, v_w_ssm_proj, v_attn_sinks, v_rel_bias_table, v_w_attn_proj, v_w_out, v_mix_post_g, v_ffn2_pre_g, v_ffn2_w_gate, v_ffn2_w_up, v_ffn2_w_down, v_ffn2_post_g):
    args = locals()
    w = {n: args[n] for n in WEIGHTS}
    m = {n: args["m_" + n] for n in WEIGHTS}
    v = {n: args["v_" + n] for n in WEIGHTS}
    batch, seq, D = x.shape
    T = batch * seq
    xi, yi, ci = _me()
    s_me = 2 * xi + yi
    x2 = x.reshape(T, D)
    tgt = loss_target.reshape(T, D)

    p704 = jnp.concatenate([ffn1_w_gate[0], ffn1_w_up[0], ffn2_w_gate[0], ffn2_w_up[0]], axis=0).astype(BF16)
    p1024 = jnp.concatenate([w_ssm_proj[0], w_attn_proj[0], w_out[0], jnp.zeros((PAD_ROWS_1024, D), F32),
                             ffn1_w_down[0], ffn2_w_down[0]], axis=0).astype(BF16)
    pin = w_in[0].astype(BF16)
    g704, g1024, gin = ag_weights([p704, p1024, pin], name="ag_weights")
    wg1, wu1, wg2, wu2 = (g704[:, D * i:D * (i + 1)] for i in range(4))
    w_ssm = g1024[:, 0:512].reshape(SSM_D_INNER, D)
    w_attn = g1024[:, 512:768].reshape(ATTN_Q_DIM, D)
    w_o = g1024[:, 768:1024].reshape(D, D)
    wd1 = g1024[:, 1408:2112]
    wd2 = g1024[:, 2112:2816]
    w_in_full = gin.transpose(1, 0, 2).reshape(D, IN_COLS)
    w_gz = w_in_full[:, 0:4096]
    w_xbc = w_in_full[:, 4096:4096 + SSM_CONV_DIM]
    w_dt = w_in_full[:, 7168:7200]
    w_qkv = w_in_full[:, 7200:]
    cw_slot = lax.dynamic_update_slice(jnp.zeros((SSM_CONV, SSM_CONV_DIM), F32),
                                       conv_w[0] * (ci == 0).astype(F32), (0, s_me * (SSM_CONV_DIM // N_SHARD)))
    conv_w_full = small_allreduce(cw_slot.reshape(-1, 128), name="ag_conv_w").reshape(SSM_CONV, SSM_CONV_DIM)

    h1, n1, gate1, up1, f1 = ffn_fwd(x2, ffn1_pre_g, wg1, wu1, wd1, ffn1_post_g, name="ffn1_fwd")
    u, gates, z, xbc, dt_raw, dt_rawT, q, k, vv = mix_in_fwd(h1, mix_pre_g, w_gz, w_xbc, w_dt, w_dt.T, w_qkv,
                                                               name="mix_in_fwd")
    y, ys, states = ssd_fwd(xbc, z, dt_raw, dt_rawT, conv_w_full, conv_b, dt_bias, a_log, d_skip, ssm_norm_g,
                            batch=batch, name="ssd_fwd")
    onehot = _bucket_onehot()
    bias = attn_bias(rel_bias_table.T, onehot, name="attn_bias").reshape(ATTN_Q_HEADS, ATTN_BLOCK, 2 * ATTN_BLOCK)
    o, lse = attn_fwd(q, k, vv, bias, attn_sinks, batch=batch, name="attn_fwd")
    h2, y_ssm, y_attn, mix, merged = mix_out_fwd(ys, o, gates, h1, w_ssm, w_attn, w_o, mix_post_g, name="mix_out_fwd")
    h3, n3, gate2, up2, f2, dy, loss_parts = ffn_fwd(h2, ffn2_pre_g, wg2, wu2, wd2, ffn2_post_g, tgt, name="ffn2_fwd")

    dh2, df2, a2, dgate2, dup2, dg_ffn2_pre, dg_ffn2_post = ffn_bwd(dy, h2, f2, gate2, up2, ffn2_pre_g, ffn2_post_g,
                                                                    wg2, wu2, wd2, name="ffn2_bwd")
    dmix, dyssm, dyattn, dgates, dys, do, dg_mix_post = mix_out_bwd(dh2, mix, y_ssm, y_attn, gates, w_o, w_ssm, w_attn,
                                                                    mix_post_g, name="mix_out_bwd")
    dq, dk, dv, dbias, dsinks = attn_bwd(q, k, vv, o, do, lse, bias, attn_sinks, batch=batch, name="attn_bwd")
    dtable = attn_bias_bwd(dbias.reshape(ATTN_Q_HEADS, -1), onehot, name="attn_bias_bwd").T
    dz, dxs, dbm, dcm, ddt, acc_xs, acc_bm, acc_cm, acc_head = ssd_bwd(
        dys, y, xbc, z, dt_raw, dt_rawT, states, conv_w_full, conv_b, dt_bias, a_log, d_skip, ssm_norm_g,
        batch=batch, name="ssd_bwd")
    dh1, dg_mix_pre = mix_in_bwd(dh2, h1, mix_pre_g, dgates, dz, dxs, dbm, dcm, ddt, dq, dk, dv, w_gz, w_xbc, w_dt, w_qkv,
                                 name="mix_in_bwd")
    dx, df1, a1, dgate1, dup1, dg_ffn1_pre, dg_ffn1_post = ffn_bwd(dh1, x2, f1, gate1, up1, ffn1_pre_g, ffn1_post_g,
                                                                   wg1, wu1, wd1, name="ffn1_bwd")

    d704 = jnp.concatenate([mm_tn(n1[None], dgate1, name="dw_gate1"), mm_tn(n1[None], dup1, name="dw_up1"),
                            mm_tn(n3[None], dgate2, name="dw_gate2"), mm_tn(n3[None], dup2, name="dw_up2")], axis=1)
    d1024 = jnp.concatenate([
        mm_tn(ys[None], dyssm[None], name="dw_ssm").reshape(N_SHARD, -1, D),
        mm_tn(o[None], dyattn[None], name="dw_attn").reshape(N_SHARD, -1, D),
        mm_tn(merged[None], dmix[None], name="dw_out").reshape(N_SHARD, -1, D),
        jnp.zeros((N_SHARD, PAD_ROWS_1024, D), F32),
        mm_tn(a1, df1[None], name="dw_down1"), mm_tn(a2, df2[None], name="dw_down2")], axis=1)
    ub = u[None]
    din = jnp.concatenate([
        mm_tn(ub, dgates[None], name="dw_in_gates", tn=1024)[0], mm_tn(ub, dz[None], name="dw_in_z", tn=1024)[0],
        mm_tn(ub, dxs[None], name="dw_in_xs", tn=1024)[0], mm_tn(ub, dbm[None], name="dw_in_b")[0],
        mm_tn(ub, dcm[None], name="dw_in_c")[0], mm_tn(ub, ddt[None], name="dw_in_dt")[0],
        mm_tn(ub, dq[None], name="dw_in_q")[0], mm_tn(ub, dk[None], name="dw_in_k")[0],
        mm_tn(ub, dv[None], name="dw_in_v")[0]], axis=1)
    din = din.reshape(D, N_SHARD, IN_COLS // N_SHARD).transpose(1, 0, 2)

    grads = [d704, d1024, din]
    tiles = [512, 704, 512]
    c_arr = ci.astype(jnp.int32).reshape(1)
    pair = rs_pair(grads, name="rs_pair")
    sums = [rs_add(g, p, c_arr, rt=rt, name=f"rs_add{i}") for i, (g, p, rt) in enumerate(zip(grads, pair, tiles))]
    parts = rs_chips(sums, name="rs_chips")
    halves = [rs_total(p, rt=rt, name=f"rs_total{i}") for i, (p, rt) in enumerate(zip(parts, tiles))]
    r704, r1024, rin = rs_share(halves, name="rs_share")
    gw = {
        'ffn1_w_gate': r704[0:D], 'ffn1_w_up': r704[D:2 * D], 'ffn2_w_gate': r704[2 * D:3 * D], 'ffn2_w_up': r704[3 * D:],
        'w_ssm_proj': r1024[0:512], 'w_attn_proj': r1024[512:768], 'w_out': r1024[768:1024],
        'ffn1_w_down': r1024[1408:2112], 'ffn2_w_down': r1024[2112:2816], 'w_in': rin,
    }

    dconv_w = jnp.concatenate([acc[:, :SSM_CONV].transpose(1, 0, 2).reshape(SSM_CONV, -1)
                               for acc in (acc_xs, acc_bm, acc_cm)], axis=1)
    dconv_b = jnp.concatenate([acc[:, SSM_CONV].reshape(-1) for acc in (acc_xs, acc_bm, acc_cm)])
    small_local = {
        'ffn1_pre_g': dg_ffn1_pre, 'ffn1_post_g': dg_ffn1_post, 'mix_pre_g': dg_mix_pre, 'conv_w': dconv_w,
        'conv_b': dconv_b, 'dt_bias': acc_head[0], 'a_log': acc_head[1], 'd_skip': acc_head[2],
        'ssm_norm_g': acc_xs[:, SSM_CONV + 1].reshape(-1), 'attn_sinks': dsinks, 'rel_bias_table': dtable,
        'mix_post_g': dg_mix_post, 'ffn2_pre_g': dg_ffn2_pre, 'ffn2_post_g': dg_ffn2_post,
    }
    full_shapes = [(SSM_CONV, SSM_CONV_DIM) if n == 'conv_w' else w[n].shape for n in SMALL]
    packed = _pack_rows([small_local[n] for n in SMALL] + [jnp.sum(loss_parts[:, 0, 0])])
    total = small_allreduce(packed, name="allreduce_small")
    *small_g, loss = _unpack_rows(total, full_shapes + [()])
    for n, g in zip(SMALL, small_g):
        gw[n] = g
    gw['conv_w'] = lax.dynamic_slice(gw['conv_w'], (0, s_me * (SSM_CONV_DIM // N_SHARD)),
                                     (SSM_CONV, SSM_CONV_DIM // N_SHARD))[None]

    delta, new_m, new_v = {}, {}, {}
    for n in BIG:
        rows = w[n].shape[1]
        d_, m_, v_ = adamw(w[n][0], gw[n], m[n][0], v[n][0], name="adamw_" + n, rt=rows // 4)
        gw[n] = gw[n][None]
        delta[n], new_m[n], new_v[n] = d_[None], m_[None], v_[None]
    shapes = [w[n].shape for n in SMALL]
    outs = adamw(_pack_rows([w[n] for n in SMALL]), _pack_rows([gw[n] for n in SMALL]),
                 _pack_rows([m[n] for n in SMALL]), _pack_rows([v[n] for n in SMALL]), name="adamw_small")
    for res, buf in zip((delta, new_m, new_v), outs):
        for n, val in zip(SMALL, _unpack_rows(buf, shapes)):
            res[n] = val
    return (loss, dx.reshape(batch, seq, D), *[gw[n].reshape(w[n].shape) for n in WEIGHTS],
            *[delta[n] for n in WEIGHTS], *[new_m[n] for n in WEIGHTS], *[new_v[n] for n in WEIGHTS])
```

```python
import functools
import math

import jax
import jax.numpy as jnp
import numpy as np
from jax import lax
from jax.experimental import pallas as pl
from jax.experimental.pallas import tpu as pltpu

F32 = jnp.float32
BF16 = jnp.bfloat16

D_MODEL = 1024
D_FF = 2816
N_SHARD = 4
SSM_D_INNER = 2048
SSM_HEAD_DIM = 64
SSM_HEADS = 32
SSM_GROUPS = 4
SSM_HPG = SSM_HEADS // SSM_GROUPS
SSM_GW = SSM_D_INNER // SSM_GROUPS
SSM_STATE = 128
SSM_CONV = 4
SSM_CHUNK = 128
SSM_CONV_DIM = SSM_D_INNER + 2 * SSM_GROUPS * SSM_STATE
ATTN_Q_HEADS = 16
ATTN_KV_HEADS = 4
ATTN_REP = ATTN_Q_HEADS // ATTN_KV_HEADS
ATTN_HEAD_DIM = 64
ATTN_BLOCK = 128
ATTN_Q_DIM = 1024
ATTN_KV_DIM = 256
REL_BUCKETS = 32
REL_MAX_DISTANCE = 128
RMS_EPS = 1e-6
IN_COLS = 8736
ADAM_LR = 0.001
ADAM_B1 = 0.9
ADAM_B2 = 0.999
ADAM_EPS = 1e-08
ADAM_WD = 0.01
ADAM_STEP = 10
HALO = 8

VMEM_LIMIT = 56 * 1024 * 1024


def _cparams(*sem):
    return pltpu.CompilerParams(dimension_semantics=tuple(sem) if sem else None, vmem_limit_bytes=VMEM_LIMIT)


def _dot(a, b):
    return jnp.dot(a, b, preferred_element_type=F32)


def _dot_nt(a, b):
    return lax.dot_general(a, b, (((1,), (1,)), ((), ())), preferred_element_type=F32)


def _dot_tn(a, b):
    return lax.dot_general(a, b, (((0,), (0,)), ((), ())), preferred_element_type=F32)


def _dot_hi(a, b):
    return jnp.dot(a, b, preferred_element_type=F32, precision=lax.Precision.HIGHEST)


def _sigmoid(x):
    return 1.0 / (1.0 + jnp.exp(-x))


def _resident(shape, index=None):
    index = (0,) * len(shape) if index is None else tuple(index)
    return pl.BlockSpec(shape, lambda *_: index, pipeline_mode=pl.Buffered(1))


def _part(packed, rows, part):
    return _resident((N_SHARD, rows, packed.shape[2]), (0, part, 0))


def _rows(tm, width):
    return pl.BlockSpec((tm, width), lambda i: (i, 0))


def ffn_fwd(h, g_pre, w704, wdn, g_post, target=None, *, parts, name, tm=512):
    T, D = h.shape
    NS, FS = N_SHARD, w704.shape[2]
    with_loss = target is not None
    nt = T // tm

    def body(*refs):
        if with_loss:
            (h_ref, gpre_ref, wg_ref, wu_ref, wd_ref, gpost_ref, tgt_ref,
             hout_ref, n_ref, gate_ref, up_ref, f_ref, dy_ref, loss_ref) = refs
        else:
            (h_ref, gpre_ref, wg_ref, wu_ref, wd_ref, gpost_ref,
             hout_ref, n_ref, gate_ref, up_ref, f_ref) = refs
        hh = h_ref[...]
        r = lax.rsqrt(jnp.mean(hh * hh, axis=-1, keepdims=True) + RMS_EPS)
        n = (hh * r * gpre_ref[...]).astype(BF16)
        n_ref[...] = n
        acc = jnp.zeros((tm, D), F32)
        for s in range(NS):
            gate = _dot(n, wg_ref[s])
            up = _dot(n, wu_ref[s])
            gate_ref[s] = gate.astype(BF16)
            up_ref[s] = up.astype(BF16)
            a = (gate * _sigmoid(gate) * up).astype(BF16)
            acc = acc + _dot(a, wd_ref[s])
        f_ref[...] = acc
        r2 = lax.rsqrt(jnp.mean(acc * acc, axis=-1, keepdims=True) + RMS_EPS)
        out = hh + 0.5 * (acc * r2 * gpost_ref[...])
        hout_ref[...] = out
        if with_loss:
            e = out - tgt_ref[...]
            dy_ref[...] = e * (1.0 / D)
            loss_ref[...] = jnp.full((1, 8, 128), 0.5 / D, F32) * jnp.sum(e * e)

    in_specs = [_rows(tm, D), _resident((1, D)), _part(w704, D, parts[0]), _part(w704, D, parts[1]),
                _part(wdn, FS, parts[2]), _resident((1, D))]
    args = [h, g_pre, w704, w704, wdn, g_post]
    out_shape = [jax.ShapeDtypeStruct((T, D), F32), jax.ShapeDtypeStruct((T, D), BF16),
                 jax.ShapeDtypeStruct((NS, T, FS), BF16), jax.ShapeDtypeStruct((NS, T, FS), BF16),
                 jax.ShapeDtypeStruct((T, D), F32)]
    seg = pl.BlockSpec((NS, tm, FS), lambda i: (0, i, 0))
    out_specs = [_rows(tm, D), _rows(tm, D), seg, seg, _rows(tm, D)]
    if with_loss:
        in_specs.append(_rows(tm, D))
        args.append(target)
        out_shape += [jax.ShapeDtypeStruct((T, D), F32), jax.ShapeDtypeStruct((nt, 8, 128), F32)]
        out_specs += [_rows(tm, D), pl.BlockSpec((1, 8, 128), lambda i: (i, 0, 0))]
    return pl.pallas_call(body, grid=(nt,), in_specs=in_specs, out_specs=out_specs, out_shape=out_shape,
                          compiler_params=_cparams("parallel"), name=name)(*args)


def ffn_bwd(dout, h, f, gate, up, g_pre, g_post, w704, wdn, *, parts, name, tm=256):
    T, D = h.shape
    NS, FS = N_SHARD, w704.shape[2]
    nt = T // tm

    def body(dout_ref, h_ref, f_ref, gate_ref, up_ref, gpre_ref, gpost_ref, wg_ref, wu_ref, wd_ref,
             dh_ref, df_ref, a_ref, dgate_ref, dup_ref, dgpre_ref, dgpost_ref):
        @pl.when(pl.program_id(0) == 0)
        def _():
            dgpre_ref[...] = jnp.zeros_like(dgpre_ref)
            dgpost_ref[...] = jnp.zeros_like(dgpost_ref)

        do = dout_ref[...]
        ff = f_ref[...]
        d_fn = 0.5 * do
        r2 = lax.rsqrt(jnp.mean(ff * ff, axis=-1, keepdims=True) + RMS_EPS)
        dgpost_ref[...] += jnp.sum(d_fn * ff * r2, axis=0, keepdims=True)
        t = d_fn * gpost_ref[...]
        df = r2 * t - ff * (r2 * r2 * r2 * jnp.mean(t * ff, axis=-1, keepdims=True))
        dfb = df.astype(BF16)
        df_ref[...] = dfb
        dn = jnp.zeros((tm, D), F32)
        for s in range(NS):
            da = _dot_nt(dfb, wd_ref[s])
            g = gate_ref[s].astype(F32)
            u = up_ref[s].astype(F32)
            sg = _sigmoid(g)
            silu = g * sg
            a_ref[s] = (silu * u).astype(BF16)
            dgt = (da * u * (sg * (1.0 + g * (1.0 - sg)))).astype(BF16)
            dupv = (da * silu).astype(BF16)
            dgate_ref[s] = dgt
            dup_ref[s] = dupv
            dn = dn + _dot_nt(dgt, wg_ref[s]) + _dot_nt(dupv, wu_ref[s])
        hh = h_ref[...]
        r1 = lax.rsqrt(jnp.mean(hh * hh, axis=-1, keepdims=True) + RMS_EPS)
        dgpre_ref[...] += jnp.sum(dn * hh * r1, axis=0, keepdims=True)
        t = dn * gpre_ref[...]
        dh_ref[...] = do + r1 * t - hh * (r1 * r1 * r1 * jnp.mean(t * hh, axis=-1, keepdims=True))

    seg = pl.BlockSpec((NS, tm, FS), lambda i: (0, i, 0))
    acc = pl.BlockSpec((1, D), lambda i: (0, 0))
    return pl.pallas_call(
        body, grid=(nt,),
        in_specs=[_rows(tm, D), _rows(tm, D), _rows(tm, D), seg, seg, _resident((1, D)), _resident((1, D)),
                  _part(w704, D, parts[0]), _part(w704, D, parts[1]), _part(wdn, FS, parts[2])],
        out_specs=[_rows(tm, D), _rows(tm, D), seg, seg, seg, acc, acc],
        out_shape=[jax.ShapeDtypeStruct((T, D), F32), jax.ShapeDtypeStruct((T, D), BF16),
                   jax.ShapeDtypeStruct((NS, T, FS), BF16), jax.ShapeDtypeStruct((NS, T, FS), BF16),
                   jax.ShapeDtypeStruct((NS, T, FS), BF16),
                   jax.ShapeDtypeStruct((1, D), F32), jax.ShapeDtypeStruct((1, D), F32)],
        compiler_params=_cparams("arbitrary"), name=name)(dout, h, f, gate, up, g_pre, g_post, w704, w704, wdn)


def mm_tn(a, g, *, name, tt=1024, tn=None, a_cols=None, into=None):
    Ba, T, _ = a.shape
    Bg, _, N = g.shape
    B, K = a_cols if a_cols else (max(Ba, Bg), a.shape[2])
    tn = N if tn is None else tn
    tt = min(tt, T)
    nsteps = T // tt

    def body(*refs):
        a_ref, g_ref, o_ref = refs[0], refs[1], refs[-1]

        @pl.when(pl.program_id(2) == 0)
        def _():
            o_ref[...] = jnp.zeros_like(o_ref)

        o_ref[0] += _dot_tn(a_ref[0], g_ref[0].astype(BF16))

    if a_cols:
        a_map = lambda b, j, t: (0, t, b)
    else:
        a_map = (lambda b, j, t: (b, t, 0)) if Ba > 1 else (lambda b, j, t: (0, t, 0))
    in_specs = [pl.BlockSpec((1, tt, K), a_map),
                pl.BlockSpec((1, tt, tn), (lambda b, j, t: (b, t, j)) if Bg > 1 else (lambda b, j, t: (0, t, j)))]
    args = [a, g]
    if into is None:
        out_shape, part, aliases = jax.ShapeDtypeStruct((B, K, N), F32), 0, {}
    else:
        buf, part = into
        out_shape, aliases = jax.ShapeDtypeStruct(buf.shape, F32), {2: 0}
        in_specs.append(ANY)
        args.append(buf)
    return pl.pallas_call(
        body, grid=(B, N // tn, nsteps), in_specs=in_specs,
        out_specs=pl.BlockSpec((1, K, tn), lambda b, j, t: (b, part, j)),
        out_shape=out_shape, input_output_aliases=aliases,
        compiler_params=_cparams("parallel", "parallel", "arbitrary"), name=name)(*args)


def mix_in_fwd(h, g, w_gz, w_xbc, w_dt, w_dtT, w_qkv, *, name, tm=256):
    T, D = h.shape
    nt = T // tm
    CB = 1024

    def body(h_ref, g_ref, wgz_ref, wxbc_ref, wdt_ref, wdtT_ref, wqkv_ref,
             u_ref, gates_ref, z_ref, xbc_ref, dt_ref, dtT_ref, q_ref, k_ref, v_ref):
        hh = h_ref[...]
        r = lax.rsqrt(jnp.mean(hh * hh, axis=-1, keepdims=True) + RMS_EPS)
        u = (hh * r * g_ref[...]).astype(BF16)
        u_ref[...] = u
        for cb in range(0, 2048, CB):
            gates_ref[:, cb:cb + CB] = _dot(u, wgz_ref[:, cb:cb + CB])
            z_ref[:, cb:cb + CB] = _dot(u, wgz_ref[:, 2048 + cb:2048 + cb + CB])
        for cb in range(0, SSM_CONV_DIM, CB):
            xbc_ref[:, cb:cb + CB] = _dot(u, wxbc_ref[:, cb:cb + CB])
        dt_ref[...] = _dot(u, wdt_ref[...])
        dtT_ref[...] = _dot_nt(wdtT_ref[...], u)
        q_ref[...] = _dot(u, wqkv_ref[:, 0:ATTN_Q_DIM]).astype(BF16)
        k_ref[...] = _dot(u, wqkv_ref[:, ATTN_Q_DIM:ATTN_Q_DIM + ATTN_KV_DIM]).astype(BF16)
        v_ref[...] = _dot(u, wqkv_ref[:, ATTN_Q_DIM + ATTN_KV_DIM:]).astype(BF16)

    sds = jax.ShapeDtypeStruct
    return pl.pallas_call(
        body, grid=(nt,),
        in_specs=[_rows(tm, D), _resident((1, D)), _resident(w_gz.shape), _resident(w_xbc.shape),
                  _resident(w_dt.shape), _resident(w_dtT.shape), _resident(w_qkv.shape)],
        out_specs=[_rows(tm, D), _rows(tm, 2048), _rows(tm, 2048), _rows(tm, SSM_CONV_DIM), _rows(tm, SSM_HEADS),
                   pl.BlockSpec((SSM_HEADS, tm), lambda i: (0, i)),
                   _rows(tm, ATTN_Q_DIM), _rows(tm, ATTN_KV_DIM), _rows(tm, ATTN_KV_DIM)],
        out_shape=[sds((T, D), BF16), sds((T, 2048), F32), sds((T, 2048), F32), sds((T, SSM_CONV_DIM), F32),
                   sds((T, SSM_HEADS), F32), sds((SSM_HEADS, T), F32),
                   sds((T, ATTN_Q_DIM), BF16), sds((T, ATTN_KV_DIM), BF16), sds((T, ATTN_KV_DIM), BF16)],
        compiler_params=_cparams("parallel"), name=name)(h, g, w_gz, w_xbc, w_dt, w_dtT, w_qkv)


def _softplus(x):
    return jnp.maximum(x, 0.0) + jnp.log(1.0 + jnp.exp(-jnp.abs(x)))


def _iota(shape, axis):
    return lax.broadcasted_iota(jnp.int32, shape, axis)


def _head_expand(g, per_head):
    shape = (SSM_HEADS, SSM_HPG * per_head)
    head = lax.shift_right_logical(_iota(shape, 1), int(math.log2(per_head)))
    return (_iota(shape, 0) == SSM_HPG * g + head).astype(F32)


def _conv_pre(x_ref, halo_ref, w_ref, b_ref, xp_ref, first):
    Q = SSM_CHUNK
    halo = jnp.where(first, 0.0, halo_ref[...])
    xp_ref[0:HALO, :] = halo
    xp_ref[HALO:HALO + Q, :] = x_ref[...]
    pre = b_ref[...] + w_ref[3:4, :] * xp_ref[HALO:HALO + Q, :]
    for k in range(SSM_CONV - 1):
        pre = pre + w_ref[k:k + 1, :] * xp_ref[pl.ds(HALO - 3 + k, Q), :]
    return pre


def _ssd_specs(nc):
    Q, GW, N = SSM_CHUNK, SSM_GW, SSM_STATE
    nb_xs = SSM_D_INNER // N
    nb_c = nb_xs + SSM_GROUPS

    def rb(cmap):
        def row(b, c, g):
            return b * nc + cmap(c)
        return row

    def specs(cmap):
        row = rb(cmap)
        hrow = lambda b, c, g: jnp.maximum(row(b, c, g) * (Q // HALO) - 1, 0)
        return dict(
            xs=pl.BlockSpec((Q, GW), lambda b, c, g: (row(b, c, g), g)),
            bm=pl.BlockSpec((Q, N), lambda b, c, g: (row(b, c, g), nb_xs + g)),
            cm=pl.BlockSpec((Q, N), lambda b, c, g: (row(b, c, g), nb_c + g)),
            xs_halo=pl.BlockSpec((HALO, GW), lambda b, c, g: (hrow(b, c, g), g)),
            bm_halo=pl.BlockSpec((HALO, N), lambda b, c, g: (hrow(b, c, g), nb_xs + g)),
            cm_halo=pl.BlockSpec((HALO, N), lambda b, c, g: (hrow(b, c, g), nb_c + g)),
            grp=pl.BlockSpec((Q, GW), lambda b, c, g: (row(b, c, g), g)),
            dt=pl.BlockSpec((Q, SSM_HEADS), lambda b, c, g: (row(b, c, g), 0)),
            dtT=pl.BlockSpec((SSM_HEADS, Q), lambda b, c, g: (0, row(b, c, g))),
            w_xs=pl.BlockSpec((SSM_CONV, GW), lambda b, c, g: (0, g)),
            w_bm=pl.BlockSpec((SSM_CONV, N), lambda b, c, g: (0, nb_xs + g)),
            w_cm=pl.BlockSpec((SSM_CONV, N), lambda b, c, g: (0, nb_c + g)),
            b_xs=pl.BlockSpec((1, GW), lambda b, c, g: (0, g)),
            b_bm=pl.BlockSpec((1, N), lambda b, c, g: (0, nb_xs + g)),
            b_cm=pl.BlockSpec((1, N), lambda b, c, g: (0, nb_c + g)),
            vec_g=pl.BlockSpec((1, GW), lambda b, c, g: (0, g)),
            row32=pl.BlockSpec((1, SSM_HEADS), lambda b, c, g: (0, 0)),
            col32=pl.BlockSpec((SSM_HEADS, 1), lambda b, c, g: (0, 0)),
            state=pl.BlockSpec((1, SSM_HPG, SSM_HEAD_DIM, N), lambda b, c, g: (row(b, c, g), g, 0, 0)),
        )
    return specs


def _ssd_chunk_common(first, g, xs_ref, bm_ref, cm_ref, xs_halo, bm_halo, cm_halo, w_xs, w_bm, w_cm, b_xs, b_bm, b_cm,
                      dt_ref, dtT_ref, dtb_ref, dtbT_ref, alog_ref, alogT_ref, xp_xs, xp_bm, xp_cm):
    Q = SSM_CHUNK
    pre_xs = _conv_pre(xs_ref, xs_halo, w_xs, b_xs, xp_xs, first)
    pre_bm = _conv_pre(bm_ref, bm_halo, w_bm, b_bm, xp_bm, first)
    pre_cm = _conv_pre(cm_ref, cm_halo, w_cm, b_cm, xp_cm, first)
    xs = pre_xs * _sigmoid(pre_xs)
    bm = pre_bm * _sigmoid(pre_bm)
    cm = pre_cm * _sigmoid(pre_cm)
    dtr = dt_ref[...] + dtb_ref[...]
    dtrT = dtT_ref[...] + dtbT_ref[...]
    dt = _softplus(dtr)
    dtT = _softplus(dtrT)
    a = -jnp.exp(alog_ref[...])
    aT = -jnp.exp(alogT_ref[...])
    tri = (_iota((Q, Q), 0) >= _iota((Q, Q), 1)).astype(F32)
    triT = (_iota((Q, Q), 0) <= _iota((Q, Q), 1)).astype(F32)
    acs = _dot_hi(tri, dt * a)
    acsT = _dot_hi(dtT * aT, triT)
    return dict(pre_xs=pre_xs, pre_bm=pre_bm, pre_cm=pre_cm, xs=xs, bm=bm, cm=cm, dtr=dtr, dt=dt, a=a,
                acs=acs, acsT=acsT, tri=tri, triT=triT)


def ssd_fwd(xbc, z, dt_raw, dt_rawT, conv_w, conv_b, dt_bias, a_log, d_skip, norm_g, *, batch, name):
    T = xbc.shape[0]
    Q, GW, N, P, HPG = SSM_CHUNK, SSM_GW, SSM_STATE, SSM_HEAD_DIM, SSM_HPG
    nc = T // batch // Q
    sp = _ssd_specs(nc)(lambda c: c)

    def body(xs_ref, bm_ref, cm_ref, xs_halo, bm_halo, cm_halo, z_ref, dt_ref, dtT_ref,
             w_xs, w_bm, w_cm, b_xs, b_bm, b_cm, dtb_ref, dtbT_ref, alog_ref, alogT_ref, dskip_ref, ng_ref,
             y_ref, ys_ref, st_ref, state, acsT_s, y_s, xp_xs, xp_bm, xp_cm):
        c = pl.program_id(1)
        g = pl.program_id(2)
        first = c == 0
        cc = _ssd_chunk_common(first, g, xs_ref, bm_ref, cm_ref, xs_halo, bm_halo, cm_halo, w_xs, w_bm, w_cm,
                               b_xs, b_bm, b_cm, dt_ref, dtT_ref, dtb_ref, dtbT_ref, alog_ref, alogT_ref,
                               xp_xs, xp_bm, xp_cm)
        xs, acs = cc["xs"], cc["acs"]
        acsT_s[...] = cc["acsT"]
        e64 = _head_expand(g, P)
        e128 = _head_expand(g, Q)
        acs_x = _dot_hi(acs, e64)
        acs_b = _dot_hi(acs, e128)
        x = xs * _dot_hi(cc["dt"], e64)
        last_x = acs_x[Q - 1:Q, :]
        xw = (x * jnp.exp(last_x - acs_x)).astype(BF16)
        ex = jnp.exp(acs_x)
        xb = x.astype(BF16)
        bb = cc["bm"].astype(BF16)
        cb = cc["cm"].astype(BF16)
        s = _dot_nt(cb, bb)
        causal = _iota((Q, Q), 0) >= _iota((Q, Q), 1)
        for r in range(HPG):
            hd = HPG * g + r

            @pl.when(first)
            def _():
                state[hd] = jnp.zeros((P, N), F32)

            seg = acs_b[:, Q * r:Q * (r + 1)] - acsT_s[pl.ds(hd, 1), :]
            m = (s * jnp.exp(jnp.where(causal, seg, -1e30))).astype(BF16)
            hp = state[hd]
            st_ref[0, r] = hp
            y_h = _dot(m, xb[:, P * r:P * (r + 1)]) + _dot_nt(cb, hp.astype(BF16)) * ex[:, P * r:P * (r + 1)]
            y_s[:, P * r:P * (r + 1)] = y_h
            decay = jnp.exp(acsT_s[pl.ds(hd, 1), pl.ds(Q - 1, 1)])
            state[hd] = hp * decay + _dot_tn(xw[:, P * r:P * (r + 1)], bb)
        dexp = _dot_hi(jnp.broadcast_to(dskip_ref[...], (8, SSM_HEADS)), e64)[0:1, :]
        y = y_s[...] + dexp * xs
        y_ref[...] = y
        zz = z_ref[...]
        yg = y * (zz * _sigmoid(zz))
        rr = lax.rsqrt(jnp.mean(yg * yg, axis=-1, keepdims=True) + RMS_EPS)
        ys_ref[...] = (yg * rr * ng_ref[...]).astype(BF16)

    col = lambda v: v.reshape(SSM_HEADS, 1)
    sds = jax.ShapeDtypeStruct
    return pl.pallas_call(
        body, grid=(batch, nc, SSM_GROUPS),
        in_specs=[sp["xs"], sp["bm"], sp["cm"], sp["xs_halo"], sp["bm_halo"], sp["cm_halo"], sp["grp"], sp["dt"],
                  sp["dtT"], sp["w_xs"], sp["w_bm"], sp["w_cm"], sp["b_xs"], sp["b_bm"], sp["b_cm"],
                  sp["row32"], sp["col32"], sp["row32"], sp["col32"], sp["row32"], sp["vec_g"]],
        out_specs=[sp["grp"], sp["grp"], sp["state"]],
        out_shape=[sds((T, SSM_D_INNER), F32), sds((T, SSM_D_INNER), BF16),
                   sds((T // Q, SSM_HEADS, P, N), F32)],
        scratch_shapes=[pltpu.VMEM((SSM_HEADS, P, N), F32), pltpu.VMEM((SSM_HEADS, Q), F32), pltpu.VMEM((Q, GW), F32),
                        pltpu.VMEM((HALO + Q, GW), F32), pltpu.VMEM((HALO + Q, N), F32), pltpu.VMEM((HALO + Q, N), F32)],
        compiler_params=_cparams("arbitrary", "arbitrary", "arbitrary"), name=name,
    )(xbc, xbc, xbc, xbc, xbc, xbc, z, dt_raw, dt_rawT, conv_w, conv_w, conv_w, conv_b, conv_b, conv_b,
      dt_bias, col(dt_bias), a_log, col(a_log), d_skip, norm_g)


def _attn_specs(nb):
    BLK = ATTN_BLOCK

    def specs(last):
        def cur(b, n):
            return b * nb + (n if last is None else jnp.minimum(n, nb - 1))

        def prev(b, n):
            return b * nb + jnp.maximum((n if last is None else jnp.minimum(n, nb - 1)) - 1, 0)
        return cur, prev
    return specs


def attn_bias(table_t, onehot, *, name):
    def body(t_ref, f_ref, o_ref):
        o_ref[...] = _dot_hi(t_ref[...], f_ref[...])
    return pl.pallas_call(body, out_shape=jax.ShapeDtypeStruct((ATTN_Q_HEADS, onehot.shape[1]), F32),
                          compiler_params=_cparams(), name=name)(table_t, onehot)


def attn_bias_bwd(dbias, onehot, *, name):
    def body(d_ref, f_ref, o_ref):
        o_ref[...] = lax.dot_general(d_ref[...], f_ref[...], (((1,), (1,)), ((), ())), preferred_element_type=F32,
                                     precision=lax.Precision.HIGHEST)
    return pl.pallas_call(body, out_shape=jax.ShapeDtypeStruct((ATTN_Q_HEADS, REL_BUCKETS), F32),
                          compiler_params=_cparams(), name=name)(dbias, onehot)


def attn_fwd(q, k, v, bias, sinks, *, batch, name):
    T = q.shape[0]
    BLK, HD = ATTN_BLOCK, ATTN_HEAD_DIM
    nb = T // batch // BLK
    cur, prev = _attn_specs(nb)(None)
    scale = HD ** -0.5

    def body(q_ref, kc_ref, kp_ref, vc_ref, vp_ref, bias_ref, sink_ref, o_ref, lse_ref):
        n = pl.program_id(1)
        ii = _iota((BLK, BLK), 0)
        jj = _iota((BLK, BLK), 1)
        m_prev = jnp.logical_and(jj > ii, n > 0)
        m_cur = jj <= ii
        for kk in range(ATTN_KV_HEADS):
            ks = slice(HD * kk, HD * (kk + 1))
            kc, kp, vc, vp = kc_ref[:, ks], kp_ref[:, ks], vc_ref[:, ks], vp_ref[:, ks]
            for r in range(ATTN_REP):
                hd = ATTN_REP * kk + r
                qh = q_ref[:, HD * hd:HD * (hd + 1)]
                lp = jnp.where(m_prev, _dot_nt(qh, kp) * scale + bias_ref[hd, :, 0:BLK], -1e30)
                lc = jnp.where(m_cur, _dot_nt(qh, kc) * scale + bias_ref[hd, :, BLK:2 * BLK], -1e30)
                sink = sink_ref[0:1, hd:hd + 1]
                mx = jnp.maximum(jnp.maximum(jnp.max(lp, axis=-1, keepdims=True), jnp.max(lc, axis=-1, keepdims=True)), sink)
                pp = jnp.exp(lp - mx)
                pc = jnp.exp(lc - mx)
                den = jnp.sum(pp, axis=-1, keepdims=True) + jnp.sum(pc, axis=-1, keepdims=True) + jnp.exp(sink - mx)
                inv = 1.0 / den
                o = _dot((pp * inv).astype(BF16), vp) + _dot((pc * inv).astype(BF16), vc)
                o_ref[:, HD * hd:HD * (hd + 1)] = o.astype(BF16)
                lse_ref[:, hd:hd + 1] = mx + jnp.log(den)

    sds = jax.ShapeDtypeStruct
    return pl.pallas_call(
        body, grid=(batch, nb),
        in_specs=[pl.BlockSpec((BLK, ATTN_Q_DIM), lambda b, n: (cur(b, n), 0)),
                  pl.BlockSpec((BLK, ATTN_KV_DIM), lambda b, n: (cur(b, n), 0)),
                  pl.BlockSpec((BLK, ATTN_KV_DIM), lambda b, n: (prev(b, n), 0)),
                  pl.BlockSpec((BLK, ATTN_KV_DIM), lambda b, n: (cur(b, n), 0)),
                  pl.BlockSpec((BLK, ATTN_KV_DIM), lambda b, n: (prev(b, n), 0)),
                  pl.BlockSpec((ATTN_Q_HEADS, BLK, 2 * BLK), lambda b, n: (0, 0, 0)),
                  pl.BlockSpec((1, ATTN_Q_HEADS), lambda b, n: (0, 0))],
        out_specs=[pl.BlockSpec((BLK, ATTN_Q_DIM), lambda b, n: (cur(b, n), 0)),
                   pl.BlockSpec((BLK, ATTN_Q_HEADS), lambda b, n: (cur(b, n), 0))],
        out_shape=[sds((T, ATTN_Q_DIM), BF16), sds((T, ATTN_Q_HEADS), F32)],
        compiler_params=_cparams("parallel", "parallel"), name=name)(q, k, k, v, v, bias, sinks)


def _proj_specs(wmix):
    return [_part(wmix, 512, 0), _part(wmix, 256, 2), _part(wmix, 256, 3)]


def _natural(w_ref):
    return w_ref[...].reshape(-1, w_ref.shape[2])


def mix_out_fwd(ys, o, gates, h, wmix, g_post, *, name, tm=512):
    T, D = h.shape
    nt = T // tm

    def body(ys_ref, o_ref, gates_ref, h_ref, wssm_ref, wattn_ref, wout_ref, g_ref,
             hout_ref, yssm_ref, yattn_ref, mix_ref, merged_ref):
        y_ssm = _dot(ys_ref[...], _natural(wssm_ref))
        y_attn = _dot(o_ref[...], _natural(wattn_ref))
        yssm_ref[...] = y_ssm
        yattn_ref[...] = y_attn
        merged = (_sigmoid(gates_ref[:, 0:D]) * y_ssm + _sigmoid(gates_ref[:, D:2 * D]) * y_attn).astype(BF16)
        merged_ref[...] = merged
        mix = _dot(merged, _natural(wout_ref))
        mix_ref[...] = mix
        r = lax.rsqrt(jnp.mean(mix * mix, axis=-1, keepdims=True) + RMS_EPS)
        hout_ref[...] = h_ref[...] + mix * r * g_ref[...]

    sds = jax.ShapeDtypeStruct
    return pl.pallas_call(
        body, grid=(nt,),
        in_specs=[_rows(tm, SSM_D_INNER), _rows(tm, ATTN_Q_DIM), _rows(tm, 2 * D), _rows(tm, D),
                  *_proj_specs(wmix), _resident((1, D))],
        out_specs=[_rows(tm, D)] * 5,
        out_shape=[sds((T, D), F32), sds((T, D), F32), sds((T, D), F32), sds((T, D), F32), sds((T, D), BF16)],
        compiler_params=_cparams("parallel"), name=name)(ys, o, gates, h, wmix, wmix, wmix, g_post)


def mix_out_bwd(dh, mix, y_ssm, y_attn, gates, wmix, g_post, *, name, tm=256):
    T, D = dh.shape
    nt = T // tm

    def body(dh_ref, mix_ref, yssm_ref, yattn_ref, gates_ref, wssm_ref, wattn_ref, wout_ref, g_ref,
             dmix_ref, dyssm_ref, dyattn_ref, dgates_ref, dys_ref, do_ref, dg_ref):
        @pl.when(pl.program_id(0) == 0)
        def _():
            dg_ref[...] = jnp.zeros_like(dg_ref)

        do = dh_ref[...]
        mix = mix_ref[...]
        r = lax.rsqrt(jnp.mean(mix * mix, axis=-1, keepdims=True) + RMS_EPS)
        dg_ref[...] += jnp.sum(do * mix * r, axis=0, keepdims=True)
        t = do * g_ref[...]
        dmix = (r * t - mix * (r * r * r * jnp.mean(t * mix, axis=-1, keepdims=True))).astype(BF16)
        dmix_ref[...] = dmix
        dmerged = _dot_nt(dmix, _natural(wout_ref))
        s1 = _sigmoid(gates_ref[:, 0:D])
        s2 = _sigmoid(gates_ref[:, D:2 * D])
        dyssm = (dmerged * s1).astype(BF16)
        dyattn = (dmerged * s2).astype(BF16)
        dyssm_ref[...] = dyssm
        dyattn_ref[...] = dyattn
        dgates_ref[:, 0:D] = (dmerged * yssm_ref[...] * (s1 * (1.0 - s1))).astype(BF16)
        dgates_ref[:, D:2 * D] = (dmerged * yattn_ref[...] * (s2 * (1.0 - s2))).astype(BF16)
        dys_ref[...] = _dot_nt(dyssm, _natural(wssm_ref))
        do_ref[...] = _dot_nt(dyattn, _natural(wattn_ref)).astype(BF16)

    sds = jax.ShapeDtypeStruct
    return pl.pallas_call(
        body, grid=(nt,),
        in_specs=[_rows(tm, D), _rows(tm, D), _rows(tm, D), _rows(tm, D), _rows(tm, 2 * D),
                  *_proj_specs(wmix), _resident((1, D))],
        out_specs=[_rows(tm, D), _rows(tm, D), _rows(tm, D), _rows(tm, 2 * D), _rows(tm, SSM_D_INNER),
                   _rows(tm, ATTN_Q_DIM), pl.BlockSpec((1, D), lambda i: (0, 0))],
        out_shape=[sds((T, D), BF16), sds((T, D), BF16), sds((T, D), BF16), sds((T, 2 * D), BF16),
                   sds((T, SSM_D_INNER), F32), sds((T, ATTN_Q_DIM), BF16), sds((1, D), F32)],
        compiler_params=_cparams("arbitrary"), name=name)(dh, mix, y_ssm, y_attn, gates, wmix, wmix, wmix, g_post)


def attn_bwd(q, k, v, o, do, lse, bias, sinks, *, batch, name):
    T = q.shape[0]
    BLK, HD = ATTN_BLOCK, ATTN_HEAD_DIM
    nb = T // batch // BLK
    cur, prev = _attn_specs(nb)(nb)
    scale = HD ** -0.5

    def body(q_ref, kc_ref, kp_ref, vc_ref, vp_ref, o_ref, do_ref, lse_ref, bias_ref, sink_ref,
             dq_ref, dk_ref, dv_ref, dbias_ref, dsink_ref, ck, cv):
        b = pl.program_id(0)
        n = pl.program_id(1)

        @pl.when(jnp.logical_and(b == 0, n == 0))
        def _():
            dbias_ref[...] = jnp.zeros_like(dbias_ref)
            dsink_ref[...] = jnp.zeros_like(dsink_ref)

        @pl.when(n == 0)
        def _():
            ck[...] = jnp.zeros_like(ck)
            cv[...] = jnp.zeros_like(cv)

        @pl.when(n == nb)
        def _():
            dk_ref[...] = ck[...].astype(BF16)
            dv_ref[...] = cv[...].astype(BF16)

        @pl.when(n < nb)
        def _():
            ii = _iota((BLK, BLK), 0)
            jj = _iota((BLK, BLK), 1)
            m_prev = jnp.logical_and(jj > ii, n > 0)
            m_cur = jj <= ii
            lane16 = _iota((1, ATTN_Q_HEADS), 1)
            dsink = jnp.zeros((1, ATTN_Q_HEADS), F32)
            for kk in range(ATTN_KV_HEADS):
                ks = slice(HD * kk, HD * (kk + 1))
                kc, kp, vc, vp = kc_ref[:, ks], kp_ref[:, ks], vc_ref[:, ks], vp_ref[:, ks]
                dkp = jnp.zeros((BLK, HD), F32)
                dkc = jnp.zeros((BLK, HD), F32)
                dvp = jnp.zeros((BLK, HD), F32)
                dvc = jnp.zeros((BLK, HD), F32)
                for r in range(ATTN_REP):
                    hd = ATTN_REP * kk + r
                    hs = slice(HD * hd, HD * (hd + 1))
                    qh = q_ref[:, hs]
                    doh = do_ref[:, hs]
                    lse_h = lse_ref[:, hd:hd + 1]
                    lp = jnp.where(m_prev, _dot_nt(qh, kp) * scale + bias_ref[hd, :, 0:BLK], -1e30)
                    lc = jnp.where(m_cur, _dot_nt(qh, kc) * scale + bias_ref[hd, :, BLK:2 * BLK], -1e30)
                    pp = jnp.exp(lp - lse_h)
                    pc = jnp.exp(lc - lse_h)
                    delta = jnp.sum(doh.astype(F32) * o_ref[:, hs].astype(F32), axis=-1, keepdims=True)
                    dlp = pp * (_dot_nt(doh, vp) - delta)
                    dlc = pc * (_dot_nt(doh, vc) - delta)
                    ps = jnp.exp(sink_ref[0:1, hd:hd + 1] - lse_h)
                    dsink = dsink + jnp.where(lane16 == hd, -jnp.sum(ps * delta, axis=0, keepdims=True), 0.0)
                    dbias_ref[hd, :, 0:BLK] += dlp
                    dbias_ref[hd, :, BLK:2 * BLK] += dlc
                    dlpb = dlp.astype(BF16)
                    dlcb = dlc.astype(BF16)
                    dq_ref[:, hs] = ((_dot(dlpb, kp) + _dot(dlcb, kc)) * scale).astype(BF16)
                    dkp = dkp + _dot_tn(dlpb, qh)
                    dkc = dkc + _dot_tn(dlcb, qh)
                    dvp = dvp + _dot_tn(pp.astype(BF16), doh)
                    dvc = dvc + _dot_tn(pc.astype(BF16), doh)
                dk_ref[:, ks] = (ck[:, ks] + dkp * scale).astype(BF16)
                dv_ref[:, ks] = (cv[:, ks] + dvp).astype(BF16)
                ck[:, ks] = dkc * scale
                cv[:, ks] = dvc
            dsink_ref[...] += dsink

    sds = jax.ShapeDtypeStruct
    qspec = pl.BlockSpec((BLK, ATTN_Q_DIM), lambda b, n: (cur(b, n), 0))
    cspec = pl.BlockSpec((BLK, ATTN_KV_DIM), lambda b, n: (cur(b, n), 0))
    pspec = pl.BlockSpec((BLK, ATTN_KV_DIM), lambda b, n: (prev(b, n), 0))
    late = pl.BlockSpec((BLK, ATTN_KV_DIM), lambda b, n: (b * nb + jnp.maximum(n - 1, 0), 0))
    return pl.pallas_call(
        body, grid=(batch, nb + 1),
        in_specs=[qspec, cspec, pspec, cspec, pspec, qspec, qspec,
                  pl.BlockSpec((BLK, ATTN_Q_HEADS), lambda b, n: (cur(b, n), 0)),
                  pl.BlockSpec((ATTN_Q_HEADS, BLK, 2 * BLK), lambda b, n: (0, 0, 0)),
                  pl.BlockSpec((1, ATTN_Q_HEADS), lambda b, n: (0, 0))],
        out_specs=[qspec, late, late,
                   pl.BlockSpec((ATTN_Q_HEADS, BLK, 2 * BLK), lambda b, n: (0, 0, 0)),
                   pl.BlockSpec((1, ATTN_Q_HEADS), lambda b, n: (0, 0))],
        out_shape=[sds((T, ATTN_Q_DIM), BF16), sds((T, ATTN_KV_DIM), BF16), sds((T, ATTN_KV_DIM), BF16),
                   sds((ATTN_Q_HEADS, BLK, 2 * BLK), F32), sds((1, ATTN_Q_HEADS), F32)],
        scratch_shapes=[pltpu.VMEM((BLK, ATTN_KV_DIM), F32), pltpu.VMEM((BLK, ATTN_KV_DIM), F32)],
        compiler_params=_cparams("arbitrary", "arbitrary"), name=name)(q, k, k, v, v, o, do, lse, bias, sinks)


def _conv_bwd(dxc, pre, xp_ref, w_ref, carry_ref, acc_ref, dp_ref, g, last):
    Q = SSM_CHUNK
    sg = _sigmoid(pre)
    dpre = dxc * (sg * (1.0 + pre * (1.0 - sg)))

    @pl.when(last)
    def _():
        carry_ref[g] = jnp.zeros(carry_ref.shape[1:], F32)

    dp_ref[0:Q, :] = dpre
    dp_ref[Q:Q + HALO, :] = carry_ref[g]
    carry_ref[g] = dpre[0:HALO, :]
    rows = [jnp.sum(dpre * xp_ref[pl.ds(HALO - 3 + k, Q), :], axis=0, keepdims=True) for k in range(SSM_CONV)]
    rows.append(jnp.sum(dpre, axis=0, keepdims=True))
    rows.append(jnp.zeros((HALO - SSM_CONV - 1, dpre.shape[1]), F32))
    acc_ref[g] += jnp.concatenate(rows, axis=0)
    dx = w_ref[3:4, :] * dpre
    for k in range(SSM_CONV - 1):
        dx = dx + w_ref[k:k + 1, :] * dp_ref[pl.ds(3 - k, Q), :]
    return dx


def ssd_bwd(dys, y, xbc, z, dt_raw, dt_rawT, states, conv_w, conv_b, dt_bias, a_log, d_skip, norm_g, *, batch, name):
    T = xbc.shape[0]
    Q, GW, N, P, HPG, G, H = SSM_CHUNK, SSM_GW, SSM_STATE, SSM_HEAD_DIM, SSM_HPG, SSM_GROUPS, SSM_HEADS
    nc = T // batch // Q
    sp = _ssd_specs(nc)(lambda c: nc - 1 - c)

    def body(xs_ref, bm_ref, cm_ref, xs_halo, bm_halo, cm_halo, z_ref, y_ref, dys_ref, dt_ref, dtT_ref, st_ref,
             w_xs, w_bm, w_cm, b_xs, b_bm, b_cm, dtb_ref, dtbT_ref, alog_ref, alogT_ref, dskip_ref, ng_ref,
             dz_ref, dxs_ref, dbm_ref, dcm_ref, ddt_ref, acc_xs, acc_bm, acc_cm, acc_head,
             dstate, acsT_s, dacsT_s, yoff_s, dxw_s, dx_s, xp_xs, xp_bm, xp_cm, dp_xs, dp_bm, dp_cm,
             cy_xs, cy_bm, cy_cm):
        b = pl.program_id(0)
        cr = pl.program_id(1)
        g = pl.program_id(2)
        c = nc - 1 - cr
        first = c == 0
        last = cr == 0

        @pl.when(jnp.logical_and(jnp.logical_and(b == 0, cr == 0), g == 0))
        def _():
            acc_xs[...] = jnp.zeros_like(acc_xs)
            acc_bm[...] = jnp.zeros_like(acc_bm)
            acc_cm[...] = jnp.zeros_like(acc_cm)
            acc_head[...] = jnp.zeros_like(acc_head)

        cc = _ssd_chunk_common(first, g, xs_ref, bm_ref, cm_ref, xs_halo, bm_halo, cm_halo, w_xs, w_bm, w_cm,
                               b_xs, b_bm, b_cm, dt_ref, dtT_ref, dtb_ref, dtbT_ref, alog_ref, alogT_ref,
                               xp_xs, xp_bm, xp_cm)
        xs, acs, dt, a = cc["xs"], cc["acs"], cc["dt"], cc["a"]
        acsT_s[...] = cc["acsT"]
        dacsT_s[...] = jnp.zeros_like(dacsT_s)
        e64 = _head_expand(g, P)
        e128 = _head_expand(g, Q)
        acs_x = _dot_hi(acs, e64)
        acs_b = _dot_hi(acs, e128)
        dt_x = _dot_hi(dt, e64)
        x = xs * dt_x
        w_x = jnp.exp(acs_x[Q - 1:Q, :] - acs_x)
        ex = jnp.exp(acs_x)

        yv = y_ref[...]
        zz = z_ref[...]
        sz = _sigmoid(zz)
        silu_z = zz * sz
        yg = yv * silu_z
        rr = lax.rsqrt(jnp.mean(yg * yg, axis=-1, keepdims=True) + RMS_EPS)
        dys_v = dys_ref[...]
        d_ng = jnp.sum(dys_v * yg * rr, axis=0, keepdims=True)
        t = dys_v * ng_ref[...]
        dyg = rr * t - yg * (rr * rr * rr * jnp.mean(t * yg, axis=-1, keepdims=True))
        dy = dyg * silu_z
        dz_ref[...] = (dyg * yv * (sz * (1.0 + zz * (1.0 - sz)))).astype(BF16)

        dexp = _dot_hi(jnp.broadcast_to(dskip_ref[...], (8, H)), e64)[0:1, :]
        d_dskip = _dot_nt(jnp.broadcast_to(jnp.sum(dy * xs, axis=0, keepdims=True), (8, GW)), e64)[0:1, :]

        dyb = dy.astype(BF16)
        xb = x.astype(BF16)
        xwb = (x * w_x).astype(BF16)
        bb = cc["bm"].astype(BF16)
        cb = cc["cm"].astype(BF16)
        s = _dot_nt(cb, bb)
        causal = _iota((Q, Q), 0) >= _iota((Q, Q), 1)
        lane_h = _iota((1, H), 1)
        ds_acc = jnp.zeros((Q, Q), F32)
        d_c = jnp.zeros((Q, N), F32)
        d_b = jnp.zeros((Q, N), F32)
        dacs = jnp.zeros((Q, H), F32)
        last_terms = jnp.zeros((1, H), F32)
        for r in range(HPG):
            hd = HPG * g + r
            cols = slice(P * r, P * (r + 1))

            @pl.when(last)
            def _():
                dstate[hd] = jnp.zeros((P, N), F32)

            seg = acs_b[:, Q * r:Q * (r + 1)] - acsT_s[pl.ds(hd, 1), :]
            l = jnp.exp(jnp.where(causal, seg, -1e30))
            m = s * l
            mb = m.astype(BF16)
            dyh = dyb[:, cols]
            hp = st_ref[0, r]
            hpb = hp.astype(BF16)
            dh = dstate[hd]
            dhb = dh.astype(BF16)
            yoff_s[:, cols] = _dot_nt(cb, hpb) * ex[:, cols]
            dye = (dy[:, cols] * ex[:, cols]).astype(BF16)
            d_c = d_c + _dot(dye, hpb)
            dhp_off = _dot_tn(dye, cb)
            dm = _dot_nt(dyh, xb[:, cols])
            dx_s[:, cols] = _dot_tn(mb, dyh)
            gmat = dm * m
            onehot = (lane_h == hd).astype(F32)
            dacs = dacs + jnp.sum(gmat, axis=-1, keepdims=True) * onehot
            dacsT_s[pl.ds(hd, 1), :] = -jnp.sum(gmat, axis=0, keepdims=True)
            ds_acc = ds_acc + dm * l
            dxw_s[:, cols] = _dot_nt(bb, dhb)
            d_b = d_b + _dot(xwb[:, cols], dhb)
            decay = jnp.exp(acsT_s[pl.ds(hd, 1), pl.ds(Q - 1, 1)])
            ddecay = jnp.sum(jnp.sum(dh * hp, axis=-1, keepdims=True), axis=0, keepdims=True)
            last_terms = last_terms + (ddecay * decay) * onehot
            dstate[hd] = dh * decay + dhp_off
        dsb = ds_acc.astype(BF16)
        d_c = d_c + _dot(dsb, bb)
        d_b = d_b + _dot_tn(dsb, cb)
        dxw = dxw_s[...]
        dx_full = dx_s[...] + dxw * w_x
        tw = _dot_nt(dxw * x * w_x, e64)
        dacs = dacs + _dot_nt(dy * yoff_s[...], e64) - tw
        last_terms = last_terms + jnp.sum(tw, axis=0, keepdims=True)
        eye = (_iota((Q, Q), 0) == _iota((Q, Q), 1)).astype(F32)
        dacs = dacs + lax.dot_general(eye, dacsT_s[...], (((1,), (1,)), ((), ())), preferred_element_type=F32,
                                      precision=lax.Precision.HIGHEST)
        dacs = dacs + jnp.where(_iota((Q, 1), 0) == Q - 1, 1.0, 0.0) * last_terms
        d_dta = _dot_hi(cc["triT"], dacs)
        ddt = d_dta * a + _dot_nt(dx_full * xs, e64)
        d_alog = jnp.sum(d_dta * dt, axis=0, keepdims=True) * a
        ddt_raw = ddt * _sigmoid(cc["dtr"])
        d_dtb = jnp.sum(ddt_raw, axis=0, keepdims=True)

        @pl.when(g == 0)
        def _():
            ddt_ref[...] = ddt_raw

        @pl.when(g > 0)
        def _():
            ddt_ref[...] += ddt_raw

        acc_head[...] += jnp.concatenate([d_dtb, d_alog, d_dskip, jnp.zeros((5, H), F32)], axis=0)
        dxs = dexp * dy + dx_full * dt_x
        dxs_ref[...] = _conv_bwd(dxs, cc["pre_xs"], xp_xs, w_xs, cy_xs, acc_xs, dp_xs, g, last).astype(BF16)
        dbm_ref[...] = _conv_bwd(d_b, cc["pre_bm"], xp_bm, w_bm, cy_bm, acc_bm, dp_bm, g, last).astype(BF16)
        dcm_ref[...] = _conv_bwd(d_c, cc["pre_cm"], xp_cm, w_cm, cy_cm, acc_cm, dp_cm, g, last).astype(BF16)
        acc_xs[g, pl.ds(SSM_CONV + 1, 1), :] += d_ng

    col = lambda v: v.reshape(H, 1)
    sds = jax.ShapeDtypeStruct
    row = lambda b, c, g: b * nc + (nc - 1 - c)
    full = lambda shape: pl.BlockSpec(shape, lambda b, c, g: (0,) * len(shape))
    return pl.pallas_call(
        body, grid=(batch, nc, G),
        in_specs=[sp["xs"], sp["bm"], sp["cm"], sp["xs_halo"], sp["bm_halo"], sp["cm_halo"], sp["grp"], sp["grp"],
                  sp["grp"], sp["dt"], sp["dtT"], sp["state"],
                  sp["w_xs"], sp["w_bm"], sp["w_cm"], sp["b_xs"], sp["b_bm"], sp["b_cm"],
                  sp["row32"], sp["col32"], sp["row32"], sp["col32"], sp["row32"], sp["vec_g"]],
        out_specs=[sp["grp"], sp["grp"],
                   pl.BlockSpec((Q, N), lambda b, c, g: (row(b, c, g), g)),
                   pl.BlockSpec((Q, N), lambda b, c, g: (row(b, c, g), g)),
                   sp["dt"], full((G, HALO, GW)), full((G, HALO, N)), full((G, HALO, N)), full((8, H))],
        out_shape=[sds((T, SSM_D_INNER), BF16), sds((T, SSM_D_INNER), BF16), sds((T, G * N), BF16),
                   sds((T, G * N), BF16), sds((T, H), F32),
                   sds((G, HALO, GW), F32), sds((G, HALO, N), F32), sds((G, HALO, N), F32), sds((8, H), F32)],
        scratch_shapes=[pltpu.VMEM((H, P, N), F32), pltpu.VMEM((H, Q), F32), pltpu.VMEM((H, Q), F32),
                        pltpu.VMEM((Q, GW), F32), pltpu.VMEM((Q, GW), F32), pltpu.VMEM((Q, GW), F32),
                        pltpu.VMEM((HALO + Q, GW), F32), pltpu.VMEM((HALO + Q, N), F32), pltpu.VMEM((HALO + Q, N), F32),
                        pltpu.VMEM((Q + HALO, GW), F32), pltpu.VMEM((Q + HALO, N), F32), pltpu.VMEM((Q + HALO, N), F32),
                        pltpu.VMEM((G, HALO, GW), F32), pltpu.VMEM((G, HALO, N), F32), pltpu.VMEM((G, HALO, N), F32)],
        compiler_params=_cparams("arbitrary", "arbitrary", "arbitrary"), name=name,
    )(xbc, xbc, xbc, xbc, xbc, xbc, z, y, dys, dt_raw, dt_rawT, states, conv_w, conv_w, conv_w, conv_b, conv_b, conv_b,
      dt_bias, col(dt_bias), a_log, col(a_log), d_skip, norm_g)


def mix_in_bwd(dh, h, g, dgates, dz, dxs, dbm, dcm, ddt, dq, dk, dv, w_gz, w_xbc, w_dt, w_qkv, *, name, tm=512):
    T, D = h.shape
    nt = T // tm
    GN = SSM_GROUPS * SSM_STATE

    def body(dh_ref, h_ref, g_ref, dgates_ref, dz_ref, dxs_ref, dbm_ref, dcm_ref, ddt_ref, dq_ref, dk_ref, dv_ref,
             wgz_ref, wxbc_ref, wdt_ref, wqkv_ref, dhin_ref, dg_ref):
        @pl.when(pl.program_id(0) == 0)
        def _():
            dg_ref[...] = jnp.zeros_like(dg_ref)

        du = _dot_nt(dgates_ref[...], wgz_ref[:, 0:2048])
        du = du + _dot_nt(dz_ref[...], wgz_ref[:, 2048:4096])
        du = du + _dot_nt(dxs_ref[...], wxbc_ref[:, 0:SSM_D_INNER])
        du = du + _dot_nt(dbm_ref[...], wxbc_ref[:, SSM_D_INNER:SSM_D_INNER + GN])
        du = du + _dot_nt(dcm_ref[...], wxbc_ref[:, SSM_D_INNER + GN:])
        du = du + _dot_nt(ddt_ref[...].astype(BF16), wdt_ref[...])
        du = du + _dot_nt(dq_ref[...], wqkv_ref[:, 0:ATTN_Q_DIM])
        du = du + _dot_nt(dk_ref[...], wqkv_ref[:, ATTN_Q_DIM:ATTN_Q_DIM + ATTN_KV_DIM])
        du = du + _dot_nt(dv_ref[...], wqkv_ref[:, ATTN_Q_DIM + ATTN_KV_DIM:])
        hh = h_ref[...]
        r = lax.rsqrt(jnp.mean(hh * hh, axis=-1, keepdims=True) + RMS_EPS)
        dg_ref[...] += jnp.sum(du * hh * r, axis=0, keepdims=True)
        t = du * g_ref[...]
        dhin_ref[...] = dh_ref[...] + r * t - hh * (r * r * r * jnp.mean(t * hh, axis=-1, keepdims=True))

    sds = jax.ShapeDtypeStruct
    return pl.pallas_call(
        body, grid=(nt,),
        in_specs=[_rows(tm, D), _rows(tm, D), _resident((1, D)), _rows(tm, 2048), _rows(tm, 2048), _rows(tm, SSM_D_INNER),
                  _rows(tm, GN), _rows(tm, GN), _rows(tm, SSM_HEADS), _rows(tm, ATTN_Q_DIM), _rows(tm, ATTN_KV_DIM),
                  _rows(tm, ATTN_KV_DIM), _resident(w_gz.shape), _resident(w_xbc.shape), _resident(w_dt.shape),
                  _resident(w_qkv.shape)],
        out_specs=[_rows(tm, D), pl.BlockSpec((1, D), lambda i: (0, 0))],
        out_shape=[sds((T, D), F32), sds((1, D), F32)],
        compiler_params=_cparams("arbitrary"), name=name,
    )(dh, h, g, dgates, dz, dxs, dbm, dcm, ddt, dq, dk, dv, w_gz, w_xbc, w_dt, w_qkv)


MESH = pl.DeviceIdType.MESH
ANY = pl.BlockSpec(memory_space=pl.ANY)
ROW_ALIGN = 16


def _me():
    return lax.axis_index("x"), lax.axis_index("y"), lax.axis_index("c")


def _other_chips(x, y):
    return [(1 - x, y), (x, 1 - y), (1 - x, 1 - y)]


def _remote(src, dst, send_sem, recv_sem, to):
    return pltpu.make_async_remote_copy(src_ref=src, dst_ref=dst, send_sem=send_sem, recv_sem=recv_sem,
                                        device_id=to, device_id_type=MESH)


def _half(c, rows):
    return pl.ds(pl.multiple_of(c * (rows // 2), ROW_ALIGN), rows // 2)


def ag_weights(bufs, *, name):
    n = len(bufs)

    def body(*refs):
        outs = refs[n:2 * n]
        ici_send, ici_recv, d2d_send, d2d_recv = refs[2 * n:]
        x, y, c = _me()
        s_me = 2 * x + y
        sib = (x, y, 1 - c)
        chips = _other_chips(x, y)
        sent = []
        for i in range(n):
            mine = outs[i].at[s_me, _half(c, outs[i].shape[1])]
            for j, chip in enumerate(chips):
                sent.append(_remote(mine, mine, ici_send.at[i, j], ici_recv.at[i, j], (*chip, c)))
                sent[-1].start()
        for i in range(n):
            for j, chip in enumerate(chips):
                landed = outs[i].at[2 * chip[0] + chip[1], _half(c, outs[i].shape[1])]
                _remote(landed, landed, ici_send.at[i, j], ici_recv.at[i, j], (*chip, c)).wait_recv()
                sent.append(_remote(landed, landed, d2d_send.at[i, j], d2d_recv.at[i, j], sib))
                sent[-1].start()
        for i in range(n):
            for j, chip in enumerate(chips):
                lands = outs[i].at[2 * chip[0] + chip[1], _half(1 - c, outs[i].shape[1])]
                _remote(lands, lands, d2d_send.at[i, j], d2d_recv.at[i, j], sib).wait_recv()
        for cp in sent:
            cp.wait_send()

    return pl.pallas_call(
        body, in_specs=[ANY] * n, out_specs=[ANY] * n,
        out_shape=[jax.ShapeDtypeStruct(b.shape, b.dtype) for b in bufs],
        input_output_aliases={i: i for i in range(n)},
        scratch_shapes=[pltpu.SemaphoreType.DMA((n, 3)), pltpu.SemaphoreType.DMA((n, 3)),
                        pltpu.SemaphoreType.DMA((n, 3)), pltpu.SemaphoreType.DMA((n, 3))],
        name=name)(*bufs)


def rs_pair(grads, *, name):
    n = len(grads)

    def body(*refs):
        ins, outs = refs[:n], refs[n:2 * n]
        send, recv = refs[2 * n:]
        x, y, c = _me()
        sib = (x, y, 1 - c)
        sent = []
        for i in range(n):
            rows = ins[i].shape[1]
            sent.append(_remote(ins[i].at[:, _half(1 - c, rows), :], outs[i], send.at[i], recv.at[i], sib))
            sent[-1].start()
        for cp in sent:
            cp.wait()

    return pl.pallas_call(
        body, in_specs=[ANY] * n, out_specs=[ANY] * n,
        out_shape=[jax.ShapeDtypeStruct((N_SHARD, g.shape[1] // 2, g.shape[2]), g.dtype) for g in grads],
        scratch_shapes=[pltpu.SemaphoreType.DMA((n,)), pltpu.SemaphoreType.DMA((n,))], name=name)(*grads)


def rs_add(grad, part, c, *, rt, name):
    _, rows, cols = grad.shape
    r2 = rows // 2
    nrb = r2 // rt

    def body(c_ref, g_ref, p_ref, o_ref):
        o_ref[...] = (g_ref[...] + p_ref[...]).astype(BF16)

    return pl.pallas_call(
        body,
        grid_spec=pltpu.PrefetchScalarGridSpec(
            num_scalar_prefetch=1, grid=(N_SHARD, nrb),
            in_specs=[pl.BlockSpec((1, rt, cols), lambda k, i, c_ref: (k, c_ref[1] * nrb + i, 0)),
                      pl.BlockSpec((1, rt, cols), lambda k, i, c_ref: (k, i, 0))],
            out_specs=pl.BlockSpec((1, rt, cols), lambda k, i, c_ref: (k, i, 0))),
        out_shape=jax.ShapeDtypeStruct((N_SHARD, r2, cols), BF16),
        compiler_params=_cparams("parallel", "parallel"), name=name)(c, grad, part)


def rs_chips(sums, *, name):
    n = len(sums)

    def body(*refs):
        ins, outs = refs[:n], refs[n:2 * n]
        send, recv = refs[2 * n:]
        x, y, c = _me()
        s_me = 2 * x + y
        sent = []
        for i in range(n):
            for j, chip in enumerate(_other_chips(x, y)):
                sent.append(_remote(ins[i].at[2 * chip[0] + chip[1]], outs[i].at[s_me], send.at[i, j], recv.at[i, j],
                                    (*chip, c)))
                sent[-1].start()
        for cp in sent:
            cp.wait()

    return pl.pallas_call(
        body, in_specs=[ANY] * n, out_specs=[ANY] * n,
        out_shape=[jax.ShapeDtypeStruct(s.shape, s.dtype) for s in sums],
        scratch_shapes=[pltpu.SemaphoreType.DMA((n, 3)), pltpu.SemaphoreType.DMA((n, 3))],
        name=name)(*sums)


def rs_total(parts, own, where, *, rt, name):
    _, r2, cols = parts.shape
    nrb = r2 // rt

    def body(w_ref, p0, p1, p2, p3, own_ref, o_ref):
        s_me = w_ref[0]
        acc = None
        for k, p in enumerate((p0, p1, p2, p3)):
            term = jnp.where(s_me == k, own_ref[0], p[0]).astype(F32)
            acc = term if acc is None else acc + term
        o_ref[...] = acc

    def slot(k):
        return pl.BlockSpec((1, rt, cols), lambda i, w: (jnp.where(w[0] == k, (k + 1) % N_SHARD, k), i, 0))

    return pl.pallas_call(
        body,
        grid_spec=pltpu.PrefetchScalarGridSpec(
            num_scalar_prefetch=1, grid=(nrb,),
            in_specs=[slot(0), slot(1), slot(2), slot(3), pl.BlockSpec((1, rt, cols), lambda i, w: (w[0], i, 0))],
            out_specs=pl.BlockSpec((rt, cols), lambda i, w: (w[1] * nrb + i, 0))),
        out_shape=jax.ShapeDtypeStruct((2 * r2, cols), F32),
        compiler_params=_cparams("parallel"), name=name)(where, parts, parts, parts, parts, own)


def rs_share(totals, *, name):
    n = len(totals)

    def body(*refs):
        outs = refs[n:2 * n]
        send, recv = refs[2 * n:]
        x, y, c = _me()
        sib = (x, y, 1 - c)
        sent = []
        for i in range(n):
            mine = outs[i].at[_half(c, outs[i].shape[0])]
            sent.append(_remote(mine, mine, send.at[i], recv.at[i], sib))
            sent[-1].start()
        for i in range(n):
            other = outs[i].at[_half(1 - c, outs[i].shape[0])]
            _remote(other, other, send.at[i], recv.at[i], sib).wait_recv()
        for cp in sent:
            cp.wait_send()

    return pl.pallas_call(
        body, in_specs=[ANY] * n, out_specs=[ANY] * n,
        out_shape=[jax.ShapeDtypeStruct(t.shape, t.dtype) for t in totals],
        input_output_aliases={i: i for i in range(n)},
        scratch_shapes=[pltpu.SemaphoreType.DMA((n,)), pltpu.SemaphoreType.DMA((n,))],
        name=name)(*totals)


def small_allreduce(buf, *, name):
    rows = buf.shape[0]

    def body(x_ref, o_ref, slots, send, recv):
        x, y, c = _me()
        me = 4 * x + 2 * y + c
        slots[me] = x_ref[...]
        sent = []
        for d in range(1, 8):
            peer = (1 - x if d & 4 else x, 1 - y if d & 2 else y, 1 - c if d & 1 else c)
            sent.append(_remote(x_ref, slots.at[me], send.at[d - 1], recv.at[d - 1], peer))
            sent[-1].start()
        for cp in sent:
            cp.wait()
        acc = slots[0]
        for k in range(1, 8):
            acc = acc + slots[k]
        o_ref[...] = acc

    return pl.pallas_call(
        body, out_shape=jax.ShapeDtypeStruct(buf.shape, F32),
        in_specs=[pl.BlockSpec(memory_space=pltpu.VMEM)], out_specs=pl.BlockSpec(memory_space=pltpu.VMEM),
        scratch_shapes=[pltpu.VMEM((8, rows, 128), F32), pltpu.SemaphoreType.DMA((7,)), pltpu.SemaphoreType.DMA((7,))],
        name=name)(buf)


def adamw(w, g, m, v, *, name, rt=None):
    rows, cols = w.shape
    rt = rows if rt is None else rt
    c1 = 1.0 - ADAM_B1 ** ADAM_STEP
    c2 = 1.0 - ADAM_B2 ** ADAM_STEP

    def body(w_ref, g_ref, m_ref, v_ref, d_ref, nm_ref, nv_ref):
        gg = g_ref[...]
        nm = ADAM_B1 * m_ref[...] + (1.0 - ADAM_B1) * gg
        nv = ADAM_B2 * v_ref[...] + (1.0 - ADAM_B2) * (gg * gg)
        nm_ref[...] = nm
        nv_ref[...] = nv
        d_ref[...] = -ADAM_LR * ((nm / c1) / (jnp.sqrt(nv / c2) + ADAM_EPS) + ADAM_WD * w_ref[...])

    spec = pl.BlockSpec((rt, cols), lambda i: (i, 0))
    return pl.pallas_call(
        body, grid=(rows // rt,), in_specs=[spec] * 4, out_specs=[spec] * 3,
        out_shape=[jax.ShapeDtypeStruct((rows, cols), F32)] * 3,
        compiler_params=_cparams("parallel"), name=name)(w, g, m, v)


WEIGHTS = ['ffn1_pre_g', 'ffn1_w_gate', 'ffn1_w_up', 'ffn1_w_down', 'ffn1_post_g', 'mix_pre_g', 'w_in', 'conv_w',
           'conv_b', 'dt_bias', 'a_log', 'd_skip', 'ssm_norm_g', 'w_ssm_proj', 'attn_sinks', 'rel_bias_table',
           'w_attn_proj', 'w_out', 'mix_post_g', 'ffn2_pre_g', 'ffn2_w_gate', 'ffn2_w_up', 'ffn2_w_down', 'ffn2_post_g']
BIG = ['ffn1_w_gate', 'ffn1_w_up', 'ffn1_w_down', 'w_in', 'w_ssm_proj', 'w_attn_proj', 'w_out',
       'ffn2_w_gate', 'ffn2_w_up', 'ffn2_w_down']
SMALL = [w for w in WEIGHTS if w not in BIG]


def _bucket_onehot():
    blk = ATTN_BLOCK
    dist = np.maximum(np.arange(blk)[:, None] + blk - np.arange(2 * blk)[None, :], 0)
    max_exact = REL_BUCKETS // 2
    d = np.maximum(dist, 1).astype(np.float32)
    large = max_exact + (np.log(d / np.float32(max_exact)) / np.float32(math.log(REL_MAX_DISTANCE / max_exact))
                         * np.float32(REL_BUCKETS - max_exact)).astype(np.int32)
    bucket = np.where(dist < max_exact, dist, np.minimum(large, REL_BUCKETS - 1)).reshape(-1)
    return jnp.asarray((bucket[None, :] == np.arange(REL_BUCKETS)[:, None]).astype(np.float32))


def _pack_rows(parts, mult=8):
    flat = jnp.concatenate([p.reshape(-1).astype(F32) for p in parts])
    rows = -(-flat.shape[0] // (128 * mult)) * mult
    return jnp.pad(flat, (0, rows * 128 - flat.shape[0])).reshape(rows, 128)


def _unpack_rows(buf, shapes):
    flat = buf.reshape(-1)
    out, at = [], 0
    for shp in shapes:
        size = int(np.prod(shp))
        out.append(flat[at:at + size].reshape(shp))
        at += size
    return out


def kernel(x, ffn1_pre_g, ffn1_w_gate, ffn1_w_up, ffn1_w_down, ffn1_post_g, mix_pre_g, w_in, conv_w, conv_b, dt_bias, a_log, d_skip, ssm_norm_g, w_ssm_proj, attn_sinks, rel_bias_table, w_attn_proj, w_out, mix_post_g, ffn2_pre_g, ffn2_w_gate, ffn2_w_up, ffn2_w_down, ffn2_post_g, loss_target, m_ffn1_pre_g, m_ffn1_w_gate, m_ffn1_w_up, m_ffn1_w_down, m_ffn1_post_g, m_mix_pre_g, m_w_in, m_conv_w, m_conv_b, m_dt_bias, m_a_log, m_d_skip, m_ssm_norm_g, m_w_ssm_proj, m_attn_sinks, m_rel_bias_table, m_w_attn_proj, m_w_out, m_mix_post_g, m_ffn2_pre_g, m_ffn2_w_gate, m_ffn2_w_up, m_ffn2_w_down, m_ffn2_post_g, v_ffn1_pre_g, v_ffn1_w_gate, v_ffn1_w_up, v_ffn1_w_down, v_ffn1_post_g, v_mix_pre_g, v_w_in, v_conv_w, v_conv_b, v_dt_bias, v_a_log, v_d_skip, v_ssm_norm_g, v_w_ssm_proj, v_attn_sinks, v_rel_bias_table, v_w_attn_proj, v_w_out, v_mix_post_g, v_ffn2_pre_g, v_ffn2_w_gate, v_ffn2_w_up, v_ffn2_w_down, v_ffn2_post_g):
    args = locals()
    w = {n: args[n] for n in WEIGHTS}
    m = {n: args["m_" + n] for n in WEIGHTS}
    v = {n: args["v_" + n] for n in WEIGHTS}
    batch, seq, D = x.shape
    T = batch * seq
    xi, yi, ci = _me()
    s_me = 2 * xi + yi
    x2 = x.reshape(T, D)
    tgt = loss_target.reshape(T, D)

    def own_slot(parts):
        p = jnp.concatenate([t[0] for t in parts], axis=0).astype(BF16)
        return lax.dynamic_update_slice(lax.empty((N_SHARD,) + p.shape, BF16), p[None], (s_me, 0, 0))

    g704, gdn, gmix, gin = ag_weights(
        [own_slot([ffn1_w_gate, ffn1_w_up, ffn2_w_gate, ffn2_w_up]), own_slot([ffn1_w_down, ffn2_w_down]),
         own_slot([w_ssm_proj, w_attn_proj, w_out]), own_slot([w_in])], name="ag_weights")
    ffn1_parts, ffn2_parts = (0, 1, 0), (2, 3, 1)
    w_in_full = gin.transpose(1, 0, 2).reshape(D, IN_COLS)
    w_gz = w_in_full[:, 0:4096]
    w_xbc = w_in_full[:, 4096:4096 + SSM_CONV_DIM]
    w_dt = w_in_full[:, 7168:7200]
    w_qkv = w_in_full[:, 7200:]
    cw_slot = lax.dynamic_update_slice(jnp.zeros((SSM_CONV, SSM_CONV_DIM), F32),
                                       conv_w[0] * (ci == 0).astype(F32), (0, s_me * (SSM_CONV_DIM // N_SHARD)))
    conv_w_full = small_allreduce(cw_slot.reshape(-1, 128), name="ag_conv_w").reshape(SSM_CONV, SSM_CONV_DIM)

    h1, n1, gate1, up1, f1 = ffn_fwd(x2, ffn1_pre_g, g704, gdn, ffn1_post_g, parts=ffn1_parts, name="ffn1_fwd")
    u, gates, z, xbc, dt_raw, dt_rawT, q, k, vv = mix_in_fwd(h1, mix_pre_g, w_gz, w_xbc, w_dt, w_dt.T, w_qkv,
                                                               name="mix_in_fwd")
    y, ys, states = ssd_fwd(xbc, z, dt_raw, dt_rawT, conv_w_full, conv_b, dt_bias, a_log, d_skip, ssm_norm_g,
                            batch=batch, name="ssd_fwd")
    onehot = _bucket_onehot()
    bias = attn_bias(rel_bias_table.T, onehot, name="attn_bias").reshape(ATTN_Q_HEADS, ATTN_BLOCK, 2 * ATTN_BLOCK)
    o, lse = attn_fwd(q, k, vv, bias, attn_sinks, batch=batch, name="attn_fwd")
    h2, y_ssm, y_attn, mix, merged = mix_out_fwd(ys, o, gates, h1, gmix, mix_post_g, name="mix_out_fwd")
    h3, n3, gate2, up2, f2, dy, loss_parts = ffn_fwd(h2, ffn2_pre_g, g704, gdn, ffn2_post_g, tgt, parts=ffn2_parts,
                                                     name="ffn2_fwd")

    dh2, df2, a2, dgate2, dup2, dg_ffn2_pre, dg_ffn2_post = ffn_bwd(dy, h2, f2, gate2, up2, ffn2_pre_g, ffn2_post_g,
                                                                    g704, gdn, parts=ffn2_parts, name="ffn2_bwd")
    dmix, dyssm, dyattn, dgates, dys, do, dg_mix_post = mix_out_bwd(dh2, mix, y_ssm, y_attn, gates, gmix, mix_post_g,
                                                                    name="mix_out_bwd")
    dq, dk, dv, dbias, dsinks = attn_bwd(q, k, vv, o, do, lse, bias, attn_sinks, batch=batch, name="attn_bwd")
    dtable = attn_bias_bwd(dbias.reshape(ATTN_Q_HEADS, -1), onehot, name="attn_bias_bwd").T
    dz, dxs, dbm, dcm, ddt, acc_xs, acc_bm, acc_cm, acc_head = ssd_bwd(
        dys, y, xbc, z, dt_raw, dt_rawT, states, conv_w_full, conv_b, dt_bias, a_log, d_skip, ssm_norm_g,
        batch=batch, name="ssd_bwd")
    dh1, dg_mix_pre = mix_in_bwd(dh2, h1, mix_pre_g, dgates, dz, dxs, dbm, dcm, ddt, dq, dk, dv, w_gz, w_xbc, w_dt, w_qkv,
                                 name="mix_in_bwd")
    dx, df1, a1, dgate1, dup1, dg_ffn1_pre, dg_ffn1_post = ffn_bwd(dh1, x2, f1, gate1, up1, ffn1_pre_g, ffn1_post_g,
                                                                   g704, gdn, parts=ffn1_parts, name="ffn1_bwd")

    d704 = lax.empty(g704.shape, F32)
    for part, (act, grad, nm) in enumerate([(n1, dgate1, "dw_gate1"), (n1, dup1, "dw_up1"),
                                            (n3, dgate2, "dw_gate2"), (n3, dup2, "dw_up2")]):
        d704 = mm_tn(act[None], grad, into=(d704, part), name=nm)
    ddn = mm_tn(a1, df1[None], into=(lax.empty(gdn.shape, F32), 0), name="dw_down1")
    ddn = mm_tn(a2, df2[None], into=(ddn, 1), name="dw_down2")
    dmx = mm_tn(ys[None], dyssm[None], a_cols=(N_SHARD, 512), into=(lax.empty(gmix.shape, F32), 0), name="dw_ssm")
    dmx = mm_tn(o[None], dyattn[None], a_cols=(N_SHARD, 256), into=(dmx, 2), name="dw_attn")
    dmx = mm_tn(merged[None], dmix[None], a_cols=(N_SHARD, 256), into=(dmx, 3), name="dw_out")
    ub = u[None]
    din = jnp.concatenate([
        mm_tn(ub, dgates[None], name="dw_in_gates", tn=1024)[0], mm_tn(ub, dz[None], name="dw_in_z", tn=1024)[0],
        mm_tn(ub, dxs[None], name="dw_in_xs", tn=1024)[0], mm_tn(ub, dbm[None], name="dw_in_b")[0],
        mm_tn(ub, dcm[None], name="dw_in_c")[0], mm_tn(ub, ddt[None], name="dw_in_dt")[0],
        mm_tn(ub, dq[None], name="dw_in_q")[0], mm_tn(ub, dk[None], name="dw_in_k")[0],
        mm_tn(ub, dv[None], name="dw_in_v")[0]], axis=1)
    din = din.reshape(D, N_SHARD, IN_COLS // N_SHARD).transpose(1, 0, 2)

    grads = [d704, ddn, dmx, din]
    tiles = [512, 352, 256, 256]
    where = jnp.stack([s_me, ci]).astype(jnp.int32)
    pair = rs_pair(grads, name="rs_pair")
    sums = [rs_add(g, p, where, rt=rt, name=f"rs_add{i}") for i, (g, p, rt) in enumerate(zip(grads, pair, tiles))]
    parts = rs_chips(sums, name="rs_chips")
    totals = [rs_total(p, s, where, rt=rt, name=f"rs_total{i}")
              for i, (p, s, rt) in enumerate(zip(parts, sums, tiles))]
    r704, rdn, rmx, rin = rs_share(totals, name="rs_share")
    FS = D_FF // N_SHARD
    gw = {
        'ffn1_w_gate': r704[0:D], 'ffn1_w_up': r704[D:2 * D], 'ffn2_w_gate': r704[2 * D:3 * D], 'ffn2_w_up': r704[3 * D:],
        'ffn1_w_down': rdn[0:FS], 'ffn2_w_down': rdn[FS:],
        'w_ssm_proj': rmx[0:512], 'w_attn_proj': rmx[512:768], 'w_out': rmx[768:1024], 'w_in': rin,
    }

    dconv_w = jnp.concatenate([acc[:, :SSM_CONV].transpose(1, 0, 2).reshape(SSM_CONV, -1)
                               for acc in (acc_xs, acc_bm, acc_cm)], axis=1)
    dconv_b = jnp.concatenate([acc[:, SSM_CONV].reshape(-1) for acc in (acc_xs, acc_bm, acc_cm)])
    small_local = {
        'ffn1_pre_g': dg_ffn1_pre, 'ffn1_post_g': dg_ffn1_post, 'mix_pre_g': dg_mix_pre, 'conv_w': dconv_w,
        'conv_b': dconv_b, 'dt_bias': acc_head[0], 'a_log': acc_head[1], 'd_skip': acc_head[2],
        'ssm_norm_g': acc_xs[:, SSM_CONV + 1].reshape(-1), 'attn_sinks': dsinks, 'rel_bias_table': dtable,
        'mix_post_g': dg_mix_post, 'ffn2_pre_g': dg_ffn2_pre, 'ffn2_post_g': dg_ffn2_post,
    }
    full_shapes = [(SSM_CONV, SSM_CONV_DIM) if n == 'conv_w' else w[n].shape for n in SMALL]
    packed = _pack_rows([small_local[n] for n in SMALL] + [jnp.sum(loss_parts[:, 0, 0])])
    total = small_allreduce(packed, name="allreduce_small")
    *small_g, loss = _unpack_rows(total, full_shapes + [()])
    for n, g in zip(SMALL, small_g):
        gw[n] = g
    gw['conv_w'] = lax.dynamic_slice(gw['conv_w'], (0, s_me * (SSM_CONV_DIM // N_SHARD)),
                                     (SSM_CONV, SSM_CONV_DIM // N_SHARD))[None]

    delta, new_m, new_v = {}, {}, {}
    for n in BIG:
        rows = w[n].shape[1]
        d_, m_, v_ = adamw(w[n][0], gw[n], m[n][0], v[n][0], name="adamw_" + n, rt=rows // 4)
        gw[n] = gw[n][None]
        delta[n], new_m[n], new_v[n] = d_[None], m_[None], v_[None]
    shapes = [w[n].shape for n in SMALL]
    outs = adamw(_pack_rows([w[n] for n in SMALL]), _pack_rows([gw[n] for n in SMALL]),
                 _pack_rows([m[n] for n in SMALL]), _pack_rows([v[n] for n in SMALL]), name="adamw_small")
    for res, buf in zip((delta, new_m, new_v), outs):
        for n, val in zip(SMALL, _unpack_rows(buf, shapes)):
            res[n] = val
    return (loss, dx.reshape(batch, seq, D), *[gw[n].reshape(w[n].shape) for n in WEIGHTS],
            *[delta[n] for n in WEIGHTS], *[new_m[n] for n in WEIGHTS], *[new_v[n] for n in WEIGHTS])
```

```python
import functools
import math

import jax
import jax.numpy as jnp
import numpy as np
from jax import lax
from jax.experimental import pallas as pl
from jax.experimental.pallas import tpu as pltpu

F32 = jnp.float32
BF16 = jnp.bfloat16

D_MODEL = 1024
D_FF = 2816
N_SHARD = 4
SSM_D_INNER = 2048
SSM_HEAD_DIM = 64
SSM_HEADS = 32
SSM_GROUPS = 4
SSM_HPG = SSM_HEADS // SSM_GROUPS
SSM_GW = SSM_D_INNER // SSM_GROUPS
SSM_STATE = 128
SSM_CONV = 4
SSM_CHUNK = 128
SSM_CONV_DIM = SSM_D_INNER + 2 * SSM_GROUPS * SSM_STATE
ATTN_Q_HEADS = 16
ATTN_KV_HEADS = 4
ATTN_REP = ATTN_Q_HEADS // ATTN_KV_HEADS
ATTN_HEAD_DIM = 64
ATTN_BLOCK = 128
ATTN_Q_DIM = 1024
ATTN_KV_DIM = 256
REL_BUCKETS = 32
REL_MAX_DISTANCE = 128
RMS_EPS = 1e-6
IN_COLS = 8736
ADAM_LR = 0.001
ADAM_B1 = 0.9
ADAM_B2 = 0.999
ADAM_EPS = 1e-08
ADAM_WD = 0.01
ADAM_STEP = 10
HALO = 8

VMEM_LIMIT = 56 * 1024 * 1024


def _cparams(*sem):
    return pltpu.CompilerParams(dimension_semantics=tuple(sem) if sem else None, vmem_limit_bytes=VMEM_LIMIT)


def _dot(a, b):
    return jnp.dot(a, b, preferred_element_type=F32)


def _dot_nt(a, b):
    return lax.dot_general(a, b, (((1,), (1,)), ((), ())), preferred_element_type=F32)


def _dot_tn(a, b):
    return lax.dot_general(a, b, (((0,), (0,)), ((), ())), preferred_element_type=F32)


def _dot_hi(a, b):
    return jnp.dot(a, b, preferred_element_type=F32, precision=lax.Precision.HIGHEST)


def _sigmoid(x):
    return 1.0 / (1.0 + jnp.exp(-x))


def _resident(shape, index=None):
    index = (0,) * len(shape) if index is None else tuple(index)
    return pl.BlockSpec(shape, lambda *_: index, pipeline_mode=pl.Buffered(1))


def _part(packed, rows, part):
    return _resident((N_SHARD, rows, packed.shape[2]), (0, part, 0))


def _rows(tm, width):
    return pl.BlockSpec((tm, width), lambda i: (i, 0))


def ffn_fwd(h, g_pre, w704, wdn, g_post, target=None, *, parts, name, tm=512):
    T, D = h.shape
    NS, FS = N_SHARD, w704.shape[2]
    with_loss = target is not None
    nt = T // tm

    def body(*refs):
        if with_loss:
            (h_ref, gpre_ref, wg_ref, wu_ref, wd_ref, gpost_ref, tgt_ref,
             hout_ref, n_ref, gate_ref, up_ref, f_ref, dy_ref, loss_ref) = refs
        else:
            (h_ref, gpre_ref, wg_ref, wu_ref, wd_ref, gpost_ref,
             hout_ref, n_ref, gate_ref, up_ref, f_ref) = refs
        hh = h_ref[...]
        r = lax.rsqrt(jnp.mean(hh * hh, axis=-1, keepdims=True) + RMS_EPS)
        n = (hh * r * gpre_ref[...]).astype(BF16)
        n_ref[...] = n
        acc = jnp.zeros((tm, D), F32)
        for s in range(NS):
            gate = _dot(n, wg_ref[s])
            up = _dot(n, wu_ref[s])
            gate_ref[s] = gate.astype(BF16)
            up_ref[s] = up.astype(BF16)
            a = (gate * _sigmoid(gate) * up).astype(BF16)
            acc = acc + _dot(a, wd_ref[s])
        f_ref[...] = acc
        r2 = lax.rsqrt(jnp.mean(acc * acc, axis=-1, keepdims=True) + RMS_EPS)
        out = hh + 0.5 * (acc * r2 * gpost_ref[...])
        hout_ref[...] = out
        if with_loss:
            e = out - tgt_ref[...]
            dy_ref[...] = e * (1.0 / D)
            loss_ref[...] = jnp.full((1, 8, 128), 0.5 / D, F32) * jnp.sum(e * e)

    in_specs = [_rows(tm, D), _resident((1, D)), _part(w704, D, parts[0]), _part(w704, D, parts[1]),
                _part(wdn, FS, parts[2]), _resident((1, D))]
    args = [h, g_pre, w704, w704, wdn, g_post]
    out_shape = [jax.ShapeDtypeStruct((T, D), F32), jax.ShapeDtypeStruct((T, D), BF16),
                 jax.ShapeDtypeStruct((NS, T, FS), BF16), jax.ShapeDtypeStruct((NS, T, FS), BF16),
                 jax.ShapeDtypeStruct((T, D), F32)]
    seg = pl.BlockSpec((NS, tm, FS), lambda i: (0, i, 0))
    out_specs = [_rows(tm, D), _rows(tm, D), seg, seg, _rows(tm, D)]
    if with_loss:
        in_specs.append(_rows(tm, D))
        args.append(target)
        out_shape += [jax.ShapeDtypeStruct((T, D), F32), jax.ShapeDtypeStruct((nt, 8, 128), F32)]
        out_specs += [_rows(tm, D), pl.BlockSpec((1, 8, 128), lambda i: (i, 0, 0))]
    return pl.pallas_call(body, grid=(nt,), in_specs=in_specs, out_specs=out_specs, out_shape=out_shape,
                          compiler_params=_cparams("parallel"), name=name)(*args)


def ffn_bwd(dout, h, f, gate, up, g_pre, g_post, w704, wdn, *, parts, name, tm=256):
    T, D = h.shape
    NS, FS = N_SHARD, w704.shape[2]
    nt = T // tm

    def body(dout_ref, h_ref, f_ref, gate_ref, up_ref, gpre_ref, gpost_ref, wg_ref, wu_ref, wd_ref,
             dh_ref, df_ref, a_ref, dgate_ref, dup_ref, dgpre_ref, dgpost_ref):
        @pl.when(pl.program_id(0) == 0)
        def _():
            dgpre_ref[...] = jnp.zeros_like(dgpre_ref)
            dgpost_ref[...] = jnp.zeros_like(dgpost_ref)

        do = dout_ref[...]
        ff = f_ref[...]
        d_fn = 0.5 * do
        r2 = lax.rsqrt(jnp.mean(ff * ff, axis=-1, keepdims=True) + RMS_EPS)
        dgpost_ref[...] += jnp.sum(d_fn * ff * r2, axis=0, keepdims=True)
        t = d_fn * gpost_ref[...]
        df = r2 * t - ff * (r2 * r2 * r2 * jnp.mean(t * ff, axis=-1, keepdims=True))
        dfb = df.astype(BF16)
        df_ref[...] = dfb
        dn = jnp.zeros((tm, D), F32)
        for s in range(NS):
            da = _dot_nt(dfb, wd_ref[s])
            g = gate_ref[s].astype(F32)
            u = up_ref[s].astype(F32)
            sg = _sigmoid(g)
            silu = g * sg
            a_ref[s] = (silu * u).astype(BF16)
            dgt = (da * u * (sg * (1.0 + g * (1.0 - sg)))).astype(BF16)
            dupv = (da * silu).astype(BF16)
            dgate_ref[s] = dgt
            dup_ref[s] = dupv
            dn = dn + _dot_nt(dgt, wg_ref[s]) + _dot_nt(dupv, wu_ref[s])
        hh = h_ref[...]
        r1 = lax.rsqrt(jnp.mean(hh * hh, axis=-1, keepdims=True) + RMS_EPS)
        dgpre_ref[...] += jnp.sum(dn * hh * r1, axis=0, keepdims=True)
        t = dn * gpre_ref[...]
        dh_ref[...] = do + r1 * t - hh * (r1 * r1 * r1 * jnp.mean(t * hh, axis=-1, keepdims=True))

    seg = pl.BlockSpec((NS, tm, FS), lambda i: (0, i, 0))
    acc = pl.BlockSpec((1, D), lambda i: (0, 0))
    return pl.pallas_call(
        body, grid=(nt,),
        in_specs=[_rows(tm, D), _rows(tm, D), _rows(tm, D), seg, seg, _resident((1, D)), _resident((1, D)),
                  _part(w704, D, parts[0]), _part(w704, D, parts[1]), _part(wdn, FS, parts[2])],
        out_specs=[_rows(tm, D), _rows(tm, D), seg, seg, seg, acc, acc],
        out_shape=[jax.ShapeDtypeStruct((T, D), F32), jax.ShapeDtypeStruct((T, D), BF16),
                   jax.ShapeDtypeStruct((NS, T, FS), BF16), jax.ShapeDtypeStruct((NS, T, FS), BF16),
                   jax.ShapeDtypeStruct((NS, T, FS), BF16),
                   jax.ShapeDtypeStruct((1, D), F32), jax.ShapeDtypeStruct((1, D), F32)],
        compiler_params=_cparams("arbitrary"), name=name)(dout, h, f, gate, up, g_pre, g_post, w704, w704, wdn)


def mm_tn(a, g, *, name, tt=1024, tn=None, a_cols=None, into=None):
    Ba, T, _ = a.shape
    Bg, _, N = g.shape
    B, K = a_cols if a_cols else (max(Ba, Bg), a.shape[2])
    tn = N if tn is None else tn
    tt = min(tt, T)
    nsteps = T // tt

    def body(*refs):
        a_ref, g_ref, o_ref = refs[0], refs[1], refs[-1]

        @pl.when(pl.program_id(2) == 0)
        def _():
            o_ref[...] = jnp.zeros_like(o_ref)

        o_ref[0] += _dot_tn(a_ref[0], g_ref[0].astype(BF16))

    if a_cols:
        a_map = lambda b, j, t: (0, t, b)
    else:
        a_map = (lambda b, j, t: (b, t, 0)) if Ba > 1 else (lambda b, j, t: (0, t, 0))
    in_specs = [pl.BlockSpec((1, tt, K), a_map),
                pl.BlockSpec((1, tt, tn), (lambda b, j, t: (b, t, j)) if Bg > 1 else (lambda b, j, t: (0, t, j)))]
    args = [a, g]
    if into is None:
        out_shape, part, aliases = jax.ShapeDtypeStruct((B, K, N), F32), 0, {}
    else:
        buf, part = into
        out_shape, aliases = jax.ShapeDtypeStruct(buf.shape, F32), {2: 0}
        in_specs.append(ANY)
        args.append(buf)
    return pl.pallas_call(
        body, grid=(B, N // tn, nsteps), in_specs=in_specs,
        out_specs=pl.BlockSpec((1, K, tn), lambda b, j, t: (b, part, j)),
        out_shape=out_shape, input_output_aliases=aliases,
        compiler_params=_cparams("parallel", "parallel", "arbitrary"), name=name)(*args)


def mix_in_fwd(h, g, w_gz, w_xbc, w_dtT, w_qkv, *, name, tm=256):
    T, D = h.shape
    nt = T // tm
    CB = 1024

    def body(h_ref, g_ref, wgz_ref, wxbc_ref, wdtT_ref, wqkv_ref,
             u_ref, gates_ref, z_ref, xbc_ref, dtT_ref, q_ref, k_ref, v_ref):
        hh = h_ref[...]
        r = lax.rsqrt(jnp.mean(hh * hh, axis=-1, keepdims=True) + RMS_EPS)
        u = (hh * r * g_ref[...]).astype(BF16)
        u_ref[...] = u
        for cb in range(0, 2048, CB):
            gates_ref[:, cb:cb + CB] = _dot(u, wgz_ref[:, cb:cb + CB])
            z_ref[:, cb:cb + CB] = _dot(u, wgz_ref[:, 2048 + cb:2048 + cb + CB])
        for cb in range(0, SSM_CONV_DIM, CB):
            xbc_ref[:, cb:cb + CB] = _dot(u, wxbc_ref[:, cb:cb + CB])
        dtT_ref[...] = _dot_nt(wdtT_ref[...], u)
        q_ref[...] = _dot(u, wqkv_ref[:, 0:ATTN_Q_DIM]).astype(BF16)
        k_ref[...] = _dot(u, wqkv_ref[:, ATTN_Q_DIM:ATTN_Q_DIM + ATTN_KV_DIM]).astype(BF16)
        v_ref[...] = _dot(u, wqkv_ref[:, ATTN_Q_DIM + ATTN_KV_DIM:]).astype(BF16)

    sds = jax.ShapeDtypeStruct
    return pl.pallas_call(
        body, grid=(nt,),
        in_specs=[_rows(tm, D), _resident((1, D)), _resident(w_gz.shape), _resident(w_xbc.shape),
                  _resident(w_dtT.shape), _resident(w_qkv.shape)],
        out_specs=[_rows(tm, D), _rows(tm, 2048), _rows(tm, 2048), _rows(tm, SSM_CONV_DIM),
                   pl.BlockSpec((SSM_HEADS, tm), lambda i: (0, i)),
                   _rows(tm, ATTN_Q_DIM), _rows(tm, ATTN_KV_DIM), _rows(tm, ATTN_KV_DIM)],
        out_shape=[sds((T, D), BF16), sds((T, 2048), F32), sds((T, 2048), F32), sds((T, SSM_CONV_DIM), F32),
                   sds((SSM_HEADS, T), F32),
                   sds((T, ATTN_Q_DIM), BF16), sds((T, ATTN_KV_DIM), BF16), sds((T, ATTN_KV_DIM), BF16)],
        compiler_params=_cparams("parallel"), name=name)(h, g, w_gz, w_xbc, w_dtT, w_qkv)


def _softplus(x):
    return jnp.maximum(x, 0.0) + jnp.log(1.0 + jnp.exp(-jnp.abs(x)))


def _iota(shape, axis):
    return lax.broadcasted_iota(jnp.int32, shape, axis)


def _head_expand(g, per_head):
    shape = (SSM_HEADS, SSM_HPG * per_head)
    head = lax.shift_right_logical(_iota(shape, 1), int(math.log2(per_head)))
    return (_iota(shape, 0) == SSM_HPG * g + head).astype(F32)


def _conv_pre(x_ref, halo_ref, w_ref, b_ref, xp_ref, first):
    Q = SSM_CHUNK
    halo = jnp.where(first, 0.0, halo_ref[...])
    xp_ref[0:HALO, :] = halo
    xp_ref[HALO:HALO + Q, :] = x_ref[...]
    pre = b_ref[...] + w_ref[3:4, :] * xp_ref[HALO:HALO + Q, :]
    for k in range(SSM_CONV - 1):
        pre = pre + w_ref[k:k + 1, :] * xp_ref[pl.ds(HALO - 3 + k, Q), :]
    return pre


def _ssd_specs(nc):
    Q, GW, N = SSM_CHUNK, SSM_GW, SSM_STATE
    nb_xs = SSM_D_INNER // N
    nb_c = nb_xs + SSM_GROUPS

    def rb(cmap):
        def row(b, c, g):
            return b * nc + cmap(c)
        return row

    def specs(cmap):
        row = rb(cmap)
        hrow = lambda b, c, g: jnp.maximum(row(b, c, g) * (Q // HALO) - 1, 0)
        return dict(
            xs=pl.BlockSpec((Q, GW), lambda b, c, g: (row(b, c, g), g)),
            bm=pl.BlockSpec((Q, N), lambda b, c, g: (row(b, c, g), nb_xs + g)),
            cm=pl.BlockSpec((Q, N), lambda b, c, g: (row(b, c, g), nb_c + g)),
            xs_halo=pl.BlockSpec((HALO, GW), lambda b, c, g: (hrow(b, c, g), g)),
            bm_halo=pl.BlockSpec((HALO, N), lambda b, c, g: (hrow(b, c, g), nb_xs + g)),
            cm_halo=pl.BlockSpec((HALO, N), lambda b, c, g: (hrow(b, c, g), nb_c + g)),
            grp=pl.BlockSpec((Q, GW), lambda b, c, g: (row(b, c, g), g)),
            dt=pl.BlockSpec((Q, SSM_HEADS), lambda b, c, g: (row(b, c, g), 0)),
            dtT=pl.BlockSpec((SSM_HEADS, Q), lambda b, c, g: (0, row(b, c, g))),
            w_xs=pl.BlockSpec((SSM_CONV, GW), lambda b, c, g: (0, g)),
            w_bm=pl.BlockSpec((SSM_CONV, N), lambda b, c, g: (0, nb_xs + g)),
            w_cm=pl.BlockSpec((SSM_CONV, N), lambda b, c, g: (0, nb_c + g)),
            b_xs=pl.BlockSpec((1, GW), lambda b, c, g: (0, g)),
            b_bm=pl.BlockSpec((1, N), lambda b, c, g: (0, nb_xs + g)),
            b_cm=pl.BlockSpec((1, N), lambda b, c, g: (0, nb_c + g)),
            vec_g=pl.BlockSpec((1, GW), lambda b, c, g: (0, g)),
            row32=pl.BlockSpec((1, SSM_HEADS), lambda b, c, g: (0, 0)),
            col32=pl.BlockSpec((SSM_HEADS, 1), lambda b, c, g: (0, 0)),
            state=pl.BlockSpec((1, SSM_HPG, SSM_HEAD_DIM, N), lambda b, c, g: (row(b, c, g), g, 0, 0)),
            dtT_g=pl.BlockSpec((SSM_HPG, Q), lambda b, c, g: (g, row(b, c, g))),
            col_g=pl.BlockSpec((SSM_HPG, 1), lambda b, c, g: (g, 0)),
            pairs=pl.BlockSpec((1, SSM_HPG // 2, N, 2 * SSM_HEAD_DIM), lambda b, c, g: (row(b, c, g), g, 0, 0)),
        )
    return specs


def _ssd_chunk_common(first, g, xs_ref, bm_ref, cm_ref, xs_halo, bm_halo, cm_halo, w_xs, w_bm, w_cm, b_xs, b_bm, b_cm,
                      dt_ref, dtT_ref, dtb_ref, dtbT_ref, alog_ref, alogT_ref, xp_xs, xp_bm, xp_cm):
    Q = SSM_CHUNK
    pre_xs = _conv_pre(xs_ref, xs_halo, w_xs, b_xs, xp_xs, first)
    pre_bm = _conv_pre(bm_ref, bm_halo, w_bm, b_bm, xp_bm, first)
    pre_cm = _conv_pre(cm_ref, cm_halo, w_cm, b_cm, xp_cm, first)
    xs = pre_xs * _sigmoid(pre_xs)
    bm = pre_bm * _sigmoid(pre_bm)
    cm = pre_cm * _sigmoid(pre_cm)
    dtr = dt_ref[...] + dtb_ref[...]
    dtrT = dtT_ref[...] + dtbT_ref[...]
    dt = _softplus(dtr)
    dtT = _softplus(dtrT)
    a = -jnp.exp(alog_ref[...])
    aT = -jnp.exp(alogT_ref[...])
    tri = (_iota((Q, Q), 0) >= _iota((Q, Q), 1)).astype(F32)
    triT = (_iota((Q, Q), 0) <= _iota((Q, Q), 1)).astype(F32)
    acs = _dot_hi(tri, dt * a)
    acsT = _dot_hi(dtT * aT, triT)
    return dict(pre_xs=pre_xs, pre_bm=pre_bm, pre_cm=pre_cm, xs=xs, bm=bm, cm=cm, dtr=dtr, dt=dt, a=a,
                acs=acs, acsT=acsT, tri=tri, triT=triT)


def _v1_ssd_fwd(xbc, z, dt_raw, dt_rawT, conv_w, conv_b, dt_bias, a_log, d_skip, norm_g, *, batch, name):
    T = xbc.shape[0]
    Q, GW, N, P, HPG = SSM_CHUNK, SSM_GW, SSM_STATE, SSM_HEAD_DIM, SSM_HPG
    nc = T // batch // Q
    sp = _ssd_specs(nc)(lambda c: c)

    def body(xs_ref, bm_ref, cm_ref, xs_halo, bm_halo, cm_halo, z_ref, dt_ref, dtT_ref,
             w_xs, w_bm, w_cm, b_xs, b_bm, b_cm, dtb_ref, dtbT_ref, alog_ref, alogT_ref, dskip_ref, ng_ref,
             y_ref, ys_ref, st_ref, state, acsT_s, y_s, xp_xs, xp_bm, xp_cm):
        c = pl.program_id(1)
        g = pl.program_id(2)
        first = c == 0
        cc = _ssd_chunk_common(first, g, xs_ref, bm_ref, cm_ref, xs_halo, bm_halo, cm_halo, w_xs, w_bm, w_cm,
                               b_xs, b_bm, b_cm, dt_ref, dtT_ref, dtb_ref, dtbT_ref, alog_ref, alogT_ref,
                               xp_xs, xp_bm, xp_cm)
        xs, acs = cc["xs"], cc["acs"]
        acsT_s[...] = cc["acsT"]
        e64 = _head_expand(g, P)
        e128 = _head_expand(g, Q)
        acs_x = _dot_hi(acs, e64)
        acs_b = _dot_hi(acs, e128)
        x = xs * _dot_hi(cc["dt"], e64)
        last_x = acs_x[Q - 1:Q, :]
        xw = (x * jnp.exp(last_x - acs_x)).astype(BF16)
        ex = jnp.exp(acs_x)
        xb = x.astype(BF16)
        bb = cc["bm"].astype(BF16)
        cb = cc["cm"].astype(BF16)
        s = _dot_nt(cb, bb)
        causal = _iota((Q, Q), 0) >= _iota((Q, Q), 1)
        for r in range(HPG):
            hd = HPG * g + r

            @pl.when(first)
            def _():
                state[hd] = jnp.zeros((P, N), F32)

            seg = acs_b[:, Q * r:Q * (r + 1)] - acsT_s[pl.ds(hd, 1), :]
            m = (s * jnp.exp(jnp.where(causal, seg, -1e30))).astype(BF16)
            hp = state[hd]
            st_ref[0, r] = hp
            y_h = _dot(m, xb[:, P * r:P * (r + 1)]) + _dot_nt(cb, hp.astype(BF16)) * ex[:, P * r:P * (r + 1)]
            y_s[:, P * r:P * (r + 1)] = y_h
            decay = jnp.exp(acsT_s[pl.ds(hd, 1), pl.ds(Q - 1, 1)])
            state[hd] = hp * decay + _dot_tn(xw[:, P * r:P * (r + 1)], bb)
        dexp = _dot_hi(jnp.broadcast_to(dskip_ref[...], (8, SSM_HEADS)), e64)[0:1, :]
        y = y_s[...] + dexp * xs
        y_ref[...] = y
        zz = z_ref[...]
        yg = y * (zz * _sigmoid(zz))
        rr = lax.rsqrt(jnp.mean(yg * yg, axis=-1, keepdims=True) + RMS_EPS)
        ys_ref[...] = (yg * rr * ng_ref[...]).astype(BF16)

    col = lambda v: v.reshape(SSM_HEADS, 1)
    sds = jax.ShapeDtypeStruct
    return pl.pallas_call(
        body, grid=(batch, nc, SSM_GROUPS),
        in_specs=[sp["xs"], sp["bm"], sp["cm"], sp["xs_halo"], sp["bm_halo"], sp["cm_halo"], sp["grp"], sp["dt"],
                  sp["dtT"], sp["w_xs"], sp["w_bm"], sp["w_cm"], sp["b_xs"], sp["b_bm"], sp["b_cm"],
                  sp["row32"], sp["col32"], sp["row32"], sp["col32"], sp["row32"], sp["vec_g"]],
        out_specs=[sp["grp"], sp["grp"], sp["state"]],
        out_shape=[sds((T, SSM_D_INNER), F32), sds((T, SSM_D_INNER), BF16),
                   sds((T // Q, SSM_HEADS, P, N), F32)],
        scratch_shapes=[pltpu.VMEM((SSM_HEADS, P, N), F32), pltpu.VMEM((SSM_HEADS, Q), F32), pltpu.VMEM((Q, GW), F32),
                        pltpu.VMEM((HALO + Q, GW), F32), pltpu.VMEM((HALO + Q, N), F32), pltpu.VMEM((HALO + Q, N), F32)],
        compiler_params=_cparams("arbitrary", "arbitrary", "arbitrary"), name=name,
    )(xbc, xbc, xbc, xbc, xbc, xbc, z, dt_raw, dt_rawT, conv_w, conv_w, conv_w, conv_b, conv_b, conv_b,
      dt_bias, col(dt_bias), a_log, col(a_log), d_skip, norm_g)


def _attn_specs(nb):
    BLK = ATTN_BLOCK

    def specs(last):
        def cur(b, n):
            return b * nb + (n if last is None else jnp.minimum(n, nb - 1))

        def prev(b, n):
            return b * nb + jnp.maximum((n if last is None else jnp.minimum(n, nb - 1)) - 1, 0)
        return cur, prev
    return specs


def _attn_masks(n):
    shape = (ATTN_REP * ATTN_BLOCK, ATTN_BLOCK)
    ii = jnp.bitwise_and(_iota(shape, 0), ATTN_BLOCK - 1)
    jj = _iota(shape, 1)
    return jnp.logical_and(jj > ii, n > 0), jj <= ii


def _attn_group(kk, q_ref, bias_ref, sink_ref):
    BLK, HD = ATTN_BLOCK, ATTN_HEAD_DIM
    heads = range(ATTN_REP * kk, ATTN_REP * (kk + 1))
    qg = jnp.concatenate([q_ref[:, HD * hd:HD * (hd + 1)] for hd in heads], axis=0)
    bias_p = jnp.concatenate([bias_ref[hd, :, 0:BLK] for hd in heads], axis=0)
    bias_c = jnp.concatenate([bias_ref[hd, :, BLK:2 * BLK] for hd in heads], axis=0)
    sink = jnp.concatenate([jnp.broadcast_to(sink_ref[0:1, hd:hd + 1], (BLK, 1)) for hd in heads], axis=0)
    return qg, bias_p, bias_c, sink


def attn_bias(table_t, onehot, *, name):
    def body(t_ref, f_ref, o_ref):
        o_ref[...] = _dot_hi(t_ref[...], f_ref[...])
    return pl.pallas_call(body, out_shape=jax.ShapeDtypeStruct((ATTN_Q_HEADS, onehot.shape[1]), F32),
                          compiler_params=_cparams(), name=name)(table_t, onehot)


def attn_bias_bwd(dbias, onehot, *, name):
    def body(d_ref, f_ref, o_ref):
        o_ref[...] = lax.dot_general(d_ref[...], f_ref[...], (((1,), (1,)), ((), ())), preferred_element_type=F32,
                                     precision=lax.Precision.HIGHEST)
    return pl.pallas_call(body, out_shape=jax.ShapeDtypeStruct((ATTN_Q_HEADS, REL_BUCKETS), F32),
                          compiler_params=_cparams(), name=name)(dbias, onehot)


def attn_fwd(q, k, v, bias, sinks, *, batch, name):
    T = q.shape[0]
    BLK, HD = ATTN_BLOCK, ATTN_HEAD_DIM
    nb = T // batch // BLK
    cur, prev = _attn_specs(nb)(None)
    scale = HD ** -0.5

    def body(q_ref, kc_ref, kp_ref, vc_ref, vp_ref, bias_ref, sink_ref, o_ref, lse_ref):
        n = pl.program_id(1)
        m_prev, m_cur = _attn_masks(n)
        for kk in range(ATTN_KV_HEADS):
            ks = slice(HD * kk, HD * (kk + 1))
            kc, kp, vc, vp = kc_ref[:, ks], kp_ref[:, ks], vc_ref[:, ks], vp_ref[:, ks]
            qg, bias_p, bias_c, sink = _attn_group(kk, q_ref, bias_ref, sink_ref)
            lp = jnp.where(m_prev, _dot_nt(qg, kp) * scale + bias_p, -1e30)
            lc = jnp.where(m_cur, _dot_nt(qg, kc) * scale + bias_c, -1e30)
            mx = jnp.maximum(jnp.max(jnp.maximum(lp, lc), axis=-1, keepdims=True), sink)
            pp = jnp.exp(lp - mx)
            pc = jnp.exp(lc - mx)
            den = jnp.sum(pp + pc, axis=-1, keepdims=True) + jnp.exp(sink - mx)
            o = ((_dot(pp.astype(BF16), vp) + _dot(pc.astype(BF16), vc)) * (1.0 / den)).astype(BF16)
            lse = mx + jnp.log(den)
            for r in range(ATTN_REP):
                hd = ATTN_REP * kk + r
                o_ref[:, HD * hd:HD * (hd + 1)] = o[BLK * r:BLK * (r + 1)]
                lse_ref[:, hd:hd + 1] = lse[BLK * r:BLK * (r + 1)]

    sds = jax.ShapeDtypeStruct
    return pl.pallas_call(
        body, grid=(batch, nb),
        in_specs=[pl.BlockSpec((BLK, ATTN_Q_DIM), lambda b, n: (cur(b, n), 0)),
                  pl.BlockSpec((BLK, ATTN_KV_DIM), lambda b, n: (cur(b, n), 0)),
                  pl.BlockSpec((BLK, ATTN_KV_DIM), lambda b, n: (prev(b, n), 0)),
                  pl.BlockSpec((BLK, ATTN_KV_DIM), lambda b, n: (cur(b, n), 0)),
                  pl.BlockSpec((BLK, ATTN_KV_DIM), lambda b, n: (prev(b, n), 0)),
                  pl.BlockSpec((ATTN_Q_HEADS, BLK, 2 * BLK), lambda b, n: (0, 0, 0)),
                  pl.BlockSpec((1, ATTN_Q_HEADS), lambda b, n: (0, 0))],
        out_specs=[pl.BlockSpec((BLK, ATTN_Q_DIM), lambda b, n: (cur(b, n), 0)),
                   pl.BlockSpec((BLK, ATTN_Q_HEADS), lambda b, n: (cur(b, n), 0))],
        out_shape=[sds((T, ATTN_Q_DIM), BF16), sds((T, ATTN_Q_HEADS), F32)],
        compiler_params=_cparams("parallel", "parallel"), name=name)(q, k, k, v, v, bias, sinks)


def _proj_specs(wmix):
    return [_part(wmix, 512, 0), _part(wmix, 256, 2), _part(wmix, 256, 3)]


def _natural(w_ref):
    return w_ref[...].reshape(-1, w_ref.shape[2])


def mix_out_fwd(ys, o, gates, h, wmix, g_post, *, name, tm=512):
    T, D = h.shape
    nt = T // tm

    def body(ys_ref, o_ref, gates_ref, h_ref, wssm_ref, wattn_ref, wout_ref, g_ref,
             hout_ref, yssm_ref, yattn_ref, mix_ref, merged_ref):
        y_ssm = _dot(ys_ref[...], _natural(wssm_ref))
        y_attn = _dot(o_ref[...], _natural(wattn_ref))
        yssm_ref[...] = y_ssm
        yattn_ref[...] = y_attn
        merged = (_sigmoid(gates_ref[:, 0:D]) * y_ssm + _sigmoid(gates_ref[:, D:2 * D]) * y_attn).astype(BF16)
        merged_ref[...] = merged
        mix = _dot(merged, _natural(wout_ref))
        mix_ref[...] = mix
        r = lax.rsqrt(jnp.mean(mix * mix, axis=-1, keepdims=True) + RMS_EPS)
        hout_ref[...] = h_ref[...] + mix * r * g_ref[...]

    sds = jax.ShapeDtypeStruct
    return pl.pallas_call(
        body, grid=(nt,),
        in_specs=[_rows(tm, SSM_D_INNER), _rows(tm, ATTN_Q_DIM), _rows(tm, 2 * D), _rows(tm, D),
                  *_proj_specs(wmix), _resident((1, D))],
        out_specs=[_rows(tm, D)] * 5,
        out_shape=[sds((T, D), F32), sds((T, D), F32), sds((T, D), F32), sds((T, D), F32), sds((T, D), BF16)],
        compiler_params=_cparams("parallel"), name=name)(ys, o, gates, h, wmix, wmix, wmix, g_post)


def mix_out_bwd(dh, mix, y_ssm, y_attn, gates, wmix, g_post, *, name, tm=256):
    T, D = dh.shape
    nt = T // tm

    def body(dh_ref, mix_ref, yssm_ref, yattn_ref, gates_ref, wssm_ref, wattn_ref, wout_ref, g_ref,
             dmix_ref, dyssm_ref, dyattn_ref, dgates_ref, dys_ref, do_ref, dg_ref):
        @pl.when(pl.program_id(0) == 0)
        def _():
            dg_ref[...] = jnp.zeros_like(dg_ref)

        do = dh_ref[...]
        mix = mix_ref[...]
        r = lax.rsqrt(jnp.mean(mix * mix, axis=-1, keepdims=True) + RMS_EPS)
        dg_ref[...] += jnp.sum(do * mix * r, axis=0, keepdims=True)
        t = do * g_ref[...]
        dmix = (r * t - mix * (r * r * r * jnp.mean(t * mix, axis=-1, keepdims=True))).astype(BF16)
        dmix_ref[...] = dmix
        dmerged = _dot_nt(dmix, _natural(wout_ref))
        s1 = _sigmoid(gates_ref[:, 0:D])
        s2 = _sigmoid(gates_ref[:, D:2 * D])
        dyssm = (dmerged * s1).astype(BF16)
        dyattn = (dmerged * s2).astype(BF16)
        dyssm_ref[...] = dyssm
        dyattn_ref[...] = dyattn
        dgates_ref[:, 0:D] = (dmerged * yssm_ref[...] * (s1 * (1.0 - s1))).astype(BF16)
        dgates_ref[:, D:2 * D] = (dmerged * yattn_ref[...] * (s2 * (1.0 - s2))).astype(BF16)
        dys_ref[...] = _dot_nt(dyssm, _natural(wssm_ref))
        do_ref[...] = _dot_nt(dyattn, _natural(wattn_ref)).astype(BF16)

    sds = jax.ShapeDtypeStruct
    return pl.pallas_call(
        body, grid=(nt,),
        in_specs=[_rows(tm, D), _rows(tm, D), _rows(tm, D), _rows(tm, D), _rows(tm, 2 * D),
                  *_proj_specs(wmix), _resident((1, D))],
        out_specs=[_rows(tm, D), _rows(tm, D), _rows(tm, D), _rows(tm, 2 * D), _rows(tm, SSM_D_INNER),
                   _rows(tm, ATTN_Q_DIM), pl.BlockSpec((1, D), lambda i: (0, 0))],
        out_shape=[sds((T, D), BF16), sds((T, D), BF16), sds((T, D), BF16), sds((T, 2 * D), BF16),
                   sds((T, SSM_D_INNER), F32), sds((T, ATTN_Q_DIM), BF16), sds((1, D), F32)],
        compiler_params=_cparams("arbitrary"), name=name)(dh, mix, y_ssm, y_attn, gates, wmix, wmix, wmix, g_post)


def attn_bwd(q, k, v, o, do, lse, bias, sinks, *, batch, name):
    T = q.shape[0]
    BLK, HD = ATTN_BLOCK, ATTN_HEAD_DIM
    nb = T // batch // BLK
    cur, prev = _attn_specs(nb)(nb)
    scale = HD ** -0.5

    def body(q_ref, kc_ref, kp_ref, vc_ref, vp_ref, o_ref, do_ref, lse_ref, bias_ref, sink_ref,
             dq_ref, dk_ref, dv_ref, dbias_ref, dsink_ref, ck, cv):
        b = pl.program_id(0)
        n = pl.program_id(1)

        @pl.when(jnp.logical_and(b == 0, n == 0))
        def _():
            dbias_ref[...] = jnp.zeros_like(dbias_ref)
            dsink_ref[...] = jnp.zeros_like(dsink_ref)

        @pl.when(n == 0)
        def _():
            ck[...] = jnp.zeros_like(ck)
            cv[...] = jnp.zeros_like(cv)

        @pl.when(n == nb)
        def _():
            dk_ref[...] = ck[...].astype(BF16)
            dv_ref[...] = cv[...].astype(BF16)

        @pl.when(n < nb)
        def _():
            m_prev, m_cur = _attn_masks(n)
            lane16 = _iota((1, ATTN_Q_HEADS), 1)
            dsink = jnp.zeros((1, ATTN_Q_HEADS), F32)
            for kk in range(ATTN_KV_HEADS):
                ks = slice(HD * kk, HD * (kk + 1))
                kc, kp, vc, vp = kc_ref[:, ks], kp_ref[:, ks], vc_ref[:, ks], vp_ref[:, ks]
                heads = range(ATTN_REP * kk, ATTN_REP * (kk + 1))
                qg, bias_p, bias_c, sink = _attn_group(kk, q_ref, bias_ref, sink_ref)
                dog = jnp.concatenate([do_ref[:, HD * hd:HD * (hd + 1)] for hd in heads], axis=0)
                og = jnp.concatenate([o_ref[:, HD * hd:HD * (hd + 1)] for hd in heads], axis=0)
                lse = jnp.concatenate([lse_ref[:, hd:hd + 1] for hd in heads], axis=0)
                lp = jnp.where(m_prev, _dot_nt(qg, kp) * scale + bias_p, -1e30)
                lc = jnp.where(m_cur, _dot_nt(qg, kc) * scale + bias_c, -1e30)
                pp = jnp.exp(lp - lse)
                pc = jnp.exp(lc - lse)
                delta = jnp.sum(dog.astype(F32) * og.astype(F32), axis=-1, keepdims=True)
                dlp = pp * (_dot_nt(dog, vp) - delta)
                dlc = pc * (_dot_nt(dog, vc) - delta)
                sd = jnp.exp(sink - lse) * delta
                dlpb = dlp.astype(BF16)
                dlcb = dlc.astype(BF16)
                dqg = ((_dot(dlpb, kp) + _dot(dlcb, kc)) * scale).astype(BF16)
                for r, hd in enumerate(heads):
                    rows = slice(BLK * r, BLK * (r + 1))
                    dsink = dsink + jnp.where(lane16 == hd, -jnp.sum(sd[rows], axis=0, keepdims=True), 0.0)
                    dbias_ref[hd, :, 0:BLK] += dlp[rows]
                    dbias_ref[hd, :, BLK:2 * BLK] += dlc[rows]
                    dq_ref[:, HD * hd:HD * (hd + 1)] = dqg[rows]
                dk_ref[:, ks] = (ck[:, ks] + _dot_tn(dlpb, qg) * scale).astype(BF16)
                dv_ref[:, ks] = (cv[:, ks] + _dot_tn(pp.astype(BF16), dog)).astype(BF16)
                ck[:, ks] = _dot_tn(dlcb, qg) * scale
                cv[:, ks] = _dot_tn(pc.astype(BF16), dog)
            dsink_ref[...] += dsink

    sds = jax.ShapeDtypeStruct
    qspec = pl.BlockSpec((BLK, ATTN_Q_DIM), lambda b, n: (cur(b, n), 0))
    cspec = pl.BlockSpec((BLK, ATTN_KV_DIM), lambda b, n: (cur(b, n), 0))
    pspec = pl.BlockSpec((BLK, ATTN_KV_DIM), lambda b, n: (prev(b, n), 0))
    late = pl.BlockSpec((BLK, ATTN_KV_DIM), lambda b, n: (b * nb + jnp.maximum(n - 1, 0), 0))
    return pl.pallas_call(
        body, grid=(batch, nb + 1),
        in_specs=[qspec, cspec, pspec, cspec, pspec, qspec, qspec,
                  pl.BlockSpec((BLK, ATTN_Q_HEADS), lambda b, n: (cur(b, n), 0)),
                  pl.BlockSpec((ATTN_Q_HEADS, BLK, 2 * BLK), lambda b, n: (0, 0, 0)),
                  pl.BlockSpec((1, ATTN_Q_HEADS), lambda b, n: (0, 0))],
        out_specs=[qspec, late, late,
                   pl.BlockSpec((ATTN_Q_HEADS, BLK, 2 * BLK), lambda b, n: (0, 0, 0)),
                   pl.BlockSpec((1, ATTN_Q_HEADS), lambda b, n: (0, 0))],
        out_shape=[sds((T, ATTN_Q_DIM), BF16), sds((T, ATTN_KV_DIM), BF16), sds((T, ATTN_KV_DIM), BF16),
                   sds((ATTN_Q_HEADS, BLK, 2 * BLK), F32), sds((1, ATTN_Q_HEADS), F32)],
        scratch_shapes=[pltpu.VMEM((BLK, ATTN_KV_DIM), F32), pltpu.VMEM((BLK, ATTN_KV_DIM), F32)],
        compiler_params=_cparams("arbitrary", "arbitrary"), name=name)(q, k, k, v, v, o, do, lse, bias, sinks)


def _conv_bwd(dxc, pre, xp_ref, w_ref, carry_ref, acc_ref, dp_ref, g, last):
    Q = SSM_CHUNK
    sg = _sigmoid(pre)
    dpre = dxc * (sg * (1.0 + pre * (1.0 - sg)))

    @pl.when(last)
    def _():
        carry_ref[g] = jnp.zeros(carry_ref.shape[1:], F32)

    dp_ref[0:Q, :] = dpre
    dp_ref[Q:Q + HALO, :] = carry_ref[g]
    carry_ref[g] = dpre[0:HALO, :]
    rows = [jnp.sum(dpre * xp_ref[pl.ds(HALO - 3 + k, Q), :], axis=0, keepdims=True) for k in range(SSM_CONV)]
    rows.append(jnp.sum(dpre, axis=0, keepdims=True))
    rows.append(jnp.zeros((HALO - SSM_CONV - 1, dpre.shape[1]), F32))
    acc_ref[g] += jnp.concatenate(rows, axis=0)
    dx = w_ref[3:4, :] * dpre
    for k in range(SSM_CONV - 1):
        dx = dx + w_ref[k:k + 1, :] * dp_ref[pl.ds(3 - k, Q), :]
    return dx


def _v1_ssd_bwd(dys, y, xbc, z, dt_raw, dt_rawT, states, conv_w, conv_b, dt_bias, a_log, d_skip, norm_g, *, batch, name):
    T = xbc.shape[0]
    Q, GW, N, P, HPG, G, H = SSM_CHUNK, SSM_GW, SSM_STATE, SSM_HEAD_DIM, SSM_HPG, SSM_GROUPS, SSM_HEADS
    nc = T // batch // Q
    sp = _ssd_specs(nc)(lambda c: nc - 1 - c)

    def body(xs_ref, bm_ref, cm_ref, xs_halo, bm_halo, cm_halo, z_ref, y_ref, dys_ref, dt_ref, dtT_ref, st_ref,
             w_xs, w_bm, w_cm, b_xs, b_bm, b_cm, dtb_ref, dtbT_ref, alog_ref, alogT_ref, dskip_ref, ng_ref,
             dz_ref, dxs_ref, dbm_ref, dcm_ref, ddt_ref, acc_xs, acc_bm, acc_cm, acc_head,
             dstate, acsT_s, dacsT_s, yoff_s, dxw_s, dx_s, xp_xs, xp_bm, xp_cm, dp_xs, dp_bm, dp_cm,
             cy_xs, cy_bm, cy_cm):
        b = pl.program_id(0)
        cr = pl.program_id(1)
        g = pl.program_id(2)
        c = nc - 1 - cr
        first = c == 0
        last = cr == 0

        @pl.when(jnp.logical_and(jnp.logical_and(b == 0, cr == 0), g == 0))
        def _():
            acc_xs[...] = jnp.zeros_like(acc_xs)
            acc_bm[...] = jnp.zeros_like(acc_bm)
            acc_cm[...] = jnp.zeros_like(acc_cm)
            acc_head[...] = jnp.zeros_like(acc_head)

        cc = _ssd_chunk_common(first, g, xs_ref, bm_ref, cm_ref, xs_halo, bm_halo, cm_halo, w_xs, w_bm, w_cm,
                               b_xs, b_bm, b_cm, dt_ref, dtT_ref, dtb_ref, dtbT_ref, alog_ref, alogT_ref,
                               xp_xs, xp_bm, xp_cm)
        xs, acs, dt, a = cc["xs"], cc["acs"], cc["dt"], cc["a"]
        acsT_s[...] = cc["acsT"]
        dacsT_s[...] = jnp.zeros_like(dacsT_s)
        e64 = _head_expand(g, P)
        e128 = _head_expand(g, Q)
        acs_x = _dot_hi(acs, e64)
        acs_b = _dot_hi(acs, e128)
        dt_x = _dot_hi(dt, e64)
        x = xs * dt_x
        w_x = jnp.exp(acs_x[Q - 1:Q, :] - acs_x)
        ex = jnp.exp(acs_x)

        yv = y_ref[...]
        zz = z_ref[...]
        sz = _sigmoid(zz)
        silu_z = zz * sz
        yg = yv * silu_z
        rr = lax.rsqrt(jnp.mean(yg * yg, axis=-1, keepdims=True) + RMS_EPS)
        dys_v = dys_ref[...]
        d_ng = jnp.sum(dys_v * yg * rr, axis=0, keepdims=True)
        t = dys_v * ng_ref[...]
        dyg = rr * t - yg * (rr * rr * rr * jnp.mean(t * yg, axis=-1, keepdims=True))
        dy = dyg * silu_z
        dz_ref[...] = (dyg * yv * (sz * (1.0 + zz * (1.0 - sz)))).astype(BF16)

        dexp = _dot_hi(jnp.broadcast_to(dskip_ref[...], (8, H)), e64)[0:1, :]
        d_dskip = _dot_nt(jnp.broadcast_to(jnp.sum(dy * xs, axis=0, keepdims=True), (8, GW)), e64)[0:1, :]

        dyb = dy.astype(BF16)
        xb = x.astype(BF16)
        xwb = (x * w_x).astype(BF16)
        bb = cc["bm"].astype(BF16)
        cb = cc["cm"].astype(BF16)
        s = _dot_nt(cb, bb)
        causal = _iota((Q, Q), 0) >= _iota((Q, Q), 1)
        lane_h = _iota((1, H), 1)
        ds_acc = jnp.zeros((Q, Q), F32)
        d_c = jnp.zeros((Q, N), F32)
        d_b = jnp.zeros((Q, N), F32)
        dacs = jnp.zeros((Q, H), F32)
        last_terms = jnp.zeros((1, H), F32)
        for r in range(HPG):
            hd = HPG * g + r
            cols = slice(P * r, P * (r + 1))

            @pl.when(last)
            def _():
                dstate[hd] = jnp.zeros((P, N), F32)

            seg = acs_b[:, Q * r:Q * (r + 1)] - acsT_s[pl.ds(hd, 1), :]
            l = jnp.exp(jnp.where(causal, seg, -1e30))
            m = s * l
            mb = m.astype(BF16)
            dyh = dyb[:, cols]
            hp = st_ref[0, r]
            hpb = hp.astype(BF16)
            dh = dstate[hd]
            dhb = dh.astype(BF16)
            yoff_s[:, cols] = _dot_nt(cb, hpb) * ex[:, cols]
            dye = (dy[:, cols] * ex[:, cols]).astype(BF16)
            d_c = d_c + _dot(dye, hpb)
            dhp_off = _dot_tn(dye, cb)
            dm = _dot_nt(dyh, xb[:, cols])
            dx_s[:, cols] = _dot_tn(mb, dyh)
            gmat = dm * m
            onehot = (lane_h == hd).astype(F32)
            dacs = dacs + jnp.sum(gmat, axis=-1, keepdims=True) * onehot
            dacsT_s[pl.ds(hd, 1), :] = -jnp.sum(gmat, axis=0, keepdims=True)
            ds_acc = ds_acc + dm * l
            dxw_s[:, cols] = _dot_nt(bb, dhb)
            d_b = d_b + _dot(xwb[:, cols], dhb)
            decay = jnp.exp(acsT_s[pl.ds(hd, 1), pl.ds(Q - 1, 1)])
            ddecay = jnp.sum(jnp.sum(dh * hp, axis=-1, keepdims=True), axis=0, keepdims=True)
            last_terms = last_terms + (ddecay * decay) * onehot
            dstate[hd] = dh * decay + dhp_off
        dsb = ds_acc.astype(BF16)
        d_c = d_c + _dot(dsb, bb)
        d_b = d_b + _dot_tn(dsb, cb)
        dxw = dxw_s[...]
        dx_full = dx_s[...] + dxw * w_x
        tw = _dot_nt(dxw * x * w_x, e64)
        dacs = dacs + _dot_nt(dy * yoff_s[...], e64) - tw
        last_terms = last_terms + jnp.sum(tw, axis=0, keepdims=True)
        eye = (_iota((Q, Q), 0) == _iota((Q, Q), 1)).astype(F32)
        dacs = dacs + lax.dot_general(eye, dacsT_s[...], (((1,), (1,)), ((), ())), preferred_element_type=F32,
                                      precision=lax.Precision.HIGHEST)
        dacs = dacs + jnp.where(_iota((Q, 1), 0) == Q - 1, 1.0, 0.0) * last_terms
        d_dta = _dot_hi(cc["triT"], dacs)
        ddt = d_dta * a + _dot_nt(dx_full * xs, e64)
        d_alog = jnp.sum(d_dta * dt, axis=0, keepdims=True) * a
        ddt_raw = ddt * _sigmoid(cc["dtr"])
        d_dtb = jnp.sum(ddt_raw, axis=0, keepdims=True)

        @pl.when(g == 0)
        def _():
            ddt_ref[...] = ddt_raw

        @pl.when(g > 0)
        def _():
            ddt_ref[...] += ddt_raw

        acc_head[...] += jnp.concatenate([d_dtb, d_alog, d_dskip, jnp.zeros((5, H), F32)], axis=0)
        dxs = dexp * dy + dx_full * dt_x
        dxs_ref[...] = _conv_bwd(dxs, cc["pre_xs"], xp_xs, w_xs, cy_xs, acc_xs, dp_xs, g, last).astype(BF16)
        dbm_ref[...] = _conv_bwd(d_b, cc["pre_bm"], xp_bm, w_bm, cy_bm, acc_bm, dp_bm, g, last).astype(BF16)
        dcm_ref[...] = _conv_bwd(d_c, cc["pre_cm"], xp_cm, w_cm, cy_cm, acc_cm, dp_cm, g, last).astype(BF16)
        acc_xs[g, pl.ds(SSM_CONV + 1, 1), :] += d_ng

    col = lambda v: v.reshape(H, 1)
    sds = jax.ShapeDtypeStruct
    row = lambda b, c, g: b * nc + (nc - 1 - c)
    full = lambda shape: pl.BlockSpec(shape, lambda b, c, g: (0,) * len(shape))
    return pl.pallas_call(
        body, grid=(batch, nc, G),
        in_specs=[sp["xs"], sp["bm"], sp["cm"], sp["xs_halo"], sp["bm_halo"], sp["cm_halo"], sp["grp"], sp["grp"],
                  sp["grp"], sp["dt"], sp["dtT"], sp["state"],
                  sp["w_xs"], sp["w_bm"], sp["w_cm"], sp["b_xs"], sp["b_bm"], sp["b_cm"],
                  sp["row32"], sp["col32"], sp["row32"], sp["col32"], sp["row32"], sp["vec_g"]],
        out_specs=[sp["grp"], sp["grp"],
                   pl.BlockSpec((Q, N), lambda b, c, g: (row(b, c, g), g)),
                   pl.BlockSpec((Q, N), lambda b, c, g: (row(b, c, g), g)),
                   sp["dt"], full((G, HALO, GW)), full((G, HALO, N)), full((G, HALO, N)), full((8, H))],
        out_shape=[sds((T, SSM_D_INNER), BF16), sds((T, SSM_D_INNER), BF16), sds((T, G * N), BF16),
                   sds((T, G * N), BF16), sds((T, H), F32),
                   sds((G, HALO, GW), F32), sds((G, HALO, N), F32), sds((G, HALO, N), F32), sds((8, H), F32)],
        scratch_shapes=[pltpu.VMEM((H, P, N), F32), pltpu.VMEM((H, Q), F32), pltpu.VMEM((H, Q), F32),
                        pltpu.VMEM((Q, GW), F32), pltpu.VMEM((Q, GW), F32), pltpu.VMEM((Q, GW), F32),
                        pltpu.VMEM((HALO + Q, GW), F32), pltpu.VMEM((HALO + Q, N), F32), pltpu.VMEM((HALO + Q, N), F32),
                        pltpu.VMEM((Q + HALO, GW), F32), pltpu.VMEM((Q + HALO, N), F32), pltpu.VMEM((Q + HALO, N), F32),
                        pltpu.VMEM((G, HALO, GW), F32), pltpu.VMEM((G, HALO, N), F32), pltpu.VMEM((G, HALO, N), F32)],
        compiler_params=_cparams("arbitrary", "arbitrary", "arbitrary"), name=name,
    )(xbc, xbc, xbc, xbc, xbc, xbc, z, y, dys, dt_raw, dt_rawT, states, conv_w, conv_w, conv_w, conv_b, conv_b, conv_b,
      dt_bias, col(dt_bias), a_log, col(a_log), d_skip, norm_g)


def mix_in_bwd(dh, h, g, dgates, dz, dxs, dbm, dcm, ddtT, dq, dk, dv, w_gz, w_xbc, w_dtT, w_qkv, *, name, tm=512):
    T, D = h.shape
    nt = T // tm
    GN = SSM_GROUPS * SSM_STATE

    def body(dh_ref, h_ref, g_ref, dgates_ref, dz_ref, dxs_ref, dbm_ref, dcm_ref, ddt_ref, dq_ref, dk_ref, dv_ref,
             wgz_ref, wxbc_ref, wdt_ref, wqkv_ref, dhin_ref, dg_ref):
        @pl.when(pl.program_id(0) == 0)
        def _():
            dg_ref[...] = jnp.zeros_like(dg_ref)

        du = _dot_nt(dgates_ref[...], wgz_ref[:, 0:2048])
        du = du + _dot_nt(dz_ref[...], wgz_ref[:, 2048:4096])
        du = du + _dot_nt(dxs_ref[...], wxbc_ref[:, 0:SSM_D_INNER])
        du = du + _dot_nt(dbm_ref[...], wxbc_ref[:, SSM_D_INNER:SSM_D_INNER + GN])
        du = du + _dot_nt(dcm_ref[...], wxbc_ref[:, SSM_D_INNER + GN:])
        du = du + _dot_tn(ddt_ref[...].astype(BF16), wdt_ref[...])
        du = du + _dot_nt(dq_ref[...], wqkv_ref[:, 0:ATTN_Q_DIM])
        du = du + _dot_nt(dk_ref[...], wqkv_ref[:, ATTN_Q_DIM:ATTN_Q_DIM + ATTN_KV_DIM])
        du = du + _dot_nt(dv_ref[...], wqkv_ref[:, ATTN_Q_DIM + ATTN_KV_DIM:])
        hh = h_ref[...]
        r = lax.rsqrt(jnp.mean(hh * hh, axis=-1, keepdims=True) + RMS_EPS)
        dg_ref[...] += jnp.sum(du * hh * r, axis=0, keepdims=True)
        t = du * g_ref[...]
        dhin_ref[...] = dh_ref[...] + r * t - hh * (r * r * r * jnp.mean(t * hh, axis=-1, keepdims=True))

    sds = jax.ShapeDtypeStruct
    return pl.pallas_call(
        body, grid=(nt,),
        in_specs=[_rows(tm, D), _rows(tm, D), _resident((1, D)), _rows(tm, 2048), _rows(tm, 2048), _rows(tm, SSM_D_INNER),
                  _rows(tm, GN), _rows(tm, GN), pl.BlockSpec((SSM_HEADS, tm), lambda i: (0, i)),
                  _rows(tm, ATTN_Q_DIM), _rows(tm, ATTN_KV_DIM),
                  _rows(tm, ATTN_KV_DIM), _resident(w_gz.shape), _resident(w_xbc.shape), _resident(w_dtT.shape),
                  _resident(w_qkv.shape)],
        out_specs=[_rows(tm, D), pl.BlockSpec((1, D), lambda i: (0, 0))],
        out_shape=[sds((T, D), F32), sds((1, D), F32)],
        compiler_params=_cparams("arbitrary"), name=name,
    )(dh, h, g, dgates, dz, dxs, dbm, dcm, ddtT, dq, dk, dv, w_gz, w_xbc, w_dtT, w_qkv)


PAIRS = SSM_HPG // 2
PW = 2 * SSM_HEAD_DIM


def _ssd_prologue(first, xs_ref, bm_ref, cm_ref, xs_halo, bm_halo, cm_halo, w_xs, w_bm, w_cm, b_xs, b_bm, b_cm,
                  dtT_ref, dtb_ref, alog_ref, xp_xs, xp_bm, xp_cm):
    Q = SSM_CHUNK
    pre_xs = _conv_pre(xs_ref, xs_halo, w_xs, b_xs, xp_xs, first)
    pre_bm = _conv_pre(bm_ref, bm_halo, w_bm, b_bm, xp_bm, first)
    pre_cm = _conv_pre(cm_ref, cm_halo, w_cm, b_cm, xp_cm, first)
    dtrT = dtT_ref[...] + dtb_ref[...]
    dtT = _softplus(dtrT)
    aT = -jnp.exp(alog_ref[...])
    triT = (_iota((Q, Q), 0) <= _iota((Q, Q), 1)).astype(F32)
    acsT = _dot_hi(dtT * aT, triT)
    lastT = acsT[:, Q - 1:Q]
    wT = jnp.exp(lastT - acsT)
    eT = jnp.exp(acsT)
    cols = jnp.concatenate([dtT, acsT, wT, eT], axis=0).T
    return dict(pre_xs=pre_xs, pre_bm=pre_bm, pre_cm=pre_cm, xs=pre_xs * _sigmoid(pre_xs), bm=pre_bm * _sigmoid(pre_bm),
                cm=pre_cm * _sigmoid(pre_cm), dtrT=dtrT, dtT=dtT, aT=aT, acsT=acsT, decayT=jnp.exp(lastT), cols=cols)


def _pair_cols(cols, base, p, lo):
    k = base + 2 * p
    return jnp.where(lo, cols[:, k:k + 1], cols[:, k + 1:k + 2])


def _pair_row(colT, p, lo_row):
    return jnp.where(lo_row, colT[2 * p:2 * p + 1, :], colT[2 * p + 1:2 * p + 2, :])


def _pair_operands(pp, p, s, causal, lo, xb):
    zero = jnp.zeros_like(xb)
    rhs = jnp.concatenate([jnp.where(lo, xb, zero), jnp.where(lo, zero, xb)], axis=0)
    ls, ms = [], []
    for k in (2 * p, 2 * p + 1):
        seg = pp["cols"][:, 8 + k:9 + k] - pp["acsT"][k:k + 1, :]
        l = jnp.exp(jnp.where(causal, seg, -1e30))
        ls.append(l)
        ms.append(s * l)
    lhs = jnp.concatenate([m.astype(BF16) for m in ms], axis=1)
    return lhs, rhs, ls


def ssd_fwd(xbc, z, dt_rawT, conv_w, conv_b, dt_bias, a_log, d_skip_x, norm_g, *, batch, name):
    T = xbc.shape[0]
    Q, GW, N = SSM_CHUNK, SSM_GW, SSM_STATE
    nc = T // batch // Q
    sp = _ssd_specs(nc)(lambda c: c)

    def body(xs_ref, bm_ref, cm_ref, xs_halo, bm_halo, cm_halo, z_ref, dtT_ref,
             w_xs, w_bm, w_cm, b_xs, b_bm, b_cm, dtb_ref, alog_ref, dsk_ref, ng_ref,
             y_ref, ys_ref, st_ref, state, xp_xs, xp_bm, xp_cm):
        c = pl.program_id(1)
        g = pl.program_id(2)
        first = c == 0
        pp = _ssd_prologue(first, xs_ref, bm_ref, cm_ref, xs_halo, bm_halo, cm_halo, w_xs, w_bm, w_cm,
                           b_xs, b_bm, b_cm, dtT_ref, dtb_ref, alog_ref, xp_xs, xp_bm, xp_cm)
        xs = pp["xs"]
        bb = pp["bm"].astype(BF16)
        cb = pp["cm"].astype(BF16)
        s = _dot_nt(cb, bb)
        causal = _iota((Q, Q), 0) >= _iota((Q, Q), 1)
        lo = _iota((Q, PW), 1) < SSM_HEAD_DIM
        lo_row = _iota((1, PW), 1) < SSM_HEAD_DIM
        ys = []
        for p in range(PAIRS):
            tile = slice(PW * p, PW * (p + 1))

            @pl.when(first)
            def _():
                state[g, p] = jnp.zeros((N, PW), F32)

            xs_p = xs[:, tile]
            x_p = xs_p * _pair_cols(pp["cols"], 0, p, lo)
            lhs, rhs, _ = _pair_operands(pp, p, s, causal, lo, x_p.astype(BF16))
            hp = state[g, p]
            st_ref[0, p] = hp
            ys.append(_dot(lhs, rhs) + _dot(cb, hp.astype(BF16)) * _pair_cols(pp["cols"], 24, p, lo)
                      + dsk_ref[:, tile] * xs_p)
            xw = (x_p * _pair_cols(pp["cols"], 16, p, lo)).astype(BF16)
            state[g, p] = hp * _pair_row(pp["decayT"], p, lo_row) + _dot_tn(bb, xw)
        y = jnp.concatenate(ys, axis=1)
        y_ref[...] = y
        zz = z_ref[...]
        yg = y * (zz * _sigmoid(zz))
        rr = lax.rsqrt(jnp.mean(yg * yg, axis=-1, keepdims=True) + RMS_EPS)
        ys_ref[...] = (yg * rr * ng_ref[...]).astype(BF16)

    sds = jax.ShapeDtypeStruct
    return pl.pallas_call(
        body, grid=(batch, nc, SSM_GROUPS),
        in_specs=[sp["xs"], sp["bm"], sp["cm"], sp["xs_halo"], sp["bm_halo"], sp["cm_halo"], sp["grp"], sp["dtT_g"],
                  sp["w_xs"], sp["w_bm"], sp["w_cm"], sp["b_xs"], sp["b_bm"], sp["b_cm"],
                  sp["col_g"], sp["col_g"], sp["vec_g"], sp["vec_g"]],
        out_specs=[sp["grp"], sp["grp"], sp["pairs"]],
        out_shape=[sds((T, SSM_D_INNER), F32), sds((T, SSM_D_INNER), BF16),
                   sds((T // Q, SSM_GROUPS * PAIRS, N, PW), F32)],
        scratch_shapes=[pltpu.VMEM((SSM_GROUPS, PAIRS, N, PW), F32),
                        pltpu.VMEM((HALO + Q, GW), F32), pltpu.VMEM((HALO + Q, N), F32), pltpu.VMEM((HALO + Q, N), F32)],
        compiler_params=_cparams("arbitrary", "arbitrary", "arbitrary"), name=name,
    )(xbc, xbc, xbc, xbc, xbc, xbc, z, dt_rawT, conv_w, conv_w, conv_w, conv_b, conv_b, conv_b,
      dt_bias, a_log, d_skip_x, norm_g)


def ssd_bwd(dys, y, xbc, z, dt_rawT, states, conv_w, conv_b, dt_bias, a_log, d_skip_x, norm_g, *, batch, name):
    T = xbc.shape[0]
    Q, GW, N, G = SSM_CHUNK, SSM_GW, SSM_STATE, SSM_GROUPS
    nc = T // batch // Q
    sp = _ssd_specs(nc)(lambda c: nc - 1 - c)

    def body(xs_ref, bm_ref, cm_ref, xs_halo, bm_halo, cm_halo, z_ref, y_ref, dys_ref, dtT_ref, st_ref,
             w_xs, w_bm, w_cm, b_xs, b_bm, b_cm, dtb_ref, alog_ref, dsk_ref, ng_ref,
             dz_ref, dxs_ref, dbm_ref, dcm_ref, ddtT_ref, acc_xs, acc_bm, acc_cm, acc_head,
             dstate, xp_xs, xp_bm, xp_cm, dp_xs, dp_bm, dp_cm, cy_xs, cy_bm, cy_cm):
        b = pl.program_id(0)
        cr = pl.program_id(1)
        g = pl.program_id(2)
        first = cr == nc - 1
        last = cr == 0

        @pl.when(jnp.logical_and(jnp.logical_and(b == 0, cr == 0), g == 0))
        def _():
            acc_xs[...] = jnp.zeros_like(acc_xs)
            acc_bm[...] = jnp.zeros_like(acc_bm)
            acc_cm[...] = jnp.zeros_like(acc_cm)
            acc_head[...] = jnp.zeros_like(acc_head)

        pp = _ssd_prologue(first, xs_ref, bm_ref, cm_ref, xs_halo, bm_halo, cm_halo, w_xs, w_bm, w_cm,
                           b_xs, b_bm, b_cm, dtT_ref, dtb_ref, alog_ref, xp_xs, xp_bm, xp_cm)
        xs, dtT, aT, decayT = pp["xs"], pp["dtT"], pp["aT"], pp["decayT"]

        yv = y_ref[...]
        zz = z_ref[...]
        sz = _sigmoid(zz)
        silu_z = zz * sz
        yg = yv * silu_z
        rr = lax.rsqrt(jnp.mean(yg * yg, axis=-1, keepdims=True) + RMS_EPS)
        dys_v = dys_ref[...]
        d_ng = jnp.sum(dys_v * yg * rr, axis=0, keepdims=True)
        t = dys_v * ng_ref[...]
        dyg = rr * t - yg * (rr * rr * rr * jnp.mean(t * yg, axis=-1, keepdims=True))
        dy = dyg * silu_z
        dz_ref[...] = (dyg * yv * (sz * (1.0 + zz * (1.0 - sz)))).astype(BF16)
        dsk = dsk_ref[...]
        d_dsk = jnp.sum(dy * xs, axis=0, keepdims=True)

        bb = pp["bm"].astype(BF16)
        cb = pp["cm"].astype(BF16)
        s = _dot_nt(cb, bb)
        causal = _iota((Q, Q), 0) >= _iota((Q, Q), 1)
        lo = _iota((Q, PW), 1) < SSM_HEAD_DIM
        lo_row = _iota((1, PW), 1) < SSM_HEAD_DIM
        sub8 = _iota((SSM_HPG, 1), 0)
        ds_acc = jnp.zeros((Q, Q), F32)
        d_c = jnp.zeros((Q, N), F32)
        d_b = jnp.zeros((Q, N), F32)
        last_terms = jnp.zeros((SSM_HPG, 1), F32)
        q1, q2, dxs = [], [], []
        for p in range(PAIRS):
            tile = slice(PW * p, PW * (p + 1))

            @pl.when(last)
            def _():
                dstate[g, p] = jnp.zeros((N, PW), F32)

            dt_p = _pair_cols(pp["cols"], 0, p, lo)
            w_p = _pair_cols(pp["cols"], 16, p, lo)
            e_p = _pair_cols(pp["cols"], 24, p, lo)
            xs_p = xs[:, tile]
            x_p = xs_p * dt_p
            xw_p = x_p * w_p
            lhs, rhs, ls = _pair_operands(pp, p, s, causal, lo, x_p.astype(BF16))
            dy_p = dy[:, tile]
            dyb = dy_p.astype(BF16)
            hp = st_ref[0, p]
            hpb = hp.astype(BF16)
            dh = dstate[g, p]
            dhb = dh.astype(BF16)
            dye = (dy_p * e_p).astype(BF16)
            d_c = d_c + _dot_nt(dye, hpb)
            dm = _dot_nt(dyb, rhs)
            dxd2 = _dot_tn(lhs, dyb)
            dxd = jnp.where(lo, dxd2[0:Q], dxd2[Q:2 * Q])
            ds_acc = ds_acc + dm[:, 0:Q] * ls[0] + dm[:, Q:2 * Q] * ls[1]
            dxw = _dot(bb, dhb)
            d_b = d_b + _dot_nt(xw_p.astype(BF16), dhb)
            dx_full = dxd + dxw * w_p
            tw = dxw * xw_p
            yd = _dot(lhs, rhs)
            yoff = _dot(cb, hpb) * e_p
            q1.append(dyb.astype(F32) * yd + dy_p * yoff - tw - x_p.astype(BF16).astype(F32) * dxd)
            q2.append(dx_full * xs_p)
            dxs.append(dsk[:, tile] * dy_p + dx_full * dt_p)
            row = jnp.sum(dh * hp, axis=0, keepdims=True) * _pair_row(decayT, p, lo_row) + jnp.sum(tw, axis=0, keepdims=True)
            t_lo = jnp.sum(jnp.where(lo_row, row, 0.0), axis=1, keepdims=True)
            t_hi = jnp.sum(jnp.where(lo_row, 0.0, row), axis=1, keepdims=True)
            last_terms = last_terms + jnp.where(sub8 == 2 * p, t_lo, 0.0) + jnp.where(sub8 == 2 * p + 1, t_hi, 0.0)
            dstate[g, p] = dh * _pair_row(decayT, p, lo_row) + _dot_tn(cb, dye)
        dsb = ds_acc.astype(BF16)
        d_c = d_c + _dot(dsb, bb)
        d_b = d_b + _dot_tn(dsb, cb)
        e8 = (_iota((SSM_HPG, GW), 0) == lax.shift_right_logical(_iota((SSM_HPG, GW), 1), 6)).astype(F32)
        seg_sum = lambda tiles: lax.dot_general(e8, jnp.concatenate(tiles, axis=1), (((1,), (1,)), ((), ())),
                                                preferred_element_type=F32, precision=lax.Precision.HIGHEST)
        dacsT = seg_sum(q1) + jnp.where(_iota((1, Q), 1) == Q - 1, 1.0, 0.0) * last_terms
        tri = (_iota((Q, Q), 0) >= _iota((Q, Q), 1)).astype(F32)
        d_dtaT = _dot_hi(dacsT, tri)
        ddtT = d_dtaT * aT + seg_sum(q2)
        d_alog = jnp.sum(d_dtaT * dtT, axis=1, keepdims=True) * aT
        ddt_rawT = ddtT * _sigmoid(pp["dtrT"])
        ddtT_ref[...] = ddt_rawT
        d_dtb = jnp.sum(ddt_rawT, axis=1, keepdims=True)
        lane = _iota((SSM_HPG, N), 1)
        acc_head[g] += jnp.where(lane == 0, d_dtb, 0.0) + jnp.where(lane == 1, d_alog, 0.0)
        dxs_v = jnp.concatenate(dxs, axis=1)
        dxs_ref[...] = _conv_bwd(dxs_v, pp["pre_xs"], xp_xs, w_xs, cy_xs, acc_xs, dp_xs, g, last).astype(BF16)
        dbm_ref[...] = _conv_bwd(d_b, pp["pre_bm"], xp_bm, w_bm, cy_bm, acc_bm, dp_bm, g, last).astype(BF16)
        dcm_ref[...] = _conv_bwd(d_c, pp["pre_cm"], xp_cm, w_cm, cy_cm, acc_cm, dp_cm, g, last).astype(BF16)
        acc_xs[g, pl.ds(SSM_CONV + 1, 2), :] += jnp.concatenate([d_ng, d_dsk], axis=0)

    sds = jax.ShapeDtypeStruct
    row = lambda b, c, g: b * nc + (nc - 1 - c)
    full = lambda shape: pl.BlockSpec(shape, lambda b, c, g: (0,) * len(shape))
    return pl.pallas_call(
        body, grid=(batch, nc, G),
        in_specs=[sp["xs"], sp["bm"], sp["cm"], sp["xs_halo"], sp["bm_halo"], sp["cm_halo"], sp["grp"], sp["grp"],
                  sp["grp"], sp["dtT_g"], sp["pairs"],
                  sp["w_xs"], sp["w_bm"], sp["w_cm"], sp["b_xs"], sp["b_bm"], sp["b_cm"],
                  sp["col_g"], sp["col_g"], sp["vec_g"], sp["vec_g"]],
        out_specs=[sp["grp"], sp["grp"],
                   pl.BlockSpec((Q, N), lambda b, c, g: (row(b, c, g), g)),
                   pl.BlockSpec((Q, N), lambda b, c, g: (row(b, c, g), g)),
                   sp["dtT_g"], full((G, HALO, GW)), full((G, HALO, N)), full((G, HALO, N)), full((G, SSM_HPG, N))],
        out_shape=[sds((T, SSM_D_INNER), BF16), sds((T, SSM_D_INNER), BF16), sds((T, G * N), BF16),
                   sds((T, G * N), BF16), sds((SSM_HEADS, T), F32),
                   sds((G, HALO, GW), F32), sds((G, HALO, N), F32), sds((G, HALO, N), F32), sds((G, SSM_HPG, N), F32)],
        scratch_shapes=[pltpu.VMEM((G, PAIRS, N, PW), F32),
                        pltpu.VMEM((HALO + Q, GW), F32), pltpu.VMEM((HALO + Q, N), F32), pltpu.VMEM((HALO + Q, N), F32),
                        pltpu.VMEM((Q + HALO, GW), F32), pltpu.VMEM((Q + HALO, N), F32), pltpu.VMEM((Q + HALO, N), F32),
                        pltpu.VMEM((G, HALO, GW), F32), pltpu.VMEM((G, HALO, N), F32), pltpu.VMEM((G, HALO, N), F32)],
        compiler_params=_cparams("arbitrary", "arbitrary", "arbitrary"), name=name,
    )(xbc, xbc, xbc, xbc, xbc, xbc, z, y, dys, dt_rawT, states, conv_w, conv_w, conv_w, conv_b, conv_b, conv_b,
      dt_bias, a_log, d_skip_x, norm_g)


def mm_rows(a, b, *, name, tt=2048):
    M, T = a.shape
    N = b.shape[1]

    def body(a_ref, b_ref, o_ref):
        @pl.when(pl.program_id(0) == 0)
        def _():
            o_ref[...] = jnp.zeros_like(o_ref)

        o_ref[...] += _dot(a_ref[...].astype(BF16), b_ref[...])

    return pl.pallas_call(
        body, grid=(T // tt,),
        in_specs=[pl.BlockSpec((M, tt), lambda t: (0, t)), pl.BlockSpec((tt, N), lambda t: (t, 0))],
        out_specs=pl.BlockSpec((M, N), lambda t: (0, 0)), out_shape=jax.ShapeDtypeStruct((M, N), F32),
        compiler_params=_cparams("arbitrary"), name=name)(a, b)


MESH = pl.DeviceIdType.MESH
ANY = pl.BlockSpec(memory_space=pl.ANY)
ROW_ALIGN = 16


def _me():
    return lax.axis_index("x"), lax.axis_index("y"), lax.axis_index("c")


def _other_chips(x, y):
    return [(1 - x, y), (x, 1 - y), (1 - x, 1 - y)]


def _remote(src, dst, send_sem, recv_sem, to):
    return pltpu.make_async_remote_copy(src_ref=src, dst_ref=dst, send_sem=send_sem, recv_sem=recv_sem,
                                        device_id=to, device_id_type=MESH)


def _half(c, rows):
    return pl.ds(pl.multiple_of(c * (rows // 2), ROW_ALIGN), rows // 2)


def ag_weights(bufs, *, name):
    n = len(bufs)

    def body(*refs):
        outs = refs[n:2 * n]
        ici_send, ici_recv, d2d_send, d2d_recv = refs[2 * n:]
        x, y, c = _me()
        s_me = 2 * x + y
        sib = (x, y, 1 - c)
        chips = _other_chips(x, y)
        sent = []
        for i in range(n):
            mine = outs[i].at[s_me, _half(c, outs[i].shape[1])]
            for j, chip in enumerate(chips):
                sent.append(_remote(mine, mine, ici_send.at[i, j], ici_recv.at[i, j], (*chip, c)))
                sent[-1].start()
        for i in range(n):
            for j, chip in enumerate(chips):
                landed = outs[i].at[2 * chip[0] + chip[1], _half(c, outs[i].shape[1])]
                _remote(landed, landed, ici_send.at[i, j], ici_recv.at[i, j], (*chip, c)).wait_recv()
                sent.append(_remote(landed, landed, d2d_send.at[i, j], d2d_recv.at[i, j], sib))
                sent[-1].start()
        for i in range(n):
            for j, chip in enumerate(chips):
                lands = outs[i].at[2 * chip[0] + chip[1], _half(1 - c, outs[i].shape[1])]
                _remote(lands, lands, d2d_send.at[i, j], d2d_recv.at[i, j], sib).wait_recv()
        for cp in sent:
            cp.wait_send()

    return pl.pallas_call(
        body, in_specs=[ANY] * n, out_specs=[ANY] * n,
        out_shape=[jax.ShapeDtypeStruct(b.shape, b.dtype) for b in bufs],
        input_output_aliases={i: i for i in range(n)},
        scratch_shapes=[pltpu.SemaphoreType.DMA((n, 3)), pltpu.SemaphoreType.DMA((n, 3)),
                        pltpu.SemaphoreType.DMA((n, 3)), pltpu.SemaphoreType.DMA((n, 3))],
        name=name)(*bufs)


def rs_pair(grads, *, name):
    n = len(grads)

    def body(*refs):
        ins, outs = refs[:n], refs[n:2 * n]
        send, recv = refs[2 * n:]
        x, y, c = _me()
        sib = (x, y, 1 - c)
        sent = []
        for i in range(n):
            rows = ins[i].shape[1]
            sent.append(_remote(ins[i].at[:, _half(1 - c, rows), :], outs[i], send.at[i], recv.at[i], sib))
            sent[-1].start()
        for cp in sent:
            cp.wait()

    return pl.pallas_call(
        body, in_specs=[ANY] * n, out_specs=[ANY] * n,
        out_shape=[jax.ShapeDtypeStruct((N_SHARD, g.shape[1] // 2, g.shape[2]), g.dtype) for g in grads],
        scratch_shapes=[pltpu.SemaphoreType.DMA((n,)), pltpu.SemaphoreType.DMA((n,))], name=name)(*grads)


def rs_add(grad, part, c, *, rt, name):
    _, rows, cols = grad.shape
    r2 = rows // 2
    nrb = r2 // rt

    def body(c_ref, g_ref, p_ref, o_ref):
        o_ref[...] = (g_ref[...] + p_ref[...]).astype(BF16)

    return pl.pallas_call(
        body,
        grid_spec=pltpu.PrefetchScalarGridSpec(
            num_scalar_prefetch=1, grid=(N_SHARD, nrb),
            in_specs=[pl.BlockSpec((1, rt, cols), lambda k, i, c_ref: (k, c_ref[1] * nrb + i, 0)),
                      pl.BlockSpec((1, rt, cols), lambda k, i, c_ref: (k, i, 0))],
            out_specs=pl.BlockSpec((1, rt, cols), lambda k, i, c_ref: (k, i, 0))),
        out_shape=jax.ShapeDtypeStruct((N_SHARD, r2, cols), BF16),
        compiler_params=_cparams("parallel", "parallel"), name=name)(c, grad, part)


def rs_chips(sums, *, name):
    n = len(sums)

    def body(*refs):
        ins, outs = refs[:n], refs[n:2 * n]
        send, recv = refs[2 * n:]
        x, y, c = _me()
        s_me = 2 * x + y
        sent = []
        for i in range(n):
            for j, chip in enumerate(_other_chips(x, y)):
                sent.append(_remote(ins[i].at[2 * chip[0] + chip[1]], outs[i].at[s_me], send.at[i, j], recv.at[i, j],
                                    (*chip, c)))
                sent[-1].start()
        for cp in sent:
            cp.wait()

    return pl.pallas_call(
        body, in_specs=[ANY] * n, out_specs=[ANY] * n,
        out_shape=[jax.ShapeDtypeStruct(s.shape, s.dtype) for s in sums],
        scratch_shapes=[pltpu.SemaphoreType.DMA((n, 3)), pltpu.SemaphoreType.DMA((n, 3))],
        name=name)(*sums)


def rs_total(parts, own, where, *, rt, name):
    _, r2, cols = parts.shape
    nrb = r2 // rt

    def body(w_ref, p0, p1, p2, p3, own_ref, o_ref):
        s_me = w_ref[0]
        acc = None
        for k, p in enumerate((p0, p1, p2, p3)):
            term = jnp.where(s_me == k, own_ref[0], p[0]).astype(F32)
            acc = term if acc is None else acc + term
        o_ref[...] = acc

    def slot(k):
        return pl.BlockSpec((1, rt, cols), lambda i, w: (jnp.where(w[0] == k, (k + 1) % N_SHARD, k), i, 0))

    return pl.pallas_call(
        body,
        grid_spec=pltpu.PrefetchScalarGridSpec(
            num_scalar_prefetch=1, grid=(nrb,),
            in_specs=[slot(0), slot(1), slot(2), slot(3), pl.BlockSpec((1, rt, cols), lambda i, w: (w[0], i, 0))],
            out_specs=pl.BlockSpec((rt, cols), lambda i, w: (w[1] * nrb + i, 0))),
        out_shape=jax.ShapeDtypeStruct((2 * r2, cols), F32),
        compiler_params=_cparams("parallel"), name=name)(where, parts, parts, parts, parts, own)


def rs_share(totals, *, name):
    n = len(totals)

    def body(*refs):
        outs = refs[n:2 * n]
        send, recv = refs[2 * n:]
        x, y, c = _me()
        sib = (x, y, 1 - c)
        sent = []
        for i in range(n):
            mine = outs[i].at[_half(c, outs[i].shape[0])]
            sent.append(_remote(mine, mine, send.at[i], recv.at[i], sib))
            sent[-1].start()
        for i in range(n):
            other = outs[i].at[_half(1 - c, outs[i].shape[0])]
            _remote(other, other, send.at[i], recv.at[i], sib).wait_recv()
        for cp in sent:
            cp.wait_send()

    return pl.pallas_call(
        body, in_specs=[ANY] * n, out_specs=[ANY] * n,
        out_shape=[jax.ShapeDtypeStruct(t.shape, t.dtype) for t in totals],
        input_output_aliases={i: i for i in range(n)},
        scratch_shapes=[pltpu.SemaphoreType.DMA((n,)), pltpu.SemaphoreType.DMA((n,))],
        name=name)(*totals)


def small_allreduce(buf, *, name):
    rows = buf.shape[0]

    def body(x_ref, o_ref, slots, send, recv):
        x, y, c = _me()
        me = 4 * x + 2 * y + c
        slots[me] = x_ref[...]
        sent = []
        for d in range(1, 8):
            peer = (1 - x if d & 4 else x, 1 - y if d & 2 else y, 1 - c if d & 1 else c)
            sent.append(_remote(x_ref, slots.at[me], send.at[d - 1], recv.at[d - 1], peer))
            sent[-1].start()
        for cp in sent:
            cp.wait()
        acc = slots[0]
        for k in range(1, 8):
            acc = acc + slots[k]
        o_ref[...] = acc

    return pl.pallas_call(
        body, out_shape=jax.ShapeDtypeStruct(buf.shape, F32),
        in_specs=[pl.BlockSpec(memory_space=pltpu.VMEM)], out_specs=pl.BlockSpec(memory_space=pltpu.VMEM),
        scratch_shapes=[pltpu.VMEM((8, rows, 128), F32), pltpu.SemaphoreType.DMA((7,)), pltpu.SemaphoreType.DMA((7,))],
        name=name)(buf)


def adamw(w, g, m, v, *, name, rt=None):
    rows, cols = w.shape
    rt = rows if rt is None else rt
    c1 = 1.0 - ADAM_B1 ** ADAM_STEP
    c2 = 1.0 - ADAM_B2 ** ADAM_STEP

    def body(w_ref, g_ref, m_ref, v_ref, d_ref, nm_ref, nv_ref):
        gg = g_ref[...]
        nm = ADAM_B1 * m_ref[...] + (1.0 - ADAM_B1) * gg
        nv = ADAM_B2 * v_ref[...] + (1.0 - ADAM_B2) * (gg * gg)
        nm_ref[...] = nm
        nv_ref[...] = nv
        d_ref[...] = -ADAM_LR * ((nm / c1) / (jnp.sqrt(nv / c2) + ADAM_EPS) + ADAM_WD * w_ref[...])

    spec = pl.BlockSpec((rt, cols), lambda i: (i, 0))
    return pl.pallas_call(
        body, grid=(rows // rt,), in_specs=[spec] * 4, out_specs=[spec] * 3,
        out_shape=[jax.ShapeDtypeStruct((rows, cols), F32)] * 3,
        compiler_params=_cparams("parallel"), name=name)(w, g, m, v)


WEIGHTS = ['ffn1_pre_g', 'ffn1_w_gate', 'ffn1_w_up', 'ffn1_w_down', 'ffn1_post_g', 'mix_pre_g', 'w_in', 'conv_w',
           'conv_b', 'dt_bias', 'a_log', 'd_skip', 'ssm_norm_g', 'w_ssm_proj', 'attn_sinks', 'rel_bias_table',
           'w_attn_proj', 'w_out', 'mix_post_g', 'ffn2_pre_g', 'ffn2_w_gate', 'ffn2_w_up', 'ffn2_w_down', 'ffn2_post_g']
BIG = ['ffn1_w_gate', 'ffn1_w_up', 'ffn1_w_down', 'w_in', 'w_ssm_proj', 'w_attn_proj', 'w_out',
       'ffn2_w_gate', 'ffn2_w_up', 'ffn2_w_down']
SMALL = [w for w in WEIGHTS if w not in BIG]


def _bucket_onehot():
    blk = ATTN_BLOCK
    dist = np.maximum(np.arange(blk)[:, None] + blk - np.arange(2 * blk)[None, :], 0)
    max_exact = REL_BUCKETS // 2
    d = np.maximum(dist, 1).astype(np.float32)
    large = max_exact + (np.log(d / np.float32(max_exact)) / np.float32(math.log(REL_MAX_DISTANCE / max_exact))
                         * np.float32(REL_BUCKETS - max_exact)).astype(np.int32)
    bucket = np.where(dist < max_exact, dist, np.minimum(large, REL_BUCKETS - 1)).reshape(-1)
    return jnp.asarray((bucket[None, :] == np.arange(REL_BUCKETS)[:, None]).astype(np.float32))


def _pack_rows(parts, mult=8):
    flat = jnp.concatenate([p.reshape(-1).astype(F32) for p in parts])
    rows = -(-flat.shape[0] // (128 * mult)) * mult
    return jnp.pad(flat, (0, rows * 128 - flat.shape[0])).reshape(rows, 128)


def _unpack_rows(buf, shapes):
    flat = buf.reshape(-1)
    out, at = [], 0
    for shp in shapes:
        size = int(np.prod(shp))
        out.append(flat[at:at + size].reshape(shp))
        at += size
    return out


def kernel(x, ffn1_pre_g, ffn1_w_gate, ffn1_w_up, ffn1_w_down, ffn1_post_g, mix_pre_g, w_in, conv_w, conv_b, dt_bias, a_log, d_skip, ssm_norm_g, w_ssm_proj, attn_sinks, rel_bias_table, w_attn_proj, w_out, mix_post_g, ffn2_pre_g, ffn2_w_gate, ffn2_w_up, ffn2_w_down, ffn2_post_g, loss_target, m_ffn1_pre_g, m_ffn1_w_gate, m_ffn1_w_up, m_ffn1_w_down, m_ffn1_post_g, m_mix_pre_g, m_w_in, m_conv_w, m_conv_b, m_dt_bias, m_a_log, m_d_skip, m_ssm_norm_g, m_w_ssm_proj, m_attn_sinks, m_rel_bias_table, m_w_attn_proj, m_w_out, m_mix_post_g, m_ffn2_pre_g, m_ffn2_w_gate, m_ffn2_w_up, m_ffn2_w_down, m_ffn2_post_g, v_ffn1_pre_g, v_ffn1_w_gate, v_ffn1_w_up, v_ffn1_w_down, v_ffn1_post_g, v_mix_pre_g, v_w_in, v_conv_w, v_conv_b, v_dt_bias, v_a_log, v_d_skip, v_ssm_norm_g, v_w_ssm_proj, v_attn_sinks, v_rel_bias_table, v_w_attn_proj, v_w_out, v_mix_post_g, v_ffn2_pre_g, v_ffn2_w_gate, v_ffn2_w_up, v_ffn2_w_down, v_ffn2_post_g):
    args = locals()
    w = {n: args[n] for n in WEIGHTS}
    m = {n: args["m_" + n] for n in WEIGHTS}
    v = {n: args["v_" + n] for n in WEIGHTS}
    batch, seq, D = x.shape
    T = batch * seq
    xi, yi, ci = _me()
    s_me = 2 * xi + yi
    x2 = x.reshape(T, D)
    tgt = loss_target.reshape(T, D)

    def own_slot(parts):
        p = jnp.concatenate([t[0] for t in parts], axis=0).astype(BF16)
        return lax.dynamic_update_slice(lax.empty((N_SHARD,) + p.shape, BF16), p[None], (s_me, 0, 0))

    g704, gdn, gmix, gin = ag_weights(
        [own_slot([ffn1_w_gate, ffn1_w_up, ffn2_w_gate, ffn2_w_up]), own_slot([ffn1_w_down, ffn2_w_down]),
         own_slot([w_ssm_proj, w_attn_proj, w_out]), own_slot([w_in])], name="ag_weights")
    ffn1_parts, ffn2_parts = (0, 1, 0), (2, 3, 1)
    w_in_full = gin.transpose(1, 0, 2).reshape(D, IN_COLS)
    w_gz = w_in_full[:, 0:4096]
    w_xbc = w_in_full[:, 4096:4096 + SSM_CONV_DIM]
    w_dtT = w_in_full[:, 7168:7200].T
    w_qkv = w_in_full[:, 7200:]
    col = lambda v: v.reshape(SSM_HEADS, 1)
    d_skip_x = jnp.repeat(d_skip, SSM_HEAD_DIM, axis=1)
    cw_slot = lax.dynamic_update_slice(jnp.zeros((SSM_CONV, SSM_CONV_DIM), F32),
                                       conv_w[0] * (ci == 0).astype(F32), (0, s_me * (SSM_CONV_DIM // N_SHARD)))
    conv_w_full = small_allreduce(cw_slot.reshape(-1, 128), name="ag_conv_w").reshape(SSM_CONV, SSM_CONV_DIM)

    h1, n1, gate1, up1, f1 = ffn_fwd(x2, ffn1_pre_g, g704, gdn, ffn1_post_g, parts=ffn1_parts, name="ffn1_fwd")
    u, gates, z, xbc, dt_rawT, q, k, vv = mix_in_fwd(h1, mix_pre_g, w_gz, w_xbc, w_dtT, w_qkv, name="mix_in_fwd")
    y, ys, states = ssd_fwd(xbc, z, dt_rawT, conv_w_full, conv_b, col(dt_bias), col(a_log), d_skip_x, ssm_norm_g,
                            batch=batch, name="ssd_fwd")
    onehot = _bucket_onehot()
    bias = attn_bias(rel_bias_table.T, onehot, name="attn_bias").reshape(ATTN_Q_HEADS, ATTN_BLOCK, 2 * ATTN_BLOCK)
    o, lse = attn_fwd(q, k, vv, bias, attn_sinks, batch=batch, name="attn_fwd")
    h2, y_ssm, y_attn, mix, merged = mix_out_fwd(ys, o, gates, h1, gmix, mix_post_g, name="mix_out_fwd")
    h3, n3, gate2, up2, f2, dy, loss_parts = ffn_fwd(h2, ffn2_pre_g, g704, gdn, ffn2_post_g, tgt, parts=ffn2_parts,
                                                     name="ffn2_fwd")

    dh2, df2, a2, dgate2, dup2, dg_ffn2_pre, dg_ffn2_post = ffn_bwd(dy, h2, f2, gate2, up2, ffn2_pre_g, ffn2_post_g,
                                                                    g704, gdn, parts=ffn2_parts, name="ffn2_bwd")
    dmix, dyssm, dyattn, dgates, dys, do, dg_mix_post = mix_out_bwd(dh2, mix, y_ssm, y_attn, gates, gmix, mix_post_g,
                                                                    name="mix_out_bwd")
    dq, dk, dv, dbias, dsinks = attn_bwd(q, k, vv, o, do, lse, bias, attn_sinks, batch=batch, name="attn_bwd")
    dtable = attn_bias_bwd(dbias.reshape(ATTN_Q_HEADS, -1), onehot, name="attn_bias_bwd").T
    dz, dxs, dbm, dcm, ddtT, acc_xs, acc_bm, acc_cm, acc_head = ssd_bwd(
        dys, y, xbc, z, dt_rawT, states, conv_w_full, conv_b, col(dt_bias), col(a_log), d_skip_x, ssm_norm_g,
        batch=batch, name="ssd_bwd")
    dh1, dg_mix_pre = mix_in_bwd(dh2, h1, mix_pre_g, dgates, dz, dxs, dbm, dcm, ddtT, dq, dk, dv, w_gz, w_xbc, w_dtT,
                                 w_qkv, name="mix_in_bwd")
    dx, df1, a1, dgate1, dup1, dg_ffn1_pre, dg_ffn1_post = ffn_bwd(dh1, x2, f1, gate1, up1, ffn1_pre_g, ffn1_post_g,
                                                                   g704, gdn, parts=ffn1_parts, name="ffn1_bwd")

    d704 = lax.empty(g704.shape, F32)
    for part, (act, grad, nm) in enumerate([(n1, dgate1, "dw_gate1"), (n1, dup1, "dw_up1"),
                                            (n3, dgate2, "dw_gate2"), (n3, dup2, "dw_up2")]):
        d704 = mm_tn(act[None], grad, into=(d704, part), name=nm)
    ddn = mm_tn(a1, df1[None], into=(lax.empty(gdn.shape, F32), 0), name="dw_down1")
    ddn = mm_tn(a2, df2[None], into=(ddn, 1), name="dw_down2")
    dmx = mm_tn(ys[None], dyssm[None], a_cols=(N_SHARD, 512), into=(lax.empty(gmix.shape, F32), 0), name="dw_ssm")
    dmx = mm_tn(o[None], dyattn[None], a_cols=(N_SHARD, 256), into=(dmx, 2), name="dw_attn")
    dmx = mm_tn(merged[None], dmix[None], a_cols=(N_SHARD, 256), into=(dmx, 3), name="dw_out")
    ub = u[None]
    din = jnp.concatenate([
        mm_tn(ub, dgates[None], name="dw_in_gates", tn=1024)[0], mm_tn(ub, dz[None], name="dw_in_z", tn=1024)[0],
        mm_tn(ub, dxs[None], name="dw_in_xs", tn=1024)[0], mm_tn(ub, dbm[None], name="dw_in_b")[0],
        mm_tn(ub, dcm[None], name="dw_in_c")[0], mm_rows(ddtT, u, name="dw_in_dt").T,
        mm_tn(ub, dq[None], name="dw_in_q")[0], mm_tn(ub, dk[None], name="dw_in_k")[0],
        mm_tn(ub, dv[None], name="dw_in_v")[0]], axis=1)
    din = din.reshape(D, N_SHARD, IN_COLS // N_SHARD).transpose(1, 0, 2)

    grads = [d704, ddn, dmx, din]
    tiles = [512, 352, 256, 256]
    where = jnp.stack([s_me, ci]).astype(jnp.int32)
    pair = rs_pair(grads, name="rs_pair")
    sums = [rs_add(g, p, where, rt=rt, name=f"rs_add{i}") for i, (g, p, rt) in enumerate(zip(grads, pair, tiles))]
    parts = rs_chips(sums, name="rs_chips")
    totals = [rs_total(p, s, where, rt=rt, name=f"rs_total{i}")
              for i, (p, s, rt) in enumerate(zip(parts, sums, tiles))]
    r704, rdn, rmx, rin = rs_share(totals, name="rs_share")
    FS = D_FF // N_SHARD
    gw = {
        'ffn1_w_gate': r704[0:D], 'ffn1_w_up': r704[D:2 * D], 'ffn2_w_gate': r704[2 * D:3 * D], 'ffn2_w_up': r704[3 * D:],
        'ffn1_w_down': rdn[0:FS], 'ffn2_w_down': rdn[FS:],
        'w_ssm_proj': rmx[0:512], 'w_attn_proj': rmx[512:768], 'w_out': rmx[768:1024], 'w_in': rin,
    }

    dconv_w = jnp.concatenate([acc[:, :SSM_CONV].transpose(1, 0, 2).reshape(SSM_CONV, -1)
                               for acc in (acc_xs, acc_bm, acc_cm)], axis=1)
    dconv_b = jnp.concatenate([acc[:, SSM_CONV].reshape(-1) for acc in (acc_xs, acc_bm, acc_cm)])
    small_local = {
        'ffn1_pre_g': dg_ffn1_pre, 'ffn1_post_g': dg_ffn1_post, 'mix_pre_g': dg_mix_pre, 'conv_w': dconv_w,
        'conv_b': dconv_b, 'dt_bias': acc_head[:, :, 0], 'a_log': acc_head[:, :, 1],
        'd_skip': acc_xs[:, SSM_CONV + 2].reshape(SSM_HEADS, SSM_HEAD_DIM).sum(axis=1),
        'ssm_norm_g': acc_xs[:, SSM_CONV + 1].reshape(-1), 'attn_sinks': dsinks, 'rel_bias_table': dtable,
        'mix_post_g': dg_mix_post, 'ffn2_pre_g': dg_ffn2_pre, 'ffn2_post_g': dg_ffn2_post,
    }
    full_shapes = [(SSM_CONV, SSM_CONV_DIM) if n == 'conv_w' else w[n].shape for n in SMALL]
    packed = _pack_rows([small_local[n] for n in SMALL] + [jnp.sum(loss_parts[:, 0, 0])])
    total = small_allreduce(packed, name="allreduce_small")
    *small_g, loss = _unpack_rows(total, full_shapes + [()])
    for n, g in zip(SMALL, small_g):
        gw[n] = g
    gw['conv_w'] = lax.dynamic_slice(gw['conv_w'], (0, s_me * (SSM_CONV_DIM // N_SHARD)),
                                     (SSM_CONV, SSM_CONV_DIM // N_SHARD))[None]

    delta, new_m, new_v = {}, {}, {}
    for n in BIG:
        rows = w[n].shape[1]
        d_, m_, v_ = adamw(w[n][0], gw[n], m[n][0], v[n][0], name="adamw_" + n, rt=rows // 4)
        gw[n] = gw[n][None]
        delta[n], new_m[n], new_v[n] = d_[None], m_[None], v_[None]
    shapes = [w[n].shape for n in SMALL]
    outs = adamw(_pack_rows([w[n] for n in SMALL]), _pack_rows([gw[n] for n in SMALL]),
                 _pack_rows([m[n] for n in SMALL]), _pack_rows([v[n] for n in SMALL]), name="adamw_small")
    for res, buf in zip((delta, new_m, new_v), outs):
        for n, val in zip(SMALL, _unpack_rows(buf, shapes)):
            res[n] = val
    return (loss, dx.reshape(batch, seq, D), *[gw[n].reshape(w[n].shape) for n in WEIGHTS],
            *[delta[n] for n in WEIGHTS], *[new_m[n] for n in WEIGHTS], *[new_v[n] for n in WEIGHTS])
```

```python
import functools
import math

import jax
import jax.numpy as jnp
import numpy as np
from jax import lax
from jax.experimental import pallas as pl
from jax.experimental.pallas import tpu as pltpu

F32 = jnp.float32
BF16 = jnp.bfloat16

D_MODEL = 1024
D_FF = 2816
N_SHARD = 4
SSM_D_INNER = 2048
SSM_HEAD_DIM = 64
SSM_HEADS = 32
SSM_GROUPS = 4
SSM_HPG = SSM_HEADS // SSM_GROUPS
SSM_GW = SSM_D_INNER // SSM_GROUPS
SSM_STATE = 128
SSM_CONV = 4
SSM_CHUNK = 128
SSM_CONV_DIM = SSM_D_INNER + 2 * SSM_GROUPS * SSM_STATE
ATTN_Q_HEADS = 16
ATTN_KV_HEADS = 4
ATTN_REP = ATTN_Q_HEADS // ATTN_KV_HEADS
ATTN_HEAD_DIM = 64
ATTN_BLOCK = 128
ATTN_Q_DIM = 1024
ATTN_KV_DIM = 256
REL_BUCKETS = 32
REL_MAX_DISTANCE = 128
RMS_EPS = 1e-6
IN_COLS = 8736
ADAM_LR = 0.001
ADAM_B1 = 0.9
ADAM_B2 = 0.999
ADAM_EPS = 1e-08
ADAM_WD = 0.01
ADAM_STEP = 10
HALO = 8

VMEM_LIMIT = 56 * 1024 * 1024


def _cparams(*sem):
    return pltpu.CompilerParams(dimension_semantics=tuple(sem) if sem else None, vmem_limit_bytes=VMEM_LIMIT)


def _dot(a, b):
    return jnp.dot(a, b, preferred_element_type=F32)


def _dot_nt(a, b):
    return lax.dot_general(a, b, (((1,), (1,)), ((), ())), preferred_element_type=F32)


def _dot_tn(a, b):
    return lax.dot_general(a, b, (((0,), (0,)), ((), ())), preferred_element_type=F32)


def _dot_hi(a, b):
    return jnp.dot(a, b, preferred_element_type=F32, precision=lax.Precision.HIGHEST)


def _sigmoid(x):
    return 1.0 / (1.0 + jnp.exp(-x))


def _resident(shape, index=None):
    index = (0,) * len(shape) if index is None else tuple(index)
    return pl.BlockSpec(shape, lambda *_: index, pipeline_mode=pl.Buffered(1))


def _part(packed, rows, part):
    return _resident((N_SHARD, rows, packed.shape[2]), (0, part, 0))


def _rows(tm, width):
    return pl.BlockSpec((tm, width), lambda i: (i, 0))


class Ride:
    def __init__(self, inputs, out_shapes, aliases, scratch, start, finish):
        self.inputs, self.out_shapes, self.aliases = list(inputs), list(out_shapes), list(aliases)
        self.scratch, self.start, self.finish = list(scratch), start, finish


def _call(body, *, grid, in_specs, args, out_specs, out_shape, name, sem, scratch=(), aliases=None, ride=None):
    aliases = dict(aliases or {})
    if ride is None:
        return pl.pallas_call(body, grid=grid, in_specs=in_specs, out_specs=out_specs, out_shape=out_shape,
                              scratch_shapes=list(scratch), input_output_aliases=aliases,
                              compiler_params=_cparams(*sem), name=name)(*args)
    n_in, n_out, n_scr = len(in_specs), len(out_specs), len(scratch)
    k_in, k_out = len(ride.inputs), len(ride.out_shapes)

    def riding(*refs):
        ins, refs = refs[:n_in], refs[n_in:]
        ex_in, refs = refs[:k_in], refs[k_in:]
        outs, refs = refs[:n_out], refs[n_out:]
        ex_out, refs = refs[:k_out], refs[k_out:]
        scr, ex_scr = refs[:n_scr], refs[n_scr:]
        first = functools.reduce(jnp.logical_and, [pl.program_id(a) == 0 for a in range(len(grid))])
        last = functools.reduce(jnp.logical_and, [pl.program_id(a) == grid[a] - 1 for a in range(len(grid))])

        @pl.when(first)
        def _():
            ride.start(ex_in, ex_out, ex_scr)

        body(*ins, *outs, *scr)

        @pl.when(last)
        def _():
            ride.finish(ex_in, ex_out, ex_scr)

    aliases.update({n_in + i: n_out + j for i, j in ride.aliases})
    res = pl.pallas_call(
        riding, grid=grid, in_specs=list(in_specs) + [ANY] * k_in, out_specs=list(out_specs) + [ANY] * k_out,
        out_shape=list(out_shape) + ride.out_shapes, scratch_shapes=list(scratch) + ride.scratch,
        input_output_aliases=aliases, compiler_params=_cparams(*["arbitrary"] * len(grid)), name=name,
    )(*args, *ride.inputs)
    return res[:n_out], res[n_out:]


def run_ride(ride, *, name):
    k_in = len(ride.inputs)

    def body(*refs):
        ex_in, ex_out, sems = refs[:k_in], refs[k_in:k_in + len(ride.out_shapes)], refs[k_in + len(ride.out_shapes):]
        ride.start(ex_in, ex_out, sems)
        ride.finish(ex_in, ex_out, sems)

    return pl.pallas_call(body, in_specs=[ANY] * k_in, out_specs=[ANY] * len(ride.out_shapes),
                          out_shape=ride.out_shapes, scratch_shapes=ride.scratch,
                          input_output_aliases=dict(ride.aliases), name=name)(*ride.inputs)


def ffn_fwd(h, g_pre, w704, wdn, g_post, target=None, *, parts, name, tm=512, ride=None):
    T, D = h.shape
    NS, FS = N_SHARD, w704.shape[2]
    with_loss = target is not None
    nt = T // tm

    def body(*refs):
        if with_loss:
            (h_ref, gpre_ref, wg_ref, wu_ref, wd_ref, gpost_ref, tgt_ref,
             hout_ref, n_ref, gate_ref, up_ref, f_ref, dy_ref, loss_ref) = refs
        else:
            (h_ref, gpre_ref, wg_ref, wu_ref, wd_ref, gpost_ref,
             hout_ref, n_ref, gate_ref, up_ref, f_ref) = refs
        hh = h_ref[...]
        r = lax.rsqrt(jnp.mean(hh * hh, axis=-1, keepdims=True) + RMS_EPS)
        n = (hh * r * gpre_ref[...]).astype(BF16)
        n_ref[...] = n
        acc = jnp.zeros((tm, D), F32)
        for s in range(NS):
            gate = _dot(n, wg_ref[s])
            up = _dot(n, wu_ref[s])
            gate_ref[s] = gate.astype(BF16)
            up_ref[s] = up.astype(BF16)
            a = (gate * _sigmoid(gate) * up).astype(BF16)
            acc = acc + _dot(a, wd_ref[s])
        f_ref[...] = acc
        r2 = lax.rsqrt(jnp.mean(acc * acc, axis=-1, keepdims=True) + RMS_EPS)
        out = hh + 0.5 * (acc * r2 * gpost_ref[...])
        hout_ref[...] = out
        if with_loss:
            e = out - tgt_ref[...]
            dy_ref[...] = e * (1.0 / D)
            loss_ref[...] = jnp.full((1, 8, 128), 0.5 / D, F32) * jnp.sum(e * e)

    in_specs = [_rows(tm, D), _resident((1, D)), _part(w704, D, parts[0]), _part(w704, D, parts[1]),
                _part(wdn, FS, parts[2]), _resident((1, D))]
    args = [h, g_pre, w704, w704, wdn, g_post]
    out_shape = [jax.ShapeDtypeStruct((T, D), F32), jax.ShapeDtypeStruct((T, D), BF16),
                 jax.ShapeDtypeStruct((NS, T, FS), BF16), jax.ShapeDtypeStruct((NS, T, FS), BF16),
                 jax.ShapeDtypeStruct((T, D), F32)]
    seg = pl.BlockSpec((NS, tm, FS), lambda i: (0, i, 0))
    out_specs = [_rows(tm, D), _rows(tm, D), seg, seg, _rows(tm, D)]
    if with_loss:
        in_specs.append(_rows(tm, D))
        args.append(target)
        out_shape += [jax.ShapeDtypeStruct((T, D), F32), jax.ShapeDtypeStruct((nt, 8, 128), F32)]
        out_specs += [_rows(tm, D), pl.BlockSpec((1, 8, 128), lambda i: (i, 0, 0))]
    return _call(body, grid=(nt,), in_specs=in_specs, args=args, out_specs=out_specs, out_shape=out_shape,
                 sem=("parallel",), name=name, ride=ride)


def ffn_bwd(dout, h, f, gate, up, g_pre, g_post, w704, wdn, *, parts, name, tm=256, ride=None):
    T, D = h.shape
    NS, FS = N_SHARD, w704.shape[2]
    nt = T // tm

    def body(dout_ref, h_ref, f_ref, gate_ref, up_ref, gpre_ref, gpost_ref, wg_ref, wu_ref, wd_ref,
             dh_ref, df_ref, a_ref, dgate_ref, dup_ref, dgpre_ref, dgpost_ref):
        @pl.when(pl.program_id(0) == 0)
        def _():
            dgpre_ref[...] = jnp.zeros_like(dgpre_ref)
            dgpost_ref[...] = jnp.zeros_like(dgpost_ref)

        do = dout_ref[...]
        ff = f_ref[...]
        d_fn = 0.5 * do
        r2 = lax.rsqrt(jnp.mean(ff * ff, axis=-1, keepdims=True) + RMS_EPS)
        dgpost_ref[...] += jnp.sum(d_fn * ff * r2, axis=0, keepdims=True)
        t = d_fn * gpost_ref[...]
        df = r2 * t - ff * (r2 * r2 * r2 * jnp.mean(t * ff, axis=-1, keepdims=True))
        dfb = df.astype(BF16)
        df_ref[...] = dfb
        dn = jnp.zeros((tm, D), F32)
        for s in range(NS):
            da = _dot_nt(dfb, wd_ref[s])
            g = gate_ref[s].astype(F32)
            u = up_ref[s].astype(F32)
            sg = _sigmoid(g)
            silu = g * sg
            a_ref[s] = (silu * u).astype(BF16)
            dgt = (da * u * (sg * (1.0 + g * (1.0 - sg)))).astype(BF16)
            dupv = (da * silu).astype(BF16)
            dgate_ref[s] = dgt
            dup_ref[s] = dupv
            dn = dn + _dot_nt(dgt, wg_ref[s]) + _dot_nt(dupv, wu_ref[s])
        hh = h_ref[...]
        r1 = lax.rsqrt(jnp.mean(hh * hh, axis=-1, keepdims=True) + RMS_EPS)
        dgpre_ref[...] += jnp.sum(dn * hh * r1, axis=0, keepdims=True)
        t = dn * gpre_ref[...]
        dh_ref[...] = do + r1 * t - hh * (r1 * r1 * r1 * jnp.mean(t * hh, axis=-1, keepdims=True))

    seg = pl.BlockSpec((NS, tm, FS), lambda i: (0, i, 0))
    acc = pl.BlockSpec((1, D), lambda i: (0, 0))
    return _call(
        body, grid=(nt,),
        in_specs=[_rows(tm, D), _rows(tm, D), _rows(tm, D), seg, seg, _resident((1, D)), _resident((1, D)),
                  _part(w704, D, parts[0]), _part(w704, D, parts[1]), _part(wdn, FS, parts[2])],
        args=[dout, h, f, gate, up, g_pre, g_post, w704, w704, wdn],
        out_specs=[_rows(tm, D), _rows(tm, D), seg, seg, seg, acc, acc],
        out_shape=[jax.ShapeDtypeStruct((T, D), F32), jax.ShapeDtypeStruct((T, D), BF16),
                   jax.ShapeDtypeStruct((NS, T, FS), BF16), jax.ShapeDtypeStruct((NS, T, FS), BF16),
                   jax.ShapeDtypeStruct((NS, T, FS), BF16),
                   jax.ShapeDtypeStruct((1, D), F32), jax.ShapeDtypeStruct((1, D), F32)],
        sem=("arbitrary",), name=name, ride=ride)


def mm_tn(a, g, *, name, tt=1024, tn=None, a_cols=None, into=None):
    Ba, T, _ = a.shape
    Bg, _, N = g.shape
    B, K = a_cols if a_cols else (max(Ba, Bg), a.shape[2])
    tn = N if tn is None else tn
    tt = min(tt, T)
    nsteps = T // tt

    def body(*refs):
        a_ref, g_ref, o_ref = refs[0], refs[1], refs[-1]

        @pl.when(pl.program_id(2) == 0)
        def _():
            o_ref[...] = jnp.zeros_like(o_ref)

        o_ref[0] += _dot_tn(a_ref[0], g_ref[0].astype(BF16))

    if a_cols:
        a_map = lambda b, j, t: (0, t, b)
    else:
        a_map = (lambda b, j, t: (b, t, 0)) if Ba > 1 else (lambda b, j, t: (0, t, 0))
    in_specs = [pl.BlockSpec((1, tt, K), a_map),
                pl.BlockSpec((1, tt, tn), (lambda b, j, t: (b, t, j)) if Bg > 1 else (lambda b, j, t: (0, t, j)))]
    args = [a, g]
    if into is None:
        out_shape, part, aliases = jax.ShapeDtypeStruct((B, K, N), F32), 0, {}
    else:
        buf, part = into
        out_shape, aliases = jax.ShapeDtypeStruct(buf.shape, F32), {2: 0}
        in_specs.append(ANY)
        args.append(buf)
    return pl.pallas_call(
        body, grid=(B, N // tn, nsteps), in_specs=in_specs,
        out_specs=pl.BlockSpec((1, K, tn), lambda b, j, t: (b, part, j)),
        out_shape=out_shape, input_output_aliases=aliases,
        compiler_params=_cparams("parallel", "parallel", "arbitrary"), name=name)(*args)


def mix_in_fwd(h, g, w_gz, w_xbc, w_dtT, w_qkv, *, name, tm=256, ride=None):
    T, D = h.shape
    nt = T // tm
    CB = 1024

    def body(h_ref, g_ref, wgz_ref, wxbc_ref, wdtT_ref, wqkv_ref,
             u_ref, gates_ref, z_ref, xbc_ref, dtT_ref, q_ref, k_ref, v_ref):
        hh = h_ref[...]
        r = lax.rsqrt(jnp.mean(hh * hh, axis=-1, keepdims=True) + RMS_EPS)
        u = (hh * r * g_ref[...]).astype(BF16)
        u_ref[...] = u
        for cb in range(0, 2048, CB):
            gates_ref[:, cb:cb + CB] = _dot(u, wgz_ref[:, cb:cb + CB])
            z_ref[:, cb:cb + CB] = _dot(u, wgz_ref[:, 2048 + cb:2048 + cb + CB])
        for cb in range(0, SSM_CONV_DIM, CB):
            xbc_ref[:, cb:cb + CB] = _dot(u, wxbc_ref[:, cb:cb + CB])
        dtT_ref[...] = _dot_nt(wdtT_ref[...], u)
        q_ref[...] = _dot(u, wqkv_ref[:, 0:ATTN_Q_DIM]).astype(BF16)
        k_ref[...] = _dot(u, wqkv_ref[:, ATTN_Q_DIM:ATTN_Q_DIM + ATTN_KV_DIM]).astype(BF16)
        v_ref[...] = _dot(u, wqkv_ref[:, ATTN_Q_DIM + ATTN_KV_DIM:]).astype(BF16)

    sds = jax.ShapeDtypeStruct
    return _call(
        body, grid=(nt,),
        in_specs=[_rows(tm, D), _resident((1, D)), _resident(w_gz.shape), _resident(w_xbc.shape),
                  _resident(w_dtT.shape), _resident(w_qkv.shape)],
        args=[h, g, w_gz, w_xbc, w_dtT, w_qkv],
        out_specs=[_rows(tm, D), _rows(tm, 2048), _rows(tm, 2048), _rows(tm, SSM_CONV_DIM),
                   pl.BlockSpec((SSM_HEADS, tm), lambda i: (0, i)),
                   _rows(tm, ATTN_Q_DIM), _rows(tm, ATTN_KV_DIM), _rows(tm, ATTN_KV_DIM)],
        out_shape=[sds((T, D), BF16), sds((T, 2048), F32), sds((T, 2048), F32), sds((T, SSM_CONV_DIM), F32),
                   sds((SSM_HEADS, T), F32),
                   sds((T, ATTN_Q_DIM), BF16), sds((T, ATTN_KV_DIM), BF16), sds((T, ATTN_KV_DIM), BF16)],
        sem=("parallel",), name=name, ride=ride)


def _softplus(x):
    return jnp.maximum(x, 0.0) + jnp.log(1.0 + jnp.exp(-jnp.abs(x)))


def _iota(shape, axis):
    return lax.broadcasted_iota(jnp.int32, shape, axis)


def _head_expand(g, per_head):
    shape = (SSM_HEADS, SSM_HPG * per_head)
    head = lax.shift_right_logical(_iota(shape, 1), int(math.log2(per_head)))
    return (_iota(shape, 0) == SSM_HPG * g + head).astype(F32)


def _conv_pre(x_ref, halo_ref, w_ref, b_ref, xp_ref, first):
    Q = SSM_CHUNK
    halo = jnp.where(first, 0.0, halo_ref[...])
    xp_ref[0:HALO, :] = halo
    xp_ref[HALO:HALO + Q, :] = x_ref[...]
    pre = b_ref[...] + w_ref[3:4, :] * xp_ref[HALO:HALO + Q, :]
    for k in range(SSM_CONV - 1):
        pre = pre + w_ref[k:k + 1, :] * xp_ref[pl.ds(HALO - 3 + k, Q), :]
    return pre


def _ssd_specs(nc):
    Q, GW, N = SSM_CHUNK, SSM_GW, SSM_STATE
    nb_xs = SSM_D_INNER // N
    nb_c = nb_xs + SSM_GROUPS

    def rb(cmap):
        def row(b, c, g):
            return b * nc + cmap(c)
        return row

    def specs(cmap):
        row = rb(cmap)
        hrow = lambda b, c, g: jnp.maximum(row(b, c, g) * (Q // HALO) - 1, 0)
        return dict(
            xs=pl.BlockSpec((Q, GW), lambda b, c, g: (row(b, c, g), g)),
            bm=pl.BlockSpec((Q, N), lambda b, c, g: (row(b, c, g), nb_xs + g)),
            cm=pl.BlockSpec((Q, N), lambda b, c, g: (row(b, c, g), nb_c + g)),
            xs_halo=pl.BlockSpec((HALO, GW), lambda b, c, g: (hrow(b, c, g), g)),
            bm_halo=pl.BlockSpec((HALO, N), lambda b, c, g: (hrow(b, c, g), nb_xs + g)),
            cm_halo=pl.BlockSpec((HALO, N), lambda b, c, g: (hrow(b, c, g), nb_c + g)),
            grp=pl.BlockSpec((Q, GW), lambda b, c, g: (row(b, c, g), g)),
            dt=pl.BlockSpec((Q, SSM_HEADS), lambda b, c, g: (row(b, c, g), 0)),
            dtT=pl.BlockSpec((SSM_HEADS, Q), lambda b, c, g: (0, row(b, c, g))),
            w_xs=pl.BlockSpec((SSM_CONV, GW), lambda b, c, g: (0, g)),
            w_bm=pl.BlockSpec((SSM_CONV, N), lambda b, c, g: (0, nb_xs + g)),
            w_cm=pl.BlockSpec((SSM_CONV, N), lambda b, c, g: (0, nb_c + g)),
            b_xs=pl.BlockSpec((1, GW), lambda b, c, g: (0, g)),
            b_bm=pl.BlockSpec((1, N), lambda b, c, g: (0, nb_xs + g)),
            b_cm=pl.BlockSpec((1, N), lambda b, c, g: (0, nb_c + g)),
            vec_g=pl.BlockSpec((1, GW), lambda b, c, g: (0, g)),
            row32=pl.BlockSpec((1, SSM_HEADS), lambda b, c, g: (0, 0)),
            col32=pl.BlockSpec((SSM_HEADS, 1), lambda b, c, g: (0, 0)),
            state=pl.BlockSpec((1, SSM_HPG, SSM_HEAD_DIM, N), lambda b, c, g: (row(b, c, g), g, 0, 0)),
            dtT_g=pl.BlockSpec((SSM_HPG, Q), lambda b, c, g: (g, row(b, c, g))),
            col_g=pl.BlockSpec((SSM_HPG, 1), lambda b, c, g: (g, 0)),
            pairs=pl.BlockSpec((1, SSM_HPG // 2, N, 2 * SSM_HEAD_DIM), lambda b, c, g: (row(b, c, g), g, 0, 0)),
        )
    return specs


def _ssd_chunk_common(first, g, xs_ref, bm_ref, cm_ref, xs_halo, bm_halo, cm_halo, w_xs, w_bm, w_cm, b_xs, b_bm, b_cm,
                      dt_ref, dtT_ref, dtb_ref, dtbT_ref, alog_ref, alogT_ref, xp_xs, xp_bm, xp_cm):
    Q = SSM_CHUNK
    pre_xs = _conv_pre(xs_ref, xs_halo, w_xs, b_xs, xp_xs, first)
    pre_bm = _conv_pre(bm_ref, bm_halo, w_bm, b_bm, xp_bm, first)
    pre_cm = _conv_pre(cm_ref, cm_halo, w_cm, b_cm, xp_cm, first)
    xs = pre_xs * _sigmoid(pre_xs)
    bm = pre_bm * _sigmoid(pre_bm)
    cm = pre_cm * _sigmoid(pre_cm)
    dtr = dt_ref[...] + dtb_ref[...]
    dtrT = dtT_ref[...] + dtbT_ref[...]
    dt = _softplus(dtr)
    dtT = _softplus(dtrT)
    a = -jnp.exp(alog_ref[...])
    aT = -jnp.exp(alogT_ref[...])
    tri = (_iota((Q, Q), 0) >= _iota((Q, Q), 1)).astype(F32)
    triT = (_iota((Q, Q), 0) <= _iota((Q, Q), 1)).astype(F32)
    acs = _dot_hi(tri, dt * a)
    acsT = _dot_hi(dtT * aT, triT)
    return dict(pre_xs=pre_xs, pre_bm=pre_bm, pre_cm=pre_cm, xs=xs, bm=bm, cm=cm, dtr=dtr, dt=dt, a=a,
                acs=acs, acsT=acsT, tri=tri, triT=triT)


def _v1_ssd_fwd(xbc, z, dt_raw, dt_rawT, conv_w, conv_b, dt_bias, a_log, d_skip, norm_g, *, batch, name):
    T = xbc.shape[0]
    Q, GW, N, P, HPG = SSM_CHUNK, SSM_GW, SSM_STATE, SSM_HEAD_DIM, SSM_HPG
    nc = T // batch // Q
    sp = _ssd_specs(nc)(lambda c: c)

    def body(xs_ref, bm_ref, cm_ref, xs_halo, bm_halo, cm_halo, z_ref, dt_ref, dtT_ref,
             w_xs, w_bm, w_cm, b_xs, b_bm, b_cm, dtb_ref, dtbT_ref, alog_ref, alogT_ref, dskip_ref, ng_ref,
             y_ref, ys_ref, st_ref, state, acsT_s, y_s, xp_xs, xp_bm, xp_cm):
        c = pl.program_id(1)
        g = pl.program_id(2)
        first = c == 0
        cc = _ssd_chunk_common(first, g, xs_ref, bm_ref, cm_ref, xs_halo, bm_halo, cm_halo, w_xs, w_bm, w_cm,
                               b_xs, b_bm, b_cm, dt_ref, dtT_ref, dtb_ref, dtbT_ref, alog_ref, alogT_ref,
                               xp_xs, xp_bm, xp_cm)
        xs, acs = cc["xs"], cc["acs"]
        acsT_s[...] = cc["acsT"]
        e64 = _head_expand(g, P)
        e128 = _head_expand(g, Q)
        acs_x = _dot_hi(acs, e64)
        acs_b = _dot_hi(acs, e128)
        x = xs * _dot_hi(cc["dt"], e64)
        last_x = acs_x[Q - 1:Q, :]
        xw = (x * jnp.exp(last_x - acs_x)).astype(BF16)
        ex = jnp.exp(acs_x)
        xb = x.astype(BF16)
        bb = cc["bm"].astype(BF16)
        cb = cc["cm"].astype(BF16)
        s = _dot_nt(cb, bb)
        causal = _iota((Q, Q), 0) >= _iota((Q, Q), 1)
        for r in range(HPG):
            hd = HPG * g + r

            @pl.when(first)
            def _():
                state[hd] = jnp.zeros((P, N), F32)

            seg = acs_b[:, Q * r:Q * (r + 1)] - acsT_s[pl.ds(hd, 1), :]
            m = (s * jnp.exp(jnp.where(causal, seg, -1e30))).astype(BF16)
            hp = state[hd]
            st_ref[0, r] = hp
            y_h = _dot(m, xb[:, P * r:P * (r + 1)]) + _dot_nt(cb, hp.astype(BF16)) * ex[:, P * r:P * (r + 1)]
            y_s[:, P * r:P * (r + 1)] = y_h
            decay = jnp.exp(acsT_s[pl.ds(hd, 1), pl.ds(Q - 1, 1)])
            state[hd] = hp * decay + _dot_tn(xw[:, P * r:P * (r + 1)], bb)
        dexp = _dot_hi(jnp.broadcast_to(dskip_ref[...], (8, SSM_HEADS)), e64)[0:1, :]
        y = y_s[...] + dexp * xs
        y_ref[...] = y
        zz = z_ref[...]
        yg = y * (zz * _sigmoid(zz))
        rr = lax.rsqrt(jnp.mean(yg * yg, axis=-1, keepdims=True) + RMS_EPS)
        ys_ref[...] = (yg * rr * ng_ref[...]).astype(BF16)

    col = lambda v: v.reshape(SSM_HEADS, 1)
    sds = jax.ShapeDtypeStruct
    return pl.pallas_call(
        body, grid=(batch, nc, SSM_GROUPS),
        in_specs=[sp["xs"], sp["bm"], sp["cm"], sp["xs_halo"], sp["bm_halo"], sp["cm_halo"], sp["grp"], sp["dt"],
                  sp["dtT"], sp["w_xs"], sp["w_bm"], sp["w_cm"], sp["b_xs"], sp["b_bm"], sp["b_cm"],
                  sp["row32"], sp["col32"], sp["row32"], sp["col32"], sp["row32"], sp["vec_g"]],
        out_specs=[sp["grp"], sp["grp"], sp["state"]],
        out_shape=[sds((T, SSM_D_INNER), F32), sds((T, SSM_D_INNER), BF16),
                   sds((T // Q, SSM_HEADS, P, N), F32)],
        scratch_shapes=[pltpu.VMEM((SSM_HEADS, P, N), F32), pltpu.VMEM((SSM_HEADS, Q), F32), pltpu.VMEM((Q, GW), F32),
                        pltpu.VMEM((HALO + Q, GW), F32), pltpu.VMEM((HALO + Q, N), F32), pltpu.VMEM((HALO + Q, N), F32)],
        compiler_params=_cparams("arbitrary", "arbitrary", "arbitrary"), name=name,
    )(xbc, xbc, xbc, xbc, xbc, xbc, z, dt_raw, dt_rawT, conv_w, conv_w, conv_w, conv_b, conv_b, conv_b,
      dt_bias, col(dt_bias), a_log, col(a_log), d_skip, norm_g)


def _attn_specs(nb):
    BLK = ATTN_BLOCK

    def specs(last):
        def cur(b, n):
            return b * nb + (n if last is None else jnp.minimum(n, nb - 1))

        def prev(b, n):
            return b * nb + jnp.maximum((n if last is None else jnp.minimum(n, nb - 1)) - 1, 0)
        return cur, prev
    return specs


def _attn_masks(n):
    shape = (ATTN_REP * ATTN_BLOCK, ATTN_BLOCK)
    ii = jnp.bitwise_and(_iota(shape, 0), ATTN_BLOCK - 1)
    jj = _iota(shape, 1)
    return jnp.logical_and(jj > ii, n > 0), jj <= ii


def _attn_group(kk, q_ref, bias_ref, sink_ref):
    BLK, HD = ATTN_BLOCK, ATTN_HEAD_DIM
    heads = range(ATTN_REP * kk, ATTN_REP * (kk + 1))
    qg = jnp.concatenate([q_ref[:, HD * hd:HD * (hd + 1)] for hd in heads], axis=0)
    bias_p = jnp.concatenate([bias_ref[hd, :, 0:BLK] for hd in heads], axis=0)
    bias_c = jnp.concatenate([bias_ref[hd, :, BLK:2 * BLK] for hd in heads], axis=0)
    sink = jnp.concatenate([jnp.broadcast_to(sink_ref[0:1, hd:hd + 1], (BLK, 1)) for hd in heads], axis=0)
    return qg, bias_p, bias_c, sink


def attn_bias(table_t, onehot, *, name):
    def body(t_ref, f_ref, o_ref):
        o_ref[...] = _dot_hi(t_ref[...], f_ref[...])
    return pl.pallas_call(body, out_shape=jax.ShapeDtypeStruct((ATTN_Q_HEADS, onehot.shape[1]), F32),
                          compiler_params=_cparams(), name=name)(table_t, onehot)


def attn_bias_bwd(dbias, onehot, *, name):
    def body(d_ref, f_ref, o_ref):
        o_ref[...] = lax.dot_general(d_ref[...], f_ref[...], (((1,), (1,)), ((), ())), preferred_element_type=F32,
                                     precision=lax.Precision.HIGHEST)
    return pl.pallas_call(body, out_shape=jax.ShapeDtypeStruct((ATTN_Q_HEADS, REL_BUCKETS), F32),
                          compiler_params=_cparams(), name=name)(dbias, onehot)


def attn_fwd(q, k, v, bias, sinks, *, batch, name):
    T = q.shape[0]
    BLK, HD = ATTN_BLOCK, ATTN_HEAD_DIM
    nb = T // batch // BLK
    cur, prev = _attn_specs(nb)(None)
    scale = HD ** -0.5

    def body(q_ref, kc_ref, kp_ref, vc_ref, vp_ref, bias_ref, sink_ref, o_ref, lse_ref):
        n = pl.program_id(1)
        m_prev, m_cur = _attn_masks(n)
        for kk in range(ATTN_KV_HEADS):
            ks = slice(HD * kk, HD * (kk + 1))
            kc, kp, vc, vp = kc_ref[:, ks], kp_ref[:, ks], vc_ref[:, ks], vp_ref[:, ks]
            qg, bias_p, bias_c, sink = _attn_group(kk, q_ref, bias_ref, sink_ref)
            lp = jnp.where(m_prev, _dot_nt(qg, kp) * scale + bias_p, -1e30)
            lc = jnp.where(m_cur, _dot_nt(qg, kc) * scale + bias_c, -1e30)
            mx = jnp.maximum(jnp.max(jnp.maximum(lp, lc), axis=-1, keepdims=True), sink)
            pp = jnp.exp(lp - mx)
            pc = jnp.exp(lc - mx)
            den = jnp.sum(pp + pc, axis=-1, keepdims=True) + jnp.exp(sink - mx)
            o = ((_dot(pp.astype(BF16), vp) + _dot(pc.astype(BF16), vc)) * (1.0 / den)).astype(BF16)
            lse = mx + jnp.log(den)
            for r in range(ATTN_REP):
                hd = ATTN_REP * kk + r
                o_ref[:, HD * hd:HD * (hd + 1)] = o[BLK * r:BLK * (r + 1)]
                lse_ref[:, hd:hd + 1] = lse[BLK * r:BLK * (r + 1)]

    sds = jax.ShapeDtypeStruct
    return pl.pallas_call(
        body, grid=(batch, nb),
        in_specs=[pl.BlockSpec((BLK, ATTN_Q_DIM), lambda b, n: (cur(b, n), 0)),
                  pl.BlockSpec((BLK, ATTN_KV_DIM), lambda b, n: (cur(b, n), 0)),
                  pl.BlockSpec((BLK, ATTN_KV_DIM), lambda b, n: (prev(b, n), 0)),
                  pl.BlockSpec((BLK, ATTN_KV_DIM), lambda b, n: (cur(b, n), 0)),
                  pl.BlockSpec((BLK, ATTN_KV_DIM), lambda b, n: (prev(b, n), 0)),
                  pl.BlockSpec((ATTN_Q_HEADS, BLK, 2 * BLK), lambda b, n: (0, 0, 0)),
                  pl.BlockSpec((1, ATTN_Q_HEADS), lambda b, n: (0, 0))],
        out_specs=[pl.BlockSpec((BLK, ATTN_Q_DIM), lambda b, n: (cur(b, n), 0)),
                   pl.BlockSpec((BLK, ATTN_Q_HEADS), lambda b, n: (cur(b, n), 0))],
        out_shape=[sds((T, ATTN_Q_DIM), BF16), sds((T, ATTN_Q_HEADS), F32)],
        compiler_params=_cparams("parallel", "parallel"), name=name)(q, k, k, v, v, bias, sinks)


def _proj_specs(wmix):
    return [_part(wmix, 512, 0), _part(wmix, 256, 2), _part(wmix, 256, 3)]


def _natural(w_ref):
    return w_ref[...].reshape(-1, w_ref.shape[2])


def mix_out_fwd(ys, o, gates, h, wmix, g_post, *, name, tm=512):
    T, D = h.shape
    nt = T // tm

    def body(ys_ref, o_ref, gates_ref, h_ref, wssm_ref, wattn_ref, wout_ref, g_ref,
             hout_ref, yssm_ref, yattn_ref, mix_ref, merged_ref):
        y_ssm = _dot(ys_ref[...], _natural(wssm_ref))
        y_attn = _dot(o_ref[...], _natural(wattn_ref))
        yssm_ref[...] = y_ssm
        yattn_ref[...] = y_attn
        merged = (_sigmoid(gates_ref[:, 0:D]) * y_ssm + _sigmoid(gates_ref[:, D:2 * D]) * y_attn).astype(BF16)
        merged_ref[...] = merged
        mix = _dot(merged, _natural(wout_ref))
        mix_ref[...] = mix
        r = lax.rsqrt(jnp.mean(mix * mix, axis=-1, keepdims=True) + RMS_EPS)
        hout_ref[...] = h_ref[...] + mix * r * g_ref[...]

    sds = jax.ShapeDtypeStruct
    return pl.pallas_call(
        body, grid=(nt,),
        in_specs=[_rows(tm, SSM_D_INNER), _rows(tm, ATTN_Q_DIM), _rows(tm, 2 * D), _rows(tm, D),
                  *_proj_specs(wmix), _resident((1, D))],
        out_specs=[_rows(tm, D)] * 5,
        out_shape=[sds((T, D), F32), sds((T, D), F32), sds((T, D), F32), sds((T, D), F32), sds((T, D), BF16)],
        compiler_params=_cparams("parallel"), name=name)(ys, o, gates, h, wmix, wmix, wmix, g_post)


def mix_out_bwd(dh, mix, y_ssm, y_attn, gates, wmix, g_post, *, name, tm=256):
    T, D = dh.shape
    nt = T // tm

    def body(dh_ref, mix_ref, yssm_ref, yattn_ref, gates_ref, wssm_ref, wattn_ref, wout_ref, g_ref,
             dmix_ref, dyssm_ref, dyattn_ref, dgates_ref, dys_ref, do_ref, dg_ref):
        @pl.when(pl.program_id(0) == 0)
        def _():
            dg_ref[...] = jnp.zeros_like(dg_ref)

        do = dh_ref[...]
        mix = mix_ref[...]
        r = lax.rsqrt(jnp.mean(mix * mix, axis=-1, keepdims=True) + RMS_EPS)
        dg_ref[...] += jnp.sum(do * mix * r, axis=0, keepdims=True)
        t = do * g_ref[...]
        dmix = (r * t - mix * (r * r * r * jnp.mean(t * mix, axis=-1, keepdims=True))).astype(BF16)
        dmix_ref[...] = dmix
        dmerged = _dot_nt(dmix, _natural(wout_ref))
        s1 = _sigmoid(gates_ref[:, 0:D])
        s2 = _sigmoid(gates_ref[:, D:2 * D])
        dyssm = (dmerged * s1).astype(BF16)
        dyattn = (dmerged * s2).astype(BF16)
        dyssm_ref[...] = dyssm
        dyattn_ref[...] = dyattn
        dgates_ref[:, 0:D] = (dmerged * yssm_ref[...] * (s1 * (1.0 - s1))).astype(BF16)
        dgates_ref[:, D:2 * D] = (dmerged * yattn_ref[...] * (s2 * (1.0 - s2))).astype(BF16)
        dys_ref[...] = _dot_nt(dyssm, _natural(wssm_ref))
        do_ref[...] = _dot_nt(dyattn, _natural(wattn_ref)).astype(BF16)

    sds = jax.ShapeDtypeStruct
    return pl.pallas_call(
        body, grid=(nt,),
        in_specs=[_rows(tm, D), _rows(tm, D), _rows(tm, D), _rows(tm, D), _rows(tm, 2 * D),
                  *_proj_specs(wmix), _resident((1, D))],
        out_specs=[_rows(tm, D), _rows(tm, D), _rows(tm, D), _rows(tm, 2 * D), _rows(tm, SSM_D_INNER),
                   _rows(tm, ATTN_Q_DIM), pl.BlockSpec((1, D), lambda i: (0, 0))],
        out_shape=[sds((T, D), BF16), sds((T, D), BF16), sds((T, D), BF16), sds((T, 2 * D), BF16),
                   sds((T, SSM_D_INNER), F32), sds((T, ATTN_Q_DIM), BF16), sds((1, D), F32)],
        compiler_params=_cparams("arbitrary"), name=name)(dh, mix, y_ssm, y_attn, gates, wmix, wmix, wmix, g_post)


def attn_bwd(q, k, v, o, do, lse, bias, sinks, *, batch, name):
    T = q.shape[0]
    BLK, HD = ATTN_BLOCK, ATTN_HEAD_DIM
    nb = T // batch // BLK
    cur, prev = _attn_specs(nb)(nb)
    scale = HD ** -0.5

    def body(q_ref, kc_ref, kp_ref, vc_ref, vp_ref, o_ref, do_ref, lse_ref, bias_ref, sink_ref,
             dq_ref, dk_ref, dv_ref, dbias_ref, dsink_ref, ck, cv):
        b = pl.program_id(0)
        n = pl.program_id(1)

        @pl.when(jnp.logical_and(b == 0, n == 0))
        def _():
            dbias_ref[...] = jnp.zeros_like(dbias_ref)
            dsink_ref[...] = jnp.zeros_like(dsink_ref)

        @pl.when(n == 0)
        def _():
            ck[...] = jnp.zeros_like(ck)
            cv[...] = jnp.zeros_like(cv)

        @pl.when(n == nb)
        def _():
            dk_ref[...] = ck[...].astype(BF16)
            dv_ref[...] = cv[...].astype(BF16)

        @pl.when(n < nb)
        def _():
            m_prev, m_cur = _attn_masks(n)
            lane16 = _iota((1, ATTN_Q_HEADS), 1)
            dsink = jnp.zeros((1, ATTN_Q_HEADS), F32)
            for kk in range(ATTN_KV_HEADS):
                ks = slice(HD * kk, HD * (kk + 1))
                kc, kp, vc, vp = kc_ref[:, ks], kp_ref[:, ks], vc_ref[:, ks], vp_ref[:, ks]
                heads = range(ATTN_REP * kk, ATTN_REP * (kk + 1))
                qg, bias_p, bias_c, sink = _attn_group(kk, q_ref, bias_ref, sink_ref)
                dog = jnp.concatenate([do_ref[:, HD * hd:HD * (hd + 1)] for hd in heads], axis=0)
                og = jnp.concatenate([o_ref[:, HD * hd:HD * (hd + 1)] for hd in heads], axis=0)
                lse = jnp.concatenate([lse_ref[:, hd:hd + 1] for hd in heads], axis=0)
                lp = jnp.where(m_prev, _dot_nt(qg, kp) * scale + bias_p, -1e30)
                lc = jnp.where(m_cur, _dot_nt(qg, kc) * scale + bias_c, -1e30)
                pp = jnp.exp(lp - lse)
                pc = jnp.exp(lc - lse)
                delta = jnp.sum(dog.astype(F32) * og.astype(F32), axis=-1, keepdims=True)
                dlp = pp * (_dot_nt(dog, vp) - delta)
                dlc = pc * (_dot_nt(dog, vc) - delta)
                sd = jnp.exp(sink - lse) * delta
                dlpb = dlp.astype(BF16)
                dlcb = dlc.astype(BF16)
                dqg = ((_dot(dlpb, kp) + _dot(dlcb, kc)) * scale).astype(BF16)
                for r, hd in enumerate(heads):
                    rows = slice(BLK * r, BLK * (r + 1))
                    dsink = dsink + jnp.where(lane16 == hd, -jnp.sum(sd[rows], axis=0, keepdims=True), 0.0)
                    dbias_ref[hd, :, 0:BLK] += dlp[rows]
                    dbias_ref[hd, :, BLK:2 * BLK] += dlc[rows]
                    dq_ref[:, HD * hd:HD * (hd + 1)] = dqg[rows]
                dk_ref[:, ks] = (ck[:, ks] + _dot_tn(dlpb, qg) * scale).astype(BF16)
                dv_ref[:, ks] = (cv[:, ks] + _dot_tn(pp.astype(BF16), dog)).astype(BF16)
                ck[:, ks] = _dot_tn(dlcb, qg) * scale
                cv[:, ks] = _dot_tn(pc.astype(BF16), dog)
            dsink_ref[...] += dsink

    sds = jax.ShapeDtypeStruct
    qspec = pl.BlockSpec((BLK, ATTN_Q_DIM), lambda b, n: (cur(b, n), 0))
    cspec = pl.BlockSpec((BLK, ATTN_KV_DIM), lambda b, n: (cur(b, n), 0))
    pspec = pl.BlockSpec((BLK, ATTN_KV_DIM), lambda b, n: (prev(b, n), 0))
    late = pl.BlockSpec((BLK, ATTN_KV_DIM), lambda b, n: (b * nb + jnp.maximum(n - 1, 0), 0))
    return pl.pallas_call(
        body, grid=(batch, nb + 1),
        in_specs=[qspec, cspec, pspec, cspec, pspec, qspec, qspec,
                  pl.BlockSpec((BLK, ATTN_Q_HEADS), lambda b, n: (cur(b, n), 0)),
                  pl.BlockSpec((ATTN_Q_HEADS, BLK, 2 * BLK), lambda b, n: (0, 0, 0)),
                  pl.BlockSpec((1, ATTN_Q_HEADS), lambda b, n: (0, 0))],
        out_specs=[qspec, late, late,
                   pl.BlockSpec((ATTN_Q_HEADS, BLK, 2 * BLK), lambda b, n: (0, 0, 0)),
                   pl.BlockSpec((1, ATTN_Q_HEADS), lambda b, n: (0, 0))],
        out_shape=[sds((T, ATTN_Q_DIM), BF16), sds((T, ATTN_KV_DIM), BF16), sds((T, ATTN_KV_DIM), BF16),
                   sds((ATTN_Q_HEADS, BLK, 2 * BLK), F32), sds((1, ATTN_Q_HEADS), F32)],
        scratch_shapes=[pltpu.VMEM((BLK, ATTN_KV_DIM), F32), pltpu.VMEM((BLK, ATTN_KV_DIM), F32)],
        compiler_params=_cparams("arbitrary", "arbitrary"), name=name)(q, k, k, v, v, o, do, lse, bias, sinks)


def _conv_bwd(dxc, pre, xp_ref, w_ref, carry_ref, acc_ref, dp_ref, g, last):
    Q = SSM_CHUNK
    sg = _sigmoid(pre)
    dpre = dxc * (sg * (1.0 + pre * (1.0 - sg)))

    @pl.when(last)
    def _():
        carry_ref[g] = jnp.zeros(carry_ref.shape[1:], F32)

    dp_ref[0:Q, :] = dpre
    dp_ref[Q:Q + HALO, :] = carry_ref[g]
    carry_ref[g] = dpre[0:HALO, :]
    rows = [jnp.sum(dpre * xp_ref[pl.ds(HALO - 3 + k, Q), :], axis=0, keepdims=True) for k in range(SSM_CONV)]
    rows.append(jnp.sum(dpre, axis=0, keepdims=True))
    rows.append(jnp.zeros((HALO - SSM_CONV - 1, dpre.shape[1]), F32))
    acc_ref[g] += jnp.concatenate(rows, axis=0)
    dx = w_ref[3:4, :] * dpre
    for k in range(SSM_CONV - 1):
        dx = dx + w_ref[k:k + 1, :] * dp_ref[pl.ds(3 - k, Q), :]
    return dx


def _v1_ssd_bwd(dys, y, xbc, z, dt_raw, dt_rawT, states, conv_w, conv_b, dt_bias, a_log, d_skip, norm_g, *, batch, name):
    T = xbc.shape[0]
    Q, GW, N, P, HPG, G, H = SSM_CHUNK, SSM_GW, SSM_STATE, SSM_HEAD_DIM, SSM_HPG, SSM_GROUPS, SSM_HEADS
    nc = T // batch // Q
    sp = _ssd_specs(nc)(lambda c: nc - 1 - c)

    def body(xs_ref, bm_ref, cm_ref, xs_halo, bm_halo, cm_halo, z_ref, y_ref, dys_ref, dt_ref, dtT_ref, st_ref,
             w_xs, w_bm, w_cm, b_xs, b_bm, b_cm, dtb_ref, dtbT_ref, alog_ref, alogT_ref, dskip_ref, ng_ref,
             dz_ref, dxs_ref, dbm_ref, dcm_ref, ddt_ref, acc_xs, acc_bm, acc_cm, acc_head,
             dstate, acsT_s, dacsT_s, yoff_s, dxw_s, dx_s, xp_xs, xp_bm, xp_cm, dp_xs, dp_bm, dp_cm,
             cy_xs, cy_bm, cy_cm):
        b = pl.program_id(0)
        cr = pl.program_id(1)
        g = pl.program_id(2)
        c = nc - 1 - cr
        first = c == 0
        last = cr == 0

        @pl.when(jnp.logical_and(jnp.logical_and(b == 0, cr == 0), g == 0))
        def _():
            acc_xs[...] = jnp.zeros_like(acc_xs)
            acc_bm[...] = jnp.zeros_like(acc_bm)
            acc_cm[...] = jnp.zeros_like(acc_cm)
            acc_head[...] = jnp.zeros_like(acc_head)

        cc = _ssd_chunk_common(first, g, xs_ref, bm_ref, cm_ref, xs_halo, bm_halo, cm_halo, w_xs, w_bm, w_cm,
                               b_xs, b_bm, b_cm, dt_ref, dtT_ref, dtb_ref, dtbT_ref, alog_ref, alogT_ref,
                               xp_xs, xp_bm, xp_cm)
        xs, acs, dt, a = cc["xs"], cc["acs"], cc["dt"], cc["a"]
        acsT_s[...] = cc["acsT"]
        dacsT_s[...] = jnp.zeros_like(dacsT_s)
        e64 = _head_expand(g, P)
        e128 = _head_expand(g, Q)
        acs_x = _dot_hi(acs, e64)
        acs_b = _dot_hi(acs, e128)
        dt_x = _dot_hi(dt, e64)
        x = xs * dt_x
        w_x = jnp.exp(acs_x[Q - 1:Q, :] - acs_x)
        ex = jnp.exp(acs_x)

        yv = y_ref[...]
        zz = z_ref[...]
        sz = _sigmoid(zz)
        silu_z = zz * sz
        yg = yv * silu_z
        rr = lax.rsqrt(jnp.mean(yg * yg, axis=-1, keepdims=True) + RMS_EPS)
        dys_v = dys_ref[...]
        d_ng = jnp.sum(dys_v * yg * rr, axis=0, keepdims=True)
        t = dys_v * ng_ref[...]
        dyg = rr * t - yg * (rr * rr * rr * jnp.mean(t * yg, axis=-1, keepdims=True))
        dy = dyg * silu_z
        dz_ref[...] = (dyg * yv * (sz * (1.0 + zz * (1.0 - sz)))).astype(BF16)

        dexp = _dot_hi(jnp.broadcast_to(dskip_ref[...], (8, H)), e64)[0:1, :]
        d_dskip = _dot_nt(jnp.broadcast_to(jnp.sum(dy * xs, axis=0, keepdims=True), (8, GW)), e64)[0:1, :]

        dyb = dy.astype(BF16)
        xb = x.astype(BF16)
        xwb = (x * w_x).astype(BF16)
        bb = cc["bm"].astype(BF16)
        cb = cc["cm"].astype(BF16)
        s = _dot_nt(cb, bb)
        causal = _iota((Q, Q), 0) >= _iota((Q, Q), 1)
        lane_h = _iota((1, H), 1)
        ds_acc = jnp.zeros((Q, Q), F32)
        d_c = jnp.zeros((Q, N), F32)
        d_b = jnp.zeros((Q, N), F32)
        dacs = jnp.zeros((Q, H), F32)
        last_terms = jnp.zeros((1, H), F32)
        for r in range(HPG):
            hd = HPG * g + r
            cols = slice(P * r, P * (r + 1))

            @pl.when(last)
            def _():
                dstate[hd] = jnp.zeros((P, N), F32)

            seg = acs_b[:, Q * r:Q * (r + 1)] - acsT_s[pl.ds(hd, 1), :]
            l = jnp.exp(jnp.where(causal, seg, -1e30))
            m = s * l
            mb = m.astype(BF16)
            dyh = dyb[:, cols]
            hp = st_ref[0, r]
            hpb = hp.astype(BF16)
            dh = dstate[hd]
            dhb = dh.astype(BF16)
            yoff_s[:, cols] = _dot_nt(cb, hpb) * ex[:, cols]
            dye = (dy[:, cols] * ex[:, cols]).astype(BF16)
            d_c = d_c + _dot(dye, hpb)
            dhp_off = _dot_tn(dye, cb)
            dm = _dot_nt(dyh, xb[:, cols])
            dx_s[:, cols] = _dot_tn(mb, dyh)
            gmat = dm * m
            onehot = (lane_h == hd).astype(F32)
            dacs = dacs + jnp.sum(gmat, axis=-1, keepdims=True) * onehot
            dacsT_s[pl.ds(hd, 1), :] = -jnp.sum(gmat, axis=0, keepdims=True)
            ds_acc = ds_acc + dm * l
            dxw_s[:, cols] = _dot_nt(bb, dhb)
            d_b = d_b + _dot(xwb[:, cols], dhb)
            decay = jnp.exp(acsT_s[pl.ds(hd, 1), pl.ds(Q - 1, 1)])
            ddecay = jnp.sum(jnp.sum(dh * hp, axis=-1, keepdims=True), axis=0, keepdims=True)
            last_terms = last_terms + (ddecay * decay) * onehot
            dstate[hd] = dh * decay + dhp_off
        dsb = ds_acc.astype(BF16)
        d_c = d_c + _dot(dsb, bb)
        d_b = d_b + _dot_tn(dsb, cb)
        dxw = dxw_s[...]
        dx_full = dx_s[...] + dxw * w_x
        tw = _dot_nt(dxw * x * w_x, e64)
        dacs = dacs + _dot_nt(dy * yoff_s[...], e64) - tw
        last_terms = last_terms + jnp.sum(tw, axis=0, keepdims=True)
        eye = (_iota((Q, Q), 0) == _iota((Q, Q), 1)).astype(F32)
        dacs = dacs + lax.dot_general(eye, dacsT_s[...], (((1,), (1,)), ((), ())), preferred_element_type=F32,
                                      precision=lax.Precision.HIGHEST)
        dacs = dacs + jnp.where(_iota((Q, 1), 0) == Q - 1, 1.0, 0.0) * last_terms
        d_dta = _dot_hi(cc["triT"], dacs)
        ddt = d_dta * a + _dot_nt(dx_full * xs, e64)
        d_alog = jnp.sum(d_dta * dt, axis=0, keepdims=True) * a
        ddt_raw = ddt * _sigmoid(cc["dtr"])
        d_dtb = jnp.sum(ddt_raw, axis=0, keepdims=True)

        @pl.when(g == 0)
        def _():
            ddt_ref[...] = ddt_raw

        @pl.when(g > 0)
        def _():
            ddt_ref[...] += ddt_raw

        acc_head[...] += jnp.concatenate([d_dtb, d_alog, d_dskip, jnp.zeros((5, H), F32)], axis=0)
        dxs = dexp * dy + dx_full * dt_x
        dxs_ref[...] = _conv_bwd(dxs, cc["pre_xs"], xp_xs, w_xs, cy_xs, acc_xs, dp_xs, g, last).astype(BF16)
        dbm_ref[...] = _conv_bwd(d_b, cc["pre_bm"], xp_bm, w_bm, cy_bm, acc_bm, dp_bm, g, last).astype(BF16)
        dcm_ref[...] = _conv_bwd(d_c, cc["pre_cm"], xp_cm, w_cm, cy_cm, acc_cm, dp_cm, g, last).astype(BF16)
        acc_xs[g, pl.ds(SSM_CONV + 1, 1), :] += d_ng

    col = lambda v: v.reshape(H, 1)
    sds = jax.ShapeDtypeStruct
    row = lambda b, c, g: b * nc + (nc - 1 - c)
    full = lambda shape: pl.BlockSpec(shape, lambda b, c, g: (0,) * len(shape))
    return pl.pallas_call(
        body, grid=(batch, nc, G),
        in_specs=[sp["xs"], sp["bm"], sp["cm"], sp["xs_halo"], sp["bm_halo"], sp["cm_halo"], sp["grp"], sp["grp"],
                  sp["grp"], sp["dt"], sp["dtT"], sp["state"],
                  sp["w_xs"], sp["w_bm"], sp["w_cm"], sp["b_xs"], sp["b_bm"], sp["b_cm"],
                  sp["row32"], sp["col32"], sp["row32"], sp["col32"], sp["row32"], sp["vec_g"]],
        out_specs=[sp["grp"], sp["grp"],
                   pl.BlockSpec((Q, N), lambda b, c, g: (row(b, c, g), g)),
                   pl.BlockSpec((Q, N), lambda b, c, g: (row(b, c, g), g)),
                   sp["dt"], full((G, HALO, GW)), full((G, HALO, N)), full((G, HALO, N)), full((8, H))],
        out_shape=[sds((T, SSM_D_INNER), BF16), sds((T, SSM_D_INNER), BF16), sds((T, G * N), BF16),
                   sds((T, G * N), BF16), sds((T, H), F32),
                   sds((G, HALO, GW), F32), sds((G, HALO, N), F32), sds((G, HALO, N), F32), sds((8, H), F32)],
        scratch_shapes=[pltpu.VMEM((H, P, N), F32), pltpu.VMEM((H, Q), F32), pltpu.VMEM((H, Q), F32),
                        pltpu.VMEM((Q, GW), F32), pltpu.VMEM((Q, GW), F32), pltpu.VMEM((Q, GW), F32),
                        pltpu.VMEM((HALO + Q, GW), F32), pltpu.VMEM((HALO + Q, N), F32), pltpu.VMEM((HALO + Q, N), F32),
                        pltpu.VMEM((Q + HALO, GW), F32), pltpu.VMEM((Q + HALO, N), F32), pltpu.VMEM((Q + HALO, N), F32),
                        pltpu.VMEM((G, HALO, GW), F32), pltpu.VMEM((G, HALO, N), F32), pltpu.VMEM((G, HALO, N), F32)],
        compiler_params=_cparams("arbitrary", "arbitrary", "arbitrary"), name=name,
    )(xbc, xbc, xbc, xbc, xbc, xbc, z, y, dys, dt_raw, dt_rawT, states, conv_w, conv_w, conv_w, conv_b, conv_b, conv_b,
      dt_bias, col(dt_bias), a_log, col(a_log), d_skip, norm_g)


def mix_in_bwd(dh, h, g, dgates, dz, dxs, dbm, dcm, ddtT, dq, dk, dv, w_gz, w_xbc, w_dtT, w_qkv, *, name, tm=512):
    T, D = h.shape
    nt = T // tm
    GN = SSM_GROUPS * SSM_STATE

    def body(dh_ref, h_ref, g_ref, dgates_ref, dz_ref, dxs_ref, dbm_ref, dcm_ref, ddt_ref, dq_ref, dk_ref, dv_ref,
             wgz_ref, wxbc_ref, wdt_ref, wqkv_ref, dhin_ref, dg_ref):
        @pl.when(pl.program_id(0) == 0)
        def _():
            dg_ref[...] = jnp.zeros_like(dg_ref)

        du = _dot_nt(dgates_ref[...], wgz_ref[:, 0:2048])
        du = du + _dot_nt(dz_ref[...], wgz_ref[:, 2048:4096])
        du = du + _dot_nt(dxs_ref[...], wxbc_ref[:, 0:SSM_D_INNER])
        du = du + _dot_nt(dbm_ref[...], wxbc_ref[:, SSM_D_INNER:SSM_D_INNER + GN])
        du = du + _dot_nt(dcm_ref[...], wxbc_ref[:, SSM_D_INNER + GN:])
        du = du + _dot_tn(ddt_ref[...].astype(BF16), wdt_ref[...])
        du = du + _dot_nt(dq_ref[...], wqkv_ref[:, 0:ATTN_Q_DIM])
        du = du + _dot_nt(dk_ref[...], wqkv_ref[:, ATTN_Q_DIM:ATTN_Q_DIM + ATTN_KV_DIM])
        du = du + _dot_nt(dv_ref[...], wqkv_ref[:, ATTN_Q_DIM + ATTN_KV_DIM:])
        hh = h_ref[...]
        r = lax.rsqrt(jnp.mean(hh * hh, axis=-1, keepdims=True) + RMS_EPS)
        dg_ref[...] += jnp.sum(du * hh * r, axis=0, keepdims=True)
        t = du * g_ref[...]
        dhin_ref[...] = dh_ref[...] + r * t - hh * (r * r * r * jnp.mean(t * hh, axis=-1, keepdims=True))

    sds = jax.ShapeDtypeStruct
    return pl.pallas_call(
        body, grid=(nt,),
        in_specs=[_rows(tm, D), _rows(tm, D), _resident((1, D)), _rows(tm, 2048), _rows(tm, 2048), _rows(tm, SSM_D_INNER),
                  _rows(tm, GN), _rows(tm, GN), pl.BlockSpec((SSM_HEADS, tm), lambda i: (0, i)),
                  _rows(tm, ATTN_Q_DIM), _rows(tm, ATTN_KV_DIM),
                  _rows(tm, ATTN_KV_DIM), _resident(w_gz.shape), _resident(w_xbc.shape), _resident(w_dtT.shape),
                  _resident(w_qkv.shape)],
        out_specs=[_rows(tm, D), pl.BlockSpec((1, D), lambda i: (0, 0))],
        out_shape=[sds((T, D), F32), sds((1, D), F32)],
        compiler_params=_cparams("arbitrary"), name=name,
    )(dh, h, g, dgates, dz, dxs, dbm, dcm, ddtT, dq, dk, dv, w_gz, w_xbc, w_dtT, w_qkv)


PAIRS = SSM_HPG // 2
PW = 2 * SSM_HEAD_DIM


def _ssd_prologue(first, xs_ref, bm_ref, cm_ref, xs_halo, bm_halo, cm_halo, w_xs, w_bm, w_cm, b_xs, b_bm, b_cm,
                  dtT_ref, dtb_ref, alog_ref, xp_xs, xp_bm, xp_cm):
    Q = SSM_CHUNK
    pre_xs = _conv_pre(xs_ref, xs_halo, w_xs, b_xs, xp_xs, first)
    pre_bm = _conv_pre(bm_ref, bm_halo, w_bm, b_bm, xp_bm, first)
    pre_cm = _conv_pre(cm_ref, cm_halo, w_cm, b_cm, xp_cm, first)
    dtrT = dtT_ref[...] + dtb_ref[...]
    dtT = _softplus(dtrT)
    aT = -jnp.exp(alog_ref[...])
    triT = (_iota((Q, Q), 0) <= _iota((Q, Q), 1)).astype(F32)
    acsT = _dot_hi(dtT * aT, triT)
    lastT = acsT[:, Q - 1:Q]
    wT = jnp.exp(lastT - acsT)
    eT = jnp.exp(acsT)
    cols = jnp.concatenate([dtT, acsT, wT, eT], axis=0).T
    return dict(pre_xs=pre_xs, pre_bm=pre_bm, pre_cm=pre_cm, xs=pre_xs * _sigmoid(pre_xs), bm=pre_bm * _sigmoid(pre_bm),
                cm=pre_cm * _sigmoid(pre_cm), dtrT=dtrT, dtT=dtT, aT=aT, acsT=acsT, decayT=jnp.exp(lastT), cols=cols)


def _pair_cols(cols, base, p, lo):
    k = base + 2 * p
    return jnp.where(lo, cols[:, k:k + 1], cols[:, k + 1:k + 2])


def _pair_row(colT, p, lo_row):
    return jnp.where(lo_row, colT[2 * p:2 * p + 1, :], colT[2 * p + 1:2 * p + 2, :])


def _pair_operands(pp, p, s, causal, lo, xb):
    zero = jnp.zeros_like(xb)
    rhs = jnp.concatenate([jnp.where(lo, xb, zero), jnp.where(lo, zero, xb)], axis=0)
    ls, ms = [], []
    for k in (2 * p, 2 * p + 1):
        seg = pp["cols"][:, 8 + k:9 + k] - pp["acsT"][k:k + 1, :]
        l = jnp.exp(jnp.where(causal, seg, -1e30))
        ls.append(l)
        ms.append(s * l)
    lhs = jnp.concatenate([m.astype(BF16) for m in ms], axis=1)
    return lhs, rhs, ls


def ssd_fwd(xbc, z, dt_rawT, conv_w, conv_b, dt_bias, a_log, d_skip_x, norm_g, *, batch, name):
    T = xbc.shape[0]
    Q, GW, N = SSM_CHUNK, SSM_GW, SSM_STATE
    nc = T // batch // Q
    sp = _ssd_specs(nc)(lambda c: c)

    def body(xs_ref, bm_ref, cm_ref, xs_halo, bm_halo, cm_halo, z_ref, dtT_ref,
             w_xs, w_bm, w_cm, b_xs, b_bm, b_cm, dtb_ref, alog_ref, dsk_ref, ng_ref,
             y_ref, ys_ref, st_ref, state, xp_xs, xp_bm, xp_cm):
        c = pl.program_id(1)
        g = pl.program_id(2)
        first = c == 0
        pp = _ssd_prologue(first, xs_ref, bm_ref, cm_ref, xs_halo, bm_halo, cm_halo, w_xs, w_bm, w_cm,
                           b_xs, b_bm, b_cm, dtT_ref, dtb_ref, alog_ref, xp_xs, xp_bm, xp_cm)
        xs = pp["xs"]
        bb = pp["bm"].astype(BF16)
        cb = pp["cm"].astype(BF16)
        s = _dot_nt(cb, bb)
        causal = _iota((Q, Q), 0) >= _iota((Q, Q), 1)
        lo = _iota((Q, PW), 1) < SSM_HEAD_DIM
        lo_row = _iota((1, PW), 1) < SSM_HEAD_DIM
        ys = []
        for p in range(PAIRS):
            tile = slice(PW * p, PW * (p + 1))

            @pl.when(first)
            def _():
                state[g, p] = jnp.zeros((N, PW), F32)

            xs_p = xs[:, tile]
            x_p = xs_p * _pair_cols(pp["cols"], 0, p, lo)
            lhs, rhs, _ = _pair_operands(pp, p, s, causal, lo, x_p.astype(BF16))
            hp = state[g, p]
            st_ref[0, p] = hp
            ys.append(_dot(lhs, rhs) + _dot(cb, hp.astype(BF16)) * _pair_cols(pp["cols"], 24, p, lo)
                      + dsk_ref[:, tile] * xs_p)
            xw = (x_p * _pair_cols(pp["cols"], 16, p, lo)).astype(BF16)
            state[g, p] = hp * _pair_row(pp["decayT"], p, lo_row) + _dot_tn(bb, xw)
        y = jnp.concatenate(ys, axis=1)
        y_ref[...] = y
        zz = z_ref[...]
        yg = y * (zz * _sigmoid(zz))
        rr = lax.rsqrt(jnp.mean(yg * yg, axis=-1, keepdims=True) + RMS_EPS)
        ys_ref[...] = (yg * rr * ng_ref[...]).astype(BF16)

    sds = jax.ShapeDtypeStruct
    return pl.pallas_call(
        body, grid=(batch, nc, SSM_GROUPS),
        in_specs=[sp["xs"], sp["bm"], sp["cm"], sp["xs_halo"], sp["bm_halo"], sp["cm_halo"], sp["grp"], sp["dtT_g"],
                  sp["w_xs"], sp["w_bm"], sp["w_cm"], sp["b_xs"], sp["b_bm"], sp["b_cm"],
                  sp["col_g"], sp["col_g"], sp["vec_g"], sp["vec_g"]],
        out_specs=[sp["grp"], sp["grp"], sp["pairs"]],
        out_shape=[sds((T, SSM_D_INNER), F32), sds((T, SSM_D_INNER), BF16),
                   sds((T // Q, SSM_GROUPS * PAIRS, N, PW), F32)],
        scratch_shapes=[pltpu.VMEM((SSM_GROUPS, PAIRS, N, PW), F32),
                        pltpu.VMEM((HALO + Q, GW), F32), pltpu.VMEM((HALO + Q, N), F32), pltpu.VMEM((HALO + Q, N), F32)],
        compiler_params=_cparams("arbitrary", "arbitrary", "arbitrary"), name=name,
    )(xbc, xbc, xbc, xbc, xbc, xbc, z, dt_rawT, conv_w, conv_w, conv_w, conv_b, conv_b, conv_b,
      dt_bias, a_log, d_skip_x, norm_g)


def ssd_bwd(dys, y, xbc, z, dt_rawT, states, conv_w, conv_b, dt_bias, a_log, d_skip_x, norm_g, *, batch, name,
            ride=None):
    T = xbc.shape[0]
    Q, GW, N, G = SSM_CHUNK, SSM_GW, SSM_STATE, SSM_GROUPS
    nc = T // batch // Q
    sp = _ssd_specs(nc)(lambda c: nc - 1 - c)

    def body(xs_ref, bm_ref, cm_ref, xs_halo, bm_halo, cm_halo, z_ref, y_ref, dys_ref, dtT_ref, st_ref,
             w_xs, w_bm, w_cm, b_xs, b_bm, b_cm, dtb_ref, alog_ref, dsk_ref, ng_ref,
             dz_ref, dxs_ref, dbm_ref, dcm_ref, ddtT_ref, acc_xs, acc_bm, acc_cm, acc_head,
             dstate, xp_xs, xp_bm, xp_cm, dp_xs, dp_bm, dp_cm, cy_xs, cy_bm, cy_cm):
        b = pl.program_id(0)
        cr = pl.program_id(1)
        g = pl.program_id(2)
        first = cr == nc - 1
        last = cr == 0

        @pl.when(jnp.logical_and(jnp.logical_and(b == 0, cr == 0), g == 0))
        def _():
            acc_xs[...] = jnp.zeros_like(acc_xs)
            acc_bm[...] = jnp.zeros_like(acc_bm)
            acc_cm[...] = jnp.zeros_like(acc_cm)
            acc_head[...] = jnp.zeros_like(acc_head)

        pp = _ssd_prologue(first, xs_ref, bm_ref, cm_ref, xs_halo, bm_halo, cm_halo, w_xs, w_bm, w_cm,
                           b_xs, b_bm, b_cm, dtT_ref, dtb_ref, alog_ref, xp_xs, xp_bm, xp_cm)
        xs, dtT, aT, decayT = pp["xs"], pp["dtT"], pp["aT"], pp["decayT"]

        yv = y_ref[...]
        zz = z_ref[...]
        sz = _sigmoid(zz)
        silu_z = zz * sz
        yg = yv * silu_z
        rr = lax.rsqrt(jnp.mean(yg * yg, axis=-1, keepdims=True) + RMS_EPS)
        dys_v = dys_ref[...]
        d_ng = jnp.sum(dys_v * yg * rr, axis=0, keepdims=True)
        t = dys_v * ng_ref[...]
        dyg = rr * t - yg * (rr * rr * rr * jnp.mean(t * yg, axis=-1, keepdims=True))
        dy = dyg * silu_z
        dz_ref[...] = (dyg * yv * (sz * (1.0 + zz * (1.0 - sz)))).astype(BF16)
        dsk = dsk_ref[...]
        d_dsk = jnp.sum(dy * xs, axis=0, keepdims=True)

        bb = pp["bm"].astype(BF16)
        cb = pp["cm"].astype(BF16)
        s = _dot_nt(cb, bb)
        causal = _iota((Q, Q), 0) >= _iota((Q, Q), 1)
        lo = _iota((Q, PW), 1) < SSM_HEAD_DIM
        lo_row = _iota((1, PW), 1) < SSM_HEAD_DIM
        sub8 = _iota((SSM_HPG, 1), 0)
        ds_acc = jnp.zeros((Q, Q), F32)
        d_c = jnp.zeros((Q, N), F32)
        d_b = jnp.zeros((Q, N), F32)
        last_terms = jnp.zeros((SSM_HPG, 1), F32)
        q1, q2, dxs = [], [], []
        for p in range(PAIRS):
            tile = slice(PW * p, PW * (p + 1))

            @pl.when(last)
            def _():
                dstate[g, p] = jnp.zeros((N, PW), F32)

            dt_p = _pair_cols(pp["cols"], 0, p, lo)
            w_p = _pair_cols(pp["cols"], 16, p, lo)
            e_p = _pair_cols(pp["cols"], 24, p, lo)
            xs_p = xs[:, tile]
            x_p = xs_p * dt_p
            xw_p = x_p * w_p
            lhs, rhs, ls = _pair_operands(pp, p, s, causal, lo, x_p.astype(BF16))
            dy_p = dy[:, tile]
            dyb = dy_p.astype(BF16)
            hp = st_ref[0, p]
            hpb = hp.astype(BF16)
            dh = dstate[g, p]
            dhb = dh.astype(BF16)
            dye = (dy_p * e_p).astype(BF16)
            d_c = d_c + _dot_nt(dye, hpb)
            dm = _dot_nt(dyb, rhs)
            dxd2 = _dot_tn(lhs, dyb)
            dxd = jnp.where(lo, dxd2[0:Q], dxd2[Q:2 * Q])
            ds_acc = ds_acc + dm[:, 0:Q] * ls[0] + dm[:, Q:2 * Q] * ls[1]
            dxw = _dot(bb, dhb)
            d_b = d_b + _dot_nt(xw_p.astype(BF16), dhb)
            dx_full = dxd + dxw * w_p
            tw = dxw * xw_p
            yd = _dot(lhs, rhs)
            yoff = _dot(cb, hpb) * e_p
            q1.append(dyb.astype(F32) * yd + dy_p * yoff - tw - x_p.astype(BF16).astype(F32) * dxd)
            q2.append(dx_full * xs_p)
            dxs.append(dsk[:, tile] * dy_p + dx_full * dt_p)
            row = jnp.sum(dh * hp, axis=0, keepdims=True) * _pair_row(decayT, p, lo_row) + jnp.sum(tw, axis=0, keepdims=True)
            t_lo = jnp.sum(jnp.where(lo_row, row, 0.0), axis=1, keepdims=True)
            t_hi = jnp.sum(jnp.where(lo_row, 0.0, row), axis=1, keepdims=True)
            last_terms = last_terms + jnp.where(sub8 == 2 * p, t_lo, 0.0) + jnp.where(sub8 == 2 * p + 1, t_hi, 0.0)
            dstate[g, p] = dh * _pair_row(decayT, p, lo_row) + _dot_tn(cb, dye)
        dsb = ds_acc.astype(BF16)
        d_c = d_c + _dot(dsb, bb)
        d_b = d_b + _dot_tn(dsb, cb)
        e8 = (_iota((SSM_HPG, GW), 0) == lax.shift_right_logical(_iota((SSM_HPG, GW), 1), 6)).astype(F32)
        seg_sum = lambda tiles: lax.dot_general(e8, jnp.concatenate(tiles, axis=1), (((1,), (1,)), ((), ())),
                                                preferred_element_type=F32, precision=lax.Precision.HIGHEST)
        dacsT = seg_sum(q1) + jnp.where(_iota((1, Q), 1) == Q - 1, 1.0, 0.0) * last_terms
        tri = (_iota((Q, Q), 0) >= _iota((Q, Q), 1)).astype(F32)
        d_dtaT = _dot_hi(dacsT, tri)
        ddtT = d_dtaT * aT + seg_sum(q2)
        d_alog = jnp.sum(d_dtaT * dtT, axis=1, keepdims=True) * aT
        ddt_rawT = ddtT * _sigmoid(pp["dtrT"])
        ddtT_ref[...] = ddt_rawT
        d_dtb = jnp.sum(ddt_rawT, axis=1, keepdims=True)
        lane = _iota((SSM_HPG, N), 1)
        acc_head[g] += jnp.where(lane == 0, d_dtb, 0.0) + jnp.where(lane == 1, d_alog, 0.0)
        dxs_v = jnp.concatenate(dxs, axis=1)
        dxs_ref[...] = _conv_bwd(dxs_v, pp["pre_xs"], xp_xs, w_xs, cy_xs, acc_xs, dp_xs, g, last).astype(BF16)
        dbm_ref[...] = _conv_bwd(d_b, pp["pre_bm"], xp_bm, w_bm, cy_bm, acc_bm, dp_bm, g, last).astype(BF16)
        dcm_ref[...] = _conv_bwd(d_c, pp["pre_cm"], xp_cm, w_cm, cy_cm, acc_cm, dp_cm, g, last).astype(BF16)
        acc_xs[g, pl.ds(SSM_CONV + 1, 2), :] += jnp.concatenate([d_ng, d_dsk], axis=0)

    sds = jax.ShapeDtypeStruct
    row = lambda b, c, g: b * nc + (nc - 1 - c)
    full = lambda shape: pl.BlockSpec(shape, lambda b, c, g: (0,) * len(shape))
    return _call(
        body, grid=(batch, nc, G),
        in_specs=[sp["xs"], sp["bm"], sp["cm"], sp["xs_halo"], sp["bm_halo"], sp["cm_halo"], sp["grp"], sp["grp"],
                  sp["grp"], sp["dtT_g"], sp["pairs"],
                  sp["w_xs"], sp["w_bm"], sp["w_cm"], sp["b_xs"], sp["b_bm"], sp["b_cm"],
                  sp["col_g"], sp["col_g"], sp["vec_g"], sp["vec_g"]],
        args=[xbc, xbc, xbc, xbc, xbc, xbc, z, y, dys, dt_rawT, states, conv_w, conv_w, conv_w, conv_b, conv_b, conv_b,
              dt_bias, a_log, d_skip_x, norm_g],
        out_specs=[sp["grp"], sp["grp"],
                   pl.BlockSpec((Q, N), lambda b, c, g: (row(b, c, g), g)),
                   pl.BlockSpec((Q, N), lambda b, c, g: (row(b, c, g), g)),
                   sp["dtT_g"], full((G, HALO, GW)), full((G, HALO, N)), full((G, HALO, N)), full((G, SSM_HPG, N))],
        out_shape=[sds((T, SSM_D_INNER), BF16), sds((T, SSM_D_INNER), BF16), sds((T, G * N), BF16),
                   sds((T, G * N), BF16), sds((SSM_HEADS, T), F32),
                   sds((G, HALO, GW), F32), sds((G, HALO, N), F32), sds((G, HALO, N), F32), sds((G, SSM_HPG, N), F32)],
        scratch=[pltpu.VMEM((G, PAIRS, N, PW), F32),
                 pltpu.VMEM((HALO + Q, GW), F32), pltpu.VMEM((HALO + Q, N), F32), pltpu.VMEM((HALO + Q, N), F32),
                 pltpu.VMEM((Q + HALO, GW), F32), pltpu.VMEM((Q + HALO, N), F32), pltpu.VMEM((Q + HALO, N), F32),
                 pltpu.VMEM((G, HALO, GW), F32), pltpu.VMEM((G, HALO, N), F32), pltpu.VMEM((G, HALO, N), F32)],
        sem=("arbitrary", "arbitrary", "arbitrary"), name=name, ride=ride)


def mm_rows(a, b, *, name, tt=2048):
    M, T = a.shape
    N = b.shape[1]
    tt = min(tt, T)

    def body(a_ref, b_ref, o_ref):
        @pl.when(pl.program_id(0) == 0)
        def _():
            o_ref[...] = jnp.zeros_like(o_ref)

        o_ref[...] += _dot(a_ref[...].astype(BF16), b_ref[...])

    return pl.pallas_call(
        body, grid=(T // tt,),
        in_specs=[pl.BlockSpec((M, tt), lambda t: (0, t)), pl.BlockSpec((tt, N), lambda t: (t, 0))],
        out_specs=pl.BlockSpec((M, N), lambda t: (0, 0)), out_shape=jax.ShapeDtypeStruct((M, N), F32),
        compiler_params=_cparams("arbitrary"), name=name)(a, b)


MESH = pl.DeviceIdType.MESH
ANY = pl.BlockSpec(memory_space=pl.ANY)
ROW_ALIGN = 16


def _me():
    return lax.axis_index("x"), lax.axis_index("y"), lax.axis_index("c")


def _other_chips(x, y):
    return [(1 - x, y), (x, 1 - y), (1 - x, 1 - y)]


def _remote(src, dst, send_sem, recv_sem, to):
    return pltpu.make_async_remote_copy(src_ref=src, dst_ref=dst, send_sem=send_sem, recv_sem=recv_sem,
                                        device_id=to, device_id_type=MESH)


def _half(c, rows):
    return pl.ds(pl.multiple_of(c * (rows // 2), ROW_ALIGN), rows // 2)


def ag_ride(bufs):
    n = len(bufs)

    def copies(outs, sems):
        ici_send, ici_recv, d2d_send, d2d_recv = sems
        x, y, c = _me()
        sib = (x, y, 1 - c)
        ici, d2d, d2d_in = [], [], []
        for i in range(n):
            rows = outs[i].shape[1]
            mine = outs[i].at[2 * x + y, _half(c, rows)]
            for j, chip in enumerate(_other_chips(x, y)):
                ici.append(_remote(mine, mine, ici_send.at[i, j], ici_recv.at[i, j], (*chip, c)))
                landed = outs[i].at[2 * chip[0] + chip[1], _half(c, rows)]
                d2d.append((_remote(landed, landed, ici_send.at[i, j], ici_recv.at[i, j], (*chip, c)),
                            _remote(landed, landed, d2d_send.at[i, j], d2d_recv.at[i, j], sib)))
                lands = outs[i].at[2 * chip[0] + chip[1], _half(1 - c, rows)]
                d2d_in.append(_remote(lands, lands, d2d_send.at[i, j], d2d_recv.at[i, j], sib))
        return ici, d2d, d2d_in

    def start(ins, outs, sems):
        for cp in copies(outs, sems)[0]:
            cp.start()

    def finish(ins, outs, sems):
        ici, d2d, d2d_in = copies(outs, sems)
        for arrived, forward in d2d:
            arrived.wait_recv()
            forward.start()
        for cp in d2d_in:
            cp.wait_recv()
        for cp in ici + [forward for _, forward in d2d]:
            cp.wait_send()

    return Ride(bufs, [jax.ShapeDtypeStruct(b.shape, b.dtype) for b in bufs], [(i, i) for i in range(n)],
                [pltpu.SemaphoreType.DMA((n, 3))] * 4, start, finish)


def rs_pair(grads, *, name):
    n = len(grads)

    def body(*refs):
        ins, outs = refs[:n], refs[n:2 * n]
        send, recv = refs[2 * n:]
        x, y, c = _me()
        sib = (x, y, 1 - c)
        sent = []
        for i in range(n):
            rows = ins[i].shape[1]
            sent.append(_remote(ins[i].at[:, _half(1 - c, rows), :], outs[i], send.at[i], recv.at[i], sib))
            sent[-1].start()
        for cp in sent:
            cp.wait()

    return pl.pallas_call(
        body, in_specs=[ANY] * n, out_specs=[ANY] * n,
        out_shape=[jax.ShapeDtypeStruct((N_SHARD, g.shape[1] // 2, g.shape[2]), g.dtype) for g in grads],
        scratch_shapes=[pltpu.SemaphoreType.DMA((n,)), pltpu.SemaphoreType.DMA((n,))], name=name)(*grads)


def rs_add(grad, part, c, *, rt, name):
    _, rows, cols = grad.shape
    r2 = rows // 2
    nrb = r2 // rt

    def body(c_ref, g_ref, p_ref, o_ref):
        o_ref[...] = (g_ref[...] + p_ref[...]).astype(BF16)

    return pl.pallas_call(
        body,
        grid_spec=pltpu.PrefetchScalarGridSpec(
            num_scalar_prefetch=1, grid=(N_SHARD, nrb),
            in_specs=[pl.BlockSpec((1, rt, cols), lambda k, i, c_ref: (k, c_ref[1] * nrb + i, 0)),
                      pl.BlockSpec((1, rt, cols), lambda k, i, c_ref: (k, i, 0))],
            out_specs=pl.BlockSpec((1, rt, cols), lambda k, i, c_ref: (k, i, 0))),
        out_shape=jax.ShapeDtypeStruct((N_SHARD, r2, cols), BF16),
        compiler_params=_cparams("parallel", "parallel"), name=name)(c, grad, part)


def chips_ride(sums):
    n = len(sums)

    def copies(ins, outs, sems):
        send, recv = sems
        x, y, c = _me()
        return [_remote(ins[i].at[2 * chip[0] + chip[1]], outs[i].at[2 * x + y], send.at[i, j], recv.at[i, j], (*chip, c))
                for i in range(n) for j, chip in enumerate(_other_chips(x, y))]

    def start(ins, outs, sems):
        for cp in copies(ins, outs, sems):
            cp.start()

    def finish(ins, outs, sems):
        for cp in copies(ins, outs, sems):
            cp.wait()

    return Ride(sums, [jax.ShapeDtypeStruct(s.shape, s.dtype) for s in sums], [],
                [pltpu.SemaphoreType.DMA((n, 3))] * 2, start, finish)


def rs_total(parts, own, where, *, rt, name):
    _, r2, cols = parts.shape
    nrb = r2 // rt

    def body(w_ref, p0, p1, p2, p3, own_ref, o_ref):
        s_me = w_ref[0]
        acc = None
        for k, p in enumerate((p0, p1, p2, p3)):
            term = jnp.where(s_me == k, own_ref[0], p[0]).astype(F32)
            acc = term if acc is None else acc + term
        o_ref[...] = acc

    def slot(k):
        return pl.BlockSpec((1, rt, cols), lambda i, w: (jnp.where(w[0] == k, (k + 1) % N_SHARD, k), i, 0))

    return pl.pallas_call(
        body,
        grid_spec=pltpu.PrefetchScalarGridSpec(
            num_scalar_prefetch=1, grid=(nrb,),
            in_specs=[slot(0), slot(1), slot(2), slot(3), pl.BlockSpec((1, rt, cols), lambda i, w: (w[0], i, 0))],
            out_specs=pl.BlockSpec((rt, cols), lambda i, w: (w[1] * nrb + i, 0))),
        out_shape=jax.ShapeDtypeStruct((2 * r2, cols), F32),
        compiler_params=_cparams("parallel"), name=name)(where, parts, parts, parts, parts, own)


def rs_share(totals, *, name):
    n = len(totals)

    def body(*refs):
        outs = refs[n:2 * n]
        send, recv = refs[2 * n:]
        x, y, c = _me()
        sib = (x, y, 1 - c)
        sent = []
        for i in range(n):
            mine = outs[i].at[_half(c, outs[i].shape[0])]
            sent.append(_remote(mine, mine, send.at[i], recv.at[i], sib))
            sent[-1].start()
        for i in range(n):
            other = outs[i].at[_half(1 - c, outs[i].shape[0])]
            _remote(other, other, send.at[i], recv.at[i], sib).wait_recv()
        for cp in sent:
            cp.wait_send()

    return pl.pallas_call(
        body, in_specs=[ANY] * n, out_specs=[ANY] * n,
        out_shape=[jax.ShapeDtypeStruct(t.shape, t.dtype) for t in totals],
        input_output_aliases={i: i for i in range(n)},
        scratch_shapes=[pltpu.SemaphoreType.DMA((n,)), pltpu.SemaphoreType.DMA((n,))],
        name=name)(*totals)


def small_allreduce(buf, *, name):
    rows = buf.shape[0]

    def body(x_ref, o_ref, slots, send, recv):
        x, y, c = _me()
        me = 4 * x + 2 * y + c
        slots[me] = x_ref[...]
        sent = []
        for d in range(1, 8):
            peer = (1 - x if d & 4 else x, 1 - y if d & 2 else y, 1 - c if d & 1 else c)
            sent.append(_remote(x_ref, slots.at[me], send.at[d - 1], recv.at[d - 1], peer))
            sent[-1].start()
        for cp in sent:
            cp.wait()
        acc = slots[0]
        for k in range(1, 8):
            acc = acc + slots[k]
        o_ref[...] = acc

    return pl.pallas_call(
        body, out_shape=jax.ShapeDtypeStruct(buf.shape, F32),
        in_specs=[pl.BlockSpec(memory_space=pltpu.VMEM)], out_specs=pl.BlockSpec(memory_space=pltpu.VMEM),
        scratch_shapes=[pltpu.VMEM((8, rows, 128), F32), pltpu.SemaphoreType.DMA((7,)), pltpu.SemaphoreType.DMA((7,))],
        name=name)(buf)


def adamw(w, g, m, v, *, name, rt=None):
    rows, cols = w.shape
    rt = rows if rt is None else rt
    c1 = 1.0 - ADAM_B1 ** ADAM_STEP
    c2 = 1.0 - ADAM_B2 ** ADAM_STEP

    def body(w_ref, g_ref, m_ref, v_ref, d_ref, nm_ref, nv_ref):
        gg = g_ref[...]
        nm = ADAM_B1 * m_ref[...] + (1.0 - ADAM_B1) * gg
        nv = ADAM_B2 * v_ref[...] + (1.0 - ADAM_B2) * (gg * gg)
        nm_ref[...] = nm
        nv_ref[...] = nv
        d_ref[...] = -ADAM_LR * ((nm / c1) / (jnp.sqrt(nv / c2) + ADAM_EPS) + ADAM_WD * w_ref[...])

    spec = pl.BlockSpec((rt, cols), lambda i: (i, 0))
    return pl.pallas_call(
        body, grid=(rows // rt,), in_specs=[spec] * 4, out_specs=[spec] * 3,
        out_shape=[jax.ShapeDtypeStruct((rows, cols), F32)] * 3,
        compiler_params=_cparams("parallel"), name=name)(w, g, m, v)


WEIGHTS = ['ffn1_pre_g', 'ffn1_w_gate', 'ffn1_w_up', 'ffn1_w_down', 'ffn1_post_g', 'mix_pre_g', 'w_in', 'conv_w',
           'conv_b', 'dt_bias', 'a_log', 'd_skip', 'ssm_norm_g', 'w_ssm_proj', 'attn_sinks', 'rel_bias_table',
           'w_attn_proj', 'w_out', 'mix_post_g', 'ffn2_pre_g', 'ffn2_w_gate', 'ffn2_w_up', 'ffn2_w_down', 'ffn2_post_g']
BIG = ['ffn1_w_gate', 'ffn1_w_up', 'ffn1_w_down', 'w_in', 'w_ssm_proj', 'w_attn_proj', 'w_out',
       'ffn2_w_gate', 'ffn2_w_up', 'ffn2_w_down']
SMALL = [w for w in WEIGHTS if w not in BIG]


def _bucket_onehot():
    blk = ATTN_BLOCK
    dist = np.maximum(np.arange(blk)[:, None] + blk - np.arange(2 * blk)[None, :], 0)
    max_exact = REL_BUCKETS // 2
    d = np.maximum(dist, 1).astype(np.float32)
    large = max_exact + (np.log(d / np.float32(max_exact)) / np.float32(math.log(REL_MAX_DISTANCE / max_exact))
                         * np.float32(REL_BUCKETS - max_exact)).astype(np.int32)
    bucket = np.where(dist < max_exact, dist, np.minimum(large, REL_BUCKETS - 1)).reshape(-1)
    return jnp.asarray((bucket[None, :] == np.arange(REL_BUCKETS)[:, None]).astype(np.float32))


def _pack_rows(parts, mult=8):
    flat = jnp.concatenate([p.reshape(-1).astype(F32) for p in parts])
    rows = -(-flat.shape[0] // (128 * mult)) * mult
    return jnp.pad(flat, (0, rows * 128 - flat.shape[0])).reshape(rows, 128)


def _unpack_rows(buf, shapes):
    flat = buf.reshape(-1)
    out, at = [], 0
    for shp in shapes:
        size = int(np.prod(shp))
        out.append(flat[at:at + size].reshape(shp))
        at += size
    return out


def kernel(x, ffn1_pre_g, ffn1_w_gate, ffn1_w_up, ffn1_w_down, ffn1_post_g, mix_pre_g, w_in, conv_w, conv_b, dt_bias, a_log, d_skip, ssm_norm_g, w_ssm_proj, attn_sinks, rel_bias_table, w_attn_proj, w_out, mix_post_g, ffn2_pre_g, ffn2_w_gate, ffn2_w_up, ffn2_w_down, ffn2_post_g, loss_target, m_ffn1_pre_g, m_ffn1_w_gate, m_ffn1_w_up, m_ffn1_w_down, m_ffn1_post_g, m_mix_pre_g, m_w_in, m_conv_w, m_conv_b, m_dt_bias, m_a_log, m_d_skip, m_ssm_norm_g, m_w_ssm_proj, m_attn_sinks, m_rel_bias_table, m_w_attn_proj, m_w_out, m_mix_post_g, m_ffn2_pre_g, m_ffn2_w_gate, m_ffn2_w_up, m_ffn2_w_down, m_ffn2_post_g, v_ffn1_pre_g, v_ffn1_w_gate, v_ffn1_w_up, v_ffn1_w_down, v_ffn1_post_g, v_mix_pre_g, v_w_in, v_conv_w, v_conv_b, v_dt_bias, v_a_log, v_d_skip, v_ssm_norm_g, v_w_ssm_proj, v_attn_sinks, v_rel_bias_table, v_w_attn_proj, v_w_out, v_mix_post_g, v_ffn2_pre_g, v_ffn2_w_gate, v_ffn2_w_up, v_ffn2_w_down, v_ffn2_post_g):
    args = locals()
    w = {n: args[n] for n in WEIGHTS}
    m = {n: args["m_" + n] for n in WEIGHTS}
    v = {n: args["v_" + n] for n in WEIGHTS}
    batch, seq, D = x.shape
    T = batch * seq
    xi, yi, ci = _me()
    s_me = 2 * xi + yi
    x2 = x.reshape(T, D)
    tgt = loss_target.reshape(T, D)

    def own_slot(parts):
        p = jnp.concatenate([t[0] for t in parts], axis=0).astype(BF16)
        return lax.dynamic_update_slice(lax.empty((N_SHARD,) + p.shape, BF16), p[None], (s_me, 0, 0))

    g704_1, gdn_1 = run_ride(ag_ride([own_slot([ffn1_w_gate, ffn1_w_up]), own_slot([ffn1_w_down])]), name="ag_ffn1")
    ffn_parts = (0, 1, 0)
    col = lambda v: v.reshape(SSM_HEADS, 1)
    d_skip_x = jnp.repeat(d_skip, SSM_HEAD_DIM, axis=1)
    cw_slot = lax.dynamic_update_slice(jnp.zeros((SSM_CONV, SSM_CONV_DIM), F32),
                                       conv_w[0] * (ci == 0).astype(F32), (0, s_me * (SSM_CONV_DIM // N_SHARD)))
    conv_w_full = small_allreduce(cw_slot.reshape(-1, 128), name="ag_conv_w").reshape(SSM_CONV, SSM_CONV_DIM)

    (h1, n1, gate1, up1, f1), (gin, gmix) = ffn_fwd(
        x2, ffn1_pre_g, g704_1, gdn_1, ffn1_post_g, parts=ffn_parts, name="ffn1_fwd",
        ride=ag_ride([own_slot([w_in]), own_slot([w_ssm_proj, w_attn_proj, w_out])]))
    w_in_full = gin.transpose(1, 0, 2).reshape(D, IN_COLS)
    w_gz = w_in_full[:, 0:4096]
    w_xbc = w_in_full[:, 4096:4096 + SSM_CONV_DIM]
    w_dtT = w_in_full[:, 7168:7200].T
    w_qkv = w_in_full[:, 7200:]
    (u, gates, z, xbc, dt_rawT, q, k, vv), (g704_2, gdn_2) = mix_in_fwd(
        h1, mix_pre_g, w_gz, w_xbc, w_dtT, w_qkv, name="mix_in_fwd",
        ride=ag_ride([own_slot([ffn2_w_gate, ffn2_w_up]), own_slot([ffn2_w_down])]))
    y, ys, states = ssd_fwd(xbc, z, dt_rawT, conv_w_full, conv_b, col(dt_bias), col(a_log), d_skip_x, ssm_norm_g,
                            batch=batch, name="ssd_fwd")
    onehot = _bucket_onehot()
    bias = attn_bias(rel_bias_table.T, onehot, name="attn_bias").reshape(ATTN_Q_HEADS, ATTN_BLOCK, 2 * ATTN_BLOCK)
    o, lse = attn_fwd(q, k, vv, bias, attn_sinks, batch=batch, name="attn_fwd")
    h2, y_ssm, y_attn, mix, merged = mix_out_fwd(ys, o, gates, h1, gmix, mix_post_g, name="mix_out_fwd")
    h3, n3, gate2, up2, f2, dy, loss_parts = ffn_fwd(h2, ffn2_pre_g, g704_2, gdn_2, ffn2_post_g, tgt, parts=ffn_parts,
                                                     name="ffn2_fwd")

    where = jnp.stack([s_me, ci]).astype(jnp.int32)

    def rs_front(grads, tiles, tag):
        pair = rs_pair(grads, name="rs_pair_" + tag)
        return [rs_add(g, p, where, rt=rt, name=f"rs_add_{tag}{i}") for i, (g, p, rt) in enumerate(zip(grads, pair, tiles))]

    def rs_back(parts, sums, tiles, tag):
        totals = [rs_total(p, s, where, rt=rt, name=f"rs_total_{tag}{i}")
                  for i, (p, s, rt) in enumerate(zip(parts, sums, tiles))]
        return rs_share(totals, name="rs_share_" + tag)

    def ffn_grads(n, dgate, dup, a, df, tag):
        d704 = mm_tn(n[None], dgate, into=(lax.empty(g704_1.shape, F32), 0), name="dw_gate" + tag)
        d704 = mm_tn(n[None], dup, into=(d704, 1), name="dw_up" + tag)
        return [d704, mm_tn(a, df[None], name="dw_down" + tag)]

    ffn_tiles, mix_tiles = [512, 352], [256, 256]
    dh2, df2, a2, dgate2, dup2, dg_ffn2_pre, dg_ffn2_post = ffn_bwd(dy, h2, f2, gate2, up2, ffn2_pre_g, ffn2_post_g,
                                                                    g704_2, gdn_2, parts=ffn_parts, name="ffn2_bwd")
    sums_f2 = rs_front(ffn_grads(n3, dgate2, dup2, a2, df2, "2"), ffn_tiles, "f2")
    dmix, dyssm, dyattn, dgates, dys, do, dg_mix_post = mix_out_bwd(dh2, mix, y_ssm, y_attn, gates, gmix, mix_post_g,
                                                                    name="mix_out_bwd")
    dq, dk, dv, dbias, dsinks = attn_bwd(q, k, vv, o, do, lse, bias, attn_sinks, batch=batch, name="attn_bwd")
    dtable = attn_bias_bwd(dbias.reshape(ATTN_Q_HEADS, -1), onehot, name="attn_bias_bwd").T
    (dz, dxs, dbm, dcm, ddtT, acc_xs, acc_bm, acc_cm, acc_head), parts_f2 = ssd_bwd(
        dys, y, xbc, z, dt_rawT, states, conv_w_full, conv_b, col(dt_bias), col(a_log), d_skip_x, ssm_norm_g,
        batch=batch, name="ssd_bwd", ride=chips_ride(sums_f2))
    r704_2, rdn_2 = rs_back(parts_f2, sums_f2, ffn_tiles, "f2")
    dh1, dg_mix_pre = mix_in_bwd(dh2, h1, mix_pre_g, dgates, dz, dxs, dbm, dcm, ddtT, dq, dk, dv, w_gz, w_xbc, w_dtT,
                                 w_qkv, name="mix_in_bwd")
    dmx = mm_tn(ys[None], dyssm[None], a_cols=(N_SHARD, 512), into=(lax.empty(gmix.shape, F32), 0), name="dw_ssm")
    dmx = mm_tn(o[None], dyattn[None], a_cols=(N_SHARD, 256), into=(dmx, 2), name="dw_attn")
    dmx = mm_tn(merged[None], dmix[None], a_cols=(N_SHARD, 256), into=(dmx, 3), name="dw_out")
    ub = u[None]
    din = jnp.concatenate([
        mm_tn(ub, dgates[None], name="dw_in_gates", tn=1024)[0], mm_tn(ub, dz[None], name="dw_in_z", tn=1024)[0],
        mm_tn(ub, dxs[None], name="dw_in_xs", tn=1024)[0], mm_tn(ub, dbm[None], name="dw_in_b")[0],
        mm_tn(ub, dcm[None], name="dw_in_c")[0], mm_rows(ddtT, u, name="dw_in_dt").T,
        mm_tn(ub, dq[None], name="dw_in_q")[0], mm_tn(ub, dk[None], name="dw_in_k")[0],
        mm_tn(ub, dv[None], name="dw_in_v")[0]], axis=1)
    din = din.reshape(D, N_SHARD, IN_COLS // N_SHARD).transpose(1, 0, 2)
    sums_mx = rs_front([dmx, din], mix_tiles, "mx")
    (dx, df1, a1, dgate1, dup1, dg_ffn1_pre, dg_ffn1_post), parts_mx = ffn_bwd(
        dh1, x2, f1, gate1, up1, ffn1_pre_g, ffn1_post_g, g704_1, gdn_1, parts=ffn_parts, name="ffn1_bwd",
        ride=chips_ride(sums_mx))
    rmx, rin = rs_back(parts_mx, sums_mx, mix_tiles, "mx")
    sums_f1 = rs_front(ffn_grads(n1, dgate1, dup1, a1, df1, "1"), ffn_tiles, "f1")
    r704_1, rdn_1 = rs_back(run_ride(chips_ride(sums_f1), name="rs_chips_f1"), sums_f1, ffn_tiles, "f1")
    gw = {
        'ffn1_w_gate': r704_1[0:D], 'ffn1_w_up': r704_1[D:], 'ffn1_w_down': rdn_1,
        'ffn2_w_gate': r704_2[0:D], 'ffn2_w_up': r704_2[D:], 'ffn2_w_down': rdn_2,
        'w_ssm_proj': rmx[0:512], 'w_attn_proj': rmx[512:768], 'w_out': rmx[768:1024], 'w_in': rin,
    }

    dconv_w = jnp.concatenate([acc[:, :SSM_CONV].transpose(1, 0, 2).reshape(SSM_CONV, -1)
                               for acc in (acc_xs, acc_bm, acc_cm)], axis=1)
    dconv_b = jnp.concatenate([acc[:, SSM_CONV].reshape(-1) for acc in (acc_xs, acc_bm, acc_cm)])
    small_local = {
        'ffn1_pre_g': dg_ffn1_pre, 'ffn1_post_g': dg_ffn1_post, 'mix_pre_g': dg_mix_pre, 'conv_w': dconv_w,
        'conv_b': dconv_b, 'dt_bias': acc_head[:, :, 0], 'a_log': acc_head[:, :, 1],
        'd_skip': acc_xs[:, SSM_CONV + 2].reshape(SSM_HEADS, SSM_HEAD_DIM).sum(axis=1),
        'ssm_norm_g': acc_xs[:, SSM_CONV + 1].reshape(-1), 'attn_sinks': dsinks, 'rel_bias_table': dtable,
        'mix_post_g': dg_mix_post, 'ffn2_pre_g': dg_ffn2_pre, 'ffn2_post_g': dg_ffn2_post,
    }
    full_shapes = [(SSM_CONV, SSM_CONV_DIM) if n == 'conv_w' else w[n].shape for n in SMALL]
    packed = _pack_rows([small_local[n] for n in SMALL] + [jnp.sum(loss_parts[:, 0, 0])])
    total = small_allreduce(packed, name="allreduce_small")
    *small_g, loss = _unpack_rows(total, full_shapes + [()])
    for n, g in zip(SMALL, small_g):
        gw[n] = g
    gw['conv_w'] = lax.dynamic_slice(gw['conv_w'], (0, s_me * (SSM_CONV_DIM // N_SHARD)),
                                     (SSM_CONV, SSM_CONV_DIM // N_SHARD))[None]

    delta, new_m, new_v = {}, {}, {}
    for n in BIG:
        rows = w[n].shape[1]
        d_, m_, v_ = adamw(w[n][0], gw[n], m[n][0], v[n][0], name="adamw_" + n, rt=rows // 4)
        gw[n] = gw[n][None]
        delta[n], new_m[n], new_v[n] = d_[None], m_[None], v_[None]
    shapes = [w[n].shape for n in SMALL]
    outs = adamw(_pack_rows([w[n] for n in SMALL]), _pack_rows([gw[n] for n in SMALL]),
                 _pack_rows([m[n] for n in SMALL]), _pack_rows([v[n] for n in SMALL]), name="adamw_small")
    for res, buf in zip((delta, new_m, new_v), outs):
        for n, val in zip(SMALL, _unpack_rows(buf, shapes)):
            res[n] = val
    return (loss, dx.reshape(batch, seq, D), *[gw[n].reshape(w[n].shape) for n in WEIGHTS],
            *[delta[n] for n in WEIGHTS], *[new_m[n] for n in WEIGHTS], *[new_v[n] for n in WEIGHTS])
```

```python
import functools
import math

import jax
import jax.numpy as jnp
import numpy as np
from jax import lax
from jax.experimental import pallas as pl
from jax.experimental.pallas import tpu as pltpu

F32 = jnp.float32
BF16 = jnp.bfloat16

D_MODEL = 1024
D_FF = 2816
N_SHARD = 4
SSM_D_INNER = 2048
SSM_HEAD_DIM = 64
SSM_HEADS = 32
SSM_GROUPS = 4
SSM_HPG = SSM_HEADS // SSM_GROUPS
SSM_GW = SSM_D_INNER // SSM_GROUPS
SSM_STATE = 128
SSM_CONV = 4
SSM_CHUNK = 128
SSM_CONV_DIM = SSM_D_INNER + 2 * SSM_GROUPS * SSM_STATE
ATTN_Q_HEADS = 16
ATTN_KV_HEADS = 4
ATTN_REP = ATTN_Q_HEADS // ATTN_KV_HEADS
ATTN_HEAD_DIM = 64
ATTN_BLOCK = 128
ATTN_Q_DIM = 1024
ATTN_KV_DIM = 256
REL_BUCKETS = 32
REL_MAX_DISTANCE = 128
RMS_EPS = 1e-6
IN_COLS = 8736
ADAM_LR = 0.001
ADAM_B1 = 0.9
ADAM_B2 = 0.999
ADAM_EPS = 1e-08
ADAM_WD = 0.01
ADAM_STEP = 10
HALO = 8

VMEM_LIMIT = 56 * 1024 * 1024


def _cparams(*sem):
    return pltpu.CompilerParams(dimension_semantics=tuple(sem) if sem else None, vmem_limit_bytes=VMEM_LIMIT)


def _dot(a, b):
    return jnp.dot(a, b, preferred_element_type=F32)


def _dot_nt(a, b):
    return lax.dot_general(a, b, (((1,), (1,)), ((), ())), preferred_element_type=F32)


def _dot_tn(a, b):
    return lax.dot_general(a, b, (((0,), (0,)), ((), ())), preferred_element_type=F32)


def _dot_hi(a, b):
    return jnp.dot(a, b, preferred_element_type=F32, precision=lax.Precision.HIGHEST)


def _sigmoid(x):
    return 1.0 / (1.0 + jnp.exp(-x))


def _resident(shape, index=None):
    index = (0,) * len(shape) if index is None else tuple(index)
    return pl.BlockSpec(shape, lambda *_: index, pipeline_mode=pl.Buffered(1))


def _part(packed, rows, part):
    return _resident((N_SHARD, rows, packed.shape[2]), (0, part, 0))


def _rows(tm, width):
    return pl.BlockSpec((tm, width), lambda i: (i, 0))


class Ride:
    def __init__(self, inputs, out_shapes, aliases, scratch, start, finish):
        self.inputs, self.out_shapes, self.aliases = list(inputs), list(out_shapes), list(aliases)
        self.scratch, self.start, self.finish = list(scratch), start, finish


def _call(body, *, grid, in_specs, args, out_specs, out_shape, name, sem, scratch=(), aliases=None, ride=None):
    aliases = dict(aliases or {})
    if ride is None:
        return pl.pallas_call(body, grid=grid, in_specs=in_specs, out_specs=out_specs, out_shape=out_shape,
                              scratch_shapes=list(scratch), input_output_aliases=aliases,
                              compiler_params=_cparams(*sem), name=name)(*args)
    n_in, n_out, n_scr = len(in_specs), len(out_specs), len(scratch)
    k_in, k_out = len(ride.inputs), len(ride.out_shapes)

    def riding(*refs):
        ins, refs = refs[:n_in], refs[n_in:]
        ex_in, refs = refs[:k_in], refs[k_in:]
        outs, refs = refs[:n_out], refs[n_out:]
        ex_out, refs = refs[:k_out], refs[k_out:]
        scr, ex_scr = refs[:n_scr], refs[n_scr:]
        first = functools.reduce(jnp.logical_and, [pl.program_id(a) == 0 for a in range(len(grid))])
        last = functools.reduce(jnp.logical_and, [pl.program_id(a) == grid[a] - 1 for a in range(len(grid))])

        @pl.when(first)
        def _():
            ride.start(ex_in, ex_out, ex_scr)

        body(*ins, *outs, *scr)

        @pl.when(last)
        def _():
            ride.finish(ex_in, ex_out, ex_scr)

    aliases.update({n_in + i: n_out + j for i, j in ride.aliases})
    res = pl.pallas_call(
        riding, grid=grid, in_specs=list(in_specs) + [ANY] * k_in, out_specs=list(out_specs) + [ANY] * k_out,
        out_shape=list(out_shape) + ride.out_shapes, scratch_shapes=list(scratch) + ride.scratch,
        input_output_aliases=aliases, compiler_params=_cparams(*["arbitrary"] * len(grid)), name=name,
    )(*args, *ride.inputs)
    return res[:n_out], res[n_out:]


def run_ride(ride, *, name):
    k_in = len(ride.inputs)

    def body(*refs):
        ex_in, ex_out, sems = refs[:k_in], refs[k_in:k_in + len(ride.out_shapes)], refs[k_in + len(ride.out_shapes):]
        ride.start(ex_in, ex_out, sems)
        ride.finish(ex_in, ex_out, sems)

    return pl.pallas_call(body, in_specs=[ANY] * k_in, out_specs=[ANY] * len(ride.out_shapes),
                          out_shape=ride.out_shapes, scratch_shapes=ride.scratch,
                          input_output_aliases=dict(ride.aliases), name=name)(*ride.inputs)


def ffn_fwd(h, g_pre, wffn, g_post, target=None, *, name, tm=512, ride=None):
    T, D = h.shape
    NS, FS = N_SHARD, wffn.shape[1] // 3
    with_loss = target is not None
    nt = T // tm

    def body(*refs):
        if with_loss:
            (h_ref, gpre_ref, wg_ref, wu_ref, wd_ref, gpost_ref, tgt_ref,
             hout_ref, n_ref, gate_ref, up_ref, f_ref, dy_ref, loss_ref) = refs
        else:
            (h_ref, gpre_ref, wg_ref, wu_ref, wd_ref, gpost_ref,
             hout_ref, n_ref, gate_ref, up_ref, f_ref) = refs
        hh = h_ref[...]
        r = lax.rsqrt(jnp.mean(hh * hh, axis=-1, keepdims=True) + RMS_EPS)
        n = (hh * r * gpre_ref[...]).astype(BF16)
        n_ref[...] = n
        acc = jnp.zeros((tm, D), F32)
        for s in range(NS):
            gate = _dot_nt(n, wg_ref[s])
            up = _dot_nt(n, wu_ref[s])
            gate_ref[s] = gate.astype(BF16)
            up_ref[s] = up.astype(BF16)
            a = (gate * _sigmoid(gate) * up).astype(BF16)
            acc = acc + _dot(a, wd_ref[s])
        f_ref[...] = acc
        r2 = lax.rsqrt(jnp.mean(acc * acc, axis=-1, keepdims=True) + RMS_EPS)
        out = hh + 0.5 * (acc * r2 * gpost_ref[...])
        hout_ref[...] = out
        if with_loss:
            e = out - tgt_ref[...]
            dy_ref[...] = e * (1.0 / D)
            loss_ref[...] = jnp.full((1, 8, 128), 0.5 / D, F32) * jnp.sum(e * e)

    in_specs = [_rows(tm, D), _resident((1, D)), _part(wffn, FS, 0), _part(wffn, FS, 1), _part(wffn, FS, 2),
                _resident((1, D))]
    args = [h, g_pre, wffn, wffn, wffn, g_post]
    out_shape = [jax.ShapeDtypeStruct((T, D), F32), jax.ShapeDtypeStruct((T, D), BF16),
                 jax.ShapeDtypeStruct((NS, T, FS), BF16), jax.ShapeDtypeStruct((NS, T, FS), BF16),
                 jax.ShapeDtypeStruct((T, D), F32)]
    seg = pl.BlockSpec((NS, tm, FS), lambda i: (0, i, 0))
    out_specs = [_rows(tm, D), _rows(tm, D), seg, seg, _rows(tm, D)]
    if with_loss:
        in_specs.append(_rows(tm, D))
        args.append(target)
        out_shape += [jax.ShapeDtypeStruct((T, D), F32), jax.ShapeDtypeStruct((nt, 8, 128), F32)]
        out_specs += [_rows(tm, D), pl.BlockSpec((1, 8, 128), lambda i: (i, 0, 0))]
    return _call(body, grid=(nt,), in_specs=in_specs, args=args, out_specs=out_specs, out_shape=out_shape,
                 sem=("parallel",), name=name, ride=ride)


def ffn_bwd(dout, h, f, gate, up, g_pre, g_post, wffn, *, name, tm=256, ride=None):
    T, D = h.shape
    NS, FS = N_SHARD, wffn.shape[1] // 3
    nt = T // tm

    def body(dout_ref, h_ref, f_ref, gate_ref, up_ref, gpre_ref, gpost_ref, wg_ref, wu_ref, wd_ref,
             dh_ref, df_ref, a_ref, dgate_ref, dup_ref, dgpre_ref, dgpost_ref):
        @pl.when(pl.program_id(0) == 0)
        def _():
            dgpre_ref[...] = jnp.zeros_like(dgpre_ref)
            dgpost_ref[...] = jnp.zeros_like(dgpost_ref)

        do = dout_ref[...]
        ff = f_ref[...]
        d_fn = 0.5 * do
        r2 = lax.rsqrt(jnp.mean(ff * ff, axis=-1, keepdims=True) + RMS_EPS)
        dgpost_ref[...] += jnp.sum(d_fn * ff * r2, axis=0, keepdims=True)
        t = d_fn * gpost_ref[...]
        df = r2 * t - ff * (r2 * r2 * r2 * jnp.mean(t * ff, axis=-1, keepdims=True))
        dfb = df.astype(BF16)
        df_ref[...] = dfb
        dn = jnp.zeros((tm, D), F32)
        for s in range(NS):
            da = _dot_nt(dfb, wd_ref[s])
            g = gate_ref[s].astype(F32)
            u = up_ref[s].astype(F32)
            sg = _sigmoid(g)
            silu = g * sg
            a_ref[s] = (silu * u).astype(BF16)
            dgt = (da * u * (sg * (1.0 + g * (1.0 - sg)))).astype(BF16)
            dupv = (da * silu).astype(BF16)
            dgate_ref[s] = dgt
            dup_ref[s] = dupv
            dn = dn + _dot(dgt, wg_ref[s]) + _dot(dupv, wu_ref[s])
        hh = h_ref[...]
        r1 = lax.rsqrt(jnp.mean(hh * hh, axis=-1, keepdims=True) + RMS_EPS)
        dgpre_ref[...] += jnp.sum(dn * hh * r1, axis=0, keepdims=True)
        t = dn * gpre_ref[...]
        dh_ref[...] = do + r1 * t - hh * (r1 * r1 * r1 * jnp.mean(t * hh, axis=-1, keepdims=True))

    seg = pl.BlockSpec((NS, tm, FS), lambda i: (0, i, 0))
    acc = pl.BlockSpec((1, D), lambda i: (0, 0))
    return _call(
        body, grid=(nt,),
        in_specs=[_rows(tm, D), _rows(tm, D), _rows(tm, D), seg, seg, _resident((1, D)), _resident((1, D)),
                  _part(wffn, FS, 0), _part(wffn, FS, 1), _part(wffn, FS, 2)],
        args=[dout, h, f, gate, up, g_pre, g_post, wffn, wffn, wffn],
        out_specs=[_rows(tm, D), _rows(tm, D), seg, seg, seg, acc, acc],
        out_shape=[jax.ShapeDtypeStruct((T, D), F32), jax.ShapeDtypeStruct((T, D), BF16),
                   jax.ShapeDtypeStruct((NS, T, FS), BF16), jax.ShapeDtypeStruct((NS, T, FS), BF16),
                   jax.ShapeDtypeStruct((NS, T, FS), BF16),
                   jax.ShapeDtypeStruct((1, D), F32), jax.ShapeDtypeStruct((1, D), F32)],
        sem=("arbitrary",), name=name, ride=ride)


def mm_tn(a, g, *, name, tt=2048, tn=None, a_cols=None, into=None):
    Ba, T, _ = a.shape
    Bg, _, N = g.shape
    B, K = a_cols if a_cols else (max(Ba, Bg), a.shape[2])
    tn = N if tn is None else tn
    tt = min(tt, T)
    nsteps = T // tt

    def body(*refs):
        a_ref, g_ref, o_ref = refs[0], refs[1], refs[-1]

        @pl.when(pl.program_id(2) == 0)
        def _():
            o_ref[...] = jnp.zeros_like(o_ref)

        o_ref[0] += _dot_tn(a_ref[0], g_ref[0].astype(BF16))

    if a_cols:
        a_map = lambda b, j, t: (0, t, b)
    else:
        a_map = (lambda b, j, t: (b, t, 0)) if Ba > 1 else (lambda b, j, t: (0, t, 0))
    in_specs = [pl.BlockSpec((1, tt, K), a_map),
                pl.BlockSpec((1, tt, tn), (lambda b, j, t: (b, t, j)) if Bg > 1 else (lambda b, j, t: (0, t, j)))]
    args = [a, g]
    if into is None:
        out_shape, part, aliases = jax.ShapeDtypeStruct((B, K, N), F32), 0, {}
    else:
        buf, part = into
        out_shape, aliases = jax.ShapeDtypeStruct(buf.shape, F32), {2: 0}
        in_specs.append(ANY)
        args.append(buf)
    return pl.pallas_call(
        body, grid=(B, N // tn, nsteps), in_specs=in_specs,
        out_specs=pl.BlockSpec((1, K, tn), lambda b, j, t: (b, part, j)),
        out_shape=out_shape, input_output_aliases=aliases,
        compiler_params=_cparams("parallel", "parallel", "arbitrary"), name=name)(*args)


def mix_in_fwd(h, g, w_gz, w_xbc, w_dtT, w_qkv, *, name, tm=256, ride=None):
    T, D = h.shape
    nt = T // tm
    CB = 1024

    def body(h_ref, g_ref, wgz_ref, wxbc_ref, wdtT_ref, wqkv_ref,
             u_ref, gates_ref, z_ref, xbc_ref, dtT_ref, q_ref, k_ref, v_ref):
        hh = h_ref[...]
        r = lax.rsqrt(jnp.mean(hh * hh, axis=-1, keepdims=True) + RMS_EPS)
        u = (hh * r * g_ref[...]).astype(BF16)
        u_ref[...] = u
        for cb in range(0, 2048, CB):
            gates_ref[:, cb:cb + CB] = _dot(u, wgz_ref[:, cb:cb + CB])
            z_ref[:, cb:cb + CB] = _dot(u, wgz_ref[:, 2048 + cb:2048 + cb + CB])
        for cb in range(0, SSM_CONV_DIM, CB):
            xbc_ref[:, cb:cb + CB] = _dot(u, wxbc_ref[:, cb:cb + CB])
        dtT_ref[...] = _dot_nt(wdtT_ref[...], u)
        q_ref[...] = _dot(u, wqkv_ref[:, 0:ATTN_Q_DIM]).astype(BF16)
        k_ref[...] = _dot(u, wqkv_ref[:, ATTN_Q_DIM:ATTN_Q_DIM + ATTN_KV_DIM]).astype(BF16)
        v_ref[...] = _dot(u, wqkv_ref[:, ATTN_Q_DIM + ATTN_KV_DIM:]).astype(BF16)

    sds = jax.ShapeDtypeStruct
    return _call(
        body, grid=(nt,),
        in_specs=[_rows(tm, D), _resident((1, D)), _resident(w_gz.shape), _resident(w_xbc.shape),
                  _resident(w_dtT.shape), _resident(w_qkv.shape)],
        args=[h, g, w_gz, w_xbc, w_dtT, w_qkv],
        out_specs=[_rows(tm, D), _rows(tm, 2048), _rows(tm, 2048), _rows(tm, SSM_CONV_DIM),
                   pl.BlockSpec((SSM_HEADS, tm), lambda i: (0, i)),
                   _rows(tm, ATTN_Q_DIM), _rows(tm, ATTN_KV_DIM), _rows(tm, ATTN_KV_DIM)],
        out_shape=[sds((T, D), BF16), sds((T, 2048), F32), sds((T, 2048), F32), sds((T, SSM_CONV_DIM), F32),
                   sds((SSM_HEADS, T), F32),
                   sds((T, ATTN_Q_DIM), BF16), sds((T, ATTN_KV_DIM), BF16), sds((T, ATTN_KV_DIM), BF16)],
        sem=("parallel",), name=name, ride=ride)


def _softplus(x):
    return jnp.maximum(x, 0.0) + jnp.log(1.0 + jnp.exp(-jnp.abs(x)))


def _iota(shape, axis):
    return lax.broadcasted_iota(jnp.int32, shape, axis)


def _head_expand(g, per_head):
    shape = (SSM_HEADS, SSM_HPG * per_head)
    head = lax.shift_right_logical(_iota(shape, 1), int(math.log2(per_head)))
    return (_iota(shape, 0) == SSM_HPG * g + head).astype(F32)


def _conv_pre(x_ref, halo_ref, w_ref, b_ref, xp_ref, first):
    Q = SSM_CHUNK
    halo = jnp.where(first, 0.0, halo_ref[...])
    xp_ref[0:HALO, :] = halo
    xp_ref[HALO:HALO + Q, :] = x_ref[...]
    pre = b_ref[...] + w_ref[3:4, :] * xp_ref[HALO:HALO + Q, :]
    for k in range(SSM_CONV - 1):
        pre = pre + w_ref[k:k + 1, :] * xp_ref[pl.ds(HALO - 3 + k, Q), :]
    return pre


def _ssd_specs(nc):
    Q, GW, N = SSM_CHUNK, SSM_GW, SSM_STATE
    nb_xs = SSM_D_INNER // N
    nb_c = nb_xs + SSM_GROUPS

    def rb(cmap):
        def row(b, c, g):
            return b * nc + cmap(c)
        return row

    def specs(cmap):
        row = rb(cmap)
        hrow = lambda b, c, g: jnp.maximum(row(b, c, g) * (Q // HALO) - 1, 0)
        return dict(
            xs=pl.BlockSpec((Q, GW), lambda b, c, g: (row(b, c, g), g)),
            bm=pl.BlockSpec((Q, N), lambda b, c, g: (row(b, c, g), nb_xs + g)),
            cm=pl.BlockSpec((Q, N), lambda b, c, g: (row(b, c, g), nb_c + g)),
            xs_halo=pl.BlockSpec((HALO, GW), lambda b, c, g: (hrow(b, c, g), g)),
            bm_halo=pl.BlockSpec((HALO, N), lambda b, c, g: (hrow(b, c, g), nb_xs + g)),
            cm_halo=pl.BlockSpec((HALO, N), lambda b, c, g: (hrow(b, c, g), nb_c + g)),
            grp=pl.BlockSpec((Q, GW), lambda b, c, g: (row(b, c, g), g)),
            dt=pl.BlockSpec((Q, SSM_HEADS), lambda b, c, g: (row(b, c, g), 0)),
            dtT=pl.BlockSpec((SSM_HEADS, Q), lambda b, c, g: (0, row(b, c, g))),
            w_xs=pl.BlockSpec((SSM_CONV, GW), lambda b, c, g: (0, g)),
            w_bm=pl.BlockSpec((SSM_CONV, N), lambda b, c, g: (0, nb_xs + g)),
            w_cm=pl.BlockSpec((SSM_CONV, N), lambda b, c, g: (0, nb_c + g)),
            b_xs=pl.BlockSpec((1, GW), lambda b, c, g: (0, g)),
            b_bm=pl.BlockSpec((1, N), lambda b, c, g: (0, nb_xs + g)),
            b_cm=pl.BlockSpec((1, N), lambda b, c, g: (0, nb_c + g)),
            vec_g=pl.BlockSpec((1, GW), lambda b, c, g: (0, g)),
            row32=pl.BlockSpec((1, SSM_HEADS), lambda b, c, g: (0, 0)),
            col32=pl.BlockSpec((SSM_HEADS, 1), lambda b, c, g: (0, 0)),
            state=pl.BlockSpec((1, SSM_HPG, SSM_HEAD_DIM, N), lambda b, c, g: (row(b, c, g), g, 0, 0)),
            dtT_g=pl.BlockSpec((SSM_HPG, Q), lambda b, c, g: (g, row(b, c, g))),
            col_g=pl.BlockSpec((SSM_HPG, 1), lambda b, c, g: (g, 0)),
            pairs=pl.BlockSpec((1, SSM_HPG // 2, N, 2 * SSM_HEAD_DIM), lambda b, c, g: (row(b, c, g), g, 0, 0)),
        )
    return specs


def _ssd_chunk_common(first, g, xs_ref, bm_ref, cm_ref, xs_halo, bm_halo, cm_halo, w_xs, w_bm, w_cm, b_xs, b_bm, b_cm,
                      dt_ref, dtT_ref, dtb_ref, dtbT_ref, alog_ref, alogT_ref, xp_xs, xp_bm, xp_cm):
    Q = SSM_CHUNK
    pre_xs = _conv_pre(xs_ref, xs_halo, w_xs, b_xs, xp_xs, first)
    pre_bm = _conv_pre(bm_ref, bm_halo, w_bm, b_bm, xp_bm, first)
    pre_cm = _conv_pre(cm_ref, cm_halo, w_cm, b_cm, xp_cm, first)
    xs = pre_xs * _sigmoid(pre_xs)
    bm = pre_bm * _sigmoid(pre_bm)
    cm = pre_cm * _sigmoid(pre_cm)
    dtr = dt_ref[...] + dtb_ref[...]
    dtrT = dtT_ref[...] + dtbT_ref[...]
    dt = _softplus(dtr)
    dtT = _softplus(dtrT)
    a = -jnp.exp(alog_ref[...])
    aT = -jnp.exp(alogT_ref[...])
    tri = (_iota((Q, Q), 0) >= _iota((Q, Q), 1)).astype(F32)
    triT = (_iota((Q, Q), 0) <= _iota((Q, Q), 1)).astype(F32)
    acs = _dot_hi(tri, dt * a)
    acsT = _dot_hi(dtT * aT, triT)
    return dict(pre_xs=pre_xs, pre_bm=pre_bm, pre_cm=pre_cm, xs=xs, bm=bm, cm=cm, dtr=dtr, dt=dt, a=a,
                acs=acs, acsT=acsT, tri=tri, triT=triT)


def _v1_ssd_fwd(xbc, z, dt_raw, dt_rawT, conv_w, conv_b, dt_bias, a_log, d_skip, norm_g, *, batch, name):
    T = xbc.shape[0]
    Q, GW, N, P, HPG = SSM_CHUNK, SSM_GW, SSM_STATE, SSM_HEAD_DIM, SSM_HPG
    nc = T // batch // Q
    sp = _ssd_specs(nc)(lambda c: c)

    def body(xs_ref, bm_ref, cm_ref, xs_halo, bm_halo, cm_halo, z_ref, dt_ref, dtT_ref,
             w_xs, w_bm, w_cm, b_xs, b_bm, b_cm, dtb_ref, dtbT_ref, alog_ref, alogT_ref, dskip_ref, ng_ref,
             y_ref, ys_ref, st_ref, state, acsT_s, y_s, xp_xs, xp_bm, xp_cm):
        c = pl.program_id(1)
        g = pl.program_id(2)
        first = c == 0
        cc = _ssd_chunk_common(first, g, xs_ref, bm_ref, cm_ref, xs_halo, bm_halo, cm_halo, w_xs, w_bm, w_cm,
                               b_xs, b_bm, b_cm, dt_ref, dtT_ref, dtb_ref, dtbT_ref, alog_ref, alogT_ref,
                               xp_xs, xp_bm, xp_cm)
        xs, acs = cc["xs"], cc["acs"]
        acsT_s[...] = cc["acsT"]
        e64 = _head_expand(g, P)
        e128 = _head_expand(g, Q)
        acs_x = _dot_hi(acs, e64)
        acs_b = _dot_hi(acs, e128)
        x = xs * _dot_hi(cc["dt"], e64)
        last_x = acs_x[Q - 1:Q, :]
        xw = (x * jnp.exp(last_x - acs_x)).astype(BF16)
        ex = jnp.exp(acs_x)
        xb = x.astype(BF16)
        bb = cc["bm"].astype(BF16)
        cb = cc["cm"].astype(BF16)
        s = _dot_nt(cb, bb)
        causal = _iota((Q, Q), 0) >= _iota((Q, Q), 1)
        for r in range(HPG):
            hd = HPG * g + r

            @pl.when(first)
            def _():
                state[hd] = jnp.zeros((P, N), F32)

            seg = acs_b[:, Q * r:Q * (r + 1)] - acsT_s[pl.ds(hd, 1), :]
            m = (s * jnp.exp(jnp.where(causal, seg, -1e30))).astype(BF16)
            hp = state[hd]
            st_ref[0, r] = hp
            y_h = _dot(m, xb[:, P * r:P * (r + 1)]) + _dot_nt(cb, hp.astype(BF16)) * ex[:, P * r:P * (r + 1)]
            y_s[:, P * r:P * (r + 1)] = y_h
            decay = jnp.exp(acsT_s[pl.ds(hd, 1), pl.ds(Q - 1, 1)])
            state[hd] = hp * decay + _dot_tn(xw[:, P * r:P * (r + 1)], bb)
        dexp = _dot_hi(jnp.broadcast_to(dskip_ref[...], (8, SSM_HEADS)), e64)[0:1, :]
        y = y_s[...] + dexp * xs
        y_ref[...] = y
        zz = z_ref[...]
        yg = y * (zz * _sigmoid(zz))
        rr = lax.rsqrt(jnp.mean(yg * yg, axis=-1, keepdims=True) + RMS_EPS)
        ys_ref[...] = (yg * rr * ng_ref[...]).astype(BF16)

    col = lambda v: v.reshape(SSM_HEADS, 1)
    sds = jax.ShapeDtypeStruct
    return pl.pallas_call(
        body, grid=(batch, nc, SSM_GROUPS),
        in_specs=[sp["xs"], sp["bm"], sp["cm"], sp["xs_halo"], sp["bm_halo"], sp["cm_halo"], sp["grp"], sp["dt"],
                  sp["dtT"], sp["w_xs"], sp["w_bm"], sp["w_cm"], sp["b_xs"], sp["b_bm"], sp["b_cm"],
                  sp["row32"], sp["col32"], sp["row32"], sp["col32"], sp["row32"], sp["vec_g"]],
        out_specs=[sp["grp"], sp["grp"], sp["state"]],
        out_shape=[sds((T, SSM_D_INNER), F32), sds((T, SSM_D_INNER), BF16),
                   sds((T // Q, SSM_HEADS, P, N), F32)],
        scratch_shapes=[pltpu.VMEM((SSM_HEADS, P, N), F32), pltpu.VMEM((SSM_HEADS, Q), F32), pltpu.VMEM((Q, GW), F32),
                        pltpu.VMEM((HALO + Q, GW), F32), pltpu.VMEM((HALO + Q, N), F32), pltpu.VMEM((HALO + Q, N), F32)],
        compiler_params=_cparams("arbitrary", "arbitrary", "arbitrary"), name=name,
    )(xbc, xbc, xbc, xbc, xbc, xbc, z, dt_raw, dt_rawT, conv_w, conv_w, conv_w, conv_b, conv_b, conv_b,
      dt_bias, col(dt_bias), a_log, col(a_log), d_skip, norm_g)


def _attn_specs(nb):
    BLK = ATTN_BLOCK

    def specs(last):
        def cur(b, n):
            return b * nb + (n if last is None else jnp.minimum(n, nb - 1))

        def prev(b, n):
            return b * nb + jnp.maximum((n if last is None else jnp.minimum(n, nb - 1)) - 1, 0)
        return cur, prev
    return specs


def _attn_masks(n):
    shape = (ATTN_REP * ATTN_BLOCK, ATTN_BLOCK)
    ii = jnp.bitwise_and(_iota(shape, 0), ATTN_BLOCK - 1)
    jj = _iota(shape, 1)
    return jnp.logical_and(jj > ii, n > 0), jj <= ii


def _attn_group(kk, q_ref, bias_ref, sink_ref):
    BLK, HD = ATTN_BLOCK, ATTN_HEAD_DIM
    heads = range(ATTN_REP * kk, ATTN_REP * (kk + 1))
    qg = jnp.concatenate([q_ref[:, HD * hd:HD * (hd + 1)] for hd in heads], axis=0)
    bias_p = jnp.concatenate([bias_ref[hd, :, 0:BLK] for hd in heads], axis=0)
    bias_c = jnp.concatenate([bias_ref[hd, :, BLK:2 * BLK] for hd in heads], axis=0)
    sink = jnp.concatenate([jnp.broadcast_to(sink_ref[0:1, hd:hd + 1], (BLK, 1)) for hd in heads], axis=0)
    return qg, bias_p, bias_c, sink


def attn_bias(table_t, onehot, *, name):
    def body(t_ref, f_ref, o_ref):
        o_ref[...] = _dot_hi(t_ref[...], f_ref[...])
    return pl.pallas_call(body, out_shape=jax.ShapeDtypeStruct((ATTN_Q_HEADS, onehot.shape[1]), F32),
                          compiler_params=_cparams(), name=name)(table_t, onehot)


def attn_bias_bwd(dbias, onehot, *, name):
    def body(d_ref, f_ref, o_ref):
        o_ref[...] = lax.dot_general(d_ref[...], f_ref[...], (((1,), (1,)), ((), ())), preferred_element_type=F32,
                                     precision=lax.Precision.HIGHEST)
    return pl.pallas_call(body, out_shape=jax.ShapeDtypeStruct((ATTN_Q_HEADS, REL_BUCKETS), F32),
                          compiler_params=_cparams(), name=name)(dbias, onehot)


def attn_fwd(q, k, v, bias, sinks, *, batch, name):
    T = q.shape[0]
    BLK, HD = ATTN_BLOCK, ATTN_HEAD_DIM
    nb = T // batch // BLK
    cur, prev = _attn_specs(nb)(None)
    scale = HD ** -0.5

    def body(q_ref, kc_ref, kp_ref, vc_ref, vp_ref, bias_ref, sink_ref, o_ref, lse_ref):
        n = pl.program_id(1)
        m_prev, m_cur = _attn_masks(n)
        for kk in range(ATTN_KV_HEADS):
            ks = slice(HD * kk, HD * (kk + 1))
            kc, kp, vc, vp = kc_ref[:, ks], kp_ref[:, ks], vc_ref[:, ks], vp_ref[:, ks]
            qg, bias_p, bias_c, sink = _attn_group(kk, q_ref, bias_ref, sink_ref)
            lp = jnp.where(m_prev, _dot_nt(qg, kp) * scale + bias_p, -1e30)
            lc = jnp.where(m_cur, _dot_nt(qg, kc) * scale + bias_c, -1e30)
            mx = jnp.maximum(jnp.max(jnp.maximum(lp, lc), axis=-1, keepdims=True), sink)
            pp = jnp.exp(lp - mx)
            pc = jnp.exp(lc - mx)
            den = jnp.sum(pp + pc, axis=-1, keepdims=True) + jnp.exp(sink - mx)
            o = ((_dot(pp.astype(BF16), vp) + _dot(pc.astype(BF16), vc)) * (1.0 / den)).astype(BF16)
            lse = mx + jnp.log(den)
            for r in range(ATTN_REP):
                hd = ATTN_REP * kk + r
                o_ref[:, HD * hd:HD * (hd + 1)] = o[BLK * r:BLK * (r + 1)]
                lse_ref[:, hd:hd + 1] = lse[BLK * r:BLK * (r + 1)]

    sds = jax.ShapeDtypeStruct
    return pl.pallas_call(
        body, grid=(batch, nb),
        in_specs=[pl.BlockSpec((BLK, ATTN_Q_DIM), lambda b, n: (cur(b, n), 0)),
                  pl.BlockSpec((BLK, ATTN_KV_DIM), lambda b, n: (cur(b, n), 0)),
                  pl.BlockSpec((BLK, ATTN_KV_DIM), lambda b, n: (prev(b, n), 0)),
                  pl.BlockSpec((BLK, ATTN_KV_DIM), lambda b, n: (cur(b, n), 0)),
                  pl.BlockSpec((BLK, ATTN_KV_DIM), lambda b, n: (prev(b, n), 0)),
                  pl.BlockSpec((ATTN_Q_HEADS, BLK, 2 * BLK), lambda b, n: (0, 0, 0)),
                  pl.BlockSpec((1, ATTN_Q_HEADS), lambda b, n: (0, 0))],
        out_specs=[pl.BlockSpec((BLK, ATTN_Q_DIM), lambda b, n: (cur(b, n), 0)),
                   pl.BlockSpec((BLK, ATTN_Q_HEADS), lambda b, n: (cur(b, n), 0))],
        out_shape=[sds((T, ATTN_Q_DIM), BF16), sds((T, ATTN_Q_HEADS), F32)],
        compiler_params=_cparams("parallel", "parallel"), name=name)(q, k, k, v, v, bias, sinks)


def _proj_specs(wmix):
    return [_part(wmix, 512, 0), _part(wmix, 256, 2), _part(wmix, 256, 3)]


def _natural(w_ref):
    return w_ref[...].reshape(-1, w_ref.shape[2])


def mix_out_fwd(ys, o, gates, h, wmix, g_post, *, name, tm=512):
    T, D = h.shape
    nt = T // tm

    def body(ys_ref, o_ref, gates_ref, h_ref, wssm_ref, wattn_ref, wout_ref, g_ref,
             hout_ref, yssm_ref, yattn_ref, mix_ref, merged_ref):
        y_ssm = _dot(ys_ref[...], _natural(wssm_ref))
        y_attn = _dot(o_ref[...], _natural(wattn_ref))
        yssm_ref[...] = y_ssm
        yattn_ref[...] = y_attn
        merged = (_sigmoid(gates_ref[:, 0:D]) * y_ssm + _sigmoid(gates_ref[:, D:2 * D]) * y_attn).astype(BF16)
        merged_ref[...] = merged
        mix = _dot(merged, _natural(wout_ref))
        mix_ref[...] = mix
        r = lax.rsqrt(jnp.mean(mix * mix, axis=-1, keepdims=True) + RMS_EPS)
        hout_ref[...] = h_ref[...] + mix * r * g_ref[...]

    sds = jax.ShapeDtypeStruct
    return pl.pallas_call(
        body, grid=(nt,),
        in_specs=[_rows(tm, SSM_D_INNER), _rows(tm, ATTN_Q_DIM), _rows(tm, 2 * D), _rows(tm, D),
                  *_proj_specs(wmix), _resident((1, D))],
        out_specs=[_rows(tm, D)] * 5,
        out_shape=[sds((T, D), F32), sds((T, D), F32), sds((T, D), F32), sds((T, D), F32), sds((T, D), BF16)],
        compiler_params=_cparams("parallel"), name=name)(ys, o, gates, h, wmix, wmix, wmix, g_post)


def mix_out_bwd(dh, mix, y_ssm, y_attn, gates, wmix, g_post, *, name, tm=256):
    T, D = dh.shape
    nt = T // tm

    def body(dh_ref, mix_ref, yssm_ref, yattn_ref, gates_ref, wssm_ref, wattn_ref, wout_ref, g_ref,
             dmix_ref, dyssm_ref, dyattn_ref, dgates_ref, dys_ref, do_ref, dg_ref):
        @pl.when(pl.program_id(0) == 0)
        def _():
            dg_ref[...] = jnp.zeros_like(dg_ref)

        do = dh_ref[...]
        mix = mix_ref[...]
        r = lax.rsqrt(jnp.mean(mix * mix, axis=-1, keepdims=True) + RMS_EPS)
        dg_ref[...] += jnp.sum(do * mix * r, axis=0, keepdims=True)
        t = do * g_ref[...]
        dmix = (r * t - mix * (r * r * r * jnp.mean(t * mix, axis=-1, keepdims=True))).astype(BF16)
        dmix_ref[...] = dmix
        dmerged = _dot_nt(dmix, _natural(wout_ref))
        s1 = _sigmoid(gates_ref[:, 0:D])
        s2 = _sigmoid(gates_ref[:, D:2 * D])
        dyssm = (dmerged * s1).astype(BF16)
        dyattn = (dmerged * s2).astype(BF16)
        dyssm_ref[...] = dyssm
        dyattn_ref[...] = dyattn
        dgates_ref[:, 0:D] = (dmerged * yssm_ref[...] * (s1 * (1.0 - s1))).astype(BF16)
        dgates_ref[:, D:2 * D] = (dmerged * yattn_ref[...] * (s2 * (1.0 - s2))).astype(BF16)
        dys_ref[...] = _dot_nt(dyssm, _natural(wssm_ref))
        do_ref[...] = _dot_nt(dyattn, _natural(wattn_ref)).astype(BF16)

    sds = jax.ShapeDtypeStruct
    return pl.pallas_call(
        body, grid=(nt,),
        in_specs=[_rows(tm, D), _rows(tm, D), _rows(tm, D), _rows(tm, D), _rows(tm, 2 * D),
                  *_proj_specs(wmix), _resident((1, D))],
        out_specs=[_rows(tm, D), _rows(tm, D), _rows(tm, D), _rows(tm, 2 * D), _rows(tm, SSM_D_INNER),
                   _rows(tm, ATTN_Q_DIM), pl.BlockSpec((1, D), lambda i: (0, 0))],
        out_shape=[sds((T, D), BF16), sds((T, D), BF16), sds((T, D), BF16), sds((T, 2 * D), BF16),
                   sds((T, SSM_D_INNER), F32), sds((T, ATTN_Q_DIM), BF16), sds((1, D), F32)],
        compiler_params=_cparams("arbitrary"), name=name)(dh, mix, y_ssm, y_attn, gates, wmix, wmix, wmix, g_post)


def attn_bwd(q, k, v, o, do, lse, bias, sinks, *, batch, name):
    T = q.shape[0]
    BLK, HD = ATTN_BLOCK, ATTN_HEAD_DIM
    nb = T // batch // BLK
    cur, prev = _attn_specs(nb)(nb)
    scale = HD ** -0.5

    def body(q_ref, kc_ref, kp_ref, vc_ref, vp_ref, o_ref, do_ref, lse_ref, bias_ref, sink_ref,
             dq_ref, dk_ref, dv_ref, dbias_ref, dsink_ref, ck, cv):
        b = pl.program_id(0)
        n = pl.program_id(1)

        @pl.when(jnp.logical_and(b == 0, n == 0))
        def _():
            dbias_ref[...] = jnp.zeros_like(dbias_ref)
            dsink_ref[...] = jnp.zeros_like(dsink_ref)

        @pl.when(n == 0)
        def _():
            ck[...] = jnp.zeros_like(ck)
            cv[...] = jnp.zeros_like(cv)

        @pl.when(n == nb)
        def _():
            dk_ref[...] = ck[...].astype(BF16)
            dv_ref[...] = cv[...].astype(BF16)

        @pl.when(n < nb)
        def _():
            m_prev, m_cur = _attn_masks(n)
            lane16 = _iota((1, ATTN_Q_HEADS), 1)
            dsink = jnp.zeros((1, ATTN_Q_HEADS), F32)
            for kk in range(ATTN_KV_HEADS):
                ks = slice(HD * kk, HD * (kk + 1))
                kc, kp, vc, vp = kc_ref[:, ks], kp_ref[:, ks], vc_ref[:, ks], vp_ref[:, ks]
                heads = range(ATTN_REP * kk, ATTN_REP * (kk + 1))
                qg, bias_p, bias_c, sink = _attn_group(kk, q_ref, bias_ref, sink_ref)
                dog = jnp.concatenate([do_ref[:, HD * hd:HD * (hd + 1)] for hd in heads], axis=0)
                og = jnp.concatenate([o_ref[:, HD * hd:HD * (hd + 1)] for hd in heads], axis=0)
                lse = jnp.concatenate([lse_ref[:, hd:hd + 1] for hd in heads], axis=0)
                lp = jnp.where(m_prev, _dot_nt(qg, kp) * scale + bias_p, -1e30)
                lc = jnp.where(m_cur, _dot_nt(qg, kc) * scale + bias_c, -1e30)
                pp = jnp.exp(lp - lse)
                pc = jnp.exp(lc - lse)
                delta = jnp.sum(dog.astype(F32) * og.astype(F32), axis=-1, keepdims=True)
                dlp = pp * (_dot_nt(dog, vp) - delta)
                dlc = pc * (_dot_nt(dog, vc) - delta)
                sd = jnp.exp(sink - lse) * delta
                dlpb = dlp.astype(BF16)
                dlcb = dlc.astype(BF16)
                dqg = ((_dot(dlpb, kp) + _dot(dlcb, kc)) * scale).astype(BF16)
                for r, hd in enumerate(heads):
                    rows = slice(BLK * r, BLK * (r + 1))
                    dsink = dsink + jnp.where(lane16 == hd, -jnp.sum(sd[rows], axis=0, keepdims=True), 0.0)
                    dbias_ref[hd, :, 0:BLK] += dlp[rows]
                    dbias_ref[hd, :, BLK:2 * BLK] += dlc[rows]
                    dq_ref[:, HD * hd:HD * (hd + 1)] = dqg[rows]
                dk_ref[:, ks] = (ck[:, ks] + _dot_tn(dlpb, qg) * scale).astype(BF16)
                dv_ref[:, ks] = (cv[:, ks] + _dot_tn(pp.astype(BF16), dog)).astype(BF16)
                ck[:, ks] = _dot_tn(dlcb, qg) * scale
                cv[:, ks] = _dot_tn(pc.astype(BF16), dog)
            dsink_ref[...] += dsink

    sds = jax.ShapeDtypeStruct
    qspec = pl.BlockSpec((BLK, ATTN_Q_DIM), lambda b, n: (cur(b, n), 0))
    cspec = pl.BlockSpec((BLK, ATTN_KV_DIM), lambda b, n: (cur(b, n), 0))
    pspec = pl.BlockSpec((BLK, ATTN_KV_DIM), lambda b, n: (prev(b, n), 0))
    late = pl.BlockSpec((BLK, ATTN_KV_DIM), lambda b, n: (b * nb + jnp.maximum(n - 1, 0), 0))
    return pl.pallas_call(
        body, grid=(batch, nb + 1),
        in_specs=[qspec, cspec, pspec, cspec, pspec, qspec, qspec,
                  pl.BlockSpec((BLK, ATTN_Q_HEADS), lambda b, n: (cur(b, n), 0)),
                  pl.BlockSpec((ATTN_Q_HEADS, BLK, 2 * BLK), lambda b, n: (0, 0, 0)),
                  pl.BlockSpec((1, ATTN_Q_HEADS), lambda b, n: (0, 0))],
        out_specs=[qspec, late, late,
                   pl.BlockSpec((ATTN_Q_HEADS, BLK, 2 * BLK), lambda b, n: (0, 0, 0)),
                   pl.BlockSpec((1, ATTN_Q_HEADS), lambda b, n: (0, 0))],
        out_shape=[sds((T, ATTN_Q_DIM), BF16), sds((T, ATTN_KV_DIM), BF16), sds((T, ATTN_KV_DIM), BF16),
                   sds((ATTN_Q_HEADS, BLK, 2 * BLK), F32), sds((1, ATTN_Q_HEADS), F32)],
        scratch_shapes=[pltpu.VMEM((BLK, ATTN_KV_DIM), F32), pltpu.VMEM((BLK, ATTN_KV_DIM), F32)],
        compiler_params=_cparams("arbitrary", "arbitrary"), name=name)(q, k, k, v, v, o, do, lse, bias, sinks)


def _conv_bwd(dxc, pre, xp_ref, w_ref, carry_ref, acc_ref, dp_ref, g, last):
    Q = SSM_CHUNK
    sg = _sigmoid(pre)
    dpre = dxc * (sg * (1.0 + pre * (1.0 - sg)))
    dp_ref[0:Q, :] = dpre
    dp_ref[Q:Q + HALO, :] = carry_ref[g]
    carry_ref[g] = dpre[0:HALO, :]
    rows = [jnp.sum(dpre * xp_ref[pl.ds(HALO - 3 + k, Q), :], axis=0, keepdims=True) for k in range(SSM_CONV)]
    rows.append(jnp.sum(dpre, axis=0, keepdims=True))
    rows.append(jnp.zeros((HALO - SSM_CONV - 1, dpre.shape[1]), F32))
    acc_ref[g] += jnp.concatenate(rows, axis=0)
    dx = w_ref[3:4, :] * dpre
    for k in range(SSM_CONV - 1):
        dx = dx + w_ref[k:k + 1, :] * dp_ref[pl.ds(3 - k, Q), :]
    return dx


def _v1_ssd_bwd(dys, y, xbc, z, dt_raw, dt_rawT, states, conv_w, conv_b, dt_bias, a_log, d_skip, norm_g, *, batch, name):
    T = xbc.shape[0]
    Q, GW, N, P, HPG, G, H = SSM_CHUNK, SSM_GW, SSM_STATE, SSM_HEAD_DIM, SSM_HPG, SSM_GROUPS, SSM_HEADS
    nc = T // batch // Q
    sp = _ssd_specs(nc)(lambda c: nc - 1 - c)

    def body(xs_ref, bm_ref, cm_ref, xs_halo, bm_halo, cm_halo, z_ref, y_ref, dys_ref, dt_ref, dtT_ref, st_ref,
             w_xs, w_bm, w_cm, b_xs, b_bm, b_cm, dtb_ref, dtbT_ref, alog_ref, alogT_ref, dskip_ref, ng_ref,
             dz_ref, dxs_ref, dbm_ref, dcm_ref, ddt_ref, acc_xs, acc_bm, acc_cm, acc_head,
             dstate, acsT_s, dacsT_s, yoff_s, dxw_s, dx_s, xp_xs, xp_bm, xp_cm, dp_xs, dp_bm, dp_cm,
             cy_xs, cy_bm, cy_cm):
        b = pl.program_id(0)
        cr = pl.program_id(1)
        g = pl.program_id(2)
        c = nc - 1 - cr
        first = c == 0
        last = cr == 0

        @pl.when(jnp.logical_and(jnp.logical_and(b == 0, cr == 0), g == 0))
        def _():
            acc_xs[...] = jnp.zeros_like(acc_xs)
            acc_bm[...] = jnp.zeros_like(acc_bm)
            acc_cm[...] = jnp.zeros_like(acc_cm)
            acc_head[...] = jnp.zeros_like(acc_head)

        cc = _ssd_chunk_common(first, g, xs_ref, bm_ref, cm_ref, xs_halo, bm_halo, cm_halo, w_xs, w_bm, w_cm,
                               b_xs, b_bm, b_cm, dt_ref, dtT_ref, dtb_ref, dtbT_ref, alog_ref, alogT_ref,
                               xp_xs, xp_bm, xp_cm)
        xs, acs, dt, a = cc["xs"], cc["acs"], cc["dt"], cc["a"]
        acsT_s[...] = cc["acsT"]
        dacsT_s[...] = jnp.zeros_like(dacsT_s)
        e64 = _head_expand(g, P)
        e128 = _head_expand(g, Q)
        acs_x = _dot_hi(acs, e64)
        acs_b = _dot_hi(acs, e128)
        dt_x = _dot_hi(dt, e64)
        x = xs * dt_x
        w_x = jnp.exp(acs_x[Q - 1:Q, :] - acs_x)
        ex = jnp.exp(acs_x)

        yv = y_ref[...]
        zz = z_ref[...]
        sz = _sigmoid(zz)
        silu_z = zz * sz
        yg = yv * silu_z
        rr = lax.rsqrt(jnp.mean(yg * yg, axis=-1, keepdims=True) + RMS_EPS)
        dys_v = dys_ref[...]
        d_ng = jnp.sum(dys_v * yg * rr, axis=0, keepdims=True)
        t = dys_v * ng_ref[...]
        dyg = rr * t - yg * (rr * rr * rr * jnp.mean(t * yg, axis=-1, keepdims=True))
        dy = dyg * silu_z
        dz_ref[...] = (dyg * yv * (sz * (1.0 + zz * (1.0 - sz)))).astype(BF16)

        dexp = _dot_hi(jnp.broadcast_to(dskip_ref[...], (8, H)), e64)[0:1, :]
        d_dskip = _dot_nt(jnp.broadcast_to(jnp.sum(dy * xs, axis=0, keepdims=True), (8, GW)), e64)[0:1, :]

        dyb = dy.astype(BF16)
        xb = x.astype(BF16)
        xwb = (x * w_x).astype(BF16)
        bb = cc["bm"].astype(BF16)
        cb = cc["cm"].astype(BF16)
        s = _dot_nt(cb, bb)
        causal = _iota((Q, Q), 0) >= _iota((Q, Q), 1)
        lane_h = _iota((1, H), 1)
        ds_acc = jnp.zeros((Q, Q), F32)
        d_c = jnp.zeros((Q, N), F32)
        d_b = jnp.zeros((Q, N), F32)
        dacs = jnp.zeros((Q, H), F32)
        last_terms = jnp.zeros((1, H), F32)
        for r in range(HPG):
            hd = HPG * g + r
            cols = slice(P * r, P * (r + 1))

            @pl.when(last)
            def _():
                dstate[hd] = jnp.zeros((P, N), F32)

            seg = acs_b[:, Q * r:Q * (r + 1)] - acsT_s[pl.ds(hd, 1), :]
            l = jnp.exp(jnp.where(causal, seg, -1e30))
            m = s * l
            mb = m.astype(BF16)
            dyh = dyb[:, cols]
            hp = st_ref[0, r]
            hpb = hp.astype(BF16)
            dh = dstate[hd]
            dhb = dh.astype(BF16)
            yoff_s[:, cols] = _dot_nt(cb, hpb) * ex[:, cols]
            dye = (dy[:, cols] * ex[:, cols]).astype(BF16)
            d_c = d_c + _dot(dye, hpb)
            dhp_off = _dot_tn(dye, cb)
            dm = _dot_nt(dyh, xb[:, cols])
            dx_s[:, cols] = _dot_tn(mb, dyh)
            gmat = dm * m
            onehot = (lane_h == hd).astype(F32)
            dacs = dacs + jnp.sum(gmat, axis=-1, keepdims=True) * onehot
            dacsT_s[pl.ds(hd, 1), :] = -jnp.sum(gmat, axis=0, keepdims=True)
            ds_acc = ds_acc + dm * l
            dxw_s[:, cols] = _dot_nt(bb, dhb)
            d_b = d_b + _dot(xwb[:, cols], dhb)
            decay = jnp.exp(acsT_s[pl.ds(hd, 1), pl.ds(Q - 1, 1)])
            ddecay = jnp.sum(jnp.sum(dh * hp, axis=-1, keepdims=True), axis=0, keepdims=True)
            last_terms = last_terms + (ddecay * decay) * onehot
            dstate[hd] = dh * decay + dhp_off
        dsb = ds_acc.astype(BF16)
        d_c = d_c + _dot(dsb, bb)
        d_b = d_b + _dot_tn(dsb, cb)
        dxw = dxw_s[...]
        dx_full = dx_s[...] + dxw * w_x
        tw = _dot_nt(dxw * x * w_x, e64)
        dacs = dacs + _dot_nt(dy * yoff_s[...], e64) - tw
        last_terms = last_terms + jnp.sum(tw, axis=0, keepdims=True)
        eye = (_iota((Q, Q), 0) == _iota((Q, Q), 1)).astype(F32)
        dacs = dacs + lax.dot_general(eye, dacsT_s[...], (((1,), (1,)), ((), ())), preferred_element_type=F32,
                                      precision=lax.Precision.HIGHEST)
        dacs = dacs + jnp.where(_iota((Q, 1), 0) == Q - 1, 1.0, 0.0) * last_terms
        d_dta = _dot_hi(cc["triT"], dacs)
        ddt = d_dta * a + _dot_nt(dx_full * xs, e64)
        d_alog = jnp.sum(d_dta * dt, axis=0, keepdims=True) * a
        ddt_raw = ddt * _sigmoid(cc["dtr"])
        d_dtb = jnp.sum(ddt_raw, axis=0, keepdims=True)

        @pl.when(g == 0)
        def _():
            ddt_ref[...] = ddt_raw

        @pl.when(g > 0)
        def _():
            ddt_ref[...] += ddt_raw

        acc_head[...] += jnp.concatenate([d_dtb, d_alog, d_dskip, jnp.zeros((5, H), F32)], axis=0)
        dxs = dexp * dy + dx_full * dt_x
        dxs_ref[...] = _conv_bwd(dxs, cc["pre_xs"], xp_xs, w_xs, cy_xs, acc_xs, dp_xs, g, last).astype(BF16)
        dbm_ref[...] = _conv_bwd(d_b, cc["pre_bm"], xp_bm, w_bm, cy_bm, acc_bm, dp_bm, g, last).astype(BF16)
        dcm_ref[...] = _conv_bwd(d_c, cc["pre_cm"], xp_cm, w_cm, cy_cm, acc_cm, dp_cm, g, last).astype(BF16)
        acc_xs[g, pl.ds(SSM_CONV + 1, 1), :] += d_ng

    col = lambda v: v.reshape(H, 1)
    sds = jax.ShapeDtypeStruct
    row = lambda b, c, g: b * nc + (nc - 1 - c)
    full = lambda shape: pl.BlockSpec(shape, lambda b, c, g: (0,) * len(shape))
    return pl.pallas_call(
        body, grid=(batch, nc, G),
        in_specs=[sp["xs"], sp["bm"], sp["cm"], sp["xs_halo"], sp["bm_halo"], sp["cm_halo"], sp["grp"], sp["grp"],
                  sp["grp"], sp["dt"], sp["dtT"], sp["state"],
                  sp["w_xs"], sp["w_bm"], sp["w_cm"], sp["b_xs"], sp["b_bm"], sp["b_cm"],
                  sp["row32"], sp["col32"], sp["row32"], sp["col32"], sp["row32"], sp["vec_g"]],
        out_specs=[sp["grp"], sp["grp"],
                   pl.BlockSpec((Q, N), lambda b, c, g: (row(b, c, g), g)),
                   pl.BlockSpec((Q, N), lambda b, c, g: (row(b, c, g), g)),
                   sp["dt"], full((G, HALO, GW)), full((G, HALO, N)), full((G, HALO, N)), full((8, H))],
        out_shape=[sds((T, SSM_D_INNER), BF16), sds((T, SSM_D_INNER), BF16), sds((T, G * N), BF16),
                   sds((T, G * N), BF16), sds((T, H), F32),
                   sds((G, HALO, GW), F32), sds((G, HALO, N), F32), sds((G, HALO, N), F32), sds((8, H), F32)],
        scratch_shapes=[pltpu.VMEM((H, P, N), F32), pltpu.VMEM((H, Q), F32), pltpu.VMEM((H, Q), F32),
                        pltpu.VMEM((Q, GW), F32), pltpu.VMEM((Q, GW), F32), pltpu.VMEM((Q, GW), F32),
                        pltpu.VMEM((HALO + Q, GW), F32), pltpu.VMEM((HALO + Q, N), F32), pltpu.VMEM((HALO + Q, N), F32),
                        pltpu.VMEM((Q + HALO, GW), F32), pltpu.VMEM((Q + HALO, N), F32), pltpu.VMEM((Q + HALO, N), F32),
                        pltpu.VMEM((G, HALO, GW), F32), pltpu.VMEM((G, HALO, N), F32), pltpu.VMEM((G, HALO, N), F32)],
        compiler_params=_cparams("arbitrary", "arbitrary", "arbitrary"), name=name,
    )(xbc, xbc, xbc, xbc, xbc, xbc, z, y, dys, dt_raw, dt_rawT, states, conv_w, conv_w, conv_w, conv_b, conv_b, conv_b,
      dt_bias, col(dt_bias), a_log, col(a_log), d_skip, norm_g)


def mix_in_bwd(dh, h, g, dgates, dz, dxs, dbm, dcm, ddtT, dq, dk, dv, w_gz, w_xbc, w_dtT, w_qkv, *, name, tm=512):
    T, D = h.shape
    nt = T // tm
    GN = SSM_GROUPS * SSM_STATE

    def body(dh_ref, h_ref, g_ref, dgates_ref, dz_ref, dxs_ref, dbm_ref, dcm_ref, ddt_ref, dq_ref, dk_ref, dv_ref,
             wgz_ref, wxbc_ref, wdt_ref, wqkv_ref, dhin_ref, dg_ref):
        @pl.when(pl.program_id(0) == 0)
        def _():
            dg_ref[...] = jnp.zeros_like(dg_ref)

        du = _dot_nt(dgates_ref[...], wgz_ref[:, 0:2048])
        du = du + _dot_nt(dz_ref[...], wgz_ref[:, 2048:4096])
        du = du + _dot_nt(dxs_ref[...], wxbc_ref[:, 0:SSM_D_INNER])
        du = du + _dot_nt(dbm_ref[...], wxbc_ref[:, SSM_D_INNER:SSM_D_INNER + GN])
        du = du + _dot_nt(dcm_ref[...], wxbc_ref[:, SSM_D_INNER + GN:])
        du = du + _dot_tn(ddt_ref[...].astype(BF16), wdt_ref[...])
        du = du + _dot_nt(dq_ref[...], wqkv_ref[:, 0:ATTN_Q_DIM])
        du = du + _dot_nt(dk_ref[...], wqkv_ref[:, ATTN_Q_DIM:ATTN_Q_DIM + ATTN_KV_DIM])
        du = du + _dot_nt(dv_ref[...], wqkv_ref[:, ATTN_Q_DIM + ATTN_KV_DIM:])
        hh = h_ref[...]
        r = lax.rsqrt(jnp.mean(hh * hh, axis=-1, keepdims=True) + RMS_EPS)
        dg_ref[...] += jnp.sum(du * hh * r, axis=0, keepdims=True)
        t = du * g_ref[...]
        dhin_ref[...] = dh_ref[...] + r * t - hh * (r * r * r * jnp.mean(t * hh, axis=-1, keepdims=True))

    sds = jax.ShapeDtypeStruct
    return pl.pallas_call(
        body, grid=(nt,),
        in_specs=[_rows(tm, D), _rows(tm, D), _resident((1, D)), _rows(tm, 2048), _rows(tm, 2048), _rows(tm, SSM_D_INNER),
                  _rows(tm, GN), _rows(tm, GN), pl.BlockSpec((SSM_HEADS, tm), lambda i: (0, i)),
                  _rows(tm, ATTN_Q_DIM), _rows(tm, ATTN_KV_DIM),
                  _rows(tm, ATTN_KV_DIM), _resident(w_gz.shape), _resident(w_xbc.shape), _resident(w_dtT.shape),
                  _resident(w_qkv.shape)],
        out_specs=[_rows(tm, D), pl.BlockSpec((1, D), lambda i: (0, 0))],
        out_shape=[sds((T, D), F32), sds((1, D), F32)],
        compiler_params=_cparams("arbitrary"), name=name,
    )(dh, h, g, dgates, dz, dxs, dbm, dcm, ddtT, dq, dk, dv, w_gz, w_xbc, w_dtT, w_qkv)


PAIRS = SSM_HPG // 2
PW = 2 * SSM_HEAD_DIM


def _ssd_prologue(first, xs_ref, bm_ref, cm_ref, xs_halo, bm_halo, cm_halo, w_xs, w_bm, w_cm, b_xs, b_bm, b_cm,
                  dtT_ref, dtb_ref, alog_ref, xp_xs, xp_bm, xp_cm):
    Q = SSM_CHUNK
    pre_xs = _conv_pre(xs_ref, xs_halo, w_xs, b_xs, xp_xs, first)
    pre_bm = _conv_pre(bm_ref, bm_halo, w_bm, b_bm, xp_bm, first)
    pre_cm = _conv_pre(cm_ref, cm_halo, w_cm, b_cm, xp_cm, first)
    dtrT = dtT_ref[...] + dtb_ref[...]
    dtT = _softplus(dtrT)
    aT = -jnp.exp(alog_ref[...])
    triT = (_iota((Q, Q), 0) <= _iota((Q, Q), 1)).astype(F32)
    acsT = _dot_hi(dtT * aT, triT)
    lastT = acsT[:, Q - 1:Q]
    wT = jnp.exp(lastT - acsT)
    eT = jnp.exp(acsT)
    cols = jnp.concatenate([dtT, acsT, wT, eT], axis=0).T
    return dict(pre_xs=pre_xs, pre_bm=pre_bm, pre_cm=pre_cm, xs=pre_xs * _sigmoid(pre_xs), bm=pre_bm * _sigmoid(pre_bm),
                cm=pre_cm * _sigmoid(pre_cm), dtrT=dtrT, dtT=dtT, aT=aT, acsT=acsT, decayT=jnp.exp(lastT), cols=cols)


def _pair_cols(cols, base, p, lo):
    k = base + 2 * p
    return jnp.where(lo, cols[:, k:k + 1], cols[:, k + 1:k + 2])


def _pair_row(colT, p, lo_row):
    return jnp.where(lo_row, colT[2 * p:2 * p + 1, :], colT[2 * p + 1:2 * p + 2, :])


def _pair_operands(pp, p, s, causal, lo, xb):
    zero = jnp.zeros_like(xb)
    rhs = jnp.concatenate([jnp.where(lo, xb, zero), jnp.where(lo, zero, xb)], axis=0)
    ls, ms = [], []
    for k in (2 * p, 2 * p + 1):
        seg = pp["cols"][:, 8 + k:9 + k] - pp["acsT"][k:k + 1, :]
        l = jnp.exp(jnp.where(causal, seg, -1e30))
        ls.append(l)
        ms.append(s * l)
    lhs = jnp.concatenate([m.astype(BF16) for m in ms], axis=1)
    return lhs, rhs, ls


def ssd_fwd(xbc, z, dt_rawT, conv_w, conv_b, dt_bias, a_log, d_skip_x, norm_g, *, batch, name):
    T = xbc.shape[0]
    Q, GW, N = SSM_CHUNK, SSM_GW, SSM_STATE
    nc = T // batch // Q
    sp = _ssd_specs(nc)(lambda c: c)

    def body(xs_ref, bm_ref, cm_ref, xs_halo, bm_halo, cm_halo, z_ref, dtT_ref,
             w_xs, w_bm, w_cm, b_xs, b_bm, b_cm, dtb_ref, alog_ref, dsk_ref, ng_ref,
             y_ref, ys_ref, st_ref, state, xp_xs, xp_bm, xp_cm):
        c = pl.program_id(1)
        g = pl.program_id(2)
        first = c == 0
        pp = _ssd_prologue(first, xs_ref, bm_ref, cm_ref, xs_halo, bm_halo, cm_halo, w_xs, w_bm, w_cm,
                           b_xs, b_bm, b_cm, dtT_ref, dtb_ref, alog_ref, xp_xs, xp_bm, xp_cm)
        xs = pp["xs"]
        bb = pp["bm"].astype(BF16)
        cb = pp["cm"].astype(BF16)
        s = _dot_nt(cb, bb)
        causal = _iota((Q, Q), 0) >= _iota((Q, Q), 1)
        lo = _iota((Q, PW), 1) < SSM_HEAD_DIM
        lo_row = _iota((1, PW), 1) < SSM_HEAD_DIM
        ys = []

        @pl.when(first)
        def _():
            state[g] = jnp.zeros((PAIRS, N, PW), F32)

        for p in range(PAIRS):
            tile = slice(PW * p, PW * (p + 1))
            xs_p = xs[:, tile]
            x_p = xs_p * _pair_cols(pp["cols"], 0, p, lo)
            lhs, rhs, _ = _pair_operands(pp, p, s, causal, lo, x_p.astype(BF16))
            hp = state[g, p]
            st_ref[0, p] = hp
            ys.append(_dot(lhs, rhs) + _dot(cb, hp.astype(BF16)) * _pair_cols(pp["cols"], 24, p, lo)
                      + dsk_ref[:, tile] * xs_p)
            xw = (x_p * _pair_cols(pp["cols"], 16, p, lo)).astype(BF16)
            state[g, p] = hp * _pair_row(pp["decayT"], p, lo_row) + _dot_tn(bb, xw)
        y = jnp.concatenate(ys, axis=1)
        y_ref[...] = y
        zz = z_ref[...]
        yg = y * (zz * _sigmoid(zz))
        rr = lax.rsqrt(jnp.mean(yg * yg, axis=-1, keepdims=True) + RMS_EPS)
        ys_ref[...] = (yg * rr * ng_ref[...]).astype(BF16)

    sds = jax.ShapeDtypeStruct
    return pl.pallas_call(
        body, grid=(batch, nc, SSM_GROUPS),
        in_specs=[sp["xs"], sp["bm"], sp["cm"], sp["xs_halo"], sp["bm_halo"], sp["cm_halo"], sp["grp"], sp["dtT_g"],
                  sp["w_xs"], sp["w_bm"], sp["w_cm"], sp["b_xs"], sp["b_bm"], sp["b_cm"],
                  sp["col_g"], sp["col_g"], sp["vec_g"], sp["vec_g"]],
        out_specs=[sp["grp"], sp["grp"], sp["pairs"]],
        out_shape=[sds((T, SSM_D_INNER), F32), sds((T, SSM_D_INNER), BF16),
                   sds((T // Q, SSM_GROUPS * PAIRS, N, PW), F32)],
        scratch_shapes=[pltpu.VMEM((SSM_GROUPS, PAIRS, N, PW), F32),
                        pltpu.VMEM((HALO + Q, GW), F32), pltpu.VMEM((HALO + Q, N), F32), pltpu.VMEM((HALO + Q, N), F32)],
        compiler_params=_cparams("arbitrary", "arbitrary", "arbitrary"), name=name,
    )(xbc, xbc, xbc, xbc, xbc, xbc, z, dt_rawT, conv_w, conv_w, conv_w, conv_b, conv_b, conv_b,
      dt_bias, a_log, d_skip_x, norm_g)


def ssd_bwd(dys, y, xbc, z, dt_rawT, states, conv_w, conv_b, dt_bias, a_log, d_skip_x, norm_g, *, batch, name,
            ride=None):
    T = xbc.shape[0]
    Q, GW, N, G = SSM_CHUNK, SSM_GW, SSM_STATE, SSM_GROUPS
    nc = T // batch // Q
    sp = _ssd_specs(nc)(lambda c: nc - 1 - c)

    def body(xs_ref, bm_ref, cm_ref, xs_halo, bm_halo, cm_halo, z_ref, y_ref, dys_ref, dtT_ref, st_ref,
             w_xs, w_bm, w_cm, b_xs, b_bm, b_cm, dtb_ref, alog_ref, dsk_ref, ng_ref,
             dz_ref, dxs_ref, dbm_ref, dcm_ref, ddtT_ref, acc_xs, acc_bm, acc_cm, acc_head,
             dstate, xp_xs, xp_bm, xp_cm, dp_xs, dp_bm, dp_cm, cy_xs, cy_bm, cy_cm):
        b = pl.program_id(0)
        cr = pl.program_id(1)
        g = pl.program_id(2)
        first = cr == nc - 1
        last = cr == 0

        @pl.when(jnp.logical_and(jnp.logical_and(b == 0, cr == 0), g == 0))
        def _():
            acc_xs[...] = jnp.zeros_like(acc_xs)
            acc_bm[...] = jnp.zeros_like(acc_bm)
            acc_cm[...] = jnp.zeros_like(acc_cm)
            acc_head[...] = jnp.zeros_like(acc_head)

        @pl.when(last)
        def _():
            dstate[g] = jnp.zeros((PAIRS, N, PW), F32)
            cy_xs[g] = jnp.zeros((HALO, GW), F32)
            cy_bm[g] = jnp.zeros((HALO, N), F32)
            cy_cm[g] = jnp.zeros((HALO, N), F32)

        pp = _ssd_prologue(first, xs_ref, bm_ref, cm_ref, xs_halo, bm_halo, cm_halo, w_xs, w_bm, w_cm,
                           b_xs, b_bm, b_cm, dtT_ref, dtb_ref, alog_ref, xp_xs, xp_bm, xp_cm)
        xs, dtT, aT, decayT = pp["xs"], pp["dtT"], pp["aT"], pp["decayT"]

        yv = y_ref[...]
        zz = z_ref[...]
        sz = _sigmoid(zz)
        silu_z = zz * sz
        yg = yv * silu_z
        rr = lax.rsqrt(jnp.mean(yg * yg, axis=-1, keepdims=True) + RMS_EPS)
        dys_v = dys_ref[...]
        d_ng = jnp.sum(dys_v * yg * rr, axis=0, keepdims=True)
        t = dys_v * ng_ref[...]
        dyg = rr * t - yg * (rr * rr * rr * jnp.mean(t * yg, axis=-1, keepdims=True))
        dy = dyg * silu_z
        dz_ref[...] = (dyg * yv * (sz * (1.0 + zz * (1.0 - sz)))).astype(BF16)
        dsk = dsk_ref[...]
        d_dsk = jnp.sum(dy * xs, axis=0, keepdims=True)

        bb = pp["bm"].astype(BF16)
        cb = pp["cm"].astype(BF16)
        s = _dot_nt(cb, bb)
        causal = _iota((Q, Q), 0) >= _iota((Q, Q), 1)
        lo = _iota((Q, PW), 1) < SSM_HEAD_DIM
        lo_row = _iota((1, PW), 1) < SSM_HEAD_DIM
        sub8 = _iota((SSM_HPG, 1), 0)
        ds_acc = jnp.zeros((Q, Q), F32)
        d_c = jnp.zeros((Q, N), F32)
        d_b = jnp.zeros((Q, N), F32)
        last_terms = jnp.zeros((SSM_HPG, 1), F32)
        q1, q2, dxs = [], [], []
        for p in range(PAIRS):
            tile = slice(PW * p, PW * (p + 1))
            dt_p = _pair_cols(pp["cols"], 0, p, lo)
            w_p = _pair_cols(pp["cols"], 16, p, lo)
            e_p = _pair_cols(pp["cols"], 24, p, lo)
            xs_p = xs[:, tile]
            x_p = xs_p * dt_p
            xw_p = x_p * w_p
            lhs, rhs, ls = _pair_operands(pp, p, s, causal, lo, x_p.astype(BF16))
            dy_p = dy[:, tile]
            dyb = dy_p.astype(BF16)
            hp = st_ref[0, p]
            hpb = hp.astype(BF16)
            dh = dstate[g, p]
            dhb = dh.astype(BF16)
            dye = (dy_p * e_p).astype(BF16)
            d_c = d_c + _dot_nt(dye, hpb)
            dm = _dot_nt(dyb, rhs)
            dxd2 = _dot_tn(lhs, dyb)
            dxd = jnp.where(lo, dxd2[0:Q], dxd2[Q:2 * Q])
            ds_acc = ds_acc + dm[:, 0:Q] * ls[0] + dm[:, Q:2 * Q] * ls[1]
            dxw = _dot(bb, dhb)
            d_b = d_b + _dot_nt(xw_p.astype(BF16), dhb)
            dx_full = dxd + dxw * w_p
            tw = dxw * xw_p
            yd = _dot(lhs, rhs)
            yoff = _dot(cb, hpb) * e_p
            q1.append(dyb.astype(F32) * yd + dy_p * yoff - tw - x_p.astype(BF16).astype(F32) * dxd)
            q2.append(dx_full * xs_p)
            dxs.append(dsk[:, tile] * dy_p + dx_full * dt_p)
            row = jnp.sum(dh * hp, axis=0, keepdims=True) * _pair_row(decayT, p, lo_row) + jnp.sum(tw, axis=0, keepdims=True)
            t_lo = jnp.sum(jnp.where(lo_row, row, 0.0), axis=1, keepdims=True)
            t_hi = jnp.sum(jnp.where(lo_row, 0.0, row), axis=1, keepdims=True)
            last_terms = last_terms + jnp.where(sub8 == 2 * p, t_lo, 0.0) + jnp.where(sub8 == 2 * p + 1, t_hi, 0.0)
            dstate[g, p] = dh * _pair_row(decayT, p, lo_row) + _dot_tn(cb, dye)
        dsb = ds_acc.astype(BF16)
        d_c = d_c + _dot(dsb, bb)
        d_b = d_b + _dot_tn(dsb, cb)
        e8 = (_iota((SSM_HPG, GW), 0) == lax.shift_right_logical(_iota((SSM_HPG, GW), 1), 6)).astype(F32)
        seg_sum = lambda tiles: lax.dot_general(e8, jnp.concatenate(tiles, axis=1), (((1,), (1,)), ((), ())),
                                                preferred_element_type=F32, precision=lax.Precision.HIGHEST)
        dacsT = seg_sum(q1) + jnp.where(_iota((1, Q), 1) == Q - 1, 1.0, 0.0) * last_terms
        tri = (_iota((Q, Q), 0) >= _iota((Q, Q), 1)).astype(F32)
        d_dtaT = _dot_hi(dacsT, tri)
        ddtT = d_dtaT * aT + seg_sum(q2)
        d_alog = jnp.sum(d_dtaT * dtT, axis=1, keepdims=True) * aT
        ddt_rawT = ddtT * _sigmoid(pp["dtrT"])
        ddtT_ref[...] = ddt_rawT
        d_dtb = jnp.sum(ddt_rawT, axis=1, keepdims=True)
        lane = _iota((SSM_HPG, N), 1)
        acc_head[g] += jnp.where(lane == 0, d_dtb, 0.0) + jnp.where(lane == 1, d_alog, 0.0)
        dxs_v = jnp.concatenate(dxs, axis=1)
        dxs_ref[...] = _conv_bwd(dxs_v, pp["pre_xs"], xp_xs, w_xs, cy_xs, acc_xs, dp_xs, g, last).astype(BF16)
        dbm_ref[...] = _conv_bwd(d_b, pp["pre_bm"], xp_bm, w_bm, cy_bm, acc_bm, dp_bm, g, last).astype(BF16)
        dcm_ref[...] = _conv_bwd(d_c, pp["pre_cm"], xp_cm, w_cm, cy_cm, acc_cm, dp_cm, g, last).astype(BF16)
        acc_xs[g, pl.ds(SSM_CONV + 1, 2), :] += jnp.concatenate([d_ng, d_dsk], axis=0)

    sds = jax.ShapeDtypeStruct
    row = lambda b, c, g: b * nc + (nc - 1 - c)
    full = lambda shape: pl.BlockSpec(shape, lambda b, c, g: (0,) * len(shape))
    return _call(
        body, grid=(batch, nc, G),
        in_specs=[sp["xs"], sp["bm"], sp["cm"], sp["xs_halo"], sp["bm_halo"], sp["cm_halo"], sp["grp"], sp["grp"],
                  sp["grp"], sp["dtT_g"], sp["pairs"],
                  sp["w_xs"], sp["w_bm"], sp["w_cm"], sp["b_xs"], sp["b_bm"], sp["b_cm"],
                  sp["col_g"], sp["col_g"], sp["vec_g"], sp["vec_g"]],
        args=[xbc, xbc, xbc, xbc, xbc, xbc, z, y, dys, dt_rawT, states, conv_w, conv_w, conv_w, conv_b, conv_b, conv_b,
              dt_bias, a_log, d_skip_x, norm_g],
        out_specs=[sp["grp"], sp["grp"],
                   pl.BlockSpec((Q, N), lambda b, c, g: (row(b, c, g), g)),
                   pl.BlockSpec((Q, N), lambda b, c, g: (row(b, c, g), g)),
                   sp["dtT_g"], full((G, HALO, GW)), full((G, HALO, N)), full((G, HALO, N)), full((G, SSM_HPG, N))],
        out_shape=[sds((T, SSM_D_INNER), BF16), sds((T, SSM_D_INNER), BF16), sds((T, G * N), BF16),
                   sds((T, G * N), BF16), sds((SSM_HEADS, T), F32),
                   sds((G, HALO, GW), F32), sds((G, HALO, N), F32), sds((G, HALO, N), F32), sds((G, SSM_HPG, N), F32)],
        scratch=[pltpu.VMEM((G, PAIRS, N, PW), F32),
                 pltpu.VMEM((HALO + Q, GW), F32), pltpu.VMEM((HALO + Q, N), F32), pltpu.VMEM((HALO + Q, N), F32),
                 pltpu.VMEM((Q + HALO, GW), F32), pltpu.VMEM((Q + HALO, N), F32), pltpu.VMEM((Q + HALO, N), F32),
                 pltpu.VMEM((G, HALO, GW), F32), pltpu.VMEM((G, HALO, N), F32), pltpu.VMEM((G, HALO, N), F32)],
        sem=("arbitrary", "arbitrary", "arbitrary"), name=name, ride=ride)


def mm_rows(a, b, *, name, tt=2048):
    M, T = a.shape
    N = b.shape[1]
    tt = min(tt, T)

    def body(a_ref, b_ref, o_ref):
        @pl.when(pl.program_id(0) == 0)
        def _():
            o_ref[...] = jnp.zeros_like(o_ref)

        o_ref[...] += _dot(a_ref[...].astype(BF16), b_ref[...])

    return pl.pallas_call(
        body, grid=(T // tt,),
        in_specs=[pl.BlockSpec((M, tt), lambda t: (0, t)), pl.BlockSpec((tt, N), lambda t: (t, 0))],
        out_specs=pl.BlockSpec((M, N), lambda t: (0, 0)), out_shape=jax.ShapeDtypeStruct((M, N), F32),
        compiler_params=_cparams("arbitrary"), name=name)(a, b)


MESH = pl.DeviceIdType.MESH
ANY = pl.BlockSpec(memory_space=pl.ANY)
ROW_ALIGN = 16


def _me():
    return lax.axis_index("x"), lax.axis_index("y"), lax.axis_index("c")


def _other_chips(x, y):
    return [(1 - x, y), (x, 1 - y), (1 - x, 1 - y)]


def _remote(src, dst, send_sem, recv_sem, to):
    return pltpu.make_async_remote_copy(src_ref=src, dst_ref=dst, send_sem=send_sem, recv_sem=recv_sem,
                                        device_id=to, device_id_type=MESH)


def _half(c, rows):
    return pl.ds(pl.multiple_of(c * (rows // 2), ROW_ALIGN), rows // 2)


def ag_ride(bufs):
    n = len(bufs)

    def copies(outs, sems):
        ici_send, ici_recv, d2d_send, d2d_recv = sems
        x, y, c = _me()
        sib = (x, y, 1 - c)
        ici, d2d, d2d_in = [], [], []
        for i in range(n):
            rows = outs[i].shape[1]
            mine = outs[i].at[2 * x + y, _half(c, rows)]
            for j, chip in enumerate(_other_chips(x, y)):
                ici.append(_remote(mine, mine, ici_send.at[i, j], ici_recv.at[i, j], (*chip, c)))
                landed = outs[i].at[2 * chip[0] + chip[1], _half(c, rows)]
                d2d.append((_remote(landed, landed, ici_send.at[i, j], ici_recv.at[i, j], (*chip, c)),
                            _remote(landed, landed, d2d_send.at[i, j], d2d_recv.at[i, j], sib)))
                lands = outs[i].at[2 * chip[0] + chip[1], _half(1 - c, rows)]
                d2d_in.append(_remote(lands, lands, d2d_send.at[i, j], d2d_recv.at[i, j], sib))
        return ici, d2d, d2d_in

    def start(ins, outs, sems):
        for cp in copies(outs, sems)[0]:
            cp.start()

    def finish(ins, outs, sems):
        ici, d2d, d2d_in = copies(outs, sems)
        for arrived, forward in d2d:
            arrived.wait_recv()
            forward.start()
        for cp in d2d_in:
            cp.wait_recv()
        for cp in ici + [forward for _, forward in d2d]:
            cp.wait_send()

    return Ride(bufs, [jax.ShapeDtypeStruct(b.shape, b.dtype) for b in bufs], [(i, i) for i in range(n)],
                [pltpu.SemaphoreType.DMA((n, 3))] * 4, start, finish)


def rs_pair(grads, *, name):
    n = len(grads)

    def body(*refs):
        ins, outs = refs[:n], refs[n:2 * n]
        send, recv = refs[2 * n:]
        x, y, c = _me()
        sib = (x, y, 1 - c)
        sent = []
        for i in range(n):
            rows = ins[i].shape[1]
            sent.append(_remote(ins[i].at[:, _half(1 - c, rows), :], outs[i], send.at[i], recv.at[i], sib))
            sent[-1].start()
        for cp in sent:
            cp.wait()

    return pl.pallas_call(
        body, in_specs=[ANY] * n, out_specs=[ANY] * n,
        out_shape=[jax.ShapeDtypeStruct((N_SHARD, g.shape[1] // 2, g.shape[2]), g.dtype) for g in grads],
        scratch_shapes=[pltpu.SemaphoreType.DMA((n,)), pltpu.SemaphoreType.DMA((n,))], name=name)(*grads)


def rs_add(grad, part, c, *, rt, name):
    _, rows, cols = grad.shape
    r2 = rows // 2
    nrb = r2 // rt

    def body(c_ref, g_ref, p_ref, o_ref):
        o_ref[...] = (g_ref[...] + p_ref[...]).astype(BF16)

    return pl.pallas_call(
        body,
        grid_spec=pltpu.PrefetchScalarGridSpec(
            num_scalar_prefetch=1, grid=(N_SHARD, nrb),
            in_specs=[pl.BlockSpec((1, rt, cols), lambda k, i, c_ref: (k, c_ref[1] * nrb + i, 0)),
                      pl.BlockSpec((1, rt, cols), lambda k, i, c_ref: (k, i, 0))],
            out_specs=pl.BlockSpec((1, rt, cols), lambda k, i, c_ref: (k, i, 0))),
        out_shape=jax.ShapeDtypeStruct((N_SHARD, r2, cols), BF16),
        compiler_params=_cparams("parallel", "parallel"), name=name)(c, grad, part)


def chips_ride(sums):
    n = len(sums)

    def copies(ins, outs, sems):
        send, recv = sems
        x, y, c = _me()
        return [_remote(ins[i].at[2 * chip[0] + chip[1]], outs[i].at[2 * x + y], send.at[i, j], recv.at[i, j], (*chip, c))
                for i in range(n) for j, chip in enumerate(_other_chips(x, y))]

    def start(ins, outs, sems):
        for cp in copies(ins, outs, sems):
            cp.start()

    def finish(ins, outs, sems):
        for cp in copies(ins, outs, sems):
            cp.wait()

    return Ride(sums, [jax.ShapeDtypeStruct(s.shape, s.dtype) for s in sums], [],
                [pltpu.SemaphoreType.DMA((n, 3))] * 2, start, finish)


def rs_total(parts, own, where, *, rt, name):
    _, r2, cols = parts.shape
    nrb = r2 // rt

    def body(w_ref, p0, p1, p2, p3, own_ref, o_ref):
        s_me = w_ref[0]
        acc = None
        for k, p in enumerate((p0, p1, p2, p3)):
            term = jnp.where(s_me == k, own_ref[0], p[0]).astype(F32)
            acc = term if acc is None else acc + term
        o_ref[...] = acc

    def slot(k):
        return pl.BlockSpec((1, rt, cols), lambda i, w: (jnp.where(w[0] == k, (k + 1) % N_SHARD, k), i, 0))

    return pl.pallas_call(
        body,
        grid_spec=pltpu.PrefetchScalarGridSpec(
            num_scalar_prefetch=1, grid=(nrb,),
            in_specs=[slot(0), slot(1), slot(2), slot(3), pl.BlockSpec((1, rt, cols), lambda i, w: (w[0], i, 0))],
            out_specs=pl.BlockSpec((rt, cols), lambda i, w: (w[1] * nrb + i, 0))),
        out_shape=jax.ShapeDtypeStruct((2 * r2, cols), F32),
        compiler_params=_cparams("parallel"), name=name)(where, parts, parts, parts, parts, own)


def rs_share(totals, *, name):
    n = len(totals)

    def body(*refs):
        outs = refs[n:2 * n]
        send, recv = refs[2 * n:]
        x, y, c = _me()
        sib = (x, y, 1 - c)
        sent = []
        for i in range(n):
            mine = outs[i].at[_half(c, outs[i].shape[0])]
            sent.append(_remote(mine, mine, send.at[i], recv.at[i], sib))
            sent[-1].start()
        for i in range(n):
            other = outs[i].at[_half(1 - c, outs[i].shape[0])]
            _remote(other, other, send.at[i], recv.at[i], sib).wait_recv()
        for cp in sent:
            cp.wait_send()

    return pl.pallas_call(
        body, in_specs=[ANY] * n, out_specs=[ANY] * n,
        out_shape=[jax.ShapeDtypeStruct(t.shape, t.dtype) for t in totals],
        input_output_aliases={i: i for i in range(n)},
        scratch_shapes=[pltpu.SemaphoreType.DMA((n,)), pltpu.SemaphoreType.DMA((n,))],
        name=name)(*totals)


def small_allreduce(buf, *, name):
    rows = buf.shape[0]

    def body(x_ref, o_ref, slots, send, recv):
        x, y, c = _me()
        me = 4 * x + 2 * y + c
        slots[me] = x_ref[...]
        sent = []
        for d in range(1, 8):
            peer = (1 - x if d & 4 else x, 1 - y if d & 2 else y, 1 - c if d & 1 else c)
            sent.append(_remote(x_ref, slots.at[me], send.at[d - 1], recv.at[d - 1], peer))
            sent[-1].start()
        for cp in sent:
            cp.wait()
        acc = slots[0]
        for k in range(1, 8):
            acc = acc + slots[k]
        o_ref[...] = acc

    return pl.pallas_call(
        body, out_shape=jax.ShapeDtypeStruct(buf.shape, F32),
        in_specs=[pl.BlockSpec(memory_space=pltpu.VMEM)], out_specs=pl.BlockSpec(memory_space=pltpu.VMEM),
        scratch_shapes=[pltpu.VMEM((8, rows, 128), F32), pltpu.SemaphoreType.DMA((7,)), pltpu.SemaphoreType.DMA((7,))],
        name=name)(buf)


def adamw(w, g, m, v, *, name, rt=None):
    rows, cols = w.shape
    rt = rows if rt is None else rt
    c1 = 1.0 - ADAM_B1 ** ADAM_STEP
    c2 = 1.0 - ADAM_B2 ** ADAM_STEP

    def body(w_ref, g_ref, m_ref, v_ref, d_ref, nm_ref, nv_ref):
        gg = g_ref[...]
        nm = ADAM_B1 * m_ref[...] + (1.0 - ADAM_B1) * gg
        nv = ADAM_B2 * v_ref[...] + (1.0 - ADAM_B2) * (gg * gg)
        nm_ref[...] = nm
        nv_ref[...] = nv
        d_ref[...] = -ADAM_LR * ((nm / c1) / (jnp.sqrt(nv / c2) + ADAM_EPS) + ADAM_WD * w_ref[...])

    spec = pl.BlockSpec((rt, cols), lambda i: (i, 0))
    return pl.pallas_call(
        body, grid=(rows // rt,), in_specs=[spec] * 4, out_specs=[spec] * 3,
        out_shape=[jax.ShapeDtypeStruct((rows, cols), F32)] * 3,
        compiler_params=_cparams("parallel"), name=name)(w, g, m, v)


WEIGHTS = ['ffn1_pre_g', 'ffn1_w_gate', 'ffn1_w_up', 'ffn1_w_down', 'ffn1_post_g', 'mix_pre_g', 'w_in', 'conv_w',
           'conv_b', 'dt_bias', 'a_log', 'd_skip', 'ssm_norm_g', 'w_ssm_proj', 'attn_sinks', 'rel_bias_table',
           'w_attn_proj', 'w_out', 'mix_post_g', 'ffn2_pre_g', 'ffn2_w_gate', 'ffn2_w_up', 'ffn2_w_down', 'ffn2_post_g']
BIG = ['ffn1_w_gate', 'ffn1_w_up', 'ffn1_w_down', 'w_in', 'w_ssm_proj', 'w_attn_proj', 'w_out',
       'ffn2_w_gate', 'ffn2_w_up', 'ffn2_w_down']
SMALL = [w for w in WEIGHTS if w not in BIG]


def _bucket_onehot():
    blk = ATTN_BLOCK
    dist = np.maximum(np.arange(blk)[:, None] + blk - np.arange(2 * blk)[None, :], 0)
    max_exact = REL_BUCKETS // 2
    d = np.maximum(dist, 1).astype(np.float32)
    large = max_exact + (np.log(d / np.float32(max_exact)) / np.float32(math.log(REL_MAX_DISTANCE / max_exact))
                         * np.float32(REL_BUCKETS - max_exact)).astype(np.int32)
    bucket = np.where(dist < max_exact, dist, np.minimum(large, REL_BUCKETS - 1)).reshape(-1)
    return jnp.asarray((bucket[None, :] == np.arange(REL_BUCKETS)[:, None]).astype(np.float32))


def _pack_rows(parts, mult=8):
    flat = jnp.concatenate([p.reshape(-1).astype(F32) for p in parts])
    rows = -(-flat.shape[0] // (128 * mult)) * mult
    return jnp.pad(flat, (0, rows * 128 - flat.shape[0])).reshape(rows, 128)


def _unpack_rows(buf, shapes):
    flat = buf.reshape(-1)
    out, at = [], 0
    for shp in shapes:
        size = int(np.prod(shp))
        out.append(flat[at:at + size].reshape(shp))
        at += size
    return out


def kernel(x, ffn1_pre_g, ffn1_w_gate, ffn1_w_up, ffn1_w_down, ffn1_post_g, mix_pre_g, w_in, conv_w, conv_b, dt_bias, a_log, d_skip, ssm_norm_g, w_ssm_proj, attn_sinks, rel_bias_table, w_attn_proj, w_out, mix_post_g, ffn2_pre_g, ffn2_w_gate, ffn2_w_up, ffn2_w_down, ffn2_post_g, loss_target, m_ffn1_pre_g, m_ffn1_w_gate, m_ffn1_w_up, m_ffn1_w_down, m_ffn1_post_g, m_mix_pre_g, m_w_in, m_conv_w, m_conv_b, m_dt_bias, m_a_log, m_d_skip, m_ssm_norm_g, m_w_ssm_proj, m_attn_sinks, m_rel_bias_table, m_w_attn_proj, m_w_out, m_mix_post_g, m_ffn2_pre_g, m_ffn2_w_gate, m_ffn2_w_up, m_ffn2_w_down, m_ffn2_post_g, v_ffn1_pre_g, v_ffn1_w_gate, v_ffn1_w_up, v_ffn1_w_down, v_ffn1_post_g, v_mix_pre_g, v_w_in, v_conv_w, v_conv_b, v_dt_bias, v_a_log, v_d_skip, v_ssm_norm_g, v_w_ssm_proj, v_attn_sinks, v_rel_bias_table, v_w_attn_proj, v_w_out, v_mix_post_g, v_ffn2_pre_g, v_ffn2_w_gate, v_ffn2_w_up, v_ffn2_w_down, v_ffn2_post_g):
    args = locals()
    w = {n: args[n] for n in WEIGHTS}
    m = {n: args["m_" + n] for n in WEIGHTS}
    v = {n: args["v_" + n] for n in WEIGHTS}
    batch, seq, D = x.shape
    T = batch * seq
    xi, yi, ci = _me()
    s_me = 2 * xi + yi
    x2 = x.reshape(T, D)
    tgt = loss_target.reshape(T, D)

    def own_slot(parts):
        p = jnp.concatenate([t[0] for t in parts], axis=0).astype(BF16)
        return lax.dynamic_update_slice(lax.empty((N_SHARD,) + p.shape, BF16), p[None], (s_me, 0, 0))

    tr = lambda a: jnp.swapaxes(a, -1, -2)
    (wffn1,) = run_ride(ag_ride([own_slot([tr(ffn1_w_gate), tr(ffn1_w_up), ffn1_w_down])]), name="ag_ffn1")
    col = lambda v: v.reshape(SSM_HEADS, 1)
    d_skip_x = jnp.repeat(d_skip, SSM_HEAD_DIM, axis=1)
    cw_slot = lax.dynamic_update_slice(jnp.zeros((SSM_CONV, SSM_CONV_DIM), F32),
                                       conv_w[0] * (ci == 0).astype(F32), (0, s_me * (SSM_CONV_DIM // N_SHARD)))
    conv_w_full = small_allreduce(cw_slot.reshape(-1, 128), name="ag_conv_w").reshape(SSM_CONV, SSM_CONV_DIM)

    (h1, n1, gate1, up1, f1), (gin, gmix) = ffn_fwd(
        x2, ffn1_pre_g, wffn1, ffn1_post_g, name="ffn1_fwd",
        ride=ag_ride([own_slot([w_in]), own_slot([w_ssm_proj, w_attn_proj, w_out])]))
    w_in_full = gin.transpose(1, 0, 2).reshape(D, IN_COLS)
    w_gz = w_in_full[:, 0:4096]
    w_xbc = w_in_full[:, 4096:4096 + SSM_CONV_DIM]
    w_dtT = w_in_full[:, 7168:7200].T
    w_qkv = w_in_full[:, 7200:]
    (u, gates, z, xbc, dt_rawT, q, k, vv), (wffn2,) = mix_in_fwd(
        h1, mix_pre_g, w_gz, w_xbc, w_dtT, w_qkv, name="mix_in_fwd",
        ride=ag_ride([own_slot([tr(ffn2_w_gate), tr(ffn2_w_up), ffn2_w_down])]))
    y, ys, states = ssd_fwd(xbc, z, dt_rawT, conv_w_full, conv_b, col(dt_bias), col(a_log), d_skip_x, ssm_norm_g,
                            batch=batch, name="ssd_fwd")
    onehot = _bucket_onehot()
    bias = attn_bias(rel_bias_table.T, onehot, name="attn_bias").reshape(ATTN_Q_HEADS, ATTN_BLOCK, 2 * ATTN_BLOCK)
    o, lse = attn_fwd(q, k, vv, bias, attn_sinks, batch=batch, name="attn_fwd")
    h2, y_ssm, y_attn, mix, merged = mix_out_fwd(ys, o, gates, h1, gmix, mix_post_g, name="mix_out_fwd")
    h3, n3, gate2, up2, f2, dy, loss_parts = ffn_fwd(h2, ffn2_pre_g, wffn2, ffn2_post_g, tgt, name="ffn2_fwd")

    where = jnp.stack([s_me, ci]).astype(jnp.int32)

    def rs_front(grads, tiles, tag):
        pair = rs_pair(grads, name="rs_pair_" + tag)
        return [rs_add(g, p, where, rt=rt, name=f"rs_add_{tag}{i}") for i, (g, p, rt) in enumerate(zip(grads, pair, tiles))]

    def rs_back(parts, sums, tiles, tag):
        totals = [rs_total(p, s, where, rt=rt, name=f"rs_total_{tag}{i}")
                  for i, (p, s, rt) in enumerate(zip(parts, sums, tiles))]
        return rs_share(totals, name="rs_share_" + tag)

    def ffn_grads(n, dgate, dup, a, df, tag):
        d = mm_tn(dgate, n[None], into=(lax.empty(wffn1.shape, F32), 0), name="dw_gate" + tag)
        d = mm_tn(dup, n[None], into=(d, 1), name="dw_up" + tag)
        return [mm_tn(a, df[None], into=(d, 2), name="dw_down" + tag)]

    ffn_tiles, mix_tiles = [352], [256, 256]
    dh2, df2, a2, dgate2, dup2, dg_ffn2_pre, dg_ffn2_post = ffn_bwd(dy, h2, f2, gate2, up2, ffn2_pre_g, ffn2_post_g,
                                                                    wffn2, name="ffn2_bwd")
    sums_f2 = rs_front(ffn_grads(n3, dgate2, dup2, a2, df2, "2"), ffn_tiles, "f2")
    dmix, dyssm, dyattn, dgates, dys, do, dg_mix_post = mix_out_bwd(dh2, mix, y_ssm, y_attn, gates, gmix, mix_post_g,
                                                                    name="mix_out_bwd")
    dq, dk, dv, dbias, dsinks = attn_bwd(q, k, vv, o, do, lse, bias, attn_sinks, batch=batch, name="attn_bwd")
    dtable = attn_bias_bwd(dbias.reshape(ATTN_Q_HEADS, -1), onehot, name="attn_bias_bwd").T
    (dz, dxs, dbm, dcm, ddtT, acc_xs, acc_bm, acc_cm, acc_head), parts_f2 = ssd_bwd(
        dys, y, xbc, z, dt_rawT, states, conv_w_full, conv_b, col(dt_bias), col(a_log), d_skip_x, ssm_norm_g,
        batch=batch, name="ssd_bwd", ride=chips_ride(sums_f2))
    (rffn2,) = rs_back(parts_f2, sums_f2, ffn_tiles, "f2")
    dh1, dg_mix_pre = mix_in_bwd(dh2, h1, mix_pre_g, dgates, dz, dxs, dbm, dcm, ddtT, dq, dk, dv, w_gz, w_xbc, w_dtT,
                                 w_qkv, name="mix_in_bwd")
    dmx = mm_tn(ys[None], dyssm[None], a_cols=(N_SHARD, 512), into=(lax.empty(gmix.shape, F32), 0), name="dw_ssm")
    dmx = mm_tn(o[None], dyattn[None], a_cols=(N_SHARD, 256), into=(dmx, 2), name="dw_attn")
    dmx = mm_tn(merged[None], dmix[None], a_cols=(N_SHARD, 256), into=(dmx, 3), name="dw_out")
    ub = u[None]
    din = jnp.concatenate([
        mm_tn(ub, dgates[None], name="dw_in_gates", tn=1024)[0], mm_tn(ub, dz[None], name="dw_in_z", tn=1024)[0],
        mm_tn(ub, dxs[None], name="dw_in_xs", tn=1024)[0], mm_tn(ub, dbm[None], name="dw_in_b")[0],
        mm_tn(ub, dcm[None], name="dw_in_c")[0], mm_rows(ddtT, u, name="dw_in_dt").T,
        mm_tn(ub, dq[None], name="dw_in_q")[0], mm_tn(ub, dk[None], name="dw_in_k")[0],
        mm_tn(ub, dv[None], name="dw_in_v")[0]], axis=1)
    din = din.reshape(D, N_SHARD, IN_COLS // N_SHARD).transpose(1, 0, 2)
    sums_mx = rs_front([dmx, din], mix_tiles, "mx")
    (dx, df1, a1, dgate1, dup1, dg_ffn1_pre, dg_ffn1_post), parts_mx = ffn_bwd(
        dh1, x2, f1, gate1, up1, ffn1_pre_g, ffn1_post_g, wffn1, name="ffn1_bwd", ride=chips_ride(sums_mx))
    rmx, rin = rs_back(parts_mx, sums_mx, mix_tiles, "mx")
    sums_f1 = rs_front(ffn_grads(n1, dgate1, dup1, a1, df1, "1"), ffn_tiles, "f1")
    (rffn1,) = rs_back(run_ride(chips_ride(sums_f1), name="rs_chips_f1"), sums_f1, ffn_tiles, "f1")
    FS = D_FF // N_SHARD
    gw = {
        'ffn1_w_gate': rffn1[0:FS], 'ffn1_w_up': rffn1[FS:2 * FS], 'ffn1_w_down': rffn1[2 * FS:],
        'ffn2_w_gate': rffn2[0:FS], 'ffn2_w_up': rffn2[FS:2 * FS], 'ffn2_w_down': rffn2[2 * FS:],
        'w_ssm_proj': rmx[0:512], 'w_attn_proj': rmx[512:768], 'w_out': rmx[768:1024], 'w_in': rin,
    }

    dconv_w = jnp.concatenate([acc[:, :SSM_CONV].transpose(1, 0, 2).reshape(SSM_CONV, -1)
                               for acc in (acc_xs, acc_bm, acc_cm)], axis=1)
    dconv_b = jnp.concatenate([acc[:, SSM_CONV].reshape(-1) for acc in (acc_xs, acc_bm, acc_cm)])
    small_local = {
        'ffn1_pre_g': dg_ffn1_pre, 'ffn1_post_g': dg_ffn1_post, 'mix_pre_g': dg_mix_pre, 'conv_w': dconv_w,
        'conv_b': dconv_b, 'dt_bias': acc_head[:, :, 0], 'a_log': acc_head[:, :, 1],
        'd_skip': acc_xs[:, SSM_CONV + 2].reshape(SSM_HEADS, SSM_HEAD_DIM).sum(axis=1),
        'ssm_norm_g': acc_xs[:, SSM_CONV + 1].reshape(-1), 'attn_sinks': dsinks, 'rel_bias_table': dtable,
        'mix_post_g': dg_mix_post, 'ffn2_pre_g': dg_ffn2_pre, 'ffn2_post_g': dg_ffn2_post,
    }
    full_shapes = [(SSM_CONV, SSM_CONV_DIM) if n == 'conv_w' else w[n].shape for n in SMALL]
    packed = _pack_rows([small_local[n] for n in SMALL] + [jnp.sum(loss_parts[:, 0, 0])])
    total = small_allreduce(packed, name="allreduce_small")
    *small_g, loss = _unpack_rows(total, full_shapes + [()])
    for n, g in zip(SMALL, small_g):
        gw[n] = g
    gw['conv_w'] = lax.dynamic_slice(gw['conv_w'], (0, s_me * (SSM_CONV_DIM // N_SHARD)),
                                     (SSM_CONV, SSM_CONV_DIM // N_SHARD))[None]

    delta, new_m, new_v = {}, {}, {}
    for n in BIG:
        lay = tr if n.endswith(('w_gate', 'w_up')) else (lambda a: a)
        d_, m_, v_ = adamw(lay(w[n][0]), gw[n], lay(m[n][0]), lay(v[n][0]), name="adamw_" + n, rt=gw[n].shape[0] // 4)
        gw[n] = lay(gw[n])[None]
        delta[n], new_m[n], new_v[n] = lay(d_)[None], lay(m_)[None], lay(v_)[None]
    shapes = [w[n].shape for n in SMALL]
    outs = adamw(_pack_rows([w[n] for n in SMALL]), _pack_rows([gw[n] for n in SMALL]),
                 _pack_rows([m[n] for n in SMALL]), _pack_rows([v[n] for n in SMALL]), name="adamw_small")
    for res, buf in zip((delta, new_m, new_v), outs):
        for n, val in zip(SMALL, _unpack_rows(buf, shapes)):
            res[n] = val
    return (loss, dx.reshape(batch, seq, D), *[gw[n].reshape(w[n].shape) for n in WEIGHTS],
            *[delta[n] for n in WEIGHTS], *[new_m[n] for n in WEIGHTS], *[new_v[n] for n in WEIGHTS])
```

```python
import functools
import math

import jax
import jax.numpy as jnp
import numpy as np
from jax import lax
from jax.experimental import pallas as pl
from jax.experimental.pallas import tpu as pltpu

F32 = jnp.float32
BF16 = jnp.bfloat16

D_MODEL = 1024
D_FF = 2816
N_SHARD = 4
SSM_D_INNER = 2048
SSM_HEAD_DIM = 64
SSM_HEADS = 32
SSM_GROUPS = 4
SSM_HPG = SSM_HEADS // SSM_GROUPS
SSM_GW = SSM_D_INNER // SSM_GROUPS
SSM_STATE = 128
SSM_CONV = 4
SSM_CHUNK = 128
SSM_CONV_DIM = SSM_D_INNER + 2 * SSM_GROUPS * SSM_STATE
ATTN_Q_HEADS = 16
ATTN_KV_HEADS = 4
ATTN_REP = ATTN_Q_HEADS // ATTN_KV_HEADS
ATTN_HEAD_DIM = 64
ATTN_BLOCK = 128
ATTN_Q_DIM = 1024
ATTN_KV_DIM = 256
REL_BUCKETS = 32
REL_MAX_DISTANCE = 128
RMS_EPS = 1e-6
IN_COLS = 8736
ADAM_LR = 0.001
ADAM_B1 = 0.9
ADAM_B2 = 0.999
ADAM_EPS = 1e-08
ADAM_WD = 0.01
ADAM_STEP = 10
HALO = 8

VMEM_LIMIT = 56 * 1024 * 1024


def _cparams(*sem):
    return pltpu.CompilerParams(dimension_semantics=tuple(sem) if sem else None, vmem_limit_bytes=VMEM_LIMIT)


def _dot(a, b):
    return jnp.dot(a, b, preferred_element_type=F32)


def _dot_nt(a, b):
    return lax.dot_general(a, b, (((1,), (1,)), ((), ())), preferred_element_type=F32)


def _dot_tn(a, b):
    return lax.dot_general(a, b, (((0,), (0,)), ((), ())), preferred_element_type=F32)


def _dot_hi(a, b):
    return jnp.dot(a, b, preferred_element_type=F32, precision=lax.Precision.HIGHEST)


def _sigmoid(x):
    return 0.5 * jnp.tanh(0.5 * x) + 0.5


def _resident(shape, index=None):
    index = (0,) * len(shape) if index is None else tuple(index)
    return pl.BlockSpec(shape, lambda *_: index, pipeline_mode=pl.Buffered(1))


def _part(packed, rows, part):
    return _resident((N_SHARD, rows, packed.shape[2]), (0, part, 0))


def _rows(tm, width):
    return pl.BlockSpec((tm, width), lambda i: (i, 0))


class Ride:
    def __init__(self, inputs, out_shapes, aliases, scratch, start, finish):
        self.inputs, self.out_shapes, self.aliases = list(inputs), list(out_shapes), list(aliases)
        self.scratch, self.start, self.finish = list(scratch), start, finish


def join_rides(*rides):
    def cut(refs, sizes):
        out, at = [], 0
        for n in sizes:
            out.append(refs[at:at + n])
            at += n
        return out

    k_in = [len(r.inputs) for r in rides]
    k_out = [len(r.out_shapes) for r in rides]
    k_scr = [len(r.scratch) for r in rides]

    def each(step):
        def run(ins, outs, sems):
            for r, i, o, s in zip(rides, cut(ins, k_in), cut(outs, k_out), cut(sems, k_scr)):
                getattr(r, step)(i, o, s)
        return run

    aliases = [(sum(k_in[:n]) + i, sum(k_out[:n]) + j) for n, r in enumerate(rides) for i, j in r.aliases]
    return Ride([a for r in rides for a in r.inputs], [s for r in rides for s in r.out_shapes], aliases,
                [s for r in rides for s in r.scratch], each("start"), each("finish"))


def _call(body, *, grid, in_specs, args, out_specs, out_shape, name, sem, scratch=(), aliases=None, ride=None):
    aliases = dict(aliases or {})
    if ride is None:
        return pl.pallas_call(body, grid=grid, in_specs=in_specs, out_specs=out_specs, out_shape=out_shape,
                              scratch_shapes=list(scratch), input_output_aliases=aliases,
                              compiler_params=_cparams(*sem), name=name)(*args)
    n_in, n_out, n_scr = len(in_specs), len(out_specs), len(scratch)
    k_in, k_out = len(ride.inputs), len(ride.out_shapes)

    def riding(*refs):
        ins, refs = refs[:n_in], refs[n_in:]
        ex_in, refs = refs[:k_in], refs[k_in:]
        outs, refs = refs[:n_out], refs[n_out:]
        ex_out, refs = refs[:k_out], refs[k_out:]
        scr, ex_scr = refs[:n_scr], refs[n_scr:]
        first = functools.reduce(jnp.logical_and, [pl.program_id(a) == 0 for a in range(len(grid))])
        last = functools.reduce(jnp.logical_and, [pl.program_id(a) == grid[a] - 1 for a in range(len(grid))])

        @pl.when(first)
        def _():
            ride.start(ex_in, ex_out, ex_scr)

        body(*ins, *outs, *scr)

        @pl.when(last)
        def _():
            ride.finish(ex_in, ex_out, ex_scr)

    aliases.update({n_in + i: n_out + j for i, j in ride.aliases})
    res = pl.pallas_call(
        riding, grid=grid, in_specs=list(in_specs) + [ANY] * k_in, out_specs=list(out_specs) + [ANY] * k_out,
        out_shape=list(out_shape) + ride.out_shapes, scratch_shapes=list(scratch) + ride.scratch,
        input_output_aliases=aliases, compiler_params=_cparams(*["arbitrary"] * len(grid)), name=name,
    )(*args, *ride.inputs)
    return res[:n_out], res[n_out:]


def run_ride(ride, *, name):
    k_in = len(ride.inputs)

    def body(*refs):
        ex_in, ex_out, sems = refs[:k_in], refs[k_in:k_in + len(ride.out_shapes)], refs[k_in + len(ride.out_shapes):]
        ride.start(ex_in, ex_out, sems)
        ride.finish(ex_in, ex_out, sems)

    return pl.pallas_call(body, in_specs=[ANY] * k_in, out_specs=[ANY] * len(ride.out_shapes),
                          out_shape=ride.out_shapes, scratch_shapes=ride.scratch,
                          input_output_aliases=dict(ride.aliases), name=name)(*ride.inputs)


def ffn_fwd(h, g_pre, wffn, g_post, target=None, *, name, tm=512, ride=None):
    T, D = h.shape
    NS, FS = N_SHARD, wffn.shape[1] // 3
    with_loss = target is not None
    nt = T // tm

    def body(*refs):
        if with_loss:
            (h_ref, gpre_ref, wg_ref, wu_ref, wd_ref, gpost_ref, tgt_ref,
             hout_ref, n_ref, gate_ref, up_ref, f_ref, dy_ref, loss_ref) = refs
        else:
            (h_ref, gpre_ref, wg_ref, wu_ref, wd_ref, gpost_ref,
             hout_ref, n_ref, gate_ref, up_ref, f_ref) = refs
        hh = h_ref[...]
        r = lax.rsqrt(jnp.mean(hh * hh, axis=-1, keepdims=True) + RMS_EPS)
        n = (hh * r * gpre_ref[...]).astype(BF16)
        n_ref[...] = n
        acc = jnp.zeros((tm, D), F32)
        for s in range(NS):
            gate = _dot_nt(n, wg_ref[s])
            up = _dot_nt(n, wu_ref[s])
            gate_ref[s] = gate.astype(BF16)
            up_ref[s] = up.astype(BF16)
            a = (gate * _sigmoid(gate) * up).astype(BF16)
            acc = acc + _dot(a, wd_ref[s])
        f_ref[...] = acc
        r2 = lax.rsqrt(jnp.mean(acc * acc, axis=-1, keepdims=True) + RMS_EPS)
        out = hh + 0.5 * (acc * r2 * gpost_ref[...])
        hout_ref[...] = out
        if with_loss:
            e = out - tgt_ref[...]
            dy_ref[...] = e * (1.0 / D)
            loss_ref[...] = jnp.full((1, 8, 128), 0.5 / D, F32) * jnp.sum(e * e)

    in_specs = [_rows(tm, D), _resident((1, D)), _part(wffn, FS, 0), _part(wffn, FS, 1), _part(wffn, FS, 2),
                _resident((1, D))]
    args = [h, g_pre, wffn, wffn, wffn, g_post]
    out_shape = [jax.ShapeDtypeStruct((T, D), F32), jax.ShapeDtypeStruct((T, D), BF16),
                 jax.ShapeDtypeStruct((NS, T, FS), BF16), jax.ShapeDtypeStruct((NS, T, FS), BF16),
                 jax.ShapeDtypeStruct((T, D), F32)]
    seg = pl.BlockSpec((NS, tm, FS), lambda i: (0, i, 0))
    out_specs = [_rows(tm, D), _rows(tm, D), seg, seg, _rows(tm, D)]
    if with_loss:
        in_specs.append(_rows(tm, D))
        args.append(target)
        out_shape += [jax.ShapeDtypeStruct((T, D), F32), jax.ShapeDtypeStruct((nt, 8, 128), F32)]
        out_specs += [_rows(tm, D), pl.BlockSpec((1, 8, 128), lambda i: (i, 0, 0))]
    return _call(body, grid=(nt,), in_specs=in_specs, args=args, out_specs=out_specs, out_shape=out_shape,
                 sem=("parallel",), name=name, ride=ride)


def ffn_bwd(dout, h, f, gate, up, g_pre, g_post, wffn, *, name, tm=256, ride=None):
    T, D = h.shape
    NS, FS = N_SHARD, wffn.shape[1] // 3
    nt = T // tm

    def body(dout_ref, h_ref, f_ref, gate_ref, up_ref, gpre_ref, gpost_ref, wg_ref, wu_ref, wd_ref,
             dh_ref, df_ref, a_ref, dgate_ref, dup_ref, dgpre_ref, dgpost_ref):
        @pl.when(pl.program_id(0) == 0)
        def _():
            dgpre_ref[...] = jnp.zeros_like(dgpre_ref)
            dgpost_ref[...] = jnp.zeros_like(dgpost_ref)

        do = dout_ref[...]
        ff = f_ref[...]
        d_fn = 0.5 * do
        r2 = lax.rsqrt(jnp.mean(ff * ff, axis=-1, keepdims=True) + RMS_EPS)
        dgpost_ref[...] += jnp.sum(d_fn * ff * r2, axis=0, keepdims=True)
        t = d_fn * gpost_ref[...]
        df = r2 * t - ff * (r2 * r2 * r2 * jnp.mean(t * ff, axis=-1, keepdims=True))
        dfb = df.astype(BF16)
        df_ref[...] = dfb
        dn = jnp.zeros((tm, D), F32)
        for s in range(NS):
            da = _dot_nt(dfb, wd_ref[s])
            g = gate_ref[s].astype(F32)
            u = up_ref[s].astype(F32)
            sg = _sigmoid(g)
            silu = g * sg
            a_ref[s] = (silu * u).astype(BF16)
            dgt = (da * u * (sg * (1.0 + g * (1.0 - sg)))).astype(BF16)
            dupv = (da * silu).astype(BF16)
            dgate_ref[s] = dgt
            dup_ref[s] = dupv
            dn = dn + _dot(dgt, wg_ref[s]) + _dot(dupv, wu_ref[s])
        hh = h_ref[...]
        r1 = lax.rsqrt(jnp.mean(hh * hh, axis=-1, keepdims=True) + RMS_EPS)
        dgpre_ref[...] += jnp.sum(dn * hh * r1, axis=0, keepdims=True)
        t = dn * gpre_ref[...]
        dh_ref[...] = do + r1 * t - hh * (r1 * r1 * r1 * jnp.mean(t * hh, axis=-1, keepdims=True))

    seg = pl.BlockSpec((NS, tm, FS), lambda i: (0, i, 0))
    acc = pl.BlockSpec((1, D), lambda i: (0, 0))
    return _call(
        body, grid=(nt,),
        in_specs=[_rows(tm, D), _rows(tm, D), _rows(tm, D), seg, seg, _resident((1, D)), _resident((1, D)),
                  _part(wffn, FS, 0), _part(wffn, FS, 1), _part(wffn, FS, 2)],
        args=[dout, h, f, gate, up, g_pre, g_post, wffn, wffn, wffn],
        out_specs=[_rows(tm, D), _rows(tm, D), seg, seg, seg, acc, acc],
        out_shape=[jax.ShapeDtypeStruct((T, D), F32), jax.ShapeDtypeStruct((T, D), BF16),
                   jax.ShapeDtypeStruct((NS, T, FS), BF16), jax.ShapeDtypeStruct((NS, T, FS), BF16),
                   jax.ShapeDtypeStruct((NS, T, FS), BF16),
                   jax.ShapeDtypeStruct((1, D), F32), jax.ShapeDtypeStruct((1, D), F32)],
        sem=("arbitrary",), name=name, ride=ride)


def mm_tn(a, g, *, name, tt=2048, tn=None, a_cols=None, into=None):
    Ba, T, _ = a.shape
    Bg, _, N = g.shape
    B, K = a_cols if a_cols else (max(Ba, Bg), a.shape[2])
    tn = N if tn is None else tn
    tt = min(tt, T)
    nsteps = T // tt

    def body(*refs):
        a_ref, g_ref, o_ref = refs[0], refs[1], refs[-1]

        @pl.when(pl.program_id(2) == 0)
        def _():
            o_ref[...] = jnp.zeros_like(o_ref)

        o_ref[0] += _dot_tn(a_ref[0], g_ref[0].astype(BF16))

    if a_cols:
        a_map = lambda b, j, t: (0, t, b)
    else:
        a_map = (lambda b, j, t: (b, t, 0)) if Ba > 1 else (lambda b, j, t: (0, t, 0))
    in_specs = [pl.BlockSpec((1, tt, K), a_map),
                pl.BlockSpec((1, tt, tn), (lambda b, j, t: (b, t, j)) if Bg > 1 else (lambda b, j, t: (0, t, j)))]
    args = [a, g]
    if into is None:
        out_shape, part, aliases = jax.ShapeDtypeStruct((B, K, N), F32), 0, {}
    else:
        buf, part = into
        out_shape, aliases = jax.ShapeDtypeStruct(buf.shape, F32), {2: 0}
        in_specs.append(ANY)
        args.append(buf)
    return pl.pallas_call(
        body, grid=(B, N // tn, nsteps), in_specs=in_specs,
        out_specs=pl.BlockSpec((1, K, tn), lambda b, j, t: (b, part, j)),
        out_shape=out_shape, input_output_aliases=aliases,
        compiler_params=_cparams("parallel", "parallel", "arbitrary"), name=name)(*args)


def mix_in_fwd(h, g, w_gz, w_xbc, w_dtT, w_qkv, *, name, tm=256, ride=None):
    T, D = h.shape
    nt = T // tm
    CB = 1024

    def body(h_ref, g_ref, wgz_ref, wxbc_ref, wdtT_ref, wqkv_ref,
             u_ref, gates_ref, z_ref, xbc_ref, dtT_ref, q_ref, k_ref, v_ref):
        hh = h_ref[...]
        r = lax.rsqrt(jnp.mean(hh * hh, axis=-1, keepdims=True) + RMS_EPS)
        u = (hh * r * g_ref[...]).astype(BF16)
        u_ref[...] = u
        for cb in range(0, 2048, CB):
            gates_ref[:, cb:cb + CB] = _dot(u, wgz_ref[:, cb:cb + CB]).astype(BF16)
            z_ref[:, cb:cb + CB] = _dot(u, wgz_ref[:, 2048 + cb:2048 + cb + CB])
        for cb in range(0, SSM_CONV_DIM, CB):
            xbc_ref[:, cb:cb + CB] = _dot(u, wxbc_ref[:, cb:cb + CB])
        dtT_ref[...] = _dot_nt(wdtT_ref[...], u)
        q_ref[...] = _dot(u, wqkv_ref[:, 0:ATTN_Q_DIM]).astype(BF16)
        k_ref[...] = _dot(u, wqkv_ref[:, ATTN_Q_DIM:ATTN_Q_DIM + ATTN_KV_DIM]).astype(BF16)
        v_ref[...] = _dot(u, wqkv_ref[:, ATTN_Q_DIM + ATTN_KV_DIM:]).astype(BF16)

    sds = jax.ShapeDtypeStruct
    return _call(
        body, grid=(nt,),
        in_specs=[_rows(tm, D), _resident((1, D)), _resident(w_gz.shape), _resident(w_xbc.shape),
                  _resident(w_dtT.shape), _resident(w_qkv.shape)],
        args=[h, g, w_gz, w_xbc, w_dtT, w_qkv],
        out_specs=[_rows(tm, D), _rows(tm, 2048), _rows(tm, 2048), _rows(tm, SSM_CONV_DIM),
                   pl.BlockSpec((SSM_HEADS, tm), lambda i: (0, i)),
                   _rows(tm, ATTN_Q_DIM), _rows(tm, ATTN_KV_DIM), _rows(tm, ATTN_KV_DIM)],
        out_shape=[sds((T, D), BF16), sds((T, 2048), BF16), sds((T, 2048), F32), sds((T, SSM_CONV_DIM), F32),
                   sds((SSM_HEADS, T), F32),
                   sds((T, ATTN_Q_DIM), BF16), sds((T, ATTN_KV_DIM), BF16), sds((T, ATTN_KV_DIM), BF16)],
        sem=("parallel",), name=name, ride=ride)


def _softplus(x):
    return jnp.maximum(x, 0.0) + jnp.log(1.0 + jnp.exp(-jnp.abs(x)))


def _iota(shape, axis):
    return lax.broadcasted_iota(jnp.int32, shape, axis)


def _head_expand(g, per_head):
    shape = (SSM_HEADS, SSM_HPG * per_head)
    head = lax.shift_right_logical(_iota(shape, 1), int(math.log2(per_head)))
    return (_iota(shape, 0) == SSM_HPG * g + head).astype(F32)


def _conv_pre(x_ref, halo_ref, w_ref, b_ref, xp_ref, first):
    Q = SSM_CHUNK
    halo = jnp.where(first, 0.0, halo_ref[...])
    xp_ref[0:HALO, :] = halo
    xp_ref[HALO:HALO + Q, :] = x_ref[...]
    pre = b_ref[...] + w_ref[3:4, :] * xp_ref[HALO:HALO + Q, :]
    for k in range(SSM_CONV - 1):
        pre = pre + w_ref[k:k + 1, :] * xp_ref[pl.ds(HALO - 3 + k, Q), :]
    return pre


def _ssd_specs(nc):
    Q, GW, N = SSM_CHUNK, SSM_GW, SSM_STATE
    nb_xs = SSM_D_INNER // N
    nb_c = nb_xs + SSM_GROUPS

    def rb(cmap):
        def row(b, c, g):
            return b * nc + cmap(c)
        return row

    def specs(cmap):
        row = rb(cmap)
        hrow = lambda b, c, g: jnp.maximum(row(b, c, g) * (Q // HALO) - 1, 0)
        return dict(
            xs=pl.BlockSpec((Q, GW), lambda b, c, g: (row(b, c, g), g)),
            bm=pl.BlockSpec((Q, N), lambda b, c, g: (row(b, c, g), nb_xs + g)),
            cm=pl.BlockSpec((Q, N), lambda b, c, g: (row(b, c, g), nb_c + g)),
            xs_halo=pl.BlockSpec((HALO, GW), lambda b, c, g: (hrow(b, c, g), g)),
            bm_halo=pl.BlockSpec((HALO, N), lambda b, c, g: (hrow(b, c, g), nb_xs + g)),
            cm_halo=pl.BlockSpec((HALO, N), lambda b, c, g: (hrow(b, c, g), nb_c + g)),
            grp=pl.BlockSpec((Q, GW), lambda b, c, g: (row(b, c, g), g)),
            dt=pl.BlockSpec((Q, SSM_HEADS), lambda b, c, g: (row(b, c, g), 0)),
            dtT=pl.BlockSpec((SSM_HEADS, Q), lambda b, c, g: (0, row(b, c, g))),
            w_xs=pl.BlockSpec((SSM_CONV, GW), lambda b, c, g: (0, g)),
            w_bm=pl.BlockSpec((SSM_CONV, N), lambda b, c, g: (0, nb_xs + g)),
            w_cm=pl.BlockSpec((SSM_CONV, N), lambda b, c, g: (0, nb_c + g)),
            b_xs=pl.BlockSpec((1, GW), lambda b, c, g: (0, g)),
            b_bm=pl.BlockSpec((1, N), lambda b, c, g: (0, nb_xs + g)),
            b_cm=pl.BlockSpec((1, N), lambda b, c, g: (0, nb_c + g)),
            vec_g=pl.BlockSpec((1, GW), lambda b, c, g: (0, g)),
            row32=pl.BlockSpec((1, SSM_HEADS), lambda b, c, g: (0, 0)),
            col32=pl.BlockSpec((SSM_HEADS, 1), lambda b, c, g: (0, 0)),
            state=pl.BlockSpec((1, SSM_HPG, SSM_HEAD_DIM, N), lambda b, c, g: (row(b, c, g), g, 0, 0)),
            dtT_g=pl.BlockSpec((SSM_HPG, Q), lambda b, c, g: (g, row(b, c, g))),
            col_g=pl.BlockSpec((SSM_HPG, 1), lambda b, c, g: (g, 0)),
            pairs=pl.BlockSpec((1, SSM_HPG // 2, N, 2 * SSM_HEAD_DIM), lambda b, c, g: (row(b, c, g), g, 0, 0)),
        )
    return specs


def _ssd_chunk_common(first, g, xs_ref, bm_ref, cm_ref, xs_halo, bm_halo, cm_halo, w_xs, w_bm, w_cm, b_xs, b_bm, b_cm,
                      dt_ref, dtT_ref, dtb_ref, dtbT_ref, alog_ref, alogT_ref, xp_xs, xp_bm, xp_cm):
    Q = SSM_CHUNK
    pre_xs = _conv_pre(xs_ref, xs_halo, w_xs, b_xs, xp_xs, first)
    pre_bm = _conv_pre(bm_ref, bm_halo, w_bm, b_bm, xp_bm, first)
    pre_cm = _conv_pre(cm_ref, cm_halo, w_cm, b_cm, xp_cm, first)
    xs = pre_xs * _sigmoid(pre_xs)
    bm = pre_bm * _sigmoid(pre_bm)
    cm = pre_cm * _sigmoid(pre_cm)
    dtr = dt_ref[...] + dtb_ref[...]
    dtrT = dtT_ref[...] + dtbT_ref[...]
    dt = _softplus(dtr)
    dtT = _softplus(dtrT)
    a = -jnp.exp(alog_ref[...])
    aT = -jnp.exp(alogT_ref[...])
    tri = (_iota((Q, Q), 0) >= _iota((Q, Q), 1)).astype(F32)
    triT = (_iota((Q, Q), 0) <= _iota((Q, Q), 1)).astype(F32)
    acs = _dot_hi(tri, dt * a)
    acsT = _dot_hi(dtT * aT, triT)
    return dict(pre_xs=pre_xs, pre_bm=pre_bm, pre_cm=pre_cm, xs=xs, bm=bm, cm=cm, dtr=dtr, dt=dt, a=a,
                acs=acs, acsT=acsT, tri=tri, triT=triT)


def _v1_ssd_fwd(xbc, z, dt_raw, dt_rawT, conv_w, conv_b, dt_bias, a_log, d_skip, norm_g, *, batch, name):
    T = xbc.shape[0]
    Q, GW, N, P, HPG = SSM_CHUNK, SSM_GW, SSM_STATE, SSM_HEAD_DIM, SSM_HPG
    nc = T // batch // Q
    sp = _ssd_specs(nc)(lambda c: c)

    def body(xs_ref, bm_ref, cm_ref, xs_halo, bm_halo, cm_halo, z_ref, dt_ref, dtT_ref,
             w_xs, w_bm, w_cm, b_xs, b_bm, b_cm, dtb_ref, dtbT_ref, alog_ref, alogT_ref, dskip_ref, ng_ref,
             y_ref, ys_ref, st_ref, state, acsT_s, y_s, xp_xs, xp_bm, xp_cm):
        c = pl.program_id(1)
        g = pl.program_id(2)
        first = c == 0
        cc = _ssd_chunk_common(first, g, xs_ref, bm_ref, cm_ref, xs_halo, bm_halo, cm_halo, w_xs, w_bm, w_cm,
                               b_xs, b_bm, b_cm, dt_ref, dtT_ref, dtb_ref, dtbT_ref, alog_ref, alogT_ref,
                               xp_xs, xp_bm, xp_cm)
        xs, acs = cc["xs"], cc["acs"]
        acsT_s[...] = cc["acsT"]
        e64 = _head_expand(g, P)
        e128 = _head_expand(g, Q)
        acs_x = _dot_hi(acs, e64)
        acs_b = _dot_hi(acs, e128)
        x = xs * _dot_hi(cc["dt"], e64)
        last_x = acs_x[Q - 1:Q, :]
        xw = (x * jnp.exp(last_x - acs_x)).astype(BF16)
        ex = jnp.exp(acs_x)
        xb = x.astype(BF16)
        bb = cc["bm"].astype(BF16)
        cb = cc["cm"].astype(BF16)
        s = _dot_nt(cb, bb)
        causal = _iota((Q, Q), 0) >= _iota((Q, Q), 1)
        for r in range(HPG):
            hd = HPG * g + r

            @pl.when(first)
            def _():
                state[hd] = jnp.zeros((P, N), F32)

            seg = acs_b[:, Q * r:Q * (r + 1)] - acsT_s[pl.ds(hd, 1), :]
            m = (s * jnp.exp(jnp.where(causal, seg, -1e30))).astype(BF16)
            hp = state[hd]
            st_ref[0, r] = hp
            y_h = _dot(m, xb[:, P * r:P * (r + 1)]) + _dot_nt(cb, hp.astype(BF16)) * ex[:, P * r:P * (r + 1)]
            y_s[:, P * r:P * (r + 1)] = y_h
            decay = jnp.exp(acsT_s[pl.ds(hd, 1), pl.ds(Q - 1, 1)])
            state[hd] = hp * decay + _dot_tn(xw[:, P * r:P * (r + 1)], bb)
        dexp = _dot_hi(jnp.broadcast_to(dskip_ref[...], (8, SSM_HEADS)), e64)[0:1, :]
        y = y_s[...] + dexp * xs
        y_ref[...] = y
        zz = z_ref[...]
        yg = y * (zz * _sigmoid(zz))
        rr = lax.rsqrt(jnp.mean(yg * yg, axis=-1, keepdims=True) + RMS_EPS)
        ys_ref[...] = (yg * rr * ng_ref[...]).astype(BF16)

    col = lambda v: v.reshape(SSM_HEADS, 1)
    sds = jax.ShapeDtypeStruct
    return pl.pallas_call(
        body, grid=(batch, nc, SSM_GROUPS),
        in_specs=[sp["xs"], sp["bm"], sp["cm"], sp["xs_halo"], sp["bm_halo"], sp["cm_halo"], sp["grp"], sp["dt"],
                  sp["dtT"], sp["w_xs"], sp["w_bm"], sp["w_cm"], sp["b_xs"], sp["b_bm"], sp["b_cm"],
                  sp["row32"], sp["col32"], sp["row32"], sp["col32"], sp["row32"], sp["vec_g"]],
        out_specs=[sp["grp"], sp["grp"], sp["state"]],
        out_shape=[sds((T, SSM_D_INNER), F32), sds((T, SSM_D_INNER), BF16),
                   sds((T // Q, SSM_HEADS, P, N), F32)],
        scratch_shapes=[pltpu.VMEM((SSM_HEADS, P, N), F32), pltpu.VMEM((SSM_HEADS, Q), F32), pltpu.VMEM((Q, GW), F32),
                        pltpu.VMEM((HALO + Q, GW), F32), pltpu.VMEM((HALO + Q, N), F32), pltpu.VMEM((HALO + Q, N), F32)],
        compiler_params=_cparams("arbitrary", "arbitrary", "arbitrary"), name=name,
    )(xbc, xbc, xbc, xbc, xbc, xbc, z, dt_raw, dt_rawT, conv_w, conv_w, conv_w, conv_b, conv_b, conv_b,
      dt_bias, col(dt_bias), a_log, col(a_log), d_skip, norm_g)


def _attn_specs(nb):
    BLK = ATTN_BLOCK

    def specs(last):
        def cur(b, n):
            return b * nb + (n if last is None else jnp.minimum(n, nb - 1))

        def prev(b, n):
            return b * nb + jnp.maximum((n if last is None else jnp.minimum(n, nb - 1)) - 1, 0)
        return cur, prev
    return specs


def _attn_masks(n):
    shape = (ATTN_REP * ATTN_BLOCK, ATTN_BLOCK)
    ii = jnp.bitwise_and(_iota(shape, 0), ATTN_BLOCK - 1)
    jj = _iota(shape, 1)
    return jnp.logical_and(jj > ii, n > 0), jj <= ii


def _attn_group(kk, q_ref, bias_ref, sink_ref):
    BLK, HD = ATTN_BLOCK, ATTN_HEAD_DIM
    heads = range(ATTN_REP * kk, ATTN_REP * (kk + 1))
    qg = jnp.concatenate([q_ref[:, HD * hd:HD * (hd + 1)] for hd in heads], axis=0)
    bias_p = jnp.concatenate([bias_ref[hd, :, 0:BLK] for hd in heads], axis=0)
    bias_c = jnp.concatenate([bias_ref[hd, :, BLK:2 * BLK] for hd in heads], axis=0)
    sink = jnp.concatenate([jnp.broadcast_to(sink_ref[0:1, hd:hd + 1], (BLK, 1)) for hd in heads], axis=0)
    return qg, bias_p, bias_c, sink


def attn_bias(table_t, onehot, *, name):
    def body(t_ref, f_ref, o_ref):
        o_ref[...] = _dot_hi(t_ref[...], f_ref[...])
    return pl.pallas_call(body, out_shape=jax.ShapeDtypeStruct((ATTN_Q_HEADS, onehot.shape[1]), F32),
                          compiler_params=_cparams(), name=name)(table_t, onehot)


def attn_bias_bwd(dbias, onehot, *, name):
    def body(d_ref, f_ref, o_ref):
        o_ref[...] = lax.dot_general(d_ref[...], f_ref[...], (((1,), (1,)), ((), ())), preferred_element_type=F32,
                                     precision=lax.Precision.HIGHEST)
    return pl.pallas_call(body, out_shape=jax.ShapeDtypeStruct((ATTN_Q_HEADS, REL_BUCKETS), F32),
                          compiler_params=_cparams(), name=name)(dbias, onehot)


def attn_fwd(q, k, v, bias, sinks, *, batch, name):
    T = q.shape[0]
    BLK, HD = ATTN_BLOCK, ATTN_HEAD_DIM
    nb = T // batch // BLK
    cur, prev = _attn_specs(nb)(None)
    scale = HD ** -0.5

    def body(q_ref, kc_ref, kp_ref, vc_ref, vp_ref, bias_ref, sink_ref, o_ref, lse_ref):
        n = pl.program_id(1)
        m_prev, m_cur = _attn_masks(n)
        for kk in range(ATTN_KV_HEADS):
            ks = slice(HD * kk, HD * (kk + 1))
            kc, kp, vc, vp = kc_ref[:, ks], kp_ref[:, ks], vc_ref[:, ks], vp_ref[:, ks]
            qg, bias_p, bias_c, sink = _attn_group(kk, q_ref, bias_ref, sink_ref)
            lp = jnp.where(m_prev, _dot_nt(qg, kp) * scale + bias_p, -1e30)
            lc = jnp.where(m_cur, _dot_nt(qg, kc) * scale + bias_c, -1e30)
            mx = jnp.maximum(jnp.max(jnp.maximum(lp, lc), axis=-1, keepdims=True), sink)
            pp = jnp.exp(lp - mx)
            pc = jnp.exp(lc - mx)
            den = jnp.sum(pp + pc, axis=-1, keepdims=True) + jnp.exp(sink - mx)
            o = ((_dot(pp.astype(BF16), vp) + _dot(pc.astype(BF16), vc)) * (1.0 / den)).astype(BF16)
            lse = mx + jnp.log(den)
            for r in range(ATTN_REP):
                hd = ATTN_REP * kk + r
                o_ref[:, HD * hd:HD * (hd + 1)] = o[BLK * r:BLK * (r + 1)]
                lse_ref[:, hd:hd + 1] = lse[BLK * r:BLK * (r + 1)]

    sds = jax.ShapeDtypeStruct
    return pl.pallas_call(
        body, grid=(batch, nb),
        in_specs=[pl.BlockSpec((BLK, ATTN_Q_DIM), lambda b, n: (cur(b, n), 0)),
                  pl.BlockSpec((BLK, ATTN_KV_DIM), lambda b, n: (cur(b, n), 0)),
                  pl.BlockSpec((BLK, ATTN_KV_DIM), lambda b, n: (prev(b, n), 0)),
                  pl.BlockSpec((BLK, ATTN_KV_DIM), lambda b, n: (cur(b, n), 0)),
                  pl.BlockSpec((BLK, ATTN_KV_DIM), lambda b, n: (prev(b, n), 0)),
                  pl.BlockSpec((ATTN_Q_HEADS, BLK, 2 * BLK), lambda b, n: (0, 0, 0)),
                  pl.BlockSpec((1, ATTN_Q_HEADS), lambda b, n: (0, 0))],
        out_specs=[pl.BlockSpec((BLK, ATTN_Q_DIM), lambda b, n: (cur(b, n), 0)),
                   pl.BlockSpec((BLK, ATTN_Q_HEADS), lambda b, n: (cur(b, n), 0))],
        out_shape=[sds((T, ATTN_Q_DIM), BF16), sds((T, ATTN_Q_HEADS), F32)],
        compiler_params=_cparams("parallel", "parallel"), name=name)(q, k, k, v, v, bias, sinks)


def _proj_specs(wmix):
    return [_part(wmix, 512, 0), _part(wmix, 256, 2), _part(wmix, 256, 3)]


def _natural(w_ref):
    return w_ref[...].reshape(-1, w_ref.shape[2])


def mix_out_fwd(ys, o, gates, h, wmix, g_post, *, name, tm=512):
    T, D = h.shape
    nt = T // tm

    def body(ys_ref, o_ref, gates_ref, h_ref, wssm_ref, wattn_ref, wout_ref, g_ref,
             hout_ref, yssm_ref, yattn_ref, mix_ref, merged_ref):
        y_ssm = _dot(ys_ref[...], _natural(wssm_ref))
        y_attn = _dot(o_ref[...], _natural(wattn_ref))
        yssm_ref[...] = y_ssm.astype(BF16)
        yattn_ref[...] = y_attn.astype(BF16)
        merged = (_sigmoid(gates_ref[:, 0:D].astype(F32)) * y_ssm
                  + _sigmoid(gates_ref[:, D:2 * D].astype(F32)) * y_attn).astype(BF16)
        merged_ref[...] = merged
        mix = _dot(merged, _natural(wout_ref))
        mix_ref[...] = mix.astype(BF16)
        r = lax.rsqrt(jnp.mean(mix * mix, axis=-1, keepdims=True) + RMS_EPS)
        hout_ref[...] = h_ref[...] + mix * r * g_ref[...]

    sds = jax.ShapeDtypeStruct
    return pl.pallas_call(
        body, grid=(nt,),
        in_specs=[_rows(tm, SSM_D_INNER), _rows(tm, ATTN_Q_DIM), _rows(tm, 2 * D), _rows(tm, D),
                  *_proj_specs(wmix), _resident((1, D))],
        out_specs=[_rows(tm, D)] * 5,
        out_shape=[sds((T, D), F32), sds((T, D), BF16), sds((T, D), BF16), sds((T, D), BF16), sds((T, D), BF16)],
        compiler_params=_cparams("parallel"), name=name)(ys, o, gates, h, wmix, wmix, wmix, g_post)


def mix_out_bwd(dh, mix, y_ssm, y_attn, gates, wmix, g_post, *, name, tm=256, ride=None):
    T, D = dh.shape
    nt = T // tm

    def body(dh_ref, mix_ref, yssm_ref, yattn_ref, gates_ref, wssm_ref, wattn_ref, wout_ref, g_ref,
             dmix_ref, dyssm_ref, dyattn_ref, dgates_ref, dys_ref, do_ref, dg_ref):
        @pl.when(pl.program_id(0) == 0)
        def _():
            dg_ref[...] = jnp.zeros_like(dg_ref)

        do = dh_ref[...]
        mix = mix_ref[...].astype(F32)
        r = lax.rsqrt(jnp.mean(mix * mix, axis=-1, keepdims=True) + RMS_EPS)
        dg_ref[...] += jnp.sum(do * mix * r, axis=0, keepdims=True)
        t = do * g_ref[...]
        dmix = (r * t - mix * (r * r * r * jnp.mean(t * mix, axis=-1, keepdims=True))).astype(BF16)
        dmix_ref[...] = dmix
        dmerged = _dot_nt(dmix, _natural(wout_ref))
        s1 = _sigmoid(gates_ref[:, 0:D].astype(F32))
        s2 = _sigmoid(gates_ref[:, D:2 * D].astype(F32))
        dyssm = (dmerged * s1).astype(BF16)
        dyattn = (dmerged * s2).astype(BF16)
        dyssm_ref[...] = dyssm
        dyattn_ref[...] = dyattn
        dgates_ref[:, 0:D] = (dmerged * yssm_ref[...].astype(F32) * (s1 * (1.0 - s1))).astype(BF16)
        dgates_ref[:, D:2 * D] = (dmerged * yattn_ref[...].astype(F32) * (s2 * (1.0 - s2))).astype(BF16)
        dys_ref[...] = _dot_nt(dyssm, _natural(wssm_ref))
        do_ref[...] = _dot_nt(dyattn, _natural(wattn_ref)).astype(BF16)

    sds = jax.ShapeDtypeStruct
    return _call(
        body, grid=(nt,),
        in_specs=[_rows(tm, D), _rows(tm, D), _rows(tm, D), _rows(tm, D), _rows(tm, 2 * D),
                  *_proj_specs(wmix), _resident((1, D))],
        args=[dh, mix, y_ssm, y_attn, gates, wmix, wmix, wmix, g_post],
        out_specs=[_rows(tm, D), _rows(tm, D), _rows(tm, D), _rows(tm, 2 * D), _rows(tm, SSM_D_INNER),
                   _rows(tm, ATTN_Q_DIM), pl.BlockSpec((1, D), lambda i: (0, 0))],
        out_shape=[sds((T, D), BF16), sds((T, D), BF16), sds((T, D), BF16), sds((T, 2 * D), BF16),
                   sds((T, SSM_D_INNER), F32), sds((T, ATTN_Q_DIM), BF16), sds((1, D), F32)],
        sem=("arbitrary",), name=name, ride=ride)


def attn_bwd(q, k, v, o, do, lse, bias, sinks, *, batch, name):
    T = q.shape[0]
    BLK, HD = ATTN_BLOCK, ATTN_HEAD_DIM
    nb = T // batch // BLK
    cur, prev = _attn_specs(nb)(nb)
    scale = HD ** -0.5

    def body(q_ref, kc_ref, kp_ref, vc_ref, vp_ref, o_ref, do_ref, lse_ref, bias_ref, sink_ref,
             dq_ref, dk_ref, dv_ref, dbias_ref, dsink_ref, ck, cv):
        b = pl.program_id(0)
        n = pl.program_id(1)

        @pl.when(jnp.logical_and(b == 0, n == 0))
        def _():
            dbias_ref[...] = jnp.zeros_like(dbias_ref)
            dsink_ref[...] = jnp.zeros_like(dsink_ref)

        @pl.when(n == 0)
        def _():
            ck[...] = jnp.zeros_like(ck)
            cv[...] = jnp.zeros_like(cv)

        @pl.when(n == nb)
        def _():
            dk_ref[...] = ck[...].astype(BF16)
            dv_ref[...] = cv[...].astype(BF16)

        @pl.when(n < nb)
        def _():
            m_prev, m_cur = _attn_masks(n)
            lane16 = _iota((1, ATTN_Q_HEADS), 1)
            dsink = jnp.zeros((1, ATTN_Q_HEADS), F32)
            for kk in range(ATTN_KV_HEADS):
                ks = slice(HD * kk, HD * (kk + 1))
                kc, kp, vc, vp = kc_ref[:, ks], kp_ref[:, ks], vc_ref[:, ks], vp_ref[:, ks]
                heads = range(ATTN_REP * kk, ATTN_REP * (kk + 1))
                qg, bias_p, bias_c, sink = _attn_group(kk, q_ref, bias_ref, sink_ref)
                dog = jnp.concatenate([do_ref[:, HD * hd:HD * (hd + 1)] for hd in heads], axis=0)
                og = jnp.concatenate([o_ref[:, HD * hd:HD * (hd + 1)] for hd in heads], axis=0)
                lse = jnp.concatenate([lse_ref[:, hd:hd + 1] for hd in heads], axis=0)
                lp = jnp.where(m_prev, _dot_nt(qg, kp) * scale + bias_p, -1e30)
                lc = jnp.where(m_cur, _dot_nt(qg, kc) * scale + bias_c, -1e30)
                pp = jnp.exp(lp - lse)
                pc = jnp.exp(lc - lse)
                delta = jnp.sum(dog.astype(F32) * og.astype(F32), axis=-1, keepdims=True)
                dlp = pp * (_dot_nt(dog, vp) - delta)
                dlc = pc * (_dot_nt(dog, vc) - delta)
                sd = jnp.exp(sink - lse) * delta
                dlpb = dlp.astype(BF16)
                dlcb = dlc.astype(BF16)
                dqg = ((_dot(dlpb, kp) + _dot(dlcb, kc)) * scale).astype(BF16)
                for r, hd in enumerate(heads):
                    rows = slice(BLK * r, BLK * (r + 1))
                    dsink = dsink + jnp.where(lane16 == hd, -jnp.sum(sd[rows], axis=0, keepdims=True), 0.0)
                    dbias_ref[hd, :, 0:BLK] += dlp[rows]
                    dbias_ref[hd, :, BLK:2 * BLK] += dlc[rows]
                    dq_ref[:, HD * hd:HD * (hd + 1)] = dqg[rows]
                dk_ref[:, ks] = (ck[:, ks] + _dot_tn(dlpb, qg) * scale).astype(BF16)
                dv_ref[:, ks] = (cv[:, ks] + _dot_tn(pp.astype(BF16), dog)).astype(BF16)
                ck[:, ks] = _dot_tn(dlcb, qg) * scale
                cv[:, ks] = _dot_tn(pc.astype(BF16), dog)
            dsink_ref[...] += dsink

    sds = jax.ShapeDtypeStruct
    qspec = pl.BlockSpec((BLK, ATTN_Q_DIM), lambda b, n: (cur(b, n), 0))
    cspec = pl.BlockSpec((BLK, ATTN_KV_DIM), lambda b, n: (cur(b, n), 0))
    pspec = pl.BlockSpec((BLK, ATTN_KV_DIM), lambda b, n: (prev(b, n), 0))
    late = pl.BlockSpec((BLK, ATTN_KV_DIM), lambda b, n: (b * nb + jnp.maximum(n - 1, 0), 0))
    return pl.pallas_call(
        body, grid=(batch, nb + 1),
        in_specs=[qspec, cspec, pspec, cspec, pspec, qspec, qspec,
                  pl.BlockSpec((BLK, ATTN_Q_HEADS), lambda b, n: (cur(b, n), 0)),
                  pl.BlockSpec((ATTN_Q_HEADS, BLK, 2 * BLK), lambda b, n: (0, 0, 0)),
                  pl.BlockSpec((1, ATTN_Q_HEADS), lambda b, n: (0, 0))],
        out_specs=[qspec, late, late,
                   pl.BlockSpec((ATTN_Q_HEADS, BLK, 2 * BLK), lambda b, n: (0, 0, 0)),
                   pl.BlockSpec((1, ATTN_Q_HEADS), lambda b, n: (0, 0))],
        out_shape=[sds((T, ATTN_Q_DIM), BF16), sds((T, ATTN_KV_DIM), BF16), sds((T, ATTN_KV_DIM), BF16),
                   sds((ATTN_Q_HEADS, BLK, 2 * BLK), F32), sds((1, ATTN_Q_HEADS), F32)],
        scratch_shapes=[pltpu.VMEM((BLK, ATTN_KV_DIM), F32), pltpu.VMEM((BLK, ATTN_KV_DIM), F32)],
        compiler_params=_cparams("arbitrary", "arbitrary"), name=name)(q, k, k, v, v, o, do, lse, bias, sinks)


def _conv_bwd(dxc, pre, xp_ref, w_ref, carry_ref, acc_ref, dp_ref, g, last):
    Q = SSM_CHUNK
    sg = _sigmoid(pre)
    dpre = dxc * (sg * (1.0 + pre * (1.0 - sg)))
    dp_ref[0:Q, :] = dpre
    dp_ref[Q:Q + HALO, :] = carry_ref[g]
    carry_ref[g] = dpre[0:HALO, :]
    rows = [jnp.sum(dpre * xp_ref[pl.ds(HALO - 3 + k, Q), :], axis=0, keepdims=True) for k in range(SSM_CONV)]
    rows.append(jnp.sum(dpre, axis=0, keepdims=True))
    rows.append(jnp.zeros((HALO - SSM_CONV - 1, dpre.shape[1]), F32))
    acc_ref[g] += jnp.concatenate(rows, axis=0)
    dx = w_ref[3:4, :] * dpre
    for k in range(SSM_CONV - 1):
        dx = dx + w_ref[k:k + 1, :] * dp_ref[pl.ds(3 - k, Q), :]
    return dx


def _v1_ssd_bwd(dys, y, xbc, z, dt_raw, dt_rawT, states, conv_w, conv_b, dt_bias, a_log, d_skip, norm_g, *, batch, name):
    T = xbc.shape[0]
    Q, GW, N, P, HPG, G, H = SSM_CHUNK, SSM_GW, SSM_STATE, SSM_HEAD_DIM, SSM_HPG, SSM_GROUPS, SSM_HEADS
    nc = T // batch // Q
    sp = _ssd_specs(nc)(lambda c: nc - 1 - c)

    def body(xs_ref, bm_ref, cm_ref, xs_halo, bm_halo, cm_halo, z_ref, y_ref, dys_ref, dt_ref, dtT_ref, st_ref,
             w_xs, w_bm, w_cm, b_xs, b_bm, b_cm, dtb_ref, dtbT_ref, alog_ref, alogT_ref, dskip_ref, ng_ref,
             dz_ref, dxs_ref, dbm_ref, dcm_ref, ddt_ref, acc_xs, acc_bm, acc_cm, acc_head,
             dstate, acsT_s, dacsT_s, yoff_s, dxw_s, dx_s, xp_xs, xp_bm, xp_cm, dp_xs, dp_bm, dp_cm,
             cy_xs, cy_bm, cy_cm):
        b = pl.program_id(0)
        cr = pl.program_id(1)
        g = pl.program_id(2)
        c = nc - 1 - cr
        first = c == 0
        last = cr == 0

        @pl.when(jnp.logical_and(jnp.logical_and(b == 0, cr == 0), g == 0))
        def _():
            acc_xs[...] = jnp.zeros_like(acc_xs)
            acc_bm[...] = jnp.zeros_like(acc_bm)
            acc_cm[...] = jnp.zeros_like(acc_cm)
            acc_head[...] = jnp.zeros_like(acc_head)

        cc = _ssd_chunk_common(first, g, xs_ref, bm_ref, cm_ref, xs_halo, bm_halo, cm_halo, w_xs, w_bm, w_cm,
                               b_xs, b_bm, b_cm, dt_ref, dtT_ref, dtb_ref, dtbT_ref, alog_ref, alogT_ref,
                               xp_xs, xp_bm, xp_cm)
        xs, acs, dt, a = cc["xs"], cc["acs"], cc["dt"], cc["a"]
        acsT_s[...] = cc["acsT"]
        dacsT_s[...] = jnp.zeros_like(dacsT_s)
        e64 = _head_expand(g, P)
        e128 = _head_expand(g, Q)
        acs_x = _dot_hi(acs, e64)
        acs_b = _dot_hi(acs, e128)
        dt_x = _dot_hi(dt, e64)
        x = xs * dt_x
        w_x = jnp.exp(acs_x[Q - 1:Q, :] - acs_x)
        ex = jnp.exp(acs_x)

        yv = y_ref[...]
        zz = z_ref[...]
        sz = _sigmoid(zz)
        silu_z = zz * sz
        yg = yv * silu_z
        rr = lax.rsqrt(jnp.mean(yg * yg, axis=-1, keepdims=True) + RMS_EPS)
        dys_v = dys_ref[...]
        d_ng = jnp.sum(dys_v * yg * rr, axis=0, keepdims=True)
        t = dys_v * ng_ref[...]
        dyg = rr * t - yg * (rr * rr * rr * jnp.mean(t * yg, axis=-1, keepdims=True))
        dy = dyg * silu_z
        dz_ref[...] = (dyg * yv * (sz * (1.0 + zz * (1.0 - sz)))).astype(BF16)

        dexp = _dot_hi(jnp.broadcast_to(dskip_ref[...], (8, H)), e64)[0:1, :]
        d_dskip = _dot_nt(jnp.broadcast_to(jnp.sum(dy * xs, axis=0, keepdims=True), (8, GW)), e64)[0:1, :]

        dyb = dy.astype(BF16)
        xb = x.astype(BF16)
        xwb = (x * w_x).astype(BF16)
        bb = cc["bm"].astype(BF16)
        cb = cc["cm"].astype(BF16)
        s = _dot_nt(cb, bb)
        causal = _iota((Q, Q), 0) >= _iota((Q, Q), 1)
        lane_h = _iota((1, H), 1)
        ds_acc = jnp.zeros((Q, Q), F32)
        d_c = jnp.zeros((Q, N), F32)
        d_b = jnp.zeros((Q, N), F32)
        dacs = jnp.zeros((Q, H), F32)
        last_terms = jnp.zeros((1, H), F32)
        for r in range(HPG):
            hd = HPG * g + r
            cols = slice(P * r, P * (r + 1))

            @pl.when(last)
            def _():
                dstate[hd] = jnp.zeros((P, N), F32)

            seg = acs_b[:, Q * r:Q * (r + 1)] - acsT_s[pl.ds(hd, 1), :]
            l = jnp.exp(jnp.where(causal, seg, -1e30))
            m = s * l
            mb = m.astype(BF16)
            dyh = dyb[:, cols]
            hp = st_ref[0, r]
            hpb = hp.astype(BF16)
            dh = dstate[hd]
            dhb = dh.astype(BF16)
            yoff_s[:, cols] = _dot_nt(cb, hpb) * ex[:, cols]
            dye = (dy[:, cols] * ex[:, cols]).astype(BF16)
            d_c = d_c + _dot(dye, hpb)
            dhp_off = _dot_tn(dye, cb)
            dm = _dot_nt(dyh, xb[:, cols])
            dx_s[:, cols] = _dot_tn(mb, dyh)
            gmat = dm * m
            onehot = (lane_h == hd).astype(F32)
            dacs = dacs + jnp.sum(gmat, axis=-1, keepdims=True) * onehot
            dacsT_s[pl.ds(hd, 1), :] = -jnp.sum(gmat, axis=0, keepdims=True)
            ds_acc = ds_acc + dm * l
            dxw_s[:, cols] = _dot_nt(bb, dhb)
            d_b = d_b + _dot(xwb[:, cols], dhb)
            decay = jnp.exp(acsT_s[pl.ds(hd, 1), pl.ds(Q - 1, 1)])
            ddecay = jnp.sum(jnp.sum(dh * hp, axis=-1, keepdims=True), axis=0, keepdims=True)
            last_terms = last_terms + (ddecay * decay) * onehot
            dstate[hd] = dh * decay + dhp_off
        dsb = ds_acc.astype(BF16)
        d_c = d_c + _dot(dsb, bb)
        d_b = d_b + _dot_tn(dsb, cb)
        dxw = dxw_s[...]
        dx_full = dx_s[...] + dxw * w_x
        tw = _dot_nt(dxw * x * w_x, e64)
        dacs = dacs + _dot_nt(dy * yoff_s[...], e64) - tw
        last_terms = last_terms + jnp.sum(tw, axis=0, keepdims=True)
        eye = (_iota((Q, Q), 0) == _iota((Q, Q), 1)).astype(F32)
        dacs = dacs + lax.dot_general(eye, dacsT_s[...], (((1,), (1,)), ((), ())), preferred_element_type=F32,
                                      precision=lax.Precision.HIGHEST)
        dacs = dacs + jnp.where(_iota((Q, 1), 0) == Q - 1, 1.0, 0.0) * last_terms
        d_dta = _dot_hi(cc["triT"], dacs)
        ddt = d_dta * a + _dot_nt(dx_full * xs, e64)
        d_alog = jnp.sum(d_dta * dt, axis=0, keepdims=True) * a
        ddt_raw = ddt * _sigmoid(cc["dtr"])
        d_dtb = jnp.sum(ddt_raw, axis=0, keepdims=True)

        @pl.when(g == 0)
        def _():
            ddt_ref[...] = ddt_raw

        @pl.when(g > 0)
        def _():
            ddt_ref[...] += ddt_raw

        acc_head[...] += jnp.concatenate([d_dtb, d_alog, d_dskip, jnp.zeros((5, H), F32)], axis=0)
        dxs = dexp * dy + dx_full * dt_x
        dxs_ref[...] = _conv_bwd(dxs, cc["pre_xs"], xp_xs, w_xs, cy_xs, acc_xs, dp_xs, g, last).astype(BF16)
        dbm_ref[...] = _conv_bwd(d_b, cc["pre_bm"], xp_bm, w_bm, cy_bm, acc_bm, dp_bm, g, last).astype(BF16)
        dcm_ref[...] = _conv_bwd(d_c, cc["pre_cm"], xp_cm, w_cm, cy_cm, acc_cm, dp_cm, g, last).astype(BF16)
        acc_xs[g, pl.ds(SSM_CONV + 1, 1), :] += d_ng

    col = lambda v: v.reshape(H, 1)
    sds = jax.ShapeDtypeStruct
    row = lambda b, c, g: b * nc + (nc - 1 - c)
    full = lambda shape: pl.BlockSpec(shape, lambda b, c, g: (0,) * len(shape))
    return pl.pallas_call(
        body, grid=(batch, nc, G),
        in_specs=[sp["xs"], sp["bm"], sp["cm"], sp["xs_halo"], sp["bm_halo"], sp["cm_halo"], sp["grp"], sp["grp"],
                  sp["grp"], sp["dt"], sp["dtT"], sp["state"],
                  sp["w_xs"], sp["w_bm"], sp["w_cm"], sp["b_xs"], sp["b_bm"], sp["b_cm"],
                  sp["row32"], sp["col32"], sp["row32"], sp["col32"], sp["row32"], sp["vec_g"]],
        out_specs=[sp["grp"], sp["grp"],
                   pl.BlockSpec((Q, N), lambda b, c, g: (row(b, c, g), g)),
                   pl.BlockSpec((Q, N), lambda b, c, g: (row(b, c, g), g)),
                   sp["dt"], full((G, HALO, GW)), full((G, HALO, N)), full((G, HALO, N)), full((8, H))],
        out_shape=[sds((T, SSM_D_INNER), BF16), sds((T, SSM_D_INNER), BF16), sds((T, G * N), BF16),
                   sds((T, G * N), BF16), sds((T, H), F32),
                   sds((G, HALO, GW), F32), sds((G, HALO, N), F32), sds((G, HALO, N), F32), sds((8, H), F32)],
        scratch_shapes=[pltpu.VMEM((H, P, N), F32), pltpu.VMEM((H, Q), F32), pltpu.VMEM((H, Q), F32),
                        pltpu.VMEM((Q, GW), F32), pltpu.VMEM((Q, GW), F32), pltpu.VMEM((Q, GW), F32),
                        pltpu.VMEM((HALO + Q, GW), F32), pltpu.VMEM((HALO + Q, N), F32), pltpu.VMEM((HALO + Q, N), F32),
                        pltpu.VMEM((Q + HALO, GW), F32), pltpu.VMEM((Q + HALO, N), F32), pltpu.VMEM((Q + HALO, N), F32),
                        pltpu.VMEM((G, HALO, GW), F32), pltpu.VMEM((G, HALO, N), F32), pltpu.VMEM((G, HALO, N), F32)],
        compiler_params=_cparams("arbitrary", "arbitrary", "arbitrary"), name=name,
    )(xbc, xbc, xbc, xbc, xbc, xbc, z, y, dys, dt_raw, dt_rawT, states, conv_w, conv_w, conv_w, conv_b, conv_b, conv_b,
      dt_bias, col(dt_bias), a_log, col(a_log), d_skip, norm_g)


def mix_in_bwd(dh, h, g, dgates, dz, dxs, dbm, dcm, ddtT, dq, dk, dv, w_gz, w_xbc, w_dtT, w_qkv, *, name, tm=512,
               ride=None):
    T, D = h.shape
    nt = T // tm
    GN = SSM_GROUPS * SSM_STATE

    def body(dh_ref, h_ref, g_ref, dgates_ref, dz_ref, dxs_ref, dbm_ref, dcm_ref, ddt_ref, dq_ref, dk_ref, dv_ref,
             wgz_ref, wxbc_ref, wdt_ref, wqkv_ref, dhin_ref, dg_ref):
        @pl.when(pl.program_id(0) == 0)
        def _():
            dg_ref[...] = jnp.zeros_like(dg_ref)

        du = _dot_nt(dgates_ref[...], wgz_ref[:, 0:2048])
        du = du + _dot_nt(dz_ref[...], wgz_ref[:, 2048:4096])
        du = du + _dot_nt(dxs_ref[...], wxbc_ref[:, 0:SSM_D_INNER])
        du = du + _dot_nt(dbm_ref[...], wxbc_ref[:, SSM_D_INNER:SSM_D_INNER + GN])
        du = du + _dot_nt(dcm_ref[...], wxbc_ref[:, SSM_D_INNER + GN:])
        du = du + _dot_tn(ddt_ref[...].astype(BF16), wdt_ref[...])
        du = du + _dot_nt(dq_ref[...], wqkv_ref[:, 0:ATTN_Q_DIM])
        du = du + _dot_nt(dk_ref[...], wqkv_ref[:, ATTN_Q_DIM:ATTN_Q_DIM + ATTN_KV_DIM])
        du = du + _dot_nt(dv_ref[...], wqkv_ref[:, ATTN_Q_DIM + ATTN_KV_DIM:])
        hh = h_ref[...]
        r = lax.rsqrt(jnp.mean(hh * hh, axis=-1, keepdims=True) + RMS_EPS)
        dg_ref[...] += jnp.sum(du * hh * r, axis=0, keepdims=True)
        t = du * g_ref[...]
        dhin_ref[...] = dh_ref[...] + r * t - hh * (r * r * r * jnp.mean(t * hh, axis=-1, keepdims=True))

    sds = jax.ShapeDtypeStruct
    return _call(
        body, grid=(nt,),
        in_specs=[_rows(tm, D), _rows(tm, D), _resident((1, D)), _rows(tm, 2048), _rows(tm, 2048), _rows(tm, SSM_D_INNER),
                  _rows(tm, GN), _rows(tm, GN), pl.BlockSpec((SSM_HEADS, tm), lambda i: (0, i)),
                  _rows(tm, ATTN_Q_DIM), _rows(tm, ATTN_KV_DIM),
                  _rows(tm, ATTN_KV_DIM), _resident(w_gz.shape), _resident(w_xbc.shape), _resident(w_dtT.shape),
                  _resident(w_qkv.shape)],
        args=[dh, h, g, dgates, dz, dxs, dbm, dcm, ddtT, dq, dk, dv, w_gz, w_xbc, w_dtT, w_qkv],
        out_specs=[_rows(tm, D), pl.BlockSpec((1, D), lambda i: (0, 0))],
        out_shape=[sds((T, D), F32), sds((1, D), F32)],
        sem=("arbitrary",), name=name, ride=ride)


PAIRS = SSM_HPG // 2
PW = 2 * SSM_HEAD_DIM


def _ssd_prologue(first, xs_ref, bm_ref, cm_ref, xs_halo, bm_halo, cm_halo, w_xs, w_bm, w_cm, b_xs, b_bm, b_cm,
                  dtT_ref, dtb_ref, alog_ref, xp_xs, xp_bm, xp_cm):
    Q = SSM_CHUNK
    pre_xs = _conv_pre(xs_ref, xs_halo, w_xs, b_xs, xp_xs, first)
    pre_bm = _conv_pre(bm_ref, bm_halo, w_bm, b_bm, xp_bm, first)
    pre_cm = _conv_pre(cm_ref, cm_halo, w_cm, b_cm, xp_cm, first)
    dtrT = dtT_ref[...] + dtb_ref[...]
    dtT = _softplus(dtrT)
    aT = -jnp.exp(alog_ref[...])
    triT = (_iota((Q, Q), 0) <= _iota((Q, Q), 1)).astype(F32)
    acsT = _dot_hi(dtT * aT, triT)
    lastT = acsT[:, Q - 1:Q]
    wT = jnp.exp(lastT - acsT)
    eT = jnp.exp(acsT)
    cols = jnp.concatenate([dtT, acsT, wT, eT], axis=0).T
    return dict(pre_xs=pre_xs, pre_bm=pre_bm, pre_cm=pre_cm, xs=pre_xs * _sigmoid(pre_xs), bm=pre_bm * _sigmoid(pre_bm),
                cm=pre_cm * _sigmoid(pre_cm), dtrT=dtrT, dtT=dtT, aT=aT, acsT=acsT, decayT=jnp.exp(lastT), cols=cols)


def _pair_cols(cols, base, p, lo):
    k = base + 2 * p
    return jnp.where(lo, cols[:, k:k + 1], cols[:, k + 1:k + 2])


def _pair_row(colT, p, lo_row):
    return jnp.where(lo_row, colT[2 * p:2 * p + 1, :], colT[2 * p + 1:2 * p + 2, :])


def _pair_operands(pp, p, s, causal, lo, xb):
    zero = jnp.zeros_like(xb)
    rhs = jnp.concatenate([jnp.where(lo, xb, zero), jnp.where(lo, zero, xb)], axis=0)
    ls, ms = [], []
    for k in (2 * p, 2 * p + 1):
        seg = pp["cols"][:, 8 + k:9 + k] - pp["acsT"][k:k + 1, :]
        l = jnp.exp(jnp.where(causal, seg, -1e30))
        ls.append(l)
        ms.append(s * l)
    lhs = jnp.concatenate([m.astype(BF16) for m in ms], axis=1)
    return lhs, rhs, ls


def ssd_fwd(xbc, z, dt_rawT, conv_w, conv_b, dt_bias, a_log, d_skip_x, norm_g, *, batch, name):
    T = xbc.shape[0]
    Q, GW, N = SSM_CHUNK, SSM_GW, SSM_STATE
    nc = T // batch // Q
    sp = _ssd_specs(nc)(lambda c: c)

    def body(xs_ref, bm_ref, cm_ref, xs_halo, bm_halo, cm_halo, z_ref, dtT_ref,
             w_xs, w_bm, w_cm, b_xs, b_bm, b_cm, dtb_ref, alog_ref, dsk_ref, ng_ref,
             y_ref, ys_ref, st_ref, state, xp_xs, xp_bm, xp_cm):
        c = pl.program_id(1)
        g = pl.program_id(2)
        first = c == 0
        pp = _ssd_prologue(first, xs_ref, bm_ref, cm_ref, xs_halo, bm_halo, cm_halo, w_xs, w_bm, w_cm,
                           b_xs, b_bm, b_cm, dtT_ref, dtb_ref, alog_ref, xp_xs, xp_bm, xp_cm)
        xs = pp["xs"]
        bb = pp["bm"].astype(BF16)
        cb = pp["cm"].astype(BF16)
        s = _dot_nt(cb, bb)
        causal = _iota((Q, Q), 0) >= _iota((Q, Q), 1)
        lo = _iota((Q, PW), 1) < SSM_HEAD_DIM
        lo_row = _iota((1, PW), 1) < SSM_HEAD_DIM
        ys = []

        @pl.when(first)
        def _():
            state[g] = jnp.zeros((PAIRS, N, PW), F32)

        for p in range(PAIRS):
            tile = slice(PW * p, PW * (p + 1))
            xs_p = xs[:, tile]
            x_p = xs_p * _pair_cols(pp["cols"], 0, p, lo)
            lhs, rhs, _ = _pair_operands(pp, p, s, causal, lo, x_p.astype(BF16))
            hp = state[g, p]
            st_ref[0, p] = hp
            ys.append(_dot(lhs, rhs) + _dot(cb, hp.astype(BF16)) * _pair_cols(pp["cols"], 24, p, lo)
                      + dsk_ref[:, tile] * xs_p)
            xw = (x_p * _pair_cols(pp["cols"], 16, p, lo)).astype(BF16)
            state[g, p] = hp * _pair_row(pp["decayT"], p, lo_row) + _dot_tn(bb, xw)
        y = jnp.concatenate(ys, axis=1)
        y_ref[...] = y
        zz = z_ref[...]
        yg = y * (zz * _sigmoid(zz))
        rr = lax.rsqrt(jnp.mean(yg * yg, axis=-1, keepdims=True) + RMS_EPS)
        ys_ref[...] = (yg * rr * ng_ref[...]).astype(BF16)

    sds = jax.ShapeDtypeStruct
    return pl.pallas_call(
        body, grid=(batch, nc, SSM_GROUPS),
        in_specs=[sp["xs"], sp["bm"], sp["cm"], sp["xs_halo"], sp["bm_halo"], sp["cm_halo"], sp["grp"], sp["dtT_g"],
                  sp["w_xs"], sp["w_bm"], sp["w_cm"], sp["b_xs"], sp["b_bm"], sp["b_cm"],
                  sp["col_g"], sp["col_g"], sp["vec_g"], sp["vec_g"]],
        out_specs=[sp["grp"], sp["grp"], sp["pairs"]],
        out_shape=[sds((T, SSM_D_INNER), F32), sds((T, SSM_D_INNER), BF16),
                   sds((T // Q, SSM_GROUPS * PAIRS, N, PW), F32)],
        scratch_shapes=[pltpu.VMEM((SSM_GROUPS, PAIRS, N, PW), F32),
                        pltpu.VMEM((HALO + Q, GW), F32), pltpu.VMEM((HALO + Q, N), F32), pltpu.VMEM((HALO + Q, N), F32)],
        compiler_params=_cparams("arbitrary", "arbitrary", "arbitrary"), name=name,
    )(xbc, xbc, xbc, xbc, xbc, xbc, z, dt_rawT, conv_w, conv_w, conv_w, conv_b, conv_b, conv_b,
      dt_bias, a_log, d_skip_x, norm_g)


def ssd_bwd(dys, y, xbc, z, dt_rawT, states, conv_w, conv_b, dt_bias, a_log, d_skip_x, norm_g, *, batch, name,
            ride=None):
    T = xbc.shape[0]
    Q, GW, N, G = SSM_CHUNK, SSM_GW, SSM_STATE, SSM_GROUPS
    nc = T // batch // Q
    sp = _ssd_specs(nc)(lambda c: nc - 1 - c)

    def body(xs_ref, bm_ref, cm_ref, xs_halo, bm_halo, cm_halo, z_ref, y_ref, dys_ref, dtT_ref, st_ref,
             w_xs, w_bm, w_cm, b_xs, b_bm, b_cm, dtb_ref, alog_ref, dsk_ref, ng_ref,
             dz_ref, dxs_ref, dbm_ref, dcm_ref, ddtT_ref, acc_xs, acc_bm, acc_cm, acc_head,
             dstate, xp_xs, xp_bm, xp_cm, dp_xs, dp_bm, dp_cm, cy_xs, cy_bm, cy_cm):
        b = pl.program_id(0)
        cr = pl.program_id(1)
        g = pl.program_id(2)
        first = cr == nc - 1
        last = cr == 0

        @pl.when(jnp.logical_and(jnp.logical_and(b == 0, cr == 0), g == 0))
        def _():
            acc_xs[...] = jnp.zeros_like(acc_xs)
            acc_bm[...] = jnp.zeros_like(acc_bm)
            acc_cm[...] = jnp.zeros_like(acc_cm)
            acc_head[...] = jnp.zeros_like(acc_head)

        @pl.when(last)
        def _():
            dstate[g] = jnp.zeros((PAIRS, N, PW), F32)
            cy_xs[g] = jnp.zeros((HALO, GW), F32)
            cy_bm[g] = jnp.zeros((HALO, N), F32)
            cy_cm[g] = jnp.zeros((HALO, N), F32)

        pp = _ssd_prologue(first, xs_ref, bm_ref, cm_ref, xs_halo, bm_halo, cm_halo, w_xs, w_bm, w_cm,
                           b_xs, b_bm, b_cm, dtT_ref, dtb_ref, alog_ref, xp_xs, xp_bm, xp_cm)
        xs, dtT, aT, decayT = pp["xs"], pp["dtT"], pp["aT"], pp["decayT"]

        yv = y_ref[...]
        zz = z_ref[...]
        sz = _sigmoid(zz)
        silu_z = zz * sz
        yg = yv * silu_z
        rr = lax.rsqrt(jnp.mean(yg * yg, axis=-1, keepdims=True) + RMS_EPS)
        dys_v = dys_ref[...]
        d_ng = jnp.sum(dys_v * yg * rr, axis=0, keepdims=True)
        t = dys_v * ng_ref[...]
        dyg = rr * t - yg * (rr * rr * rr * jnp.mean(t * yg, axis=-1, keepdims=True))
        dy = dyg * silu_z
        dz_ref[...] = (dyg * yv * (sz * (1.0 + zz * (1.0 - sz)))).astype(BF16)
        dsk = dsk_ref[...]
        d_dsk = jnp.sum(dy * xs, axis=0, keepdims=True)

        bb = pp["bm"].astype(BF16)
        cb = pp["cm"].astype(BF16)
        s = _dot_nt(cb, bb)
        causal = _iota((Q, Q), 0) >= _iota((Q, Q), 1)
        lo = _iota((Q, PW), 1) < SSM_HEAD_DIM
        lo_row = _iota((1, PW), 1) < SSM_HEAD_DIM
        sub8 = _iota((SSM_HPG, 1), 0)
        ds_acc = jnp.zeros((Q, Q), F32)
        d_c = jnp.zeros((Q, N), F32)
        d_b = jnp.zeros((Q, N), F32)
        last_terms = jnp.zeros((SSM_HPG, 1), F32)
        q1, q2, dxs = [], [], []
        for p in range(PAIRS):
            tile = slice(PW * p, PW * (p + 1))
            dt_p = _pair_cols(pp["cols"], 0, p, lo)
            w_p = _pair_cols(pp["cols"], 16, p, lo)
            e_p = _pair_cols(pp["cols"], 24, p, lo)
            xs_p = xs[:, tile]
            x_p = xs_p * dt_p
            xw_p = x_p * w_p
            lhs, rhs, ls = _pair_operands(pp, p, s, causal, lo, x_p.astype(BF16))
            dy_p = dy[:, tile]
            dyb = dy_p.astype(BF16)
            hp = st_ref[0, p]
            hpb = hp.astype(BF16)
            dh = dstate[g, p]
            dhb = dh.astype(BF16)
            dye = (dy_p * e_p).astype(BF16)
            d_c = d_c + _dot_nt(dye, hpb)
            dm = _dot_nt(dyb, rhs)
            dxd2 = _dot_tn(lhs, dyb)
            dxd = jnp.where(lo, dxd2[0:Q], dxd2[Q:2 * Q])
            ds_acc = ds_acc + dm[:, 0:Q] * ls[0] + dm[:, Q:2 * Q] * ls[1]
            dxw = _dot(bb, dhb)
            d_b = d_b + _dot_nt(xw_p.astype(BF16), dhb)
            dx_full = dxd + dxw * w_p
            tw = dxw * xw_p
            yd = _dot(lhs, rhs)
            yoff = _dot(cb, hpb) * e_p
            q1.append(dyb.astype(F32) * yd + dy_p * yoff - tw - x_p.astype(BF16).astype(F32) * dxd)
            q2.append(dx_full * xs_p)
            dxs.append(dsk[:, tile] * dy_p + dx_full * dt_p)
            row = jnp.sum(dh * hp, axis=0, keepdims=True) * _pair_row(decayT, p, lo_row) + jnp.sum(tw, axis=0, keepdims=True)
            t_lo = jnp.sum(jnp.where(lo_row, row, 0.0), axis=1, keepdims=True)
            t_hi = jnp.sum(jnp.where(lo_row, 0.0, row), axis=1, keepdims=True)
            last_terms = last_terms + jnp.where(sub8 == 2 * p, t_lo, 0.0) + jnp.where(sub8 == 2 * p + 1, t_hi, 0.0)
            dstate[g, p] = dh * _pair_row(decayT, p, lo_row) + _dot_tn(cb, dye)
        dsb = ds_acc.astype(BF16)
        d_c = d_c + _dot(dsb, bb)
        d_b = d_b + _dot_tn(dsb, cb)
        e8 = (_iota((SSM_HPG, GW), 0) == lax.shift_right_logical(_iota((SSM_HPG, GW), 1), 6)).astype(F32)
        seg_sum = lambda tiles: lax.dot_general(e8, jnp.concatenate(tiles, axis=1), (((1,), (1,)), ((), ())),
                                                preferred_element_type=F32, precision=lax.Precision.HIGHEST)
        dacsT = seg_sum(q1) + jnp.where(_iota((1, Q), 1) == Q - 1, 1.0, 0.0) * last_terms
        tri = (_iota((Q, Q), 0) >= _iota((Q, Q), 1)).astype(F32)
        d_dtaT = _dot_hi(dacsT, tri)
        ddtT = d_dtaT * aT + seg_sum(q2)
        d_alog = jnp.sum(d_dtaT * dtT, axis=1, keepdims=True) * aT
        ddt_rawT = ddtT * _sigmoid(pp["dtrT"])
        ddtT_ref[...] = ddt_rawT
        d_dtb = jnp.sum(ddt_rawT, axis=1, keepdims=True)
        lane = _iota((SSM_HPG, N), 1)
        acc_head[g] += jnp.where(lane == 0, d_dtb, 0.0) + jnp.where(lane == 1, d_alog, 0.0)
        dxs_v = jnp.concatenate(dxs, axis=1)
        dxs_ref[...] = _conv_bwd(dxs_v, pp["pre_xs"], xp_xs, w_xs, cy_xs, acc_xs, dp_xs, g, last).astype(BF16)
        dbm_ref[...] = _conv_bwd(d_b, pp["pre_bm"], xp_bm, w_bm, cy_bm, acc_bm, dp_bm, g, last).astype(BF16)
        dcm_ref[...] = _conv_bwd(d_c, pp["pre_cm"], xp_cm, w_cm, cy_cm, acc_cm, dp_cm, g, last).astype(BF16)
        acc_xs[g, pl.ds(SSM_CONV + 1, 2), :] += jnp.concatenate([d_ng, d_dsk], axis=0)

    sds = jax.ShapeDtypeStruct
    row = lambda b, c, g: b * nc + (nc - 1 - c)
    full = lambda shape: pl.BlockSpec(shape, lambda b, c, g: (0,) * len(shape))
    return _call(
        body, grid=(batch, nc, G),
        in_specs=[sp["xs"], sp["bm"], sp["cm"], sp["xs_halo"], sp["bm_halo"], sp["cm_halo"], sp["grp"], sp["grp"],
                  sp["grp"], sp["dtT_g"], sp["pairs"],
                  sp["w_xs"], sp["w_bm"], sp["w_cm"], sp["b_xs"], sp["b_bm"], sp["b_cm"],
                  sp["col_g"], sp["col_g"], sp["vec_g"], sp["vec_g"]],
        args=[xbc, xbc, xbc, xbc, xbc, xbc, z, y, dys, dt_rawT, states, conv_w, conv_w, conv_w, conv_b, conv_b, conv_b,
              dt_bias, a_log, d_skip_x, norm_g],
        out_specs=[sp["grp"], sp["grp"],
                   pl.BlockSpec((Q, N), lambda b, c, g: (row(b, c, g), g)),
                   pl.BlockSpec((Q, N), lambda b, c, g: (row(b, c, g), g)),
                   sp["dtT_g"], full((G, HALO, GW)), full((G, HALO, N)), full((G, HALO, N)), full((G, SSM_HPG, N))],
        out_shape=[sds((T, SSM_D_INNER), BF16), sds((T, SSM_D_INNER), BF16), sds((T, G * N), BF16),
                   sds((T, G * N), BF16), sds((SSM_HEADS, T), F32),
                   sds((G, HALO, GW), F32), sds((G, HALO, N), F32), sds((G, HALO, N), F32), sds((G, SSM_HPG, N), F32)],
        scratch=[pltpu.VMEM((G, PAIRS, N, PW), F32),
                 pltpu.VMEM((HALO + Q, GW), F32), pltpu.VMEM((HALO + Q, N), F32), pltpu.VMEM((HALO + Q, N), F32),
                 pltpu.VMEM((Q + HALO, GW), F32), pltpu.VMEM((Q + HALO, N), F32), pltpu.VMEM((Q + HALO, N), F32),
                 pltpu.VMEM((G, HALO, GW), F32), pltpu.VMEM((G, HALO, N), F32), pltpu.VMEM((G, HALO, N), F32)],
        sem=("arbitrary", "arbitrary", "arbitrary"), name=name, ride=ride)


def mm_rows(a, b, *, name, tt=2048):
    M, T = a.shape
    N = b.shape[1]
    tt = min(tt, T)

    def body(a_ref, b_ref, o_ref):
        @pl.when(pl.program_id(0) == 0)
        def _():
            o_ref[...] = jnp.zeros_like(o_ref)

        o_ref[...] += _dot(a_ref[...].astype(BF16), b_ref[...])

    return pl.pallas_call(
        body, grid=(T // tt,),
        in_specs=[pl.BlockSpec((M, tt), lambda t: (0, t)), pl.BlockSpec((tt, N), lambda t: (t, 0))],
        out_specs=pl.BlockSpec((M, N), lambda t: (0, 0)), out_shape=jax.ShapeDtypeStruct((M, N), F32),
        compiler_params=_cparams("arbitrary"), name=name)(a, b)


MESH = pl.DeviceIdType.MESH
ANY = pl.BlockSpec(memory_space=pl.ANY)
ROW_ALIGN = 16


def _me():
    return lax.axis_index("x"), lax.axis_index("y"), lax.axis_index("c")


def _other_chips(x, y):
    return [(1 - x, y), (x, 1 - y), (1 - x, 1 - y)]


def _remote(src, dst, send_sem, recv_sem, to):
    return pltpu.make_async_remote_copy(src_ref=src, dst_ref=dst, send_sem=send_sem, recv_sem=recv_sem,
                                        device_id=to, device_id_type=MESH)


def _half(c, rows):
    return pl.ds(pl.multiple_of(c * (rows // 2), ROW_ALIGN), rows // 2)


def ag_ride(bufs):
    n = len(bufs)

    def copies(outs, sems):
        ici_send, ici_recv, d2d_send, d2d_recv = sems
        x, y, c = _me()
        sib = (x, y, 1 - c)
        ici, d2d, d2d_in = [], [], []
        for i in range(n):
            rows = outs[i].shape[1]
            mine = outs[i].at[2 * x + y, _half(c, rows)]
            for j, chip in enumerate(_other_chips(x, y)):
                ici.append(_remote(mine, mine, ici_send.at[i, j], ici_recv.at[i, j], (*chip, c)))
                landed = outs[i].at[2 * chip[0] + chip[1], _half(c, rows)]
                d2d.append((_remote(landed, landed, ici_send.at[i, j], ici_recv.at[i, j], (*chip, c)),
                            _remote(landed, landed, d2d_send.at[i, j], d2d_recv.at[i, j], sib)))
                lands = outs[i].at[2 * chip[0] + chip[1], _half(1 - c, rows)]
                d2d_in.append(_remote(lands, lands, d2d_send.at[i, j], d2d_recv.at[i, j], sib))
        return ici, d2d, d2d_in

    def start(ins, outs, sems):
        for cp in copies(outs, sems)[0]:
            cp.start()

    def finish(ins, outs, sems):
        ici, d2d, d2d_in = copies(outs, sems)
        for arrived, forward in d2d:
            arrived.wait_recv()
            forward.start()
        for cp in d2d_in:
            cp.wait_recv()
        for cp in ici + [forward for _, forward in d2d]:
            cp.wait_send()

    return Ride(bufs, [jax.ShapeDtypeStruct(b.shape, b.dtype) for b in bufs], [(i, i) for i in range(n)],
                [pltpu.SemaphoreType.DMA((n, 3))] * 4, start, finish)


def pair_ride(grads):
    n = len(grads)

    def copies(ins, outs, sems):
        x, y, c = _me()
        return [_remote(ins[i].at[:, _half(1 - c, ins[i].shape[1]), :], outs[i], sems[0].at[i], sems[1].at[i], (x, y, 1 - c))
                for i in range(n)]

    def start(ins, outs, sems):
        for cp in copies(ins, outs, sems):
            cp.start()

    def finish(ins, outs, sems):
        for cp in copies(ins, outs, sems):
            cp.wait()

    return Ride(grads, [jax.ShapeDtypeStruct((N_SHARD, g.shape[1] // 2, g.shape[2]), g.dtype) for g in grads], [],
                [pltpu.SemaphoreType.DMA((n,))] * 2, start, finish)


def rs_add(grad, part, c, *, rt, name):
    _, rows, cols = grad.shape
    r2 = rows // 2
    nrb = r2 // rt

    def body(c_ref, g_ref, p_ref, o_ref):
        o_ref[...] = (g_ref[...] + p_ref[...]).astype(BF16)

    return pl.pallas_call(
        body,
        grid_spec=pltpu.PrefetchScalarGridSpec(
            num_scalar_prefetch=1, grid=(N_SHARD, nrb),
            in_specs=[pl.BlockSpec((1, rt, cols), lambda k, i, c_ref: (k, c_ref[1] * nrb + i, 0)),
                      pl.BlockSpec((1, rt, cols), lambda k, i, c_ref: (k, i, 0))],
            out_specs=pl.BlockSpec((1, rt, cols), lambda k, i, c_ref: (k, i, 0))),
        out_shape=jax.ShapeDtypeStruct((N_SHARD, r2, cols), BF16),
        compiler_params=_cparams("parallel", "parallel"), name=name)(c, grad, part)


def chips_ride(sums):
    n = len(sums)

    def copies(ins, outs, sems):
        send, recv = sems
        x, y, c = _me()
        return [_remote(ins[i].at[2 * chip[0] + chip[1]], outs[i].at[2 * x + y], send.at[i, j], recv.at[i, j], (*chip, c))
                for i in range(n) for j, chip in enumerate(_other_chips(x, y))]

    def start(ins, outs, sems):
        for cp in copies(ins, outs, sems):
            cp.start()

    def finish(ins, outs, sems):
        for cp in copies(ins, outs, sems):
            cp.wait()

    return Ride(sums, [jax.ShapeDtypeStruct(s.shape, s.dtype) for s in sums], [],
                [pltpu.SemaphoreType.DMA((n, 3))] * 2, start, finish)


def rs_total(parts, own, where, *, rt, name):
    _, r2, cols = parts.shape
    nrb = r2 // rt

    def body(w_ref, p0, p1, p2, p3, own_ref, o_ref):
        s_me = w_ref[0]
        acc = None
        for k, p in enumerate((p0, p1, p2, p3)):
            term = jnp.where(s_me == k, own_ref[0], p[0]).astype(F32)
            acc = term if acc is None else acc + term
        o_ref[...] = acc

    def slot(k):
        return pl.BlockSpec((1, rt, cols), lambda i, w: (jnp.where(w[0] == k, (k + 1) % N_SHARD, k), i, 0))

    return pl.pallas_call(
        body,
        grid_spec=pltpu.PrefetchScalarGridSpec(
            num_scalar_prefetch=1, grid=(nrb,),
            in_specs=[slot(0), slot(1), slot(2), slot(3), pl.BlockSpec((1, rt, cols), lambda i, w: (w[0], i, 0))],
            out_specs=pl.BlockSpec((rt, cols), lambda i, w: (w[1] * nrb + i, 0))),
        out_shape=jax.ShapeDtypeStruct((2 * r2, cols), F32),
        compiler_params=_cparams("parallel"), name=name)(where, parts, parts, parts, parts, own)


def share_ride(totals):
    n = len(totals)

    def halves(outs, sems):
        x, y, c = _me()
        mine = [outs[i].at[_half(c, outs[i].shape[0])] for i in range(n)]
        other = [outs[i].at[_half(1 - c, outs[i].shape[0])] for i in range(n)]
        return ([_remote(m, m, sems[0].at[i], sems[1].at[i], (x, y, 1 - c)) for i, m in enumerate(mine)],
                [_remote(o, o, sems[0].at[i], sems[1].at[i], (x, y, 1 - c)) for i, o in enumerate(other)])

    def start(ins, outs, sems):
        for cp in halves(outs, sems)[0]:
            cp.start()

    def finish(ins, outs, sems):
        sent, landing = halves(outs, sems)
        for cp in landing:
            cp.wait_recv()
        for cp in sent:
            cp.wait_send()

    return Ride(totals, [jax.ShapeDtypeStruct(t.shape, t.dtype) for t in totals], [(i, i) for i in range(n)],
                [pltpu.SemaphoreType.DMA((n,))] * 2, start, finish)


def small_allreduce(buf, *, name):
    rows = buf.shape[0]

    def body(x_ref, o_ref, slots, send, recv):
        x, y, c = _me()
        me = 4 * x + 2 * y + c
        slots[me] = x_ref[...]
        sent = []
        for d in range(1, 8):
            peer = (1 - x if d & 4 else x, 1 - y if d & 2 else y, 1 - c if d & 1 else c)
            sent.append(_remote(x_ref, slots.at[me], send.at[d - 1], recv.at[d - 1], peer))
            sent[-1].start()
        for cp in sent:
            cp.wait()
        acc = slots[0]
        for k in range(1, 8):
            acc = acc + slots[k]
        o_ref[...] = acc

    return pl.pallas_call(
        body, out_shape=jax.ShapeDtypeStruct(buf.shape, F32),
        in_specs=[pl.BlockSpec(memory_space=pltpu.VMEM)], out_specs=pl.BlockSpec(memory_space=pltpu.VMEM),
        scratch_shapes=[pltpu.VMEM((8, rows, 128), F32), pltpu.SemaphoreType.DMA((7,)), pltpu.SemaphoreType.DMA((7,))],
        name=name)(buf)


def adamw(w, g, m, v, *, name, rt=None):
    rows, cols = w.shape
    rt = rows if rt is None else rt
    c1 = 1.0 - ADAM_B1 ** ADAM_STEP
    c2 = 1.0 - ADAM_B2 ** ADAM_STEP

    def body(w_ref, g_ref, m_ref, v_ref, d_ref, nm_ref, nv_ref):
        gg = g_ref[...]
        nm = ADAM_B1 * m_ref[...] + (1.0 - ADAM_B1) * gg
        nv = ADAM_B2 * v_ref[...] + (1.0 - ADAM_B2) * (gg * gg)
        nm_ref[...] = nm
        nv_ref[...] = nv
        d_ref[...] = -ADAM_LR * ((nm / c1) / (jnp.sqrt(nv / c2) + ADAM_EPS) + ADAM_WD * w_ref[...])

    spec = pl.BlockSpec((rt, cols), lambda i: (i, 0))
    return pl.pallas_call(
        body, grid=(rows // rt,), in_specs=[spec] * 4, out_specs=[spec] * 3,
        out_shape=[jax.ShapeDtypeStruct((rows, cols), F32)] * 3,
        compiler_params=_cparams("parallel"), name=name)(w, g, m, v)


WEIGHTS = ['ffn1_pre_g', 'ffn1_w_gate', 'ffn1_w_up', 'ffn1_w_down', 'ffn1_post_g', 'mix_pre_g', 'w_in', 'conv_w',
           'conv_b', 'dt_bias', 'a_log', 'd_skip', 'ssm_norm_g', 'w_ssm_proj', 'attn_sinks', 'rel_bias_table',
           'w_attn_proj', 'w_out', 'mix_post_g', 'ffn2_pre_g', 'ffn2_w_gate', 'ffn2_w_up', 'ffn2_w_down', 'ffn2_post_g']
BIG = ['ffn1_w_gate', 'ffn1_w_up', 'ffn1_w_down', 'w_in', 'w_ssm_proj', 'w_attn_proj', 'w_out',
       'ffn2_w_gate', 'ffn2_w_up', 'ffn2_w_down']
SMALL = [w for w in WEIGHTS if w not in BIG]


def _bucket_onehot():
    blk = ATTN_BLOCK
    dist = np.maximum(np.arange(blk)[:, None] + blk - np.arange(2 * blk)[None, :], 0)
    max_exact = REL_BUCKETS // 2
    d = np.maximum(dist, 1).astype(np.float32)
    large = max_exact + (np.log(d / np.float32(max_exact)) / np.float32(math.log(REL_MAX_DISTANCE / max_exact))
                         * np.float32(REL_BUCKETS - max_exact)).astype(np.int32)
    bucket = np.where(dist < max_exact, dist, np.minimum(large, REL_BUCKETS - 1)).reshape(-1)
    return jnp.asarray((bucket[None, :] == np.arange(REL_BUCKETS)[:, None]).astype(np.float32))


def _pack_rows(parts, mult=8):
    flat = jnp.concatenate([p.reshape(-1).astype(F32) for p in parts])
    rows = -(-flat.shape[0] // (128 * mult)) * mult
    return jnp.pad(flat, (0, rows * 128 - flat.shape[0])).reshape(rows, 128)


def _unpack_rows(buf, shapes):
    flat = buf.reshape(-1)
    out, at = [], 0
    for shp in shapes:
        size = int(np.prod(shp))
        out.append(flat[at:at + size].reshape(shp))
        at += size
    return out


def kernel(x, ffn1_pre_g, ffn1_w_gate, ffn1_w_up, ffn1_w_down, ffn1_post_g, mix_pre_g, w_in, conv_w, conv_b, dt_bias, a_log, d_skip, ssm_norm_g, w_ssm_proj, attn_sinks, rel_bias_table, w_attn_proj, w_out, mix_post_g, ffn2_pre_g, ffn2_w_gate, ffn2_w_up, ffn2_w_down, ffn2_post_g, loss_target, m_ffn1_pre_g, m_ffn1_w_gate, m_ffn1_w_up, m_ffn1_w_down, m_ffn1_post_g, m_mix_pre_g, m_w_in, m_conv_w, m_conv_b, m_dt_bias, m_a_log, m_d_skip, m_ssm_norm_g, m_w_ssm_proj, m_attn_sinks, m_rel_bias_table, m_w_attn_proj, m_w_out, m_mix_post_g, m_ffn2_pre_g, m_ffn2_w_gate, m_ffn2_w_up, m_ffn2_w_down, m_ffn2_post_g, v_ffn1_pre_g, v_ffn1_w_gate, v_ffn1_w_up, v_ffn1_w_down, v_ffn1_post_g, v_mix_pre_g, v_w_in, v_conv_w, v_conv_b, v_dt_bias, v_a_log, v_d_skip, v_ssm_norm_g, v_w_ssm_proj, v_attn_sinks, v_rel_bias_table, v_w_attn_proj, v_w_out, v_mix_post_g, v_ffn2_pre_g, v_ffn2_w_gate, v_ffn2_w_up, v_ffn2_w_down, v_ffn2_post_g):
    args = locals()
    w = {n: args[n] for n in WEIGHTS}
    m = {n: args["m_" + n] for n in WEIGHTS}
    v = {n: args["v_" + n] for n in WEIGHTS}
    batch, seq, D = x.shape
    T = batch * seq
    xi, yi, ci = _me()
    s_me = 2 * xi + yi
    x2 = x.reshape(T, D)
    tgt = loss_target.reshape(T, D)

    def own_slot(parts):
        p = jnp.concatenate([t[0] for t in parts], axis=0).astype(BF16)
        return lax.dynamic_update_slice(lax.empty((N_SHARD,) + p.shape, BF16), p[None], (s_me, 0, 0))

    tr = lambda a: jnp.swapaxes(a, -1, -2)
    (wffn1,) = run_ride(ag_ride([own_slot([tr(ffn1_w_gate), tr(ffn1_w_up), ffn1_w_down])]), name="ag_ffn1")
    col = lambda v: v.reshape(SSM_HEADS, 1)
    d_skip_x = jnp.repeat(d_skip, SSM_HEAD_DIM, axis=1)
    cw_slot = lax.dynamic_update_slice(jnp.zeros((SSM_CONV, SSM_CONV_DIM), F32),
                                       conv_w[0] * (ci == 0).astype(F32), (0, s_me * (SSM_CONV_DIM // N_SHARD)))
    conv_w_full = small_allreduce(cw_slot.reshape(-1, 128), name="ag_conv_w").reshape(SSM_CONV, SSM_CONV_DIM)

    (h1, n1, gate1, up1, f1), (gin, gmix) = ffn_fwd(
        x2, ffn1_pre_g, wffn1, ffn1_post_g, name="ffn1_fwd",
        ride=ag_ride([own_slot([w_in]), own_slot([w_ssm_proj, w_attn_proj, w_out])]))
    w_in_full = gin.transpose(1, 0, 2).reshape(D, IN_COLS)
    w_gz = w_in_full[:, 0:4096]
    w_xbc = w_in_full[:, 4096:4096 + SSM_CONV_DIM]
    w_dtT = w_in_full[:, 7168:7200].T
    w_qkv = w_in_full[:, 7200:]
    (u, gates, z, xbc, dt_rawT, q, k, vv), (wffn2,) = mix_in_fwd(
        h1, mix_pre_g, w_gz, w_xbc, w_dtT, w_qkv, name="mix_in_fwd",
        ride=ag_ride([own_slot([tr(ffn2_w_gate), tr(ffn2_w_up), ffn2_w_down])]))
    y, ys, states = ssd_fwd(xbc, z, dt_rawT, conv_w_full, conv_b, col(dt_bias), col(a_log), d_skip_x, ssm_norm_g,
                            batch=batch, name="ssd_fwd")
    onehot = _bucket_onehot()
    bias = attn_bias(rel_bias_table.T, onehot, name="attn_bias").reshape(ATTN_Q_HEADS, ATTN_BLOCK, 2 * ATTN_BLOCK)
    o, lse = attn_fwd(q, k, vv, bias, attn_sinks, batch=batch, name="attn_fwd")
    h2, y_ssm, y_attn, mix, merged = mix_out_fwd(ys, o, gates, h1, gmix, mix_post_g, name="mix_out_fwd")
    h3, n3, gate2, up2, f2, dy, loss_parts = ffn_fwd(h2, ffn2_pre_g, wffn2, ffn2_post_g, tgt, name="ffn2_fwd")

    where = jnp.stack([s_me, ci]).astype(jnp.int32)

    def chip_sums(grads, pair, tiles, tag):
        return [rs_add(g, p, where, rt=rt, name=f"rs_add_{tag}{i}") for i, (g, p, rt) in enumerate(zip(grads, pair, tiles))]

    def totals(parts, sums, tiles, tag):
        return [rs_total(p, s, where, rt=rt, name=f"rs_total_{tag}{i}")
                for i, (p, s, rt) in enumerate(zip(parts, sums, tiles))]

    def ffn_grads(n, dgate, dup, a, df, tag):
        d = mm_tn(dgate, n[None], into=(lax.empty(wffn1.shape, F32), 0), name="dw_gate" + tag)
        d = mm_tn(dup, n[None], into=(d, 1), name="dw_up" + tag)
        return [mm_tn(a, df[None], into=(d, 2), name="dw_down" + tag)]

    ffn_tiles, mix_tiles = [352], [256, 256]
    dh2, df2, a2, dgate2, dup2, dg_ffn2_pre, dg_ffn2_post = ffn_bwd(dy, h2, f2, gate2, up2, ffn2_pre_g, ffn2_post_g,
                                                                    wffn2, name="ffn2_bwd")
    d_f2 = ffn_grads(n3, dgate2, dup2, a2, df2, "2")
    (dmix, dyssm, dyattn, dgates, dys, do, dg_mix_post), pair_f2 = mix_out_bwd(
        dh2, mix, y_ssm, y_attn, gates, gmix, mix_post_g, name="mix_out_bwd", ride=pair_ride(d_f2))
    sums_f2 = chip_sums(d_f2, pair_f2, ffn_tiles, "f2")
    dq, dk, dv, dbias, dsinks = attn_bwd(q, k, vv, o, do, lse, bias, attn_sinks, batch=batch, name="attn_bwd")
    dtable = attn_bias_bwd(dbias.reshape(ATTN_Q_HEADS, -1), onehot, name="attn_bias_bwd").T
    (dz, dxs, dbm, dcm, ddtT, acc_xs, acc_bm, acc_cm, acc_head), parts_f2 = ssd_bwd(
        dys, y, xbc, z, dt_rawT, states, conv_w_full, conv_b, col(dt_bias), col(a_log), d_skip_x, ssm_norm_g,
        batch=batch, name="ssd_bwd", ride=chips_ride(sums_f2))
    tot_f2 = totals(parts_f2, sums_f2, ffn_tiles, "f2")
    dmx = mm_tn(ys[None], dyssm[None], a_cols=(N_SHARD, 512), into=(lax.empty(gmix.shape, F32), 0), name="dw_ssm")
    dmx = mm_tn(o[None], dyattn[None], a_cols=(N_SHARD, 256), into=(dmx, 2), name="dw_attn")
    dmx = mm_tn(merged[None], dmix[None], a_cols=(N_SHARD, 256), into=(dmx, 3), name="dw_out")
    ub = u[None]
    din = jnp.concatenate([
        mm_tn(ub, dgates[None], name="dw_in_gates", tn=1024)[0], mm_tn(ub, dz[None], name="dw_in_z", tn=1024)[0],
        mm_tn(ub, dxs[None], name="dw_in_xs", tn=1024)[0], mm_tn(ub, dbm[None], name="dw_in_b")[0],
        mm_tn(ub, dcm[None], name="dw_in_c")[0], mm_rows(ddtT, u, name="dw_in_dt").T,
        mm_tn(ub, dq[None], name="dw_in_q")[0], mm_tn(ub, dk[None], name="dw_in_k")[0],
        mm_tn(ub, dv[None], name="dw_in_v")[0]], axis=1)
    din = din.reshape(D, N_SHARD, IN_COLS // N_SHARD).transpose(1, 0, 2)
    d_mx = [dmx, din]
    (dh1, dg_mix_pre), (pair_mx0, pair_mx1, rffn2) = mix_in_bwd(
        dh2, h1, mix_pre_g, dgates, dz, dxs, dbm, dcm, ddtT, dq, dk, dv, w_gz, w_xbc, w_dtT, w_qkv, name="mix_in_bwd",
        ride=join_rides(pair_ride(d_mx), share_ride(tot_f2)))
    sums_mx = chip_sums(d_mx, [pair_mx0, pair_mx1], mix_tiles, "mx")
    (dx, df1, a1, dgate1, dup1, dg_ffn1_pre, dg_ffn1_post), parts_mx = ffn_bwd(
        dh1, x2, f1, gate1, up1, ffn1_pre_g, ffn1_post_g, wffn1, name="ffn1_bwd", ride=chips_ride(sums_mx))
    rmx, rin = run_ride(share_ride(totals(parts_mx, sums_mx, mix_tiles, "mx")), name="rs_share_mx")
    d_f1 = ffn_grads(n1, dgate1, dup1, a1, df1, "1")
    sums_f1 = chip_sums(d_f1, run_ride(pair_ride(d_f1), name="rs_pair_f1"), ffn_tiles, "f1")
    parts_f1 = run_ride(chips_ride(sums_f1), name="rs_chips_f1")
    (rffn1,) = run_ride(share_ride(totals(parts_f1, sums_f1, ffn_tiles, "f1")), name="rs_share_f1")
    FS = D_FF // N_SHARD
    gw = {
        'ffn1_w_gate': rffn1[0:FS], 'ffn1_w_up': rffn1[FS:2 * FS], 'ffn1_w_down': rffn1[2 * FS:],
        'ffn2_w_gate': rffn2[0:FS], 'ffn2_w_up': rffn2[FS:2 * FS], 'ffn2_w_down': rffn2[2 * FS:],
        'w_ssm_proj': rmx[0:512], 'w_attn_proj': rmx[512:768], 'w_out': rmx[768:1024], 'w_in': rin,
    }

    dconv_w = jnp.concatenate([acc[:, :SSM_CONV].transpose(1, 0, 2).reshape(SSM_CONV, -1)
                               for acc in (acc_xs, acc_bm, acc_cm)], axis=1)
    dconv_b = jnp.concatenate([acc[:, SSM_CONV].reshape(-1) for acc in (acc_xs, acc_bm, acc_cm)])
    small_local = {
        'ffn1_pre_g': dg_ffn1_pre, 'ffn1_post_g': dg_ffn1_post, 'mix_pre_g': dg_mix_pre, 'conv_w': dconv_w,
        'conv_b': dconv_b, 'dt_bias': acc_head[:, :, 0], 'a_log': acc_head[:, :, 1],
        'd_skip': acc_xs[:, SSM_CONV + 2].reshape(SSM_HEADS, SSM_HEAD_DIM).sum(axis=1),
        'ssm_norm_g': acc_xs[:, SSM_CONV + 1].reshape(-1), 'attn_sinks': dsinks, 'rel_bias_table': dtable,
        'mix_post_g': dg_mix_post, 'ffn2_pre_g': dg_ffn2_pre, 'ffn2_post_g': dg_ffn2_post,
    }
    full_shapes = [(SSM_CONV, SSM_CONV_DIM) if n == 'conv_w' else w[n].shape for n in SMALL]
    packed = _pack_rows([small_local[n] for n in SMALL] + [jnp.sum(loss_parts[:, 0, 0])])
    total = small_allreduce(packed, name="allreduce_small")
    *small_g, loss = _unpack_rows(total, full_shapes + [()])
    for n, g in zip(SMALL, small_g):
        gw[n] = g
    gw['conv_w'] = lax.dynamic_slice(gw['conv_w'], (0, s_me * (SSM_CONV_DIM // N_SHARD)),
                                     (SSM_CONV, SSM_CONV_DIM // N_SHARD))[None]

    delta, new_m, new_v = {}, {}, {}
    for n in BIG:
        lay = tr if n.endswith(('w_gate', 'w_up')) else (lambda a: a)
        d_, m_, v_ = adamw(lay(w[n][0]), gw[n], lay(m[n][0]), lay(v[n][0]), name="adamw_" + n, rt=gw[n].shape[0] // 4)
        gw[n] = lay(gw[n])[None]
        delta[n], new_m[n], new_v[n] = lay(d_)[None], lay(m_)[None], lay(v_)[None]
    shapes = [w[n].shape for n in SMALL]
    outs = adamw(_pack_rows([w[n] for n in SMALL]), _pack_rows([gw[n] for n in SMALL]),
                 _pack_rows([m[n] for n in SMALL]), _pack_rows([v[n] for n in SMALL]), name="adamw_small")
    for res, buf in zip((delta, new_m, new_v), outs):
        for n, val in zip(SMALL, _unpack_rows(buf, shapes)):
            res[n] = val
    return (loss, dx.reshape(batch, seq, D), *[gw[n].reshape(w[n].shape) for n in WEIGHTS],
            *[delta[n] for n in WEIGHTS], *[new_m[n] for n in WEIGHTS], *[new_v[n] for n in WEIGHTS])
```

```python
import functools
import math

import jax
import jax.numpy as jnp
import numpy as np
from jax import lax
from jax.experimental import pallas as pl
from jax.experimental.pallas import tpu as pltpu

F32 = jnp.float32
BF16 = jnp.bfloat16

D_MODEL = 1024
D_FF = 2816
N_SHARD = 4
SSM_D_INNER = 2048
SSM_HEAD_DIM = 64
SSM_HEADS = 32
SSM_GROUPS = 4
SSM_HPG = SSM_HEADS // SSM_GROUPS
SSM_GW = SSM_D_INNER // SSM_GROUPS
SSM_STATE = 128
SSM_CONV = 4
SSM_CHUNK = 128
SSM_CONV_DIM = SSM_D_INNER + 2 * SSM_GROUPS * SSM_STATE
ATTN_Q_HEADS = 16
ATTN_KV_HEADS = 4
ATTN_REP = ATTN_Q_HEADS // ATTN_KV_HEADS
ATTN_HEAD_DIM = 64
ATTN_BLOCK = 128
ATTN_Q_DIM = 1024
ATTN_KV_DIM = 256
REL_BUCKETS = 32
REL_MAX_DISTANCE = 128
RMS_EPS = 1e-6
IN_COLS = 8736
ADAM_LR = 0.001
ADAM_B1 = 0.9
ADAM_B2 = 0.999
ADAM_EPS = 1e-08
ADAM_WD = 0.01
ADAM_STEP = 10
HALO = 8

VMEM_LIMIT = 56 * 1024 * 1024


def _cparams(*sem):
    return pltpu.CompilerParams(dimension_semantics=tuple(sem) if sem else None, vmem_limit_bytes=VMEM_LIMIT)


def _dot(a, b):
    return jnp.dot(a, b, preferred_element_type=F32)


def _dot_nt(a, b):
    return lax.dot_general(a, b, (((1,), (1,)), ((), ())), preferred_element_type=F32)


def _dot_tn(a, b):
    return lax.dot_general(a, b, (((0,), (0,)), ((), ())), preferred_element_type=F32)


def _dot_hi(a, b):
    return jnp.dot(a, b, preferred_element_type=F32, precision=lax.Precision.HIGHEST)


def _sigmoid(x):
    return 0.5 * jnp.tanh(0.5 * x) + 0.5


def _resident(shape, index=None):
    index = (0,) * len(shape) if index is None else tuple(index)
    return pl.BlockSpec(shape, lambda *_: index, pipeline_mode=pl.Buffered(1))


def _part(packed, rows, part):
    return _resident((N_SHARD, rows, packed.shape[2]), (0, part, 0))


def _rows(tm, width):
    return pl.BlockSpec((tm, width), lambda i: (i, 0))


class Ride:
    def __init__(self, inputs, out_shapes, aliases, scratch, start, finish):
        self.inputs, self.out_shapes, self.aliases = list(inputs), list(out_shapes), list(aliases)
        self.scratch, self.start, self.finish = list(scratch), start, finish


def join_rides(*rides):
    def cut(refs, sizes):
        out, at = [], 0
        for n in sizes:
            out.append(refs[at:at + n])
            at += n
        return out

    k_in = [len(r.inputs) for r in rides]
    k_out = [len(r.out_shapes) for r in rides]
    k_scr = [len(r.scratch) for r in rides]

    def each(step):
        def run(ins, outs, sems):
            for r, i, o, s in zip(rides, cut(ins, k_in), cut(outs, k_out), cut(sems, k_scr)):
                getattr(r, step)(i, o, s)
        return run

    aliases = [(sum(k_in[:n]) + i, sum(k_out[:n]) + j) for n, r in enumerate(rides) for i, j in r.aliases]
    return Ride([a for r in rides for a in r.inputs], [s for r in rides for s in r.out_shapes], aliases,
                [s for r in rides for s in r.scratch], each("start"), each("finish"))


def _call(body, *, grid, in_specs, args, out_specs, out_shape, name, sem, scratch=(), aliases=None, ride=None):
    aliases = dict(aliases or {})
    if ride is None:
        return pl.pallas_call(body, grid=grid, in_specs=in_specs, out_specs=out_specs, out_shape=out_shape,
                              scratch_shapes=list(scratch), input_output_aliases=aliases,
                              compiler_params=_cparams(*sem), name=name)(*args)
    n_in, n_out, n_scr = len(in_specs), len(out_specs), len(scratch)
    k_in, k_out = len(ride.inputs), len(ride.out_shapes)

    def riding(*refs):
        ins, refs = refs[:n_in], refs[n_in:]
        ex_in, refs = refs[:k_in], refs[k_in:]
        outs, refs = refs[:n_out], refs[n_out:]
        ex_out, refs = refs[:k_out], refs[k_out:]
        scr, ex_scr = refs[:n_scr], refs[n_scr:]
        first = functools.reduce(jnp.logical_and, [pl.program_id(a) == 0 for a in range(len(grid))])
        last = functools.reduce(jnp.logical_and, [pl.program_id(a) == grid[a] - 1 for a in range(len(grid))])

        @pl.when(first)
        def _():
            ride.start(ex_in, ex_out, ex_scr)

        body(*ins, *outs, *scr)

        @pl.when(last)
        def _():
            ride.finish(ex_in, ex_out, ex_scr)

    aliases.update({n_in + i: n_out + j for i, j in ride.aliases})
    res = pl.pallas_call(
        riding, grid=grid, in_specs=list(in_specs) + [ANY] * k_in, out_specs=list(out_specs) + [ANY] * k_out,
        out_shape=list(out_shape) + ride.out_shapes, scratch_shapes=list(scratch) + ride.scratch,
        input_output_aliases=aliases, compiler_params=_cparams(*["arbitrary"] * len(grid)), name=name,
    )(*args, *ride.inputs)
    return res[:n_out], res[n_out:]


def run_ride(ride, *, name):
    k_in = len(ride.inputs)

    def body(*refs):
        ex_in, ex_out, sems = refs[:k_in], refs[k_in:k_in + len(ride.out_shapes)], refs[k_in + len(ride.out_shapes):]
        ride.start(ex_in, ex_out, sems)
        ride.finish(ex_in, ex_out, sems)

    return pl.pallas_call(body, in_specs=[ANY] * k_in, out_specs=[ANY] * len(ride.out_shapes),
                          out_shape=ride.out_shapes, scratch_shapes=ride.scratch,
                          input_output_aliases=dict(ride.aliases), name=name)(*ride.inputs)


def ffn_fwd(h, g_pre, wffn, g_post, target=None, *, name, tm=512, ride=None):
    T, D = h.shape
    NS, FS = N_SHARD, wffn.shape[1] // 3
    with_loss = target is not None
    nt = T // tm

    def body(*refs):
        if with_loss:
            (h_ref, gpre_ref, wg_ref, wu_ref, wd_ref, gpost_ref, tgt_ref,
             hout_ref, n_ref, gate_ref, up_ref, f_ref, dy_ref, loss_ref) = refs
        else:
            (h_ref, gpre_ref, wg_ref, wu_ref, wd_ref, gpost_ref,
             hout_ref, n_ref, gate_ref, up_ref, f_ref) = refs
        hh = h_ref[...]
        r = lax.rsqrt(jnp.mean(hh * hh, axis=-1, keepdims=True) + RMS_EPS)
        n = (hh * r * gpre_ref[...]).astype(BF16)
        n_ref[...] = n
        acc = jnp.zeros((tm, D), F32)
        for s in range(NS):
            gate = _dot_nt(n, wg_ref[s])
            up = _dot_nt(n, wu_ref[s])
            gate_ref[s] = gate.astype(BF16)
            up_ref[s] = up.astype(BF16)
            a = (gate * _sigmoid(gate) * up).astype(BF16)
            acc = acc + _dot(a, wd_ref[s])
        f_ref[...] = acc
        r2 = lax.rsqrt(jnp.mean(acc * acc, axis=-1, keepdims=True) + RMS_EPS)
        out = hh + 0.5 * (acc * r2 * gpost_ref[...])
        hout_ref[...] = out
        if with_loss:
            e = out - tgt_ref[...]
            dy_ref[...] = e * (1.0 / D)
            loss_ref[...] = jnp.full((1, 8, 128), 0.5 / D, F32) * jnp.sum(e * e)

    in_specs = [_rows(tm, D), _resident((1, D)), _part(wffn, FS, 0), _part(wffn, FS, 1), _part(wffn, FS, 2),
                _resident((1, D))]
    args = [h, g_pre, wffn, wffn, wffn, g_post]
    out_shape = [jax.ShapeDtypeStruct((T, D), F32), jax.ShapeDtypeStruct((T, D), BF16),
                 jax.ShapeDtypeStruct((NS, T, FS), BF16), jax.ShapeDtypeStruct((NS, T, FS), BF16),
                 jax.ShapeDtypeStruct((T, D), F32)]
    seg = pl.BlockSpec((NS, tm, FS), lambda i: (0, i, 0))
    out_specs = [_rows(tm, D), _rows(tm, D), seg, seg, _rows(tm, D)]
    if with_loss:
        in_specs.append(_rows(tm, D))
        args.append(target)
        out_shape += [jax.ShapeDtypeStruct((T, D), F32), jax.ShapeDtypeStruct((nt, 8, 128), F32)]
        out_specs += [_rows(tm, D), pl.BlockSpec((1, 8, 128), lambda i: (i, 0, 0))]
    return _call(body, grid=(nt,), in_specs=in_specs, args=args, out_specs=out_specs, out_shape=out_shape,
                 sem=("parallel",), name=name, ride=ride)


def ffn_bwd(dout, h, f, gate, up, g_pre, g_post, wffn, *, name, tm=256, ride=None):
    T, D = h.shape
    NS, FS = N_SHARD, wffn.shape[1] // 3
    nt = T // tm

    def body(dout_ref, h_ref, f_ref, gate_ref, up_ref, gpre_ref, gpost_ref, wg_ref, wu_ref, wd_ref,
             dh_ref, df_ref, a_ref, dgate_ref, dup_ref, dgpre_ref, dgpost_ref):
        @pl.when(pl.program_id(0) == 0)
        def _():
            dgpre_ref[...] = jnp.zeros_like(dgpre_ref)
            dgpost_ref[...] = jnp.zeros_like(dgpost_ref)

        do = dout_ref[...]
        ff = f_ref[...]
        d_fn = 0.5 * do
        r2 = lax.rsqrt(jnp.mean(ff * ff, axis=-1, keepdims=True) + RMS_EPS)
        dgpost_ref[...] += jnp.sum(d_fn * ff * r2, axis=0, keepdims=True)
        t = d_fn * gpost_ref[...]
        df = r2 * t - ff * (r2 * r2 * r2 * jnp.mean(t * ff, axis=-1, keepdims=True))
        dfb = df.astype(BF16)
        df_ref[...] = dfb
        dn = jnp.zeros((tm, D), F32)
        for s in range(NS):
            da = _dot_nt(dfb, wd_ref[s])
            g = gate_ref[s].astype(F32)
            u = up_ref[s].astype(F32)
            sg = _sigmoid(g)
            silu = g * sg
            a_ref[s] = (silu * u).astype(BF16)
            dgt = (da * u * (sg * (1.0 + g * (1.0 - sg)))).astype(BF16)
            dupv = (da * silu).astype(BF16)
            dgate_ref[s] = dgt
            dup_ref[s] = dupv
            dn = dn + _dot(dgt, wg_ref[s]) + _dot(dupv, wu_ref[s])
        hh = h_ref[...]
        r1 = lax.rsqrt(jnp.mean(hh * hh, axis=-1, keepdims=True) + RMS_EPS)
        dgpre_ref[...] += jnp.sum(dn * hh * r1, axis=0, keepdims=True)
        t = dn * gpre_ref[...]
        dh_ref[...] = do + r1 * t - hh * (r1 * r1 * r1 * jnp.mean(t * hh, axis=-1, keepdims=True))

    seg = pl.BlockSpec((NS, tm, FS), lambda i: (0, i, 0))
    acc = pl.BlockSpec((1, D), lambda i: (0, 0))
    return _call(
        body, grid=(nt,),
        in_specs=[_rows(tm, D), _rows(tm, D), _rows(tm, D), seg, seg, _resident((1, D)), _resident((1, D)),
                  _part(wffn, FS, 0), _part(wffn, FS, 1), _part(wffn, FS, 2)],
        args=[dout, h, f, gate, up, g_pre, g_post, wffn, wffn, wffn],
        out_specs=[_rows(tm, D), _rows(tm, D), seg, seg, seg, acc, acc],
        out_shape=[jax.ShapeDtypeStruct((T, D), F32), jax.ShapeDtypeStruct((T, D), BF16),
                   jax.ShapeDtypeStruct((NS, T, FS), BF16), jax.ShapeDtypeStruct((NS, T, FS), BF16),
                   jax.ShapeDtypeStruct((NS, T, FS), BF16),
                   jax.ShapeDtypeStruct((1, D), F32), jax.ShapeDtypeStruct((1, D), F32)],
        sem=("arbitrary",), name=name, ride=ride)


def mm_tn(a, g, *, name, tt=2048, tn=None, a_cols=None, into=None):
    Ba, T, _ = a.shape
    Bg, _, N = g.shape
    B, K = a_cols if a_cols else (max(Ba, Bg), a.shape[2])
    tn = N if tn is None else tn
    tt = min(tt, T)
    nsteps = T // tt

    def body(*refs):
        a_ref, g_ref, o_ref = refs[0], refs[1], refs[-1]

        @pl.when(pl.program_id(2) == 0)
        def _():
            o_ref[...] = jnp.zeros_like(o_ref)

        o_ref[0] += _dot_tn(a_ref[0], g_ref[0].astype(BF16))

    if a_cols:
        a_map = lambda b, j, t: (0, t, b)
    else:
        a_map = (lambda b, j, t: (b, t, 0)) if Ba > 1 else (lambda b, j, t: (0, t, 0))
    in_specs = [pl.BlockSpec((1, tt, K), a_map),
                pl.BlockSpec((1, tt, tn), (lambda b, j, t: (b, t, j)) if Bg > 1 else (lambda b, j, t: (0, t, j)))]
    args = [a, g]
    if into is None:
        out_shape, part, aliases = jax.ShapeDtypeStruct((B, K, N), F32), 0, {}
    else:
        buf, part = into
        out_shape, aliases = jax.ShapeDtypeStruct(buf.shape, F32), {2: 0}
        in_specs.append(ANY)
        args.append(buf)
    return pl.pallas_call(
        body, grid=(B, N // tn, nsteps), in_specs=in_specs,
        out_specs=pl.BlockSpec((1, K, tn), lambda b, j, t: (b, part, j)),
        out_shape=out_shape, input_output_aliases=aliases,
        compiler_params=_cparams("parallel", "parallel", "arbitrary"), name=name)(*args)


def mix_in_fwd(h, g, w_gz, w_xbc, w_dtT, w_qkv, *, name, tm=256, ride=None):
    T, D = h.shape
    nt = T // tm
    CB = 1024

    def body(h_ref, g_ref, wgz_ref, wxbc_ref, wdtT_ref, wqkv_ref,
             u_ref, gates_ref, z_ref, xbc_ref, dtT_ref, q_ref, k_ref, v_ref):
        hh = h_ref[...]
        r = lax.rsqrt(jnp.mean(hh * hh, axis=-1, keepdims=True) + RMS_EPS)
        u = (hh * r * g_ref[...]).astype(BF16)
        u_ref[...] = u
        for cb in range(0, 2048, CB):
            gates_ref[:, cb:cb + CB] = _dot(u, wgz_ref[:, cb:cb + CB]).astype(BF16)
            z_ref[:, cb:cb + CB] = _dot(u, wgz_ref[:, 2048 + cb:2048 + cb + CB])
        for cb in range(0, SSM_CONV_DIM, CB):
            xbc_ref[:, cb:cb + CB] = _dot(u, wxbc_ref[:, cb:cb + CB])
        dtT_ref[...] = _dot_nt(wdtT_ref[...], u)
        q_ref[...] = _dot(u, wqkv_ref[:, 0:ATTN_Q_DIM]).astype(BF16)
        k_ref[...] = _dot(u, wqkv_ref[:, ATTN_Q_DIM:ATTN_Q_DIM + ATTN_KV_DIM]).astype(BF16)
        v_ref[...] = _dot(u, wqkv_ref[:, ATTN_Q_DIM + ATTN_KV_DIM:]).astype(BF16)

    sds = jax.ShapeDtypeStruct
    return _call(
        body, grid=(nt,),
        in_specs=[_rows(tm, D), _resident((1, D)), _resident(w_gz.shape), _resident(w_xbc.shape),
                  _resident(w_dtT.shape), _resident(w_qkv.shape)],
        args=[h, g, w_gz, w_xbc, w_dtT, w_qkv],
        out_specs=[_rows(tm, D), _rows(tm, 2048), _rows(tm, 2048), _rows(tm, SSM_CONV_DIM),
                   pl.BlockSpec((SSM_HEADS, tm), lambda i: (0, i)),
                   _rows(tm, ATTN_Q_DIM), _rows(tm, ATTN_KV_DIM), _rows(tm, ATTN_KV_DIM)],
        out_shape=[sds((T, D), BF16), sds((T, 2048), BF16), sds((T, 2048), F32), sds((T, SSM_CONV_DIM), F32),
                   sds((SSM_HEADS, T), F32),
                   sds((T, ATTN_Q_DIM), BF16), sds((T, ATTN_KV_DIM), BF16), sds((T, ATTN_KV_DIM), BF16)],
        sem=("parallel",), name=name, ride=ride)


def _softplus(x):
    return jnp.maximum(x, 0.0) + jnp.log(1.0 + jnp.exp(-jnp.abs(x)))


def _iota(shape, axis):
    return lax.broadcasted_iota(jnp.int32, shape, axis)


def _head_expand(g, per_head):
    shape = (SSM_HEADS, SSM_HPG * per_head)
    head = lax.shift_right_logical(_iota(shape, 1), int(math.log2(per_head)))
    return (_iota(shape, 0) == SSM_HPG * g + head).astype(F32)


def _conv_pre(x_ref, halo_ref, w_ref, b_ref, xp_ref, first):
    Q = SSM_CHUNK
    halo = jnp.where(first, 0.0, halo_ref[...])
    xp_ref[0:HALO, :] = halo
    xp_ref[HALO:HALO + Q, :] = x_ref[...]
    pre = b_ref[...] + w_ref[3:4, :] * xp_ref[HALO:HALO + Q, :]
    for k in range(SSM_CONV - 1):
        pre = pre + w_ref[k:k + 1, :] * xp_ref[pl.ds(HALO - 3 + k, Q), :]
    return pre


def _ssd_specs(nc):
    Q, GW, N = SSM_CHUNK, SSM_GW, SSM_STATE
    nb_xs = SSM_D_INNER // N
    nb_c = nb_xs + SSM_GROUPS

    def rb(cmap):
        def row(b, c, g):
            return b * nc + cmap(c)
        return row

    def specs(cmap):
        row = rb(cmap)
        hrow = lambda b, c, g: jnp.maximum(row(b, c, g) * (Q // HALO) - 1, 0)
        return dict(
            xs=pl.BlockSpec((Q, GW), lambda b, c, g: (row(b, c, g), g)),
            bm=pl.BlockSpec((Q, N), lambda b, c, g: (row(b, c, g), nb_xs + g)),
            cm=pl.BlockSpec((Q, N), lambda b, c, g: (row(b, c, g), nb_c + g)),
            xs_halo=pl.BlockSpec((HALO, GW), lambda b, c, g: (hrow(b, c, g), g)),
            bm_halo=pl.BlockSpec((HALO, N), lambda b, c, g: (hrow(b, c, g), nb_xs + g)),
            cm_halo=pl.BlockSpec((HALO, N), lambda b, c, g: (hrow(b, c, g), nb_c + g)),
            grp=pl.BlockSpec((Q, GW), lambda b, c, g: (row(b, c, g), g)),
            dt=pl.BlockSpec((Q, SSM_HEADS), lambda b, c, g: (row(b, c, g), 0)),
            dtT=pl.BlockSpec((SSM_HEADS, Q), lambda b, c, g: (0, row(b, c, g))),
            w_xs=pl.BlockSpec((SSM_CONV, GW), lambda b, c, g: (0, g)),
            w_bm=pl.BlockSpec((SSM_CONV, N), lambda b, c, g: (0, nb_xs + g)),
            w_cm=pl.BlockSpec((SSM_CONV, N), lambda b, c, g: (0, nb_c + g)),
            b_xs=pl.BlockSpec((1, GW), lambda b, c, g: (0, g)),
            b_bm=pl.BlockSpec((1, N), lambda b, c, g: (0, nb_xs + g)),
            b_cm=pl.BlockSpec((1, N), lambda b, c, g: (0, nb_c + g)),
            vec_g=pl.BlockSpec((1, GW), lambda b, c, g: (0, g)),
            row32=pl.BlockSpec((1, SSM_HEADS), lambda b, c, g: (0, 0)),
            col32=pl.BlockSpec((SSM_HEADS, 1), lambda b, c, g: (0, 0)),
            state=pl.BlockSpec((1, SSM_HPG, SSM_HEAD_DIM, N), lambda b, c, g: (row(b, c, g), g, 0, 0)),
            dtT_g=pl.BlockSpec((SSM_HPG, Q), lambda b, c, g: (g, row(b, c, g))),
            conv=pl.BlockSpec((1, HALO, GW + 2 * N), lambda b, c, g: (g, 0, 0)),
            cols=pl.BlockSpec((1, Q, 4 * SSM_HPG), lambda b, c, g: (g, row(b, c, g), 0)),
            decay=pl.BlockSpec((1, SSM_HPG, 1), lambda b, c, g: (row(b, c, g), g, 0)),
            vec2=pl.BlockSpec((2, GW), lambda b, c, g: (0, g)),
            col2=pl.BlockSpec((SSM_HPG, 2), lambda b, c, g: (g, 0)),
            col_g=pl.BlockSpec((SSM_HPG, 1), lambda b, c, g: (g, 0)),
            pairs=pl.BlockSpec((1, SSM_HPG // 2, N, 2 * SSM_HEAD_DIM), lambda b, c, g: (row(b, c, g), g, 0, 0)),
        )
    return specs


def _ssd_chunk_common(first, g, xs_ref, bm_ref, cm_ref, xs_halo, bm_halo, cm_halo, w_xs, w_bm, w_cm, b_xs, b_bm, b_cm,
                      dt_ref, dtT_ref, dtb_ref, dtbT_ref, alog_ref, alogT_ref, xp_xs, xp_bm, xp_cm):
    Q = SSM_CHUNK
    pre_xs = _conv_pre(xs_ref, xs_halo, w_xs, b_xs, xp_xs, first)
    pre_bm = _conv_pre(bm_ref, bm_halo, w_bm, b_bm, xp_bm, first)
    pre_cm = _conv_pre(cm_ref, cm_halo, w_cm, b_cm, xp_cm, first)
    xs = pre_xs * _sigmoid(pre_xs)
    bm = pre_bm * _sigmoid(pre_bm)
    cm = pre_cm * _sigmoid(pre_cm)
    dtr = dt_ref[...] + dtb_ref[...]
    dtrT = dtT_ref[...] + dtbT_ref[...]
    dt = _softplus(dtr)
    dtT = _softplus(dtrT)
    a = -jnp.exp(alog_ref[...])
    aT = -jnp.exp(alogT_ref[...])
    tri = (_iota((Q, Q), 0) >= _iota((Q, Q), 1)).astype(F32)
    triT = (_iota((Q, Q), 0) <= _iota((Q, Q), 1)).astype(F32)
    acs = _dot_hi(tri, dt * a)
    acsT = _dot_hi(dtT * aT, triT)
    return dict(pre_xs=pre_xs, pre_bm=pre_bm, pre_cm=pre_cm, xs=xs, bm=bm, cm=cm, dtr=dtr, dt=dt, a=a,
                acs=acs, acsT=acsT, tri=tri, triT=triT)


def _v1_ssd_fwd(xbc, z, dt_raw, dt_rawT, conv_w, conv_b, dt_bias, a_log, d_skip, norm_g, *, batch, name):
    T = xbc.shape[0]
    Q, GW, N, P, HPG = SSM_CHUNK, SSM_GW, SSM_STATE, SSM_HEAD_DIM, SSM_HPG
    nc = T // batch // Q
    sp = _ssd_specs(nc)(lambda c: c)

    def body(xs_ref, bm_ref, cm_ref, xs_halo, bm_halo, cm_halo, z_ref, dt_ref, dtT_ref,
             w_xs, w_bm, w_cm, b_xs, b_bm, b_cm, dtb_ref, dtbT_ref, alog_ref, alogT_ref, dskip_ref, ng_ref,
             y_ref, ys_ref, st_ref, state, acsT_s, y_s, xp_xs, xp_bm, xp_cm):
        c = pl.program_id(1)
        g = pl.program_id(2)
        first = c == 0
        cc = _ssd_chunk_common(first, g, xs_ref, bm_ref, cm_ref, xs_halo, bm_halo, cm_halo, w_xs, w_bm, w_cm,
                               b_xs, b_bm, b_cm, dt_ref, dtT_ref, dtb_ref, dtbT_ref, alog_ref, alogT_ref,
                               xp_xs, xp_bm, xp_cm)
        xs, acs = cc["xs"], cc["acs"]
        acsT_s[...] = cc["acsT"]
        e64 = _head_expand(g, P)
        e128 = _head_expand(g, Q)
        acs_x = _dot_hi(acs, e64)
        acs_b = _dot_hi(acs, e128)
        x = xs * _dot_hi(cc["dt"], e64)
        last_x = acs_x[Q - 1:Q, :]
        xw = (x * jnp.exp(last_x - acs_x)).astype(BF16)
        ex = jnp.exp(acs_x)
        xb = x.astype(BF16)
        bb = cc["bm"].astype(BF16)
        cb = cc["cm"].astype(BF16)
        s = _dot_nt(cb, bb)
        causal = _iota((Q, Q), 0) >= _iota((Q, Q), 1)
        for r in range(HPG):
            hd = HPG * g + r

            @pl.when(first)
            def _():
                state[hd] = jnp.zeros((P, N), F32)

            seg = acs_b[:, Q * r:Q * (r + 1)] - acsT_s[pl.ds(hd, 1), :]
            m = (s * jnp.exp(jnp.where(causal, seg, -1e30))).astype(BF16)
            hp = state[hd]
            st_ref[0, r] = hp
            y_h = _dot(m, xb[:, P * r:P * (r + 1)]) + _dot_nt(cb, hp.astype(BF16)) * ex[:, P * r:P * (r + 1)]
            y_s[:, P * r:P * (r + 1)] = y_h
            decay = jnp.exp(acsT_s[pl.ds(hd, 1), pl.ds(Q - 1, 1)])
            state[hd] = hp * decay + _dot_tn(xw[:, P * r:P * (r + 1)], bb)
        dexp = _dot_hi(jnp.broadcast_to(dskip_ref[...], (8, SSM_HEADS)), e64)[0:1, :]
        y = y_s[...] + dexp * xs
        y_ref[...] = y
        zz = z_ref[...]
        yg = y * (zz * _sigmoid(zz))
        rr = lax.rsqrt(jnp.mean(yg * yg, axis=-1, keepdims=True) + RMS_EPS)
        ys_ref[...] = (yg * rr * ng_ref[...]).astype(BF16)

    col = lambda v: v.reshape(SSM_HEADS, 1)
    sds = jax.ShapeDtypeStruct
    return pl.pallas_call(
        body, grid=(batch, nc, SSM_GROUPS),
        in_specs=[sp["xs"], sp["bm"], sp["cm"], sp["xs_halo"], sp["bm_halo"], sp["cm_halo"], sp["grp"], sp["dt"],
                  sp["dtT"], sp["w_xs"], sp["w_bm"], sp["w_cm"], sp["b_xs"], sp["b_bm"], sp["b_cm"],
                  sp["row32"], sp["col32"], sp["row32"], sp["col32"], sp["row32"], sp["vec_g"]],
        out_specs=[sp["grp"], sp["grp"], sp["state"]],
        out_shape=[sds((T, SSM_D_INNER), F32), sds((T, SSM_D_INNER), BF16),
                   sds((T // Q, SSM_HEADS, P, N), F32)],
        scratch_shapes=[pltpu.VMEM((SSM_HEADS, P, N), F32), pltpu.VMEM((SSM_HEADS, Q), F32), pltpu.VMEM((Q, GW), F32),
                        pltpu.VMEM((HALO + Q, GW), F32), pltpu.VMEM((HALO + Q, N), F32), pltpu.VMEM((HALO + Q, N), F32)],
        compiler_params=_cparams("arbitrary", "arbitrary", "arbitrary"), name=name,
    )(xbc, xbc, xbc, xbc, xbc, xbc, z, dt_raw, dt_rawT, conv_w, conv_w, conv_w, conv_b, conv_b, conv_b,
      dt_bias, col(dt_bias), a_log, col(a_log), d_skip, norm_g)


def _attn_specs(nb):
    BLK = ATTN_BLOCK

    def specs(last):
        def cur(b, n):
            return b * nb + (n if last is None else jnp.minimum(n, nb - 1))

        def prev(b, n):
            return b * nb + jnp.maximum((n if last is None else jnp.minimum(n, nb - 1)) - 1, 0)
        return cur, prev
    return specs


def _attn_masks(n):
    shape = (ATTN_REP * ATTN_BLOCK, ATTN_BLOCK)
    ii = jnp.bitwise_and(_iota(shape, 0), ATTN_BLOCK - 1)
    jj = _iota(shape, 1)
    return jnp.logical_and(jj > ii, n > 0), jj <= ii


def _attn_group(kk, q_ref, bias_ref, sink_ref):
    BLK, HD = ATTN_BLOCK, ATTN_HEAD_DIM
    heads = range(ATTN_REP * kk, ATTN_REP * (kk + 1))
    qg = jnp.concatenate([q_ref[:, HD * hd:HD * (hd + 1)] for hd in heads], axis=0)
    bias_p = jnp.concatenate([bias_ref[hd, :, 0:BLK] for hd in heads], axis=0)
    bias_c = jnp.concatenate([bias_ref[hd, :, BLK:2 * BLK] for hd in heads], axis=0)
    sink = jnp.concatenate([jnp.broadcast_to(sink_ref[0:1, hd:hd + 1], (BLK, 1)) for hd in heads], axis=0)
    return qg, bias_p, bias_c, sink


def attn_bias(table_t, onehot, *, name):
    def body(t_ref, f_ref, o_ref):
        o_ref[...] = _dot_hi(t_ref[...], f_ref[...])
    return pl.pallas_call(body, out_shape=jax.ShapeDtypeStruct((ATTN_Q_HEADS, onehot.shape[1]), F32),
                          compiler_params=_cparams(), name=name)(table_t, onehot)


def attn_bias_bwd(dbias, onehot, *, name):
    def body(d_ref, f_ref, o_ref):
        o_ref[...] = lax.dot_general(d_ref[...], f_ref[...], (((1,), (1,)), ((), ())), preferred_element_type=F32,
                                     precision=lax.Precision.HIGHEST)
    return pl.pallas_call(body, out_shape=jax.ShapeDtypeStruct((ATTN_Q_HEADS, REL_BUCKETS), F32),
                          compiler_params=_cparams(), name=name)(dbias, onehot)


def attn_fwd(q, k, v, bias, sinks, *, batch, name):
    T = q.shape[0]
    BLK, HD = ATTN_BLOCK, ATTN_HEAD_DIM
    nb = T // batch // BLK
    cur, prev = _attn_specs(nb)(None)
    scale = HD ** -0.5

    def body(q_ref, kc_ref, kp_ref, vc_ref, vp_ref, bias_ref, sink_ref, o_ref, lse_ref):
        n = pl.program_id(1)
        m_prev, m_cur = _attn_masks(n)
        for kk in range(ATTN_KV_HEADS):
            ks = slice(HD * kk, HD * (kk + 1))
            kc, kp, vc, vp = kc_ref[:, ks], kp_ref[:, ks], vc_ref[:, ks], vp_ref[:, ks]
            qg, bias_p, bias_c, sink = _attn_group(kk, q_ref, bias_ref, sink_ref)
            lp = jnp.where(m_prev, _dot_nt(qg, kp) * scale + bias_p, -1e30)
            lc = jnp.where(m_cur, _dot_nt(qg, kc) * scale + bias_c, -1e30)
            mx = jnp.maximum(jnp.max(jnp.maximum(lp, lc), axis=-1, keepdims=True), sink)
            pp = jnp.exp(lp - mx)
            pc = jnp.exp(lc - mx)
            den = jnp.sum(pp + pc, axis=-1, keepdims=True) + jnp.exp(sink - mx)
            o = ((_dot(pp.astype(BF16), vp) + _dot(pc.astype(BF16), vc)) * (1.0 / den)).astype(BF16)
            lse = mx + jnp.log(den)
            for r in range(ATTN_REP):
                hd = ATTN_REP * kk + r
                o_ref[:, HD * hd:HD * (hd + 1)] = o[BLK * r:BLK * (r + 1)]
                lse_ref[:, hd:hd + 1] = lse[BLK * r:BLK * (r + 1)]

    sds = jax.ShapeDtypeStruct
    return pl.pallas_call(
        body, grid=(batch, nb),
        in_specs=[pl.BlockSpec((BLK, ATTN_Q_DIM), lambda b, n: (cur(b, n), 0)),
                  pl.BlockSpec((BLK, ATTN_KV_DIM), lambda b, n: (cur(b, n), 0)),
                  pl.BlockSpec((BLK, ATTN_KV_DIM), lambda b, n: (prev(b, n), 0)),
                  pl.BlockSpec((BLK, ATTN_KV_DIM), lambda b, n: (cur(b, n), 0)),
                  pl.BlockSpec((BLK, ATTN_KV_DIM), lambda b, n: (prev(b, n), 0)),
                  pl.BlockSpec((ATTN_Q_HEADS, BLK, 2 * BLK), lambda b, n: (0, 0, 0)),
                  pl.BlockSpec((1, ATTN_Q_HEADS), lambda b, n: (0, 0))],
        out_specs=[pl.BlockSpec((BLK, ATTN_Q_DIM), lambda b, n: (cur(b, n), 0)),
                   pl.BlockSpec((BLK, ATTN_Q_HEADS), lambda b, n: (cur(b, n), 0))],
        out_shape=[sds((T, ATTN_Q_DIM), BF16), sds((T, ATTN_Q_HEADS), F32)],
        compiler_params=_cparams("parallel", "parallel"), name=name)(q, k, k, v, v, bias, sinks)


def _proj_specs(wmix):
    return [_part(wmix, 512, 0), _part(wmix, 256, 2), _part(wmix, 256, 3)]


def _natural(w_ref):
    return w_ref[...].reshape(-1, w_ref.shape[2])


def mix_out_fwd(ys, o, gates, h, wmix, g_post, *, name, tm=512):
    T, D = h.shape
    nt = T // tm

    def body(ys_ref, o_ref, gates_ref, h_ref, wssm_ref, wattn_ref, wout_ref, g_ref,
             hout_ref, yssm_ref, yattn_ref, mix_ref, merged_ref):
        y_ssm = _dot(ys_ref[...], _natural(wssm_ref))
        y_attn = _dot(o_ref[...], _natural(wattn_ref))
        yssm_ref[...] = y_ssm.astype(BF16)
        yattn_ref[...] = y_attn.astype(BF16)
        merged = (_sigmoid(gates_ref[:, 0:D].astype(F32)) * y_ssm
                  + _sigmoid(gates_ref[:, D:2 * D].astype(F32)) * y_attn).astype(BF16)
        merged_ref[...] = merged
        mix = _dot(merged, _natural(wout_ref))
        mix_ref[...] = mix.astype(BF16)
        r = lax.rsqrt(jnp.mean(mix * mix, axis=-1, keepdims=True) + RMS_EPS)
        hout_ref[...] = h_ref[...] + mix * r * g_ref[...]

    sds = jax.ShapeDtypeStruct
    return pl.pallas_call(
        body, grid=(nt,),
        in_specs=[_rows(tm, SSM_D_INNER), _rows(tm, ATTN_Q_DIM), _rows(tm, 2 * D), _rows(tm, D),
                  *_proj_specs(wmix), _resident((1, D))],
        out_specs=[_rows(tm, D)] * 5,
        out_shape=[sds((T, D), F32), sds((T, D), BF16), sds((T, D), BF16), sds((T, D), BF16), sds((T, D), BF16)],
        compiler_params=_cparams("parallel"), name=name)(ys, o, gates, h, wmix, wmix, wmix, g_post)


def mix_out_bwd(dh, mix, y_ssm, y_attn, gates, wmix, g_post, *, name, tm=256, ride=None):
    T, D = dh.shape
    nt = T // tm

    def body(dh_ref, mix_ref, yssm_ref, yattn_ref, gates_ref, wssm_ref, wattn_ref, wout_ref, g_ref,
             dmix_ref, dyssm_ref, dyattn_ref, dgates_ref, dys_ref, do_ref, dg_ref):
        @pl.when(pl.program_id(0) == 0)
        def _():
            dg_ref[...] = jnp.zeros_like(dg_ref)

        do = dh_ref[...]
        mix = mix_ref[...].astype(F32)
        r = lax.rsqrt(jnp.mean(mix * mix, axis=-1, keepdims=True) + RMS_EPS)
        dg_ref[...] += jnp.sum(do * mix * r, axis=0, keepdims=True)
        t = do * g_ref[...]
        dmix = (r * t - mix * (r * r * r * jnp.mean(t * mix, axis=-1, keepdims=True))).astype(BF16)
        dmix_ref[...] = dmix
        dmerged = _dot_nt(dmix, _natural(wout_ref))
        s1 = _sigmoid(gates_ref[:, 0:D].astype(F32))
        s2 = _sigmoid(gates_ref[:, D:2 * D].astype(F32))
        dyssm = (dmerged * s1).astype(BF16)
        dyattn = (dmerged * s2).astype(BF16)
        dyssm_ref[...] = dyssm
        dyattn_ref[...] = dyattn
        dgates_ref[:, 0:D] = (dmerged * yssm_ref[...].astype(F32) * (s1 * (1.0 - s1))).astype(BF16)
        dgates_ref[:, D:2 * D] = (dmerged * yattn_ref[...].astype(F32) * (s2 * (1.0 - s2))).astype(BF16)
        dys_ref[...] = _dot_nt(dyssm, _natural(wssm_ref))
        do_ref[...] = _dot_nt(dyattn, _natural(wattn_ref)).astype(BF16)

    sds = jax.ShapeDtypeStruct
    return _call(
        body, grid=(nt,),
        in_specs=[_rows(tm, D), _rows(tm, D), _rows(tm, D), _rows(tm, D), _rows(tm, 2 * D),
                  *_proj_specs(wmix), _resident((1, D))],
        args=[dh, mix, y_ssm, y_attn, gates, wmix, wmix, wmix, g_post],
        out_specs=[_rows(tm, D), _rows(tm, D), _rows(tm, D), _rows(tm, 2 * D), _rows(tm, SSM_D_INNER),
                   _rows(tm, ATTN_Q_DIM), pl.BlockSpec((1, D), lambda i: (0, 0))],
        out_shape=[sds((T, D), BF16), sds((T, D), BF16), sds((T, D), BF16), sds((T, 2 * D), BF16),
                   sds((T, SSM_D_INNER), F32), sds((T, ATTN_Q_DIM), BF16), sds((1, D), F32)],
        sem=("arbitrary",), name=name, ride=ride)


def attn_bwd(q, k, v, o, do, lse, bias, sinks, *, batch, name):
    T = q.shape[0]
    BLK, HD = ATTN_BLOCK, ATTN_HEAD_DIM
    nb = T // batch // BLK
    cur, prev = _attn_specs(nb)(nb)
    scale = HD ** -0.5

    def body(q_ref, kc_ref, kp_ref, vc_ref, vp_ref, o_ref, do_ref, lse_ref, bias_ref, sink_ref,
             dq_ref, dk_ref, dv_ref, dbias_ref, dsink_ref, ck, cv):
        b = pl.program_id(0)
        n = pl.program_id(1)

        @pl.when(jnp.logical_and(b == 0, n == 0))
        def _():
            dbias_ref[...] = jnp.zeros_like(dbias_ref)
            dsink_ref[...] = jnp.zeros_like(dsink_ref)

        @pl.when(n == 0)
        def _():
            ck[...] = jnp.zeros_like(ck)
            cv[...] = jnp.zeros_like(cv)

        @pl.when(n == nb)
        def _():
            dk_ref[...] = ck[...].astype(BF16)
            dv_ref[...] = cv[...].astype(BF16)

        @pl.when(n < nb)
        def _():
            m_prev, m_cur = _attn_masks(n)
            lane16 = _iota((1, ATTN_Q_HEADS), 1)
            dsink = jnp.zeros((1, ATTN_Q_HEADS), F32)
            for kk in range(ATTN_KV_HEADS):
                ks = slice(HD * kk, HD * (kk + 1))
                kc, kp, vc, vp = kc_ref[:, ks], kp_ref[:, ks], vc_ref[:, ks], vp_ref[:, ks]
                heads = range(ATTN_REP * kk, ATTN_REP * (kk + 1))
                qg, bias_p, bias_c, sink = _attn_group(kk, q_ref, bias_ref, sink_ref)
                dog = jnp.concatenate([do_ref[:, HD * hd:HD * (hd + 1)] for hd in heads], axis=0)
                og = jnp.concatenate([o_ref[:, HD * hd:HD * (hd + 1)] for hd in heads], axis=0)
                lse = jnp.concatenate([lse_ref[:, hd:hd + 1] for hd in heads], axis=0)
                lp = jnp.where(m_prev, _dot_nt(qg, kp) * scale + bias_p, -1e30)
                lc = jnp.where(m_cur, _dot_nt(qg, kc) * scale + bias_c, -1e30)
                pp = jnp.exp(lp - lse)
                pc = jnp.exp(lc - lse)
                delta = jnp.sum(dog.astype(F32) * og.astype(F32), axis=-1, keepdims=True)
                dlp = pp * (_dot_nt(dog, vp) - delta)
                dlc = pc * (_dot_nt(dog, vc) - delta)
                sd = jnp.exp(sink - lse) * delta
                dlpb = dlp.astype(BF16)
                dlcb = dlc.astype(BF16)
                dqg = ((_dot(dlpb, kp) + _dot(dlcb, kc)) * scale).astype(BF16)
                for r, hd in enumerate(heads):
                    rows = slice(BLK * r, BLK * (r + 1))
                    dsink = dsink + jnp.where(lane16 == hd, -jnp.sum(sd[rows], axis=0, keepdims=True), 0.0)
                    dbias_ref[hd, :, 0:BLK] += dlp[rows]
                    dbias_ref[hd, :, BLK:2 * BLK] += dlc[rows]
                    dq_ref[:, HD * hd:HD * (hd + 1)] = dqg[rows]
                dk_ref[:, ks] = (ck[:, ks] + _dot_tn(dlpb, qg) * scale).astype(BF16)
                dv_ref[:, ks] = (cv[:, ks] + _dot_tn(pp.astype(BF16), dog)).astype(BF16)
                ck[:, ks] = _dot_tn(dlcb, qg) * scale
                cv[:, ks] = _dot_tn(pc.astype(BF16), dog)
            dsink_ref[...] += dsink

    sds = jax.ShapeDtypeStruct
    qspec = pl.BlockSpec((BLK, ATTN_Q_DIM), lambda b, n: (cur(b, n), 0))
    cspec = pl.BlockSpec((BLK, ATTN_KV_DIM), lambda b, n: (cur(b, n), 0))
    pspec = pl.BlockSpec((BLK, ATTN_KV_DIM), lambda b, n: (prev(b, n), 0))
    late = pl.BlockSpec((BLK, ATTN_KV_DIM), lambda b, n: (b * nb + jnp.maximum(n - 1, 0), 0))
    return pl.pallas_call(
        body, grid=(batch, nb + 1),
        in_specs=[qspec, cspec, pspec, cspec, pspec, qspec, qspec,
                  pl.BlockSpec((BLK, ATTN_Q_HEADS), lambda b, n: (cur(b, n), 0)),
                  pl.BlockSpec((ATTN_Q_HEADS, BLK, 2 * BLK), lambda b, n: (0, 0, 0)),
                  pl.BlockSpec((1, ATTN_Q_HEADS), lambda b, n: (0, 0))],
        out_specs=[qspec, late, late,
                   pl.BlockSpec((ATTN_Q_HEADS, BLK, 2 * BLK), lambda b, n: (0, 0, 0)),
                   pl.BlockSpec((1, ATTN_Q_HEADS), lambda b, n: (0, 0))],
        out_shape=[sds((T, ATTN_Q_DIM), BF16), sds((T, ATTN_KV_DIM), BF16), sds((T, ATTN_KV_DIM), BF16),
                   sds((ATTN_Q_HEADS, BLK, 2 * BLK), F32), sds((1, ATTN_Q_HEADS), F32)],
        scratch_shapes=[pltpu.VMEM((BLK, ATTN_KV_DIM), F32), pltpu.VMEM((BLK, ATTN_KV_DIM), F32)],
        compiler_params=_cparams("arbitrary", "arbitrary"), name=name)(q, k, k, v, v, o, do, lse, bias, sinks)


def _conv_bwd(dxc, pre, xp_ref, w_ref, carry_ref, acc_ref, dp_ref, g, last):
    Q = SSM_CHUNK
    sg = _sigmoid(pre)
    dpre = dxc * (sg * (1.0 + pre * (1.0 - sg)))
    dp_ref[0:Q, :] = dpre
    dp_ref[Q:Q + HALO, :] = carry_ref[g]
    carry_ref[g] = dpre[0:HALO, :]
    rows = [jnp.sum(dpre * xp_ref[pl.ds(HALO - 3 + k, Q), :], axis=0, keepdims=True) for k in range(SSM_CONV)]
    rows.append(jnp.sum(dpre, axis=0, keepdims=True))
    rows.append(jnp.zeros((HALO - SSM_CONV - 1, dpre.shape[1]), F32))
    acc_ref[g] += jnp.concatenate(rows, axis=0)
    dx = w_ref[3:4, :] * dpre
    for k in range(SSM_CONV - 1):
        dx = dx + w_ref[k:k + 1, :] * dp_ref[pl.ds(3 - k, Q), :]
    return dx


def _v1_ssd_bwd(dys, y, xbc, z, dt_raw, dt_rawT, states, conv_w, conv_b, dt_bias, a_log, d_skip, norm_g, *, batch, name):
    T = xbc.shape[0]
    Q, GW, N, P, HPG, G, H = SSM_CHUNK, SSM_GW, SSM_STATE, SSM_HEAD_DIM, SSM_HPG, SSM_GROUPS, SSM_HEADS
    nc = T // batch // Q
    sp = _ssd_specs(nc)(lambda c: nc - 1 - c)

    def body(xs_ref, bm_ref, cm_ref, xs_halo, bm_halo, cm_halo, z_ref, y_ref, dys_ref, dt_ref, dtT_ref, st_ref,
             w_xs, w_bm, w_cm, b_xs, b_bm, b_cm, dtb_ref, dtbT_ref, alog_ref, alogT_ref, dskip_ref, ng_ref,
             dz_ref, dxs_ref, dbm_ref, dcm_ref, ddt_ref, acc_xs, acc_bm, acc_cm, acc_head,
             dstate, acsT_s, dacsT_s, yoff_s, dxw_s, dx_s, xp_xs, xp_bm, xp_cm, dp_xs, dp_bm, dp_cm,
             cy_xs, cy_bm, cy_cm):
        b = pl.program_id(0)
        cr = pl.program_id(1)
        g = pl.program_id(2)
        c = nc - 1 - cr
        first = c == 0
        last = cr == 0

        @pl.when(jnp.logical_and(jnp.logical_and(b == 0, cr == 0), g == 0))
        def _():
            acc_xs[...] = jnp.zeros_like(acc_xs)
            acc_bm[...] = jnp.zeros_like(acc_bm)
            acc_cm[...] = jnp.zeros_like(acc_cm)
            acc_head[...] = jnp.zeros_like(acc_head)

        cc = _ssd_chunk_common(first, g, xs_ref, bm_ref, cm_ref, xs_halo, bm_halo, cm_halo, w_xs, w_bm, w_cm,
                               b_xs, b_bm, b_cm, dt_ref, dtT_ref, dtb_ref, dtbT_ref, alog_ref, alogT_ref,
                               xp_xs, xp_bm, xp_cm)
        xs, acs, dt, a = cc["xs"], cc["acs"], cc["dt"], cc["a"]
        acsT_s[...] = cc["acsT"]
        dacsT_s[...] = jnp.zeros_like(dacsT_s)
        e64 = _head_expand(g, P)
        e128 = _head_expand(g, Q)
        acs_x = _dot_hi(acs, e64)
        acs_b = _dot_hi(acs, e128)
        dt_x = _dot_hi(dt, e64)
        x = xs * dt_x
        w_x = jnp.exp(acs_x[Q - 1:Q, :] - acs_x)
        ex = jnp.exp(acs_x)

        yv = y_ref[...]
        zz = z_ref[...]
        sz = _sigmoid(zz)
        silu_z = zz * sz
        yg = yv * silu_z
        rr = lax.rsqrt(jnp.mean(yg * yg, axis=-1, keepdims=True) + RMS_EPS)
        dys_v = dys_ref[...]
        d_ng = jnp.sum(dys_v * yg * rr, axis=0, keepdims=True)
        t = dys_v * ng_ref[...]
        dyg = rr * t - yg * (rr * rr * rr * jnp.mean(t * yg, axis=-1, keepdims=True))
        dy = dyg * silu_z
        dz_ref[...] = (dyg * yv * (sz * (1.0 + zz * (1.0 - sz)))).astype(BF16)

        dexp = _dot_hi(jnp.broadcast_to(dskip_ref[...], (8, H)), e64)[0:1, :]
        d_dskip = _dot_nt(jnp.broadcast_to(jnp.sum(dy * xs, axis=0, keepdims=True), (8, GW)), e64)[0:1, :]

        dyb = dy.astype(BF16)
        xb = x.astype(BF16)
        xwb = (x * w_x).astype(BF16)
        bb = cc["bm"].astype(BF16)
        cb = cc["cm"].astype(BF16)
        s = _dot_nt(cb, bb)
        causal = _iota((Q, Q), 0) >= _iota((Q, Q), 1)
        lane_h = _iota((1, H), 1)
        ds_acc = jnp.zeros((Q, Q), F32)
        d_c = jnp.zeros((Q, N), F32)
        d_b = jnp.zeros((Q, N), F32)
        dacs = jnp.zeros((Q, H), F32)
        last_terms = jnp.zeros((1, H), F32)
        for r in range(HPG):
            hd = HPG * g + r
            cols = slice(P * r, P * (r + 1))

            @pl.when(last)
            def _():
                dstate[hd] = jnp.zeros((P, N), F32)

            seg = acs_b[:, Q * r:Q * (r + 1)] - acsT_s[pl.ds(hd, 1), :]
            l = jnp.exp(jnp.where(causal, seg, -1e30))
            m = s * l
            mb = m.astype(BF16)
            dyh = dyb[:, cols]
            hp = st_ref[0, r]
            hpb = hp.astype(BF16)
            dh = dstate[hd]
            dhb = dh.astype(BF16)
            yoff_s[:, cols] = _dot_nt(cb, hpb) * ex[:, cols]
            dye = (dy[:, cols] * ex[:, cols]).astype(BF16)
            d_c = d_c + _dot(dye, hpb)
            dhp_off = _dot_tn(dye, cb)
            dm = _dot_nt(dyh, xb[:, cols])
            dx_s[:, cols] = _dot_tn(mb, dyh)
            gmat = dm * m
            onehot = (lane_h == hd).astype(F32)
            dacs = dacs + jnp.sum(gmat, axis=-1, keepdims=True) * onehot
            dacsT_s[pl.ds(hd, 1), :] = -jnp.sum(gmat, axis=0, keepdims=True)
            ds_acc = ds_acc + dm * l
            dxw_s[:, cols] = _dot_nt(bb, dhb)
            d_b = d_b + _dot(xwb[:, cols], dhb)
            decay = jnp.exp(acsT_s[pl.ds(hd, 1), pl.ds(Q - 1, 1)])
            ddecay = jnp.sum(jnp.sum(dh * hp, axis=-1, keepdims=True), axis=0, keepdims=True)
            last_terms = last_terms + (ddecay * decay) * onehot
            dstate[hd] = dh * decay + dhp_off
        dsb = ds_acc.astype(BF16)
        d_c = d_c + _dot(dsb, bb)
        d_b = d_b + _dot_tn(dsb, cb)
        dxw = dxw_s[...]
        dx_full = dx_s[...] + dxw * w_x
        tw = _dot_nt(dxw * x * w_x, e64)
        dacs = dacs + _dot_nt(dy * yoff_s[...], e64) - tw
        last_terms = last_terms + jnp.sum(tw, axis=0, keepdims=True)
        eye = (_iota((Q, Q), 0) == _iota((Q, Q), 1)).astype(F32)
        dacs = dacs + lax.dot_general(eye, dacsT_s[...], (((1,), (1,)), ((), ())), preferred_element_type=F32,
                                      precision=lax.Precision.HIGHEST)
        dacs = dacs + jnp.where(_iota((Q, 1), 0) == Q - 1, 1.0, 0.0) * last_terms
        d_dta = _dot_hi(cc["triT"], dacs)
        ddt = d_dta * a + _dot_nt(dx_full * xs, e64)
        d_alog = jnp.sum(d_dta * dt, axis=0, keepdims=True) * a
        ddt_raw = ddt * _sigmoid(cc["dtr"])
        d_dtb = jnp.sum(ddt_raw, axis=0, keepdims=True)

        @pl.when(g == 0)
        def _():
            ddt_ref[...] = ddt_raw

        @pl.when(g > 0)
        def _():
            ddt_ref[...] += ddt_raw

        acc_head[...] += jnp.concatenate([d_dtb, d_alog, d_dskip, jnp.zeros((5, H), F32)], axis=0)
        dxs = dexp * dy + dx_full * dt_x
        dxs_ref[...] = _conv_bwd(dxs, cc["pre_xs"], xp_xs, w_xs, cy_xs, acc_xs, dp_xs, g, last).astype(BF16)
        dbm_ref[...] = _conv_bwd(d_b, cc["pre_bm"], xp_bm, w_bm, cy_bm, acc_bm, dp_bm, g, last).astype(BF16)
        dcm_ref[...] = _conv_bwd(d_c, cc["pre_cm"], xp_cm, w_cm, cy_cm, acc_cm, dp_cm, g, last).astype(BF16)
        acc_xs[g, pl.ds(SSM_CONV + 1, 1), :] += d_ng

    col = lambda v: v.reshape(H, 1)
    sds = jax.ShapeDtypeStruct
    row = lambda b, c, g: b * nc + (nc - 1 - c)
    full = lambda shape: pl.BlockSpec(shape, lambda b, c, g: (0,) * len(shape))
    return pl.pallas_call(
        body, grid=(batch, nc, G),
        in_specs=[sp["xs"], sp["bm"], sp["cm"], sp["xs_halo"], sp["bm_halo"], sp["cm_halo"], sp["grp"], sp["grp"],
                  sp["grp"], sp["dt"], sp["dtT"], sp["state"],
                  sp["w_xs"], sp["w_bm"], sp["w_cm"], sp["b_xs"], sp["b_bm"], sp["b_cm"],
                  sp["row32"], sp["col32"], sp["row32"], sp["col32"], sp["row32"], sp["vec_g"]],
        out_specs=[sp["grp"], sp["grp"],
                   pl.BlockSpec((Q, N), lambda b, c, g: (row(b, c, g), g)),
                   pl.BlockSpec((Q, N), lambda b, c, g: (row(b, c, g), g)),
                   sp["dt"], full((G, HALO, GW)), full((G, HALO, N)), full((G, HALO, N)), full((8, H))],
        out_shape=[sds((T, SSM_D_INNER), BF16), sds((T, SSM_D_INNER), BF16), sds((T, G * N), BF16),
                   sds((T, G * N), BF16), sds((T, H), F32),
                   sds((G, HALO, GW), F32), sds((G, HALO, N), F32), sds((G, HALO, N), F32), sds((8, H), F32)],
        scratch_shapes=[pltpu.VMEM((H, P, N), F32), pltpu.VMEM((H, Q), F32), pltpu.VMEM((H, Q), F32),
                        pltpu.VMEM((Q, GW), F32), pltpu.VMEM((Q, GW), F32), pltpu.VMEM((Q, GW), F32),
                        pltpu.VMEM((HALO + Q, GW), F32), pltpu.VMEM((HALO + Q, N), F32), pltpu.VMEM((HALO + Q, N), F32),
                        pltpu.VMEM((Q + HALO, GW), F32), pltpu.VMEM((Q + HALO, N), F32), pltpu.VMEM((Q + HALO, N), F32),
                        pltpu.VMEM((G, HALO, GW), F32), pltpu.VMEM((G, HALO, N), F32), pltpu.VMEM((G, HALO, N), F32)],
        compiler_params=_cparams("arbitrary", "arbitrary", "arbitrary"), name=name,
    )(xbc, xbc, xbc, xbc, xbc, xbc, z, y, dys, dt_raw, dt_rawT, states, conv_w, conv_w, conv_w, conv_b, conv_b, conv_b,
      dt_bias, col(dt_bias), a_log, col(a_log), d_skip, norm_g)


def mix_in_bwd(dh, h, g, dgates, dz, dxs, dbm, dcm, ddtT, dq, dk, dv, w_gz, w_xbc, w_dtT, w_qkv, *, name, tm=512,
               ride=None):
    T, D = h.shape
    nt = T // tm
    GN = SSM_GROUPS * SSM_STATE

    def body(dh_ref, h_ref, g_ref, dgates_ref, dz_ref, dxs_ref, dbm_ref, dcm_ref, ddt_ref, dq_ref, dk_ref, dv_ref,
             wgz_ref, wxbc_ref, wdt_ref, wqkv_ref, dhin_ref, dg_ref):
        @pl.when(pl.program_id(0) == 0)
        def _():
            dg_ref[...] = jnp.zeros_like(dg_ref)

        du = _dot_nt(dgates_ref[...], wgz_ref[:, 0:2048])
        du = du + _dot_nt(dz_ref[...], wgz_ref[:, 2048:4096])
        du = du + _dot_nt(dxs_ref[...], wxbc_ref[:, 0:SSM_D_INNER])
        du = du + _dot_nt(dbm_ref[...], wxbc_ref[:, SSM_D_INNER:SSM_D_INNER + GN])
        du = du + _dot_nt(dcm_ref[...], wxbc_ref[:, SSM_D_INNER + GN:])
        du = du + _dot_tn(ddt_ref[...].astype(BF16), wdt_ref[...])
        du = du + _dot_nt(dq_ref[...], wqkv_ref[:, 0:ATTN_Q_DIM])
        du = du + _dot_nt(dk_ref[...], wqkv_ref[:, ATTN_Q_DIM:ATTN_Q_DIM + ATTN_KV_DIM])
        du = du + _dot_nt(dv_ref[...], wqkv_ref[:, ATTN_Q_DIM + ATTN_KV_DIM:])
        hh = h_ref[...]
        r = lax.rsqrt(jnp.mean(hh * hh, axis=-1, keepdims=True) + RMS_EPS)
        dg_ref[...] += jnp.sum(du * hh * r, axis=0, keepdims=True)
        t = du * g_ref[...]
        dhin_ref[...] = dh_ref[...] + r * t - hh * (r * r * r * jnp.mean(t * hh, axis=-1, keepdims=True))

    sds = jax.ShapeDtypeStruct
    return _call(
        body, grid=(nt,),
        in_specs=[_rows(tm, D), _rows(tm, D), _resident((1, D)), _rows(tm, 2048), _rows(tm, 2048), _rows(tm, SSM_D_INNER),
                  _rows(tm, GN), _rows(tm, GN), pl.BlockSpec((SSM_HEADS, tm), lambda i: (0, i)),
                  _rows(tm, ATTN_Q_DIM), _rows(tm, ATTN_KV_DIM),
                  _rows(tm, ATTN_KV_DIM), _resident(w_gz.shape), _resident(w_xbc.shape), _resident(w_dtT.shape),
                  _resident(w_qkv.shape)],
        args=[dh, h, g, dgates, dz, dxs, dbm, dcm, ddtT, dq, dk, dv, w_gz, w_xbc, w_dtT, w_qkv],
        out_specs=[_rows(tm, D), pl.BlockSpec((1, D), lambda i: (0, 0))],
        out_shape=[sds((T, D), F32), sds((1, D), F32)],
        sem=("arbitrary",), name=name, ride=ride)


PAIRS = SSM_HPG // 2
PW = 2 * SSM_HEAD_DIM


def ssd_scalars(dt_rawT, dt_bias, a_log, *, name, chunks=4):
    H, T = dt_rawT.shape
    Q, G, HPG = SSM_CHUNK, SSM_GROUPS, SSM_HPG
    span = Q * chunks

    def body(dtT_ref, dtb_ref, alog_ref, cols_ref, acsT_ref, dec_ref):
        aT = -jnp.exp(alog_ref[...])
        triT = (_iota((Q, Q), 0) <= _iota((Q, Q), 1)).astype(F32)
        for j in range(chunks):
            at = slice(Q * j, Q * (j + 1))
            dtT = _softplus(dtT_ref[:, at] + dtb_ref[...])
            acsT = _dot_hi(dtT * aT, triT)
            lastT = acsT[:, Q - 1:Q]
            acsT_ref[:, at] = acsT
            dec_ref[j] = jnp.exp(lastT)
            parts = [dtT, acsT, jnp.exp(lastT - acsT), jnp.exp(acsT)]
            colsT = jnp.concatenate([q[HPG * g:HPG * (g + 1)] for g in range(G) for q in parts], axis=0).T
            for g in range(G):
                cols_ref[g, at, :] = colsT[:, 4 * HPG * g:4 * HPG * (g + 1)]

    sds = jax.ShapeDtypeStruct
    return pl.pallas_call(
        body, grid=(T // span,),
        in_specs=[pl.BlockSpec((H, span), lambda i: (0, i)), pl.BlockSpec((H, 1), lambda i: (0, 0)),
                  pl.BlockSpec((H, 1), lambda i: (0, 0))],
        out_specs=[pl.BlockSpec((G, span, 4 * HPG), lambda i: (0, i, 0)), pl.BlockSpec((H, span), lambda i: (0, i)),
                   pl.BlockSpec((chunks, H, 1), lambda i: (i, 0, 0))],
        out_shape=[sds((G, T, 4 * HPG), F32), sds((H, T), F32), sds((T // Q, H, 1), F32)],
        compiler_params=_cparams("parallel"), name=name)(dt_rawT, dt_bias, a_log)


CONV_XS, CONV_BM, CONV_CM = slice(0, SSM_GW), slice(SSM_GW, SSM_GW + SSM_STATE), slice(SSM_GW + SSM_STATE, SSM_GW + 2 * SSM_STATE)


def _ssd_prologue(first, xs_ref, bm_ref, cm_ref, xs_halo, bm_halo, cm_halo, conv_ref, cols_ref, acsT_ref, dec_ref,
                  xp_xs, xp_bm, xp_cm):
    cp = conv_ref[0]
    taps = {n: (cp[:, at], cp[SSM_CONV:SSM_CONV + 1, at]) for n, at in (("xs", CONV_XS), ("bm", CONV_BM), ("cm", CONV_CM))}
    pre_xs = _conv_pre(xs_ref, xs_halo, *taps["xs"], xp_xs, first)
    pre_bm = _conv_pre(bm_ref, bm_halo, *taps["bm"], xp_bm, first)
    pre_cm = _conv_pre(cm_ref, cm_halo, *taps["cm"], xp_cm, first)
    return dict(pre_xs=pre_xs, pre_bm=pre_bm, pre_cm=pre_cm, xs=pre_xs * _sigmoid(pre_xs), bm=pre_bm * _sigmoid(pre_bm),
                cm=pre_cm * _sigmoid(pre_cm), taps=taps, acsT=acsT_ref[...], decayT=dec_ref[0], cols=cols_ref[0])


def _pair_cols(cols, base, p, lo):
    k = base + 2 * p
    return jnp.where(lo, cols[:, k:k + 1], cols[:, k + 1:k + 2])


def _pair_row(colT, p, lo_row):
    return jnp.where(lo_row, colT[2 * p:2 * p + 1, :], colT[2 * p + 1:2 * p + 2, :])


def _pair_operands(pp, p, s, causal, lo, xb):
    zero = jnp.zeros_like(xb)
    rhs = jnp.concatenate([jnp.where(lo, xb, zero), jnp.where(lo, zero, xb)], axis=0)
    ls, ms = [], []
    for k in (2 * p, 2 * p + 1):
        seg = pp["cols"][:, 8 + k:9 + k] - pp["acsT"][k:k + 1, :]
        l = jnp.exp(jnp.where(causal, seg, -1e30))
        ls.append(l)
        ms.append(s * l)
    lhs = jnp.concatenate([m.astype(BF16) for m in ms], axis=1)
    return lhs, rhs, ls


def ssd_fwd(xbc, z, scalars, conv_pack, vec_pack, *, batch, name):
    T = xbc.shape[0]
    Q, GW, N = SSM_CHUNK, SSM_GW, SSM_STATE
    nc = T // batch // Q
    sp = _ssd_specs(nc)(lambda c: c)

    def body(xs_ref, bm_ref, cm_ref, xs_halo, bm_halo, cm_halo, z_ref, conv_ref, cols_ref, acsT_ref, dec_ref, vec_ref,
             y_ref, ys_ref, st_ref, state, xp_xs, xp_bm, xp_cm):
        c = pl.program_id(1)
        g = pl.program_id(2)
        first = c == 0
        pp = _ssd_prologue(first, xs_ref, bm_ref, cm_ref, xs_halo, bm_halo, cm_halo, conv_ref, cols_ref, acsT_ref,
                           dec_ref, xp_xs, xp_bm, xp_cm)
        dsk_ref, ng_ref = vec_ref.at[0:1], vec_ref.at[1:2]
        xs = pp["xs"]
        bb = pp["bm"].astype(BF16)
        cb = pp["cm"].astype(BF16)
        s = _dot_nt(cb, bb)
        causal = _iota((Q, Q), 0) >= _iota((Q, Q), 1)
        lo = _iota((Q, PW), 1) < SSM_HEAD_DIM
        lo_row = _iota((1, PW), 1) < SSM_HEAD_DIM
        ys = []

        @pl.when(first)
        def _():
            state[g] = jnp.zeros((PAIRS, N, PW), F32)

        entering = state[g]
        st_ref[0] = entering
        leaving = []
        for p in range(PAIRS):
            tile = slice(PW * p, PW * (p + 1))
            xs_p = xs[:, tile]
            x_p = xs_p * _pair_cols(pp["cols"], 0, p, lo)
            lhs, rhs, _ = _pair_operands(pp, p, s, causal, lo, x_p.astype(BF16))
            hp = entering[p]
            ys.append(_dot(lhs, rhs) + _dot(cb, hp.astype(BF16)) * _pair_cols(pp["cols"], 24, p, lo)
                      + dsk_ref[:, tile] * xs_p)
            xw = (x_p * _pair_cols(pp["cols"], 16, p, lo)).astype(BF16)
            leaving.append(hp * _pair_row(pp["decayT"], p, lo_row) + _dot_tn(bb, xw))
        state[g] = jnp.stack(leaving)
        y = jnp.concatenate(ys, axis=1)
        y_ref[...] = y
        zz = z_ref[...]
        yg = y * (zz * _sigmoid(zz))
        rr = lax.rsqrt(jnp.mean(yg * yg, axis=-1, keepdims=True) + RMS_EPS)
        ys_ref[...] = (yg * rr * ng_ref[...]).astype(BF16)

    sds = jax.ShapeDtypeStruct
    return pl.pallas_call(
        body, grid=(batch, nc, SSM_GROUPS),
        in_specs=[sp["xs"], sp["bm"], sp["cm"], sp["xs_halo"], sp["bm_halo"], sp["cm_halo"], sp["grp"],
                  sp["conv"], sp["cols"], sp["dtT_g"], sp["decay"], sp["vec2"]],
        out_specs=[sp["grp"], sp["grp"], sp["pairs"]],
        out_shape=[sds((T, SSM_D_INNER), F32), sds((T, SSM_D_INNER), BF16),
                   sds((T // Q, SSM_GROUPS * PAIRS, N, PW), F32)],
        scratch_shapes=[pltpu.VMEM((SSM_GROUPS, PAIRS, N, PW), F32),
                        pltpu.VMEM((HALO + Q, GW), F32), pltpu.VMEM((HALO + Q, N), F32), pltpu.VMEM((HALO + Q, N), F32)],
        compiler_params=_cparams("arbitrary", "arbitrary", "arbitrary"), name=name,
    )(xbc, xbc, xbc, xbc, xbc, xbc, z, conv_pack, *scalars, vec_pack)


def ssd_bwd(dys, y, xbc, z, dt_rawT, states, scalars, conv_pack, col_pack, vec_pack, *, batch, name, ride=None):
    T = xbc.shape[0]
    Q, GW, N, G = SSM_CHUNK, SSM_GW, SSM_STATE, SSM_GROUPS
    nc = T // batch // Q
    sp = _ssd_specs(nc)(lambda c: nc - 1 - c)

    def body(xs_ref, bm_ref, cm_ref, xs_halo, bm_halo, cm_halo, z_ref, y_ref, dys_ref, dtT_ref, st_ref,
             conv_ref, cols_ref, acsT_ref, dec_ref, col_ref, vec_ref,
             dz_ref, dxs_ref, dbm_ref, dcm_ref, ddtT_ref, acc_xs, acc_bm, acc_cm, acc_head,
             dstate, xp_xs, xp_bm, xp_cm, dp_xs, dp_bm, dp_cm, cy_xs, cy_bm, cy_cm):
        b = pl.program_id(0)
        cr = pl.program_id(1)
        g = pl.program_id(2)
        first = cr == nc - 1
        last = cr == 0

        @pl.when(jnp.logical_and(jnp.logical_and(b == 0, cr == 0), g == 0))
        def _():
            acc_xs[...] = jnp.zeros_like(acc_xs)
            acc_bm[...] = jnp.zeros_like(acc_bm)
            acc_cm[...] = jnp.zeros_like(acc_cm)
            acc_head[...] = jnp.zeros_like(acc_head)

        @pl.when(last)
        def _():
            dstate[g] = jnp.zeros((PAIRS, N, PW), F32)
            cy_xs[g] = jnp.zeros((HALO, GW), F32)
            cy_bm[g] = jnp.zeros((HALO, N), F32)
            cy_cm[g] = jnp.zeros((HALO, N), F32)

        pp = _ssd_prologue(first, xs_ref, bm_ref, cm_ref, xs_halo, bm_halo, cm_halo, conv_ref, cols_ref, acsT_ref,
                           dec_ref, xp_xs, xp_bm, xp_cm)
        xs, decayT = pp["xs"], pp["decayT"]
        dsk_ref, ng_ref = vec_ref.at[0:1], vec_ref.at[1:2]
        dtrT = dtT_ref[...] + col_ref[:, 0:1]
        dtT = _softplus(dtrT)
        aT = -jnp.exp(col_ref[:, 1:2])

        yv = y_ref[...]
        zz = z_ref[...]
        sz = _sigmoid(zz)
        silu_z = zz * sz
        yg = yv * silu_z
        rr = lax.rsqrt(jnp.mean(yg * yg, axis=-1, keepdims=True) + RMS_EPS)
        dys_v = dys_ref[...]
        d_ng = jnp.sum(dys_v * yg * rr, axis=0, keepdims=True)
        t = dys_v * ng_ref[...]
        dyg = rr * t - yg * (rr * rr * rr * jnp.mean(t * yg, axis=-1, keepdims=True))
        dy = dyg * silu_z
        dz_ref[...] = (dyg * yv * (sz * (1.0 + zz * (1.0 - sz)))).astype(BF16)
        dsk = dsk_ref[...]
        d_dsk = jnp.sum(dy * xs, axis=0, keepdims=True)

        bb = pp["bm"].astype(BF16)
        cb = pp["cm"].astype(BF16)
        s = _dot_nt(cb, bb)
        causal = _iota((Q, Q), 0) >= _iota((Q, Q), 1)
        lo = _iota((Q, PW), 1) < SSM_HEAD_DIM
        lo_row = _iota((1, PW), 1) < SSM_HEAD_DIM
        sub8 = _iota((SSM_HPG, 1), 0)
        ds_acc = jnp.zeros((Q, Q), F32)
        d_c = jnp.zeros((Q, N), F32)
        d_b = jnp.zeros((Q, N), F32)
        last_terms = jnp.zeros((SSM_HPG, 1), F32)
        q1, q2, dxs = [], [], []
        d_leaving = dstate[g]
        d_entering = []
        for p in range(PAIRS):
            tile = slice(PW * p, PW * (p + 1))
            dt_p = _pair_cols(pp["cols"], 0, p, lo)
            w_p = _pair_cols(pp["cols"], 16, p, lo)
            e_p = _pair_cols(pp["cols"], 24, p, lo)
            xs_p = xs[:, tile]
            x_p = xs_p * dt_p
            xw_p = x_p * w_p
            lhs, rhs, ls = _pair_operands(pp, p, s, causal, lo, x_p.astype(BF16))
            dy_p = dy[:, tile]
            dyb = dy_p.astype(BF16)
            hp = st_ref[0, p]
            hpb = hp.astype(BF16)
            dh = d_leaving[p]
            dhb = dh.astype(BF16)
            dye = (dy_p * e_p).astype(BF16)
            d_c = d_c + _dot_nt(dye, hpb)
            dm = _dot_nt(dyb, rhs)
            dxd2 = _dot_tn(lhs, dyb)
            dxd = jnp.where(lo, dxd2[0:Q], dxd2[Q:2 * Q])
            ds_acc = ds_acc + dm[:, 0:Q] * ls[0] + dm[:, Q:2 * Q] * ls[1]
            dxw = _dot(bb, dhb)
            d_b = d_b + _dot_nt(xw_p.astype(BF16), dhb)
            dx_full = dxd + dxw * w_p
            tw = dxw * xw_p
            yd = _dot(lhs, rhs)
            yoff = _dot(cb, hpb) * e_p
            q1.append(dyb.astype(F32) * yd + dy_p * yoff - tw - x_p.astype(BF16).astype(F32) * dxd)
            q2.append(dx_full * xs_p)
            dxs.append(dsk[:, tile] * dy_p + dx_full * dt_p)
            row = jnp.sum(dh * hp, axis=0, keepdims=True) * _pair_row(decayT, p, lo_row) + jnp.sum(tw, axis=0, keepdims=True)
            t_lo = jnp.sum(jnp.where(lo_row, row, 0.0), axis=1, keepdims=True)
            t_hi = jnp.sum(jnp.where(lo_row, 0.0, row), axis=1, keepdims=True)
            last_terms = last_terms + jnp.where(sub8 == 2 * p, t_lo, 0.0) + jnp.where(sub8 == 2 * p + 1, t_hi, 0.0)
            d_entering.append(dh * _pair_row(decayT, p, lo_row) + _dot_tn(cb, dye))
        dstate[g] = jnp.stack(d_entering)
        dsb = ds_acc.astype(BF16)
        d_c = d_c + _dot(dsb, bb)
        d_b = d_b + _dot_tn(dsb, cb)
        e8 = (_iota((SSM_HPG, GW), 0) == lax.shift_right_logical(_iota((SSM_HPG, GW), 1), 6)).astype(F32)
        seg_sum = lambda tiles: lax.dot_general(e8, jnp.concatenate(tiles, axis=1), (((1,), (1,)), ((), ())),
                                                preferred_element_type=F32, precision=lax.Precision.HIGHEST)
        dacsT = seg_sum(q1) + jnp.where(_iota((1, Q), 1) == Q - 1, 1.0, 0.0) * last_terms
        tri = (_iota((Q, Q), 0) >= _iota((Q, Q), 1)).astype(F32)
        d_dtaT = _dot_hi(dacsT, tri)
        ddtT = d_dtaT * aT + seg_sum(q2)
        d_alog = jnp.sum(d_dtaT * dtT, axis=1, keepdims=True) * aT
        ddt_rawT = ddtT * _sigmoid(dtrT)
        ddtT_ref[...] = ddt_rawT
        d_dtb = jnp.sum(ddt_rawT, axis=1, keepdims=True)
        lane = _iota((SSM_HPG, N), 1)
        acc_head[g] += jnp.where(lane == 0, d_dtb, 0.0) + jnp.where(lane == 1, d_alog, 0.0)
        dxs_v = jnp.concatenate(dxs, axis=1)
        taps = pp["taps"]
        dxs_ref[...] = _conv_bwd(dxs_v, pp["pre_xs"], xp_xs, taps["xs"][0], cy_xs, acc_xs, dp_xs, g, last).astype(BF16)
        dbm_ref[...] = _conv_bwd(d_b, pp["pre_bm"], xp_bm, taps["bm"][0], cy_bm, acc_bm, dp_bm, g, last).astype(BF16)
        dcm_ref[...] = _conv_bwd(d_c, pp["pre_cm"], xp_cm, taps["cm"][0], cy_cm, acc_cm, dp_cm, g, last).astype(BF16)
        acc_xs[g, pl.ds(SSM_CONV + 1, 2), :] += jnp.concatenate([d_ng, d_dsk], axis=0)

    sds = jax.ShapeDtypeStruct
    row = lambda b, c, g: b * nc + (nc - 1 - c)
    full = lambda shape: pl.BlockSpec(shape, lambda b, c, g: (0,) * len(shape))
    return _call(
        body, grid=(batch, nc, G),
        in_specs=[sp["xs"], sp["bm"], sp["cm"], sp["xs_halo"], sp["bm_halo"], sp["cm_halo"], sp["grp"], sp["grp"],
                  sp["grp"], sp["dtT_g"], sp["pairs"],
                  sp["conv"], sp["cols"], sp["dtT_g"], sp["decay"], sp["col2"], sp["vec2"]],
        args=[xbc, xbc, xbc, xbc, xbc, xbc, z, y, dys, dt_rawT, states, conv_pack, *scalars, col_pack, vec_pack],
        out_specs=[sp["grp"], sp["grp"],
                   pl.BlockSpec((Q, N), lambda b, c, g: (row(b, c, g), g)),
                   pl.BlockSpec((Q, N), lambda b, c, g: (row(b, c, g), g)),
                   sp["dtT_g"], full((G, HALO, GW)), full((G, HALO, N)), full((G, HALO, N)), full((G, SSM_HPG, N))],
        out_shape=[sds((T, SSM_D_INNER), BF16), sds((T, SSM_D_INNER), BF16), sds((T, G * N), BF16),
                   sds((T, G * N), BF16), sds((SSM_HEADS, T), F32),
                   sds((G, HALO, GW), F32), sds((G, HALO, N), F32), sds((G, HALO, N), F32), sds((G, SSM_HPG, N), F32)],
        scratch=[pltpu.VMEM((G, PAIRS, N, PW), F32),
                 pltpu.VMEM((HALO + Q, GW), F32), pltpu.VMEM((HALO + Q, N), F32), pltpu.VMEM((HALO + Q, N), F32),
                 pltpu.VMEM((Q + HALO, GW), F32), pltpu.VMEM((Q + HALO, N), F32), pltpu.VMEM((Q + HALO, N), F32),
                 pltpu.VMEM((G, HALO, GW), F32), pltpu.VMEM((G, HALO, N), F32), pltpu.VMEM((G, HALO, N), F32)],
        sem=("arbitrary", "arbitrary", "arbitrary"), name=name, ride=ride)


def mm_rows(a, b, *, name, tt=2048):
    M, T = a.shape
    N = b.shape[1]
    tt = min(tt, T)

    def body(a_ref, b_ref, o_ref):
        @pl.when(pl.program_id(0) == 0)
        def _():
            o_ref[...] = jnp.zeros_like(o_ref)

        o_ref[...] += _dot(a_ref[...].astype(BF16), b_ref[...])

    return pl.pallas_call(
        body, grid=(T // tt,),
        in_specs=[pl.BlockSpec((M, tt), lambda t: (0, t)), pl.BlockSpec((tt, N), lambda t: (t, 0))],
        out_specs=pl.BlockSpec((M, N), lambda t: (0, 0)), out_shape=jax.ShapeDtypeStruct((M, N), F32),
        compiler_params=_cparams("arbitrary"), name=name)(a, b)


MESH = pl.DeviceIdType.MESH
ANY = pl.BlockSpec(memory_space=pl.ANY)
ROW_ALIGN = 16


def _me():
    return lax.axis_index("x"), lax.axis_index("y"), lax.axis_index("c")


def _other_chips(x, y):
    return [(1 - x, y), (x, 1 - y), (1 - x, 1 - y)]


def _remote(src, dst, send_sem, recv_sem, to):
    return pltpu.make_async_remote_copy(src_ref=src, dst_ref=dst, send_sem=send_sem, recv_sem=recv_sem,
                                        device_id=to, device_id_type=MESH)


def _half(c, rows):
    return pl.ds(pl.multiple_of(c * (rows // 2), ROW_ALIGN), rows // 2)


def ag_ride(bufs):
    n = len(bufs)

    def copies(outs, sems):
        ici_send, ici_recv, d2d_send, d2d_recv = sems
        x, y, c = _me()
        sib = (x, y, 1 - c)
        ici, d2d, d2d_in = [], [], []
        for i in range(n):
            rows = outs[i].shape[1]
            mine = outs[i].at[2 * x + y, _half(c, rows)]
            for j, chip in enumerate(_other_chips(x, y)):
                ici.append(_remote(mine, mine, ici_send.at[i, j], ici_recv.at[i, j], (*chip, c)))
                landed = outs[i].at[2 * chip[0] + chip[1], _half(c, rows)]
                d2d.append((_remote(landed, landed, ici_send.at[i, j], ici_recv.at[i, j], (*chip, c)),
                            _remote(landed, landed, d2d_send.at[i, j], d2d_recv.at[i, j], sib)))
                lands = outs[i].at[2 * chip[0] + chip[1], _half(1 - c, rows)]
                d2d_in.append(_remote(lands, lands, d2d_send.at[i, j], d2d_recv.at[i, j], sib))
        return ici, d2d, d2d_in

    def start(ins, outs, sems):
        for cp in copies(outs, sems)[0]:
            cp.start()

    def finish(ins, outs, sems):
        ici, d2d, d2d_in = copies(outs, sems)
        for arrived, forward in d2d:
            arrived.wait_recv()
            forward.start()
        for cp in d2d_in:
            cp.wait_recv()
        for cp in ici + [forward for _, forward in d2d]:
            cp.wait_send()

    return Ride(bufs, [jax.ShapeDtypeStruct(b.shape, b.dtype) for b in bufs], [(i, i) for i in range(n)],
                [pltpu.SemaphoreType.DMA((n, 3))] * 4, start, finish)


def pair_ride(grads):
    n = len(grads)

    def copies(ins, outs, sems):
        x, y, c = _me()
        return [_remote(ins[i].at[:, _half(1 - c, ins[i].shape[1]), :], outs[i], sems[0].at[i], sems[1].at[i], (x, y, 1 - c))
                for i in range(n)]

    def start(ins, outs, sems):
        for cp in copies(ins, outs, sems):
            cp.start()

    def finish(ins, outs, sems):
        for cp in copies(ins, outs, sems):
            cp.wait()

    return Ride(grads, [jax.ShapeDtypeStruct((N_SHARD, g.shape[1] // 2, g.shape[2]), g.dtype) for g in grads], [],
                [pltpu.SemaphoreType.DMA((n,))] * 2, start, finish)


def rs_add(grad, part, c, *, rt, name):
    _, rows, cols = grad.shape
    r2 = rows // 2
    nrb = r2 // rt

    def body(c_ref, g_ref, p_ref, o_ref):
        o_ref[...] = (g_ref[...] + p_ref[...]).astype(BF16)

    return pl.pallas_call(
        body,
        grid_spec=pltpu.PrefetchScalarGridSpec(
            num_scalar_prefetch=1, grid=(N_SHARD, nrb),
            in_specs=[pl.BlockSpec((1, rt, cols), lambda k, i, c_ref: (k, c_ref[1] * nrb + i, 0)),
                      pl.BlockSpec((1, rt, cols), lambda k, i, c_ref: (k, i, 0))],
            out_specs=pl.BlockSpec((1, rt, cols), lambda k, i, c_ref: (k, i, 0))),
        out_shape=jax.ShapeDtypeStruct((N_SHARD, r2, cols), BF16),
        compiler_params=_cparams("parallel", "parallel"), name=name)(c, grad, part)


def chips_ride(sums):
    n = len(sums)

    def copies(ins, outs, sems):
        send, recv = sems
        x, y, c = _me()
        return [_remote(ins[i].at[2 * chip[0] + chip[1]], outs[i].at[2 * x + y], send.at[i, j], recv.at[i, j], (*chip, c))
                for i in range(n) for j, chip in enumerate(_other_chips(x, y))]

    def start(ins, outs, sems):
        for cp in copies(ins, outs, sems):
            cp.start()

    def finish(ins, outs, sems):
        for cp in copies(ins, outs, sems):
            cp.wait()

    return Ride(sums, [jax.ShapeDtypeStruct(s.shape, s.dtype) for s in sums], [],
                [pltpu.SemaphoreType.DMA((n, 3))] * 2, start, finish)


def rs_total(parts, own, where, *, rt, name):
    _, r2, cols = parts.shape
    nrb = r2 // rt

    def body(w_ref, p0, p1, p2, p3, own_ref, o_ref):
        s_me = w_ref[0]
        acc = None
        for k, p in enumerate((p0, p1, p2, p3)):
            term = jnp.where(s_me == k, own_ref[0], p[0]).astype(F32)
            acc = term if acc is None else acc + term
        o_ref[...] = acc

    def slot(k):
        return pl.BlockSpec((1, rt, cols), lambda i, w: (jnp.where(w[0] == k, (k + 1) % N_SHARD, k), i, 0))

    return pl.pallas_call(
        body,
        grid_spec=pltpu.PrefetchScalarGridSpec(
            num_scalar_prefetch=1, grid=(nrb,),
            in_specs=[slot(0), slot(1), slot(2), slot(3), pl.BlockSpec((1, rt, cols), lambda i, w: (w[0], i, 0))],
            out_specs=pl.BlockSpec((rt, cols), lambda i, w: (w[1] * nrb + i, 0))),
        out_shape=jax.ShapeDtypeStruct((2 * r2, cols), F32),
        compiler_params=_cparams("parallel"), name=name)(where, parts, parts, parts, parts, own)


def share_ride(totals):
    n = len(totals)

    def halves(outs, sems):
        x, y, c = _me()
        mine = [outs[i].at[_half(c, outs[i].shape[0])] for i in range(n)]
        other = [outs[i].at[_half(1 - c, outs[i].shape[0])] for i in range(n)]
        return ([_remote(m, m, sems[0].at[i], sems[1].at[i], (x, y, 1 - c)) for i, m in enumerate(mine)],
                [_remote(o, o, sems[0].at[i], sems[1].at[i], (x, y, 1 - c)) for i, o in enumerate(other)])

    def start(ins, outs, sems):
        for cp in halves(outs, sems)[0]:
            cp.start()

    def finish(ins, outs, sems):
        sent, landing = halves(outs, sems)
        for cp in landing:
            cp.wait_recv()
        for cp in sent:
            cp.wait_send()

    return Ride(totals, [jax.ShapeDtypeStruct(t.shape, t.dtype) for t in totals], [(i, i) for i in range(n)],
                [pltpu.SemaphoreType.DMA((n,))] * 2, start, finish)


def small_allreduce(buf, *, name):
    rows = buf.shape[0]

    def body(x_ref, o_ref, slots, send, recv):
        x, y, c = _me()
        me = 4 * x + 2 * y + c
        slots[me] = x_ref[...]
        sent = []
        for d in range(1, 8):
            peer = (1 - x if d & 4 else x, 1 - y if d & 2 else y, 1 - c if d & 1 else c)
            sent.append(_remote(x_ref, slots.at[me], send.at[d - 1], recv.at[d - 1], peer))
            sent[-1].start()
        for cp in sent:
            cp.wait()
        acc = slots[0]
        for k in range(1, 8):
            acc = acc + slots[k]
        o_ref[...] = acc

    return pl.pallas_call(
        body, out_shape=jax.ShapeDtypeStruct(buf.shape, F32),
        in_specs=[pl.BlockSpec(memory_space=pltpu.VMEM)], out_specs=pl.BlockSpec(memory_space=pltpu.VMEM),
        scratch_shapes=[pltpu.VMEM((8, rows, 128), F32), pltpu.SemaphoreType.DMA((7,)), pltpu.SemaphoreType.DMA((7,))],
        name=name)(buf)


def adamw(w, g, m, v, *, name, rt=None):
    rows, cols = w.shape
    rt = rows if rt is None else rt
    c1 = 1.0 - ADAM_B1 ** ADAM_STEP
    c2 = 1.0 - ADAM_B2 ** ADAM_STEP

    def body(w_ref, g_ref, m_ref, v_ref, d_ref, nm_ref, nv_ref):
        gg = g_ref[...]
        nm = ADAM_B1 * m_ref[...] + (1.0 - ADAM_B1) * gg
        nv = ADAM_B2 * v_ref[...] + (1.0 - ADAM_B2) * (gg * gg)
        nm_ref[...] = nm
        nv_ref[...] = nv
        d_ref[...] = -ADAM_LR * ((nm / c1) / (jnp.sqrt(nv / c2) + ADAM_EPS) + ADAM_WD * w_ref[...])

    spec = pl.BlockSpec((rt, cols), lambda i: (i, 0))
    return pl.pallas_call(
        body, grid=(rows // rt,), in_specs=[spec] * 4, out_specs=[spec] * 3,
        out_shape=[jax.ShapeDtypeStruct((rows, cols), F32)] * 3,
        compiler_params=_cparams("parallel"), name=name)(w, g, m, v)


WEIGHTS = ['ffn1_pre_g', 'ffn1_w_gate', 'ffn1_w_up', 'ffn1_w_down', 'ffn1_post_g', 'mix_pre_g', 'w_in', 'conv_w',
           'conv_b', 'dt_bias', 'a_log', 'd_skip', 'ssm_norm_g', 'w_ssm_proj', 'attn_sinks', 'rel_bias_table',
           'w_attn_proj', 'w_out', 'mix_post_g', 'ffn2_pre_g', 'ffn2_w_gate', 'ffn2_w_up', 'ffn2_w_down', 'ffn2_post_g']
BIG = ['ffn1_w_gate', 'ffn1_w_up', 'ffn1_w_down', 'w_in', 'w_ssm_proj', 'w_attn_proj', 'w_out',
       'ffn2_w_gate', 'ffn2_w_up', 'ffn2_w_down']
SMALL = [w for w in WEIGHTS if w not in BIG]


def _bucket_onehot():
    blk = ATTN_BLOCK
    dist = np.maximum(np.arange(blk)[:, None] + blk - np.arange(2 * blk)[None, :], 0)
    max_exact = REL_BUCKETS // 2
    d = np.maximum(dist, 1).astype(np.float32)
    large = max_exact + (np.log(d / np.float32(max_exact)) / np.float32(math.log(REL_MAX_DISTANCE / max_exact))
                         * np.float32(REL_BUCKETS - max_exact)).astype(np.int32)
    bucket = np.where(dist < max_exact, dist, np.minimum(large, REL_BUCKETS - 1)).reshape(-1)
    return jnp.asarray((bucket[None, :] == np.arange(REL_BUCKETS)[:, None]).astype(np.float32))


def _pack_rows(parts, mult=8):
    flat = jnp.concatenate([p.reshape(-1).astype(F32) for p in parts])
    rows = -(-flat.shape[0] // (128 * mult)) * mult
    return jnp.pad(flat, (0, rows * 128 - flat.shape[0])).reshape(rows, 128)


def _unpack_rows(buf, shapes):
    flat = buf.reshape(-1)
    out, at = [], 0
    for shp in shapes:
        size = int(np.prod(shp))
        out.append(flat[at:at + size].reshape(shp))
        at += size
    return out


def kernel(x, ffn1_pre_g, ffn1_w_gate, ffn1_w_up, ffn1_w_down, ffn1_post_g, mix_pre_g, w_in, conv_w, conv_b, dt_bias, a_log, d_skip, ssm_norm_g, w_ssm_proj, attn_sinks, rel_bias_table, w_attn_proj, w_out, mix_post_g, ffn2_pre_g, ffn2_w_gate, ffn2_w_up, ffn2_w_down, ffn2_post_g, loss_target, m_ffn1_pre_g, m_ffn1_w_gate, m_ffn1_w_up, m_ffn1_w_down, m_ffn1_post_g, m_mix_pre_g, m_w_in, m_conv_w, m_conv_b, m_dt_bias, m_a_log, m_d_skip, m_ssm_norm_g, m_w_ssm_proj, m_attn_sinks, m_rel_bias_table, m_w_attn_proj, m_w_out, m_mix_post_g, m_ffn2_pre_g, m_ffn2_w_gate, m_ffn2_w_up, m_ffn2_w_down, m_ffn2_post_g, v_ffn1_pre_g, v_ffn1_w_gate, v_ffn1_w_up, v_ffn1_w_down, v_ffn1_post_g, v_mix_pre_g, v_w_in, v_conv_w, v_conv_b, v_dt_bias, v_a_log, v_d_skip, v_ssm_norm_g, v_w_ssm_proj, v_attn_sinks, v_rel_bias_table, v_w_attn_proj, v_w_out, v_mix_post_g, v_ffn2_pre_g, v_ffn2_w_gate, v_ffn2_w_up, v_ffn2_w_down, v_ffn2_post_g):
    args = locals()
    w = {n: args[n] for n in WEIGHTS}
    m = {n: args["m_" + n] for n in WEIGHTS}
    v = {n: args["v_" + n] for n in WEIGHTS}
    batch, seq, D = x.shape
    T = batch * seq
    xi, yi, ci = _me()
    s_me = 2 * xi + yi
    x2 = x.reshape(T, D)
    tgt = loss_target.reshape(T, D)

    def own_slot(parts):
        p = jnp.concatenate([t[0] for t in parts], axis=0).astype(BF16)
        return lax.dynamic_update_slice(lax.empty((N_SHARD,) + p.shape, BF16), p[None], (s_me, 0, 0))

    tr = lambda a: jnp.swapaxes(a, -1, -2)
    (wffn1,) = run_ride(ag_ride([own_slot([tr(ffn1_w_gate), tr(ffn1_w_up), ffn1_w_down])]), name="ag_ffn1")
    col = lambda v: v.reshape(SSM_HEADS, 1)
    d_skip_x = jnp.repeat(d_skip, SSM_HEAD_DIM, axis=1)
    cw_slot = lax.dynamic_update_slice(jnp.zeros((SSM_CONV, SSM_CONV_DIM), F32),
                                       conv_w[0] * (ci == 0).astype(F32), (0, s_me * (SSM_CONV_DIM // N_SHARD)))
    conv_w_full = small_allreduce(cw_slot.reshape(-1, 128), name="ag_conv_w").reshape(SSM_CONV, SSM_CONV_DIM)
    cwb = jnp.concatenate([conv_w_full, conv_b, jnp.zeros((HALO - SSM_CONV - 1, SSM_CONV_DIM), F32)], axis=0)
    conv_pack = jnp.stack([jnp.concatenate([cwb[:, SSM_GW * g:SSM_GW * (g + 1)],
                                            cwb[:, SSM_D_INNER + SSM_STATE * g:SSM_D_INNER + SSM_STATE * (g + 1)],
                                            cwb[:, SSM_D_INNER + SSM_STATE * (SSM_GROUPS + g):
                                                SSM_D_INNER + SSM_STATE * (SSM_GROUPS + g + 1)]], axis=1)
                           for g in range(SSM_GROUPS)])
    vec_pack = jnp.concatenate([d_skip_x, ssm_norm_g], axis=0)
    col_pack = jnp.concatenate([col(dt_bias), col(a_log)], axis=1)

    (h1, n1, gate1, up1, f1), (gin, gmix) = ffn_fwd(
        x2, ffn1_pre_g, wffn1, ffn1_post_g, name="ffn1_fwd",
        ride=ag_ride([own_slot([w_in]), own_slot([w_ssm_proj, w_attn_proj, w_out])]))
    w_in_full = gin.transpose(1, 0, 2).reshape(D, IN_COLS)
    w_gz = w_in_full[:, 0:4096]
    w_xbc = w_in_full[:, 4096:4096 + SSM_CONV_DIM]
    w_dtT = w_in_full[:, 7168:7200].T
    w_qkv = w_in_full[:, 7200:]
    (u, gates, z, xbc, dt_rawT, q, k, vv), (wffn2,) = mix_in_fwd(
        h1, mix_pre_g, w_gz, w_xbc, w_dtT, w_qkv, name="mix_in_fwd",
        ride=ag_ride([own_slot([tr(ffn2_w_gate), tr(ffn2_w_up), ffn2_w_down])]))
    scalars = ssd_scalars(dt_rawT, col(dt_bias), col(a_log), name="ssd_scalars")
    y, ys, states = ssd_fwd(xbc, z, scalars, conv_pack, vec_pack, batch=batch, name="ssd_fwd")
    onehot = _bucket_onehot()
    bias = attn_bias(rel_bias_table.T, onehot, name="attn_bias").reshape(ATTN_Q_HEADS, ATTN_BLOCK, 2 * ATTN_BLOCK)
    o, lse = attn_fwd(q, k, vv, bias, attn_sinks, batch=batch, name="attn_fwd")
    h2, y_ssm, y_attn, mix, merged = mix_out_fwd(ys, o, gates, h1, gmix, mix_post_g, name="mix_out_fwd")
    h3, n3, gate2, up2, f2, dy, loss_parts = ffn_fwd(h2, ffn2_pre_g, wffn2, ffn2_post_g, tgt, name="ffn2_fwd")

    where = jnp.stack([s_me, ci]).astype(jnp.int32)

    def chip_sums(grads, pair, tiles, tag):
        return [rs_add(g, p, where, rt=rt, name=f"rs_add_{tag}{i}") for i, (g, p, rt) in enumerate(zip(grads, pair, tiles))]

    def totals(parts, sums, tiles, tag):
        return [rs_total(p, s, where, rt=rt, name=f"rs_total_{tag}{i}")
                for i, (p, s, rt) in enumerate(zip(parts, sums, tiles))]

    def ffn_grads(n, dgate, dup, a, df, tag):
        d = mm_tn(dgate, n[None], into=(lax.empty(wffn1.shape, F32), 0), name="dw_gate" + tag)
        d = mm_tn(dup, n[None], into=(d, 1), name="dw_up" + tag)
        return [mm_tn(a, df[None], into=(d, 2), name="dw_down" + tag)]

    ffn_tiles, mix_tiles = [352], [256, 256]
    dh2, df2, a2, dgate2, dup2, dg_ffn2_pre, dg_ffn2_post = ffn_bwd(dy, h2, f2, gate2, up2, ffn2_pre_g, ffn2_post_g,
                                                                    wffn2, name="ffn2_bwd")
    d_f2 = ffn_grads(n3, dgate2, dup2, a2, df2, "2")
    (dmix, dyssm, dyattn, dgates, dys, do, dg_mix_post), pair_f2 = mix_out_bwd(
        dh2, mix, y_ssm, y_attn, gates, gmix, mix_post_g, name="mix_out_bwd", ride=pair_ride(d_f2))
    sums_f2 = chip_sums(d_f2, pair_f2, ffn_tiles, "f2")
    dq, dk, dv, dbias, dsinks = attn_bwd(q, k, vv, o, do, lse, bias, attn_sinks, batch=batch, name="attn_bwd")
    dtable = attn_bias_bwd(dbias.reshape(ATTN_Q_HEADS, -1), onehot, name="attn_bias_bwd").T
    (dz, dxs, dbm, dcm, ddtT, acc_xs, acc_bm, acc_cm, acc_head), parts_f2 = ssd_bwd(
        dys, y, xbc, z, dt_rawT, states, scalars, conv_pack, col_pack, vec_pack,
        batch=batch, name="ssd_bwd", ride=chips_ride(sums_f2))
    tot_f2 = totals(parts_f2, sums_f2, ffn_tiles, "f2")
    dmx = mm_tn(ys[None], dyssm[None], a_cols=(N_SHARD, 512), into=(lax.empty(gmix.shape, F32), 0), name="dw_ssm")
    dmx = mm_tn(o[None], dyattn[None], a_cols=(N_SHARD, 256), into=(dmx, 2), name="dw_attn")
    dmx = mm_tn(merged[None], dmix[None], a_cols=(N_SHARD, 256), into=(dmx, 3), name="dw_out")
    ub = u[None]
    din = jnp.concatenate([
        mm_tn(ub, dgates[None], name="dw_in_gates", tn=1024)[0], mm_tn(ub, dz[None], name="dw_in_z", tn=1024)[0],
        mm_tn(ub, dxs[None], name="dw_in_xs", tn=1024)[0], mm_tn(ub, dbm[None], name="dw_in_b")[0],
        mm_tn(ub, dcm[None], name="dw_in_c")[0], mm_rows(ddtT, u, name="dw_in_dt").T,
        mm_tn(ub, dq[None], name="dw_in_q")[0], mm_tn(ub, dk[None], name="dw_in_k")[0],
        mm_tn(ub, dv[None], name="dw_in_v")[0]], axis=1)
    din = din.reshape(D, N_SHARD, IN_COLS // N_SHARD).transpose(1, 0, 2)
    d_mx = [dmx, din]
    (dh1, dg_mix_pre), (pair_mx0, pair_mx1, rffn2) = mix_in_bwd(
        dh2, h1, mix_pre_g, dgates, dz, dxs, dbm, dcm, ddtT, dq, dk, dv, w_gz, w_xbc, w_dtT, w_qkv, name="mix_in_bwd",
        ride=join_rides(pair_ride(d_mx), share_ride(tot_f2)))
    sums_mx = chip_sums(d_mx, [pair_mx0, pair_mx1], mix_tiles, "mx")
    (dx, df1, a1, dgate1, dup1, dg_ffn1_pre, dg_ffn1_post), parts_mx = ffn_bwd(
        dh1, x2, f1, gate1, up1, ffn1_pre_g, ffn1_post_g, wffn1, name="ffn1_bwd", ride=chips_ride(sums_mx))
    rmx, rin = run_ride(share_ride(totals(parts_mx, sums_mx, mix_tiles, "mx")), name="rs_share_mx")
    d_f1 = ffn_grads(n1, dgate1, dup1, a1, df1, "1")
    sums_f1 = chip_sums(d_f1, run_ride(pair_ride(d_f1), name="rs_pair_f1"), ffn_tiles, "f1")
    parts_f1 = run_ride(chips_ride(sums_f1), name="rs_chips_f1")
    (rffn1,) = run_ride(share_ride(totals(parts_f1, sums_f1, ffn_tiles, "f1")), name="rs_share_f1")
    FS = D_FF // N_SHARD
    gw = {
        'ffn1_w_gate': rffn1[0:FS], 'ffn1_w_up': rffn1[FS:2 * FS], 'ffn1_w_down': rffn1[2 * FS:],
        'ffn2_w_gate': rffn2[0:FS], 'ffn2_w_up': rffn2[FS:2 * FS], 'ffn2_w_down': rffn2[2 * FS:],
        'w_ssm_proj': rmx[0:512], 'w_attn_proj': rmx[512:768], 'w_out': rmx[768:1024], 'w_in': rin,
    }

    dconv_w = jnp.concatenate([acc[:, :SSM_CONV].transpose(1, 0, 2).reshape(SSM_CONV, -1)
                               for acc in (acc_xs, acc_bm, acc_cm)], axis=1)
    dconv_b = jnp.concatenate([acc[:, SSM_CONV].reshape(-1) for acc in (acc_xs, acc_bm, acc_cm)])
    small_local = {
        'ffn1_pre_g': dg_ffn1_pre, 'ffn1_post_g': dg_ffn1_post, 'mix_pre_g': dg_mix_pre, 'conv_w': dconv_w,
        'conv_b': dconv_b, 'dt_bias': acc_head[:, :, 0], 'a_log': acc_head[:, :, 1],
        'd_skip': acc_xs[:, SSM_CONV + 2].reshape(SSM_HEADS, SSM_HEAD_DIM).sum(axis=1),
        'ssm_norm_g': acc_xs[:, SSM_CONV + 1].reshape(-1), 'attn_sinks': dsinks, 'rel_bias_table': dtable,
        'mix_post_g': dg_mix_post, 'ffn2_pre_g': dg_ffn2_pre, 'ffn2_post_g': dg_ffn2_post,
    }
    full_shapes = [(SSM_CONV, SSM_CONV_DIM) if n == 'conv_w' else w[n].shape for n in SMALL]
    packed = _pack_rows([small_local[n] for n in SMALL] + [jnp.sum(loss_parts[:, 0, 0])])
    total = small_allreduce(packed, name="allreduce_small")
    *small_g, loss = _unpack_rows(total, full_shapes + [()])
    for n, g in zip(SMALL, small_g):
        gw[n] = g
    gw['conv_w'] = lax.dynamic_slice(gw['conv_w'], (0, s_me * (SSM_CONV_DIM // N_SHARD)),
                                     (SSM_CONV, SSM_CONV_DIM // N_SHARD))[None]

    delta, new_m, new_v = {}, {}, {}
    for n in BIG:
        lay = tr if n.endswith(('w_gate', 'w_up')) else (lambda a: a)
        d_, m_, v_ = adamw(lay(w[n][0]), gw[n], lay(m[n][0]), lay(v[n][0]), name="adamw_" + n, rt=gw[n].shape[0] // 4)
        gw[n] = lay(gw[n])[None]
        delta[n], new_m[n], new_v[n] = lay(d_)[None], lay(m_)[None], lay(v_)[None]
    shapes = [w[n].shape for n in SMALL]
    outs = adamw(_pack_rows([w[n] for n in SMALL]), _pack_rows([gw[n] for n in SMALL]),
                 _pack_rows([m[n] for n in SMALL]), _pack_rows([v[n] for n in SMALL]), name="adamw_small")
    for res, buf in zip((delta, new_m, new_v), outs):
        for n, val in zip(SMALL, _unpack_rows(buf, shapes)):
            res[n] = val
    return (loss, dx.reshape(batch, seq, D), *[gw[n].reshape(w[n].shape) for n in WEIGHTS],
            *[delta[n] for n in WEIGHTS], *[new_m[n] for n in WEIGHTS], *[new_v[n] for n in WEIGHTS])
```

```python
import functools
import math

import jax
import jax.numpy as jnp
import numpy as np
from jax import lax
from jax.experimental import pallas as pl
from jax.experimental.pallas import tpu as pltpu

F32 = jnp.float32
BF16 = jnp.bfloat16

D_MODEL = 1024
D_FF = 2816
N_SHARD = 4
SSM_D_INNER = 2048
SSM_HEAD_DIM = 64
SSM_HEADS = 32
SSM_GROUPS = 4
SSM_HPG = SSM_HEADS // SSM_GROUPS
SSM_GW = SSM_D_INNER // SSM_GROUPS
SSM_STATE = 128
SSM_CONV = 4
SSM_CHUNK = 128
SSM_CONV_DIM = SSM_D_INNER + 2 * SSM_GROUPS * SSM_STATE
ATTN_Q_HEADS = 16
ATTN_KV_HEADS = 4
ATTN_REP = ATTN_Q_HEADS // ATTN_KV_HEADS
ATTN_HEAD_DIM = 64
ATTN_BLOCK = 128
ATTN_Q_DIM = 1024
ATTN_KV_DIM = 256
REL_BUCKETS = 32
REL_MAX_DISTANCE = 128
RMS_EPS = 1e-6
IN_COLS = 8736
ADAM_LR = 0.001
ADAM_B1 = 0.9
ADAM_B2 = 0.999
ADAM_EPS = 1e-08
ADAM_WD = 0.01
ADAM_STEP = 10
HALO = 8

VMEM_LIMIT = 56 * 1024 * 1024


def _cparams(*sem):
    return pltpu.CompilerParams(dimension_semantics=tuple(sem) if sem else None, vmem_limit_bytes=VMEM_LIMIT)


def _dot(a, b):
    return jnp.dot(a, b, preferred_element_type=F32)


def _dot_nt(a, b):
    return lax.dot_general(a, b, (((1,), (1,)), ((), ())), preferred_element_type=F32)


def _dot_tn(a, b):
    return lax.dot_general(a, b, (((0,), (0,)), ((), ())), preferred_element_type=F32)


def _dot_hi(a, b):
    return jnp.dot(a, b, preferred_element_type=F32, precision=lax.Precision.HIGHEST)


def _sigmoid(x):
    return 0.5 * jnp.tanh(0.5 * x) + 0.5


def _resident(shape, index=None):
    index = (0,) * len(shape) if index is None else tuple(index)
    return pl.BlockSpec(shape, lambda *_: index, pipeline_mode=pl.Buffered(1))


def _part(packed, rows, part):
    return _resident((N_SHARD, rows, packed.shape[2]), (0, part, 0))


def _rows(tm, width):
    return pl.BlockSpec((tm, width), lambda i: (i, 0))


class Ride:
    def __init__(self, inputs, out_shapes, aliases, scratch, start, finish):
        self.inputs, self.out_shapes, self.aliases = list(inputs), list(out_shapes), list(aliases)
        self.scratch, self.start, self.finish = list(scratch), start, finish


def join_rides(*rides):
    def cut(refs, sizes):
        out, at = [], 0
        for n in sizes:
            out.append(refs[at:at + n])
            at += n
        return out

    k_in = [len(r.inputs) for r in rides]
    k_out = [len(r.out_shapes) for r in rides]
    k_scr = [len(r.scratch) for r in rides]

    def each(step):
        def run(ins, outs, sems):
            for r, i, o, s in zip(rides, cut(ins, k_in), cut(outs, k_out), cut(sems, k_scr)):
                getattr(r, step)(i, o, s)
        return run

    aliases = [(sum(k_in[:n]) + i, sum(k_out[:n]) + j) for n, r in enumerate(rides) for i, j in r.aliases]
    return Ride([a for r in rides for a in r.inputs], [s for r in rides for s in r.out_shapes], aliases,
                [s for r in rides for s in r.scratch], each("start"), each("finish"))


def _call(body, *, grid, in_specs, args, out_specs, out_shape, name, sem, scratch=(), aliases=None, ride=None):
    aliases = dict(aliases or {})
    if ride is None:
        return pl.pallas_call(body, grid=grid, in_specs=in_specs, out_specs=out_specs, out_shape=out_shape,
                              scratch_shapes=list(scratch), input_output_aliases=aliases,
                              compiler_params=_cparams(*sem), name=name)(*args)
    n_in, n_out, n_scr = len(in_specs), len(out_specs), len(scratch)
    k_in, k_out = len(ride.inputs), len(ride.out_shapes)

    def riding(*refs):
        ins, refs = refs[:n_in], refs[n_in:]
        ex_in, refs = refs[:k_in], refs[k_in:]
        outs, refs = refs[:n_out], refs[n_out:]
        ex_out, refs = refs[:k_out], refs[k_out:]
        scr, ex_scr = refs[:n_scr], refs[n_scr:]
        first = functools.reduce(jnp.logical_and, [pl.program_id(a) == 0 for a in range(len(grid))])
        last = functools.reduce(jnp.logical_and, [pl.program_id(a) == grid[a] - 1 for a in range(len(grid))])

        @pl.when(first)
        def _():
            ride.start(ex_in, ex_out, ex_scr)

        body(*ins, *outs, *scr)

        @pl.when(last)
        def _():
            ride.finish(ex_in, ex_out, ex_scr)

    aliases.update({n_in + i: n_out + j for i, j in ride.aliases})
    res = pl.pallas_call(
        riding, grid=grid, in_specs=list(in_specs) + [ANY] * k_in, out_specs=list(out_specs) + [ANY] * k_out,
        out_shape=list(out_shape) + ride.out_shapes, scratch_shapes=list(scratch) + ride.scratch,
        input_output_aliases=aliases, compiler_params=_cparams(*["arbitrary"] * len(grid)), name=name,
    )(*args, *ride.inputs)
    return res[:n_out], res[n_out:]


def run_ride(ride, *, name):
    k_in = len(ride.inputs)

    def body(*refs):
        ex_in, ex_out, sems = refs[:k_in], refs[k_in:k_in + len(ride.out_shapes)], refs[k_in + len(ride.out_shapes):]
        ride.start(ex_in, ex_out, sems)
        ride.finish(ex_in, ex_out, sems)

    return pl.pallas_call(body, in_specs=[ANY] * k_in, out_specs=[ANY] * len(ride.out_shapes),
                          out_shape=ride.out_shapes, scratch_shapes=ride.scratch,
                          input_output_aliases=dict(ride.aliases), name=name)(*ride.inputs)


def ffn_fwd(h, g_pre, wffn, g_post, target=None, *, name, tm=512, ride=None):
    T, D = h.shape
    NS, FS = N_SHARD, wffn.shape[1] // 3
    with_loss = target is not None
    nt = T // tm

    def body(*refs):
        if with_loss:
            (h_ref, gpre_ref, wg_ref, wu_ref, wd_ref, gpost_ref, tgt_ref,
             hout_ref, n_ref, gate_ref, up_ref, f_ref, dy_ref, loss_ref) = refs
        else:
            (h_ref, gpre_ref, wg_ref, wu_ref, wd_ref, gpost_ref,
             hout_ref, n_ref, gate_ref, up_ref, f_ref) = refs
        hh = h_ref[...]
        r = lax.rsqrt(jnp.mean(hh * hh, axis=-1, keepdims=True) + RMS_EPS)
        n = (hh * r * gpre_ref[...]).astype(BF16)
        n_ref[...] = n
        acc = jnp.zeros((tm, D), F32)
        for s in range(NS):
            gate = _dot_nt(n, wg_ref[s])
            up = _dot_nt(n, wu_ref[s])
            gate_ref[s] = gate.astype(BF16)
            up_ref[s] = up.astype(BF16)
            a = (gate * _sigmoid(gate) * up).astype(BF16)
            acc = acc + _dot(a, wd_ref[s])
        f_ref[...] = acc
        r2 = lax.rsqrt(jnp.mean(acc * acc, axis=-1, keepdims=True) + RMS_EPS)
        out = hh + 0.5 * (acc * r2 * gpost_ref[...])
        hout_ref[...] = out
        if with_loss:
            e = out - tgt_ref[...]
            dy_ref[...] = e * (1.0 / D)
            loss_ref[...] = jnp.full((1, 8, 128), 0.5 / D, F32) * jnp.sum(e * e)

    in_specs = [_rows(tm, D), _resident((1, D)), _part(wffn, FS, 0), _part(wffn, FS, 1), _part(wffn, FS, 2),
                _resident((1, D))]
    args = [h, g_pre, wffn, wffn, wffn, g_post]
    out_shape = [jax.ShapeDtypeStruct((T, D), F32), jax.ShapeDtypeStruct((T, D), BF16),
                 jax.ShapeDtypeStruct((NS, T, FS), BF16), jax.ShapeDtypeStruct((NS, T, FS), BF16),
                 jax.ShapeDtypeStruct((T, D), F32)]
    seg = pl.BlockSpec((NS, tm, FS), lambda i: (0, i, 0))
    out_specs = [_rows(tm, D), _rows(tm, D), seg, seg, _rows(tm, D)]
    if with_loss:
        in_specs.append(_rows(tm, D))
        args.append(target)
        out_shape += [jax.ShapeDtypeStruct((T, D), F32), jax.ShapeDtypeStruct((nt, 8, 128), F32)]
        out_specs += [_rows(tm, D), pl.BlockSpec((1, 8, 128), lambda i: (i, 0, 0))]
    return _call(body, grid=(nt,), in_specs=in_specs, args=args, out_specs=out_specs, out_shape=out_shape,
                 sem=("parallel",), name=name, ride=ride)


def ffn_bwd(dout, h, f, gate, up, g_pre, g_post, wffn, *, name, tm=256, ride=None):
    T, D = h.shape
    NS, FS = N_SHARD, wffn.shape[1] // 3
    nt = T // tm

    def body(dout_ref, h_ref, f_ref, gate_ref, up_ref, gpre_ref, gpost_ref, wg_ref, wu_ref, wd_ref,
             dh_ref, df_ref, a_ref, dgate_ref, dup_ref, dgpre_ref, dgpost_ref):
        @pl.when(pl.program_id(0) == 0)
        def _():
            dgpre_ref[...] = jnp.zeros_like(dgpre_ref)
            dgpost_ref[...] = jnp.zeros_like(dgpost_ref)

        do = dout_ref[...]
        ff = f_ref[...]
        d_fn = 0.5 * do
        r2 = lax.rsqrt(jnp.mean(ff * ff, axis=-1, keepdims=True) + RMS_EPS)
        dgpost_ref[...] += jnp.sum(d_fn * ff * r2, axis=0, keepdims=True)
        t = d_fn * gpost_ref[...]
        df = r2 * t - ff * (r2 * r2 * r2 * jnp.mean(t * ff, axis=-1, keepdims=True))
        dfb = df.astype(BF16)
        df_ref[...] = dfb
        dn = jnp.zeros((tm, D), F32)
        for s in range(NS):
            da = _dot_nt(dfb, wd_ref[s])
            g = gate_ref[s].astype(F32)
            u = up_ref[s].astype(F32)
            sg = _sigmoid(g)
            silu = g * sg
            a_ref[s] = (silu * u).astype(BF16)
            dgt = (da * u * (sg * (1.0 + g * (1.0 - sg)))).astype(BF16)
            dupv = (da * silu).astype(BF16)
            dgate_ref[s] = dgt
            dup_ref[s] = dupv
            dn = dn + _dot(dgt, wg_ref[s]) + _dot(dupv, wu_ref[s])
        hh = h_ref[...]
        r1 = lax.rsqrt(jnp.mean(hh * hh, axis=-1, keepdims=True) + RMS_EPS)
        dgpre_ref[...] += jnp.sum(dn * hh * r1, axis=0, keepdims=True)
        t = dn * gpre_ref[...]
        dh_ref[...] = do + r1 * t - hh * (r1 * r1 * r1 * jnp.mean(t * hh, axis=-1, keepdims=True))

    seg = pl.BlockSpec((NS, tm, FS), lambda i: (0, i, 0))
    acc = pl.BlockSpec((1, D), lambda i: (0, 0))
    return _call(
        body, grid=(nt,),
        in_specs=[_rows(tm, D), _rows(tm, D), _rows(tm, D), seg, seg, _resident((1, D)), _resident((1, D)),
                  _part(wffn, FS, 0), _part(wffn, FS, 1), _part(wffn, FS, 2)],
        args=[dout, h, f, gate, up, g_pre, g_post, wffn, wffn, wffn],
        out_specs=[_rows(tm, D), _rows(tm, D), seg, seg, seg, acc, acc],
        out_shape=[jax.ShapeDtypeStruct((T, D), F32), jax.ShapeDtypeStruct((T, D), BF16),
                   jax.ShapeDtypeStruct((NS, T, FS), BF16), jax.ShapeDtypeStruct((NS, T, FS), BF16),
                   jax.ShapeDtypeStruct((NS, T, FS), BF16),
                   jax.ShapeDtypeStruct((1, D), F32), jax.ShapeDtypeStruct((1, D), F32)],
        sem=("arbitrary",), name=name, ride=ride)


def mm_tn(a, g, *, name, tt=2048, tn=None, a_cols=None, into=None):
    Ba, T, _ = a.shape
    Bg, _, N = g.shape
    B, K = a_cols if a_cols else (max(Ba, Bg), a.shape[2])
    tn = N if tn is None else tn
    tt = min(tt, T)
    nsteps = T // tt

    def body(*refs):
        a_ref, g_ref, o_ref = refs[0], refs[1], refs[-1]

        @pl.when(pl.program_id(2) == 0)
        def _():
            o_ref[...] = jnp.zeros_like(o_ref)

        o_ref[0] += _dot_tn(a_ref[0], g_ref[0].astype(BF16))

    if a_cols:
        a_map = lambda b, j, t: (0, t, b)
    else:
        a_map = (lambda b, j, t: (b, t, 0)) if Ba > 1 else (lambda b, j, t: (0, t, 0))
    in_specs = [pl.BlockSpec((1, tt, K), a_map),
                pl.BlockSpec((1, tt, tn), (lambda b, j, t: (b, t, j)) if Bg > 1 else (lambda b, j, t: (0, t, j)))]
    args = [a, g]
    if into is None:
        out_shape, part, aliases = jax.ShapeDtypeStruct((B, K, N), F32), 0, {}
    else:
        buf, part = into
        out_shape, aliases = jax.ShapeDtypeStruct(buf.shape, F32), {2: 0}
        in_specs.append(ANY)
        args.append(buf)
    return pl.pallas_call(
        body, grid=(B, N // tn, nsteps), in_specs=in_specs,
        out_specs=pl.BlockSpec((1, K, tn), lambda b, j, t: (b, part, j)),
        out_shape=out_shape, input_output_aliases=aliases,
        compiler_params=_cparams("parallel", "parallel", "arbitrary"), name=name)(*args)


def mix_in_fwd(h, g, w_gz, w_xbc, w_dtT, w_qkv, *, name, tm=256, ride=None):
    T, D = h.shape
    nt = T // tm
    CB = 1024

    def body(h_ref, g_ref, wgz_ref, wxbc_ref, wdtT_ref, wqkv_ref,
             u_ref, gates_ref, z_ref, xbc_ref, dtT_ref, q_ref, k_ref, v_ref):
        hh = h_ref[...]
        r = lax.rsqrt(jnp.mean(hh * hh, axis=-1, keepdims=True) + RMS_EPS)
        u = (hh * r * g_ref[...]).astype(BF16)
        u_ref[...] = u
        for cb in range(0, 2048, CB):
            gates_ref[:, cb:cb + CB] = _dot(u, wgz_ref[:, cb:cb + CB]).astype(BF16)
            z_ref[:, cb:cb + CB] = _dot(u, wgz_ref[:, 2048 + cb:2048 + cb + CB])
        for cb in range(0, SSM_CONV_DIM, CB):
            xbc_ref[:, cb:cb + CB] = _dot(u, wxbc_ref[:, cb:cb + CB])
        dtT_ref[...] = _dot_nt(wdtT_ref[...], u)
        q_ref[...] = (_dot(u, wqkv_ref[:, 0:ATTN_Q_DIM]) * ATTN_SCALE).astype(BF16)
        k_ref[...] = _dot(u, wqkv_ref[:, ATTN_Q_DIM:ATTN_Q_DIM + ATTN_KV_DIM]).astype(BF16)
        v_ref[...] = _dot(u, wqkv_ref[:, ATTN_Q_DIM + ATTN_KV_DIM:]).astype(BF16)

    sds = jax.ShapeDtypeStruct
    return _call(
        body, grid=(nt,),
        in_specs=[_rows(tm, D), _resident((1, D)), _resident(w_gz.shape), _resident(w_xbc.shape),
                  _resident(w_dtT.shape), _resident(w_qkv.shape)],
        args=[h, g, w_gz, w_xbc, w_dtT, w_qkv],
        out_specs=[_rows(tm, D), _rows(tm, 2048), _rows(tm, 2048), _rows(tm, SSM_CONV_DIM),
                   pl.BlockSpec((SSM_HEADS, tm), lambda i: (0, i)),
                   _rows(tm, ATTN_Q_DIM), _rows(tm, ATTN_KV_DIM), _rows(tm, ATTN_KV_DIM)],
        out_shape=[sds((T, D), BF16), sds((T, 2048), BF16), sds((T, 2048), F32), sds((T, SSM_CONV_DIM), F32),
                   sds((SSM_HEADS, T), F32),
                   sds((T, ATTN_Q_DIM), BF16), sds((T, ATTN_KV_DIM), BF16), sds((T, ATTN_KV_DIM), BF16)],
        sem=("parallel",), name=name, ride=ride)


def _softplus(x):
    return jnp.maximum(x, 0.0) + jnp.log(1.0 + jnp.exp(-jnp.abs(x)))


def _iota(shape, axis):
    return lax.broadcasted_iota(jnp.int32, shape, axis)


def _conv_pre(x_ref, halo_ref, w_ref, b_ref, xp_ref, first):
    Q = SSM_CHUNK
    halo = jnp.where(first, 0.0, halo_ref[...])
    xp_ref[0:HALO, :] = halo
    xp_ref[HALO:HALO + Q, :] = x_ref[...]
    pre = b_ref[...] + w_ref[3:4, :] * xp_ref[HALO:HALO + Q, :]
    for k in range(SSM_CONV - 1):
        pre = pre + w_ref[k:k + 1, :] * xp_ref[pl.ds(HALO - 3 + k, Q), :]
    return pre


def _ssd_specs(nc):
    Q, GW, N = SSM_CHUNK, SSM_GW, SSM_STATE
    nb_xs = SSM_D_INNER // N
    nb_c = nb_xs + SSM_GROUPS

    def rb(cmap):
        def row(b, c, g):
            return b * nc + cmap(c)
        return row

    def specs(cmap):
        row = rb(cmap)
        hrow = lambda b, c, g: jnp.maximum(row(b, c, g) * (Q // HALO) - 1, 0)
        return dict(
            xs=pl.BlockSpec((Q, GW), lambda b, c, g: (row(b, c, g), g)),
            bm=pl.BlockSpec((Q, N), lambda b, c, g: (row(b, c, g), nb_xs + g)),
            cm=pl.BlockSpec((Q, N), lambda b, c, g: (row(b, c, g), nb_c + g)),
            xs_halo=pl.BlockSpec((HALO, GW), lambda b, c, g: (hrow(b, c, g), g)),
            bm_halo=pl.BlockSpec((HALO, N), lambda b, c, g: (hrow(b, c, g), nb_xs + g)),
            cm_halo=pl.BlockSpec((HALO, N), lambda b, c, g: (hrow(b, c, g), nb_c + g)),
            grp=pl.BlockSpec((Q, GW), lambda b, c, g: (row(b, c, g), g)),
            dtT_g=pl.BlockSpec((SSM_HPG, Q), lambda b, c, g: (g, row(b, c, g))),
            conv=pl.BlockSpec((1, HALO, GW + 2 * N), lambda b, c, g: (g, 0, 0)),
            cols=pl.BlockSpec((1, Q, 4 * SSM_HPG), lambda b, c, g: (g, row(b, c, g), 0)),
            decay=pl.BlockSpec((1, SSM_HPG, 1), lambda b, c, g: (row(b, c, g), g, 0)),
            vec2=pl.BlockSpec((2, GW), lambda b, c, g: (0, g)),
            col2=pl.BlockSpec((SSM_HPG, 2), lambda b, c, g: (g, 0)),
            pairs=pl.BlockSpec((1, 1, N, GW), lambda b, c, g: (row(b, c, g), g, 0, 0)),
        )
    return specs


def _attn_specs(nb):
    BLK = ATTN_BLOCK

    def specs(last):
        def cur(b, n):
            return b * nb + (n if last is None else jnp.minimum(n, nb - 1))

        def prev(b, n):
            return b * nb + jnp.maximum((n if last is None else jnp.minimum(n, nb - 1)) - 1, 0)
        return cur, prev
    return specs


MASKED = -1e30
ATTN_SCALE = ATTN_HEAD_DIM ** -0.5


def attn_window():
    i = np.arange(ATTN_BLOCK)[:, None]
    j = np.arange(2 * ATTN_BLOCK)[None, :]
    return (j > i) & (j <= i + ATTN_BLOCK)


def _attn_group(kk, q_ref, bias_ref, sink_ref):
    BLK, HD = ATTN_BLOCK, ATTN_HEAD_DIM
    heads = range(ATTN_REP * kk, ATTN_REP * (kk + 1))
    qg = jnp.concatenate([q_ref[:, HD * hd:HD * (hd + 1)] for hd in heads], axis=0)
    bias_p = jnp.concatenate([bias_ref[hd, :, 0:BLK] for hd in heads], axis=0)
    bias_c = jnp.concatenate([bias_ref[hd, :, BLK:2 * BLK] for hd in heads], axis=0)
    sink = jnp.concatenate([jnp.broadcast_to(sink_ref[0:1, hd:hd + 1], (BLK, 1)) for hd in heads], axis=0)
    return qg, bias_p, bias_c, sink


def attn_bias(table_t, onehot, *, name):
    def body(t_ref, f_ref, o_ref):
        o_ref[...] = _dot_hi(t_ref[...], f_ref[...])
    return pl.pallas_call(body, out_shape=jax.ShapeDtypeStruct((ATTN_Q_HEADS, onehot.shape[1]), F32),
                          compiler_params=_cparams(), name=name)(table_t, onehot)


def attn_bias_bwd(dbias, onehot, *, name):
    def body(d_ref, f_ref, o_ref):
        o_ref[...] = lax.dot_general(d_ref[...], f_ref[...], (((1,), (1,)), ((), ())), preferred_element_type=F32,
                                     precision=lax.Precision.HIGHEST)
    return pl.pallas_call(body, out_shape=jax.ShapeDtypeStruct((ATTN_Q_HEADS, REL_BUCKETS), F32),
                          compiler_params=_cparams(), name=name)(dbias, onehot)


def attn_fwd(q, k, v, bias, sinks, *, batch, name):
    T = q.shape[0]
    BLK, HD = ATTN_BLOCK, ATTN_HEAD_DIM
    nb = T // batch // BLK
    cur, prev = _attn_specs(nb)(None)

    def body(q_ref, kc_ref, kp_ref, vc_ref, vp_ref, bias_ref, sink_ref, o_ref, lse_ref):
        n = pl.program_id(1)
        for kk in range(ATTN_KV_HEADS):
            ks = slice(HD * kk, HD * (kk + 1))
            kc, kp, vc, vp = kc_ref[:, ks], kp_ref[:, ks], vc_ref[:, ks], vp_ref[:, ks]
            qg, bias_p, bias_c, sink = _attn_group(kk, q_ref, bias_ref, sink_ref)
            lp = jnp.where(n > 0, _dot_nt(qg, kp) + bias_p, MASKED)
            lc = _dot_nt(qg, kc) + bias_c
            mx = jnp.maximum(jnp.max(jnp.maximum(lp, lc), axis=-1, keepdims=True), sink)
            pp = jnp.exp(lp - mx)
            pc = jnp.exp(lc - mx)
            den = jnp.sum(pp + pc, axis=-1, keepdims=True) + jnp.exp(sink - mx)
            o = ((_dot(pp.astype(BF16), vp) + _dot(pc.astype(BF16), vc)) * (1.0 / den)).astype(BF16)
            lse = mx + jnp.log(den)
            for r in range(ATTN_REP):
                hd = ATTN_REP * kk + r
                o_ref[:, HD * hd:HD * (hd + 1)] = o[BLK * r:BLK * (r + 1)]
                lse_ref[:, hd:hd + 1] = lse[BLK * r:BLK * (r + 1)]

    sds = jax.ShapeDtypeStruct
    return pl.pallas_call(
        body, grid=(batch, nb),
        in_specs=[pl.BlockSpec((BLK, ATTN_Q_DIM), lambda b, n: (cur(b, n), 0)),
                  pl.BlockSpec((BLK, ATTN_KV_DIM), lambda b, n: (cur(b, n), 0)),
                  pl.BlockSpec((BLK, ATTN_KV_DIM), lambda b, n: (prev(b, n), 0)),
                  pl.BlockSpec((BLK, ATTN_KV_DIM), lambda b, n: (cur(b, n), 0)),
                  pl.BlockSpec((BLK, ATTN_KV_DIM), lambda b, n: (prev(b, n), 0)),
                  pl.BlockSpec((ATTN_Q_HEADS, BLK, 2 * BLK), lambda b, n: (0, 0, 0)),
                  pl.BlockSpec((1, ATTN_Q_HEADS), lambda b, n: (0, 0))],
        out_specs=[pl.BlockSpec((BLK, ATTN_Q_DIM), lambda b, n: (cur(b, n), 0)),
                   pl.BlockSpec((BLK, ATTN_Q_HEADS), lambda b, n: (cur(b, n), 0))],
        out_shape=[sds((T, ATTN_Q_DIM), BF16), sds((T, ATTN_Q_HEADS), F32)],
        compiler_params=_cparams("parallel", "parallel"), name=name)(q, k, k, v, v, bias, sinks)


def _proj_specs(wmix):
    return [_part(wmix, 512, 0), _part(wmix, 256, 2), _part(wmix, 256, 3)]


def _natural(w_ref):
    return w_ref[...].reshape(-1, w_ref.shape[2])


def mix_out_fwd(ys, o, gates, h, wmix, g_post, *, name, tm=512):
    T, D = h.shape
    nt = T // tm

    def body(ys_ref, o_ref, gates_ref, h_ref, wssm_ref, wattn_ref, wout_ref, g_ref,
             hout_ref, yssm_ref, yattn_ref, mix_ref, merged_ref):
        y_ssm = _dot(ys_ref[...], _natural(wssm_ref))
        y_attn = _dot(o_ref[...], _natural(wattn_ref))
        yssm_ref[...] = y_ssm.astype(BF16)
        yattn_ref[...] = y_attn.astype(BF16)
        merged = (_sigmoid(gates_ref[:, 0:D].astype(F32)) * y_ssm
                  + _sigmoid(gates_ref[:, D:2 * D].astype(F32)) * y_attn).astype(BF16)
        merged_ref[...] = merged
        mix = _dot(merged, _natural(wout_ref))
        mix_ref[...] = mix.astype(BF16)
        r = lax.rsqrt(jnp.mean(mix * mix, axis=-1, keepdims=True) + RMS_EPS)
        hout_ref[...] = h_ref[...] + mix * r * g_ref[...]

    sds = jax.ShapeDtypeStruct
    return pl.pallas_call(
        body, grid=(nt,),
        in_specs=[_rows(tm, SSM_D_INNER), _rows(tm, ATTN_Q_DIM), _rows(tm, 2 * D), _rows(tm, D),
                  *_proj_specs(wmix), _resident((1, D))],
        out_specs=[_rows(tm, D)] * 5,
        out_shape=[sds((T, D), F32), sds((T, D), BF16), sds((T, D), BF16), sds((T, D), BF16), sds((T, D), BF16)],
        compiler_params=_cparams("parallel"), name=name)(ys, o, gates, h, wmix, wmix, wmix, g_post)


def mix_out_bwd(dh, mix, y_ssm, y_attn, gates, wmix, g_post, *, name, tm=256, ride=None):
    T, D = dh.shape
    nt = T // tm

    def body(dh_ref, mix_ref, yssm_ref, yattn_ref, gates_ref, wssm_ref, wattn_ref, wout_ref, g_ref,
             dmix_ref, dyssm_ref, dyattn_ref, dgates_ref, dys_ref, do_ref, dg_ref):
        @pl.when(pl.program_id(0) == 0)
        def _():
            dg_ref[...] = jnp.zeros_like(dg_ref)

        do = dh_ref[...]
        mix = mix_ref[...].astype(F32)
        r = lax.rsqrt(jnp.mean(mix * mix, axis=-1, keepdims=True) + RMS_EPS)
        dg_ref[...] += jnp.sum(do * mix * r, axis=0, keepdims=True)
        t = do * g_ref[...]
        dmix = (r * t - mix * (r * r * r * jnp.mean(t * mix, axis=-1, keepdims=True))).astype(BF16)
        dmix_ref[...] = dmix
        dmerged = _dot_nt(dmix, _natural(wout_ref))
        s1 = _sigmoid(gates_ref[:, 0:D].astype(F32))
        s2 = _sigmoid(gates_ref[:, D:2 * D].astype(F32))
        dyssm = (dmerged * s1).astype(BF16)
        dyattn = (dmerged * s2).astype(BF16)
        dyssm_ref[...] = dyssm
        dyattn_ref[...] = dyattn
        dgates_ref[:, 0:D] = (dmerged * yssm_ref[...].astype(F32) * (s1 * (1.0 - s1))).astype(BF16)
        dgates_ref[:, D:2 * D] = (dmerged * yattn_ref[...].astype(F32) * (s2 * (1.0 - s2))).astype(BF16)
        dys_ref[...] = _dot_nt(dyssm, _natural(wssm_ref))
        do_ref[...] = _dot_nt(dyattn, _natural(wattn_ref)).astype(BF16)

    sds = jax.ShapeDtypeStruct
    return _call(
        body, grid=(nt,),
        in_specs=[_rows(tm, D), _rows(tm, D), _rows(tm, D), _rows(tm, D), _rows(tm, 2 * D),
                  *_proj_specs(wmix), _resident((1, D))],
        args=[dh, mix, y_ssm, y_attn, gates, wmix, wmix, wmix, g_post],
        out_specs=[_rows(tm, D), _rows(tm, D), _rows(tm, D), _rows(tm, 2 * D), _rows(tm, SSM_D_INNER),
                   _rows(tm, ATTN_Q_DIM), pl.BlockSpec((1, D), lambda i: (0, 0))],
        out_shape=[sds((T, D), BF16), sds((T, D), BF16), sds((T, D), BF16), sds((T, 2 * D), BF16),
                   sds((T, SSM_D_INNER), F32), sds((T, ATTN_Q_DIM), BF16), sds((1, D), F32)],
        sem=("arbitrary",), name=name, ride=ride)


def attn_bwd(q, k, v, o, do, lse, bias, sinks, *, batch, name):
    T = q.shape[0]
    BLK, HD = ATTN_BLOCK, ATTN_HEAD_DIM
    nb = T // batch // BLK
    cur, prev = _attn_specs(nb)(nb)
    scale = HD ** -0.5

    def body(q_ref, kc_ref, kp_ref, vc_ref, vp_ref, o_ref, do_ref, lse_ref, bias_ref, sink_ref,
             dq_ref, dk_ref, dv_ref, dbias_ref, dsink_ref, ck, cv):
        b = pl.program_id(0)
        n = pl.program_id(1)

        @pl.when(jnp.logical_and(b == 0, n == 0))
        def _():
            dbias_ref[...] = jnp.zeros_like(dbias_ref)
            dsink_ref[...] = jnp.zeros_like(dsink_ref)

        @pl.when(n == 0)
        def _():
            ck[...] = jnp.zeros_like(ck)
            cv[...] = jnp.zeros_like(cv)

        @pl.when(n == nb)
        def _():
            dk_ref[...] = ck[...].astype(BF16)
            dv_ref[...] = cv[...].astype(BF16)

        @pl.when(n < nb)
        def _():
            lane16 = _iota((1, ATTN_Q_HEADS), 1)
            dsink = jnp.zeros((1, ATTN_Q_HEADS), F32)
            for kk in range(ATTN_KV_HEADS):
                ks = slice(HD * kk, HD * (kk + 1))
                kc, kp, vc, vp = kc_ref[:, ks], kp_ref[:, ks], vc_ref[:, ks], vp_ref[:, ks]
                heads = range(ATTN_REP * kk, ATTN_REP * (kk + 1))
                qg, bias_p, bias_c, sink = _attn_group(kk, q_ref, bias_ref, sink_ref)
                dog = jnp.concatenate([do_ref[:, HD * hd:HD * (hd + 1)] for hd in heads], axis=0)
                og = jnp.concatenate([o_ref[:, HD * hd:HD * (hd + 1)] for hd in heads], axis=0)
                lse = jnp.concatenate([lse_ref[:, hd:hd + 1] for hd in heads], axis=0)
                lp = jnp.where(n > 0, _dot_nt(qg, kp) + bias_p, MASKED)
                lc = _dot_nt(qg, kc) + bias_c
                pp = jnp.exp(lp - lse)
                pc = jnp.exp(lc - lse)
                delta = jnp.sum(dog.astype(F32) * og.astype(F32), axis=-1, keepdims=True)
                dlp = pp * (_dot_nt(dog, vp) - delta)
                dlc = pc * (_dot_nt(dog, vc) - delta)
                sd = jnp.exp(sink - lse) * delta
                dlpb = dlp.astype(BF16)
                dlcb = dlc.astype(BF16)
                dqg = ((_dot(dlpb, kp) + _dot(dlcb, kc)) * scale).astype(BF16)
                for r, hd in enumerate(heads):
                    rows = slice(BLK * r, BLK * (r + 1))
                    dsink = dsink + jnp.where(lane16 == hd, -jnp.sum(sd[rows], axis=0, keepdims=True), 0.0)
                    dbias_ref[hd, :, 0:BLK] += dlp[rows]
                    dbias_ref[hd, :, BLK:2 * BLK] += dlc[rows]
                    dq_ref[:, HD * hd:HD * (hd + 1)] = dqg[rows]
                dk_ref[:, ks] = (ck[:, ks] + _dot_tn(dlpb, qg)).astype(BF16)
                dv_ref[:, ks] = (cv[:, ks] + _dot_tn(pp.astype(BF16), dog)).astype(BF16)
                ck[:, ks] = _dot_tn(dlcb, qg)
                cv[:, ks] = _dot_tn(pc.astype(BF16), dog)
            dsink_ref[...] += dsink

    sds = jax.ShapeDtypeStruct
    qspec = pl.BlockSpec((BLK, ATTN_Q_DIM), lambda b, n: (cur(b, n), 0))
    cspec = pl.BlockSpec((BLK, ATTN_KV_DIM), lambda b, n: (cur(b, n), 0))
    pspec = pl.BlockSpec((BLK, ATTN_KV_DIM), lambda b, n: (prev(b, n), 0))
    late = pl.BlockSpec((BLK, ATTN_KV_DIM), lambda b, n: (b * nb + jnp.maximum(n - 1, 0), 0))
    return pl.pallas_call(
        body, grid=(batch, nb + 1),
        in_specs=[qspec, cspec, pspec, cspec, pspec, qspec, qspec,
                  pl.BlockSpec((BLK, ATTN_Q_HEADS), lambda b, n: (cur(b, n), 0)),
                  pl.BlockSpec((ATTN_Q_HEADS, BLK, 2 * BLK), lambda b, n: (0, 0, 0)),
                  pl.BlockSpec((1, ATTN_Q_HEADS), lambda b, n: (0, 0))],
        out_specs=[qspec, late, late,
                   pl.BlockSpec((ATTN_Q_HEADS, BLK, 2 * BLK), lambda b, n: (0, 0, 0)),
                   pl.BlockSpec((1, ATTN_Q_HEADS), lambda b, n: (0, 0))],
        out_shape=[sds((T, ATTN_Q_DIM), BF16), sds((T, ATTN_KV_DIM), BF16), sds((T, ATTN_KV_DIM), BF16),
                   sds((ATTN_Q_HEADS, BLK, 2 * BLK), F32), sds((1, ATTN_Q_HEADS), F32)],
        scratch_shapes=[pltpu.VMEM((BLK, ATTN_KV_DIM), F32), pltpu.VMEM((BLK, ATTN_KV_DIM), F32)],
        compiler_params=_cparams("arbitrary", "arbitrary"), name=name)(q, k, k, v, v, o, do, lse, bias, sinks)


def _conv_bwd(dxc, pre, xp_ref, w_ref, carry_ref, acc_ref, dp_ref, g, last):
    Q = SSM_CHUNK
    sg = _sigmoid(pre)
    dpre = dxc * (sg * (1.0 + pre * (1.0 - sg)))
    dp_ref[0:Q, :] = dpre
    dp_ref[Q:Q + HALO, :] = carry_ref[g]
    carry_ref[g] = dpre[0:HALO, :]
    rows = [jnp.sum(dpre * xp_ref[pl.ds(HALO - 3 + k, Q), :], axis=0, keepdims=True) for k in range(SSM_CONV)]
    rows.append(jnp.sum(dpre, axis=0, keepdims=True))
    rows.append(jnp.zeros((HALO - SSM_CONV - 1, dpre.shape[1]), F32))
    acc_ref[g] += jnp.concatenate(rows, axis=0)
    dx = w_ref[3:4, :] * dpre
    for k in range(SSM_CONV - 1):
        dx = dx + w_ref[k:k + 1, :] * dp_ref[pl.ds(3 - k, Q), :]
    return dx


def mix_in_bwd(dh, h, g, dgates, dz, dxs, dbm, dcm, ddtT, dq, dk, dv, w_gz, w_xbc, w_dtT, w_qkv, *, name, tm=512,
               ride=None):
    T, D = h.shape
    nt = T // tm
    GN = SSM_GROUPS * SSM_STATE

    def body(dh_ref, h_ref, g_ref, dgates_ref, dz_ref, dxs_ref, dbm_ref, dcm_ref, ddt_ref, dq_ref, dk_ref, dv_ref,
             wgz_ref, wxbc_ref, wdt_ref, wqkv_ref, dhin_ref, dg_ref):
        @pl.when(pl.program_id(0) == 0)
        def _():
            dg_ref[...] = jnp.zeros_like(dg_ref)

        du = _dot_nt(dgates_ref[...], wgz_ref[:, 0:2048])
        du = du + _dot_nt(dz_ref[...], wgz_ref[:, 2048:4096])
        du = du + _dot_nt(dxs_ref[...], wxbc_ref[:, 0:SSM_D_INNER])
        du = du + _dot_nt(dbm_ref[...], wxbc_ref[:, SSM_D_INNER:SSM_D_INNER + GN])
        du = du + _dot_nt(dcm_ref[...], wxbc_ref[:, SSM_D_INNER + GN:])
        du = du + _dot_tn(ddt_ref[...].astype(BF16), wdt_ref[...])
        du = du + _dot_nt(dq_ref[...], wqkv_ref[:, 0:ATTN_Q_DIM])
        du = du + _dot_nt(dk_ref[...], wqkv_ref[:, ATTN_Q_DIM:ATTN_Q_DIM + ATTN_KV_DIM])
        du = du + _dot_nt(dv_ref[...], wqkv_ref[:, ATTN_Q_DIM + ATTN_KV_DIM:])
        hh = h_ref[...]
        r = lax.rsqrt(jnp.mean(hh * hh, axis=-1, keepdims=True) + RMS_EPS)
        dg_ref[...] += jnp.sum(du * hh * r, axis=0, keepdims=True)
        t = du * g_ref[...]
        dhin_ref[...] = dh_ref[...] + r * t - hh * (r * r * r * jnp.mean(t * hh, axis=-1, keepdims=True))

    sds = jax.ShapeDtypeStruct
    return _call(
        body, grid=(nt,),
        in_specs=[_rows(tm, D), _rows(tm, D), _resident((1, D)), _rows(tm, 2048), _rows(tm, 2048), _rows(tm, SSM_D_INNER),
                  _rows(tm, GN), _rows(tm, GN), pl.BlockSpec((SSM_HEADS, tm), lambda i: (0, i)),
                  _rows(tm, ATTN_Q_DIM), _rows(tm, ATTN_KV_DIM),
                  _rows(tm, ATTN_KV_DIM), _resident(w_gz.shape), _resident(w_xbc.shape), _resident(w_dtT.shape),
                  _resident(w_qkv.shape)],
        args=[dh, h, g, dgates, dz, dxs, dbm, dcm, ddtT, dq, dk, dv, w_gz, w_xbc, w_dtT, w_qkv],
        out_specs=[_rows(tm, D), pl.BlockSpec((1, D), lambda i: (0, 0))],
        out_shape=[sds((T, D), F32), sds((1, D), F32)],
        sem=("arbitrary",), name=name, ride=ride)


PAIRS = SSM_HPG // 2
PW = 2 * SSM_HEAD_DIM


def ssd_scalars(dt_rawT, dt_bias, a_log, *, name, chunks=4):
    H, T = dt_rawT.shape
    Q, G, HPG = SSM_CHUNK, SSM_GROUPS, SSM_HPG
    span = Q * chunks

    def body(dtT_ref, dtb_ref, alog_ref, cols_ref, acsT_ref, dec_ref):
        aT = -jnp.exp(alog_ref[...])
        triT = (_iota((Q, Q), 0) <= _iota((Q, Q), 1)).astype(F32)
        for j in range(chunks):
            at = slice(Q * j, Q * (j + 1))
            dtT = _softplus(dtT_ref[:, at] + dtb_ref[...])
            acsT = _dot_hi(dtT * aT, triT)
            lastT = acsT[:, Q - 1:Q]
            acsT_ref[:, at] = acsT
            dec_ref[j] = jnp.exp(lastT)
            parts = [dtT, acsT, jnp.exp(lastT - acsT), jnp.exp(acsT)]
            colsT = jnp.concatenate([q[HPG * g:HPG * (g + 1)] for g in range(G) for q in parts], axis=0).T
            for g in range(G):
                cols_ref[g, at, :] = colsT[:, 4 * HPG * g:4 * HPG * (g + 1)]

    sds = jax.ShapeDtypeStruct
    return pl.pallas_call(
        body, grid=(T // span,),
        in_specs=[pl.BlockSpec((H, span), lambda i: (0, i)), pl.BlockSpec((H, 1), lambda i: (0, 0)),
                  pl.BlockSpec((H, 1), lambda i: (0, 0))],
        out_specs=[pl.BlockSpec((G, span, 4 * HPG), lambda i: (0, i, 0)), pl.BlockSpec((H, span), lambda i: (0, i)),
                   pl.BlockSpec((chunks, H, 1), lambda i: (i, 0, 0))],
        out_shape=[sds((G, T, 4 * HPG), F32), sds((H, T), F32), sds((T // Q, H, 1), F32)],
        compiler_params=_cparams("parallel"), name=name)(dt_rawT, dt_bias, a_log)


CONV_XS, CONV_BM, CONV_CM = slice(0, SSM_GW), slice(SSM_GW, SSM_GW + SSM_STATE), slice(SSM_GW + SSM_STATE, SSM_GW + 2 * SSM_STATE)


def _ssd_prologue(first, xs_ref, bm_ref, cm_ref, xs_halo, bm_halo, cm_halo, conv_ref, cols_ref, acsT_ref, dec_ref,
                  xp_xs, xp_bm, xp_cm):
    cp = conv_ref[0]
    taps = {n: (cp[:, at], cp[SSM_CONV:SSM_CONV + 1, at]) for n, at in (("xs", CONV_XS), ("bm", CONV_BM), ("cm", CONV_CM))}
    pre_xs = _conv_pre(xs_ref, xs_halo, *taps["xs"], xp_xs, first)
    pre_bm = _conv_pre(bm_ref, bm_halo, *taps["bm"], xp_bm, first)
    pre_cm = _conv_pre(cm_ref, cm_halo, *taps["cm"], xp_cm, first)
    return dict(pre_xs=pre_xs, pre_bm=pre_bm, pre_cm=pre_cm, xs=pre_xs * _sigmoid(pre_xs), bm=pre_bm * _sigmoid(pre_bm),
                cm=pre_cm * _sigmoid(pre_cm), taps=taps, acsT=acsT_ref[...], decayT=dec_ref[0], cols=cols_ref[0])


def _pair_cols(cols, base, p, lo):
    k = base + 2 * p
    return jnp.where(lo, cols[:, k:k + 1], cols[:, k + 1:k + 2])


def _pair_row(colT, p, lo_row):
    return jnp.where(lo_row, colT[2 * p:2 * p + 1, :], colT[2 * p + 1:2 * p + 2, :])


def _pair_operands(pp, p, s, causal, lo, xb):
    zero = jnp.zeros_like(xb)
    rhs = jnp.concatenate([jnp.where(lo, xb, zero), jnp.where(lo, zero, xb)], axis=0)
    ls, ms = [], []
    for k in (2 * p, 2 * p + 1):
        seg = pp["cols"][:, 8 + k:9 + k] - pp["acsT"][k:k + 1, :]
        l = jnp.exp(jnp.where(causal, seg, -1e30))
        ls.append(l)
        ms.append(s * l)
    lhs = jnp.concatenate([m.astype(BF16) for m in ms], axis=1)
    return lhs, rhs, ls


def ssd_fwd(xbc, z, scalars, conv_pack, vec_pack, *, batch, name):
    T = xbc.shape[0]
    Q, GW, N = SSM_CHUNK, SSM_GW, SSM_STATE
    nc = T // batch // Q
    sp = _ssd_specs(nc)(lambda c: c)

    def body(xs_ref, bm_ref, cm_ref, xs_halo, bm_halo, cm_halo, z_ref, conv_ref, cols_ref, acsT_ref, dec_ref, vec_ref,
             y_ref, ys_ref, st_ref, state, xp_xs, xp_bm, xp_cm):
        c = pl.program_id(1)
        g = pl.program_id(2)
        first = c == 0
        pp = _ssd_prologue(first, xs_ref, bm_ref, cm_ref, xs_halo, bm_halo, cm_halo, conv_ref, cols_ref, acsT_ref,
                           dec_ref, xp_xs, xp_bm, xp_cm)
        dsk_ref, ng_ref = vec_ref.at[0:1], vec_ref.at[1:2]
        xs = pp["xs"]
        bb = pp["bm"].astype(BF16)
        cb = pp["cm"].astype(BF16)
        s = _dot_nt(cb, bb)
        causal = _iota((Q, Q), 0) >= _iota((Q, Q), 1)
        lo = _iota((Q, PW), 1) < SSM_HEAD_DIM
        lo_row = _iota((1, PW), 1) < SSM_HEAD_DIM

        @pl.when(first)
        def _():
            state[g] = jnp.zeros((N, GW), F32)

        entering = state[g]
        st_ref[0, 0] = entering
        wide = lambda base: jnp.concatenate([_pair_cols(pp["cols"], base, p, lo) for p in range(PAIRS)], axis=1)
        x = xs * wide(0)
        xb = x.astype(BF16)
        yd = []
        for p in range(PAIRS):
            lhs, rhs, _ = _pair_operands(pp, p, s, causal, lo, xb[:, PW * p:PW * (p + 1)])
            yd.append(_dot(lhs, rhs))
        y = jnp.concatenate(yd, axis=1) + _dot(cb, entering.astype(BF16)) * wide(24) + dsk_ref[...] * xs
        decay = jnp.concatenate([_pair_row(pp["decayT"], p, lo_row) for p in range(PAIRS)], axis=1)
        state[g] = entering * decay + _dot_tn(bb, (x * wide(16)).astype(BF16))
        y_ref[...] = y
        zz = z_ref[...]
        yg = y * (zz * _sigmoid(zz))
        rr = lax.rsqrt(jnp.mean(yg * yg, axis=-1, keepdims=True) + RMS_EPS)
        ys_ref[...] = (yg * rr * ng_ref[...]).astype(BF16)

    sds = jax.ShapeDtypeStruct
    return pl.pallas_call(
        body, grid=(batch, nc, SSM_GROUPS),
        in_specs=[sp["xs"], sp["bm"], sp["cm"], sp["xs_halo"], sp["bm_halo"], sp["cm_halo"], sp["grp"],
                  sp["conv"], sp["cols"], sp["dtT_g"], sp["decay"], sp["vec2"]],
        out_specs=[sp["grp"], sp["grp"], sp["pairs"]],
        out_shape=[sds((T, SSM_D_INNER), F32), sds((T, SSM_D_INNER), BF16), sds((T // Q, SSM_GROUPS, N, GW), F32)],
        scratch_shapes=[pltpu.VMEM((SSM_GROUPS, N, GW), F32),
                        pltpu.VMEM((HALO + Q, GW), F32), pltpu.VMEM((HALO + Q, N), F32), pltpu.VMEM((HALO + Q, N), F32)],
        compiler_params=_cparams("arbitrary", "arbitrary", "arbitrary"), name=name,
    )(xbc, xbc, xbc, xbc, xbc, xbc, z, conv_pack, *scalars, vec_pack)


def ssd_bwd(dys, y, xbc, z, dt_rawT, states, scalars, conv_pack, col_pack, vec_pack, *, batch, name, ride=None):
    T = xbc.shape[0]
    Q, GW, N, G = SSM_CHUNK, SSM_GW, SSM_STATE, SSM_GROUPS
    nc = T // batch // Q
    sp = _ssd_specs(nc)(lambda c: nc - 1 - c)

    def body(xs_ref, bm_ref, cm_ref, xs_halo, bm_halo, cm_halo, z_ref, y_ref, dys_ref, dtT_ref, st_ref,
             conv_ref, cols_ref, acsT_ref, dec_ref, col_ref, vec_ref,
             dz_ref, dxs_ref, dbm_ref, dcm_ref, ddtT_ref, acc_xs, acc_bm, acc_cm, acc_head,
             dstate, xp_xs, xp_bm, xp_cm, dp_xs, dp_bm, dp_cm, cy_xs, cy_bm, cy_cm):
        b = pl.program_id(0)
        cr = pl.program_id(1)
        g = pl.program_id(2)
        first = cr == nc - 1
        last = cr == 0

        @pl.when(jnp.logical_and(jnp.logical_and(b == 0, cr == 0), g == 0))
        def _():
            acc_xs[...] = jnp.zeros_like(acc_xs)
            acc_bm[...] = jnp.zeros_like(acc_bm)
            acc_cm[...] = jnp.zeros_like(acc_cm)
            acc_head[...] = jnp.zeros_like(acc_head)

        @pl.when(last)
        def _():
            dstate[g] = jnp.zeros((N, GW), F32)
            cy_xs[g] = jnp.zeros((HALO, GW), F32)
            cy_bm[g] = jnp.zeros((HALO, N), F32)
            cy_cm[g] = jnp.zeros((HALO, N), F32)

        pp = _ssd_prologue(first, xs_ref, bm_ref, cm_ref, xs_halo, bm_halo, cm_halo, conv_ref, cols_ref, acsT_ref,
                           dec_ref, xp_xs, xp_bm, xp_cm)
        xs, decayT = pp["xs"], pp["decayT"]
        dsk_ref, ng_ref = vec_ref.at[0:1], vec_ref.at[1:2]
        dtrT = dtT_ref[...] + col_ref[:, 0:1]
        dtT = _softplus(dtrT)
        aT = -jnp.exp(col_ref[:, 1:2])

        yv = y_ref[...]
        zz = z_ref[...]
        sz = _sigmoid(zz)
        silu_z = zz * sz
        yg = yv * silu_z
        rr = lax.rsqrt(jnp.mean(yg * yg, axis=-1, keepdims=True) + RMS_EPS)
        dys_v = dys_ref[...]
        d_ng = jnp.sum(dys_v * yg * rr, axis=0, keepdims=True)
        t = dys_v * ng_ref[...]
        dyg = rr * t - yg * (rr * rr * rr * jnp.mean(t * yg, axis=-1, keepdims=True))
        dy = dyg * silu_z
        dz_ref[...] = (dyg * yv * (sz * (1.0 + zz * (1.0 - sz)))).astype(BF16)
        dsk = dsk_ref[...]
        d_dsk = jnp.sum(dy * xs, axis=0, keepdims=True)

        bb = pp["bm"].astype(BF16)
        cb = pp["cm"].astype(BF16)
        s = _dot_nt(cb, bb)
        causal = _iota((Q, Q), 0) >= _iota((Q, Q), 1)
        lo = _iota((Q, PW), 1) < SSM_HEAD_DIM
        lo_row = _iota((1, PW), 1) < SSM_HEAD_DIM
        wide = lambda base: jnp.concatenate([_pair_cols(pp["cols"], base, p, lo) for p in range(PAIRS)], axis=1)
        dt_x, w_x, e_x = wide(0), wide(16), wide(24)
        decay = jnp.concatenate([_pair_row(decayT, p, lo_row) for p in range(PAIRS)], axis=1)
        x = xs * dt_x
        xw = x * w_x
        xb = x.astype(BF16)
        dyb = dy.astype(BF16)
        dye = (dy * e_x).astype(BF16)
        hp = st_ref[0, 0]
        hpb = hp.astype(BF16)
        dh = dstate[g]
        dhb = dh.astype(BF16)
        ds_acc = jnp.zeros((Q, Q), F32)
        yd, dxd = [], []
        for p in range(PAIRS):
            tile = slice(PW * p, PW * (p + 1))
            lhs, rhs, ls = _pair_operands(pp, p, s, causal, lo, xb[:, tile])
            dm = _dot_nt(dyb[:, tile], rhs)
            dxd2 = _dot_tn(lhs, dyb[:, tile])
            dxd.append(jnp.where(lo, dxd2[0:Q], dxd2[Q:2 * Q]))
            ds_acc = ds_acc + dm[:, 0:Q] * ls[0] + dm[:, Q:2 * Q] * ls[1]
            yd.append(_dot(lhs, rhs))
        yd = jnp.concatenate(yd, axis=1)
        dxd = jnp.concatenate(dxd, axis=1)
        dxw = _dot(bb, dhb)
        dx_full = dxd + dxw * w_x
        tw = dxw * xw
        q1 = dyb.astype(F32) * yd + dy * (_dot(cb, hpb) * e_x) - tw - xb.astype(F32) * dxd
        q2 = dx_full * xs
        dxs_v = dsk * dy + dx_full * dt_x
        dstate[g] = dh * decay + _dot_tn(cb, dye)
        dsb = ds_acc.astype(BF16)
        d_c = _dot_nt(dye, hpb) + _dot(dsb, bb)
        d_b = _dot_nt(xw.astype(BF16), dhb) + _dot_tn(dsb, cb)
        e8 = (_iota((SSM_HPG, GW), 0) == lax.shift_right_logical(_iota((SSM_HPG, GW), 1), 6)).astype(F32)
        seg_sum = lambda v: lax.dot_general(e8, v, (((1,), (1,)), ((), ())), preferred_element_type=F32,
                                            precision=lax.Precision.HIGHEST)
        row = jnp.sum(dh * hp, axis=0, keepdims=True) * decay + jnp.sum(tw, axis=0, keepdims=True)
        last_terms = jnp.sum(e8 * row, axis=1, keepdims=True)
        dacsT = seg_sum(q1) + jnp.where(_iota((1, Q), 1) == Q - 1, 1.0, 0.0) * last_terms
        tri = (_iota((Q, Q), 0) >= _iota((Q, Q), 1)).astype(F32)
        d_dtaT = _dot_hi(dacsT, tri)
        ddtT = d_dtaT * aT + seg_sum(q2)
        d_alog = jnp.sum(d_dtaT * dtT, axis=1, keepdims=True) * aT
        ddt_rawT = ddtT * _sigmoid(dtrT)
        ddtT_ref[...] = ddt_rawT
        d_dtb = jnp.sum(ddt_rawT, axis=1, keepdims=True)
        lane = _iota((SSM_HPG, N), 1)
        acc_head[g] += jnp.where(lane == 0, d_dtb, 0.0) + jnp.where(lane == 1, d_alog, 0.0)
        taps = pp["taps"]
        dxs_ref[...] = _conv_bwd(dxs_v, pp["pre_xs"], xp_xs, taps["xs"][0], cy_xs, acc_xs, dp_xs, g, last).astype(BF16)
        dbm_ref[...] = _conv_bwd(d_b, pp["pre_bm"], xp_bm, taps["bm"][0], cy_bm, acc_bm, dp_bm, g, last).astype(BF16)
        dcm_ref[...] = _conv_bwd(d_c, pp["pre_cm"], xp_cm, taps["cm"][0], cy_cm, acc_cm, dp_cm, g, last).astype(BF16)
        acc_xs[g, pl.ds(SSM_CONV + 1, 2), :] += jnp.concatenate([d_ng, d_dsk], axis=0)

    sds = jax.ShapeDtypeStruct
    row = lambda b, c, g: b * nc + (nc - 1 - c)
    full = lambda shape: pl.BlockSpec(shape, lambda b, c, g: (0,) * len(shape))
    return _call(
        body, grid=(batch, nc, G),
        in_specs=[sp["xs"], sp["bm"], sp["cm"], sp["xs_halo"], sp["bm_halo"], sp["cm_halo"], sp["grp"], sp["grp"],
                  sp["grp"], sp["dtT_g"], sp["pairs"],
                  sp["conv"], sp["cols"], sp["dtT_g"], sp["decay"], sp["col2"], sp["vec2"]],
        args=[xbc, xbc, xbc, xbc, xbc, xbc, z, y, dys, dt_rawT, states, conv_pack, *scalars, col_pack, vec_pack],
        out_specs=[sp["grp"], sp["grp"],
                   pl.BlockSpec((Q, N), lambda b, c, g: (row(b, c, g), g)),
                   pl.BlockSpec((Q, N), lambda b, c, g: (row(b, c, g), g)),
                   sp["dtT_g"], full((G, HALO, GW)), full((G, HALO, N)), full((G, HALO, N)), full((G, SSM_HPG, N))],
        out_shape=[sds((T, SSM_D_INNER), BF16), sds((T, SSM_D_INNER), BF16), sds((T, G * N), BF16),
                   sds((T, G * N), BF16), sds((SSM_HEADS, T), F32),
                   sds((G, HALO, GW), F32), sds((G, HALO, N), F32), sds((G, HALO, N), F32), sds((G, SSM_HPG, N), F32)],
        scratch=[pltpu.VMEM((G, N, GW), F32),
                 pltpu.VMEM((HALO + Q, GW), F32), pltpu.VMEM((HALO + Q, N), F32), pltpu.VMEM((HALO + Q, N), F32),
                 pltpu.VMEM((Q + HALO, GW), F32), pltpu.VMEM((Q + HALO, N), F32), pltpu.VMEM((Q + HALO, N), F32),
                 pltpu.VMEM((G, HALO, GW), F32), pltpu.VMEM((G, HALO, N), F32), pltpu.VMEM((G, HALO, N), F32)],
        sem=("arbitrary", "arbitrary", "arbitrary"), name=name, ride=ride)


def mm_rows(a, b, *, name, tt=2048):
    M, T = a.shape
    N = b.shape[1]
    tt = min(tt, T)

    def body(a_ref, b_ref, o_ref):
        @pl.when(pl.program_id(0) == 0)
        def _():
            o_ref[...] = jnp.zeros_like(o_ref)

        o_ref[...] += _dot(a_ref[...].astype(BF16), b_ref[...])

    return pl.pallas_call(
        body, grid=(T // tt,),
        in_specs=[pl.BlockSpec((M, tt), lambda t: (0, t)), pl.BlockSpec((tt, N), lambda t: (t, 0))],
        out_specs=pl.BlockSpec((M, N), lambda t: (0, 0)), out_shape=jax.ShapeDtypeStruct((M, N), F32),
        compiler_params=_cparams("arbitrary"), name=name)(a, b)


MESH = pl.DeviceIdType.MESH
ANY = pl.BlockSpec(memory_space=pl.ANY)
ROW_ALIGN = 16


def _me():
    return lax.axis_index("x"), lax.axis_index("y"), lax.axis_index("c")


def _other_chips(x, y):
    return [(1 - x, y), (x, 1 - y), (1 - x, 1 - y)]


def _remote(src, dst, send_sem, recv_sem, to):
    return pltpu.make_async_remote_copy(src_ref=src, dst_ref=dst, send_sem=send_sem, recv_sem=recv_sem,
                                        device_id=to, device_id_type=MESH)


def _half(c, rows):
    return pl.ds(pl.multiple_of(c * (rows // 2), ROW_ALIGN), rows // 2)


def ag_ride(bufs):
    n = len(bufs)

    def copies(outs, sems):
        ici_send, ici_recv, d2d_send, d2d_recv = sems
        x, y, c = _me()
        sib = (x, y, 1 - c)
        ici, d2d, d2d_in = [], [], []
        for i in range(n):
            rows = outs[i].shape[1]
            mine = outs[i].at[2 * x + y, _half(c, rows)]
            for j, chip in enumerate(_other_chips(x, y)):
                ici.append(_remote(mine, mine, ici_send.at[i, j], ici_recv.at[i, j], (*chip, c)))
                landed = outs[i].at[2 * chip[0] + chip[1], _half(c, rows)]
                d2d.append((_remote(landed, landed, ici_send.at[i, j], ici_recv.at[i, j], (*chip, c)),
                            _remote(landed, landed, d2d_send.at[i, j], d2d_recv.at[i, j], sib)))
                lands = outs[i].at[2 * chip[0] + chip[1], _half(1 - c, rows)]
                d2d_in.append(_remote(lands, lands, d2d_send.at[i, j], d2d_recv.at[i, j], sib))
        return ici, d2d, d2d_in

    def start(ins, outs, sems):
        for cp in copies(outs, sems)[0]:
            cp.start()

    def finish(ins, outs, sems):
        ici, d2d, d2d_in = copies(outs, sems)
        for arrived, forward in d2d:
            arrived.wait_recv()
            forward.start()
        for cp in d2d_in:
            cp.wait_recv()
        for cp in ici + [forward for _, forward in d2d]:
            cp.wait_send()

    return Ride(bufs, [jax.ShapeDtypeStruct(b.shape, b.dtype) for b in bufs], [(i, i) for i in range(n)],
                [pltpu.SemaphoreType.DMA((n, 3))] * 4, start, finish)


def pair_ride(grads):
    n = len(grads)

    def copies(ins, outs, sems):
        x, y, c = _me()
        return [_remote(ins[i].at[:, _half(1 - c, ins[i].shape[1]), :], outs[i], sems[0].at[i], sems[1].at[i], (x, y, 1 - c))
                for i in range(n)]

    def start(ins, outs, sems):
        for cp in copies(ins, outs, sems):
            cp.start()

    def finish(ins, outs, sems):
        for cp in copies(ins, outs, sems):
            cp.wait()

    return Ride(grads, [jax.ShapeDtypeStruct((N_SHARD, g.shape[1] // 2, g.shape[2]), g.dtype) for g in grads], [],
                [pltpu.SemaphoreType.DMA((n,))] * 2, start, finish)


def rs_add(grad, part, c, *, rt, name):
    _, rows, cols = grad.shape
    r2 = rows // 2
    nrb = r2 // rt

    def body(c_ref, g_ref, p_ref, o_ref):
        o_ref[...] = (g_ref[...] + p_ref[...]).astype(BF16)

    return pl.pallas_call(
        body,
        grid_spec=pltpu.PrefetchScalarGridSpec(
            num_scalar_prefetch=1, grid=(N_SHARD, nrb),
            in_specs=[pl.BlockSpec((1, rt, cols), lambda k, i, c_ref: (k, c_ref[1] * nrb + i, 0)),
                      pl.BlockSpec((1, rt, cols), lambda k, i, c_ref: (k, i, 0))],
            out_specs=pl.BlockSpec((1, rt, cols), lambda k, i, c_ref: (k, i, 0))),
        out_shape=jax.ShapeDtypeStruct((N_SHARD, r2, cols), BF16),
        compiler_params=_cparams("parallel", "parallel"), name=name)(c, grad, part)


def chips_ride(sums):
    n = len(sums)

    def copies(ins, outs, sems):
        send, recv = sems
        x, y, c = _me()
        return [_remote(ins[i].at[2 * chip[0] + chip[1]], outs[i].at[2 * x + y], send.at[i, j], recv.at[i, j], (*chip, c))
                for i in range(n) for j, chip in enumerate(_other_chips(x, y))]

    def start(ins, outs, sems):
        for cp in copies(ins, outs, sems):
            cp.start()

    def finish(ins, outs, sems):
        for cp in copies(ins, outs, sems):
            cp.wait()

    return Ride(sums, [jax.ShapeDtypeStruct(s.shape, s.dtype) for s in sums], [],
                [pltpu.SemaphoreType.DMA((n, 3))] * 2, start, finish)


def rs_total(parts, own, where, *, rt, name):
    _, r2, cols = parts.shape
    nrb = r2 // rt

    def body(w_ref, p0, p1, p2, p3, own_ref, o_ref):
        s_me = w_ref[0]
        acc = None
        for k, p in enumerate((p0, p1, p2, p3)):
            term = jnp.where(s_me == k, own_ref[0], p[0]).astype(F32)
            acc = term if acc is None else acc + term
        o_ref[...] = acc

    def slot(k):
        return pl.BlockSpec((1, rt, cols), lambda i, w: (jnp.where(w[0] == k, (k + 1) % N_SHARD, k), i, 0))

    return pl.pallas_call(
        body,
        grid_spec=pltpu.PrefetchScalarGridSpec(
            num_scalar_prefetch=1, grid=(nrb,),
            in_specs=[slot(0), slot(1), slot(2), slot(3), pl.BlockSpec((1, rt, cols), lambda i, w: (w[0], i, 0))],
            out_specs=pl.BlockSpec((rt, cols), lambda i, w: (w[1] * nrb + i, 0))),
        out_shape=jax.ShapeDtypeStruct((2 * r2, cols), F32),
        compiler_params=_cparams("parallel"), name=name)(where, parts, parts, parts, parts, own)


def share_ride(totals):
    n = len(totals)

    def halves(outs, sems):
        x, y, c = _me()
        mine = [outs[i].at[_half(c, outs[i].shape[0])] for i in range(n)]
        other = [outs[i].at[_half(1 - c, outs[i].shape[0])] for i in range(n)]
        return ([_remote(m, m, sems[0].at[i], sems[1].at[i], (x, y, 1 - c)) for i, m in enumerate(mine)],
                [_remote(o, o, sems[0].at[i], sems[1].at[i], (x, y, 1 - c)) for i, o in enumerate(other)])

    def start(ins, outs, sems):
        for cp in halves(outs, sems)[0]:
            cp.start()

    def finish(ins, outs, sems):
        sent, landing = halves(outs, sems)
        for cp in landing:
            cp.wait_recv()
        for cp in sent:
            cp.wait_send()

    return Ride(totals, [jax.ShapeDtypeStruct(t.shape, t.dtype) for t in totals], [(i, i) for i in range(n)],
                [pltpu.SemaphoreType.DMA((n,))] * 2, start, finish)


def small_allreduce(buf, *, name):
    rows = buf.shape[0]

    def body(x_ref, o_ref, slots, send, recv):
        x, y, c = _me()
        me = 4 * x + 2 * y + c
        slots[me] = x_ref[...]
        sent = []
        for d in range(1, 8):
            peer = (1 - x if d & 4 else x, 1 - y if d & 2 else y, 1 - c if d & 1 else c)
            sent.append(_remote(x_ref, slots.at[me], send.at[d - 1], recv.at[d - 1], peer))
            sent[-1].start()
        for cp in sent:
            cp.wait()
        acc = slots[0]
        for k in range(1, 8):
            acc = acc + slots[k]
        o_ref[...] = acc

    return pl.pallas_call(
        body, out_shape=jax.ShapeDtypeStruct(buf.shape, F32),
        in_specs=[pl.BlockSpec(memory_space=pltpu.VMEM)], out_specs=pl.BlockSpec(memory_space=pltpu.VMEM),
        scratch_shapes=[pltpu.VMEM((8, rows, 128), F32), pltpu.SemaphoreType.DMA((7,)), pltpu.SemaphoreType.DMA((7,))],
        name=name)(buf)


def adamw(w, g, m, v, *, name, rt=None):
    rows, cols = w.shape
    rt = rows if rt is None else rt
    c1 = 1.0 - ADAM_B1 ** ADAM_STEP
    c2 = 1.0 - ADAM_B2 ** ADAM_STEP

    def body(w_ref, g_ref, m_ref, v_ref, d_ref, nm_ref, nv_ref):
        gg = g_ref[...]
        nm = ADAM_B1 * m_ref[...] + (1.0 - ADAM_B1) * gg
        nv = ADAM_B2 * v_ref[...] + (1.0 - ADAM_B2) * (gg * gg)
        nm_ref[...] = nm
        nv_ref[...] = nv
        d_ref[...] = -ADAM_LR * ((nm / c1) / (jnp.sqrt(nv / c2) + ADAM_EPS) + ADAM_WD * w_ref[...])

    spec = pl.BlockSpec((rt, cols), lambda i: (i, 0))
    return pl.pallas_call(
        body, grid=(rows // rt,), in_specs=[spec] * 4, out_specs=[spec] * 3,
        out_shape=[jax.ShapeDtypeStruct((rows, cols), F32)] * 3,
        compiler_params=_cparams("parallel"), name=name)(w, g, m, v)


WEIGHTS = ['ffn1_pre_g', 'ffn1_w_gate', 'ffn1_w_up', 'ffn1_w_down', 'ffn1_post_g', 'mix_pre_g', 'w_in', 'conv_w',
           'conv_b', 'dt_bias', 'a_log', 'd_skip', 'ssm_norm_g', 'w_ssm_proj', 'attn_sinks', 'rel_bias_table',
           'w_attn_proj', 'w_out', 'mix_post_g', 'ffn2_pre_g', 'ffn2_w_gate', 'ffn2_w_up', 'ffn2_w_down', 'ffn2_post_g']
BIG = ['ffn1_w_gate', 'ffn1_w_up', 'ffn1_w_down', 'w_in', 'w_ssm_proj', 'w_attn_proj', 'w_out',
       'ffn2_w_gate', 'ffn2_w_up', 'ffn2_w_down']
SMALL = [w for w in WEIGHTS if w not in BIG]


def _bucket_onehot():
    blk = ATTN_BLOCK
    dist = np.maximum(np.arange(blk)[:, None] + blk - np.arange(2 * blk)[None, :], 0)
    max_exact = REL_BUCKETS // 2
    d = np.maximum(dist, 1).astype(np.float32)
    large = max_exact + (np.log(d / np.float32(max_exact)) / np.float32(math.log(REL_MAX_DISTANCE / max_exact))
                         * np.float32(REL_BUCKETS - max_exact)).astype(np.int32)
    bucket = np.where(dist < max_exact, dist, np.minimum(large, REL_BUCKETS - 1)).reshape(-1)
    return jnp.asarray((bucket[None, :] == np.arange(REL_BUCKETS)[:, None]).astype(np.float32))


def _pack_rows(parts, mult=8):
    flat = jnp.concatenate([p.reshape(-1).astype(F32) for p in parts])
    rows = -(-flat.shape[0] // (128 * mult)) * mult
    return jnp.pad(flat, (0, rows * 128 - flat.shape[0])).reshape(rows, 128)


def _unpack_rows(buf, shapes):
    flat = buf.reshape(-1)
    out, at = [], 0
    for shp in shapes:
        size = int(np.prod(shp))
        out.append(flat[at:at + size].reshape(shp))
        at += size
    return out


def kernel(x, ffn1_pre_g, ffn1_w_gate, ffn1_w_up, ffn1_w_down, ffn1_post_g, mix_pre_g, w_in, conv_w, conv_b, dt_bias, a_log, d_skip, ssm_norm_g, w_ssm_proj, attn_sinks, rel_bias_table, w_attn_proj, w_out, mix_post_g, ffn2_pre_g, ffn2_w_gate, ffn2_w_up, ffn2_w_down, ffn2_post_g, loss_target, m_ffn1_pre_g, m_ffn1_w_gate, m_ffn1_w_up, m_ffn1_w_down, m_ffn1_post_g, m_mix_pre_g, m_w_in, m_conv_w, m_conv_b, m_dt_bias, m_a_log, m_d_skip, m_ssm_norm_g, m_w_ssm_proj, m_attn_sinks, m_rel_bias_table, m_w_attn_proj, m_w_out, m_mix_post_g, m_ffn2_pre_g, m_ffn2_w_gate, m_ffn2_w_up, m_ffn2_w_down, m_ffn2_post_g, v_ffn1_pre_g, v_ffn1_w_gate, v_ffn1_w_up, v_ffn1_w_down, v_ffn1_post_g, v_mix_pre_g, v_w_in, v_conv_w, v_conv_b, v_dt_bias, v_a_log, v_d_skip, v_ssm_norm_g, v_w_ssm_proj, v_attn_sinks, v_rel_bias_table, v_w_attn_proj, v_w_out, v_mix_post_g, v_ffn2_pre_g, v_ffn2_w_gate, v_ffn2_w_up, v_ffn2_w_down, v_ffn2_post_g):
    args = locals()
    w = {n: args[n] for n in WEIGHTS}
    m = {n: args["m_" + n] for n in WEIGHTS}
    v = {n: args["v_" + n] for n in WEIGHTS}
    batch, seq, D = x.shape
    T = batch * seq
    xi, yi, ci = _me()
    s_me = 2 * xi + yi
    x2 = x.reshape(T, D)
    tgt = loss_target.reshape(T, D)

    def own_slot(parts):
        p = jnp.concatenate([t[0] for t in parts], axis=0).astype(BF16)
        return lax.dynamic_update_slice(lax.empty((N_SHARD,) + p.shape, BF16), p[None], (s_me, 0, 0))

    tr = lambda a: jnp.swapaxes(a, -1, -2)
    (wffn1,) = run_ride(ag_ride([own_slot([tr(ffn1_w_gate), tr(ffn1_w_up), ffn1_w_down])]), name="ag_ffn1")
    col = lambda v: v.reshape(SSM_HEADS, 1)
    d_skip_x = jnp.repeat(d_skip, SSM_HEAD_DIM, axis=1)
    cw_slot = lax.dynamic_update_slice(jnp.zeros((SSM_CONV, SSM_CONV_DIM), F32),
                                       conv_w[0] * (ci == 0).astype(F32), (0, s_me * (SSM_CONV_DIM // N_SHARD)))
    conv_w_full = small_allreduce(cw_slot.reshape(-1, 128), name="ag_conv_w").reshape(SSM_CONV, SSM_CONV_DIM)
    cwb = jnp.concatenate([conv_w_full, conv_b, jnp.zeros((HALO - SSM_CONV - 1, SSM_CONV_DIM), F32)], axis=0)
    conv_pack = jnp.stack([jnp.concatenate([cwb[:, SSM_GW * g:SSM_GW * (g + 1)],
                                            cwb[:, SSM_D_INNER + SSM_STATE * g:SSM_D_INNER + SSM_STATE * (g + 1)],
                                            cwb[:, SSM_D_INNER + SSM_STATE * (SSM_GROUPS + g):
                                                SSM_D_INNER + SSM_STATE * (SSM_GROUPS + g + 1)]], axis=1)
                           for g in range(SSM_GROUPS)])
    vec_pack = jnp.concatenate([d_skip_x, ssm_norm_g], axis=0)
    col_pack = jnp.concatenate([col(dt_bias), col(a_log)], axis=1)

    (h1, n1, gate1, up1, f1), (gin, gmix) = ffn_fwd(
        x2, ffn1_pre_g, wffn1, ffn1_post_g, name="ffn1_fwd",
        ride=ag_ride([own_slot([w_in]), own_slot([w_ssm_proj, w_attn_proj, w_out])]))
    w_in_full = gin.transpose(1, 0, 2).reshape(D, IN_COLS)
    w_gz = w_in_full[:, 0:4096]
    w_xbc = w_in_full[:, 4096:4096 + SSM_CONV_DIM]
    w_dtT = w_in_full[:, 7168:7200].T
    w_qkv = w_in_full[:, 7200:]
    (u, gates, z, xbc, dt_rawT, q, k, vv), (wffn2,) = mix_in_fwd(
        h1, mix_pre_g, w_gz, w_xbc, w_dtT, w_qkv, name="mix_in_fwd",
        ride=ag_ride([own_slot([tr(ffn2_w_gate), tr(ffn2_w_up), ffn2_w_down])]))
    scalars = ssd_scalars(dt_rawT, col(dt_bias), col(a_log), name="ssd_scalars")
    y, ys, states = ssd_fwd(xbc, z, scalars, conv_pack, vec_pack, batch=batch, name="ssd_fwd")
    onehot = _bucket_onehot()
    bias = attn_bias(rel_bias_table.T, onehot, name="attn_bias").reshape(ATTN_Q_HEADS, ATTN_BLOCK, 2 * ATTN_BLOCK)
    bias = jnp.where(attn_window()[None], bias, MASKED)
    o, lse = attn_fwd(q, k, vv, bias, attn_sinks, batch=batch, name="attn_fwd")
    h2, y_ssm, y_attn, mix, merged = mix_out_fwd(ys, o, gates, h1, gmix, mix_post_g, name="mix_out_fwd")
    h3, n3, gate2, up2, f2, dy, loss_parts = ffn_fwd(h2, ffn2_pre_g, wffn2, ffn2_post_g, tgt, name="ffn2_fwd")

    where = jnp.stack([s_me, ci]).astype(jnp.int32)

    def chip_sums(grads, pair, tiles, tag):
        return [rs_add(g, p, where, rt=rt, name=f"rs_add_{tag}{i}") for i, (g, p, rt) in enumerate(zip(grads, pair, tiles))]

    def totals(parts, sums, tiles, tag):
        return [rs_total(p, s, where, rt=rt, name=f"rs_total_{tag}{i}")
                for i, (p, s, rt) in enumerate(zip(parts, sums, tiles))]

    def ffn_grads(n, dgate, dup, a, df, tag):
        d = mm_tn(dgate, n[None], into=(lax.empty(wffn1.shape, F32), 0), name="dw_gate" + tag)
        d = mm_tn(dup, n[None], into=(d, 1), name="dw_up" + tag)
        return [mm_tn(a, df[None], into=(d, 2), name="dw_down" + tag)]

    ffn_tiles, mix_tiles = [352], [256, 256]
    dh2, df2, a2, dgate2, dup2, dg_ffn2_pre, dg_ffn2_post = ffn_bwd(dy, h2, f2, gate2, up2, ffn2_pre_g, ffn2_post_g,
                                                                    wffn2, name="ffn2_bwd")
    d_f2 = ffn_grads(n3, dgate2, dup2, a2, df2, "2")
    (dmix, dyssm, dyattn, dgates, dys, do, dg_mix_post), pair_f2 = mix_out_bwd(
        dh2, mix, y_ssm, y_attn, gates, gmix, mix_post_g, name="mix_out_bwd", ride=pair_ride(d_f2))
    sums_f2 = chip_sums(d_f2, pair_f2, ffn_tiles, "f2")
    dq, dk, dv, dbias, dsinks = attn_bwd(q, k, vv, o, do, lse, bias, attn_sinks, batch=batch, name="attn_bwd")
    dtable = attn_bias_bwd(dbias.reshape(ATTN_Q_HEADS, -1), onehot, name="attn_bias_bwd").T
    (dz, dxs, dbm, dcm, ddtT, acc_xs, acc_bm, acc_cm, acc_head), parts_f2 = ssd_bwd(
        dys, y, xbc, z, dt_rawT, states, scalars, conv_pack, col_pack, vec_pack,
        batch=batch, name="ssd_bwd", ride=chips_ride(sums_f2))
    tot_f2 = totals(parts_f2, sums_f2, ffn_tiles, "f2")
    dmx = mm_tn(ys[None], dyssm[None], a_cols=(N_SHARD, 512), into=(lax.empty(gmix.shape, F32), 0), name="dw_ssm")
    dmx = mm_tn(o[None], dyattn[None], a_cols=(N_SHARD, 256), into=(dmx, 2), name="dw_attn")
    dmx = mm_tn(merged[None], dmix[None], a_cols=(N_SHARD, 256), into=(dmx, 3), name="dw_out")
    ub = u[None]
    din = jnp.concatenate([
        mm_tn(ub, dgates[None], name="dw_in_gates", tn=1024)[0], mm_tn(ub, dz[None], name="dw_in_z", tn=1024)[0],
        mm_tn(ub, dxs[None], name="dw_in_xs", tn=1024)[0], mm_tn(ub, dbm[None], name="dw_in_b")[0],
        mm_tn(ub, dcm[None], name="dw_in_c")[0], mm_rows(ddtT, u, name="dw_in_dt").T,
        mm_tn(ub, dq[None], name="dw_in_q")[0], mm_tn(ub, dk[None], name="dw_in_k")[0],
        mm_tn(ub, dv[None], name="dw_in_v")[0]], axis=1)
    din = din.reshape(D, N_SHARD, IN_COLS // N_SHARD).transpose(1, 0, 2)
    d_mx = [dmx, din]
    (dh1, dg_mix_pre), (pair_mx0, pair_mx1, rffn2) = mix_in_bwd(
        dh2, h1, mix_pre_g, dgates, dz, dxs, dbm, dcm, ddtT, dq, dk, dv, w_gz, w_xbc, w_dtT, w_qkv, name="mix_in_bwd",
        ride=join_rides(pair_ride(d_mx), share_ride(tot_f2)))
    sums_mx = chip_sums(d_mx, [pair_mx0, pair_mx1], mix_tiles, "mx")
    (dx, df1, a1, dgate1, dup1, dg_ffn1_pre, dg_ffn1_post), parts_mx = ffn_bwd(
        dh1, x2, f1, gate1, up1, ffn1_pre_g, ffn1_post_g, wffn1, name="ffn1_bwd", ride=chips_ride(sums_mx))
    rmx, rin = run_ride(share_ride(totals(parts_mx, sums_mx, mix_tiles, "mx")), name="rs_share_mx")
    d_f1 = ffn_grads(n1, dgate1, dup1, a1, df1, "1")
    sums_f1 = chip_sums(d_f1, run_ride(pair_ride(d_f1), name="rs_pair_f1"), ffn_tiles, "f1")
    parts_f1 = run_ride(chips_ride(sums_f1), name="rs_chips_f1")
    (rffn1,) = run_ride(share_ride(totals(parts_f1, sums_f1, ffn_tiles, "f1")), name="rs_share_f1")
    FS = D_FF // N_SHARD
    gw = {
        'ffn1_w_gate': rffn1[0:FS], 'ffn1_w_up': rffn1[FS:2 * FS], 'ffn1_w_down': rffn1[2 * FS:],
        'ffn2_w_gate': rffn2[0:FS], 'ffn2_w_up': rffn2[FS:2 * FS], 'ffn2_w_down': rffn2[2 * FS:],
        'w_ssm_proj': rmx[0:512], 'w_attn_proj': rmx[512:768], 'w_out': rmx[768:1024], 'w_in': rin,
    }

    dconv_w = jnp.concatenate([acc[:, :SSM_CONV].transpose(1, 0, 2).reshape(SSM_CONV, -1)
                               for acc in (acc_xs, acc_bm, acc_cm)], axis=1)
    dconv_b = jnp.concatenate([acc[:, SSM_CONV].reshape(-1) for acc in (acc_xs, acc_bm, acc_cm)])
    small_local = {
        'ffn1_pre_g': dg_ffn1_pre, 'ffn1_post_g': dg_ffn1_post, 'mix_pre_g': dg_mix_pre, 'conv_w': dconv_w,
        'conv_b': dconv_b, 'dt_bias': acc_head[:, :, 0], 'a_log': acc_head[:, :, 1],
        'd_skip': acc_xs[:, SSM_CONV + 2].reshape(SSM_HEADS, SSM_HEAD_DIM).sum(axis=1),
        'ssm_norm_g': acc_xs[:, SSM_CONV + 1].reshape(-1), 'attn_sinks': dsinks, 'rel_bias_table': dtable,
        'mix_post_g': dg_mix_post, 'ffn2_pre_g': dg_ffn2_pre, 'ffn2_post_g': dg_ffn2_post,
    }
    full_shapes = [(SSM_CONV, SSM_CONV_DIM) if n == 'conv_w' else w[n].shape for n in SMALL]
    packed = _pack_rows([small_local[n] for n in SMALL] + [jnp.sum(loss_parts[:, 0, 0])])
    total = small_allreduce(packed, name="allreduce_small")
    *small_g, loss = _unpack_rows(total, full_shapes + [()])
    for n, g in zip(SMALL, small_g):
        gw[n] = g
    gw['conv_w'] = lax.dynamic_slice(gw['conv_w'], (0, s_me * (SSM_CONV_DIM // N_SHARD)),
                                     (SSM_CONV, SSM_CONV_DIM // N_SHARD))[None]

    delta, new_m, new_v = {}, {}, {}
    for n in BIG:
        lay = tr if n.endswith(('w_gate', 'w_up')) else (lambda a: a)
        d_, m_, v_ = adamw(lay(w[n][0]), gw[n], lay(m[n][0]), lay(v[n][0]), name="adamw_" + n, rt=gw[n].shape[0] // 4)
        gw[n] = lay(gw[n])[None]
        delta[n], new_m[n], new_v[n] = lay(d_)[None], lay(m_)[None], lay(v_)[None]
    shapes = [w[n].shape for n in SMALL]
    outs = adamw(_pack_rows([w[n] for n in SMALL]), _pack_rows([gw[n] for n in SMALL]),
                 _pack_rows([m[n] for n in SMALL]), _pack_rows([v[n] for n in SMALL]), name="adamw_small")
    for res, buf in zip((delta, new_m, new_v), outs):
        for n, val in zip(SMALL, _unpack_rows(buf, shapes)):
            res[n] = val
    return (loss, dx.reshape(batch, seq, D), *[gw[n].reshape(w[n].shape) for n in WEIGHTS],
            *[delta[n] for n in WEIGHTS], *[new_m[n] for n in WEIGHTS], *[new_v[n] for n in WEIGHTS])
```

```python
import functools
import math

import jax
import jax.numpy as jnp
import numpy as np
from jax import lax
from jax.experimental import pallas as pl
from jax.experimental.pallas import tpu as pltpu

F32 = jnp.float32
BF16 = jnp.bfloat16

D_MODEL = 1024
D_FF = 2816
N_SHARD = 4
SSM_D_INNER = 2048
SSM_HEAD_DIM = 64
SSM_HEADS = 32
SSM_GROUPS = 4
SSM_HPG = SSM_HEADS // SSM_GROUPS
SSM_GW = SSM_D_INNER // SSM_GROUPS
SSM_STATE = 128
SSM_CONV = 4
SSM_CHUNK = 128
SSM_CONV_DIM = SSM_D_INNER + 2 * SSM_GROUPS * SSM_STATE
ATTN_Q_HEADS = 16
ATTN_KV_HEADS = 4
ATTN_REP = ATTN_Q_HEADS // ATTN_KV_HEADS
ATTN_HEAD_DIM = 64
ATTN_BLOCK = 128
ATTN_Q_DIM = 1024
ATTN_KV_DIM = 256
REL_BUCKETS = 32
REL_MAX_DISTANCE = 128
RMS_EPS = 1e-6
IN_COLS = 8736
ADAM_LR = 0.001
ADAM_B1 = 0.9
ADAM_B2 = 0.999
ADAM_EPS = 1e-08
ADAM_WD = 0.01
ADAM_STEP = 10
HALO = 8

VMEM_LIMIT = 56 * 1024 * 1024


def _cparams(*sem):
    return pltpu.CompilerParams(dimension_semantics=tuple(sem) if sem else None, vmem_limit_bytes=VMEM_LIMIT)


def _dot(a, b):
    return jnp.dot(a, b, preferred_element_type=F32)


def _dot_nt(a, b):
    return lax.dot_general(a, b, (((1,), (1,)), ((), ())), preferred_element_type=F32)


def _dot_tn(a, b):
    return lax.dot_general(a, b, (((0,), (0,)), ((), ())), preferred_element_type=F32)


def _dot_hi(a, b):
    return jnp.dot(a, b, preferred_element_type=F32, precision=lax.Precision.HIGHEST)


def _sigmoid(x):
    return 0.5 * jnp.tanh(0.5 * x) + 0.5


def _resident(shape, index=None):
    index = (0,) * len(shape) if index is None else tuple(index)
    return pl.BlockSpec(shape, lambda *_: index, pipeline_mode=pl.Buffered(1))


def _part(packed, rows, part):
    return _resident((N_SHARD, rows, packed.shape[2]), (0, part, 0))


def _rows(tm, width):
    return pl.BlockSpec((tm, width), lambda i: (i, 0))


class Ride:
    def __init__(self, inputs, out_shapes, aliases, scratch, start, finish):
        self.inputs, self.out_shapes, self.aliases = list(inputs), list(out_shapes), list(aliases)
        self.scratch, self.start, self.finish = list(scratch), start, finish


def join_rides(*rides):
    def cut(refs, sizes):
        out, at = [], 0
        for n in sizes:
            out.append(refs[at:at + n])
            at += n
        return out

    k_in = [len(r.inputs) for r in rides]
    k_out = [len(r.out_shapes) for r in rides]
    k_scr = [len(r.scratch) for r in rides]

    def each(step):
        def run(ins, outs, sems):
            for r, i, o, s in zip(rides, cut(ins, k_in), cut(outs, k_out), cut(sems, k_scr)):
                getattr(r, step)(i, o, s)
        return run

    aliases = [(sum(k_in[:n]) + i, sum(k_out[:n]) + j) for n, r in enumerate(rides) for i, j in r.aliases]
    return Ride([a for r in rides for a in r.inputs], [s for r in rides for s in r.out_shapes], aliases,
                [s for r in rides for s in r.scratch], each("start"), each("finish"))


def _call(body, *, grid, in_specs, args, out_specs, out_shape, name, sem, scratch=(), aliases=None, ride=None):
    aliases = dict(aliases or {})
    if ride is None:
        return pl.pallas_call(body, grid=grid, in_specs=in_specs, out_specs=out_specs, out_shape=out_shape,
                              scratch_shapes=list(scratch), input_output_aliases=aliases,
                              compiler_params=_cparams(*sem), name=name)(*args)
    n_in, n_out, n_scr = len(in_specs), len(out_specs), len(scratch)
    k_in, k_out = len(ride.inputs), len(ride.out_shapes)

    def riding(*refs):
        ins, refs = refs[:n_in], refs[n_in:]
        ex_in, refs = refs[:k_in], refs[k_in:]
        outs, refs = refs[:n_out], refs[n_out:]
        ex_out, refs = refs[:k_out], refs[k_out:]
        scr, ex_scr = refs[:n_scr], refs[n_scr:]
        first = functools.reduce(jnp.logical_and, [pl.program_id(a) == 0 for a in range(len(grid))])
        last = functools.reduce(jnp.logical_and, [pl.program_id(a) == grid[a] - 1 for a in range(len(grid))])

        @pl.when(first)
        def _():
            ride.start(ex_in, ex_out, ex_scr)

        body(*ins, *outs, *scr)

        @pl.when(last)
        def _():
            ride.finish(ex_in, ex_out, ex_scr)

    aliases.update({n_in + i: n_out + j for i, j in ride.aliases})
    res = pl.pallas_call(
        riding, grid=grid, in_specs=list(in_specs) + [ANY] * k_in, out_specs=list(out_specs) + [ANY] * k_out,
        out_shape=list(out_shape) + ride.out_shapes, scratch_shapes=list(scratch) + ride.scratch,
        input_output_aliases=aliases, compiler_params=_cparams(*["arbitrary"] * len(grid)), name=name,
    )(*args, *ride.inputs)
    return res[:n_out], res[n_out:]


def run_ride(ride, *, name):
    k_in = len(ride.inputs)

    def body(*refs):
        ex_in, ex_out, sems = refs[:k_in], refs[k_in:k_in + len(ride.out_shapes)], refs[k_in + len(ride.out_shapes):]
        ride.start(ex_in, ex_out, sems)
        ride.finish(ex_in, ex_out, sems)

    return pl.pallas_call(body, in_specs=[ANY] * k_in, out_specs=[ANY] * len(ride.out_shapes),
                          out_shape=ride.out_shapes, scratch_shapes=ride.scratch,
                          input_output_aliases=dict(ride.aliases), name=name)(*ride.inputs)


def ffn_fwd(h, g_pre, wffn, g_post, target=None, *, name, tm=512, ride=None):
    T, D = h.shape
    NS, FS = N_SHARD, wffn.shape[1] // 3
    with_loss = target is not None
    nt = T // tm

    def body(*refs):
        if with_loss:
            (h_ref, gpre_ref, wg_ref, wu_ref, wd_ref, gpost_ref, tgt_ref,
             hout_ref, n_ref, gate_ref, up_ref, f_ref, dy_ref, loss_ref) = refs
        else:
            (h_ref, gpre_ref, wg_ref, wu_ref, wd_ref, gpost_ref,
             hout_ref, n_ref, gate_ref, up_ref, f_ref) = refs
        hh = h_ref[...]
        r = lax.rsqrt(jnp.mean(hh * hh, axis=-1, keepdims=True) + RMS_EPS)
        n = (hh * r * gpre_ref[...]).astype(BF16)
        n_ref[...] = n
        acc = jnp.zeros((tm, D), F32)
        for s in range(NS):
            gate = _dot_nt(n, wg_ref[s])
            up = _dot_nt(n, wu_ref[s])
            gate_ref[s] = gate.astype(BF16)
            up_ref[s] = up.astype(BF16)
            a = (gate * _sigmoid(gate) * up).astype(BF16)
            acc = acc + _dot(a, wd_ref[s])
        f_ref[...] = acc
        r2 = lax.rsqrt(jnp.mean(acc * acc, axis=-1, keepdims=True) + RMS_EPS)
        out = hh + 0.5 * (acc * r2 * gpost_ref[...])
        hout_ref[...] = out
        if with_loss:
            e = out - tgt_ref[...]
            dy_ref[...] = e * (1.0 / D)
            loss_ref[...] = jnp.full((1, 8, 128), 0.5 / D, F32) * jnp.sum(e * e)

    in_specs = [_rows(tm, D), _resident((1, D)), _part(wffn, FS, 0), _part(wffn, FS, 1), _part(wffn, FS, 2),
                _resident((1, D))]
    args = [h, g_pre, wffn, wffn, wffn, g_post]
    out_shape = [jax.ShapeDtypeStruct((T, D), F32), jax.ShapeDtypeStruct((T, D), BF16),
                 jax.ShapeDtypeStruct((NS, T, FS), BF16), jax.ShapeDtypeStruct((NS, T, FS), BF16),
                 jax.ShapeDtypeStruct((T, D), F32)]
    seg = pl.BlockSpec((NS, tm, FS), lambda i: (0, i, 0))
    out_specs = [_rows(tm, D), _rows(tm, D), seg, seg, _rows(tm, D)]
    if with_loss:
        in_specs.append(_rows(tm, D))
        args.append(target)
        out_shape += [jax.ShapeDtypeStruct((T, D), F32), jax.ShapeDtypeStruct((nt, 8, 128), F32)]
        out_specs += [_rows(tm, D), pl.BlockSpec((1, 8, 128), lambda i: (i, 0, 0))]
    return _call(body, grid=(nt,), in_specs=in_specs, args=args, out_specs=out_specs, out_shape=out_shape,
                 sem=("parallel",), name=name, ride=ride)


def ffn_bwd(dout, h, f, gate, up, g_pre, g_post, wffn, *, name, tm=256, ride=None):
    T, D = h.shape
    NS, FS = N_SHARD, wffn.shape[1] // 3
    nt = T // tm

    def body(dout_ref, h_ref, f_ref, gate_ref, up_ref, gpre_ref, gpost_ref, wg_ref, wu_ref, wd_ref,
             dh_ref, df_ref, a_ref, dgate_ref, dup_ref, dgpre_ref, dgpost_ref):
        @pl.when(pl.program_id(0) == 0)
        def _():
            dgpre_ref[...] = jnp.zeros_like(dgpre_ref)
            dgpost_ref[...] = jnp.zeros_like(dgpost_ref)

        do = dout_ref[...]
        ff = f_ref[...]
        d_fn = 0.5 * do
        r2 = lax.rsqrt(jnp.mean(ff * ff, axis=-1, keepdims=True) + RMS_EPS)
        dgpost_ref[...] += jnp.sum(d_fn * ff * r2, axis=0, keepdims=True)
        t = d_fn * gpost_ref[...]
        df = r2 * t - ff * (r2 * r2 * r2 * jnp.mean(t * ff, axis=-1, keepdims=True))
        dfb = df.astype(BF16)
        df_ref[...] = dfb
        dn = jnp.zeros((tm, D), F32)
        for s in range(NS):
            da = _dot_nt(dfb, wd_ref[s])
            g = gate_ref[s].astype(F32)
            u = up_ref[s].astype(F32)
            sg = _sigmoid(g)
            silu = g * sg
            a_ref[s] = (silu * u).astype(BF16)
            dgt = (da * u * (sg * (1.0 + g * (1.0 - sg)))).astype(BF16)
            dupv = (da * silu).astype(BF16)
            dgate_ref[s] = dgt
            dup_ref[s] = dupv
            dn = dn + _dot(dgt, wg_ref[s]) + _dot(dupv, wu_ref[s])
        hh = h_ref[...]
        r1 = lax.rsqrt(jnp.mean(hh * hh, axis=-1, keepdims=True) + RMS_EPS)
        dgpre_ref[...] += jnp.sum(dn * hh * r1, axis=0, keepdims=True)
        t = dn * gpre_ref[...]
        dh_ref[...] = do + r1 * t - hh * (r1 * r1 * r1 * jnp.mean(t * hh, axis=-1, keepdims=True))

    seg = pl.BlockSpec((NS, tm, FS), lambda i: (0, i, 0))
    acc = pl.BlockSpec((1, D), lambda i: (0, 0))
    return _call(
        body, grid=(nt,),
        in_specs=[_rows(tm, D), _rows(tm, D), _rows(tm, D), seg, seg, _resident((1, D)), _resident((1, D)),
                  _part(wffn, FS, 0), _part(wffn, FS, 1), _part(wffn, FS, 2)],
        args=[dout, h, f, gate, up, g_pre, g_post, wffn, wffn, wffn],
        out_specs=[_rows(tm, D), _rows(tm, D), seg, seg, seg, acc, acc],
        out_shape=[jax.ShapeDtypeStruct((T, D), F32), jax.ShapeDtypeStruct((T, D), BF16),
                   jax.ShapeDtypeStruct((NS, T, FS), BF16), jax.ShapeDtypeStruct((NS, T, FS), BF16),
                   jax.ShapeDtypeStruct((NS, T, FS), BF16),
                   jax.ShapeDtypeStruct((1, D), F32), jax.ShapeDtypeStruct((1, D), F32)],
        sem=("arbitrary",), name=name, ride=ride)


def mm_tn(a, g, *, name, tt=2048, tn=None, a_cols=None, into=None):
    Ba, T, _ = a.shape
    Bg, _, N = g.shape
    B, K = a_cols if a_cols else (max(Ba, Bg), a.shape[2])
    tn = N if tn is None else tn
    tt = min(tt, T)
    nsteps = T // tt

    def body(*refs):
        a_ref, g_ref, o_ref = refs[0], refs[1], refs[-1]

        @pl.when(pl.program_id(2) == 0)
        def _():
            o_ref[...] = jnp.zeros_like(o_ref)

        o_ref[0] += _dot_tn(a_ref[0], g_ref[0].astype(BF16))

    if a_cols:
        a_map = lambda b, j, t: (0, t, b)
    else:
        a_map = (lambda b, j, t: (b, t, 0)) if Ba > 1 else (lambda b, j, t: (0, t, 0))
    in_specs = [pl.BlockSpec((1, tt, K), a_map),
                pl.BlockSpec((1, tt, tn), (lambda b, j, t: (b, t, j)) if Bg > 1 else (lambda b, j, t: (0, t, j)))]
    args = [a, g]
    if into is None:
        out_shape, part, aliases = jax.ShapeDtypeStruct((B, K, N), F32), 0, {}
    else:
        buf, part = into
        out_shape, aliases = jax.ShapeDtypeStruct(buf.shape, F32), {2: 0}
        in_specs.append(ANY)
        args.append(buf)
    return pl.pallas_call(
        body, grid=(B, N // tn, nsteps), in_specs=in_specs,
        out_specs=pl.BlockSpec((1, K, tn), lambda b, j, t: (b, part, j)),
        out_shape=out_shape, input_output_aliases=aliases,
        compiler_params=_cparams("parallel", "parallel", "arbitrary"), name=name)(*args)


def mix_in_fwd(h, g, w_gz, w_xbc, w_dtT, w_qkv, *, name, tm=256, ride=None):
    T, D = h.shape
    nt = T // tm
    CB = 1024

    def body(h_ref, g_ref, wgz_ref, wxbc_ref, wdtT_ref, wqkv_ref,
             u_ref, gates_ref, z_ref, xbc_ref, dtT_ref, q_ref, k_ref, v_ref):
        hh = h_ref[...]
        r = lax.rsqrt(jnp.mean(hh * hh, axis=-1, keepdims=True) + RMS_EPS)
        u = (hh * r * g_ref[...]).astype(BF16)
        u_ref[...] = u
        for cb in range(0, 2048, CB):
            gates_ref[:, cb:cb + CB] = _dot(u, wgz_ref[:, cb:cb + CB]).astype(BF16)
            z_ref[:, cb:cb + CB] = _dot(u, wgz_ref[:, 2048 + cb:2048 + cb + CB])
        for cb in range(0, SSM_CONV_DIM, CB):
            xbc_ref[:, cb:cb + CB] = _dot(u, wxbc_ref[:, cb:cb + CB])
        dtT_ref[...] = _dot_nt(wdtT_ref[...], u)
        q_ref[...] = (_dot(u, wqkv_ref[:, 0:ATTN_Q_DIM]) * ATTN_SCALE).astype(BF16)
        k_ref[...] = _dot(u, wqkv_ref[:, ATTN_Q_DIM:ATTN_Q_DIM + ATTN_KV_DIM]).astype(BF16)
        v_ref[...] = _dot(u, wqkv_ref[:, ATTN_Q_DIM + ATTN_KV_DIM:]).astype(BF16)

    sds = jax.ShapeDtypeStruct
    return _call(
        body, grid=(nt,),
        in_specs=[_rows(tm, D), _resident((1, D)), _resident(w_gz.shape), _resident(w_xbc.shape),
                  _resident(w_dtT.shape), _resident(w_qkv.shape)],
        args=[h, g, w_gz, w_xbc, w_dtT, w_qkv],
        out_specs=[_rows(tm, D), _rows(tm, 2048), _rows(tm, 2048), _rows(tm, SSM_CONV_DIM),
                   pl.BlockSpec((SSM_HEADS, tm), lambda i: (0, i)),
                   _rows(tm, ATTN_Q_DIM), _rows(tm, ATTN_KV_DIM), _rows(tm, ATTN_KV_DIM)],
        out_shape=[sds((T, D), BF16), sds((T, 2048), BF16), sds((T, 2048), F32), sds((T, SSM_CONV_DIM), F32),
                   sds((SSM_HEADS, T), F32),
                   sds((T, ATTN_Q_DIM), BF16), sds((T, ATTN_KV_DIM), BF16), sds((T, ATTN_KV_DIM), BF16)],
        sem=("parallel",), name=name, ride=ride)


def _softplus(x):
    return jnp.maximum(x, 0.0) + jnp.log(1.0 + jnp.exp(-jnp.abs(x)))


def _iota(shape, axis):
    return lax.broadcasted_iota(jnp.int32, shape, axis)


def _conv_pre(x_ref, halo_ref, w_ref, b_ref, xp_ref, first):
    Q = SSM_CHUNK
    halo = jnp.where(first, 0.0, halo_ref[...])
    xp_ref[0:HALO, :] = halo
    xp_ref[HALO:HALO + Q, :] = x_ref[...]
    pre = b_ref[...] + w_ref[3:4, :] * xp_ref[HALO:HALO + Q, :]
    for k in range(SSM_CONV - 1):
        pre = pre + w_ref[k:k + 1, :] * xp_ref[pl.ds(HALO - 3 + k, Q), :]
    return pre


def _ssd_specs(nc):
    Q, GW, N = SSM_CHUNK, SSM_GW, SSM_STATE
    nb_xs = SSM_D_INNER // N
    nb_c = nb_xs + SSM_GROUPS

    def rb(cmap):
        def row(b, c, g):
            return b * nc + cmap(c)
        return row

    def specs(cmap):
        row = rb(cmap)
        hrow = lambda b, c, g: jnp.maximum(row(b, c, g) * (Q // HALO) - 1, 0)
        return dict(
            xs=pl.BlockSpec((Q, GW), lambda b, c, g: (row(b, c, g), g)),
            bm=pl.BlockSpec((Q, N), lambda b, c, g: (row(b, c, g), nb_xs + g)),
            cm=pl.BlockSpec((Q, N), lambda b, c, g: (row(b, c, g), nb_c + g)),
            xs_halo=pl.BlockSpec((HALO, GW), lambda b, c, g: (hrow(b, c, g), g)),
            bm_halo=pl.BlockSpec((HALO, N), lambda b, c, g: (hrow(b, c, g), nb_xs + g)),
            cm_halo=pl.BlockSpec((HALO, N), lambda b, c, g: (hrow(b, c, g), nb_c + g)),
            grp=pl.BlockSpec((Q, GW), lambda b, c, g: (row(b, c, g), g)),
            dtT_g=pl.BlockSpec((SSM_HPG, Q), lambda b, c, g: (g, row(b, c, g))),
            conv=pl.BlockSpec((1, HALO, GW + 2 * N), lambda b, c, g: (g, 0, 0)),
            cols=pl.BlockSpec((1, Q, 4 * SSM_HPG), lambda b, c, g: (g, row(b, c, g), 0)),
            decay=pl.BlockSpec((1, SSM_HPG, 1), lambda b, c, g: (row(b, c, g), g, 0)),
            vec2=pl.BlockSpec((2, GW), lambda b, c, g: (0, g)),
            col2=pl.BlockSpec((SSM_HPG, 2), lambda b, c, g: (g, 0)),
            pairs=pl.BlockSpec((1, 1, N, GW), lambda b, c, g: (row(b, c, g), g, 0, 0)),
        )
    return specs


def _attn_specs(nb):
    BLK = ATTN_BLOCK

    def specs(last):
        def cur(b, n):
            return b * nb + (n if last is None else jnp.minimum(n, nb - 1))

        def prev(b, n):
            return b * nb + jnp.maximum((n if last is None else jnp.minimum(n, nb - 1)) - 1, 0)
        return cur, prev
    return specs


MASKED = -1e30
ATTN_SCALE = ATTN_HEAD_DIM ** -0.5


def attn_window():
    i = np.arange(ATTN_BLOCK)[:, None]
    j = np.arange(2 * ATTN_BLOCK)[None, :]
    return (j > i) & (j <= i + ATTN_BLOCK)


def _attn_group(kk, q_ref, bias_ref, sink_ref):
    BLK, HD = ATTN_BLOCK, ATTN_HEAD_DIM
    heads = range(ATTN_REP * kk, ATTN_REP * (kk + 1))
    qg = jnp.concatenate([q_ref[:, HD * hd:HD * (hd + 1)] for hd in heads], axis=0)
    bias_p = jnp.concatenate([bias_ref[hd, :, 0:BLK] for hd in heads], axis=0)
    bias_c = jnp.concatenate([bias_ref[hd, :, BLK:2 * BLK] for hd in heads], axis=0)
    sink = jnp.concatenate([jnp.broadcast_to(sink_ref[0:1, hd:hd + 1], (BLK, 1)) for hd in heads], axis=0)
    return qg, bias_p, bias_c, sink


def attn_bias(table_t, onehot, *, name):
    def body(t_ref, f_ref, o_ref):
        o_ref[...] = _dot_hi(t_ref[...], f_ref[...])
    return pl.pallas_call(body, out_shape=jax.ShapeDtypeStruct((ATTN_Q_HEADS, onehot.shape[1]), F32),
                          compiler_params=_cparams(), name=name)(table_t, onehot)


def attn_bias_bwd(dbias, onehot, *, name):
    def body(d_ref, f_ref, o_ref):
        o_ref[...] = lax.dot_general(d_ref[...], f_ref[...], (((1,), (1,)), ((), ())), preferred_element_type=F32,
                                     precision=lax.Precision.HIGHEST)
    return pl.pallas_call(body, out_shape=jax.ShapeDtypeStruct((ATTN_Q_HEADS, REL_BUCKETS), F32),
                          compiler_params=_cparams(), name=name)(dbias, onehot)


def attn_fwd(q, k, v, bias, sinks, *, batch, name):
    T = q.shape[0]
    BLK, HD = ATTN_BLOCK, ATTN_HEAD_DIM
    nb = T // batch // BLK
    cur, prev = _attn_specs(nb)(None)

    def body(q_ref, kc_ref, kp_ref, vc_ref, vp_ref, bias_ref, sink_ref, o_ref, lse_ref):
        n = pl.program_id(1)
        for kk in range(ATTN_KV_HEADS):
            ks = slice(HD * kk, HD * (kk + 1))
            kc, kp, vc, vp = kc_ref[:, ks], kp_ref[:, ks], vc_ref[:, ks], vp_ref[:, ks]
            qg, bias_p, bias_c, sink = _attn_group(kk, q_ref, bias_ref, sink_ref)
            lp = jnp.where(n > 0, _dot_nt(qg, kp) + bias_p, MASKED)
            lc = _dot_nt(qg, kc) + bias_c
            mx = jnp.maximum(jnp.max(jnp.maximum(lp, lc), axis=-1, keepdims=True), sink)
            pp = jnp.exp(lp - mx)
            pc = jnp.exp(lc - mx)
            den = jnp.sum(pp + pc, axis=-1, keepdims=True) + jnp.exp(sink - mx)
            o = ((_dot(pp.astype(BF16), vp) + _dot(pc.astype(BF16), vc)) * (1.0 / den)).astype(BF16)
            lse = mx + jnp.log(den)
            for r in range(ATTN_REP):
                hd = ATTN_REP * kk + r
                o_ref[:, HD * hd:HD * (hd + 1)] = o[BLK * r:BLK * (r + 1)]
                lse_ref[:, hd:hd + 1] = lse[BLK * r:BLK * (r + 1)]

    sds = jax.ShapeDtypeStruct
    return pl.pallas_call(
        body, grid=(batch, nb),
        in_specs=[pl.BlockSpec((BLK, ATTN_Q_DIM), lambda b, n: (cur(b, n), 0)),
                  pl.BlockSpec((BLK, ATTN_KV_DIM), lambda b, n: (cur(b, n), 0)),
                  pl.BlockSpec((BLK, ATTN_KV_DIM), lambda b, n: (prev(b, n), 0)),
                  pl.BlockSpec((BLK, ATTN_KV_DIM), lambda b, n: (cur(b, n), 0)),
                  pl.BlockSpec((BLK, ATTN_KV_DIM), lambda b, n: (prev(b, n), 0)),
                  pl.BlockSpec((ATTN_Q_HEADS, BLK, 2 * BLK), lambda b, n: (0, 0, 0)),
                  pl.BlockSpec((1, ATTN_Q_HEADS), lambda b, n: (0, 0))],
        out_specs=[pl.BlockSpec((BLK, ATTN_Q_DIM), lambda b, n: (cur(b, n), 0)),
                   pl.BlockSpec((BLK, ATTN_Q_HEADS), lambda b, n: (cur(b, n), 0))],
        out_shape=[sds((T, ATTN_Q_DIM), BF16), sds((T, ATTN_Q_HEADS), F32)],
        compiler_params=_cparams("parallel", "parallel"), name=name)(q, k, k, v, v, bias, sinks)


def _proj_specs(wmix):
    return [_part(wmix, 512, 0), _part(wmix, 256, 2), _part(wmix, 256, 3)]


def _natural(w_ref):
    return w_ref[...].reshape(-1, w_ref.shape[2])


def mix_out_fwd(ys, o, gates, h, wmix, g_post, *, name, tm=512):
    T, D = h.shape
    nt = T // tm

    def body(ys_ref, o_ref, gates_ref, h_ref, wssm_ref, wattn_ref, wout_ref, g_ref,
             hout_ref, yssm_ref, yattn_ref, mix_ref, merged_ref):
        y_ssm = _dot(ys_ref[...], _natural(wssm_ref))
        y_attn = _dot(o_ref[...], _natural(wattn_ref))
        yssm_ref[...] = y_ssm.astype(BF16)
        yattn_ref[...] = y_attn.astype(BF16)
        merged = (_sigmoid(gates_ref[:, 0:D].astype(F32)) * y_ssm
                  + _sigmoid(gates_ref[:, D:2 * D].astype(F32)) * y_attn).astype(BF16)
        merged_ref[...] = merged
        mix = _dot(merged, _natural(wout_ref))
        mix_ref[...] = mix.astype(BF16)
        r = lax.rsqrt(jnp.mean(mix * mix, axis=-1, keepdims=True) + RMS_EPS)
        hout_ref[...] = h_ref[...] + mix * r * g_ref[...]

    sds = jax.ShapeDtypeStruct
    return pl.pallas_call(
        body, grid=(nt,),
        in_specs=[_rows(tm, SSM_D_INNER), _rows(tm, ATTN_Q_DIM), _rows(tm, 2 * D), _rows(tm, D),
                  *_proj_specs(wmix), _resident((1, D))],
        out_specs=[_rows(tm, D)] * 5,
        out_shape=[sds((T, D), F32), sds((T, D), BF16), sds((T, D), BF16), sds((T, D), BF16), sds((T, D), BF16)],
        compiler_params=_cparams("parallel"), name=name)(ys, o, gates, h, wmix, wmix, wmix, g_post)


def mix_out_bwd(dh, mix, y_ssm, y_attn, gates, wmix, g_post, *, name, tm=256, ride=None):
    T, D = dh.shape
    nt = T // tm

    def body(dh_ref, mix_ref, yssm_ref, yattn_ref, gates_ref, wssm_ref, wattn_ref, wout_ref, g_ref,
             dmix_ref, dyssm_ref, dyattn_ref, dgates_ref, dys_ref, do_ref, dg_ref):
        @pl.when(pl.program_id(0) == 0)
        def _():
            dg_ref[...] = jnp.zeros_like(dg_ref)

        do = dh_ref[...]
        mix = mix_ref[...].astype(F32)
        r = lax.rsqrt(jnp.mean(mix * mix, axis=-1, keepdims=True) + RMS_EPS)
        dg_ref[...] += jnp.sum(do * mix * r, axis=0, keepdims=True)
        t = do * g_ref[...]
        dmix = (r * t - mix * (r * r * r * jnp.mean(t * mix, axis=-1, keepdims=True))).astype(BF16)
        dmix_ref[...] = dmix
        dmerged = _dot_nt(dmix, _natural(wout_ref))
        s1 = _sigmoid(gates_ref[:, 0:D].astype(F32))
        s2 = _sigmoid(gates_ref[:, D:2 * D].astype(F32))
        dyssm = (dmerged * s1).astype(BF16)
        dyattn = (dmerged * s2).astype(BF16)
        dyssm_ref[...] = dyssm
        dyattn_ref[...] = dyattn
        dgates_ref[:, 0:D] = (dmerged * yssm_ref[...].astype(F32) * (s1 * (1.0 - s1))).astype(BF16)
        dgates_ref[:, D:2 * D] = (dmerged * yattn_ref[...].astype(F32) * (s2 * (1.0 - s2))).astype(BF16)
        dys_ref[...] = _dot_nt(dyssm, _natural(wssm_ref))
        do_ref[...] = _dot_nt(dyattn, _natural(wattn_ref)).astype(BF16)

    sds = jax.ShapeDtypeStruct
    return _call(
        body, grid=(nt,),
        in_specs=[_rows(tm, D), _rows(tm, D), _rows(tm, D), _rows(tm, D), _rows(tm, 2 * D),
                  *_proj_specs(wmix), _resident((1, D))],
        args=[dh, mix, y_ssm, y_attn, gates, wmix, wmix, wmix, g_post],
        out_specs=[_rows(tm, D), _rows(tm, D), _rows(tm, D), _rows(tm, 2 * D), _rows(tm, SSM_D_INNER),
                   _rows(tm, ATTN_Q_DIM), pl.BlockSpec((1, D), lambda i: (0, 0))],
        out_shape=[sds((T, D), BF16), sds((T, D), BF16), sds((T, D), BF16), sds((T, 2 * D), BF16),
                   sds((T, SSM_D_INNER), F32), sds((T, ATTN_Q_DIM), BF16), sds((1, D), F32)],
        sem=("arbitrary",), name=name, ride=ride)


def attn_bwd(q, k, v, o, do, lse, bias, sinks, *, batch, name):
    T = q.shape[0]
    BLK, HD = ATTN_BLOCK, ATTN_HEAD_DIM
    nb = T // batch // BLK
    cur, prev = _attn_specs(nb)(nb)
    scale = HD ** -0.5

    def body(q_ref, kc_ref, kp_ref, vc_ref, vp_ref, o_ref, do_ref, lse_ref, bias_ref, sink_ref,
             dq_ref, dk_ref, dv_ref, dbias_ref, dsink_ref, ck, cv):
        b = pl.program_id(0)
        n = pl.program_id(1)

        @pl.when(jnp.logical_and(b == 0, n == 0))
        def _():
            dbias_ref[...] = jnp.zeros_like(dbias_ref)
            dsink_ref[...] = jnp.zeros_like(dsink_ref)

        @pl.when(n == 0)
        def _():
            ck[...] = jnp.zeros_like(ck)
            cv[...] = jnp.zeros_like(cv)

        @pl.when(n == nb)
        def _():
            dk_ref[...] = ck[...].astype(BF16)
            dv_ref[...] = cv[...].astype(BF16)

        @pl.when(n < nb)
        def _():
            lane16 = _iota((1, ATTN_Q_HEADS), 1)
            dsink = jnp.zeros((1, ATTN_Q_HEADS), F32)
            for kk in range(ATTN_KV_HEADS):
                ks = slice(HD * kk, HD * (kk + 1))
                kc, kp, vc, vp = kc_ref[:, ks], kp_ref[:, ks], vc_ref[:, ks], vp_ref[:, ks]
                heads = range(ATTN_REP * kk, ATTN_REP * (kk + 1))
                qg, bias_p, bias_c, sink = _attn_group(kk, q_ref, bias_ref, sink_ref)
                dog = jnp.concatenate([do_ref[:, HD * hd:HD * (hd + 1)] for hd in heads], axis=0)
                og = jnp.concatenate([o_ref[:, HD * hd:HD * (hd + 1)] for hd in heads], axis=0)
                lse = jnp.concatenate([lse_ref[:, hd:hd + 1] for hd in heads], axis=0)
                lp = jnp.where(n > 0, _dot_nt(qg, kp) + bias_p, MASKED)
                lc = _dot_nt(qg, kc) + bias_c
                pp = jnp.exp(lp - lse)
                pc = jnp.exp(lc - lse)
                delta = jnp.sum(dog.astype(F32) * og.astype(F32), axis=-1, keepdims=True)
                dlp = pp * (_dot_nt(dog, vp) - delta)
                dlc = pc * (_dot_nt(dog, vc) - delta)
                sd = jnp.exp(sink - lse) * delta
                dlpb = dlp.astype(BF16)
                dlcb = dlc.astype(BF16)
                dqg = ((_dot(dlpb, kp) + _dot(dlcb, kc)) * scale).astype(BF16)
                for r, hd in enumerate(heads):
                    rows = slice(BLK * r, BLK * (r + 1))
                    dsink = dsink + jnp.where(lane16 == hd, -jnp.sum(sd[rows], axis=0, keepdims=True), 0.0)
                    dbias_ref[hd, :, 0:BLK] += dlp[rows]
                    dbias_ref[hd, :, BLK:2 * BLK] += dlc[rows]
                    dq_ref[:, HD * hd:HD * (hd + 1)] = dqg[rows]
                dk_ref[:, ks] = (ck[:, ks] + _dot_tn(dlpb, qg)).astype(BF16)
                dv_ref[:, ks] = (cv[:, ks] + _dot_tn(pp.astype(BF16), dog)).astype(BF16)
                ck[:, ks] = _dot_tn(dlcb, qg)
                cv[:, ks] = _dot_tn(pc.astype(BF16), dog)
            dsink_ref[...] += dsink

    sds = jax.ShapeDtypeStruct
    qspec = pl.BlockSpec((BLK, ATTN_Q_DIM), lambda b, n: (cur(b, n), 0))
    cspec = pl.BlockSpec((BLK, ATTN_KV_DIM), lambda b, n: (cur(b, n), 0))
    pspec = pl.BlockSpec((BLK, ATTN_KV_DIM), lambda b, n: (prev(b, n), 0))
    late = pl.BlockSpec((BLK, ATTN_KV_DIM), lambda b, n: (b * nb + jnp.maximum(n - 1, 0), 0))
    return pl.pallas_call(
        body, grid=(batch, nb + 1),
        in_specs=[qspec, cspec, pspec, cspec, pspec, qspec, qspec,
                  pl.BlockSpec((BLK, ATTN_Q_HEADS), lambda b, n: (cur(b, n), 0)),
                  pl.BlockSpec((ATTN_Q_HEADS, BLK, 2 * BLK), lambda b, n: (0, 0, 0)),
                  pl.BlockSpec((1, ATTN_Q_HEADS), lambda b, n: (0, 0))],
        out_specs=[qspec, late, late,
                   pl.BlockSpec((ATTN_Q_HEADS, BLK, 2 * BLK), lambda b, n: (0, 0, 0)),
                   pl.BlockSpec((1, ATTN_Q_HEADS), lambda b, n: (0, 0))],
        out_shape=[sds((T, ATTN_Q_DIM), BF16), sds((T, ATTN_KV_DIM), BF16), sds((T, ATTN_KV_DIM), BF16),
                   sds((ATTN_Q_HEADS, BLK, 2 * BLK), F32), sds((1, ATTN_Q_HEADS), F32)],
        scratch_shapes=[pltpu.VMEM((BLK, ATTN_KV_DIM), F32), pltpu.VMEM((BLK, ATTN_KV_DIM), F32)],
        compiler_params=_cparams("arbitrary", "arbitrary"), name=name)(q, k, k, v, v, o, do, lse, bias, sinks)


def _conv_bwd(dxc, pre, xp_ref, w_ref, carry_ref, acc_ref, dp_ref, g, last):
    Q = SSM_CHUNK
    sg = _sigmoid(pre)
    dpre = dxc * (sg * (1.0 + pre * (1.0 - sg)))
    dp_ref[0:Q, :] = dpre
    dp_ref[Q:Q + HALO, :] = carry_ref[g]
    carry_ref[g] = dpre[0:HALO, :]
    rows = [jnp.sum(dpre * xp_ref[pl.ds(HALO - 3 + k, Q), :], axis=0, keepdims=True) for k in range(SSM_CONV)]
    rows.append(jnp.sum(dpre, axis=0, keepdims=True))
    rows.append(jnp.zeros((HALO - SSM_CONV - 1, dpre.shape[1]), F32))
    acc_ref[g] += jnp.concatenate(rows, axis=0)
    dx = w_ref[3:4, :] * dpre
    for k in range(SSM_CONV - 1):
        dx = dx + w_ref[k:k + 1, :] * dp_ref[pl.ds(3 - k, Q), :]
    return dx


def mix_in_bwd(dh, h, g, dgates, dz, dxs, dbm, dcm, ddtT, dq, dk, dv, w_gz, w_xbc, w_dtT, w_qkv, *, name, tm=512,
               ride=None):
    T, D = h.shape
    nt = T // tm
    GN = SSM_GROUPS * SSM_STATE

    def body(dh_ref, h_ref, g_ref, dgates_ref, dz_ref, dxs_ref, dbm_ref, dcm_ref, ddt_ref, dq_ref, dk_ref, dv_ref,
             wgz_ref, wxbc_ref, wdt_ref, wqkv_ref, dhin_ref, dg_ref):
        @pl.when(pl.program_id(0) == 0)
        def _():
            dg_ref[...] = jnp.zeros_like(dg_ref)

        du = _dot_nt(dgates_ref[...], wgz_ref[:, 0:2048])
        du = du + _dot_nt(dz_ref[...], wgz_ref[:, 2048:4096])
        du = du + _dot_nt(dxs_ref[...], wxbc_ref[:, 0:SSM_D_INNER])
        du = du + _dot_nt(dbm_ref[...], wxbc_ref[:, SSM_D_INNER:SSM_D_INNER + GN])
        du = du + _dot_nt(dcm_ref[...], wxbc_ref[:, SSM_D_INNER + GN:])
        du = du + _dot_tn(ddt_ref[...].astype(BF16), wdt_ref[...])
        du = du + _dot_nt(dq_ref[...], wqkv_ref[:, 0:ATTN_Q_DIM])
        du = du + _dot_nt(dk_ref[...], wqkv_ref[:, ATTN_Q_DIM:ATTN_Q_DIM + ATTN_KV_DIM])
        du = du + _dot_nt(dv_ref[...], wqkv_ref[:, ATTN_Q_DIM + ATTN_KV_DIM:])
        hh = h_ref[...]
        r = lax.rsqrt(jnp.mean(hh * hh, axis=-1, keepdims=True) + RMS_EPS)
        dg_ref[...] += jnp.sum(du * hh * r, axis=0, keepdims=True)
        t = du * g_ref[...]
        dhin_ref[...] = dh_ref[...] + r * t - hh * (r * r * r * jnp.mean(t * hh, axis=-1, keepdims=True))

    sds = jax.ShapeDtypeStruct
    return _call(
        body, grid=(nt,),
        in_specs=[_rows(tm, D), _rows(tm, D), _resident((1, D)), _rows(tm, 2048), _rows(tm, 2048), _rows(tm, SSM_D_INNER),
                  _rows(tm, GN), _rows(tm, GN), pl.BlockSpec((SSM_HEADS, tm), lambda i: (0, i)),
                  _rows(tm, ATTN_Q_DIM), _rows(tm, ATTN_KV_DIM),
                  _rows(tm, ATTN_KV_DIM), _resident(w_gz.shape), _resident(w_xbc.shape), _resident(w_dtT.shape),
                  _resident(w_qkv.shape)],
        args=[dh, h, g, dgates, dz, dxs, dbm, dcm, ddtT, dq, dk, dv, w_gz, w_xbc, w_dtT, w_qkv],
        out_specs=[_rows(tm, D), pl.BlockSpec((1, D), lambda i: (0, 0))],
        out_shape=[sds((T, D), F32), sds((1, D), F32)],
        sem=("arbitrary",), name=name, ride=ride)


PAIRS = SSM_HPG // 2
PW = 2 * SSM_HEAD_DIM


def ssd_scalars(dt_rawT, dt_bias, a_log, *, name, chunks=4):
    H, T = dt_rawT.shape
    Q, G, HPG = SSM_CHUNK, SSM_GROUPS, SSM_HPG
    span = Q * chunks

    def body(dtT_ref, dtb_ref, alog_ref, cols_ref, acsT_ref, dec_ref):
        aT = -jnp.exp(alog_ref[...])
        triT = (_iota((Q, Q), 0) <= _iota((Q, Q), 1)).astype(F32)
        for j in range(chunks):
            at = slice(Q * j, Q * (j + 1))
            dtT = _softplus(dtT_ref[:, at] + dtb_ref[...])
            acsT = _dot_hi(dtT * aT, triT)
            lastT = acsT[:, Q - 1:Q]
            acsT_ref[:, at] = acsT
            dec_ref[j] = jnp.exp(lastT)
            parts = [dtT, acsT, jnp.exp(lastT - acsT), jnp.exp(acsT)]
            colsT = jnp.concatenate([q[HPG * g:HPG * (g + 1)] for g in range(G) for q in parts], axis=0).T
            for g in range(G):
                cols_ref[g, at, :] = colsT[:, 4 * HPG * g:4 * HPG * (g + 1)]

    sds = jax.ShapeDtypeStruct
    return pl.pallas_call(
        body, grid=(T // span,),
        in_specs=[pl.BlockSpec((H, span), lambda i: (0, i)), pl.BlockSpec((H, 1), lambda i: (0, 0)),
                  pl.BlockSpec((H, 1), lambda i: (0, 0))],
        out_specs=[pl.BlockSpec((G, span, 4 * HPG), lambda i: (0, i, 0)), pl.BlockSpec((H, span), lambda i: (0, i)),
                   pl.BlockSpec((chunks, H, 1), lambda i: (i, 0, 0))],
        out_shape=[sds((G, T, 4 * HPG), F32), sds((H, T), F32), sds((T // Q, H, 1), F32)],
        compiler_params=_cparams("parallel"), name=name)(dt_rawT, dt_bias, a_log)


CONV_XS, CONV_BM, CONV_CM = slice(0, SSM_GW), slice(SSM_GW, SSM_GW + SSM_STATE), slice(SSM_GW + SSM_STATE, SSM_GW + 2 * SSM_STATE)


def _ssd_prologue(first, xs_ref, bm_ref, cm_ref, xs_halo, bm_halo, cm_halo, conv_ref, cols_ref, acsT_ref, dec_ref,
                  xp_xs, xp_bm, xp_cm):
    cp = conv_ref[0]
    taps = {n: (cp[:, at], cp[SSM_CONV:SSM_CONV + 1, at]) for n, at in (("xs", CONV_XS), ("bm", CONV_BM), ("cm", CONV_CM))}
    pre_xs = _conv_pre(xs_ref, xs_halo, *taps["xs"], xp_xs, first)
    pre_bm = _conv_pre(bm_ref, bm_halo, *taps["bm"], xp_bm, first)
    pre_cm = _conv_pre(cm_ref, cm_halo, *taps["cm"], xp_cm, first)
    return dict(pre_xs=pre_xs, pre_bm=pre_bm, pre_cm=pre_cm, xs=pre_xs * _sigmoid(pre_xs), bm=pre_bm * _sigmoid(pre_bm),
                cm=pre_cm * _sigmoid(pre_cm), taps=taps, acsT=acsT_ref[...], decayT=dec_ref[0], cols=cols_ref[0])


def _pair_cols(cols, base, p, lo):
    k = base + 2 * p
    return jnp.where(lo, cols[:, k:k + 1], cols[:, k + 1:k + 2])


def _pair_row(colT, p, lo_row):
    return jnp.where(lo_row, colT[2 * p:2 * p + 1, :], colT[2 * p + 1:2 * p + 2, :])


def _pair_operands(pp, p, s, causal, lo, xb):
    zero = jnp.zeros_like(xb)
    rhs = jnp.concatenate([jnp.where(lo, xb, zero), jnp.where(lo, zero, xb)], axis=0)
    ls, ms = [], []
    for k in (2 * p, 2 * p + 1):
        seg = pp["cols"][:, 8 + k:9 + k] - pp["acsT"][k:k + 1, :]
        l = jnp.exp(jnp.where(causal, seg, -1e30))
        ls.append(l)
        ms.append(s * l)
    lhs = jnp.concatenate([m.astype(BF16) for m in ms], axis=1)
    return lhs, rhs, ls


def ssd_fwd(xbc, z, scalars, conv_pack, vec_pack, *, batch, name):
    T = xbc.shape[0]
    Q, GW, N = SSM_CHUNK, SSM_GW, SSM_STATE
    nc = T // batch // Q
    sp = _ssd_specs(nc)(lambda c: c)

    def body(xs_ref, bm_ref, cm_ref, xs_halo, bm_halo, cm_halo, z_ref, conv_ref, cols_ref, acsT_ref, dec_ref, vec_ref,
             y_ref, ys_ref, st_ref, state, xp_xs, xp_bm, xp_cm):
        c = pl.program_id(1)
        g = pl.program_id(2)
        first = c == 0
        pp = _ssd_prologue(first, xs_ref, bm_ref, cm_ref, xs_halo, bm_halo, cm_halo, conv_ref, cols_ref, acsT_ref,
                           dec_ref, xp_xs, xp_bm, xp_cm)
        dsk_ref, ng_ref = vec_ref.at[0:1], vec_ref.at[1:2]
        xs = pp["xs"]
        bb = pp["bm"].astype(BF16)
        cb = pp["cm"].astype(BF16)
        s = _dot_nt(cb, bb)
        causal = _iota((Q, Q), 0) >= _iota((Q, Q), 1)
        lo = _iota((Q, PW), 1) < SSM_HEAD_DIM
        lo_row = _iota((1, PW), 1) < SSM_HEAD_DIM

        @pl.when(first)
        def _():
            state[g] = jnp.zeros((N, GW), F32)

        entering = state[g]
        st_ref[0, 0] = entering
        wide = lambda base: jnp.concatenate([_pair_cols(pp["cols"], base, p, lo) for p in range(PAIRS)], axis=1)
        x = xs * wide(0)
        xb = x.astype(BF16)
        yd = []
        for p in range(PAIRS):
            lhs, rhs, _ = _pair_operands(pp, p, s, causal, lo, xb[:, PW * p:PW * (p + 1)])
            yd.append(_dot(lhs, rhs))
        y = jnp.concatenate(yd, axis=1) + _dot(cb, entering.astype(BF16)) * wide(24) + dsk_ref[...] * xs
        decay = jnp.concatenate([_pair_row(pp["decayT"], p, lo_row) for p in range(PAIRS)], axis=1)
        state[g] = entering * decay + _dot_tn(bb, (x * wide(16)).astype(BF16))
        y_ref[...] = y
        zz = z_ref[...]
        yg = y * (zz * _sigmoid(zz))
        rr = lax.rsqrt(jnp.mean(yg * yg, axis=-1, keepdims=True) + RMS_EPS)
        ys_ref[...] = (yg * rr * ng_ref[...]).astype(BF16)

    sds = jax.ShapeDtypeStruct
    return pl.pallas_call(
        body, grid=(batch, nc, SSM_GROUPS),
        in_specs=[sp["xs"], sp["bm"], sp["cm"], sp["xs_halo"], sp["bm_halo"], sp["cm_halo"], sp["grp"],
                  sp["conv"], sp["cols"], sp["dtT_g"], sp["decay"], sp["vec2"]],
        out_specs=[sp["grp"], sp["grp"], sp["pairs"]],
        out_shape=[sds((T, SSM_D_INNER), F32), sds((T, SSM_D_INNER), BF16), sds((T // Q, SSM_GROUPS, N, GW), F32)],
        scratch_shapes=[pltpu.VMEM((SSM_GROUPS, N, GW), F32),
                        pltpu.VMEM((HALO + Q, GW), F32), pltpu.VMEM((HALO + Q, N), F32), pltpu.VMEM((HALO + Q, N), F32)],
        compiler_params=_cparams("arbitrary", "arbitrary", "arbitrary"), name=name,
    )(xbc, xbc, xbc, xbc, xbc, xbc, z, conv_pack, *scalars, vec_pack)


def _group_cols(g):
    return (slice(SSM_GW * g, SSM_GW * (g + 1)),
            slice(SSM_D_INNER + SSM_STATE * g, SSM_D_INNER + SSM_STATE * (g + 1)),
            slice(SSM_D_INNER + SSM_STATE * (SSM_GROUPS + g), SSM_D_INNER + SSM_STATE * (SSM_GROUPS + g + 1)))


def _conv_pre_v(x, halo, w, b, xp_ref):
    Q = SSM_CHUNK
    xp_ref[0:HALO, :] = halo
    xp_ref[HALO:HALO + Q, :] = x
    pre = b + w[3:4, :] * x
    for k in range(SSM_CONV - 1):
        pre = pre + w[k:k + 1, :] * xp_ref[pl.ds(HALO - 3 + k, Q), :]
    return pre


def _ssd_prologue4(g, first, xbc_ref, halo_ref, conv_ref, cols_ref, acsT_ref, dec_ref, xp_xs, xp_bm, xp_cm):
    cp = conv_ref[g]
    heads = slice(SSM_HPG * g, SSM_HPG * (g + 1))
    out, taps = {}, {}
    for n, at, pk, xp in zip(("xs", "bm", "cm"), _group_cols(g), (CONV_XS, CONV_BM, CONV_CM), (xp_xs, xp_bm, xp_cm)):
        taps[n] = (cp[:, pk], cp[SSM_CONV:SSM_CONV + 1, pk])
        halo = jnp.where(first, 0.0, halo_ref[:, at])
        pre = _conv_pre_v(xbc_ref[:, at], halo, *taps[n], xp.at[g])
        out["pre_" + n] = pre
        out[n] = pre * _sigmoid(pre)
    out.update(taps=taps, acsT=acsT_ref[heads, :], decayT=dec_ref[0, heads, :], cols=cols_ref[g])
    return out


def ssd_fwd4(xbc, z, scalars, conv_pack, vec_pack, *, batch, name):
    T = xbc.shape[0]
    Q, GW, N, G = SSM_CHUNK, SSM_GW, SSM_STATE, SSM_GROUPS
    nc = T // batch // Q
    row = lambda b, c: b * nc + c
    hrow = lambda b, c: jnp.maximum(row(b, c) * (Q // HALO) - 1, 0)

    def body(xbc_ref, halo_ref, z_ref, conv_ref, cols_ref, acsT_ref, dec_ref, vec_ref, y_ref, ys_ref, st_ref,
             state, xp_xs, xp_bm, xp_cm):
        first = pl.program_id(1) == 0
        causal = _iota((Q, Q), 0) >= _iota((Q, Q), 1)
        lo = _iota((Q, PW), 1) < SSM_HEAD_DIM
        lo_row = _iota((1, PW), 1) < SSM_HEAD_DIM

        @pl.when(first)
        def _():
            state[...] = jnp.zeros_like(state)

        for g in range(G):
            cols = slice(GW * g, GW * (g + 1))
            pp = _ssd_prologue4(g, first, xbc_ref, halo_ref, conv_ref, cols_ref, acsT_ref, dec_ref, xp_xs, xp_bm, xp_cm)
            xs = pp["xs"]
            bb = pp["bm"].astype(BF16)
            cb = pp["cm"].astype(BF16)
            s = _dot_nt(cb, bb)
            entering = state[g]
            st_ref[0, g] = entering
            wide = lambda base: jnp.concatenate([_pair_cols(pp["cols"], base, p, lo) for p in range(PAIRS)], axis=1)
            x = xs * wide(0)
            xb = x.astype(BF16)
            yd = []
            for p in range(PAIRS):
                lhs, rhs, _ = _pair_operands(pp, p, s, causal, lo, xb[:, PW * p:PW * (p + 1)])
                yd.append(_dot(lhs, rhs))
            y = jnp.concatenate(yd, axis=1) + _dot(cb, entering.astype(BF16)) * wide(24) + vec_ref[0:1, cols] * xs
            decay = jnp.concatenate([_pair_row(pp["decayT"], p, lo_row) for p in range(PAIRS)], axis=1)
            state[g] = entering * decay + _dot_tn(bb, (x * wide(16)).astype(BF16))
            y_ref[:, cols] = y
            zz = z_ref[:, cols]
            yg = y * (zz * _sigmoid(zz))
            rr = lax.rsqrt(jnp.mean(yg * yg, axis=-1, keepdims=True) + RMS_EPS)
            ys_ref[:, cols] = (yg * rr * vec_ref[1:2, cols]).astype(BF16)

    sds = jax.ShapeDtypeStruct
    cols3, acsT, decay = scalars
    return pl.pallas_call(
        body, grid=(batch, nc),
        in_specs=[pl.BlockSpec((Q, SSM_CONV_DIM), lambda b, c: (row(b, c), 0)),
                  pl.BlockSpec((HALO, SSM_CONV_DIM), lambda b, c: (hrow(b, c), 0)),
                  pl.BlockSpec((Q, SSM_D_INNER), lambda b, c: (row(b, c), 0)),
                  pl.BlockSpec((G, HALO, GW + 2 * N), lambda b, c: (0, 0, 0)),
                  pl.BlockSpec((G, Q, 4 * SSM_HPG), lambda b, c: (0, row(b, c), 0)),
                  pl.BlockSpec((SSM_HEADS, Q), lambda b, c: (0, row(b, c))),
                  pl.BlockSpec((1, SSM_HEADS, 1), lambda b, c: (row(b, c), 0, 0)),
                  pl.BlockSpec((2, SSM_D_INNER), lambda b, c: (0, 0))],
        out_specs=[pl.BlockSpec((Q, SSM_D_INNER), lambda b, c: (row(b, c), 0)),
                   pl.BlockSpec((Q, SSM_D_INNER), lambda b, c: (row(b, c), 0)),
                   pl.BlockSpec((1, G, N, GW), lambda b, c: (row(b, c), 0, 0, 0))],
        out_shape=[sds((T, SSM_D_INNER), F32), sds((T, SSM_D_INNER), BF16), sds((T // Q, G, N, GW), F32)],
        scratch_shapes=[pltpu.VMEM((G, N, GW), F32), pltpu.VMEM((G, HALO + Q, GW), F32),
                        pltpu.VMEM((G, HALO + Q, N), F32), pltpu.VMEM((G, HALO + Q, N), F32)],
        compiler_params=_cparams("arbitrary", "arbitrary"), name=name,
    )(xbc, xbc, z, conv_pack, cols3, acsT, decay, vec_pack)


def ssd_bwd(dys, y, xbc, z, dt_rawT, states, scalars, conv_pack, col_pack, vec_pack, *, batch, name, ride=None):
    T = xbc.shape[0]
    Q, GW, N, G = SSM_CHUNK, SSM_GW, SSM_STATE, SSM_GROUPS
    nc = T // batch // Q
    sp = _ssd_specs(nc)(lambda c: nc - 1 - c)

    def body(xs_ref, bm_ref, cm_ref, xs_halo, bm_halo, cm_halo, z_ref, y_ref, dys_ref, dtT_ref, st_ref,
             conv_ref, cols_ref, acsT_ref, dec_ref, col_ref, vec_ref,
             dz_ref, dxs_ref, dbm_ref, dcm_ref, ddtT_ref, acc_xs, acc_bm, acc_cm, acc_head,
             dstate, xp_xs, xp_bm, xp_cm, dp_xs, dp_bm, dp_cm, cy_xs, cy_bm, cy_cm):
        b = pl.program_id(0)
        cr = pl.program_id(1)
        g = pl.program_id(2)
        first = cr == nc - 1
        last = cr == 0

        @pl.when(jnp.logical_and(jnp.logical_and(b == 0, cr == 0), g == 0))
        def _():
            acc_xs[...] = jnp.zeros_like(acc_xs)
            acc_bm[...] = jnp.zeros_like(acc_bm)
            acc_cm[...] = jnp.zeros_like(acc_cm)
            acc_head[...] = jnp.zeros_like(acc_head)

        @pl.when(last)
        def _():
            dstate[g] = jnp.zeros((N, GW), F32)
            cy_xs[g] = jnp.zeros((HALO, GW), F32)
            cy_bm[g] = jnp.zeros((HALO, N), F32)
            cy_cm[g] = jnp.zeros((HALO, N), F32)

        pp = _ssd_prologue(first, xs_ref, bm_ref, cm_ref, xs_halo, bm_halo, cm_halo, conv_ref, cols_ref, acsT_ref,
                           dec_ref, xp_xs, xp_bm, xp_cm)
        xs, decayT = pp["xs"], pp["decayT"]
        dsk_ref, ng_ref = vec_ref.at[0:1], vec_ref.at[1:2]
        dtrT = dtT_ref[...] + col_ref[:, 0:1]
        dtT = _softplus(dtrT)
        aT = -jnp.exp(col_ref[:, 1:2])

        yv = y_ref[...]
        zz = z_ref[...]
        sz = _sigmoid(zz)
        silu_z = zz * sz
        yg = yv * silu_z
        rr = lax.rsqrt(jnp.mean(yg * yg, axis=-1, keepdims=True) + RMS_EPS)
        dys_v = dys_ref[...]
        d_ng = jnp.sum(dys_v * yg * rr, axis=0, keepdims=True)
        t = dys_v * ng_ref[...]
        dyg = rr * t - yg * (rr * rr * rr * jnp.mean(t * yg, axis=-1, keepdims=True))
        dy = dyg * silu_z
        dz_ref[...] = (dyg * yv * (sz * (1.0 + zz * (1.0 - sz)))).astype(BF16)
        dsk = dsk_ref[...]
        d_dsk = jnp.sum(dy * xs, axis=0, keepdims=True)

        bb = pp["bm"].astype(BF16)
        cb = pp["cm"].astype(BF16)
        s = _dot_nt(cb, bb)
        causal = _iota((Q, Q), 0) >= _iota((Q, Q), 1)
        lo = _iota((Q, PW), 1) < SSM_HEAD_DIM
        lo_row = _iota((1, PW), 1) < SSM_HEAD_DIM
        wide = lambda base: jnp.concatenate([_pair_cols(pp["cols"], base, p, lo) for p in range(PAIRS)], axis=1)
        dt_x, w_x, e_x = wide(0), wide(16), wide(24)
        decay = jnp.concatenate([_pair_row(decayT, p, lo_row) for p in range(PAIRS)], axis=1)
        x = xs * dt_x
        xw = x * w_x
        xb = x.astype(BF16)
        dyb = dy.astype(BF16)
        dye = (dy * e_x).astype(BF16)
        hp = st_ref[0, 0]
        hpb = hp.astype(BF16)
        dh = dstate[g]
        dhb = dh.astype(BF16)
        ds_acc = jnp.zeros((Q, Q), F32)
        yd, dxd = [], []
        for p in range(PAIRS):
            tile = slice(PW * p, PW * (p + 1))
            lhs, rhs, ls = _pair_operands(pp, p, s, causal, lo, xb[:, tile])
            dm = _dot_nt(dyb[:, tile], rhs)
            dxd2 = _dot_tn(lhs, dyb[:, tile])
            dxd.append(jnp.where(lo, dxd2[0:Q], dxd2[Q:2 * Q]))
            ds_acc = ds_acc + dm[:, 0:Q] * ls[0] + dm[:, Q:2 * Q] * ls[1]
            yd.append(_dot(lhs, rhs))
        yd = jnp.concatenate(yd, axis=1)
        dxd = jnp.concatenate(dxd, axis=1)
        dxw = _dot(bb, dhb)
        dx_full = dxd + dxw * w_x
        tw = dxw * xw
        q1 = dyb.astype(F32) * yd + dy * (_dot(cb, hpb) * e_x) - tw - xb.astype(F32) * dxd
        q2 = dx_full * xs
        dxs_v = dsk * dy + dx_full * dt_x
        dstate[g] = dh * decay + _dot_tn(cb, dye)
        dsb = ds_acc.astype(BF16)
        d_c = _dot_nt(dye, hpb) + _dot(dsb, bb)
        d_b = _dot_nt(xw.astype(BF16), dhb) + _dot_tn(dsb, cb)
        e8 = (_iota((SSM_HPG, GW), 0) == lax.shift_right_logical(_iota((SSM_HPG, GW), 1), 6)).astype(F32)
        seg_sum = lambda v: lax.dot_general(e8, v, (((1,), (1,)), ((), ())), preferred_element_type=F32,
                                            precision=lax.Precision.HIGHEST)
        row = jnp.sum(dh * hp, axis=0, keepdims=True) * decay + jnp.sum(tw, axis=0, keepdims=True)
        last_terms = jnp.sum(e8 * row, axis=1, keepdims=True)
        dacsT = seg_sum(q1) + jnp.where(_iota((1, Q), 1) == Q - 1, 1.0, 0.0) * last_terms
        tri = (_iota((Q, Q), 0) >= _iota((Q, Q), 1)).astype(F32)
        d_dtaT = _dot_hi(dacsT, tri)
        ddtT = d_dtaT * aT + seg_sum(q2)
        d_alog = jnp.sum(d_dtaT * dtT, axis=1, keepdims=True) * aT
        ddt_rawT = ddtT * _sigmoid(dtrT)
        ddtT_ref[...] = ddt_rawT
        d_dtb = jnp.sum(ddt_rawT, axis=1, keepdims=True)
        lane = _iota((SSM_HPG, N), 1)
        acc_head[g] += jnp.where(lane == 0, d_dtb, 0.0) + jnp.where(lane == 1, d_alog, 0.0)
        taps = pp["taps"]
        dxs_ref[...] = _conv_bwd(dxs_v, pp["pre_xs"], xp_xs, taps["xs"][0], cy_xs, acc_xs, dp_xs, g, last).astype(BF16)
        dbm_ref[...] = _conv_bwd(d_b, pp["pre_bm"], xp_bm, taps["bm"][0], cy_bm, acc_bm, dp_bm, g, last).astype(BF16)
        dcm_ref[...] = _conv_bwd(d_c, pp["pre_cm"], xp_cm, taps["cm"][0], cy_cm, acc_cm, dp_cm, g, last).astype(BF16)
        acc_xs[g, pl.ds(SSM_CONV + 1, 2), :] += jnp.concatenate([d_ng, d_dsk], axis=0)

    sds = jax.ShapeDtypeStruct
    row = lambda b, c, g: b * nc + (nc - 1 - c)
    full = lambda shape: pl.BlockSpec(shape, lambda b, c, g: (0,) * len(shape))
    return _call(
        body, grid=(batch, nc, G),
        in_specs=[sp["xs"], sp["bm"], sp["cm"], sp["xs_halo"], sp["bm_halo"], sp["cm_halo"], sp["grp"], sp["grp"],
                  sp["grp"], sp["dtT_g"], sp["pairs"],
                  sp["conv"], sp["cols"], sp["dtT_g"], sp["decay"], sp["col2"], sp["vec2"]],
        args=[xbc, xbc, xbc, xbc, xbc, xbc, z, y, dys, dt_rawT, states, conv_pack, *scalars, col_pack, vec_pack],
        out_specs=[sp["grp"], sp["grp"],
                   pl.BlockSpec((Q, N), lambda b, c, g: (row(b, c, g), g)),
                   pl.BlockSpec((Q, N), lambda b, c, g: (row(b, c, g), g)),
                   sp["dtT_g"], full((G, HALO, GW)), full((G, HALO, N)), full((G, HALO, N)), full((G, SSM_HPG, N))],
        out_shape=[sds((T, SSM_D_INNER), BF16), sds((T, SSM_D_INNER), BF16), sds((T, G * N), BF16),
                   sds((T, G * N), BF16), sds((SSM_HEADS, T), F32),
                   sds((G, HALO, GW), F32), sds((G, HALO, N), F32), sds((G, HALO, N), F32), sds((G, SSM_HPG, N), F32)],
        scratch=[pltpu.VMEM((G, N, GW), F32),
                 pltpu.VMEM((HALO + Q, GW), F32), pltpu.VMEM((HALO + Q, N), F32), pltpu.VMEM((HALO + Q, N), F32),
                 pltpu.VMEM((Q + HALO, GW), F32), pltpu.VMEM((Q + HALO, N), F32), pltpu.VMEM((Q + HALO, N), F32),
                 pltpu.VMEM((G, HALO, GW), F32), pltpu.VMEM((G, HALO, N), F32), pltpu.VMEM((G, HALO, N), F32)],
        sem=("arbitrary", "arbitrary", "arbitrary"), name=name, ride=ride)


def ssd_bwd4(dys, y, xbc, z, dt_rawT, states, scalars, conv_pack, col_pack, vec_pack, *, batch, name, ride=None):
    T = xbc.shape[0]
    Q, GW, N, G, HPG = SSM_CHUNK, SSM_GW, SSM_STATE, SSM_GROUPS, SSM_HPG
    nc = T // batch // Q
    row = lambda b, c: b * nc + (nc - 1 - c)
    hrow = lambda b, c: jnp.maximum(row(b, c) * (Q // HALO) - 1, 0)

    def body(xbc_ref, halo_ref, z_ref, y_ref, dys_ref, dtT_ref, st_ref, conv_ref, cols_ref, acsT_ref, dec_ref, col_ref,
             vec_ref, dz_ref, dxs_ref, dbm_ref, dcm_ref, ddtT_ref, acc_xs, acc_bm, acc_cm, acc_head,
             dstate, xp_xs, xp_bm, xp_cm, dp_xs, dp_bm, dp_cm, cy_xs, cy_bm, cy_cm):
        b = pl.program_id(0)
        cr = pl.program_id(1)
        first = cr == nc - 1
        last = cr == 0

        @pl.when(jnp.logical_and(b == 0, cr == 0))
        def _():
            acc_xs[...] = jnp.zeros_like(acc_xs)
            acc_bm[...] = jnp.zeros_like(acc_bm)
            acc_cm[...] = jnp.zeros_like(acc_cm)
            acc_head[...] = jnp.zeros_like(acc_head)

        @pl.when(last)
        def _():
            dstate[...] = jnp.zeros_like(dstate)
            cy_xs[...] = jnp.zeros_like(cy_xs)
            cy_bm[...] = jnp.zeros_like(cy_bm)
            cy_cm[...] = jnp.zeros_like(cy_cm)

        causal = _iota((Q, Q), 0) >= _iota((Q, Q), 1)
        lo = _iota((Q, PW), 1) < SSM_HEAD_DIM
        lo_row = _iota((1, PW), 1) < SSM_HEAD_DIM
        tri = (_iota((Q, Q), 0) >= _iota((Q, Q), 1)).astype(F32)
        e8 = (_iota((HPG, GW), 0) == lax.shift_right_logical(_iota((HPG, GW), 1), 6)).astype(F32)
        seg_sum = lambda v: lax.dot_general(e8, v, (((1,), (1,)), ((), ())), preferred_element_type=F32,
                                            precision=lax.Precision.HIGHEST)
        lane = _iota((HPG, N), 1)
        for g in range(G):
            cols = slice(GW * g, GW * (g + 1))
            ncols = slice(N * g, N * (g + 1))
            heads = slice(HPG * g, HPG * (g + 1))
            pp = _ssd_prologue4(g, first, xbc_ref, halo_ref, conv_ref, cols_ref, acsT_ref, dec_ref, xp_xs, xp_bm, xp_cm)
            xs, decayT = pp["xs"], pp["decayT"]
            dtrT = dtT_ref[heads, :] + col_ref[heads, 0:1]
            dtT = _softplus(dtrT)
            aT = -jnp.exp(col_ref[heads, 1:2])

            yv = y_ref[:, cols]
            zz = z_ref[:, cols]
            sz = _sigmoid(zz)
            silu_z = zz * sz
            yg = yv * silu_z
            rr = lax.rsqrt(jnp.mean(yg * yg, axis=-1, keepdims=True) + RMS_EPS)
            dys_v = dys_ref[:, cols]
            d_ng = jnp.sum(dys_v * yg * rr, axis=0, keepdims=True)
            t = dys_v * vec_ref[1:2, cols]
            dyg = rr * t - yg * (rr * rr * rr * jnp.mean(t * yg, axis=-1, keepdims=True))
            dy = dyg * silu_z
            dz_ref[:, cols] = (dyg * yv * (sz * (1.0 + zz * (1.0 - sz)))).astype(BF16)
            dsk = vec_ref[0:1, cols]
            d_dsk = jnp.sum(dy * xs, axis=0, keepdims=True)

            bb = pp["bm"].astype(BF16)
            cb = pp["cm"].astype(BF16)
            s = _dot_nt(cb, bb)
            wide = lambda base: jnp.concatenate([_pair_cols(pp["cols"], base, p, lo) for p in range(PAIRS)], axis=1)
            dt_x, w_x, e_x = wide(0), wide(16), wide(24)
            decay = jnp.concatenate([_pair_row(decayT, p, lo_row) for p in range(PAIRS)], axis=1)
            x = xs * dt_x
            xw = x * w_x
            xb = x.astype(BF16)
            dyb = dy.astype(BF16)
            dye = (dy * e_x).astype(BF16)
            hp = st_ref[0, g]
            hpb = hp.astype(BF16)
            dh = dstate[g]
            dhb = dh.astype(BF16)
            ds_acc = jnp.zeros((Q, Q), F32)
            yd, dxd = [], []
            for p in range(PAIRS):
                tile = slice(PW * p, PW * (p + 1))
                lhs, rhs, ls = _pair_operands(pp, p, s, causal, lo, xb[:, tile])
                dm = _dot_nt(dyb[:, tile], rhs)
                dxd2 = _dot_tn(lhs, dyb[:, tile])
                dxd.append(jnp.where(lo, dxd2[0:Q], dxd2[Q:2 * Q]))
                ds_acc = ds_acc + dm[:, 0:Q] * ls[0] + dm[:, Q:2 * Q] * ls[1]
                yd.append(_dot(lhs, rhs))
            yd = jnp.concatenate(yd, axis=1)
            dxd = jnp.concatenate(dxd, axis=1)
            dxw = _dot(bb, dhb)
            dx_full = dxd + dxw * w_x
            tw = dxw * xw
            q1 = dyb.astype(F32) * yd + dy * (_dot(cb, hpb) * e_x) - tw - xb.astype(F32) * dxd
            q2 = dx_full * xs
            dxs_v = dsk * dy + dx_full * dt_x
            dstate[g] = dh * decay + _dot_tn(cb, dye)
            dsb = ds_acc.astype(BF16)
            d_c = _dot_nt(dye, hpb) + _dot(dsb, bb)
            d_b = _dot_nt(xw.astype(BF16), dhb) + _dot_tn(dsb, cb)
            rowv = jnp.sum(dh * hp, axis=0, keepdims=True) * decay + jnp.sum(tw, axis=0, keepdims=True)
            last_terms = jnp.sum(e8 * rowv, axis=1, keepdims=True)
            dacsT = seg_sum(q1) + jnp.where(_iota((1, Q), 1) == Q - 1, 1.0, 0.0) * last_terms
            d_dtaT = _dot_hi(dacsT, tri)
            ddtT = d_dtaT * aT + seg_sum(q2)
            d_alog = jnp.sum(d_dtaT * dtT, axis=1, keepdims=True) * aT
            ddt_rawT = ddtT * _sigmoid(dtrT)
            ddtT_ref[heads, :] = ddt_rawT
            d_dtb = jnp.sum(ddt_rawT, axis=1, keepdims=True)
            acc_head[g] += jnp.where(lane == 0, d_dtb, 0.0) + jnp.where(lane == 1, d_alog, 0.0)
            taps = pp["taps"]
            dxs_ref[:, cols] = _conv_bwd(dxs_v, pp["pre_xs"], xp_xs.at[g], taps["xs"][0], cy_xs, acc_xs, dp_xs.at[g], g,
                                         last).astype(BF16)
            dbm_ref[:, ncols] = _conv_bwd(d_b, pp["pre_bm"], xp_bm.at[g], taps["bm"][0], cy_bm, acc_bm, dp_bm.at[g], g,
                                          last).astype(BF16)
            dcm_ref[:, ncols] = _conv_bwd(d_c, pp["pre_cm"], xp_cm.at[g], taps["cm"][0], cy_cm, acc_cm, dp_cm.at[g], g,
                                          last).astype(BF16)
            acc_xs[g, pl.ds(SSM_CONV + 1, 2), :] += jnp.concatenate([d_ng, d_dsk], axis=0)

    sds = jax.ShapeDtypeStruct
    cols3, acsT, decay3 = scalars
    full = lambda shape: pl.BlockSpec(shape, lambda b, c: (0,) * len(shape))
    wide_in = pl.BlockSpec((Q, SSM_D_INNER), lambda b, c: (row(b, c), 0))
    heads_in = pl.BlockSpec((SSM_HEADS, Q), lambda b, c: (0, row(b, c)))
    return _call(
        body, grid=(batch, nc),
        in_specs=[pl.BlockSpec((Q, SSM_CONV_DIM), lambda b, c: (row(b, c), 0)),
                  pl.BlockSpec((HALO, SSM_CONV_DIM), lambda b, c: (hrow(b, c), 0)),
                  wide_in, wide_in, wide_in, heads_in,
                  pl.BlockSpec((1, G, N, GW), lambda b, c: (row(b, c), 0, 0, 0)),
                  full((G, HALO, GW + 2 * N)),
                  pl.BlockSpec((G, Q, 4 * HPG), lambda b, c: (0, row(b, c), 0)),
                  heads_in,
                  pl.BlockSpec((1, SSM_HEADS, 1), lambda b, c: (row(b, c), 0, 0)),
                  full((SSM_HEADS, 2)), full((2, SSM_D_INNER))],
        args=[xbc, xbc, z, y, dys, dt_rawT, states, conv_pack, cols3, acsT, decay3, col_pack, vec_pack],
        out_specs=[wide_in, wide_in,
                   pl.BlockSpec((Q, G * N), lambda b, c: (row(b, c), 0)),
                   pl.BlockSpec((Q, G * N), lambda b, c: (row(b, c), 0)),
                   heads_in, full((G, HALO, GW)), full((G, HALO, N)), full((G, HALO, N)), full((G, HPG, N))],
        out_shape=[sds((T, SSM_D_INNER), BF16), sds((T, SSM_D_INNER), BF16), sds((T, G * N), BF16),
                   sds((T, G * N), BF16), sds((SSM_HEADS, T), F32),
                   sds((G, HALO, GW), F32), sds((G, HALO, N), F32), sds((G, HALO, N), F32), sds((G, HPG, N), F32)],
        scratch=[pltpu.VMEM((G, N, GW), F32),
                 pltpu.VMEM((G, HALO + Q, GW), F32), pltpu.VMEM((G, HALO + Q, N), F32), pltpu.VMEM((G, HALO + Q, N), F32),
                 pltpu.VMEM((G, Q + HALO, GW), F32), pltpu.VMEM((G, Q + HALO, N), F32), pltpu.VMEM((G, Q + HALO, N), F32),
                 pltpu.VMEM((G, HALO, GW), F32), pltpu.VMEM((G, HALO, N), F32), pltpu.VMEM((G, HALO, N), F32)],
        sem=("arbitrary", "arbitrary"), name=name, ride=ride)


def mm_rows(a, b, *, name, tt=2048):
    M, T = a.shape
    N = b.shape[1]
    tt = min(tt, T)

    def body(a_ref, b_ref, o_ref):
        @pl.when(pl.program_id(0) == 0)
        def _():
            o_ref[...] = jnp.zeros_like(o_ref)

        o_ref[...] += _dot(a_ref[...].astype(BF16), b_ref[...])

    return pl.pallas_call(
        body, grid=(T // tt,),
        in_specs=[pl.BlockSpec((M, tt), lambda t: (0, t)), pl.BlockSpec((tt, N), lambda t: (t, 0))],
        out_specs=pl.BlockSpec((M, N), lambda t: (0, 0)), out_shape=jax.ShapeDtypeStruct((M, N), F32),
        compiler_params=_cparams("arbitrary"), name=name)(a, b)


MESH = pl.DeviceIdType.MESH
ANY = pl.BlockSpec(memory_space=pl.ANY)
ROW_ALIGN = 16


def _me():
    return lax.axis_index("x"), lax.axis_index("y"), lax.axis_index("c")


def _other_chips(x, y):
    return [(1 - x, y), (x, 1 - y), (1 - x, 1 - y)]


def _remote(src, dst, send_sem, recv_sem, to):
    return pltpu.make_async_remote_copy(src_ref=src, dst_ref=dst, send_sem=send_sem, recv_sem=recv_sem,
                                        device_id=to, device_id_type=MESH)


def _half(c, rows):
    return pl.ds(pl.multiple_of(c * (rows // 2), ROW_ALIGN), rows // 2)


def ag_ride(bufs):
    n = len(bufs)

    def copies(outs, sems):
        ici_send, ici_recv, d2d_send, d2d_recv = sems
        x, y, c = _me()
        sib = (x, y, 1 - c)
        ici, d2d, d2d_in = [], [], []
        for i in range(n):
            rows = outs[i].shape[1]
            mine = outs[i].at[2 * x + y, _half(c, rows)]
            for j, chip in enumerate(_other_chips(x, y)):
                ici.append(_remote(mine, mine, ici_send.at[i, j], ici_recv.at[i, j], (*chip, c)))
                landed = outs[i].at[2 * chip[0] + chip[1], _half(c, rows)]
                d2d.append((_remote(landed, landed, ici_send.at[i, j], ici_recv.at[i, j], (*chip, c)),
                            _remote(landed, landed, d2d_send.at[i, j], d2d_recv.at[i, j], sib)))
                lands = outs[i].at[2 * chip[0] + chip[1], _half(1 - c, rows)]
                d2d_in.append(_remote(lands, lands, d2d_send.at[i, j], d2d_recv.at[i, j], sib))
        return ici, d2d, d2d_in

    def start(ins, outs, sems):
        for cp in copies(outs, sems)[0]:
            cp.start()

    def finish(ins, outs, sems):
        ici, d2d, d2d_in = copies(outs, sems)
        for arrived, forward in d2d:
            arrived.wait_recv()
            forward.start()
        for cp in d2d_in:
            cp.wait_recv()
        for cp in ici + [forward for _, forward in d2d]:
            cp.wait_send()

    return Ride(bufs, [jax.ShapeDtypeStruct(b.shape, b.dtype) for b in bufs], [(i, i) for i in range(n)],
                [pltpu.SemaphoreType.DMA((n, 3))] * 4, start, finish)


def pair_ride(grads):
    n = len(grads)

    def copies(ins, outs, sems):
        x, y, c = _me()
        return [_remote(ins[i].at[:, _half(1 - c, ins[i].shape[1]), :], outs[i], sems[0].at[i], sems[1].at[i], (x, y, 1 - c))
                for i in range(n)]

    def start(ins, outs, sems):
        for cp in copies(ins, outs, sems):
            cp.start()

    def finish(ins, outs, sems):
        for cp in copies(ins, outs, sems):
            cp.wait()

    return Ride(grads, [jax.ShapeDtypeStruct((N_SHARD, g.shape[1] // 2, g.shape[2]), g.dtype) for g in grads], [],
                [pltpu.SemaphoreType.DMA((n,))] * 2, start, finish)


def rs_add(grad, part, c, *, rt, name):
    _, rows, cols = grad.shape
    r2 = rows // 2
    nrb = r2 // rt

    def body(c_ref, g_ref, p_ref, o_ref):
        o_ref[...] = (g_ref[...] + p_ref[...]).astype(BF16)

    return pl.pallas_call(
        body,
        grid_spec=pltpu.PrefetchScalarGridSpec(
            num_scalar_prefetch=1, grid=(N_SHARD, nrb),
            in_specs=[pl.BlockSpec((1, rt, cols), lambda k, i, c_ref: (k, c_ref[1] * nrb + i, 0)),
                      pl.BlockSpec((1, rt, cols), lambda k, i, c_ref: (k, i, 0))],
            out_specs=pl.BlockSpec((1, rt, cols), lambda k, i, c_ref: (k, i, 0))),
        out_shape=jax.ShapeDtypeStruct((N_SHARD, r2, cols), BF16),
        compiler_params=_cparams("parallel", "parallel"), name=name)(c, grad, part)


def chips_ride(sums):
    n = len(sums)

    def copies(ins, outs, sems):
        send, recv = sems
        x, y, c = _me()
        return [_remote(ins[i].at[2 * chip[0] + chip[1]], outs[i].at[2 * x + y], send.at[i, j], recv.at[i, j], (*chip, c))
                for i in range(n) for j, chip in enumerate(_other_chips(x, y))]

    def start(ins, outs, sems):
        for cp in copies(ins, outs, sems):
            cp.start()

    def finish(ins, outs, sems):
        for cp in copies(ins, outs, sems):
            cp.wait()

    return Ride(sums, [jax.ShapeDtypeStruct(s.shape, s.dtype) for s in sums], [],
                [pltpu.SemaphoreType.DMA((n, 3))] * 2, start, finish)


def rs_total(parts, own, where, *, rt, name):
    _, r2, cols = parts.shape
    nrb = r2 // rt

    def body(w_ref, p0, p1, p2, p3, own_ref, o_ref):
        s_me = w_ref[0]
        acc = None
        for k, p in enumerate((p0, p1, p2, p3)):
            term = jnp.where(s_me == k, own_ref[0], p[0]).astype(F32)
            acc = term if acc is None else acc + term
        o_ref[...] = acc

    def slot(k):
        return pl.BlockSpec((1, rt, cols), lambda i, w: (jnp.where(w[0] == k, (k + 1) % N_SHARD, k), i, 0))

    return pl.pallas_call(
        body,
        grid_spec=pltpu.PrefetchScalarGridSpec(
            num_scalar_prefetch=1, grid=(nrb,),
            in_specs=[slot(0), slot(1), slot(2), slot(3), pl.BlockSpec((1, rt, cols), lambda i, w: (w[0], i, 0))],
            out_specs=pl.BlockSpec((rt, cols), lambda i, w: (w[1] * nrb + i, 0))),
        out_shape=jax.ShapeDtypeStruct((2 * r2, cols), F32),
        compiler_params=_cparams("parallel"), name=name)(where, parts, parts, parts, parts, own)


def share_ride(totals):
    n = len(totals)

    def halves(outs, sems):
        x, y, c = _me()
        mine = [outs[i].at[_half(c, outs[i].shape[0])] for i in range(n)]
        other = [outs[i].at[_half(1 - c, outs[i].shape[0])] for i in range(n)]
        return ([_remote(m, m, sems[0].at[i], sems[1].at[i], (x, y, 1 - c)) for i, m in enumerate(mine)],
                [_remote(o, o, sems[0].at[i], sems[1].at[i], (x, y, 1 - c)) for i, o in enumerate(other)])

    def start(ins, outs, sems):
        for cp in halves(outs, sems)[0]:
            cp.start()

    def finish(ins, outs, sems):
        sent, landing = halves(outs, sems)
        for cp in landing:
            cp.wait_recv()
        for cp in sent:
            cp.wait_send()

    return Ride(totals, [jax.ShapeDtypeStruct(t.shape, t.dtype) for t in totals], [(i, i) for i in range(n)],
                [pltpu.SemaphoreType.DMA((n,))] * 2, start, finish)


def small_allreduce(buf, *, name):
    rows = buf.shape[0]

    def body(x_ref, o_ref, slots, send, recv):
        x, y, c = _me()
        me = 4 * x + 2 * y + c
        slots[me] = x_ref[...]
        sent = []
        for d in range(1, 8):
            peer = (1 - x if d & 4 else x, 1 - y if d & 2 else y, 1 - c if d & 1 else c)
            sent.append(_remote(x_ref, slots.at[me], send.at[d - 1], recv.at[d - 1], peer))
            sent[-1].start()
        for cp in sent:
            cp.wait()
        acc = slots[0]
        for k in range(1, 8):
            acc = acc + slots[k]
        o_ref[...] = acc

    return pl.pallas_call(
        body, out_shape=jax.ShapeDtypeStruct(buf.shape, F32),
        in_specs=[pl.BlockSpec(memory_space=pltpu.VMEM)], out_specs=pl.BlockSpec(memory_space=pltpu.VMEM),
        scratch_shapes=[pltpu.VMEM((8, rows, 128), F32), pltpu.SemaphoreType.DMA((7,)), pltpu.SemaphoreType.DMA((7,))],
        name=name)(buf)


def adamw(w, g, m, v, *, name, rt=None):
    rows, cols = w.shape
    rt = rows if rt is None else rt
    c1 = 1.0 - ADAM_B1 ** ADAM_STEP
    c2 = 1.0 - ADAM_B2 ** ADAM_STEP

    def body(w_ref, g_ref, m_ref, v_ref, d_ref, nm_ref, nv_ref):
        gg = g_ref[...]
        nm = ADAM_B1 * m_ref[...] + (1.0 - ADAM_B1) * gg
        nv = ADAM_B2 * v_ref[...] + (1.0 - ADAM_B2) * (gg * gg)
        nm_ref[...] = nm
        nv_ref[...] = nv
        d_ref[...] = -ADAM_LR * ((nm / c1) / (jnp.sqrt(nv / c2) + ADAM_EPS) + ADAM_WD * w_ref[...])

    spec = pl.BlockSpec((rt, cols), lambda i: (i, 0))
    return pl.pallas_call(
        body, grid=(rows // rt,), in_specs=[spec] * 4, out_specs=[spec] * 3,
        out_shape=[jax.ShapeDtypeStruct((rows, cols), F32)] * 3,
        compiler_params=_cparams("parallel"), name=name)(w, g, m, v)


WEIGHTS = ['ffn1_pre_g', 'ffn1_w_gate', 'ffn1_w_up', 'ffn1_w_down', 'ffn1_post_g', 'mix_pre_g', 'w_in', 'conv_w',
           'conv_b', 'dt_bias', 'a_log', 'd_skip', 'ssm_norm_g', 'w_ssm_proj', 'attn_sinks', 'rel_bias_table',
           'w_attn_proj', 'w_out', 'mix_post_g', 'ffn2_pre_g', 'ffn2_w_gate', 'ffn2_w_up', 'ffn2_w_down', 'ffn2_post_g']
BIG = ['ffn1_w_gate', 'ffn1_w_up', 'ffn1_w_down', 'w_in', 'w_ssm_proj', 'w_attn_proj', 'w_out',
       'ffn2_w_gate', 'ffn2_w_up', 'ffn2_w_down']
SMALL = [w for w in WEIGHTS if w not in BIG]


def _bucket_onehot():
    blk = ATTN_BLOCK
    dist = np.maximum(np.arange(blk)[:, None] + blk - np.arange(2 * blk)[None, :], 0)
    max_exact = REL_BUCKETS // 2
    d = np.maximum(dist, 1).astype(np.float32)
    large = max_exact + (np.log(d / np.float32(max_exact)) / np.float32(math.log(REL_MAX_DISTANCE / max_exact))
                         * np.float32(REL_BUCKETS - max_exact)).astype(np.int32)
    bucket = np.where(dist < max_exact, dist, np.minimum(large, REL_BUCKETS - 1)).reshape(-1)
    return jnp.asarray((bucket[None, :] == np.arange(REL_BUCKETS)[:, None]).astype(np.float32))


def _pack_rows(parts, mult=8):
    flat = jnp.concatenate([p.reshape(-1).astype(F32) for p in parts])
    rows = -(-flat.shape[0] // (128 * mult)) * mult
    return jnp.pad(flat, (0, rows * 128 - flat.shape[0])).reshape(rows, 128)


def _unpack_rows(buf, shapes):
    flat = buf.reshape(-1)
    out, at = [], 0
    for shp in shapes:
        size = int(np.prod(shp))
        out.append(flat[at:at + size].reshape(shp))
        at += size
    return out


def kernel(x, ffn1_pre_g, ffn1_w_gate, ffn1_w_up, ffn1_w_down, ffn1_post_g, mix_pre_g, w_in, conv_w, conv_b, dt_bias, a_log, d_skip, ssm_norm_g, w_ssm_proj, attn_sinks, rel_bias_table, w_attn_proj, w_out, mix_post_g, ffn2_pre_g, ffn2_w_gate, ffn2_w_up, ffn2_w_down, ffn2_post_g, loss_target, m_ffn1_pre_g, m_ffn1_w_gate, m_ffn1_w_up, m_ffn1_w_down, m_ffn1_post_g, m_mix_pre_g, m_w_in, m_conv_w, m_conv_b, m_dt_bias, m_a_log, m_d_skip, m_ssm_norm_g, m_w_ssm_proj, m_attn_sinks, m_rel_bias_table, m_w_attn_proj, m_w_out, m_mix_post_g, m_ffn2_pre_g, m_ffn2_w_gate, m_ffn2_w_up, m_ffn2_w_down, m_ffn2_post_g, v_ffn1_pre_g, v_ffn1_w_gate, v_ffn1_w_up, v_ffn1_w_down, v_ffn1_post_g, v_mix_pre_g, v_w_in, v_conv_w, v_conv_b, v_dt_bias, v_a_log, v_d_skip, v_ssm_norm_g, v_w_ssm_proj, v_attn_sinks, v_rel_bias_table, v_w_attn_proj, v_w_out, v_mix_post_g, v_ffn2_pre_g, v_ffn2_w_gate, v_ffn2_w_up, v_ffn2_w_down, v_ffn2_post_g):
    args = locals()
    w = {n: args[n] for n in WEIGHTS}
    m = {n: args["m_" + n] for n in WEIGHTS}
    v = {n: args["v_" + n] for n in WEIGHTS}
    batch, seq, D = x.shape
    T = batch * seq
    xi, yi, ci = _me()
    s_me = 2 * xi + yi
    x2 = x.reshape(T, D)
    tgt = loss_target.reshape(T, D)

    def own_slot(parts):
        p = jnp.concatenate([t[0] for t in parts], axis=0).astype(BF16)
        return lax.dynamic_update_slice(lax.empty((N_SHARD,) + p.shape, BF16), p[None], (s_me, 0, 0))

    tr = lambda a: jnp.swapaxes(a, -1, -2)
    (wffn1,) = run_ride(ag_ride([own_slot([tr(ffn1_w_gate), tr(ffn1_w_up), ffn1_w_down])]), name="ag_ffn1")
    col = lambda v: v.reshape(SSM_HEADS, 1)
    d_skip_x = jnp.repeat(d_skip, SSM_HEAD_DIM, axis=1)
    cw_slot = lax.dynamic_update_slice(jnp.zeros((SSM_CONV, SSM_CONV_DIM), F32),
                                       conv_w[0] * (ci == 0).astype(F32), (0, s_me * (SSM_CONV_DIM // N_SHARD)))
    conv_w_full = small_allreduce(cw_slot.reshape(-1, 128), name="ag_conv_w").reshape(SSM_CONV, SSM_CONV_DIM)
    cwb = jnp.concatenate([conv_w_full, conv_b, jnp.zeros((HALO - SSM_CONV - 1, SSM_CONV_DIM), F32)], axis=0)
    conv_pack = jnp.stack([jnp.concatenate([cwb[:, SSM_GW * g:SSM_GW * (g + 1)],
                                            cwb[:, SSM_D_INNER + SSM_STATE * g:SSM_D_INNER + SSM_STATE * (g + 1)],
                                            cwb[:, SSM_D_INNER + SSM_STATE * (SSM_GROUPS + g):
                                                SSM_D_INNER + SSM_STATE * (SSM_GROUPS + g + 1)]], axis=1)
                           for g in range(SSM_GROUPS)])
    vec_pack = jnp.concatenate([d_skip_x, ssm_norm_g], axis=0)
    col_pack = jnp.concatenate([col(dt_bias), col(a_log)], axis=1)

    (h1, n1, gate1, up1, f1), (gin, gmix) = ffn_fwd(
        x2, ffn1_pre_g, wffn1, ffn1_post_g, name="ffn1_fwd",
        ride=ag_ride([own_slot([w_in]), own_slot([w_ssm_proj, w_attn_proj, w_out])]))
    w_in_full = gin.transpose(1, 0, 2).reshape(D, IN_COLS)
    w_gz = w_in_full[:, 0:4096]
    w_xbc = w_in_full[:, 4096:4096 + SSM_CONV_DIM]
    w_dtT = w_in_full[:, 7168:7200].T
    w_qkv = w_in_full[:, 7200:]
    (u, gates, z, xbc, dt_rawT, q, k, vv), (wffn2,) = mix_in_fwd(
        h1, mix_pre_g, w_gz, w_xbc, w_dtT, w_qkv, name="mix_in_fwd",
        ride=ag_ride([own_slot([tr(ffn2_w_gate), tr(ffn2_w_up), ffn2_w_down])]))
    scalars = ssd_scalars(dt_rawT, col(dt_bias), col(a_log), name="ssd_scalars")
    y, ys, states = ssd_fwd4(xbc, z, scalars, conv_pack, vec_pack, batch=batch, name="ssd_fwd")
    onehot = _bucket_onehot()
    bias = attn_bias(rel_bias_table.T, onehot, name="attn_bias").reshape(ATTN_Q_HEADS, ATTN_BLOCK, 2 * ATTN_BLOCK)
    bias = jnp.where(attn_window()[None], bias, MASKED)
    o, lse = attn_fwd(q, k, vv, bias, attn_sinks, batch=batch, name="attn_fwd")
    h2, y_ssm, y_attn, mix, merged = mix_out_fwd(ys, o, gates, h1, gmix, mix_post_g, name="mix_out_fwd")
    h3, n3, gate2, up2, f2, dy, loss_parts = ffn_fwd(h2, ffn2_pre_g, wffn2, ffn2_post_g, tgt, name="ffn2_fwd")

    where = jnp.stack([s_me, ci]).astype(jnp.int32)

    def chip_sums(grads, pair, tiles, tag):
        return [rs_add(g, p, where, rt=rt, name=f"rs_add_{tag}{i}") for i, (g, p, rt) in enumerate(zip(grads, pair, tiles))]

    def totals(parts, sums, tiles, tag):
        return [rs_total(p, s, where, rt=rt, name=f"rs_total_{tag}{i}")
                for i, (p, s, rt) in enumerate(zip(parts, sums, tiles))]

    def ffn_grads(n, dgate, dup, a, df, tag):
        d = mm_tn(dgate, n[None], into=(lax.empty(wffn1.shape, F32), 0), name="dw_gate" + tag)
        d = mm_tn(dup, n[None], into=(d, 1), name="dw_up" + tag)
        return [mm_tn(a, df[None], into=(d, 2), name="dw_down" + tag)]

    ffn_tiles, mix_tiles = [352], [256, 256]
    dh2, df2, a2, dgate2, dup2, dg_ffn2_pre, dg_ffn2_post = ffn_bwd(dy, h2, f2, gate2, up2, ffn2_pre_g, ffn2_post_g,
                                                                    wffn2, name="ffn2_bwd")
    d_f2 = ffn_grads(n3, dgate2, dup2, a2, df2, "2")
    (dmix, dyssm, dyattn, dgates, dys, do, dg_mix_post), pair_f2 = mix_out_bwd(
        dh2, mix, y_ssm, y_attn, gates, gmix, mix_post_g, name="mix_out_bwd", ride=pair_ride(d_f2))
    sums_f2 = chip_sums(d_f2, pair_f2, ffn_tiles, "f2")
    dq, dk, dv, dbias, dsinks = attn_bwd(q, k, vv, o, do, lse, bias, attn_sinks, batch=batch, name="attn_bwd")
    dtable = attn_bias_bwd(dbias.reshape(ATTN_Q_HEADS, -1), onehot, name="attn_bias_bwd").T
    (dz, dxs, dbm, dcm, ddtT, acc_xs, acc_bm, acc_cm, acc_head), parts_f2 = ssd_bwd4(
        dys, y, xbc, z, dt_rawT, states, scalars, conv_pack, col_pack, vec_pack,
        batch=batch, name="ssd_bwd", ride=chips_ride(sums_f2))
    tot_f2 = totals(parts_f2, sums_f2, ffn_tiles, "f2")
    dmx = mm_tn(ys[None], dyssm[None], a_cols=(N_SHARD, 512), into=(lax.empty(gmix.shape, F32), 0), name="dw_ssm")
    dmx = mm_tn(o[None], dyattn[None], a_cols=(N_SHARD, 256), into=(dmx, 2), name="dw_attn")
    dmx = mm_tn(merged[None], dmix[None], a_cols=(N_SHARD, 256), into=(dmx, 3), name="dw_out")
    ub = u[None]
    din = jnp.concatenate([
        mm_tn(ub, dgates[None], name="dw_in_gates", tn=1024)[0], mm_tn(ub, dz[None], name="dw_in_z", tn=1024)[0],
        mm_tn(ub, dxs[None], name="dw_in_xs", tn=1024)[0], mm_tn(ub, dbm[None], name="dw_in_b")[0],
        mm_tn(ub, dcm[None], name="dw_in_c")[0], mm_rows(ddtT, u, name="dw_in_dt").T,
        mm_tn(ub, dq[None], name="dw_in_q")[0], mm_tn(ub, dk[None], name="dw_in_k")[0],
        mm_tn(ub, dv[None], name="dw_in_v")[0]], axis=1)
    din = din.reshape(D, N_SHARD, IN_COLS // N_SHARD).transpose(1, 0, 2)
    d_mx = [dmx, din]
    (dh1, dg_mix_pre), (pair_mx0, pair_mx1, rffn2) = mix_in_bwd(
        dh2, h1, mix_pre_g, dgates, dz, dxs, dbm, dcm, ddtT, dq, dk, dv, w_gz, w_xbc, w_dtT, w_qkv, name="mix_in_bwd",
        ride=join_rides(pair_ride(d_mx), share_ride(tot_f2)))
    sums_mx = chip_sums(d_mx, [pair_mx0, pair_mx1], mix_tiles, "mx")
    (dx, df1, a1, dgate1, dup1, dg_ffn1_pre, dg_ffn1_post), parts_mx = ffn_bwd(
        dh1, x2, f1, gate1, up1, ffn1_pre_g, ffn1_post_g, wffn1, name="ffn1_bwd", ride=chips_ride(sums_mx))
    rmx, rin = run_ride(share_ride(totals(parts_mx, sums_mx, mix_tiles, "mx")), name="rs_share_mx")
    d_f1 = ffn_grads(n1, dgate1, dup1, a1, df1, "1")
    sums_f1 = chip_sums(d_f1, run_ride(pair_ride(d_f1), name="rs_pair_f1"), ffn_tiles, "f1")
    parts_f1 = run_ride(chips_ride(sums_f1), name="rs_chips_f1")
    (rffn1,) = run_ride(share_ride(totals(parts_f1, sums_f1, ffn_tiles, "f1")), name="rs_share_f1")
    FS = D_FF // N_SHARD
    gw = {
        'ffn1_w_gate': rffn1[0:FS], 'ffn1_w_up': rffn1[FS:2 * FS], 'ffn1_w_down': rffn1[2 * FS:],
        'ffn2_w_gate': rffn2[0:FS], 'ffn2_w_up': rffn2[FS:2 * FS], 'ffn2_w_down': rffn2[2 * FS:],
        'w_ssm_proj': rmx[0:512], 'w_attn_proj': rmx[512:768], 'w_out': rmx[768:1024], 'w_in': rin,
    }

    dconv_w = jnp.concatenate([acc[:, :SSM_CONV].transpose(1, 0, 2).reshape(SSM_CONV, -1)
                               for acc in (acc_xs, acc_bm, acc_cm)], axis=1)
    dconv_b = jnp.concatenate([acc[:, SSM_CONV].reshape(-1) for acc in (acc_xs, acc_bm, acc_cm)])
    small_local = {
        'ffn1_pre_g': dg_ffn1_pre, 'ffn1_post_g': dg_ffn1_post, 'mix_pre_g': dg_mix_pre, 'conv_w': dconv_w,
        'conv_b': dconv_b, 'dt_bias': acc_head[:, :, 0], 'a_log': acc_head[:, :, 1],
        'd_skip': acc_xs[:, SSM_CONV + 2].reshape(SSM_HEADS, SSM_HEAD_DIM).sum(axis=1),
        'ssm_norm_g': acc_xs[:, SSM_CONV + 1].reshape(-1), 'attn_sinks': dsinks, 'rel_bias_table': dtable,
        'mix_post_g': dg_mix_post, 'ffn2_pre_g': dg_ffn2_pre, 'ffn2_post_g': dg_ffn2_post,
    }
    full_shapes = [(SSM_CONV, SSM_CONV_DIM) if n == 'conv_w' else w[n].shape for n in SMALL]
    packed = _pack_rows([small_local[n] for n in SMALL] + [jnp.sum(loss_parts[:, 0, 0])])
    total = small_allreduce(packed, name="allreduce_small")
    *small_g, loss = _unpack_rows(total, full_shapes + [()])
    for n, g in zip(SMALL, small_g):
        gw[n] = g
    gw['conv_w'] = lax.dynamic_slice(gw['conv_w'], (0, s_me * (SSM_CONV_DIM // N_SHARD)),
                                     (SSM_CONV, SSM_CONV_DIM // N_SHARD))[None]

    delta, new_m, new_v = {}, {}, {}
    for n in BIG:
        lay = tr if n.endswith(('w_gate', 'w_up')) else (lambda a: a)
        d_, m_, v_ = adamw(lay(w[n][0]), gw[n], lay(m[n][0]), lay(v[n][0]), name="adamw_" + n, rt=gw[n].shape[0] // 4)
        gw[n] = lay(gw[n])[None]
        delta[n], new_m[n], new_v[n] = lay(d_)[None], lay(m_)[None], lay(v_)[None]
    shapes = [w[n].shape for n in SMALL]
    outs = adamw(_pack_rows([w[n] for n in SMALL]), _pack_rows([gw[n] for n in SMALL]),
                 _pack_rows([m[n] for n in SMALL]), _pack_rows([v[n] for n in SMALL]), name="adamw_small")
    for res, buf in zip((delta, new_m, new_v), outs):
        for n, val in zip(SMALL, _unpack_rows(buf, shapes)):
            res[n] = val
    return (loss, dx.reshape(batch, seq, D), *[gw[n].reshape(w[n].shape) for n in WEIGHTS],
            *[delta[n] for n in WEIGHTS], *[new_m[n] for n in WEIGHTS], *[new_v[n] for n in WEIGHTS])
```

```python
import functools
import math

import jax
import jax.numpy as jnp
import numpy as np
from jax import lax
from jax.experimental import pallas as pl
from jax.experimental.pallas import tpu as pltpu

F32 = jnp.float32
BF16 = jnp.bfloat16

D_MODEL = 1024
D_FF = 2816
N_SHARD = 4
SSM_D_INNER = 2048
SSM_HEAD_DIM = 64
SSM_HEADS = 32
SSM_GROUPS = 4
SSM_HPG = SSM_HEADS // SSM_GROUPS
SSM_GW = SSM_D_INNER // SSM_GROUPS
SSM_STATE = 128
SSM_CONV = 4
SSM_CHUNK = 128
SSM_CONV_DIM = SSM_D_INNER + 2 * SSM_GROUPS * SSM_STATE
ATTN_Q_HEADS = 16
ATTN_KV_HEADS = 4
ATTN_REP = ATTN_Q_HEADS // ATTN_KV_HEADS
ATTN_HEAD_DIM = 64
ATTN_BLOCK = 128
ATTN_Q_DIM = 1024
ATTN_KV_DIM = 256
REL_BUCKETS = 32
REL_MAX_DISTANCE = 128
RMS_EPS = 1e-6
IN_COLS = 8736
ADAM_LR = 0.001
ADAM_B1 = 0.9
ADAM_B2 = 0.999
ADAM_EPS = 1e-08
ADAM_WD = 0.01
ADAM_STEP = 10
HALO = 8

VMEM_LIMIT = 56 * 1024 * 1024


def _cparams(*sem):
    return pltpu.CompilerParams(dimension_semantics=tuple(sem) if sem else None, vmem_limit_bytes=VMEM_LIMIT)


def _dot(a, b):
    return jnp.dot(a, b, preferred_element_type=F32)


def _dot_nt(a, b):
    return lax.dot_general(a, b, (((1,), (1,)), ((), ())), preferred_element_type=F32)


def _dot_tn(a, b):
    return lax.dot_general(a, b, (((0,), (0,)), ((), ())), preferred_element_type=F32)


def _dot_hi(a, b):
    return jnp.dot(a, b, preferred_element_type=F32, precision=lax.Precision.HIGHEST)


def _sigmoid(x):
    return 0.5 * jnp.tanh(0.5 * x) + 0.5


def _resident(shape, index=None):
    index = (0,) * len(shape) if index is None else tuple(index)
    return pl.BlockSpec(shape, lambda *_: index, pipeline_mode=pl.Buffered(1))


def _part(packed, rows, part):
    return _resident((N_SHARD, rows, packed.shape[2]), (0, part, 0))


def _rows(tm, width):
    return pl.BlockSpec((tm, width), lambda i: (i, 0))


class Ride:
    def __init__(self, inputs, out_shapes, aliases, scratch, start, finish):
        self.inputs, self.out_shapes, self.aliases = list(inputs), list(out_shapes), list(aliases)
        self.scratch, self.start, self.finish = list(scratch), start, finish


def join_rides(*rides):
    def cut(refs, sizes):
        out, at = [], 0
        for n in sizes:
            out.append(refs[at:at + n])
            at += n
        return out

    k_in = [len(r.inputs) for r in rides]
    k_out = [len(r.out_shapes) for r in rides]
    k_scr = [len(r.scratch) for r in rides]

    def each(step):
        def run(ins, outs, sems):
            for r, i, o, s in zip(rides, cut(ins, k_in), cut(outs, k_out), cut(sems, k_scr)):
                getattr(r, step)(i, o, s)
        return run

    aliases = [(sum(k_in[:n]) + i, sum(k_out[:n]) + j) for n, r in enumerate(rides) for i, j in r.aliases]
    return Ride([a for r in rides for a in r.inputs], [s for r in rides for s in r.out_shapes], aliases,
                [s for r in rides for s in r.scratch], each("start"), each("finish"))


def _call(body, *, grid, in_specs, args, out_specs, out_shape, name, sem, scratch=(), aliases=None, ride=None):
    aliases = dict(aliases or {})
    if ride is None:
        return pl.pallas_call(body, grid=grid, in_specs=in_specs, out_specs=out_specs, out_shape=out_shape,
                              scratch_shapes=list(scratch), input_output_aliases=aliases,
                              compiler_params=_cparams(*sem), name=name)(*args)
    n_in, n_out, n_scr = len(in_specs), len(out_specs), len(scratch)
    k_in, k_out = len(ride.inputs), len(ride.out_shapes)

    def riding(*refs):
        ins, refs = refs[:n_in], refs[n_in:]
        ex_in, refs = refs[:k_in], refs[k_in:]
        outs, refs = refs[:n_out], refs[n_out:]
        ex_out, refs = refs[:k_out], refs[k_out:]
        scr, ex_scr = refs[:n_scr], refs[n_scr:]
        first = functools.reduce(jnp.logical_and, [pl.program_id(a) == 0 for a in range(len(grid))])
        last = functools.reduce(jnp.logical_and, [pl.program_id(a) == grid[a] - 1 for a in range(len(grid))])

        @pl.when(first)
        def _():
            ride.start(ex_in, ex_out, ex_scr)

        body(*ins, *outs, *scr)

        @pl.when(last)
        def _():
            ride.finish(ex_in, ex_out, ex_scr)

    aliases.update({n_in + i: n_out + j for i, j in ride.aliases})
    res = pl.pallas_call(
        riding, grid=grid, in_specs=list(in_specs) + [ANY] * k_in, out_specs=list(out_specs) + [ANY] * k_out,
        out_shape=list(out_shape) + ride.out_shapes, scratch_shapes=list(scratch) + ride.scratch,
        input_output_aliases=aliases, compiler_params=_cparams(*["arbitrary"] * len(grid)), name=name,
    )(*args, *ride.inputs)
    return res[:n_out], res[n_out:]


def run_ride(ride, *, name):
    k_in = len(ride.inputs)

    def body(*refs):
        ex_in, ex_out, sems = refs[:k_in], refs[k_in:k_in + len(ride.out_shapes)], refs[k_in + len(ride.out_shapes):]
        ride.start(ex_in, ex_out, sems)
        ride.finish(ex_in, ex_out, sems)

    return pl.pallas_call(body, in_specs=[ANY] * k_in, out_specs=[ANY] * len(ride.out_shapes),
                          out_shape=ride.out_shapes, scratch_shapes=ride.scratch,
                          input_output_aliases=dict(ride.aliases), name=name)(*ride.inputs)


def ffn_fwd(h, g_pre, wffn, g_post, target=None, *, name, tm=512, ride=None):
    T, D = h.shape
    NS, FS = N_SHARD, wffn.shape[1] // 3
    with_loss = target is not None
    nt = T // tm

    def body(*refs):
        if with_loss:
            (h_ref, gpre_ref, wg_ref, wu_ref, wd_ref, gpost_ref, tgt_ref,
             hout_ref, n_ref, gate_ref, up_ref, f_ref, dy_ref, loss_ref) = refs
        else:
            (h_ref, gpre_ref, wg_ref, wu_ref, wd_ref, gpost_ref,
             hout_ref, n_ref, gate_ref, up_ref, f_ref) = refs
        hh = h_ref[...]
        r = lax.rsqrt(jnp.mean(hh * hh, axis=-1, keepdims=True) + RMS_EPS)
        n = (hh * r * gpre_ref[...]).astype(BF16)
        n_ref[...] = n
        acc = jnp.zeros((tm, D), F32)
        for s in range(NS):
            gate = _dot_nt(n, wg_ref[s])
            up = _dot_nt(n, wu_ref[s])
            gate_ref[s] = gate.astype(BF16)
            up_ref[s] = up.astype(BF16)
            a = (gate * _sigmoid(gate) * up).astype(BF16)
            acc = acc + _dot(a, wd_ref[s])
        f_ref[...] = acc
        r2 = lax.rsqrt(jnp.mean(acc * acc, axis=-1, keepdims=True) + RMS_EPS)
        out = hh + 0.5 * (acc * r2 * gpost_ref[...])
        hout_ref[...] = out
        if with_loss:
            e = out - tgt_ref[...]
            dy_ref[...] = e * (1.0 / D)
            loss_ref[...] = jnp.full((1, 8, 128), 0.5 / D, F32) * jnp.sum(e * e)

    in_specs = [_rows(tm, D), _resident((1, D)), _part(wffn, FS, 0), _part(wffn, FS, 1), _part(wffn, FS, 2),
                _resident((1, D))]
    args = [h, g_pre, wffn, wffn, wffn, g_post]
    out_shape = [jax.ShapeDtypeStruct((T, D), F32), jax.ShapeDtypeStruct((T, D), BF16),
                 jax.ShapeDtypeStruct((NS, T, FS), BF16), jax.ShapeDtypeStruct((NS, T, FS), BF16),
                 jax.ShapeDtypeStruct((T, D), F32)]
    seg = pl.BlockSpec((NS, tm, FS), lambda i: (0, i, 0))
    out_specs = [_rows(tm, D), _rows(tm, D), seg, seg, _rows(tm, D)]
    if with_loss:
        in_specs.append(_rows(tm, D))
        args.append(target)
        out_shape += [jax.ShapeDtypeStruct((T, D), F32), jax.ShapeDtypeStruct((nt, 8, 128), F32)]
        out_specs += [_rows(tm, D), pl.BlockSpec((1, 8, 128), lambda i: (i, 0, 0))]
    return _call(body, grid=(nt,), in_specs=in_specs, args=args, out_specs=out_specs, out_shape=out_shape,
                 sem=("parallel",), name=name, ride=ride)


def ffn_bwd(dout, h, f, gate, up, g_pre, g_post, wffn, *, name, tm=256, ride=None):
    T, D = h.shape
    NS, FS = N_SHARD, wffn.shape[1] // 3
    nt = T // tm

    def body(dout_ref, h_ref, f_ref, gate_ref, up_ref, gpre_ref, gpost_ref, wg_ref, wu_ref, wd_ref,
             dh_ref, df_ref, a_ref, dgate_ref, dup_ref, dgpre_ref, dgpost_ref):
        @pl.when(pl.program_id(0) == 0)
        def _():
            dgpre_ref[...] = jnp.zeros_like(dgpre_ref)
            dgpost_ref[...] = jnp.zeros_like(dgpost_ref)

        do = dout_ref[...]
        ff = f_ref[...]
        d_fn = 0.5 * do
        r2 = lax.rsqrt(jnp.mean(ff * ff, axis=-1, keepdims=True) + RMS_EPS)
        dgpost_ref[...] += jnp.sum(d_fn * ff * r2, axis=0, keepdims=True)
        t = d_fn * gpost_ref[...]
        df = r2 * t - ff * (r2 * r2 * r2 * jnp.mean(t * ff, axis=-1, keepdims=True))
        dfb = df.astype(BF16)
        df_ref[...] = dfb
        dn = jnp.zeros((tm, D), F32)
        for s in range(NS):
            da = _dot_nt(dfb, wd_ref[s])
            g = gate_ref[s].astype(F32)
            u = up_ref[s].astype(F32)
            sg = _sigmoid(g)
            silu = g * sg
            a_ref[s] = (silu * u).astype(BF16)
            dgt = (da * u * (sg * (1.0 + g * (1.0 - sg)))).astype(BF16)
            dupv = (da * silu).astype(BF16)
            dgate_ref[s] = dgt
            dup_ref[s] = dupv
            dn = dn + _dot(dgt, wg_ref[s]) + _dot(dupv, wu_ref[s])
        hh = h_ref[...]
        r1 = lax.rsqrt(jnp.mean(hh * hh, axis=-1, keepdims=True) + RMS_EPS)
        dgpre_ref[...] += jnp.sum(dn * hh * r1, axis=0, keepdims=True)
        t = dn * gpre_ref[...]
        dh_ref[...] = do + r1 * t - hh * (r1 * r1 * r1 * jnp.mean(t * hh, axis=-1, keepdims=True))

    seg = pl.BlockSpec((NS, tm, FS), lambda i: (0, i, 0))
    acc = pl.BlockSpec((1, D), lambda i: (0, 0))
    return _call(
        body, grid=(nt,),
        in_specs=[_rows(tm, D), _rows(tm, D), _rows(tm, D), seg, seg, _resident((1, D)), _resident((1, D)),
                  _part(wffn, FS, 0), _part(wffn, FS, 1), _part(wffn, FS, 2)],
        args=[dout, h, f, gate, up, g_pre, g_post, wffn, wffn, wffn],
        out_specs=[_rows(tm, D), _rows(tm, D), seg, seg, seg, acc, acc],
        out_shape=[jax.ShapeDtypeStruct((T, D), F32), jax.ShapeDtypeStruct((T, D), BF16),
                   jax.ShapeDtypeStruct((NS, T, FS), BF16), jax.ShapeDtypeStruct((NS, T, FS), BF16),
                   jax.ShapeDtypeStruct((NS, T, FS), BF16),
                   jax.ShapeDtypeStruct((1, D), F32), jax.ShapeDtypeStruct((1, D), F32)],
        sem=("arbitrary",), name=name, ride=ride)


def mm_tn(a, g, *, name, tt=4096, tn=None, a_cols=None, into=None):
    Ba, T, _ = a.shape
    Bg, _, N = g.shape
    B, K = a_cols if a_cols else (max(Ba, Bg), a.shape[2])
    tn = N if tn is None else tn
    tt = min(tt, T)
    nsteps = T // tt

    def body(*refs):
        a_ref, g_ref, o_ref = refs[0], refs[1], refs[-1]

        @pl.when(pl.program_id(2) == 0)
        def _():
            o_ref[...] = jnp.zeros_like(o_ref)

        o_ref[0] += _dot_tn(a_ref[0], g_ref[0].astype(BF16))

    if a_cols:
        a_map = lambda b, j, t: (0, t, b)
    else:
        a_map = (lambda b, j, t: (b, t, 0)) if Ba > 1 else (lambda b, j, t: (0, t, 0))
    in_specs = [pl.BlockSpec((1, tt, K), a_map),
                pl.BlockSpec((1, tt, tn), (lambda b, j, t: (b, t, j)) if Bg > 1 else (lambda b, j, t: (0, t, j)))]
    args = [a, g]
    if into is None:
        out_shape, part, aliases = jax.ShapeDtypeStruct((B, K, N), F32), 0, {}
    else:
        buf, part = into
        out_shape, aliases = jax.ShapeDtypeStruct(buf.shape, F32), {2: 0}
        in_specs.append(ANY)
        args.append(buf)
    return pl.pallas_call(
        body, grid=(B, N // tn, nsteps), in_specs=in_specs,
        out_specs=pl.BlockSpec((1, K, tn), lambda b, j, t: (b, part, j)),
        out_shape=out_shape, input_output_aliases=aliases,
        compiler_params=_cparams("parallel", "parallel", "arbitrary"), name=name)(*args)


def mix_in_fwd(h, g, w_gz, w_xbc, w_dtT, w_qkv, *, name, tm=256, ride=None):
    T, D = h.shape
    nt = T // tm
    CB = 1024

    def body(h_ref, g_ref, wgz_ref, wxbc_ref, wdtT_ref, wqkv_ref,
             u_ref, gates_ref, z_ref, xbc_ref, dtT_ref, q_ref, k_ref, v_ref):
        hh = h_ref[...]
        r = lax.rsqrt(jnp.mean(hh * hh, axis=-1, keepdims=True) + RMS_EPS)
        u = (hh * r * g_ref[...]).astype(BF16)
        u_ref[...] = u
        for cb in range(0, 2048, CB):
            gates_ref[:, cb:cb + CB] = _dot(u, wgz_ref[:, cb:cb + CB]).astype(BF16)
            z_ref[:, cb:cb + CB] = _dot(u, wgz_ref[:, 2048 + cb:2048 + cb + CB])
        for cb in range(0, SSM_CONV_DIM, CB):
            xbc_ref[:, cb:cb + CB] = _dot(u, wxbc_ref[:, cb:cb + CB])
        dtT_ref[...] = _dot_nt(wdtT_ref[...], u)
        q_ref[...] = (_dot(u, wqkv_ref[:, 0:ATTN_Q_DIM]) * ATTN_SCALE).astype(BF16)
        k_ref[...] = _dot(u, wqkv_ref[:, ATTN_Q_DIM:ATTN_Q_DIM + ATTN_KV_DIM]).astype(BF16)
        v_ref[...] = _dot(u, wqkv_ref[:, ATTN_Q_DIM + ATTN_KV_DIM:]).astype(BF16)

    sds = jax.ShapeDtypeStruct
    return _call(
        body, grid=(nt,),
        in_specs=[_rows(tm, D), _resident((1, D)), _resident(w_gz.shape), _resident(w_xbc.shape),
                  _resident(w_dtT.shape), _resident(w_qkv.shape)],
        args=[h, g, w_gz, w_xbc, w_dtT, w_qkv],
        out_specs=[_rows(tm, D), _rows(tm, 2048), _rows(tm, 2048), _rows(tm, SSM_CONV_DIM),
                   pl.BlockSpec((SSM_HEADS, tm), lambda i: (0, i)),
                   _rows(tm, ATTN_Q_DIM), _rows(tm, ATTN_KV_DIM), _rows(tm, ATTN_KV_DIM)],
        out_shape=[sds((T, D), BF16), sds((T, 2048), BF16), sds((T, 2048), F32), sds((T, SSM_CONV_DIM), F32),
                   sds((SSM_HEADS, T), F32),
                   sds((T, ATTN_Q_DIM), BF16), sds((T, ATTN_KV_DIM), BF16), sds((T, ATTN_KV_DIM), BF16)],
        sem=("parallel",), name=name, ride=ride)


def _softplus(x):
    return jnp.maximum(x, 0.0) + jnp.log(1.0 + jnp.exp(-jnp.abs(x)))


def _iota(shape, axis):
    return lax.broadcasted_iota(jnp.int32, shape, axis)


def _attn_specs(nb):
    BLK = ATTN_BLOCK

    def specs(last):
        def cur(b, n):
            return b * nb + (n if last is None else jnp.minimum(n, nb - 1))

        def prev(b, n):
            return b * nb + jnp.maximum((n if last is None else jnp.minimum(n, nb - 1)) - 1, 0)
        return cur, prev
    return specs


MASKED = -1e30
ATTN_SCALE = ATTN_HEAD_DIM ** -0.5


def attn_window():
    i = np.arange(ATTN_BLOCK)[:, None]
    j = np.arange(2 * ATTN_BLOCK)[None, :]
    return (j > i) & (j <= i + ATTN_BLOCK)


def _attn_group(kk, q_ref, bias_ref, sink_ref):
    BLK, HD = ATTN_BLOCK, ATTN_HEAD_DIM
    heads = range(ATTN_REP * kk, ATTN_REP * (kk + 1))
    qg = jnp.concatenate([q_ref[:, HD * hd:HD * (hd + 1)] for hd in heads], axis=0)
    bias_p = jnp.concatenate([bias_ref[hd, :, 0:BLK] for hd in heads], axis=0)
    bias_c = jnp.concatenate([bias_ref[hd, :, BLK:2 * BLK] for hd in heads], axis=0)
    sink = jnp.concatenate([jnp.broadcast_to(sink_ref[0:1, hd:hd + 1], (BLK, 1)) for hd in heads], axis=0)
    return qg, bias_p, bias_c, sink


def attn_bias(table_t, onehot, *, name):
    def body(t_ref, f_ref, o_ref):
        o_ref[...] = _dot_hi(t_ref[...], f_ref[...])
    return pl.pallas_call(body, out_shape=jax.ShapeDtypeStruct((ATTN_Q_HEADS, onehot.shape[1]), F32),
                          compiler_params=_cparams(), name=name)(table_t, onehot)


def attn_bias_bwd(dbias, onehot, *, name):
    def body(d_ref, f_ref, o_ref):
        o_ref[...] = lax.dot_general(d_ref[...], f_ref[...], (((1,), (1,)), ((), ())), preferred_element_type=F32,
                                     precision=lax.Precision.HIGHEST)
    return pl.pallas_call(body, out_shape=jax.ShapeDtypeStruct((ATTN_Q_HEADS, REL_BUCKETS), F32),
                          compiler_params=_cparams(), name=name)(dbias, onehot)


def attn_fwd(q, k, v, bias, sinks, *, batch, name):
    T = q.shape[0]
    BLK, HD = ATTN_BLOCK, ATTN_HEAD_DIM
    nb = T // batch // BLK
    cur, prev = _attn_specs(nb)(None)

    def body(q_ref, kc_ref, kp_ref, vc_ref, vp_ref, bias_ref, sink_ref, o_ref, lse_ref):
        n = pl.program_id(1)
        for kk in range(ATTN_KV_HEADS):
            ks = slice(HD * kk, HD * (kk + 1))
            kc, kp, vc, vp = kc_ref[:, ks], kp_ref[:, ks], vc_ref[:, ks], vp_ref[:, ks]
            qg, bias_p, bias_c, sink = _attn_group(kk, q_ref, bias_ref, sink_ref)
            lp = jnp.where(n > 0, _dot_nt(qg, kp) + bias_p, MASKED)
            lc = _dot_nt(qg, kc) + bias_c
            mx = jnp.maximum(jnp.max(jnp.maximum(lp, lc), axis=-1, keepdims=True), sink)
            pp = jnp.exp(lp - mx)
            pc = jnp.exp(lc - mx)
            den = jnp.sum(pp + pc, axis=-1, keepdims=True) + jnp.exp(sink - mx)
            o = ((_dot(pp.astype(BF16), vp) + _dot(pc.astype(BF16), vc)) * (1.0 / den)).astype(BF16)
            lse = mx + jnp.log(den)
            for r in range(ATTN_REP):
                hd = ATTN_REP * kk + r
                o_ref[:, HD * hd:HD * (hd + 1)] = o[BLK * r:BLK * (r + 1)]
                lse_ref[:, hd:hd + 1] = lse[BLK * r:BLK * (r + 1)]

    sds = jax.ShapeDtypeStruct
    return pl.pallas_call(
        body, grid=(batch, nb),
        in_specs=[pl.BlockSpec((BLK, ATTN_Q_DIM), lambda b, n: (cur(b, n), 0)),
                  pl.BlockSpec((BLK, ATTN_KV_DIM), lambda b, n: (cur(b, n), 0)),
                  pl.BlockSpec((BLK, ATTN_KV_DIM), lambda b, n: (prev(b, n), 0)),
                  pl.BlockSpec((BLK, ATTN_KV_DIM), lambda b, n: (cur(b, n), 0)),
                  pl.BlockSpec((BLK, ATTN_KV_DIM), lambda b, n: (prev(b, n), 0)),
                  pl.BlockSpec((ATTN_Q_HEADS, BLK, 2 * BLK), lambda b, n: (0, 0, 0)),
                  pl.BlockSpec((1, ATTN_Q_HEADS), lambda b, n: (0, 0))],
        out_specs=[pl.BlockSpec((BLK, ATTN_Q_DIM), lambda b, n: (cur(b, n), 0)),
                   pl.BlockSpec((BLK, ATTN_Q_HEADS), lambda b, n: (cur(b, n), 0))],
        out_shape=[sds((T, ATTN_Q_DIM), BF16), sds((T, ATTN_Q_HEADS), F32)],
        compiler_params=_cparams("parallel", "parallel"), name=name)(q, k, k, v, v, bias, sinks)


def _proj_specs(wmix):
    return [_part(wmix, 512, 0), _part(wmix, 256, 2), _part(wmix, 256, 3)]


def _natural(w_ref):
    return w_ref[...].reshape(-1, w_ref.shape[2])


def mix_out_fwd(ys, o, gates, h, wmix, g_post, *, name, tm=512):
    T, D = h.shape
    nt = T // tm

    def body(ys_ref, o_ref, gates_ref, h_ref, wssm_ref, wattn_ref, wout_ref, g_ref,
             hout_ref, yssm_ref, yattn_ref, mix_ref, merged_ref):
        y_ssm = _dot(ys_ref[...], _natural(wssm_ref))
        y_attn = _dot(o_ref[...], _natural(wattn_ref))
        yssm_ref[...] = y_ssm.astype(BF16)
        yattn_ref[...] = y_attn.astype(BF16)
        merged = (_sigmoid(gates_ref[:, 0:D].astype(F32)) * y_ssm
                  + _sigmoid(gates_ref[:, D:2 * D].astype(F32)) * y_attn).astype(BF16)
        merged_ref[...] = merged
        mix = _dot(merged, _natural(wout_ref))
        mix_ref[...] = mix.astype(BF16)
        r = lax.rsqrt(jnp.mean(mix * mix, axis=-1, keepdims=True) + RMS_EPS)
        hout_ref[...] = h_ref[...] + mix * r * g_ref[...]

    sds = jax.ShapeDtypeStruct
    return pl.pallas_call(
        body, grid=(nt,),
        in_specs=[_rows(tm, SSM_D_INNER), _rows(tm, ATTN_Q_DIM), _rows(tm, 2 * D), _rows(tm, D),
                  *_proj_specs(wmix), _resident((1, D))],
        out_specs=[_rows(tm, D)] * 5,
        out_shape=[sds((T, D), F32), sds((T, D), BF16), sds((T, D), BF16), sds((T, D), BF16), sds((T, D), BF16)],
        compiler_params=_cparams("parallel"), name=name)(ys, o, gates, h, wmix, wmix, wmix, g_post)


def mix_out_bwd(dh, mix, y_ssm, y_attn, gates, wmix, g_post, *, name, tm=256, ride=None):
    T, D = dh.shape
    nt = T // tm

    def body(dh_ref, mix_ref, yssm_ref, yattn_ref, gates_ref, wssm_ref, wattn_ref, wout_ref, g_ref,
             dmix_ref, dyssm_ref, dyattn_ref, dgates_ref, dys_ref, do_ref, dg_ref):
        @pl.when(pl.program_id(0) == 0)
        def _():
            dg_ref[...] = jnp.zeros_like(dg_ref)

        do = dh_ref[...]
        mix = mix_ref[...].astype(F32)
        r = lax.rsqrt(jnp.mean(mix * mix, axis=-1, keepdims=True) + RMS_EPS)
        dg_ref[...] += jnp.sum(do * mix * r, axis=0, keepdims=True)
        t = do * g_ref[...]
        dmix = (r * t - mix * (r * r * r * jnp.mean(t * mix, axis=-1, keepdims=True))).astype(BF16)
        dmix_ref[...] = dmix
        dmerged = _dot_nt(dmix, _natural(wout_ref))
        s1 = _sigmoid(gates_ref[:, 0:D].astype(F32))
        s2 = _sigmoid(gates_ref[:, D:2 * D].astype(F32))
        dyssm = (dmerged * s1).astype(BF16)
        dyattn = (dmerged * s2).astype(BF16)
        dyssm_ref[...] = dyssm
        dyattn_ref[...] = dyattn
        dgates_ref[:, 0:D] = (dmerged * yssm_ref[...].astype(F32) * (s1 * (1.0 - s1))).astype(BF16)
        dgates_ref[:, D:2 * D] = (dmerged * yattn_ref[...].astype(F32) * (s2 * (1.0 - s2))).astype(BF16)
        dys_ref[...] = _dot_nt(dyssm, _natural(wssm_ref))
        do_ref[...] = _dot_nt(dyattn, _natural(wattn_ref)).astype(BF16)

    sds = jax.ShapeDtypeStruct
    return _call(
        body, grid=(nt,),
        in_specs=[_rows(tm, D), _rows(tm, D), _rows(tm, D), _rows(tm, D), _rows(tm, 2 * D),
                  *_proj_specs(wmix), _resident((1, D))],
        args=[dh, mix, y_ssm, y_attn, gates, wmix, wmix, wmix, g_post],
        out_specs=[_rows(tm, D), _rows(tm, D), _rows(tm, D), _rows(tm, 2 * D), _rows(tm, SSM_D_INNER),
                   _rows(tm, ATTN_Q_DIM), pl.BlockSpec((1, D), lambda i: (0, 0))],
        out_shape=[sds((T, D), BF16), sds((T, D), BF16), sds((T, D), BF16), sds((T, 2 * D), BF16),
                   sds((T, SSM_D_INNER), F32), sds((T, ATTN_Q_DIM), BF16), sds((1, D), F32)],
        sem=("arbitrary",), name=name, ride=ride)


def attn_bwd(q, k, v, o, do, lse, bias, sinks, *, batch, name):
    T = q.shape[0]
    BLK, HD = ATTN_BLOCK, ATTN_HEAD_DIM
    nb = T // batch // BLK
    cur, prev = _attn_specs(nb)(nb)
    scale = HD ** -0.5

    def body(q_ref, kc_ref, kp_ref, vc_ref, vp_ref, o_ref, do_ref, lse_ref, bias_ref, sink_ref,
             dq_ref, dk_ref, dv_ref, dbias_ref, dsink_ref, ck, cv):
        b = pl.program_id(0)
        n = pl.program_id(1)

        @pl.when(jnp.logical_and(b == 0, n == 0))
        def _():
            dbias_ref[...] = jnp.zeros_like(dbias_ref)
            dsink_ref[...] = jnp.zeros_like(dsink_ref)

        @pl.when(n == 0)
        def _():
            ck[...] = jnp.zeros_like(ck)
            cv[...] = jnp.zeros_like(cv)

        @pl.when(n == nb)
        def _():
            dk_ref[...] = ck[...].astype(BF16)
            dv_ref[...] = cv[...].astype(BF16)

        @pl.when(n < nb)
        def _():
            lane16 = _iota((1, ATTN_Q_HEADS), 1)
            dsink = jnp.zeros((1, ATTN_Q_HEADS), F32)
            for kk in range(ATTN_KV_HEADS):
                ks = slice(HD * kk, HD * (kk + 1))
                kc, kp, vc, vp = kc_ref[:, ks], kp_ref[:, ks], vc_ref[:, ks], vp_ref[:, ks]
                heads = range(ATTN_REP * kk, ATTN_REP * (kk + 1))
                qg, bias_p, bias_c, sink = _attn_group(kk, q_ref, bias_ref, sink_ref)
                dog = jnp.concatenate([do_ref[:, HD * hd:HD * (hd + 1)] for hd in heads], axis=0)
                og = jnp.concatenate([o_ref[:, HD * hd:HD * (hd + 1)] for hd in heads], axis=0)
                lse = jnp.concatenate([lse_ref[:, hd:hd + 1] for hd in heads], axis=0)
                lp = jnp.where(n > 0, _dot_nt(qg, kp) + bias_p, MASKED)
                lc = _dot_nt(qg, kc) + bias_c
                pp = jnp.exp(lp - lse)
                pc = jnp.exp(lc - lse)
                delta = jnp.sum(dog.astype(F32) * og.astype(F32), axis=-1, keepdims=True)
                dlp = pp * (_dot_nt(dog, vp) - delta)
                dlc = pc * (_dot_nt(dog, vc) - delta)
                sd = jnp.exp(sink - lse) * delta
                dlpb = dlp.astype(BF16)
                dlcb = dlc.astype(BF16)
                dqg = ((_dot(dlpb, kp) + _dot(dlcb, kc)) * scale).astype(BF16)
                for r, hd in enumerate(heads):
                    rows = slice(BLK * r, BLK * (r + 1))
                    dsink = dsink + jnp.where(lane16 == hd, -jnp.sum(sd[rows], axis=0, keepdims=True), 0.0)
                    dbias_ref[hd, :, 0:BLK] += dlp[rows]
                    dbias_ref[hd, :, BLK:2 * BLK] += dlc[rows]
                    dq_ref[:, HD * hd:HD * (hd + 1)] = dqg[rows]
                dk_ref[:, ks] = (ck[:, ks] + _dot_tn(dlpb, qg)).astype(BF16)
                dv_ref[:, ks] = (cv[:, ks] + _dot_tn(pp.astype(BF16), dog)).astype(BF16)
                ck[:, ks] = _dot_tn(dlcb, qg)
                cv[:, ks] = _dot_tn(pc.astype(BF16), dog)
            dsink_ref[...] += dsink

    sds = jax.ShapeDtypeStruct
    qspec = pl.BlockSpec((BLK, ATTN_Q_DIM), lambda b, n: (cur(b, n), 0))
    cspec = pl.BlockSpec((BLK, ATTN_KV_DIM), lambda b, n: (cur(b, n), 0))
    pspec = pl.BlockSpec((BLK, ATTN_KV_DIM), lambda b, n: (prev(b, n), 0))
    late = pl.BlockSpec((BLK, ATTN_KV_DIM), lambda b, n: (b * nb + jnp.maximum(n - 1, 0), 0))
    return pl.pallas_call(
        body, grid=(batch, nb + 1),
        in_specs=[qspec, cspec, pspec, cspec, pspec, qspec, qspec,
                  pl.BlockSpec((BLK, ATTN_Q_HEADS), lambda b, n: (cur(b, n), 0)),
                  pl.BlockSpec((ATTN_Q_HEADS, BLK, 2 * BLK), lambda b, n: (0, 0, 0)),
                  pl.BlockSpec((1, ATTN_Q_HEADS), lambda b, n: (0, 0))],
        out_specs=[qspec, late, late,
                   pl.BlockSpec((ATTN_Q_HEADS, BLK, 2 * BLK), lambda b, n: (0, 0, 0)),
                   pl.BlockSpec((1, ATTN_Q_HEADS), lambda b, n: (0, 0))],
        out_shape=[sds((T, ATTN_Q_DIM), BF16), sds((T, ATTN_KV_DIM), BF16), sds((T, ATTN_KV_DIM), BF16),
                   sds((ATTN_Q_HEADS, BLK, 2 * BLK), F32), sds((1, ATTN_Q_HEADS), F32)],
        scratch_shapes=[pltpu.VMEM((BLK, ATTN_KV_DIM), F32), pltpu.VMEM((BLK, ATTN_KV_DIM), F32)],
        compiler_params=_cparams("arbitrary", "arbitrary"), name=name)(q, k, k, v, v, o, do, lse, bias, sinks)


def _conv_bwd(dxc, pre, xp_ref, w_ref, carry_ref, acc_ref, dp_ref, g, last):
    Q = SSM_CHUNK
    sg = _sigmoid(pre)
    dpre = dxc * (sg * (1.0 + pre * (1.0 - sg)))
    dp_ref[0:Q, :] = dpre
    dp_ref[Q:Q + HALO, :] = carry_ref[g]
    carry_ref[g] = dpre[0:HALO, :]
    rows = [jnp.sum(dpre * xp_ref[pl.ds(HALO - 3 + k, Q), :], axis=0, keepdims=True) for k in range(SSM_CONV)]
    rows.append(jnp.sum(dpre, axis=0, keepdims=True))
    rows.append(jnp.zeros((HALO - SSM_CONV - 1, dpre.shape[1]), F32))
    acc_ref[g] += jnp.concatenate(rows, axis=0)
    dx = w_ref[3:4, :] * dpre
    for k in range(SSM_CONV - 1):
        dx = dx + w_ref[k:k + 1, :] * dp_ref[pl.ds(3 - k, Q), :]
    return dx


def mix_in_bwd(dh, h, g, dgates, dz, dxs, dbm, dcm, ddtT, dq, dk, dv, w_gz, w_xbc, w_dtT, w_qkv, *, name, tm=512,
               ride=None):
    T, D = h.shape
    nt = T // tm
    GN = SSM_GROUPS * SSM_STATE

    def body(dh_ref, h_ref, g_ref, dgates_ref, dz_ref, dxs_ref, dbm_ref, dcm_ref, ddt_ref, dq_ref, dk_ref, dv_ref,
             wgz_ref, wxbc_ref, wdt_ref, wqkv_ref, dhin_ref, dg_ref):
        @pl.when(pl.program_id(0) == 0)
        def _():
            dg_ref[...] = jnp.zeros_like(dg_ref)

        du = _dot_nt(dgates_ref[...], wgz_ref[:, 0:2048])
        du = du + _dot_nt(dz_ref[...], wgz_ref[:, 2048:4096])
        du = du + _dot_nt(dxs_ref[...], wxbc_ref[:, 0:SSM_D_INNER])
        du = du + _dot_nt(dbm_ref[...], wxbc_ref[:, SSM_D_INNER:SSM_D_INNER + GN])
        du = du + _dot_nt(dcm_ref[...], wxbc_ref[:, SSM_D_INNER + GN:])
        du = du + _dot_tn(ddt_ref[...].astype(BF16), wdt_ref[...])
        du = du + _dot_nt(dq_ref[...], wqkv_ref[:, 0:ATTN_Q_DIM])
        du = du + _dot_nt(dk_ref[...], wqkv_ref[:, ATTN_Q_DIM:ATTN_Q_DIM + ATTN_KV_DIM])
        du = du + _dot_nt(dv_ref[...], wqkv_ref[:, ATTN_Q_DIM + ATTN_KV_DIM:])
        hh = h_ref[...]
        r = lax.rsqrt(jnp.mean(hh * hh, axis=-1, keepdims=True) + RMS_EPS)
        dg_ref[...] += jnp.sum(du * hh * r, axis=0, keepdims=True)
        t = du * g_ref[...]
        dhin_ref[...] = dh_ref[...] + r * t - hh * (r * r * r * jnp.mean(t * hh, axis=-1, keepdims=True))

    sds = jax.ShapeDtypeStruct
    return _call(
        body, grid=(nt,),
        in_specs=[_rows(tm, D), _rows(tm, D), _resident((1, D)), _rows(tm, 2048), _rows(tm, 2048), _rows(tm, SSM_D_INNER),
                  _rows(tm, GN), _rows(tm, GN), pl.BlockSpec((SSM_HEADS, tm), lambda i: (0, i)),
                  _rows(tm, ATTN_Q_DIM), _rows(tm, ATTN_KV_DIM),
                  _rows(tm, ATTN_KV_DIM), _resident(w_gz.shape), _resident(w_xbc.shape), _resident(w_dtT.shape),
                  _resident(w_qkv.shape)],
        args=[dh, h, g, dgates, dz, dxs, dbm, dcm, ddtT, dq, dk, dv, w_gz, w_xbc, w_dtT, w_qkv],
        out_specs=[_rows(tm, D), pl.BlockSpec((1, D), lambda i: (0, 0))],
        out_shape=[sds((T, D), F32), sds((1, D), F32)],
        sem=("arbitrary",), name=name, ride=ride)


PAIRS = SSM_HPG // 2
PW = 2 * SSM_HEAD_DIM


def ssd_scalars(dt_rawT, dt_bias, a_log, *, name, chunks=8):
    H, T = dt_rawT.shape
    Q, G, HPG = SSM_CHUNK, SSM_GROUPS, SSM_HPG
    chunks = math.gcd(chunks, T // Q)
    span = Q * chunks

    def body(dtT_ref, dtb_ref, alog_ref, cols_ref, acsT_ref, dec_ref):
        aT = -jnp.exp(alog_ref[...])
        triT = (_iota((Q, Q), 0) <= _iota((Q, Q), 1)).astype(F32)
        for j in range(chunks):
            at = slice(Q * j, Q * (j + 1))
            dtT = _softplus(dtT_ref[:, at] + dtb_ref[...])
            acsT = _dot_hi(dtT * aT, triT)
            lastT = acsT[:, Q - 1:Q]
            acsT_ref[:, at] = acsT
            dec_ref[j] = jnp.exp(lastT)
            parts = [dtT, acsT, jnp.exp(lastT - acsT), jnp.exp(acsT)]
            colsT = jnp.concatenate([q[HPG * g:HPG * (g + 1)] for g in range(G) for q in parts], axis=0).T
            for g in range(G):
                cols_ref[g, at, :] = colsT[:, 4 * HPG * g:4 * HPG * (g + 1)]

    sds = jax.ShapeDtypeStruct
    return pl.pallas_call(
        body, grid=(T // span,),
        in_specs=[pl.BlockSpec((H, span), lambda i: (0, i)), pl.BlockSpec((H, 1), lambda i: (0, 0)),
                  pl.BlockSpec((H, 1), lambda i: (0, 0))],
        out_specs=[pl.BlockSpec((G, span, 4 * HPG), lambda i: (0, i, 0)), pl.BlockSpec((H, span), lambda i: (0, i)),
                   pl.BlockSpec((chunks, H, 1), lambda i: (i, 0, 0))],
        out_shape=[sds((G, T, 4 * HPG), F32), sds((H, T), F32), sds((T // Q, H, 1), F32)],
        compiler_params=_cparams("parallel"), name=name)(dt_rawT, dt_bias, a_log)


CONV_XS, CONV_BM, CONV_CM = slice(0, SSM_GW), slice(SSM_GW, SSM_GW + SSM_STATE), slice(SSM_GW + SSM_STATE, SSM_GW + 2 * SSM_STATE)


def _pair_cols(cols, base, p, lo):
    k = base + 2 * p
    return jnp.where(lo, cols[:, k:k + 1], cols[:, k + 1:k + 2])


def _pair_row(colT, p, lo_row):
    return jnp.where(lo_row, colT[2 * p:2 * p + 1, :], colT[2 * p + 1:2 * p + 2, :])


def _pair_operands(pp, p, s, causal, lo, xb):
    zero = jnp.zeros_like(xb)
    rhs = jnp.concatenate([jnp.where(lo, xb, zero), jnp.where(lo, zero, xb)], axis=0)
    ls, ms = [], []
    for k in (2 * p, 2 * p + 1):
        seg = pp["cols"][:, 8 + k:9 + k] - pp["acsT"][k:k + 1, :]
        l = jnp.exp(jnp.where(causal, seg, -1e30))
        ls.append(l)
        ms.append(s * l)
    lhs = jnp.concatenate([m.astype(BF16) for m in ms], axis=1)
    return lhs, rhs, ls


def _group_cols(g):
    return (slice(SSM_GW * g, SSM_GW * (g + 1)),
            slice(SSM_D_INNER + SSM_STATE * g, SSM_D_INNER + SSM_STATE * (g + 1)),
            slice(SSM_D_INNER + SSM_STATE * (SSM_GROUPS + g), SSM_D_INNER + SSM_STATE * (SSM_GROUPS + g + 1)))


def _conv_pre(x, halo, w, b, xp_ref):
    Q = SSM_CHUNK
    xp_ref[0:HALO, :] = halo
    xp_ref[HALO:HALO + Q, :] = x
    pre = b + w[3:4, :] * x
    for k in range(SSM_CONV - 1):
        pre = pre + w[k:k + 1, :] * xp_ref[pl.ds(HALO - 3 + k, Q), :]
    return pre


def _ssd_prologue(g, first, xbc_ref, halo_ref, conv_ref, cols_ref, acsT_ref, dec_ref, xp_xs, xp_bm, xp_cm):
    cp = conv_ref[g]
    heads = slice(SSM_HPG * g, SSM_HPG * (g + 1))
    out, taps = {}, {}
    for n, at, pk, xp in zip(("xs", "bm", "cm"), _group_cols(g), (CONV_XS, CONV_BM, CONV_CM), (xp_xs, xp_bm, xp_cm)):
        taps[n] = (cp[:, pk], cp[SSM_CONV:SSM_CONV + 1, pk])
        halo = jnp.where(first, 0.0, halo_ref[:, at])
        pre = _conv_pre(xbc_ref[:, at], halo, *taps[n], xp.at[g])
        out["pre_" + n] = pre
        out[n] = pre * _sigmoid(pre)
    out.update(taps=taps, acsT=acsT_ref[heads, :], decayT=dec_ref[0, heads, :], cols=cols_ref[g])
    return out


def ssd_fwd(xbc, z, scalars, conv_pack, vec_pack, *, batch, name):
    T = xbc.shape[0]
    Q, GW, N, G = SSM_CHUNK, SSM_GW, SSM_STATE, SSM_GROUPS
    nc = T // batch // Q
    row = lambda b, c: b * nc + c
    hrow = lambda b, c: jnp.maximum(row(b, c) * (Q // HALO) - 1, 0)

    def body(xbc_ref, halo_ref, z_ref, conv_ref, cols_ref, acsT_ref, dec_ref, vec_ref, y_ref, ys_ref, st_ref,
             state, xp_xs, xp_bm, xp_cm):
        first = pl.program_id(1) == 0
        causal = _iota((Q, Q), 0) >= _iota((Q, Q), 1)
        lo = _iota((Q, PW), 1) < SSM_HEAD_DIM
        lo_row = _iota((1, PW), 1) < SSM_HEAD_DIM

        @pl.when(first)
        def _():
            state[...] = jnp.zeros_like(state)

        for g in range(G):
            cols = slice(GW * g, GW * (g + 1))
            pp = _ssd_prologue(g, first, xbc_ref, halo_ref, conv_ref, cols_ref, acsT_ref, dec_ref, xp_xs, xp_bm, xp_cm)
            xs = pp["xs"]
            bb = pp["bm"].astype(BF16)
            cb = pp["cm"].astype(BF16)
            s = _dot_nt(cb, bb)
            entering = state[g]
            st_ref[0, g] = entering
            wide = lambda base: jnp.concatenate([_pair_cols(pp["cols"], base, p, lo) for p in range(PAIRS)], axis=1)
            x = xs * wide(0)
            xb = x.astype(BF16)
            yd = []
            for p in range(PAIRS):
                lhs, rhs, _ = _pair_operands(pp, p, s, causal, lo, xb[:, PW * p:PW * (p + 1)])
                yd.append(_dot(lhs, rhs))
            y = jnp.concatenate(yd, axis=1) + _dot(cb, entering.astype(BF16)) * wide(24) + vec_ref[0:1, cols] * xs
            decay = jnp.concatenate([_pair_row(pp["decayT"], p, lo_row) for p in range(PAIRS)], axis=1)
            state[g] = entering * decay + _dot_tn(bb, (x * wide(16)).astype(BF16))
            y_ref[:, cols] = y
            zz = z_ref[:, cols]
            yg = y * (zz * _sigmoid(zz))
            rr = lax.rsqrt(jnp.mean(yg * yg, axis=-1, keepdims=True) + RMS_EPS)
            ys_ref[:, cols] = (yg * rr * vec_ref[1:2, cols]).astype(BF16)

    sds = jax.ShapeDtypeStruct
    cols3, acsT, decay = scalars
    return pl.pallas_call(
        body, grid=(batch, nc),
        in_specs=[pl.BlockSpec((Q, SSM_CONV_DIM), lambda b, c: (row(b, c), 0)),
                  pl.BlockSpec((HALO, SSM_CONV_DIM), lambda b, c: (hrow(b, c), 0)),
                  pl.BlockSpec((Q, SSM_D_INNER), lambda b, c: (row(b, c), 0)),
                  pl.BlockSpec((G, HALO, GW + 2 * N), lambda b, c: (0, 0, 0)),
                  pl.BlockSpec((G, Q, 4 * SSM_HPG), lambda b, c: (0, row(b, c), 0)),
                  pl.BlockSpec((SSM_HEADS, Q), lambda b, c: (0, row(b, c))),
                  pl.BlockSpec((1, SSM_HEADS, 1), lambda b, c: (row(b, c), 0, 0)),
                  pl.BlockSpec((2, SSM_D_INNER), lambda b, c: (0, 0))],
        out_specs=[pl.BlockSpec((Q, SSM_D_INNER), lambda b, c: (row(b, c), 0)),
                   pl.BlockSpec((Q, SSM_D_INNER), lambda b, c: (row(b, c), 0)),
                   pl.BlockSpec((1, G, N, GW), lambda b, c: (row(b, c), 0, 0, 0))],
        out_shape=[sds((T, SSM_D_INNER), F32), sds((T, SSM_D_INNER), BF16), sds((T // Q, G, N, GW), F32)],
        scratch_shapes=[pltpu.VMEM((G, N, GW), F32), pltpu.VMEM((G, HALO + Q, GW), F32),
                        pltpu.VMEM((G, HALO + Q, N), F32), pltpu.VMEM((G, HALO + Q, N), F32)],
        compiler_params=_cparams("arbitrary", "arbitrary"), name=name,
    )(xbc, xbc, z, conv_pack, cols3, acsT, decay, vec_pack)


def ssd_bwd(dys, y, xbc, z, dt_rawT, states, scalars, conv_pack, col_pack, vec_pack, *, batch, name, ride=None):
    T = xbc.shape[0]
    Q, GW, N, G, HPG = SSM_CHUNK, SSM_GW, SSM_STATE, SSM_GROUPS, SSM_HPG
    nc = T // batch // Q
    row = lambda b, c: b * nc + (nc - 1 - c)
    hrow = lambda b, c: jnp.maximum(row(b, c) * (Q // HALO) - 1, 0)

    def body(xbc_ref, halo_ref, z_ref, y_ref, dys_ref, dtT_ref, st_ref, conv_ref, cols_ref, acsT_ref, dec_ref, col_ref,
             vec_ref, dz_ref, dxs_ref, dbm_ref, dcm_ref, ddtT_ref, acc_xs, acc_bm, acc_cm, acc_head,
             dstate, xp_xs, xp_bm, xp_cm, dp_xs, dp_bm, dp_cm, cy_xs, cy_bm, cy_cm):
        b = pl.program_id(0)
        cr = pl.program_id(1)
        first = cr == nc - 1
        last = cr == 0

        @pl.when(jnp.logical_and(b == 0, cr == 0))
        def _():
            acc_xs[...] = jnp.zeros_like(acc_xs)
            acc_bm[...] = jnp.zeros_like(acc_bm)
            acc_cm[...] = jnp.zeros_like(acc_cm)
            acc_head[...] = jnp.zeros_like(acc_head)

        @pl.when(last)
        def _():
            dstate[...] = jnp.zeros_like(dstate)
            cy_xs[...] = jnp.zeros_like(cy_xs)
            cy_bm[...] = jnp.zeros_like(cy_bm)
            cy_cm[...] = jnp.zeros_like(cy_cm)

        causal = _iota((Q, Q), 0) >= _iota((Q, Q), 1)
        lo = _iota((Q, PW), 1) < SSM_HEAD_DIM
        lo_row = _iota((1, PW), 1) < SSM_HEAD_DIM
        tri = (_iota((Q, Q), 0) >= _iota((Q, Q), 1)).astype(F32)
        e8 = (_iota((HPG, GW), 0) == lax.shift_right_logical(_iota((HPG, GW), 1), 6)).astype(F32)
        seg_sum = lambda v: lax.dot_general(e8, v, (((1,), (1,)), ((), ())), preferred_element_type=F32,
                                            precision=lax.Precision.HIGHEST)
        lane = _iota((HPG, N), 1)
        for g in range(G):
            cols = slice(GW * g, GW * (g + 1))
            ncols = slice(N * g, N * (g + 1))
            heads = slice(HPG * g, HPG * (g + 1))
            pp = _ssd_prologue(g, first, xbc_ref, halo_ref, conv_ref, cols_ref, acsT_ref, dec_ref, xp_xs, xp_bm, xp_cm)
            xs, decayT = pp["xs"], pp["decayT"]
            dtrT = dtT_ref[heads, :] + col_ref[heads, 0:1]
            dtT = _softplus(dtrT)
            aT = -jnp.exp(col_ref[heads, 1:2])

            yv = y_ref[:, cols]
            zz = z_ref[:, cols]
            sz = _sigmoid(zz)
            silu_z = zz * sz
            yg = yv * silu_z
            rr = lax.rsqrt(jnp.mean(yg * yg, axis=-1, keepdims=True) + RMS_EPS)
            dys_v = dys_ref[:, cols]
            d_ng = jnp.sum(dys_v * yg * rr, axis=0, keepdims=True)
            t = dys_v * vec_ref[1:2, cols]
            dyg = rr * t - yg * (rr * rr * rr * jnp.mean(t * yg, axis=-1, keepdims=True))
            dy = dyg * silu_z
            dz_ref[:, cols] = (dyg * yv * (sz * (1.0 + zz * (1.0 - sz)))).astype(BF16)
            dsk = vec_ref[0:1, cols]
            d_dsk = jnp.sum(dy * xs, axis=0, keepdims=True)

            bb = pp["bm"].astype(BF16)
            cb = pp["cm"].astype(BF16)
            s = _dot_nt(cb, bb)
            wide = lambda base: jnp.concatenate([_pair_cols(pp["cols"], base, p, lo) for p in range(PAIRS)], axis=1)
            dt_x, w_x, e_x = wide(0), wide(16), wide(24)
            decay = jnp.concatenate([_pair_row(decayT, p, lo_row) for p in range(PAIRS)], axis=1)
            x = xs * dt_x
            xw = x * w_x
            xb = x.astype(BF16)
            dyb = dy.astype(BF16)
            dye = (dy * e_x).astype(BF16)
            hp = st_ref[0, g]
            hpb = hp.astype(BF16)
            dh = dstate[g]
            dhb = dh.astype(BF16)
            ds_acc = jnp.zeros((Q, Q), F32)
            yd, dxd = [], []
            for p in range(PAIRS):
                tile = slice(PW * p, PW * (p + 1))
                lhs, rhs, ls = _pair_operands(pp, p, s, causal, lo, xb[:, tile])
                dm = _dot_nt(dyb[:, tile], rhs)
                dxd2 = _dot_tn(lhs, dyb[:, tile])
                dxd.append(jnp.where(lo, dxd2[0:Q], dxd2[Q:2 * Q]))
                ds_acc = ds_acc + dm[:, 0:Q] * ls[0] + dm[:, Q:2 * Q] * ls[1]
                yd.append(_dot(lhs, rhs))
            yd = jnp.concatenate(yd, axis=1)
            dxd = jnp.concatenate(dxd, axis=1)
            dxw = _dot(bb, dhb)
            dx_full = dxd + dxw * w_x
            tw = dxw * xw
            q1 = dyb.astype(F32) * yd + dy * (_dot(cb, hpb) * e_x) - tw - xb.astype(F32) * dxd
            q2 = dx_full * xs
            dxs_v = dsk * dy + dx_full * dt_x
            dstate[g] = dh * decay + _dot_tn(cb, dye)
            dsb = ds_acc.astype(BF16)
            d_c = _dot_nt(dye, hpb) + _dot(dsb, bb)
            d_b = _dot_nt(xw.astype(BF16), dhb) + _dot_tn(dsb, cb)
            rowv = jnp.sum(dh * hp, axis=0, keepdims=True) * decay + jnp.sum(tw, axis=0, keepdims=True)
            last_terms = jnp.sum(e8 * rowv, axis=1, keepdims=True)
            dacsT = seg_sum(q1) + jnp.where(_iota((1, Q), 1) == Q - 1, 1.0, 0.0) * last_terms
            d_dtaT = _dot_hi(dacsT, tri)
            ddtT = d_dtaT * aT + seg_sum(q2)
            d_alog = jnp.sum(d_dtaT * dtT, axis=1, keepdims=True) * aT
            ddt_rawT = ddtT * _sigmoid(dtrT)
            ddtT_ref[heads, :] = ddt_rawT
            d_dtb = jnp.sum(ddt_rawT, axis=1, keepdims=True)
            acc_head[g] += jnp.where(lane == 0, d_dtb, 0.0) + jnp.where(lane == 1, d_alog, 0.0)
            taps = pp["taps"]
            dxs_ref[:, cols] = _conv_bwd(dxs_v, pp["pre_xs"], xp_xs.at[g], taps["xs"][0], cy_xs, acc_xs, dp_xs.at[g], g,
                                         last).astype(BF16)
            dbm_ref[:, ncols] = _conv_bwd(d_b, pp["pre_bm"], xp_bm.at[g], taps["bm"][0], cy_bm, acc_bm, dp_bm.at[g], g,
                                          last).astype(BF16)
            dcm_ref[:, ncols] = _conv_bwd(d_c, pp["pre_cm"], xp_cm.at[g], taps["cm"][0], cy_cm, acc_cm, dp_cm.at[g], g,
                                          last).astype(BF16)
            acc_xs[g, pl.ds(SSM_CONV + 1, 2), :] += jnp.concatenate([d_ng, d_dsk], axis=0)

    sds = jax.ShapeDtypeStruct
    cols3, acsT, decay3 = scalars
    full = lambda shape: pl.BlockSpec(shape, lambda b, c: (0,) * len(shape))
    wide_in = pl.BlockSpec((Q, SSM_D_INNER), lambda b, c: (row(b, c), 0))
    heads_in = pl.BlockSpec((SSM_HEADS, Q), lambda b, c: (0, row(b, c)))
    return _call(
        body, grid=(batch, nc),
        in_specs=[pl.BlockSpec((Q, SSM_CONV_DIM), lambda b, c: (row(b, c), 0)),
                  pl.BlockSpec((HALO, SSM_CONV_DIM), lambda b, c: (hrow(b, c), 0)),
                  wide_in, wide_in, wide_in, heads_in,
                  pl.BlockSpec((1, G, N, GW), lambda b, c: (row(b, c), 0, 0, 0)),
                  full((G, HALO, GW + 2 * N)),
                  pl.BlockSpec((G, Q, 4 * HPG), lambda b, c: (0, row(b, c), 0)),
                  heads_in,
                  pl.BlockSpec((1, SSM_HEADS, 1), lambda b, c: (row(b, c), 0, 0)),
                  full((SSM_HEADS, 2)), full((2, SSM_D_INNER))],
        args=[xbc, xbc, z, y, dys, dt_rawT, states, conv_pack, cols3, acsT, decay3, col_pack, vec_pack],
        out_specs=[wide_in, wide_in,
                   pl.BlockSpec((Q, G * N), lambda b, c: (row(b, c), 0)),
                   pl.BlockSpec((Q, G * N), lambda b, c: (row(b, c), 0)),
                   heads_in, full((G, HALO, GW)), full((G, HALO, N)), full((G, HALO, N)), full((G, HPG, N))],
        out_shape=[sds((T, SSM_D_INNER), BF16), sds((T, SSM_D_INNER), BF16), sds((T, G * N), BF16),
                   sds((T, G * N), BF16), sds((SSM_HEADS, T), F32),
                   sds((G, HALO, GW), F32), sds((G, HALO, N), F32), sds((G, HALO, N), F32), sds((G, HPG, N), F32)],
        scratch=[pltpu.VMEM((G, N, GW), F32),
                 pltpu.VMEM((G, HALO + Q, GW), F32), pltpu.VMEM((G, HALO + Q, N), F32), pltpu.VMEM((G, HALO + Q, N), F32),
                 pltpu.VMEM((G, Q + HALO, GW), F32), pltpu.VMEM((G, Q + HALO, N), F32), pltpu.VMEM((G, Q + HALO, N), F32),
                 pltpu.VMEM((G, HALO, GW), F32), pltpu.VMEM((G, HALO, N), F32), pltpu.VMEM((G, HALO, N), F32)],
        sem=("arbitrary", "arbitrary"), name=name, ride=ride)


def mm_rows(a, b, *, name, tt=2048):
    M, T = a.shape
    N = b.shape[1]
    tt = min(tt, T)

    def body(a_ref, b_ref, o_ref):
        @pl.when(pl.program_id(0) == 0)
        def _():
            o_ref[...] = jnp.zeros_like(o_ref)

        o_ref[...] += _dot(a_ref[...].astype(BF16), b_ref[...])

    return pl.pallas_call(
        body, grid=(T // tt,),
        in_specs=[pl.BlockSpec((M, tt), lambda t: (0, t)), pl.BlockSpec((tt, N), lambda t: (t, 0))],
        out_specs=pl.BlockSpec((M, N), lambda t: (0, 0)), out_shape=jax.ShapeDtypeStruct((M, N), F32),
        compiler_params=_cparams("arbitrary"), name=name)(a, b)


MESH = pl.DeviceIdType.MESH
ANY = pl.BlockSpec(memory_space=pl.ANY)
ROW_ALIGN = 16


def _me():
    return lax.axis_index("x"), lax.axis_index("y"), lax.axis_index("c")


def _other_chips(x, y):
    return [(1 - x, y), (x, 1 - y), (1 - x, 1 - y)]


def _remote(src, dst, send_sem, recv_sem, to):
    return pltpu.make_async_remote_copy(src_ref=src, dst_ref=dst, send_sem=send_sem, recv_sem=recv_sem,
                                        device_id=to, device_id_type=MESH)


def _half(c, rows):
    return pl.ds(pl.multiple_of(c * (rows // 2), ROW_ALIGN), rows // 2)


def ag_ride(bufs):
    n = len(bufs)

    def copies(outs, sems):
        ici_send, ici_recv, d2d_send, d2d_recv = sems
        x, y, c = _me()
        sib = (x, y, 1 - c)
        ici, d2d, d2d_in = [], [], []
        for i in range(n):
            rows = outs[i].shape[1]
            mine = outs[i].at[2 * x + y, _half(c, rows)]
            for j, chip in enumerate(_other_chips(x, y)):
                ici.append(_remote(mine, mine, ici_send.at[i, j], ici_recv.at[i, j], (*chip, c)))
                landed = outs[i].at[2 * chip[0] + chip[1], _half(c, rows)]
                d2d.append((_remote(landed, landed, ici_send.at[i, j], ici_recv.at[i, j], (*chip, c)),
                            _remote(landed, landed, d2d_send.at[i, j], d2d_recv.at[i, j], sib)))
                lands = outs[i].at[2 * chip[0] + chip[1], _half(1 - c, rows)]
                d2d_in.append(_remote(lands, lands, d2d_send.at[i, j], d2d_recv.at[i, j], sib))
        return ici, d2d, d2d_in

    def start(ins, outs, sems):
        for cp in copies(outs, sems)[0]:
            cp.start()

    def finish(ins, outs, sems):
        ici, d2d, d2d_in = copies(outs, sems)
        for arrived, forward in d2d:
            arrived.wait_recv()
            forward.start()
        for cp in d2d_in:
            cp.wait_recv()
        for cp in ici + [forward for _, forward in d2d]:
            cp.wait_send()

    return Ride(bufs, [jax.ShapeDtypeStruct(b.shape, b.dtype) for b in bufs], [(i, i) for i in range(n)],
                [pltpu.SemaphoreType.DMA((n, 3))] * 4, start, finish)


def pair_ride(grads):
    n = len(grads)

    def copies(ins, outs, sems):
        x, y, c = _me()
        return [_remote(ins[i].at[:, _half(1 - c, ins[i].shape[1]), :], outs[i], sems[0].at[i], sems[1].at[i], (x, y, 1 - c))
                for i in range(n)]

    def start(ins, outs, sems):
        for cp in copies(ins, outs, sems):
            cp.start()

    def finish(ins, outs, sems):
        for cp in copies(ins, outs, sems):
            cp.wait()

    return Ride(grads, [jax.ShapeDtypeStruct((N_SHARD, g.shape[1] // 2, g.shape[2]), g.dtype) for g in grads], [],
                [pltpu.SemaphoreType.DMA((n,))] * 2, start, finish)


def rs_add(grad, part, c, *, rt, name):
    _, rows, cols = grad.shape
    r2 = rows // 2
    nrb = r2 // rt

    def body(c_ref, g_ref, p_ref, o_ref):
        o_ref[...] = (g_ref[...] + p_ref[...]).astype(BF16)

    return pl.pallas_call(
        body,
        grid_spec=pltpu.PrefetchScalarGridSpec(
            num_scalar_prefetch=1, grid=(N_SHARD, nrb),
            in_specs=[pl.BlockSpec((1, rt, cols), lambda k, i, c_ref: (k, c_ref[1] * nrb + i, 0)),
                      pl.BlockSpec((1, rt, cols), lambda k, i, c_ref: (k, i, 0))],
            out_specs=pl.BlockSpec((1, rt, cols), lambda k, i, c_ref: (k, i, 0))),
        out_shape=jax.ShapeDtypeStruct((N_SHARD, r2, cols), BF16),
        compiler_params=_cparams("parallel", "parallel"), name=name)(c, grad, part)


def chips_ride(sums):
    n = len(sums)

    def copies(ins, outs, sems):
        send, recv = sems
        x, y, c = _me()
        return [_remote(ins[i].at[2 * chip[0] + chip[1]], outs[i].at[2 * x + y], send.at[i, j], recv.at[i, j], (*chip, c))
                for i in range(n) for j, chip in enumerate(_other_chips(x, y))]

    def start(ins, outs, sems):
        for cp in copies(ins, outs, sems):
            cp.start()

    def finish(ins, outs, sems):
        for cp in copies(ins, outs, sems):
            cp.wait()

    return Ride(sums, [jax.ShapeDtypeStruct(s.shape, s.dtype) for s in sums], [],
                [pltpu.SemaphoreType.DMA((n, 3))] * 2, start, finish)


def rs_total(parts, own, where, *, rt, name):
    _, r2, cols = parts.shape
    nrb = r2 // rt

    def body(w_ref, p0, p1, p2, p3, own_ref, o_ref):
        s_me = w_ref[0]
        acc = None
        for k, p in enumerate((p0, p1, p2, p3)):
            term = jnp.where(s_me == k, own_ref[0], p[0]).astype(F32)
            acc = term if acc is None else acc + term
        o_ref[...] = acc

    def slot(k):
        return pl.BlockSpec((1, rt, cols), lambda i, w: (jnp.where(w[0] == k, (k + 1) % N_SHARD, k), i, 0))

    return pl.pallas_call(
        body,
        grid_spec=pltpu.PrefetchScalarGridSpec(
            num_scalar_prefetch=1, grid=(nrb,),
            in_specs=[slot(0), slot(1), slot(2), slot(3), pl.BlockSpec((1, rt, cols), lambda i, w: (w[0], i, 0))],
            out_specs=pl.BlockSpec((rt, cols), lambda i, w: (w[1] * nrb + i, 0))),
        out_shape=jax.ShapeDtypeStruct((2 * r2, cols), F32),
        compiler_params=_cparams("parallel"), name=name)(where, parts, parts, parts, parts, own)


def share_ride(totals):
    n = len(totals)

    def halves(outs, sems):
        x, y, c = _me()
        mine = [outs[i].at[_half(c, outs[i].shape[0])] for i in range(n)]
        other = [outs[i].at[_half(1 - c, outs[i].shape[0])] for i in range(n)]
        return ([_remote(m, m, sems[0].at[i], sems[1].at[i], (x, y, 1 - c)) for i, m in enumerate(mine)],
                [_remote(o, o, sems[0].at[i], sems[1].at[i], (x, y, 1 - c)) for i, o in enumerate(other)])

    def start(ins, outs, sems):
        for cp in halves(outs, sems)[0]:
            cp.start()

    def finish(ins, outs, sems):
        sent, landing = halves(outs, sems)
        for cp in landing:
            cp.wait_recv()
        for cp in sent:
            cp.wait_send()

    return Ride(totals, [jax.ShapeDtypeStruct(t.shape, t.dtype) for t in totals], [(i, i) for i in range(n)],
                [pltpu.SemaphoreType.DMA((n,))] * 2, start, finish)


def small_allreduce(buf, *, name):
    rows = buf.shape[0]

    def body(x_ref, o_ref, slots, send, recv):
        x, y, c = _me()
        me = 4 * x + 2 * y + c
        slots[me] = x_ref[...]
        sent = []
        for d in range(1, 8):
            peer = (1 - x if d & 4 else x, 1 - y if d & 2 else y, 1 - c if d & 1 else c)
            sent.append(_remote(x_ref, slots.at[me], send.at[d - 1], recv.at[d - 1], peer))
            sent[-1].start()
        for cp in sent:
            cp.wait()
        acc = slots[0]
        for k in range(1, 8):
            acc = acc + slots[k]
        o_ref[...] = acc

    return pl.pallas_call(
        body, out_shape=jax.ShapeDtypeStruct(buf.shape, F32),
        in_specs=[pl.BlockSpec(memory_space=pltpu.VMEM)], out_specs=pl.BlockSpec(memory_space=pltpu.VMEM),
        scratch_shapes=[pltpu.VMEM((8, rows, 128), F32), pltpu.SemaphoreType.DMA((7,)), pltpu.SemaphoreType.DMA((7,))],
        name=name)(buf)


def adamw(w, g, m, v, *, name, rt=None):
    rows, cols = w.shape
    rt = rows if rt is None else rt
    c1 = 1.0 - ADAM_B1 ** ADAM_STEP
    c2 = 1.0 - ADAM_B2 ** ADAM_STEP

    def body(w_ref, g_ref, m_ref, v_ref, d_ref, nm_ref, nv_ref):
        gg = g_ref[...]
        nm = ADAM_B1 * m_ref[...] + (1.0 - ADAM_B1) * gg
        nv = ADAM_B2 * v_ref[...] + (1.0 - ADAM_B2) * (gg * gg)
        nm_ref[...] = nm
        nv_ref[...] = nv
        d_ref[...] = -ADAM_LR * ((nm / c1) / (jnp.sqrt(nv / c2) + ADAM_EPS) + ADAM_WD * w_ref[...])

    spec = pl.BlockSpec((rt, cols), lambda i: (i, 0))
    return pl.pallas_call(
        body, grid=(rows // rt,), in_specs=[spec] * 4, out_specs=[spec] * 3,
        out_shape=[jax.ShapeDtypeStruct((rows, cols), F32)] * 3,
        compiler_params=_cparams("parallel"), name=name)(w, g, m, v)


WEIGHTS = ['ffn1_pre_g', 'ffn1_w_gate', 'ffn1_w_up', 'ffn1_w_down', 'ffn1_post_g', 'mix_pre_g', 'w_in', 'conv_w',
           'conv_b', 'dt_bias', 'a_log', 'd_skip', 'ssm_norm_g', 'w_ssm_proj', 'attn_sinks', 'rel_bias_table',
           'w_attn_proj', 'w_out', 'mix_post_g', 'ffn2_pre_g', 'ffn2_w_gate', 'ffn2_w_up', 'ffn2_w_down', 'ffn2_post_g']
BIG = ['ffn1_w_gate', 'ffn1_w_up', 'ffn1_w_down', 'w_in', 'w_ssm_proj', 'w_attn_proj', 'w_out',
       'ffn2_w_gate', 'ffn2_w_up', 'ffn2_w_down']
SMALL = [w for w in WEIGHTS if w not in BIG]


def _bucket_onehot():
    blk = ATTN_BLOCK
    dist = np.maximum(np.arange(blk)[:, None] + blk - np.arange(2 * blk)[None, :], 0)
    max_exact = REL_BUCKETS // 2
    d = np.maximum(dist, 1).astype(np.float32)
    large = max_exact + (np.log(d / np.float32(max_exact)) / np.float32(math.log(REL_MAX_DISTANCE / max_exact))
                         * np.float32(REL_BUCKETS - max_exact)).astype(np.int32)
    bucket = np.where(dist < max_exact, dist, np.minimum(large, REL_BUCKETS - 1)).reshape(-1)
    return jnp.asarray((bucket[None, :] == np.arange(REL_BUCKETS)[:, None]).astype(np.float32))


def _pack_rows(parts, mult=8):
    flat = jnp.concatenate([p.reshape(-1).astype(F32) for p in parts])
    rows = -(-flat.shape[0] // (128 * mult)) * mult
    return jnp.pad(flat, (0, rows * 128 - flat.shape[0])).reshape(rows, 128)


def _unpack_rows(buf, shapes):
    flat = buf.reshape(-1)
    out, at = [], 0
    for shp in shapes:
        size = int(np.prod(shp))
        out.append(flat[at:at + size].reshape(shp))
        at += size
    return out


def kernel(x, ffn1_pre_g, ffn1_w_gate, ffn1_w_up, ffn1_w_down, ffn1_post_g, mix_pre_g, w_in, conv_w, conv_b, dt_bias, a_log, d_skip, ssm_norm_g, w_ssm_proj, attn_sinks, rel_bias_table, w_attn_proj, w_out, mix_post_g, ffn2_pre_g, ffn2_w_gate, ffn2_w_up, ffn2_w_down, ffn2_post_g, loss_target, m_ffn1_pre_g, m_ffn1_w_gate, m_ffn1_w_up, m_ffn1_w_down, m_ffn1_post_g, m_mix_pre_g, m_w_in, m_conv_w, m_conv_b, m_dt_bias, m_a_log, m_d_skip, m_ssm_norm_g, m_w_ssm_proj, m_attn_sinks, m_rel_bias_table, m_w_attn_proj, m_w_out, m_mix_post_g, m_ffn2_pre_g, m_ffn2_w_gate, m_ffn2_w_up, m_ffn2_w_down, m_ffn2_post_g, v_ffn1_pre_g, v_ffn1_w_gate, v_ffn1_w_up, v_ffn1_w_down, v_ffn1_post_g, v_mix_pre_g, v_w_in, v_conv_w, v_conv_b, v_dt_bias, v_a_log, v_d_skip, v_ssm_norm_g, v_w_ssm_proj, v_attn_sinks, v_rel_bias_table, v_w_attn_proj, v_w_out, v_mix_post_g, v_ffn2_pre_g, v_ffn2_w_gate, v_ffn2_w_up, v_ffn2_w_down, v_ffn2_post_g):
    args = locals()
    w = {n: args[n] for n in WEIGHTS}
    m = {n: args["m_" + n] for n in WEIGHTS}
    v = {n: args["v_" + n] for n in WEIGHTS}
    batch, seq, D = x.shape
    T = batch * seq
    xi, yi, ci = _me()
    s_me = 2 * xi + yi
    x2 = x.reshape(T, D)
    tgt = loss_target.reshape(T, D)

    def own_slot(parts):
        p = jnp.concatenate([t[0] for t in parts], axis=0).astype(BF16)
        return lax.dynamic_update_slice(lax.empty((N_SHARD,) + p.shape, BF16), p[None], (s_me, 0, 0))

    tr = lambda a: jnp.swapaxes(a, -1, -2)
    (wffn1,) = run_ride(ag_ride([own_slot([tr(ffn1_w_gate), tr(ffn1_w_up), ffn1_w_down])]), name="ag_ffn1")
    col = lambda v: v.reshape(SSM_HEADS, 1)
    d_skip_x = jnp.repeat(d_skip, SSM_HEAD_DIM, axis=1)
    cw_slot = lax.dynamic_update_slice(jnp.zeros((SSM_CONV, SSM_CONV_DIM), F32),
                                       conv_w[0] * (ci == 0).astype(F32), (0, s_me * (SSM_CONV_DIM // N_SHARD)))
    conv_w_full = small_allreduce(cw_slot.reshape(-1, 128), name="ag_conv_w").reshape(SSM_CONV, SSM_CONV_DIM)
    cwb = jnp.concatenate([conv_w_full, conv_b, jnp.zeros((HALO - SSM_CONV - 1, SSM_CONV_DIM), F32)], axis=0)
    conv_pack = jnp.stack([jnp.concatenate([cwb[:, SSM_GW * g:SSM_GW * (g + 1)],
                                            cwb[:, SSM_D_INNER + SSM_STATE * g:SSM_D_INNER + SSM_STATE * (g + 1)],
                                            cwb[:, SSM_D_INNER + SSM_STATE * (SSM_GROUPS + g):
                                                SSM_D_INNER + SSM_STATE * (SSM_GROUPS + g + 1)]], axis=1)
                           for g in range(SSM_GROUPS)])
    vec_pack = jnp.concatenate([d_skip_x, ssm_norm_g], axis=0)
    col_pack = jnp.concatenate([col(dt_bias), col(a_log)], axis=1)

    (h1, n1, gate1, up1, f1), (gin, gmix) = ffn_fwd(
        x2, ffn1_pre_g, wffn1, ffn1_post_g, name="ffn1_fwd",
        ride=ag_ride([own_slot([w_in]), own_slot([w_ssm_proj, w_attn_proj, w_out])]))
    w_in_full = gin.transpose(1, 0, 2).reshape(D, IN_COLS)
    w_gz = w_in_full[:, 0:4096]
    w_xbc = w_in_full[:, 4096:4096 + SSM_CONV_DIM]
    w_dtT = w_in_full[:, 7168:7200].T
    w_qkv = w_in_full[:, 7200:]
    (u, gates, z, xbc, dt_rawT, q, k, vv), (wffn2,) = mix_in_fwd(
        h1, mix_pre_g, w_gz, w_xbc, w_dtT, w_qkv, name="mix_in_fwd",
        ride=ag_ride([own_slot([tr(ffn2_w_gate), tr(ffn2_w_up), ffn2_w_down])]))
    scalars = ssd_scalars(dt_rawT, col(dt_bias), col(a_log), name="ssd_scalars")
    y, ys, states = ssd_fwd(xbc, z, scalars, conv_pack, vec_pack, batch=batch, name="ssd_fwd")
    onehot = _bucket_onehot()
    bias = attn_bias(rel_bias_table.T, onehot, name="attn_bias").reshape(ATTN_Q_HEADS, ATTN_BLOCK, 2 * ATTN_BLOCK)
    bias = jnp.where(attn_window()[None], bias, MASKED)
    o, lse = attn_fwd(q, k, vv, bias, attn_sinks, batch=batch, name="attn_fwd")
    h2, y_ssm, y_attn, mix, merged = mix_out_fwd(ys, o, gates, h1, gmix, mix_post_g, name="mix_out_fwd")
    h3, n3, gate2, up2, f2, dy, loss_parts = ffn_fwd(h2, ffn2_pre_g, wffn2, ffn2_post_g, tgt, name="ffn2_fwd")

    where = jnp.stack([s_me, ci]).astype(jnp.int32)

    def chip_sums(grads, pair, tiles, tag):
        return [rs_add(g, p, where, rt=rt, name=f"rs_add_{tag}{i}") for i, (g, p, rt) in enumerate(zip(grads, pair, tiles))]

    def totals(parts, sums, tiles, tag):
        return [rs_total(p, s, where, rt=rt, name=f"rs_total_{tag}{i}")
                for i, (p, s, rt) in enumerate(zip(parts, sums, tiles))]

    def ffn_grads(n, dgate, dup, a, df, tag):
        d = mm_tn(dgate, n[None], into=(lax.empty(wffn1.shape, F32), 0), name="dw_gate" + tag)
        d = mm_tn(dup, n[None], into=(d, 1), name="dw_up" + tag)
        return [mm_tn(a, df[None], into=(d, 2), name="dw_down" + tag)]

    ffn_tiles, mix_tiles = [352], [256, 256]
    dh2, df2, a2, dgate2, dup2, dg_ffn2_pre, dg_ffn2_post = ffn_bwd(dy, h2, f2, gate2, up2, ffn2_pre_g, ffn2_post_g,
                                                                    wffn2, name="ffn2_bwd")
    d_f2 = ffn_grads(n3, dgate2, dup2, a2, df2, "2")
    (dmix, dyssm, dyattn, dgates, dys, do, dg_mix_post), pair_f2 = mix_out_bwd(
        dh2, mix, y_ssm, y_attn, gates, gmix, mix_post_g, name="mix_out_bwd", ride=pair_ride(d_f2))
    sums_f2 = chip_sums(d_f2, pair_f2, ffn_tiles, "f2")
    dq, dk, dv, dbias, dsinks = attn_bwd(q, k, vv, o, do, lse, bias, attn_sinks, batch=batch, name="attn_bwd")
    dtable = attn_bias_bwd(dbias.reshape(ATTN_Q_HEADS, -1), onehot, name="attn_bias_bwd").T
    (dz, dxs, dbm, dcm, ddtT, acc_xs, acc_bm, acc_cm, acc_head), parts_f2 = ssd_bwd(
        dys, y, xbc, z, dt_rawT, states, scalars, conv_pack, col_pack, vec_pack,
        batch=batch, name="ssd_bwd", ride=chips_ride(sums_f2))
    tot_f2 = totals(parts_f2, sums_f2, ffn_tiles, "f2")
    dmx = mm_tn(ys[None], dyssm[None], a_cols=(N_SHARD, 512), into=(lax.empty(gmix.shape, F32), 0), name="dw_ssm")
    dmx = mm_tn(o[None], dyattn[None], a_cols=(N_SHARD, 256), into=(dmx, 2), name="dw_attn")
    dmx = mm_tn(merged[None], dmix[None], a_cols=(N_SHARD, 256), into=(dmx, 3), name="dw_out")
    ub = u[None]
    din = jnp.concatenate([
        mm_tn(ub, dgates[None], name="dw_in_gates", tn=1024)[0], mm_tn(ub, dz[None], name="dw_in_z", tn=1024)[0],
        mm_tn(ub, dxs[None], name="dw_in_xs", tn=1024)[0], mm_tn(ub, dbm[None], name="dw_in_b")[0],
        mm_tn(ub, dcm[None], name="dw_in_c")[0], mm_rows(ddtT, u, name="dw_in_dt").T,
        mm_tn(ub, dq[None], name="dw_in_q")[0], mm_tn(ub, dk[None], name="dw_in_k")[0],
        mm_tn(ub, dv[None], name="dw_in_v")[0]], axis=1)
    din = din.reshape(D, N_SHARD, IN_COLS // N_SHARD).transpose(1, 0, 2)
    d_mx = [dmx, din]
    (dh1, dg_mix_pre), (pair_mx0, pair_mx1, rffn2) = mix_in_bwd(
        dh2, h1, mix_pre_g, dgates, dz, dxs, dbm, dcm, ddtT, dq, dk, dv, w_gz, w_xbc, w_dtT, w_qkv, name="mix_in_bwd",
        ride=join_rides(pair_ride(d_mx), share_ride(tot_f2)))
    sums_mx = chip_sums(d_mx, [pair_mx0, pair_mx1], mix_tiles, "mx")
    (dx, df1, a1, dgate1, dup1, dg_ffn1_pre, dg_ffn1_post), parts_mx = ffn_bwd(
        dh1, x2, f1, gate1, up1, ffn1_pre_g, ffn1_post_g, wffn1, name="ffn1_bwd", ride=chips_ride(sums_mx))
    rmx, rin = run_ride(share_ride(totals(parts_mx, sums_mx, mix_tiles, "mx")), name="rs_share_mx")
    d_f1 = ffn_grads(n1, dgate1, dup1, a1, df1, "1")
    sums_f1 = chip_sums(d_f1, run_ride(pair_ride(d_f1), name="rs_pair_f1"), ffn_tiles, "f1")
    parts_f1 = run_ride(chips_ride(sums_f1), name="rs_chips_f1")
    (rffn1,) = run_ride(share_ride(totals(parts_f1, sums_f1, ffn_tiles, "f1")), name="rs_share_f1")
    FS = D_FF // N_SHARD
    gw = {
        'ffn1_w_gate': rffn1[0:FS], 'ffn1_w_up': rffn1[FS:2 * FS], 'ffn1_w_down': rffn1[2 * FS:],
        'ffn2_w_gate': rffn2[0:FS], 'ffn2_w_up': rffn2[FS:2 * FS], 'ffn2_w_down': rffn2[2 * FS:],
        'w_ssm_proj': rmx[0:512], 'w_attn_proj': rmx[512:768], 'w_out': rmx[768:1024], 'w_in': rin,
    }

    dconv_w = jnp.concatenate([acc[:, :SSM_CONV].transpose(1, 0, 2).reshape(SSM_CONV, -1)
                               for acc in (acc_xs, acc_bm, acc_cm)], axis=1)
    dconv_b = jnp.concatenate([acc[:, SSM_CONV].reshape(-1) for acc in (acc_xs, acc_bm, acc_cm)])
    small_local = {
        'ffn1_pre_g': dg_ffn1_pre, 'ffn1_post_g': dg_ffn1_post, 'mix_pre_g': dg_mix_pre, 'conv_w': dconv_w,
        'conv_b': dconv_b, 'dt_bias': acc_head[:, :, 0], 'a_log': acc_head[:, :, 1],
        'd_skip': acc_xs[:, SSM_CONV + 2].reshape(SSM_HEADS, SSM_HEAD_DIM).sum(axis=1),
        'ssm_norm_g': acc_xs[:, SSM_CONV + 1].reshape(-1), 'attn_sinks': dsinks, 'rel_bias_table': dtable,
        'mix_post_g': dg_mix_post, 'ffn2_pre_g': dg_ffn2_pre, 'ffn2_post_g': dg_ffn2_post,
    }
    full_shapes = [(SSM_CONV, SSM_CONV_DIM) if n == 'conv_w' else w[n].shape for n in SMALL]
    packed = _pack_rows([small_local[n] for n in SMALL] + [jnp.sum(loss_parts[:, 0, 0])])
    total = small_allreduce(packed, name="allreduce_small")
    *small_g, loss = _unpack_rows(total, full_shapes + [()])
    for n, g in zip(SMALL, small_g):
        gw[n] = g
    gw['conv_w'] = lax.dynamic_slice(gw['conv_w'], (0, s_me * (SSM_CONV_DIM // N_SHARD)),
                                     (SSM_CONV, SSM_CONV_DIM // N_SHARD))[None]

    delta, new_m, new_v = {}, {}, {}
    for n in BIG:
        lay = tr if n.endswith(('w_gate', 'w_up')) else (lambda a: a)
        d_, m_, v_ = adamw(lay(w[n][0]), gw[n], lay(m[n][0]), lay(v[n][0]), name="adamw_" + n, rt=gw[n].shape[0] // 4)
        gw[n] = lay(gw[n])[None]
        delta[n], new_m[n], new_v[n] = lay(d_)[None], lay(m_)[None], lay(v_)[None]
    shapes = [w[n].shape for n in SMALL]
    outs = adamw(_pack_rows([w[n] for n in SMALL]), _pack_rows([gw[n] for n in SMALL]),
                 _pack_rows([m[n] for n in SMALL]), _pack_rows([v[n] for n in SMALL]), name="adamw_small")
    for res, buf in zip((delta, new_m, new_v), outs):
        for n, val in zip(SMALL, _unpack_rows(buf, shapes)):
            res[n] = val
    return (loss, dx.reshape(batch, seq, D), *[gw[n].reshape(w[n].shape) for n in WEIGHTS],
            *[delta[n] for n in WEIGHTS], *[new_m[n] for n in WEIGHTS], *[new_v[n] for n in WEIGHTS])
```

```python
import functools
import math

import jax
import jax.numpy as jnp
import numpy as np
from jax import lax
from jax.experimental import pallas as pl
from jax.experimental.pallas import tpu as pltpu
from jax.experimental.pallas import tpu_sc as plsc

F32 = jnp.float32
BF16 = jnp.bfloat16

D_MODEL = 1024
D_FF = 2816
N_SHARD = 4
SSM_D_INNER = 2048
SSM_HEAD_DIM = 64
SSM_HEADS = 32
SSM_GROUPS = 4
SSM_HPG = SSM_HEADS // SSM_GROUPS
SSM_GW = SSM_D_INNER // SSM_GROUPS
SSM_STATE = 128
SSM_CONV = 4
SSM_CHUNK = 128
SSM_CONV_DIM = SSM_D_INNER + 2 * SSM_GROUPS * SSM_STATE
ATTN_Q_HEADS = 16
ATTN_KV_HEADS = 4
ATTN_REP = ATTN_Q_HEADS // ATTN_KV_HEADS
ATTN_HEAD_DIM = 64
ATTN_BLOCK = 128
ATTN_Q_DIM = 1024
ATTN_KV_DIM = 256
REL_BUCKETS = 32
REL_MAX_DISTANCE = 128
RMS_EPS = 1e-6
IN_COLS = 8736
ADAM_LR = 0.001
ADAM_B1 = 0.9
ADAM_B2 = 0.999
ADAM_EPS = 1e-08
ADAM_WD = 0.01
ADAM_STEP = 10
HALO = 8

VMEM_LIMIT = 56 * 1024 * 1024


def _cparams(*sem):
    return pltpu.CompilerParams(dimension_semantics=tuple(sem) if sem else None, vmem_limit_bytes=VMEM_LIMIT)


def _dot(a, b):
    return jnp.dot(a, b, preferred_element_type=F32)


def _dot_nt(a, b):
    return lax.dot_general(a, b, (((1,), (1,)), ((), ())), preferred_element_type=F32)


def _dot_tn(a, b):
    return lax.dot_general(a, b, (((0,), (0,)), ((), ())), preferred_element_type=F32)


def _dot_hi(a, b):
    return jnp.dot(a, b, preferred_element_type=F32, precision=lax.Precision.HIGHEST)


def _sigmoid(x):
    return 0.5 * jnp.tanh(0.5 * x) + 0.5


def _resident(shape, index=None):
    index = (0,) * len(shape) if index is None else tuple(index)
    return pl.BlockSpec(shape, lambda *_: index, pipeline_mode=pl.Buffered(1))


def _part(packed, rows, part):
    return _resident((N_SHARD, rows, packed.shape[2]), (0, part, 0))


def _rows(tm, width):
    return pl.BlockSpec((tm, width), lambda i: (i, 0))


class Ride:
    def __init__(self, inputs, out_shapes, aliases, scratch, start, finish):
        self.inputs, self.out_shapes, self.aliases = list(inputs), list(out_shapes), list(aliases)
        self.scratch, self.start, self.finish = list(scratch), start, finish


def join_rides(*rides):
    def cut(refs, sizes):
        out, at = [], 0
        for n in sizes:
            out.append(refs[at:at + n])
            at += n
        return out

    k_in = [len(r.inputs) for r in rides]
    k_out = [len(r.out_shapes) for r in rides]
    k_scr = [len(r.scratch) for r in rides]

    def each(step):
        def run(ins, outs, sems):
            for r, i, o, s in zip(rides, cut(ins, k_in), cut(outs, k_out), cut(sems, k_scr)):
                getattr(r, step)(i, o, s)
        return run

    aliases = [(sum(k_in[:n]) + i, sum(k_out[:n]) + j) for n, r in enumerate(rides) for i, j in r.aliases]
    return Ride([a for r in rides for a in r.inputs], [s for r in rides for s in r.out_shapes], aliases,
                [s for r in rides for s in r.scratch], each("start"), each("finish"))


def _call(body, *, grid, in_specs, args, out_specs, out_shape, name, sem, scratch=(), aliases=None, ride=None):
    aliases = dict(aliases or {})
    if ride is None:
        return pl.pallas_call(body, grid=grid, in_specs=in_specs, out_specs=out_specs, out_shape=out_shape,
                              scratch_shapes=list(scratch), input_output_aliases=aliases,
                              compiler_params=_cparams(*sem), name=name)(*args)
    n_in, n_out, n_scr = len(in_specs), len(out_specs), len(scratch)
    k_in, k_out = len(ride.inputs), len(ride.out_shapes)

    def riding(*refs):
        ins, refs = refs[:n_in], refs[n_in:]
        ex_in, refs = refs[:k_in], refs[k_in:]
        outs, refs = refs[:n_out], refs[n_out:]
        ex_out, refs = refs[:k_out], refs[k_out:]
        scr, ex_scr = refs[:n_scr], refs[n_scr:]
        first = functools.reduce(jnp.logical_and, [pl.program_id(a) == 0 for a in range(len(grid))])
        last = functools.reduce(jnp.logical_and, [pl.program_id(a) == grid[a] - 1 for a in range(len(grid))])

        @pl.when(first)
        def _():
            ride.start(ex_in, ex_out, ex_scr)

        body(*ins, *outs, *scr)

        @pl.when(last)
        def _():
            ride.finish(ex_in, ex_out, ex_scr)

    aliases.update({n_in + i: n_out + j for i, j in ride.aliases})
    res = pl.pallas_call(
        riding, grid=grid, in_specs=list(in_specs) + [ANY] * k_in, out_specs=list(out_specs) + [ANY] * k_out,
        out_shape=list(out_shape) + ride.out_shapes, scratch_shapes=list(scratch) + ride.scratch,
        input_output_aliases=aliases, compiler_params=_cparams(*["arbitrary"] * len(grid)), name=name,
    )(*args, *ride.inputs)
    return res[:n_out], res[n_out:]


def run_ride(ride, *, name):
    k_in = len(ride.inputs)

    def body(*refs):
        ex_in, ex_out, sems = refs[:k_in], refs[k_in:k_in + len(ride.out_shapes)], refs[k_in + len(ride.out_shapes):]
        ride.start(ex_in, ex_out, sems)
        ride.finish(ex_in, ex_out, sems)

    return pl.pallas_call(body, in_specs=[ANY] * k_in, out_specs=[ANY] * len(ride.out_shapes),
                          out_shape=ride.out_shapes, scratch_shapes=ride.scratch,
                          input_output_aliases=dict(ride.aliases), name=name)(*ride.inputs)


def ffn_fwd(h, g_pre, wffn, g_post, target=None, *, name, tm=512, ride=None):
    T, D = h.shape
    NS, FS = N_SHARD, wffn.shape[1] // 3
    with_loss = target is not None
    nt = T // tm

    def body(*refs):
        if with_loss:
            (h_ref, gpre_ref, wg_ref, wu_ref, wd_ref, gpost_ref, tgt_ref,
             hout_ref, n_ref, gate_ref, up_ref, f_ref, dy_ref, loss_ref) = refs
        else:
            (h_ref, gpre_ref, wg_ref, wu_ref, wd_ref, gpost_ref,
             hout_ref, n_ref, gate_ref, up_ref, f_ref) = refs
        hh = h_ref[...]
        r = lax.rsqrt(jnp.mean(hh * hh, axis=-1, keepdims=True) + RMS_EPS)
        n = (hh * r * gpre_ref[...]).astype(BF16)
        n_ref[...] = n
        acc = jnp.zeros((tm, D), F32)
        for s in range(NS):
            gate = _dot_nt(n, wg_ref[s])
            up = _dot_nt(n, wu_ref[s])
            gate_ref[s] = gate.astype(BF16)
            up_ref[s] = up.astype(BF16)
            a = (gate * _sigmoid(gate) * up).astype(BF16)
            acc = acc + _dot(a, wd_ref[s])
        f_ref[...] = acc
        r2 = lax.rsqrt(jnp.mean(acc * acc, axis=-1, keepdims=True) + RMS_EPS)
        out = hh + 0.5 * (acc * r2 * gpost_ref[...])
        hout_ref[...] = out
        if with_loss:
            e = out - tgt_ref[...]
            dy_ref[...] = e * (1.0 / D)
            loss_ref[...] = jnp.full((1, 8, 128), 0.5 / D, F32) * jnp.sum(e * e)

    in_specs = [_rows(tm, D), _resident((1, D)), _part(wffn, FS, 0), _part(wffn, FS, 1), _part(wffn, FS, 2),
                _resident((1, D))]
    args = [h, g_pre, wffn, wffn, wffn, g_post]
    out_shape = [jax.ShapeDtypeStruct((T, D), F32), jax.ShapeDtypeStruct((T, D), BF16),
                 jax.ShapeDtypeStruct((NS, T, FS), BF16), jax.ShapeDtypeStruct((NS, T, FS), BF16),
                 jax.ShapeDtypeStruct((T, D), F32)]
    seg = pl.BlockSpec((NS, tm, FS), lambda i: (0, i, 0))
    out_specs = [_rows(tm, D), _rows(tm, D), seg, seg, _rows(tm, D)]
    if with_loss:
        in_specs.append(_rows(tm, D))
        args.append(target)
        out_shape += [jax.ShapeDtypeStruct((T, D), F32), jax.ShapeDtypeStruct((nt, 8, 128), F32)]
        out_specs += [_rows(tm, D), pl.BlockSpec((1, 8, 128), lambda i: (i, 0, 0))]
    return _call(body, grid=(nt,), in_specs=in_specs, args=args, out_specs=out_specs, out_shape=out_shape,
                 sem=("parallel",), name=name, ride=ride)


def ffn_bwd(dout, h, f, gate, up, g_pre, g_post, wffn, *, name, tm=256, ride=None):
    T, D = h.shape
    NS, FS = N_SHARD, wffn.shape[1] // 3
    nt = T // tm

    def body(dout_ref, h_ref, f_ref, gate_ref, up_ref, gpre_ref, gpost_ref, wg_ref, wu_ref, wd_ref,
             dh_ref, df_ref, a_ref, dgate_ref, dup_ref, dgpre_ref, dgpost_ref):
        @pl.when(pl.program_id(0) == 0)
        def _():
            dgpre_ref[...] = jnp.zeros_like(dgpre_ref)
            dgpost_ref[...] = jnp.zeros_like(dgpost_ref)

        do = dout_ref[...]
        ff = f_ref[...]
        d_fn = 0.5 * do
        r2 = lax.rsqrt(jnp.mean(ff * ff, axis=-1, keepdims=True) + RMS_EPS)
        dgpost_ref[...] += jnp.sum(d_fn * ff * r2, axis=0, keepdims=True)
        t = d_fn * gpost_ref[...]
        df = r2 * t - ff * (r2 * r2 * r2 * jnp.mean(t * ff, axis=-1, keepdims=True))
        dfb = df.astype(BF16)
        df_ref[...] = dfb
        dn = jnp.zeros((tm, D), F32)
        for s in range(NS):
            da = _dot_nt(dfb, wd_ref[s])
            g = gate_ref[s].astype(F32)
            u = up_ref[s].astype(F32)
            sg = _sigmoid(g)
            silu = g * sg
            a_ref[s] = (silu * u).astype(BF16)
            dgt = (da * u * (sg * (1.0 + g * (1.0 - sg)))).astype(BF16)
            dupv = (da * silu).astype(BF16)
            dgate_ref[s] = dgt
            dup_ref[s] = dupv
            dn = dn + _dot(dgt, wg_ref[s]) + _dot(dupv, wu_ref[s])
        hh = h_ref[...]
        r1 = lax.rsqrt(jnp.mean(hh * hh, axis=-1, keepdims=True) + RMS_EPS)
        dgpre_ref[...] += jnp.sum(dn * hh * r1, axis=0, keepdims=True)
        t = dn * gpre_ref[...]
        dh_ref[...] = do + r1 * t - hh * (r1 * r1 * r1 * jnp.mean(t * hh, axis=-1, keepdims=True))

    seg = pl.BlockSpec((NS, tm, FS), lambda i: (0, i, 0))
    acc = pl.BlockSpec((1, D), lambda i: (0, 0))
    return _call(
        body, grid=(nt,),
        in_specs=[_rows(tm, D), _rows(tm, D), _rows(tm, D), seg, seg, _resident((1, D)), _resident((1, D)),
                  _part(wffn, FS, 0), _part(wffn, FS, 1), _part(wffn, FS, 2)],
        args=[dout, h, f, gate, up, g_pre, g_post, wffn, wffn, wffn],
        out_specs=[_rows(tm, D), _rows(tm, D), seg, seg, seg, acc, acc],
        out_shape=[jax.ShapeDtypeStruct((T, D), F32), jax.ShapeDtypeStruct((T, D), BF16),
                   jax.ShapeDtypeStruct((NS, T, FS), BF16), jax.ShapeDtypeStruct((NS, T, FS), BF16),
                   jax.ShapeDtypeStruct((NS, T, FS), BF16),
                   jax.ShapeDtypeStruct((1, D), F32), jax.ShapeDtypeStruct((1, D), F32)],
        sem=("arbitrary",), name=name, ride=ride)


def mm_tn(a, g, *, name, tt=4096, tn=None, a_cols=None, into=None):
    Ba, T, _ = a.shape
    Bg, _, N = g.shape
    B, K = a_cols if a_cols else (max(Ba, Bg), a.shape[2])
    tn = N if tn is None else tn
    tt = min(tt, T)
    nsteps = T // tt

    def body(*refs):
        a_ref, g_ref, o_ref = refs[0], refs[1], refs[-1]

        @pl.when(pl.program_id(2) == 0)
        def _():
            o_ref[...] = jnp.zeros_like(o_ref)

        o_ref[0] += _dot_tn(a_ref[0], g_ref[0].astype(BF16))

    if a_cols:
        a_map = lambda b, j, t: (0, t, b)
    else:
        a_map = (lambda b, j, t: (b, t, 0)) if Ba > 1 else (lambda b, j, t: (0, t, 0))
    in_specs = [pl.BlockSpec((1, tt, K), a_map),
                pl.BlockSpec((1, tt, tn), (lambda b, j, t: (b, t, j)) if Bg > 1 else (lambda b, j, t: (0, t, j)))]
    args = [a, g]
    if into is None:
        out_shape, part, aliases = jax.ShapeDtypeStruct((B, K, N), F32), 0, {}
    else:
        buf, part = into
        out_shape, aliases = jax.ShapeDtypeStruct(buf.shape, F32), {2: 0}
        in_specs.append(ANY)
        args.append(buf)
    return pl.pallas_call(
        body, grid=(B, N // tn, nsteps), in_specs=in_specs,
        out_specs=pl.BlockSpec((1, K, tn), lambda b, j, t: (b, part, j)),
        out_shape=out_shape, input_output_aliases=aliases,
        compiler_params=_cparams("parallel", "parallel", "arbitrary"), name=name)(*args)


def mix_in_fwd(h, g, w_gz, w_xbc, w_dtT, w_qkv, *, name, tm=256, ride=None):
    T, D = h.shape
    nt = T // tm
    CB = 1024

    def body(h_ref, g_ref, wgz_ref, wxbc_ref, wdtT_ref, wqkv_ref,
             u_ref, gates_ref, z_ref, xbc_ref, dtT_ref, q_ref, k_ref, v_ref):
        hh = h_ref[...]
        r = lax.rsqrt(jnp.mean(hh * hh, axis=-1, keepdims=True) + RMS_EPS)
        u = (hh * r * g_ref[...]).astype(BF16)
        u_ref[...] = u
        for cb in range(0, 2048, CB):
            gates_ref[:, cb:cb + CB] = _dot(u, wgz_ref[:, cb:cb + CB]).astype(BF16)
            z_ref[:, cb:cb + CB] = _dot(u, wgz_ref[:, 2048 + cb:2048 + cb + CB])
        for cb in range(0, SSM_CONV_DIM, CB):
            xbc_ref[:, cb:cb + CB] = _dot(u, wxbc_ref[:, cb:cb + CB])
        dtT_ref[...] = _dot_nt(wdtT_ref[...], u)
        q_ref[...] = (_dot(u, wqkv_ref[:, 0:ATTN_Q_DIM]) * ATTN_SCALE).astype(BF16)
        k_ref[...] = _dot(u, wqkv_ref[:, ATTN_Q_DIM:ATTN_Q_DIM + ATTN_KV_DIM]).astype(BF16)
        v_ref[...] = _dot(u, wqkv_ref[:, ATTN_Q_DIM + ATTN_KV_DIM:]).astype(BF16)

    sds = jax.ShapeDtypeStruct
    return _call(
        body, grid=(nt,),
        in_specs=[_rows(tm, D), _resident((1, D)), _resident(w_gz.shape), _resident(w_xbc.shape),
                  _resident(w_dtT.shape), _resident(w_qkv.shape)],
        args=[h, g, w_gz, w_xbc, w_dtT, w_qkv],
        out_specs=[_rows(tm, D), _rows(tm, 2048), _rows(tm, 2048), _rows(tm, SSM_CONV_DIM),
                   pl.BlockSpec((SSM_HEADS, tm), lambda i: (0, i)),
                   _rows(tm, ATTN_Q_DIM), _rows(tm, ATTN_KV_DIM), _rows(tm, ATTN_KV_DIM)],
        out_shape=[sds((T, D), BF16), sds((T, 2048), BF16), sds((T, 2048), F32), sds((T, SSM_CONV_DIM), F32),
                   sds((SSM_HEADS, T), F32),
                   sds((T, ATTN_Q_DIM), BF16), sds((T, ATTN_KV_DIM), BF16), sds((T, ATTN_KV_DIM), BF16)],
        sem=("parallel",), name=name, ride=ride)


def _softplus(x):
    return jnp.maximum(x, 0.0) + jnp.log(1.0 + jnp.exp(-jnp.abs(x)))


def _iota(shape, axis):
    return lax.broadcasted_iota(jnp.int32, shape, axis)


def _attn_specs(nb):
    BLK = ATTN_BLOCK

    def specs(last):
        def cur(b, n):
            return b * nb + (n if last is None else jnp.minimum(n, nb - 1))

        def prev(b, n):
            return b * nb + jnp.maximum((n if last is None else jnp.minimum(n, nb - 1)) - 1, 0)
        return cur, prev
    return specs


MASKED = -1e30
ATTN_SCALE = ATTN_HEAD_DIM ** -0.5


def attn_window():
    i = np.arange(ATTN_BLOCK)[:, None]
    j = np.arange(2 * ATTN_BLOCK)[None, :]
    return (j > i) & (j <= i + ATTN_BLOCK)


def _attn_group(kk, q_ref, bias_ref, sink_ref):
    BLK, HD = ATTN_BLOCK, ATTN_HEAD_DIM
    heads = range(ATTN_REP * kk, ATTN_REP * (kk + 1))
    qg = jnp.concatenate([q_ref[:, HD * hd:HD * (hd + 1)] for hd in heads], axis=0)
    bias_p = jnp.concatenate([bias_ref[hd, :, 0:BLK] for hd in heads], axis=0)
    bias_c = jnp.concatenate([bias_ref[hd, :, BLK:2 * BLK] for hd in heads], axis=0)
    sink = jnp.concatenate([jnp.broadcast_to(sink_ref[0:1, hd:hd + 1], (BLK, 1)) for hd in heads], axis=0)
    return qg, bias_p, bias_c, sink


def attn_bias(table_t, onehot, *, name):
    def body(t_ref, f_ref, o_ref):
        o_ref[...] = _dot_hi(t_ref[...], f_ref[...])
    return pl.pallas_call(body, out_shape=jax.ShapeDtypeStruct((ATTN_Q_HEADS, onehot.shape[1]), F32),
                          compiler_params=_cparams(), name=name)(table_t, onehot)


def attn_bias_bwd(dbias, onehot, *, name):
    def body(d_ref, f_ref, o_ref):
        o_ref[...] = lax.dot_general(d_ref[...], f_ref[...], (((1,), (1,)), ((), ())), preferred_element_type=F32,
                                     precision=lax.Precision.HIGHEST)
    return pl.pallas_call(body, out_shape=jax.ShapeDtypeStruct((ATTN_Q_HEADS, REL_BUCKETS), F32),
                          compiler_params=_cparams(), name=name)(dbias, onehot)


def attn_fwd(q, k, v, bias, sinks, *, batch, name):
    T = q.shape[0]
    BLK, HD = ATTN_BLOCK, ATTN_HEAD_DIM
    nb = T // batch // BLK
    cur, prev = _attn_specs(nb)(None)

    def body(q_ref, kc_ref, kp_ref, vc_ref, vp_ref, bias_ref, sink_ref, o_ref, lse_ref):
        n = pl.program_id(1)
        for kk in range(ATTN_KV_HEADS):
            ks = slice(HD * kk, HD * (kk + 1))
            kc, kp, vc, vp = kc_ref[:, ks], kp_ref[:, ks], vc_ref[:, ks], vp_ref[:, ks]
            qg, bias_p, bias_c, sink = _attn_group(kk, q_ref, bias_ref, sink_ref)
            lp = jnp.where(n > 0, _dot_nt(qg, kp) + bias_p, MASKED)
            lc = _dot_nt(qg, kc) + bias_c
            mx = jnp.maximum(jnp.max(jnp.maximum(lp, lc), axis=-1, keepdims=True), sink)
            pp = jnp.exp(lp - mx)
            pc = jnp.exp(lc - mx)
            den = jnp.sum(pp + pc, axis=-1, keepdims=True) + jnp.exp(sink - mx)
            o = ((_dot(pp.astype(BF16), vp) + _dot(pc.astype(BF16), vc)) * (1.0 / den)).astype(BF16)
            lse = mx + jnp.log(den)
            for r in range(ATTN_REP):
                hd = ATTN_REP * kk + r
                o_ref[:, HD * hd:HD * (hd + 1)] = o[BLK * r:BLK * (r + 1)]
                lse_ref[:, hd:hd + 1] = lse[BLK * r:BLK * (r + 1)]

    sds = jax.ShapeDtypeStruct
    return pl.pallas_call(
        body, grid=(batch, nb),
        in_specs=[pl.BlockSpec((BLK, ATTN_Q_DIM), lambda b, n: (cur(b, n), 0)),
                  pl.BlockSpec((BLK, ATTN_KV_DIM), lambda b, n: (cur(b, n), 0)),
                  pl.BlockSpec((BLK, ATTN_KV_DIM), lambda b, n: (prev(b, n), 0)),
                  pl.BlockSpec((BLK, ATTN_KV_DIM), lambda b, n: (cur(b, n), 0)),
                  pl.BlockSpec((BLK, ATTN_KV_DIM), lambda b, n: (prev(b, n), 0)),
                  pl.BlockSpec((ATTN_Q_HEADS, BLK, 2 * BLK), lambda b, n: (0, 0, 0)),
                  pl.BlockSpec((1, ATTN_Q_HEADS), lambda b, n: (0, 0))],
        out_specs=[pl.BlockSpec((BLK, ATTN_Q_DIM), lambda b, n: (cur(b, n), 0)),
                   pl.BlockSpec((BLK, ATTN_Q_HEADS), lambda b, n: (cur(b, n), 0))],
        out_shape=[sds((T, ATTN_Q_DIM), BF16), sds((T, ATTN_Q_HEADS), F32)],
        compiler_params=_cparams("parallel", "parallel"), name=name)(q, k, k, v, v, bias, sinks)


def _proj_specs(wmix):
    return [_part(wmix, 512, 0), _part(wmix, 256, 2), _part(wmix, 256, 3)]


def _natural(w_ref):
    return w_ref[...].reshape(-1, w_ref.shape[2])


def mix_out_fwd(ys, o, gates, h, wmix, g_post, *, name, tm=512):
    T, D = h.shape
    nt = T // tm

    def body(ys_ref, o_ref, gates_ref, h_ref, wssm_ref, wattn_ref, wout_ref, g_ref,
             hout_ref, yssm_ref, yattn_ref, mix_ref, merged_ref):
        y_ssm = _dot(ys_ref[...], _natural(wssm_ref))
        y_attn = _dot(o_ref[...], _natural(wattn_ref))
        yssm_ref[...] = y_ssm.astype(BF16)
        yattn_ref[...] = y_attn.astype(BF16)
        merged = (_sigmoid(gates_ref[:, 0:D].astype(F32)) * y_ssm
                  + _sigmoid(gates_ref[:, D:2 * D].astype(F32)) * y_attn).astype(BF16)
        merged_ref[...] = merged
        mix = _dot(merged, _natural(wout_ref))
        mix_ref[...] = mix.astype(BF16)
        r = lax.rsqrt(jnp.mean(mix * mix, axis=-1, keepdims=True) + RMS_EPS)
        hout_ref[...] = h_ref[...] + mix * r * g_ref[...]

    sds = jax.ShapeDtypeStruct
    return pl.pallas_call(
        body, grid=(nt,),
        in_specs=[_rows(tm, SSM_D_INNER), _rows(tm, ATTN_Q_DIM), _rows(tm, 2 * D), _rows(tm, D),
                  *_proj_specs(wmix), _resident((1, D))],
        out_specs=[_rows(tm, D)] * 5,
        out_shape=[sds((T, D), F32), sds((T, D), BF16), sds((T, D), BF16), sds((T, D), BF16), sds((T, D), BF16)],
        compiler_params=_cparams("parallel"), name=name)(ys, o, gates, h, wmix, wmix, wmix, g_post)


def mix_out_bwd(dh, mix, y_ssm, y_attn, gates, wmix, g_post, *, name, tm=256, ride=None):
    T, D = dh.shape
    nt = T // tm

    def body(dh_ref, mix_ref, yssm_ref, yattn_ref, gates_ref, wssm_ref, wattn_ref, wout_ref, g_ref,
             dmix_ref, dyssm_ref, dyattn_ref, dgates_ref, dys_ref, do_ref, dg_ref):
        @pl.when(pl.program_id(0) == 0)
        def _():
            dg_ref[...] = jnp.zeros_like(dg_ref)

        do = dh_ref[...]
        mix = mix_ref[...].astype(F32)
        r = lax.rsqrt(jnp.mean(mix * mix, axis=-1, keepdims=True) + RMS_EPS)
        dg_ref[...] += jnp.sum(do * mix * r, axis=0, keepdims=True)
        t = do * g_ref[...]
        dmix = (r * t - mix * (r * r * r * jnp.mean(t * mix, axis=-1, keepdims=True))).astype(BF16)
        dmix_ref[...] = dmix
        dmerged = _dot_nt(dmix, _natural(wout_ref))
        s1 = _sigmoid(gates_ref[:, 0:D].astype(F32))
        s2 = _sigmoid(gates_ref[:, D:2 * D].astype(F32))
        dyssm = (dmerged * s1).astype(BF16)
        dyattn = (dmerged * s2).astype(BF16)
        dyssm_ref[...] = dyssm
        dyattn_ref[...] = dyattn
        dgates_ref[:, 0:D] = (dmerged * yssm_ref[...].astype(F32) * (s1 * (1.0 - s1))).astype(BF16)
        dgates_ref[:, D:2 * D] = (dmerged * yattn_ref[...].astype(F32) * (s2 * (1.0 - s2))).astype(BF16)
        dys_ref[...] = _dot_nt(dyssm, _natural(wssm_ref))
        do_ref[...] = _dot_nt(dyattn, _natural(wattn_ref)).astype(BF16)

    sds = jax.ShapeDtypeStruct
    return _call(
        body, grid=(nt,),
        in_specs=[_rows(tm, D), _rows(tm, D), _rows(tm, D), _rows(tm, D), _rows(tm, 2 * D),
                  *_proj_specs(wmix), _resident((1, D))],
        args=[dh, mix, y_ssm, y_attn, gates, wmix, wmix, wmix, g_post],
        out_specs=[_rows(tm, D), _rows(tm, D), _rows(tm, D), _rows(tm, 2 * D), _rows(tm, SSM_D_INNER),
                   _rows(tm, ATTN_Q_DIM), pl.BlockSpec((1, D), lambda i: (0, 0))],
        out_shape=[sds((T, D), BF16), sds((T, D), BF16), sds((T, D), BF16), sds((T, 2 * D), BF16),
                   sds((T, SSM_D_INNER), F32), sds((T, ATTN_Q_DIM), BF16), sds((1, D), F32)],
        sem=("arbitrary",), name=name, ride=ride)


def attn_bwd(q, k, v, o, do, lse, bias, sinks, *, batch, name):
    T = q.shape[0]
    BLK, HD = ATTN_BLOCK, ATTN_HEAD_DIM
    nb = T // batch // BLK
    cur, prev = _attn_specs(nb)(nb)
    scale = HD ** -0.5

    def body(q_ref, kc_ref, kp_ref, vc_ref, vp_ref, o_ref, do_ref, lse_ref, bias_ref, sink_ref,
             dq_ref, dk_ref, dv_ref, dbias_ref, dsink_ref, ck, cv):
        b = pl.program_id(0)
        n = pl.program_id(1)

        @pl.when(jnp.logical_and(b == 0, n == 0))
        def _():
            dbias_ref[...] = jnp.zeros_like(dbias_ref)
            dsink_ref[...] = jnp.zeros_like(dsink_ref)

        @pl.when(n == 0)
        def _():
            ck[...] = jnp.zeros_like(ck)
            cv[...] = jnp.zeros_like(cv)

        @pl.when(n == nb)
        def _():
            dk_ref[...] = ck[...].astype(BF16)
            dv_ref[...] = cv[...].astype(BF16)

        @pl.when(n < nb)
        def _():
            lane16 = _iota((1, ATTN_Q_HEADS), 1)
            dsink = jnp.zeros((1, ATTN_Q_HEADS), F32)
            for kk in range(ATTN_KV_HEADS):
                ks = slice(HD * kk, HD * (kk + 1))
                kc, kp, vc, vp = kc_ref[:, ks], kp_ref[:, ks], vc_ref[:, ks], vp_ref[:, ks]
                heads = range(ATTN_REP * kk, ATTN_REP * (kk + 1))
                qg, bias_p, bias_c, sink = _attn_group(kk, q_ref, bias_ref, sink_ref)
                dog = jnp.concatenate([do_ref[:, HD * hd:HD * (hd + 1)] for hd in heads], axis=0)
                og = jnp.concatenate([o_ref[:, HD * hd:HD * (hd + 1)] for hd in heads], axis=0)
                lse = jnp.concatenate([lse_ref[:, hd:hd + 1] for hd in heads], axis=0)
                lp = jnp.where(n > 0, _dot_nt(qg, kp) + bias_p, MASKED)
                lc = _dot_nt(qg, kc) + bias_c
                pp = jnp.exp(lp - lse)
                pc = jnp.exp(lc - lse)
                delta = jnp.sum(dog.astype(F32) * og.astype(F32), axis=-1, keepdims=True)
                dlp = pp * (_dot_nt(dog, vp) - delta)
                dlc = pc * (_dot_nt(dog, vc) - delta)
                sd = jnp.exp(sink - lse) * delta
                dlpb = dlp.astype(BF16)
                dlcb = dlc.astype(BF16)
                dqg = ((_dot(dlpb, kp) + _dot(dlcb, kc)) * scale).astype(BF16)
                for r, hd in enumerate(heads):
                    rows = slice(BLK * r, BLK * (r + 1))
                    dsink = dsink + jnp.where(lane16 == hd, -jnp.sum(sd[rows], axis=0, keepdims=True), 0.0)
                    dbias_ref[hd, :, 0:BLK] += dlp[rows]
                    dbias_ref[hd, :, BLK:2 * BLK] += dlc[rows]
                    dq_ref[:, HD * hd:HD * (hd + 1)] = dqg[rows]
                dk_ref[:, ks] = (ck[:, ks] + _dot_tn(dlpb, qg)).astype(BF16)
                dv_ref[:, ks] = (cv[:, ks] + _dot_tn(pp.astype(BF16), dog)).astype(BF16)
                ck[:, ks] = _dot_tn(dlcb, qg)
                cv[:, ks] = _dot_tn(pc.astype(BF16), dog)
            dsink_ref[...] += dsink

    sds = jax.ShapeDtypeStruct
    qspec = pl.BlockSpec((BLK, ATTN_Q_DIM), lambda b, n: (cur(b, n), 0))
    cspec = pl.BlockSpec((BLK, ATTN_KV_DIM), lambda b, n: (cur(b, n), 0))
    pspec = pl.BlockSpec((BLK, ATTN_KV_DIM), lambda b, n: (prev(b, n), 0))
    late = pl.BlockSpec((BLK, ATTN_KV_DIM), lambda b, n: (b * nb + jnp.maximum(n - 1, 0), 0))
    return pl.pallas_call(
        body, grid=(batch, nb + 1),
        in_specs=[qspec, cspec, pspec, cspec, pspec, qspec, qspec,
                  pl.BlockSpec((BLK, ATTN_Q_HEADS), lambda b, n: (cur(b, n), 0)),
                  pl.BlockSpec((ATTN_Q_HEADS, BLK, 2 * BLK), lambda b, n: (0, 0, 0)),
                  pl.BlockSpec((1, ATTN_Q_HEADS), lambda b, n: (0, 0))],
        out_specs=[qspec, late, late,
                   pl.BlockSpec((ATTN_Q_HEADS, BLK, 2 * BLK), lambda b, n: (0, 0, 0)),
                   pl.BlockSpec((1, ATTN_Q_HEADS), lambda b, n: (0, 0))],
        out_shape=[sds((T, ATTN_Q_DIM), BF16), sds((T, ATTN_KV_DIM), BF16), sds((T, ATTN_KV_DIM), BF16),
                   sds((ATTN_Q_HEADS, BLK, 2 * BLK), F32), sds((1, ATTN_Q_HEADS), F32)],
        scratch_shapes=[pltpu.VMEM((BLK, ATTN_KV_DIM), F32), pltpu.VMEM((BLK, ATTN_KV_DIM), F32)],
        compiler_params=_cparams("arbitrary", "arbitrary"), name=name)(q, k, k, v, v, o, do, lse, bias, sinks)


def _conv_bwd(dxc, pre, xp_ref, w_ref, carry_ref, acc_ref, dp_ref, g, last):
    Q = SSM_CHUNK
    sg = _sigmoid(pre)
    dpre = dxc * (sg * (1.0 + pre * (1.0 - sg)))
    dp_ref[0:Q, :] = dpre
    dp_ref[Q:Q + HALO, :] = carry_ref[g]
    carry_ref[g] = dpre[0:HALO, :]
    rows = [jnp.sum(dpre * xp_ref[pl.ds(HALO - 3 + k, Q), :], axis=0, keepdims=True) for k in range(SSM_CONV)]
    rows.append(jnp.sum(dpre, axis=0, keepdims=True))
    rows.append(jnp.zeros((HALO - SSM_CONV - 1, dpre.shape[1]), F32))
    acc_ref[g] += jnp.concatenate(rows, axis=0)
    dx = w_ref[3:4, :] * dpre
    for k in range(SSM_CONV - 1):
        dx = dx + w_ref[k:k + 1, :] * dp_ref[pl.ds(3 - k, Q), :]
    return dx


def mix_in_bwd(dh, h, g, dgates, dz, dxs, dbm, dcm, ddtT, dq, dk, dv, w_gz, w_xbc, w_dtT, w_qkv, *, name, tm=512,
               ride=None):
    T, D = h.shape
    nt = T // tm
    GN = SSM_GROUPS * SSM_STATE

    def body(dh_ref, h_ref, g_ref, dgates_ref, dz_ref, dxs_ref, dbm_ref, dcm_ref, ddt_ref, dq_ref, dk_ref, dv_ref,
             wgz_ref, wxbc_ref, wdt_ref, wqkv_ref, dhin_ref, dg_ref):
        @pl.when(pl.program_id(0) == 0)
        def _():
            dg_ref[...] = jnp.zeros_like(dg_ref)

        du = _dot_nt(dgates_ref[...], wgz_ref[:, 0:2048])
        du = du + _dot_nt(dz_ref[...], wgz_ref[:, 2048:4096])
        du = du + _dot_nt(dxs_ref[...], wxbc_ref[:, 0:SSM_D_INNER])
        du = du + _dot_nt(dbm_ref[...], wxbc_ref[:, SSM_D_INNER:SSM_D_INNER + GN])
        du = du + _dot_nt(dcm_ref[...], wxbc_ref[:, SSM_D_INNER + GN:])
        du = du + _dot_tn(ddt_ref[...].astype(BF16), wdt_ref[...])
        du = du + _dot_nt(dq_ref[...], wqkv_ref[:, 0:ATTN_Q_DIM])
        du = du + _dot_nt(dk_ref[...], wqkv_ref[:, ATTN_Q_DIM:ATTN_Q_DIM + ATTN_KV_DIM])
        du = du + _dot_nt(dv_ref[...], wqkv_ref[:, ATTN_Q_DIM + ATTN_KV_DIM:])
        hh = h_ref[...]
        r = lax.rsqrt(jnp.mean(hh * hh, axis=-1, keepdims=True) + RMS_EPS)
        dg_ref[...] += jnp.sum(du * hh * r, axis=0, keepdims=True)
        t = du * g_ref[...]
        dhin_ref[...] = dh_ref[...] + r * t - hh * (r * r * r * jnp.mean(t * hh, axis=-1, keepdims=True))

    sds = jax.ShapeDtypeStruct
    return _call(
        body, grid=(nt,),
        in_specs=[_rows(tm, D), _rows(tm, D), _resident((1, D)), _rows(tm, 2048), _rows(tm, 2048), _rows(tm, SSM_D_INNER),
                  _rows(tm, GN), _rows(tm, GN), pl.BlockSpec((SSM_HEADS, tm), lambda i: (0, i)),
                  _rows(tm, ATTN_Q_DIM), _rows(tm, ATTN_KV_DIM),
                  _rows(tm, ATTN_KV_DIM), _resident(w_gz.shape), _resident(w_xbc.shape), _resident(w_dtT.shape),
                  _resident(w_qkv.shape)],
        args=[dh, h, g, dgates, dz, dxs, dbm, dcm, ddtT, dq, dk, dv, w_gz, w_xbc, w_dtT, w_qkv],
        out_specs=[_rows(tm, D), pl.BlockSpec((1, D), lambda i: (0, 0))],
        out_shape=[sds((T, D), F32), sds((1, D), F32)],
        sem=("arbitrary",), name=name, ride=ride)


PAIRS = SSM_HPG // 2
PW = 2 * SSM_HEAD_DIM


def ssd_scalars(dt_rawT, dt_bias, a_log, *, name, chunks=8):
    H, T = dt_rawT.shape
    Q, G, HPG = SSM_CHUNK, SSM_GROUPS, SSM_HPG
    chunks = math.gcd(chunks, T // Q)
    span = Q * chunks

    def body(dtT_ref, dtb_ref, alog_ref, cols_ref, acsT_ref, dec_ref):
        aT = -jnp.exp(alog_ref[...])
        triT = (_iota((Q, Q), 0) <= _iota((Q, Q), 1)).astype(F32)
        for j in range(chunks):
            at = slice(Q * j, Q * (j + 1))
            dtT = _softplus(dtT_ref[:, at] + dtb_ref[...])
            acsT = _dot_hi(dtT * aT, triT)
            lastT = acsT[:, Q - 1:Q]
            acsT_ref[:, at] = acsT
            dec_ref[j] = jnp.exp(lastT)
            parts = [dtT, acsT, jnp.exp(lastT - acsT), jnp.exp(acsT)]
            colsT = jnp.concatenate([q[HPG * g:HPG * (g + 1)] for g in range(G) for q in parts], axis=0).T
            for g in range(G):
                cols_ref[g, at, :] = colsT[:, 4 * HPG * g:4 * HPG * (g + 1)]

    sds = jax.ShapeDtypeStruct
    return pl.pallas_call(
        body, grid=(T // span,),
        in_specs=[pl.BlockSpec((H, span), lambda i: (0, i)), pl.BlockSpec((H, 1), lambda i: (0, 0)),
                  pl.BlockSpec((H, 1), lambda i: (0, 0))],
        out_specs=[pl.BlockSpec((G, span, 4 * HPG), lambda i: (0, i, 0)), pl.BlockSpec((H, span), lambda i: (0, i)),
                   pl.BlockSpec((chunks, H, 1), lambda i: (i, 0, 0))],
        out_shape=[sds((G, T, 4 * HPG), F32), sds((H, T), F32), sds((T // Q, H, 1), F32)],
        compiler_params=_cparams("parallel"), name=name)(dt_rawT, dt_bias, a_log)


CONV_XS, CONV_BM, CONV_CM = slice(0, SSM_GW), slice(SSM_GW, SSM_GW + SSM_STATE), slice(SSM_GW + SSM_STATE, SSM_GW + 2 * SSM_STATE)


def _pair_cols(cols, base, p, lo):
    k = base + 2 * p
    return jnp.where(lo, cols[:, k:k + 1], cols[:, k + 1:k + 2])


def _pair_row(colT, p, lo_row):
    return jnp.where(lo_row, colT[2 * p:2 * p + 1, :], colT[2 * p + 1:2 * p + 2, :])


def _pair_operands(pp, p, s, causal, lo, xb):
    zero = jnp.zeros_like(xb)
    rhs = jnp.concatenate([jnp.where(lo, xb, zero), jnp.where(lo, zero, xb)], axis=0)
    ls, ms = [], []
    for k in (2 * p, 2 * p + 1):
        seg = pp["cols"][:, 8 + k:9 + k] - pp["acsT"][k:k + 1, :]
        l = jnp.exp(jnp.where(causal, seg, -1e30))
        ls.append(l)
        ms.append(s * l)
    lhs = jnp.concatenate([m.astype(BF16) for m in ms], axis=1)
    return lhs, rhs, ls


def _group_cols(g):
    return (slice(SSM_GW * g, SSM_GW * (g + 1)),
            slice(SSM_D_INNER + SSM_STATE * g, SSM_D_INNER + SSM_STATE * (g + 1)),
            slice(SSM_D_INNER + SSM_STATE * (SSM_GROUPS + g), SSM_D_INNER + SSM_STATE * (SSM_GROUPS + g + 1)))


def _conv_pre(x, halo, w, b, xp_ref):
    Q = SSM_CHUNK
    xp_ref[0:HALO, :] = halo
    xp_ref[HALO:HALO + Q, :] = x
    pre = b + w[3:4, :] * x
    for k in range(SSM_CONV - 1):
        pre = pre + w[k:k + 1, :] * xp_ref[pl.ds(HALO - 3 + k, Q), :]
    return pre


def _ssd_prologue(g, first, xbc_ref, halo_ref, conv_ref, cols_ref, acsT_ref, dec_ref, xp_xs, xp_bm, xp_cm):
    cp = conv_ref[g]
    heads = slice(SSM_HPG * g, SSM_HPG * (g + 1))
    out, taps = {}, {}
    for n, at, pk, xp in zip(("xs", "bm", "cm"), _group_cols(g), (CONV_XS, CONV_BM, CONV_CM), (xp_xs, xp_bm, xp_cm)):
        taps[n] = (cp[:, pk], cp[SSM_CONV:SSM_CONV + 1, pk])
        halo = jnp.where(first, 0.0, halo_ref[:, at])
        pre = _conv_pre(xbc_ref[:, at], halo, *taps[n], xp.at[g])
        out["pre_" + n] = pre
        out[n] = pre * _sigmoid(pre)
    out.update(taps=taps, acsT=acsT_ref[heads, :], decayT=dec_ref[0, heads, :], cols=cols_ref[g])
    return out


def ssd_fwd(xbc, z, scalars, conv_pack, vec_pack, *, batch, name):
    T = xbc.shape[0]
    Q, GW, N, G = SSM_CHUNK, SSM_GW, SSM_STATE, SSM_GROUPS
    nc = T // batch // Q
    row = lambda b, c: b * nc + c
    hrow = lambda b, c: jnp.maximum(row(b, c) * (Q // HALO) - 1, 0)

    def body(xbc_ref, halo_ref, z_ref, conv_ref, cols_ref, acsT_ref, dec_ref, vec_ref, y_ref, ys_ref, st_ref,
             state, xp_xs, xp_bm, xp_cm):
        first = pl.program_id(1) == 0
        causal = _iota((Q, Q), 0) >= _iota((Q, Q), 1)
        lo = _iota((Q, PW), 1) < SSM_HEAD_DIM
        lo_row = _iota((1, PW), 1) < SSM_HEAD_DIM

        @pl.when(first)
        def _():
            state[...] = jnp.zeros_like(state)

        for g in range(G):
            cols = slice(GW * g, GW * (g + 1))
            pp = _ssd_prologue(g, first, xbc_ref, halo_ref, conv_ref, cols_ref, acsT_ref, dec_ref, xp_xs, xp_bm, xp_cm)
            xs = pp["xs"]
            bb = pp["bm"].astype(BF16)
            cb = pp["cm"].astype(BF16)
            s = _dot_nt(cb, bb)
            entering = state[g]
            st_ref[0, g] = entering
            wide = lambda base: jnp.concatenate([_pair_cols(pp["cols"], base, p, lo) for p in range(PAIRS)], axis=1)
            x = xs * wide(0)
            xb = x.astype(BF16)
            yd = []
            for p in range(PAIRS):
                lhs, rhs, _ = _pair_operands(pp, p, s, causal, lo, xb[:, PW * p:PW * (p + 1)])
                yd.append(_dot(lhs, rhs))
            y = jnp.concatenate(yd, axis=1) + _dot(cb, entering.astype(BF16)) * wide(24) + vec_ref[0:1, cols] * xs
            decay = jnp.concatenate([_pair_row(pp["decayT"], p, lo_row) for p in range(PAIRS)], axis=1)
            state[g] = entering * decay + _dot_tn(bb, (x * wide(16)).astype(BF16))
            y_ref[:, cols] = y
            zz = z_ref[:, cols]
            yg = y * (zz * _sigmoid(zz))
            rr = lax.rsqrt(jnp.mean(yg * yg, axis=-1, keepdims=True) + RMS_EPS)
            ys_ref[:, cols] = (yg * rr * vec_ref[1:2, cols]).astype(BF16)

    sds = jax.ShapeDtypeStruct
    cols3, acsT, decay = scalars
    return pl.pallas_call(
        body, grid=(batch, nc),
        in_specs=[pl.BlockSpec((Q, SSM_CONV_DIM), lambda b, c: (row(b, c), 0)),
                  pl.BlockSpec((HALO, SSM_CONV_DIM), lambda b, c: (hrow(b, c), 0)),
                  pl.BlockSpec((Q, SSM_D_INNER), lambda b, c: (row(b, c), 0)),
                  pl.BlockSpec((G, HALO, GW + 2 * N), lambda b, c: (0, 0, 0)),
                  pl.BlockSpec((G, Q, 4 * SSM_HPG), lambda b, c: (0, row(b, c), 0)),
                  pl.BlockSpec((SSM_HEADS, Q), lambda b, c: (0, row(b, c))),
                  pl.BlockSpec((1, SSM_HEADS, 1), lambda b, c: (row(b, c), 0, 0)),
                  pl.BlockSpec((2, SSM_D_INNER), lambda b, c: (0, 0))],
        out_specs=[pl.BlockSpec((Q, SSM_D_INNER), lambda b, c: (row(b, c), 0)),
                   pl.BlockSpec((Q, SSM_D_INNER), lambda b, c: (row(b, c), 0)),
                   pl.BlockSpec((1, G, N, GW), lambda b, c: (row(b, c), 0, 0, 0))],
        out_shape=[sds((T, SSM_D_INNER), F32), sds((T, SSM_D_INNER), BF16), sds((T // Q, G, N, GW), F32)],
        scratch_shapes=[pltpu.VMEM((G, N, GW), F32), pltpu.VMEM((G, HALO + Q, GW), F32),
                        pltpu.VMEM((G, HALO + Q, N), F32), pltpu.VMEM((G, HALO + Q, N), F32)],
        compiler_params=_cparams("arbitrary", "arbitrary"), name=name,
    )(xbc, xbc, z, conv_pack, cols3, acsT, decay, vec_pack)


def ssd_bwd(dys, y, xbc, z, dt_rawT, states, scalars, conv_pack, col_pack, vec_pack, *, batch, name, ride=None):
    T = xbc.shape[0]
    Q, GW, N, G, HPG = SSM_CHUNK, SSM_GW, SSM_STATE, SSM_GROUPS, SSM_HPG
    nc = T // batch // Q
    row = lambda b, c: b * nc + (nc - 1 - c)
    hrow = lambda b, c: jnp.maximum(row(b, c) * (Q // HALO) - 1, 0)

    def body(xbc_ref, halo_ref, z_ref, y_ref, dys_ref, dtT_ref, st_ref, conv_ref, cols_ref, acsT_ref, dec_ref, col_ref,
             vec_ref, dz_ref, dxs_ref, dbm_ref, dcm_ref, ddtT_ref, acc_xs, acc_bm, acc_cm, acc_head,
             dstate, xp_xs, xp_bm, xp_cm, dp_xs, dp_bm, dp_cm, cy_xs, cy_bm, cy_cm):
        b = pl.program_id(0)
        cr = pl.program_id(1)
        first = cr == nc - 1
        last = cr == 0

        @pl.when(jnp.logical_and(b == 0, cr == 0))
        def _():
            acc_xs[...] = jnp.zeros_like(acc_xs)
            acc_bm[...] = jnp.zeros_like(acc_bm)
            acc_cm[...] = jnp.zeros_like(acc_cm)
            acc_head[...] = jnp.zeros_like(acc_head)

        @pl.when(last)
        def _():
            dstate[...] = jnp.zeros_like(dstate)
            cy_xs[...] = jnp.zeros_like(cy_xs)
            cy_bm[...] = jnp.zeros_like(cy_bm)
            cy_cm[...] = jnp.zeros_like(cy_cm)

        causal = _iota((Q, Q), 0) >= _iota((Q, Q), 1)
        lo = _iota((Q, PW), 1) < SSM_HEAD_DIM
        lo_row = _iota((1, PW), 1) < SSM_HEAD_DIM
        tri = (_iota((Q, Q), 0) >= _iota((Q, Q), 1)).astype(F32)
        e8 = (_iota((HPG, GW), 0) == lax.shift_right_logical(_iota((HPG, GW), 1), 6)).astype(F32)
        seg_sum = lambda v: lax.dot_general(e8, v, (((1,), (1,)), ((), ())), preferred_element_type=F32,
                                            precision=lax.Precision.HIGHEST)
        lane = _iota((HPG, N), 1)
        for g in range(G):
            cols = slice(GW * g, GW * (g + 1))
            ncols = slice(N * g, N * (g + 1))
            heads = slice(HPG * g, HPG * (g + 1))
            pp = _ssd_prologue(g, first, xbc_ref, halo_ref, conv_ref, cols_ref, acsT_ref, dec_ref, xp_xs, xp_bm, xp_cm)
            xs, decayT = pp["xs"], pp["decayT"]
            dtrT = dtT_ref[heads, :] + col_ref[heads, 0:1]
            dtT = _softplus(dtrT)
            aT = -jnp.exp(col_ref[heads, 1:2])

            yv = y_ref[:, cols]
            zz = z_ref[:, cols]
            sz = _sigmoid(zz)
            silu_z = zz * sz
            yg = yv * silu_z
            rr = lax.rsqrt(jnp.mean(yg * yg, axis=-1, keepdims=True) + RMS_EPS)
            dys_v = dys_ref[:, cols]
            d_ng = jnp.sum(dys_v * yg * rr, axis=0, keepdims=True)
            t = dys_v * vec_ref[1:2, cols]
            dyg = rr * t - yg * (rr * rr * rr * jnp.mean(t * yg, axis=-1, keepdims=True))
            dy = dyg * silu_z
            dz_ref[:, cols] = (dyg * yv * (sz * (1.0 + zz * (1.0 - sz)))).astype(BF16)
            dsk = vec_ref[0:1, cols]
            d_dsk = jnp.sum(dy * xs, axis=0, keepdims=True)

            bb = pp["bm"].astype(BF16)
            cb = pp["cm"].astype(BF16)
            s = _dot_nt(cb, bb)
            wide = lambda base: jnp.concatenate([_pair_cols(pp["cols"], base, p, lo) for p in range(PAIRS)], axis=1)
            dt_x, w_x, e_x = wide(0), wide(16), wide(24)
            decay = jnp.concatenate([_pair_row(decayT, p, lo_row) for p in range(PAIRS)], axis=1)
            x = xs * dt_x
            xw = x * w_x
            xb = x.astype(BF16)
            dyb = dy.astype(BF16)
            dye = (dy * e_x).astype(BF16)
            hp = st_ref[0, g]
            hpb = hp.astype(BF16)
            dh = dstate[g]
            dhb = dh.astype(BF16)
            ds_acc = jnp.zeros((Q, Q), F32)
            yd, dxd = [], []
            for p in range(PAIRS):
                tile = slice(PW * p, PW * (p + 1))
                lhs, rhs, ls = _pair_operands(pp, p, s, causal, lo, xb[:, tile])
                dm = _dot_nt(dyb[:, tile], rhs)
                dxd2 = _dot_tn(lhs, dyb[:, tile])
                dxd.append(jnp.where(lo, dxd2[0:Q], dxd2[Q:2 * Q]))
                ds_acc = ds_acc + dm[:, 0:Q] * ls[0] + dm[:, Q:2 * Q] * ls[1]
                yd.append(_dot(lhs, rhs))
            yd = jnp.concatenate(yd, axis=1)
            dxd = jnp.concatenate(dxd, axis=1)
            dxw = _dot(bb, dhb)
            dx_full = dxd + dxw * w_x
            tw = dxw * xw
            q1 = dyb.astype(F32) * yd + dy * (_dot(cb, hpb) * e_x) - tw - xb.astype(F32) * dxd
            q2 = dx_full * xs
            dxs_v = dsk * dy + dx_full * dt_x
            dstate[g] = dh * decay + _dot_tn(cb, dye)
            dsb = ds_acc.astype(BF16)
            d_c = _dot_nt(dye, hpb) + _dot(dsb, bb)
            d_b = _dot_nt(xw.astype(BF16), dhb) + _dot_tn(dsb, cb)
            rowv = jnp.sum(dh * hp, axis=0, keepdims=True) * decay + jnp.sum(tw, axis=0, keepdims=True)
            last_terms = jnp.sum(e8 * rowv, axis=1, keepdims=True)
            dacsT = seg_sum(q1) + jnp.where(_iota((1, Q), 1) == Q - 1, 1.0, 0.0) * last_terms
            d_dtaT = _dot_hi(dacsT, tri)
            ddtT = d_dtaT * aT + seg_sum(q2)
            d_alog = jnp.sum(d_dtaT * dtT, axis=1, keepdims=True) * aT
            ddt_rawT = ddtT * _sigmoid(dtrT)
            ddtT_ref[heads, :] = ddt_rawT
            d_dtb = jnp.sum(ddt_rawT, axis=1, keepdims=True)
            acc_head[g] += jnp.where(lane == 0, d_dtb, 0.0) + jnp.where(lane == 1, d_alog, 0.0)
            taps = pp["taps"]
            dxs_ref[:, cols] = _conv_bwd(dxs_v, pp["pre_xs"], xp_xs.at[g], taps["xs"][0], cy_xs, acc_xs, dp_xs.at[g], g,
                                         last).astype(BF16)
            dbm_ref[:, ncols] = _conv_bwd(d_b, pp["pre_bm"], xp_bm.at[g], taps["bm"][0], cy_bm, acc_bm, dp_bm.at[g], g,
                                          last).astype(BF16)
            dcm_ref[:, ncols] = _conv_bwd(d_c, pp["pre_cm"], xp_cm.at[g], taps["cm"][0], cy_cm, acc_cm, dp_cm.at[g], g,
                                          last).astype(BF16)
            acc_xs[g, pl.ds(SSM_CONV + 1, 2), :] += jnp.concatenate([d_ng, d_dsk], axis=0)

    sds = jax.ShapeDtypeStruct
    cols3, acsT, decay3 = scalars
    full = lambda shape: pl.BlockSpec(shape, lambda b, c: (0,) * len(shape))
    wide_in = pl.BlockSpec((Q, SSM_D_INNER), lambda b, c: (row(b, c), 0))
    heads_in = pl.BlockSpec((SSM_HEADS, Q), lambda b, c: (0, row(b, c)))
    return _call(
        body, grid=(batch, nc),
        in_specs=[pl.BlockSpec((Q, SSM_CONV_DIM), lambda b, c: (row(b, c), 0)),
                  pl.BlockSpec((HALO, SSM_CONV_DIM), lambda b, c: (hrow(b, c), 0)),
                  wide_in, wide_in, wide_in, heads_in,
                  pl.BlockSpec((1, G, N, GW), lambda b, c: (row(b, c), 0, 0, 0)),
                  full((G, HALO, GW + 2 * N)),
                  pl.BlockSpec((G, Q, 4 * HPG), lambda b, c: (0, row(b, c), 0)),
                  heads_in,
                  pl.BlockSpec((1, SSM_HEADS, 1), lambda b, c: (row(b, c), 0, 0)),
                  full((SSM_HEADS, 2)), full((2, SSM_D_INNER))],
        args=[xbc, xbc, z, y, dys, dt_rawT, states, conv_pack, cols3, acsT, decay3, col_pack, vec_pack],
        out_specs=[wide_in, wide_in,
                   pl.BlockSpec((Q, G * N), lambda b, c: (row(b, c), 0)),
                   pl.BlockSpec((Q, G * N), lambda b, c: (row(b, c), 0)),
                   heads_in, full((G, HALO, GW)), full((G, HALO, N)), full((G, HALO, N)), full((G, HPG, N))],
        out_shape=[sds((T, SSM_D_INNER), BF16), sds((T, SSM_D_INNER), BF16), sds((T, G * N), BF16),
                   sds((T, G * N), BF16), sds((SSM_HEADS, T), F32),
                   sds((G, HALO, GW), F32), sds((G, HALO, N), F32), sds((G, HALO, N), F32), sds((G, HPG, N), F32)],
        scratch=[pltpu.VMEM((G, N, GW), F32),
                 pltpu.VMEM((G, HALO + Q, GW), F32), pltpu.VMEM((G, HALO + Q, N), F32), pltpu.VMEM((G, HALO + Q, N), F32),
                 pltpu.VMEM((G, Q + HALO, GW), F32), pltpu.VMEM((G, Q + HALO, N), F32), pltpu.VMEM((G, Q + HALO, N), F32),
                 pltpu.VMEM((G, HALO, GW), F32), pltpu.VMEM((G, HALO, N), F32), pltpu.VMEM((G, HALO, N), F32)],
        sem=("arbitrary", "arbitrary"), name=name, ride=ride)


def mm_rows(a, b, *, name, tt=2048):
    M, T = a.shape
    N = b.shape[1]
    tt = min(tt, T)

    def body(a_ref, b_ref, o_ref):
        @pl.when(pl.program_id(0) == 0)
        def _():
            o_ref[...] = jnp.zeros_like(o_ref)

        o_ref[...] += _dot(a_ref[...].astype(BF16), b_ref[...])

    return pl.pallas_call(
        body, grid=(T // tt,),
        in_specs=[pl.BlockSpec((M, tt), lambda t: (0, t)), pl.BlockSpec((tt, N), lambda t: (t, 0))],
        out_specs=pl.BlockSpec((M, N), lambda t: (0, 0)), out_shape=jax.ShapeDtypeStruct((M, N), F32),
        compiler_params=_cparams("arbitrary"), name=name)(a, b)


MESH = pl.DeviceIdType.MESH
ANY = pl.BlockSpec(memory_space=pl.ANY)
ROW_ALIGN = 16


def _me():
    return lax.axis_index("x"), lax.axis_index("y"), lax.axis_index("c")


def _other_chips(x, y):
    return [(1 - x, y), (x, 1 - y), (1 - x, 1 - y)]


def _remote(src, dst, send_sem, recv_sem, to):
    return pltpu.make_async_remote_copy(src_ref=src, dst_ref=dst, send_sem=send_sem, recv_sem=recv_sem,
                                        device_id=to, device_id_type=MESH)


def _half(c, rows):
    return pl.ds(pl.multiple_of(c * (rows // 2), ROW_ALIGN), rows // 2)


def ag_ride(bufs):
    n = len(bufs)

    def copies(outs, sems):
        ici_send, ici_recv, d2d_send, d2d_recv = sems
        x, y, c = _me()
        sib = (x, y, 1 - c)
        ici, d2d, d2d_in = [], [], []
        for i in range(n):
            rows = outs[i].shape[1]
            mine = outs[i].at[2 * x + y, _half(c, rows)]
            for j, chip in enumerate(_other_chips(x, y)):
                ici.append(_remote(mine, mine, ici_send.at[i, j], ici_recv.at[i, j], (*chip, c)))
                landed = outs[i].at[2 * chip[0] + chip[1], _half(c, rows)]
                d2d.append((_remote(landed, landed, ici_send.at[i, j], ici_recv.at[i, j], (*chip, c)),
                            _remote(landed, landed, d2d_send.at[i, j], d2d_recv.at[i, j], sib)))
                lands = outs[i].at[2 * chip[0] + chip[1], _half(1 - c, rows)]
                d2d_in.append(_remote(lands, lands, d2d_send.at[i, j], d2d_recv.at[i, j], sib))
        return ici, d2d, d2d_in

    def start(ins, outs, sems):
        for cp in copies(outs, sems)[0]:
            cp.start()

    def finish(ins, outs, sems):
        ici, d2d, d2d_in = copies(outs, sems)
        for arrived, forward in d2d:
            arrived.wait_recv()
            forward.start()
        for cp in d2d_in:
            cp.wait_recv()
        for cp in ici + [forward for _, forward in d2d]:
            cp.wait_send()

    return Ride(bufs, [jax.ShapeDtypeStruct(b.shape, b.dtype) for b in bufs], [(i, i) for i in range(n)],
                [pltpu.SemaphoreType.DMA((n, 3))] * 4, start, finish)


def pair_ride(grads):
    n = len(grads)

    def copies(ins, outs, sems):
        x, y, c = _me()
        return [_remote(ins[i].at[:, _half(1 - c, ins[i].shape[1]), :], outs[i], sems[0].at[i], sems[1].at[i], (x, y, 1 - c))
                for i in range(n)]

    def start(ins, outs, sems):
        for cp in copies(ins, outs, sems):
            cp.start()

    def finish(ins, outs, sems):
        for cp in copies(ins, outs, sems):
            cp.wait()

    return Ride(grads, [jax.ShapeDtypeStruct((N_SHARD, g.shape[1] // 2, g.shape[2]), g.dtype) for g in grads], [],
                [pltpu.SemaphoreType.DMA((n,))] * 2, start, finish)


def rs_add(grad, part, c, *, rt, name):
    _, rows, cols = grad.shape
    r2 = rows // 2
    nrb = r2 // rt

    def body(c_ref, g_ref, p_ref, o_ref):
        o_ref[...] = (g_ref[...] + p_ref[...]).astype(BF16)

    return pl.pallas_call(
        body,
        grid_spec=pltpu.PrefetchScalarGridSpec(
            num_scalar_prefetch=1, grid=(N_SHARD, nrb),
            in_specs=[pl.BlockSpec((1, rt, cols), lambda k, i, c_ref: (k, c_ref[1] * nrb + i, 0)),
                      pl.BlockSpec((1, rt, cols), lambda k, i, c_ref: (k, i, 0))],
            out_specs=pl.BlockSpec((1, rt, cols), lambda k, i, c_ref: (k, i, 0))),
        out_shape=jax.ShapeDtypeStruct((N_SHARD, r2, cols), BF16),
        compiler_params=_cparams("parallel", "parallel"), name=name)(c, grad, part)


def chips_ride(sums):
    n = len(sums)

    def copies(ins, outs, sems):
        send, recv = sems
        x, y, c = _me()
        return [_remote(ins[i].at[2 * chip[0] + chip[1]], outs[i].at[2 * x + y], send.at[i, j], recv.at[i, j], (*chip, c))
                for i in range(n) for j, chip in enumerate(_other_chips(x, y))]

    def start(ins, outs, sems):
        for cp in copies(ins, outs, sems):
            cp.start()

    def finish(ins, outs, sems):
        for cp in copies(ins, outs, sems):
            cp.wait()

    return Ride(sums, [jax.ShapeDtypeStruct(s.shape, s.dtype) for s in sums], [],
                [pltpu.SemaphoreType.DMA((n, 3))] * 2, start, finish)


def rs_total(parts, own, where, *, rt, name):
    _, r2, cols = parts.shape
    nrb = r2 // rt

    def body(w_ref, p0, p1, p2, p3, own_ref, o_ref):
        s_me = w_ref[0]
        acc = None
        for k, p in enumerate((p0, p1, p2, p3)):
            term = jnp.where(s_me == k, own_ref[0], p[0]).astype(F32)
            acc = term if acc is None else acc + term
        o_ref[...] = acc

    def slot(k):
        return pl.BlockSpec((1, rt, cols), lambda i, w: (jnp.where(w[0] == k, (k + 1) % N_SHARD, k), i, 0))

    return pl.pallas_call(
        body,
        grid_spec=pltpu.PrefetchScalarGridSpec(
            num_scalar_prefetch=1, grid=(nrb,),
            in_specs=[slot(0), slot(1), slot(2), slot(3), pl.BlockSpec((1, rt, cols), lambda i, w: (w[0], i, 0))],
            out_specs=pl.BlockSpec((rt, cols), lambda i, w: (w[1] * nrb + i, 0))),
        out_shape=jax.ShapeDtypeStruct((2 * r2, cols), F32),
        compiler_params=_cparams("parallel"), name=name)(where, parts, parts, parts, parts, own)


def share_ride(totals):
    n = len(totals)

    def halves(outs, sems):
        x, y, c = _me()
        mine = [outs[i].at[_half(c, outs[i].shape[0])] for i in range(n)]
        other = [outs[i].at[_half(1 - c, outs[i].shape[0])] for i in range(n)]
        return ([_remote(m, m, sems[0].at[i], sems[1].at[i], (x, y, 1 - c)) for i, m in enumerate(mine)],
                [_remote(o, o, sems[0].at[i], sems[1].at[i], (x, y, 1 - c)) for i, o in enumerate(other)])

    def start(ins, outs, sems):
        for cp in halves(outs, sems)[0]:
            cp.start()

    def finish(ins, outs, sems):
        sent, landing = halves(outs, sems)
        for cp in landing:
            cp.wait_recv()
        for cp in sent:
            cp.wait_send()

    return Ride(totals, [jax.ShapeDtypeStruct(t.shape, t.dtype) for t in totals], [(i, i) for i in range(n)],
                [pltpu.SemaphoreType.DMA((n,))] * 2, start, finish)


def small_allreduce(buf, *, name):
    rows = buf.shape[0]

    def body(x_ref, o_ref, slots, send, recv):
        x, y, c = _me()
        me = 4 * x + 2 * y + c
        slots[me] = x_ref[...]
        sent = []
        for d in range(1, 8):
            peer = (1 - x if d & 4 else x, 1 - y if d & 2 else y, 1 - c if d & 1 else c)
            sent.append(_remote(x_ref, slots.at[me], send.at[d - 1], recv.at[d - 1], peer))
            sent[-1].start()
        for cp in sent:
            cp.wait()
        acc = slots[0]
        for k in range(1, 8):
            acc = acc + slots[k]
        o_ref[...] = acc

    return pl.pallas_call(
        body, out_shape=jax.ShapeDtypeStruct(buf.shape, F32),
        in_specs=[pl.BlockSpec(memory_space=pltpu.VMEM)], out_specs=pl.BlockSpec(memory_space=pltpu.VMEM),
        scratch_shapes=[pltpu.VMEM((8, rows, 128), F32), pltpu.SemaphoreType.DMA((7,)), pltpu.SemaphoreType.DMA((7,))],
        name=name)(buf)


def adamw(w, g, m, v, *, name, rt=None):
    rows, cols = w.shape
    rt = rows if rt is None else rt
    c1 = 1.0 - ADAM_B1 ** ADAM_STEP
    c2 = 1.0 - ADAM_B2 ** ADAM_STEP

    def body(w_ref, g_ref, m_ref, v_ref, d_ref, nm_ref, nv_ref):
        gg = g_ref[...]
        nm = ADAM_B1 * m_ref[...] + (1.0 - ADAM_B1) * gg
        nv = ADAM_B2 * v_ref[...] + (1.0 - ADAM_B2) * (gg * gg)
        nm_ref[...] = nm
        nv_ref[...] = nv
        d_ref[...] = -ADAM_LR * ((nm / c1) / (jnp.sqrt(nv / c2) + ADAM_EPS) + ADAM_WD * w_ref[...])

    spec = pl.BlockSpec((rt, cols), lambda i: (i, 0))
    return pl.pallas_call(
        body, grid=(rows // rt,), in_specs=[spec] * 4, out_specs=[spec] * 3,
        out_shape=[jax.ShapeDtypeStruct((rows, cols), F32)] * 3,
        compiler_params=_cparams("parallel"), name=name)(w, g, m, v)


SC_TILES = 32
SC_LANES = 16
SC_CHUNK = 8192


def adamw_sc(w, g, m, v, *, name):
    shape = w.shape
    n = w.size
    per_tile = n // SC_TILES
    chunk = math.gcd(per_tile, SC_CHUNK)
    assert n % SC_TILES == 0 and chunk % SC_LANES == 0, (shape, chunk)
    c1 = 1.0 - ADAM_B1 ** ADAM_STEP
    c2 = 1.0 - ADAM_B2 ** ADAM_STEP

    def body(w_hbm, g_hbm, m_hbm, v_hbm, d_hbm, nm_hbm, nv_hbm, wb, gb, mb, vb, db):
        tile = lax.axis_index("subcore") * 2 + lax.axis_index("core")

        @pl.loop(0, per_tile, step=chunk)
        def _(off):
            at = pl.ds(tile * per_tile + off, chunk)
            pltpu.sync_copy(w_hbm.at[at], wb)
            pltpu.sync_copy(g_hbm.at[at], gb)
            pltpu.sync_copy(m_hbm.at[at], mb)
            pltpu.sync_copy(v_hbm.at[at], vb)

            @pl.loop(0, chunk, step=SC_LANES)
            def _(i):
                s = pl.ds(i, SC_LANES)
                gg = gb[s]
                nm = ADAM_B1 * mb[s] + (1.0 - ADAM_B1) * gg
                nv = ADAM_B2 * vb[s] + (1.0 - ADAM_B2) * (gg * gg)
                mb[s] = nm
                vb[s] = nv
                db[s] = -ADAM_LR * ((nm / c1) / (jnp.sqrt(nv / c2) + ADAM_EPS) + ADAM_WD * wb[s])

            pltpu.sync_copy(db, d_hbm.at[at])
            pltpu.sync_copy(mb, nm_hbm.at[at])
            pltpu.sync_copy(vb, nv_hbm.at[at])

    flat = jax.ShapeDtypeStruct((n,), F32)
    outs = pl.kernel(
        body, name=name, out_type=(flat, flat, flat),
        mesh=plsc.VectorSubcoreMesh(core_axis_name="core", subcore_axis_name="subcore"),
        scratch_types=[pltpu.VMEM((chunk,), F32)] * 5,
    )(w.reshape(-1), g.reshape(-1), m.reshape(-1), v.reshape(-1))
    return [o.reshape(shape) for o in outs]


WEIGHTS = ['ffn1_pre_g', 'ffn1_w_gate', 'ffn1_w_up', 'ffn1_w_down', 'ffn1_post_g', 'mix_pre_g', 'w_in', 'conv_w',
           'conv_b', 'dt_bias', 'a_log', 'd_skip', 'ssm_norm_g', 'w_ssm_proj', 'attn_sinks', 'rel_bias_table',
           'w_attn_proj', 'w_out', 'mix_post_g', 'ffn2_pre_g', 'ffn2_w_gate', 'ffn2_w_up', 'ffn2_w_down', 'ffn2_post_g']
BIG = ['ffn1_w_gate', 'ffn1_w_up', 'ffn1_w_down', 'w_in', 'w_ssm_proj', 'w_attn_proj', 'w_out',
       'ffn2_w_gate', 'ffn2_w_up', 'ffn2_w_down']
SMALL = [w for w in WEIGHTS if w not in BIG]
SC_UPDATED = ['ffn2_w_gate', 'ffn2_w_up', 'ffn2_w_down', 'w_ssm_proj', 'w_attn_proj', 'w_out']


def _bucket_onehot():
    blk = ATTN_BLOCK
    dist = np.maximum(np.arange(blk)[:, None] + blk - np.arange(2 * blk)[None, :], 0)
    max_exact = REL_BUCKETS // 2
    d = np.maximum(dist, 1).astype(np.float32)
    large = max_exact + (np.log(d / np.float32(max_exact)) / np.float32(math.log(REL_MAX_DISTANCE / max_exact))
                         * np.float32(REL_BUCKETS - max_exact)).astype(np.int32)
    bucket = np.where(dist < max_exact, dist, np.minimum(large, REL_BUCKETS - 1)).reshape(-1)
    return jnp.asarray((bucket[None, :] == np.arange(REL_BUCKETS)[:, None]).astype(np.float32))


def _pack_rows(parts, mult=8):
    flat = jnp.concatenate([p.reshape(-1).astype(F32) for p in parts])
    rows = -(-flat.shape[0] // (128 * mult)) * mult
    return jnp.pad(flat, (0, rows * 128 - flat.shape[0])).reshape(rows, 128)


def _unpack_rows(buf, shapes):
    flat = buf.reshape(-1)
    out, at = [], 0
    for shp in shapes:
        size = int(np.prod(shp))
        out.append(flat[at:at + size].reshape(shp))
        at += size
    return out


def kernel(x, ffn1_pre_g, ffn1_w_gate, ffn1_w_up, ffn1_w_down, ffn1_post_g, mix_pre_g, w_in, conv_w, conv_b, dt_bias, a_log, d_skip, ssm_norm_g, w_ssm_proj, attn_sinks, rel_bias_table, w_attn_proj, w_out, mix_post_g, ffn2_pre_g, ffn2_w_gate, ffn2_w_up, ffn2_w_down, ffn2_post_g, loss_target, m_ffn1_pre_g, m_ffn1_w_gate, m_ffn1_w_up, m_ffn1_w_down, m_ffn1_post_g, m_mix_pre_g, m_w_in, m_conv_w, m_conv_b, m_dt_bias, m_a_log, m_d_skip, m_ssm_norm_g, m_w_ssm_proj, m_attn_sinks, m_rel_bias_table, m_w_attn_proj, m_w_out, m_mix_post_g, m_ffn2_pre_g, m_ffn2_w_gate, m_ffn2_w_up, m_ffn2_w_down, m_ffn2_post_g, v_ffn1_pre_g, v_ffn1_w_gate, v_ffn1_w_up, v_ffn1_w_down, v_ffn1_post_g, v_mix_pre_g, v_w_in, v_conv_w, v_conv_b, v_dt_bias, v_a_log, v_d_skip, v_ssm_norm_g, v_w_ssm_proj, v_attn_sinks, v_rel_bias_table, v_w_attn_proj, v_w_out, v_mix_post_g, v_ffn2_pre_g, v_ffn2_w_gate, v_ffn2_w_up, v_ffn2_w_down, v_ffn2_post_g):
    args = locals()
    w = {n: args[n] for n in WEIGHTS}
    m = {n: args["m_" + n] for n in WEIGHTS}
    v = {n: args["v_" + n] for n in WEIGHTS}
    batch, seq, D = x.shape
    T = batch * seq
    xi, yi, ci = _me()
    s_me = 2 * xi + yi
    x2 = x.reshape(T, D)
    tgt = loss_target.reshape(T, D)

    def own_slot(parts):
        p = jnp.concatenate([t[0] for t in parts], axis=0).astype(BF16)
        return lax.dynamic_update_slice(lax.empty((N_SHARD,) + p.shape, BF16), p[None], (s_me, 0, 0))

    tr = lambda a: jnp.swapaxes(a, -1, -2)
    (wffn1,) = run_ride(ag_ride([own_slot([tr(ffn1_w_gate), tr(ffn1_w_up), ffn1_w_down])]), name="ag_ffn1")
    col = lambda v: v.reshape(SSM_HEADS, 1)
    d_skip_x = jnp.repeat(d_skip, SSM_HEAD_DIM, axis=1)
    cw_slot = lax.dynamic_update_slice(jnp.zeros((SSM_CONV, SSM_CONV_DIM), F32),
                                       conv_w[0] * (ci == 0).astype(F32), (0, s_me * (SSM_CONV_DIM // N_SHARD)))
    conv_w_full = small_allreduce(cw_slot.reshape(-1, 128), name="ag_conv_w").reshape(SSM_CONV, SSM_CONV_DIM)
    cwb = jnp.concatenate([conv_w_full, conv_b, jnp.zeros((HALO - SSM_CONV - 1, SSM_CONV_DIM), F32)], axis=0)
    conv_pack = jnp.stack([jnp.concatenate([cwb[:, SSM_GW * g:SSM_GW * (g + 1)],
                                            cwb[:, SSM_D_INNER + SSM_STATE * g:SSM_D_INNER + SSM_STATE * (g + 1)],
                                            cwb[:, SSM_D_INNER + SSM_STATE * (SSM_GROUPS + g):
                                                SSM_D_INNER + SSM_STATE * (SSM_GROUPS + g + 1)]], axis=1)
                           for g in range(SSM_GROUPS)])
    vec_pack = jnp.concatenate([d_skip_x, ssm_norm_g], axis=0)
    col_pack = jnp.concatenate([col(dt_bias), col(a_log)], axis=1)

    (h1, n1, gate1, up1, f1), (gin, gmix) = ffn_fwd(
        x2, ffn1_pre_g, wffn1, ffn1_post_g, name="ffn1_fwd",
        ride=ag_ride([own_slot([w_in]), own_slot([w_ssm_proj, w_attn_proj, w_out])]))
    w_in_full = gin.transpose(1, 0, 2).reshape(D, IN_COLS)
    w_gz = w_in_full[:, 0:4096]
    w_xbc = w_in_full[:, 4096:4096 + SSM_CONV_DIM]
    w_dtT = w_in_full[:, 7168:7200].T
    w_qkv = w_in_full[:, 7200:]
    (u, gates, z, xbc, dt_rawT, q, k, vv), (wffn2,) = mix_in_fwd(
        h1, mix_pre_g, w_gz, w_xbc, w_dtT, w_qkv, name="mix_in_fwd",
        ride=ag_ride([own_slot([tr(ffn2_w_gate), tr(ffn2_w_up), ffn2_w_down])]))
    scalars = ssd_scalars(dt_rawT, col(dt_bias), col(a_log), name="ssd_scalars")
    y, ys, states = ssd_fwd(xbc, z, scalars, conv_pack, vec_pack, batch=batch, name="ssd_fwd")
    onehot = _bucket_onehot()
    bias = attn_bias(rel_bias_table.T, onehot, name="attn_bias").reshape(ATTN_Q_HEADS, ATTN_BLOCK, 2 * ATTN_BLOCK)
    bias = jnp.where(attn_window()[None], bias, MASKED)
    o, lse = attn_fwd(q, k, vv, bias, attn_sinks, batch=batch, name="attn_fwd")
    h2, y_ssm, y_attn, mix, merged = mix_out_fwd(ys, o, gates, h1, gmix, mix_post_g, name="mix_out_fwd")
    h3, n3, gate2, up2, f2, dy, loss_parts = ffn_fwd(h2, ffn2_pre_g, wffn2, ffn2_post_g, tgt, name="ffn2_fwd")

    where = jnp.stack([s_me, ci]).astype(jnp.int32)

    def chip_sums(grads, pair, tiles, tag):
        return [rs_add(g, p, where, rt=rt, name=f"rs_add_{tag}{i}") for i, (g, p, rt) in enumerate(zip(grads, pair, tiles))]

    def totals(parts, sums, tiles, tag):
        return [rs_total(p, s, where, rt=rt, name=f"rs_total_{tag}{i}")
                for i, (p, s, rt) in enumerate(zip(parts, sums, tiles))]

    def ffn_grads(n, dgate, dup, a, df, tag):
        d = mm_tn(dgate, n[None], into=(lax.empty(wffn1.shape, F32), 0), name="dw_gate" + tag)
        d = mm_tn(dup, n[None], into=(d, 1), name="dw_up" + tag)
        return [mm_tn(a, df[None], into=(d, 2), name="dw_down" + tag)]

    ffn_tiles, mix_tiles = [352], [256, 256]
    dh2, df2, a2, dgate2, dup2, dg_ffn2_pre, dg_ffn2_post = ffn_bwd(dy, h2, f2, gate2, up2, ffn2_pre_g, ffn2_post_g,
                                                                    wffn2, name="ffn2_bwd")
    d_f2 = ffn_grads(n3, dgate2, dup2, a2, df2, "2")
    (dmix, dyssm, dyattn, dgates, dys, do, dg_mix_post), pair_f2 = mix_out_bwd(
        dh2, mix, y_ssm, y_attn, gates, gmix, mix_post_g, name="mix_out_bwd", ride=pair_ride(d_f2))
    sums_f2 = chip_sums(d_f2, pair_f2, ffn_tiles, "f2")
    dq, dk, dv, dbias, dsinks = attn_bwd(q, k, vv, o, do, lse, bias, attn_sinks, batch=batch, name="attn_bwd")
    dtable = attn_bias_bwd(dbias.reshape(ATTN_Q_HEADS, -1), onehot, name="attn_bias_bwd").T
    (dz, dxs, dbm, dcm, ddtT, acc_xs, acc_bm, acc_cm, acc_head), parts_f2 = ssd_bwd(
        dys, y, xbc, z, dt_rawT, states, scalars, conv_pack, col_pack, vec_pack,
        batch=batch, name="ssd_bwd", ride=chips_ride(sums_f2))
    tot_f2 = totals(parts_f2, sums_f2, ffn_tiles, "f2")
    dmx = mm_tn(ys[None], dyssm[None], a_cols=(N_SHARD, 512), into=(lax.empty(gmix.shape, F32), 0), name="dw_ssm")
    dmx = mm_tn(o[None], dyattn[None], a_cols=(N_SHARD, 256), into=(dmx, 2), name="dw_attn")
    dmx = mm_tn(merged[None], dmix[None], a_cols=(N_SHARD, 256), into=(dmx, 3), name="dw_out")
    ub = u[None]
    din = jnp.concatenate([
        mm_tn(ub, dgates[None], name="dw_in_gates", tn=1024)[0], mm_tn(ub, dz[None], name="dw_in_z", tn=1024)[0],
        mm_tn(ub, dxs[None], name="dw_in_xs", tn=1024)[0], mm_tn(ub, dbm[None], name="dw_in_b")[0],
        mm_tn(ub, dcm[None], name="dw_in_c")[0], mm_rows(ddtT, u, name="dw_in_dt").T,
        mm_tn(ub, dq[None], name="dw_in_q")[0], mm_tn(ub, dk[None], name="dw_in_k")[0],
        mm_tn(ub, dv[None], name="dw_in_v")[0]], axis=1)
    din = din.reshape(D, N_SHARD, IN_COLS // N_SHARD).transpose(1, 0, 2)
    d_mx = [dmx, din]
    (dh1, dg_mix_pre), (pair_mx0, pair_mx1, rffn2) = mix_in_bwd(
        dh2, h1, mix_pre_g, dgates, dz, dxs, dbm, dcm, ddtT, dq, dk, dv, w_gz, w_xbc, w_dtT, w_qkv, name="mix_in_bwd",
        ride=join_rides(pair_ride(d_mx), share_ride(tot_f2)))
    sums_mx = chip_sums(d_mx, [pair_mx0, pair_mx1], mix_tiles, "mx")
    (dx, df1, a1, dgate1, dup1, dg_ffn1_pre, dg_ffn1_post), parts_mx = ffn_bwd(
        dh1, x2, f1, gate1, up1, ffn1_pre_g, ffn1_post_g, wffn1, name="ffn1_bwd", ride=chips_ride(sums_mx))
    rmx, rin = run_ride(share_ride(totals(parts_mx, sums_mx, mix_tiles, "mx")), name="rs_share_mx")
    d_f1 = ffn_grads(n1, dgate1, dup1, a1, df1, "1")
    sums_f1 = chip_sums(d_f1, run_ride(pair_ride(d_f1), name="rs_pair_f1"), ffn_tiles, "f1")
    parts_f1 = run_ride(chips_ride(sums_f1), name="rs_chips_f1")
    (rffn1,) = run_ride(share_ride(totals(parts_f1, sums_f1, ffn_tiles, "f1")), name="rs_share_f1")
    FS = D_FF // N_SHARD
    gw = {
        'ffn1_w_gate': rffn1[0:FS], 'ffn1_w_up': rffn1[FS:2 * FS], 'ffn1_w_down': rffn1[2 * FS:],
        'ffn2_w_gate': rffn2[0:FS], 'ffn2_w_up': rffn2[FS:2 * FS], 'ffn2_w_down': rffn2[2 * FS:],
        'w_ssm_proj': rmx[0:512], 'w_attn_proj': rmx[512:768], 'w_out': rmx[768:1024], 'w_in': rin,
    }

    dconv_w = jnp.concatenate([acc[:, :SSM_CONV].transpose(1, 0, 2).reshape(SSM_CONV, -1)
                               for acc in (acc_xs, acc_bm, acc_cm)], axis=1)
    dconv_b = jnp.concatenate([acc[:, SSM_CONV].reshape(-1) for acc in (acc_xs, acc_bm, acc_cm)])
    small_local = {
        'ffn1_pre_g': dg_ffn1_pre, 'ffn1_post_g': dg_ffn1_post, 'mix_pre_g': dg_mix_pre, 'conv_w': dconv_w,
        'conv_b': dconv_b, 'dt_bias': acc_head[:, :, 0], 'a_log': acc_head[:, :, 1],
        'd_skip': acc_xs[:, SSM_CONV + 2].reshape(SSM_HEADS, SSM_HEAD_DIM).sum(axis=1),
        'ssm_norm_g': acc_xs[:, SSM_CONV + 1].reshape(-1), 'attn_sinks': dsinks, 'rel_bias_table': dtable,
        'mix_post_g': dg_mix_post, 'ffn2_pre_g': dg_ffn2_pre, 'ffn2_post_g': dg_ffn2_post,
    }
    full_shapes = [(SSM_CONV, SSM_CONV_DIM) if n == 'conv_w' else w[n].shape for n in SMALL]
    packed = _pack_rows([small_local[n] for n in SMALL] + [jnp.sum(loss_parts[:, 0, 0])])
    total = small_allreduce(packed, name="allreduce_small")
    *small_g, loss = _unpack_rows(total, full_shapes + [()])
    for n, g in zip(SMALL, small_g):
        gw[n] = g
    gw['conv_w'] = lax.dynamic_slice(gw['conv_w'], (0, s_me * (SSM_CONV_DIM // N_SHARD)),
                                     (SSM_CONV, SSM_CONV_DIM // N_SHARD))[None]

    delta, new_m, new_v = {}, {}, {}
    for n in BIG:
        lay = tr if n.endswith(('w_gate', 'w_up')) else (lambda a: a)
        if n in SC_UPDATED:
            d_, m_, v_ = adamw_sc(lay(w[n][0]), gw[n], lay(m[n][0]), lay(v[n][0]), name="adamw_sc_" + n)
        else:
            d_, m_, v_ = adamw(lay(w[n][0]), gw[n], lay(m[n][0]), lay(v[n][0]), name="adamw_" + n,
                               rt=gw[n].shape[0] // 4)
        gw[n] = lay(gw[n])[None]
        delta[n], new_m[n], new_v[n] = lay(d_)[None], lay(m_)[None], lay(v_)[None]
    shapes = [w[n].shape for n in SMALL]
    outs = adamw(_pack_rows([w[n] for n in SMALL]), _pack_rows([gw[n] for n in SMALL]),
                 _pack_rows([m[n] for n in SMALL]), _pack_rows([v[n] for n in SMALL]), name="adamw_small")
    for res, buf in zip((delta, new_m, new_v), outs):
        for n, val in zip(SMALL, _unpack_rows(buf, shapes)):
            res[n] = val
    return (loss, dx.reshape(batch, seq, D), *[gw[n].reshape(w[n].shape) for n in WEIGHTS],
            *[delta[n] for n in WEIGHTS], *[new_m[n] for n in WEIGHTS], *[new_v[n] for n in WEIGHTS])
```

```python
import functools
import math

import jax
import jax.numpy as jnp
import numpy as np
from jax import lax
from jax.experimental import pallas as pl
from jax.experimental.pallas import tpu as pltpu

F32 = jnp.float32
BF16 = jnp.bfloat16

D_MODEL = 1024
D_FF = 2816
N_SHARD = 4
SSM_D_INNER = 2048
SSM_HEAD_DIM = 64
SSM_HEADS = 32
SSM_GROUPS = 4
SSM_HPG = SSM_HEADS // SSM_GROUPS
SSM_GW = SSM_D_INNER // SSM_GROUPS
SSM_STATE = 128
SSM_CONV = 4
SSM_CHUNK = 128
SSM_CONV_DIM = SSM_D_INNER + 2 * SSM_GROUPS * SSM_STATE
ATTN_Q_HEADS = 16
ATTN_KV_HEADS = 4
ATTN_REP = ATTN_Q_HEADS // ATTN_KV_HEADS
ATTN_HEAD_DIM = 64
ATTN_BLOCK = 128
ATTN_Q_DIM = 1024
ATTN_KV_DIM = 256
REL_BUCKETS = 32
REL_MAX_DISTANCE = 128
RMS_EPS = 1e-6
IN_COLS = 8736
ADAM_LR = 0.001
ADAM_B1 = 0.9
ADAM_B2 = 0.999
ADAM_EPS = 1e-08
ADAM_WD = 0.01
ADAM_STEP = 10
HALO = 8

VMEM_LIMIT = 56 * 1024 * 1024


def _cparams(*sem):
    return pltpu.CompilerParams(dimension_semantics=tuple(sem) if sem else None, vmem_limit_bytes=VMEM_LIMIT)


def _dot(a, b):
    return jnp.dot(a, b, preferred_element_type=F32)


def _dot_nt(a, b):
    return lax.dot_general(a, b, (((1,), (1,)), ((), ())), preferred_element_type=F32)


def _dot_tn(a, b):
    return lax.dot_general(a, b, (((0,), (0,)), ((), ())), preferred_element_type=F32)


def _dot_hi(a, b):
    return jnp.dot(a, b, preferred_element_type=F32, precision=lax.Precision.HIGHEST)


def _sigmoid(x):
    return 0.5 * jnp.tanh(0.5 * x) + 0.5


def _resident(shape, index=None):
    index = (0,) * len(shape) if index is None else tuple(index)
    return pl.BlockSpec(shape, lambda *_: index, pipeline_mode=pl.Buffered(1))


def _part(packed, rows, part):
    return _resident((N_SHARD, rows, packed.shape[2]), (0, part, 0))


def _rows(tm, width):
    return pl.BlockSpec((tm, width), lambda i: (i, 0))


class Ride:
    def __init__(self, inputs, out_shapes, aliases, scratch, start, finish):
        self.inputs, self.out_shapes, self.aliases = list(inputs), list(out_shapes), list(aliases)
        self.scratch, self.start, self.finish = list(scratch), start, finish


def join_rides(*rides):
    def cut(refs, sizes):
        out, at = [], 0
        for n in sizes:
            out.append(refs[at:at + n])
            at += n
        return out

    k_in = [len(r.inputs) for r in rides]
    k_out = [len(r.out_shapes) for r in rides]
    k_scr = [len(r.scratch) for r in rides]

    def each(step):
        def run(ins, outs, sems):
            for r, i, o, s in zip(rides, cut(ins, k_in), cut(outs, k_out), cut(sems, k_scr)):
                getattr(r, step)(i, o, s)
        return run

    aliases = [(sum(k_in[:n]) + i, sum(k_out[:n]) + j) for n, r in enumerate(rides) for i, j in r.aliases]
    return Ride([a for r in rides for a in r.inputs], [s for r in rides for s in r.out_shapes], aliases,
                [s for r in rides for s in r.scratch], each("start"), each("finish"))


def _call(body, *, grid, in_specs, args, out_specs, out_shape, name, sem, scratch=(), aliases=None, ride=None):
    aliases = dict(aliases or {})
    if ride is None:
        return pl.pallas_call(body, grid=grid, in_specs=in_specs, out_specs=out_specs, out_shape=out_shape,
                              scratch_shapes=list(scratch), input_output_aliases=aliases,
                              compiler_params=_cparams(*sem), name=name)(*args)
    n_in, n_out, n_scr = len(in_specs), len(out_specs), len(scratch)
    k_in, k_out = len(ride.inputs), len(ride.out_shapes)

    def riding(*refs):
        ins, refs = refs[:n_in], refs[n_in:]
        ex_in, refs = refs[:k_in], refs[k_in:]
        outs, refs = refs[:n_out], refs[n_out:]
        ex_out, refs = refs[:k_out], refs[k_out:]
        scr, ex_scr = refs[:n_scr], refs[n_scr:]
        first = functools.reduce(jnp.logical_and, [pl.program_id(a) == 0 for a in range(len(grid))])
        last = functools.reduce(jnp.logical_and, [pl.program_id(a) == grid[a] - 1 for a in range(len(grid))])

        @pl.when(first)
        def _():
            ride.start(ex_in, ex_out, ex_scr)

        body(*ins, *outs, *scr)

        @pl.when(last)
        def _():
            ride.finish(ex_in, ex_out, ex_scr)

    aliases.update({n_in + i: n_out + j for i, j in ride.aliases})
    res = pl.pallas_call(
        riding, grid=grid, in_specs=list(in_specs) + [ANY] * k_in, out_specs=list(out_specs) + [ANY] * k_out,
        out_shape=list(out_shape) + ride.out_shapes, scratch_shapes=list(scratch) + ride.scratch,
        input_output_aliases=aliases, compiler_params=_cparams(*["arbitrary"] * len(grid)), name=name,
    )(*args, *ride.inputs)
    return res[:n_out], res[n_out:]


def run_ride(ride, *, name):
    k_in = len(ride.inputs)

    def body(*refs):
        ex_in, ex_out, sems = refs[:k_in], refs[k_in:k_in + len(ride.out_shapes)], refs[k_in + len(ride.out_shapes):]
        ride.start(ex_in, ex_out, sems)
        ride.finish(ex_in, ex_out, sems)

    return pl.pallas_call(body, in_specs=[ANY] * k_in, out_specs=[ANY] * len(ride.out_shapes),
                          out_shape=ride.out_shapes, scratch_shapes=ride.scratch,
                          input_output_aliases=dict(ride.aliases), name=name)(*ride.inputs)


def ffn_fwd(h, g_pre, wffn, g_post, target=None, *, name, tm=512, ride=None):
    T, D = h.shape
    NS, FS = N_SHARD, wffn.shape[1] // 3
    with_loss = target is not None
    nt = T // tm

    def body(*refs):
        if with_loss:
            (h_ref, gpre_ref, wg_ref, wu_ref, wd_ref, gpost_ref, tgt_ref,
             hout_ref, n_ref, gate_ref, up_ref, f_ref, dy_ref, loss_ref) = refs
        else:
            (h_ref, gpre_ref, wg_ref, wu_ref, wd_ref, gpost_ref,
             hout_ref, n_ref, gate_ref, up_ref, f_ref) = refs
        hh = h_ref[...]
        r = lax.rsqrt(jnp.mean(hh * hh, axis=-1, keepdims=True) + RMS_EPS)
        n = (hh * r * gpre_ref[...]).astype(BF16)
        n_ref[...] = n
        acc = jnp.zeros((tm, D), F32)
        for s in range(NS):
            gate = _dot_nt(n, wg_ref[s])
            up = _dot_nt(n, wu_ref[s])
            gate_ref[s] = gate.astype(BF16)
            up_ref[s] = up.astype(BF16)
            a = (gate * _sigmoid(gate) * up).astype(BF16)
            acc = acc + _dot(a, wd_ref[s])
        f_ref[...] = acc
        r2 = lax.rsqrt(jnp.mean(acc * acc, axis=-1, keepdims=True) + RMS_EPS)
        out = hh + 0.5 * (acc * r2 * gpost_ref[...])
        hout_ref[...] = out
        if with_loss:
            e = out - tgt_ref[...]
            dy_ref[...] = e * (1.0 / D)
            loss_ref[...] = jnp.full((1, 8, 128), 0.5 / D, F32) * jnp.sum(e * e)

    in_specs = [_rows(tm, D), _resident((1, D)), _part(wffn, FS, 0), _part(wffn, FS, 1), _part(wffn, FS, 2),
                _resident((1, D))]
    args = [h, g_pre, wffn, wffn, wffn, g_post]
    out_shape = [jax.ShapeDtypeStruct((T, D), F32), jax.ShapeDtypeStruct((T, D), BF16),
                 jax.ShapeDtypeStruct((NS, T, FS), BF16), jax.ShapeDtypeStruct((NS, T, FS), BF16),
                 jax.ShapeDtypeStruct((T, D), F32)]
    seg = pl.BlockSpec((NS, tm, FS), lambda i: (0, i, 0))
    out_specs = [_rows(tm, D), _rows(tm, D), seg, seg, _rows(tm, D)]
    if with_loss:
        in_specs.append(_rows(tm, D))
        args.append(target)
        out_shape += [jax.ShapeDtypeStruct((T, D), F32), jax.ShapeDtypeStruct((nt, 8, 128), F32)]
        out_specs += [_rows(tm, D), pl.BlockSpec((1, 8, 128), lambda i: (i, 0, 0))]
    return _call(body, grid=(nt,), in_specs=in_specs, args=args, out_specs=out_specs, out_shape=out_shape,
                 sem=("parallel",), name=name, ride=ride)


def ffn_bwd(dout, h, f, gate, up, g_pre, g_post, wffn, *, name, tm=256, ride=None):
    T, D = h.shape
    NS, FS = N_SHARD, wffn.shape[1] // 3
    nt = T // tm

    def body(dout_ref, h_ref, f_ref, gate_ref, up_ref, gpre_ref, gpost_ref, wg_ref, wu_ref, wd_ref,
             dh_ref, df_ref, a_ref, dgate_ref, dup_ref, dgpre_ref, dgpost_ref):
        @pl.when(pl.program_id(0) == 0)
        def _():
            dgpre_ref[...] = jnp.zeros_like(dgpre_ref)
            dgpost_ref[...] = jnp.zeros_like(dgpost_ref)

        do = dout_ref[...]
        ff = f_ref[...]
        d_fn = 0.5 * do
        r2 = lax.rsqrt(jnp.mean(ff * ff, axis=-1, keepdims=True) + RMS_EPS)
        dgpost_ref[...] += jnp.sum(d_fn * ff * r2, axis=0, keepdims=True)
        t = d_fn * gpost_ref[...]
        df = r2 * t - ff * (r2 * r2 * r2 * jnp.mean(t * ff, axis=-1, keepdims=True))
        dfb = df.astype(BF16)
        df_ref[...] = dfb
        dn = jnp.zeros((tm, D), F32)
        for s in range(NS):
            da = _dot_nt(dfb, wd_ref[s])
            g = gate_ref[s].astype(F32)
            u = up_ref[s].astype(F32)
            sg = _sigmoid(g)
            silu = g * sg
            a_ref[s] = (silu * u).astype(BF16)
            dgt = (da * u * (sg * (1.0 + g * (1.0 - sg)))).astype(BF16)
            dupv = (da * silu).astype(BF16)
            dgate_ref[s] = dgt
            dup_ref[s] = dupv
            dn = dn + _dot(dgt, wg_ref[s]) + _dot(dupv, wu_ref[s])
        hh = h_ref[...]
        r1 = lax.rsqrt(jnp.mean(hh * hh, axis=-1, keepdims=True) + RMS_EPS)
        dgpre_ref[...] += jnp.sum(dn * hh * r1, axis=0, keepdims=True)
        t = dn * gpre_ref[...]
        dh_ref[...] = do + r1 * t - hh * (r1 * r1 * r1 * jnp.mean(t * hh, axis=-1, keepdims=True))

    seg = pl.BlockSpec((NS, tm, FS), lambda i: (0, i, 0))
    acc = pl.BlockSpec((1, D), lambda i: (0, 0))
    return _call(
        body, grid=(nt,),
        in_specs=[_rows(tm, D), _rows(tm, D), _rows(tm, D), seg, seg, _resident((1, D)), _resident((1, D)),
                  _part(wffn, FS, 0), _part(wffn, FS, 1), _part(wffn, FS, 2)],
        args=[dout, h, f, gate, up, g_pre, g_post, wffn, wffn, wffn],
        out_specs=[_rows(tm, D), _rows(tm, D), seg, seg, seg, acc, acc],
        out_shape=[jax.ShapeDtypeStruct((T, D), F32), jax.ShapeDtypeStruct((T, D), BF16),
                   jax.ShapeDtypeStruct((NS, T, FS), BF16), jax.ShapeDtypeStruct((NS, T, FS), BF16),
                   jax.ShapeDtypeStruct((NS, T, FS), BF16),
                   jax.ShapeDtypeStruct((1, D), F32), jax.ShapeDtypeStruct((1, D), F32)],
        sem=("arbitrary",), name=name, ride=ride)


def mm_tn(a, g, *, name, tt=4096, tn=None, a_cols=None, into=None, ride=None):
    Ba, T, _ = a.shape
    Bg, _, N = g.shape
    B, K = a_cols if a_cols else (max(Ba, Bg), a.shape[2])
    tn = N if tn is None else tn
    tt = min(tt, T)
    nsteps = T // tt

    def body(*refs):
        a_ref, g_ref, o_ref = refs[0], refs[1], refs[-1]

        @pl.when(pl.program_id(2) == 0)
        def _():
            o_ref[...] = jnp.zeros_like(o_ref)

        o_ref[0] += _dot_tn(a_ref[0], g_ref[0].astype(BF16))

    if a_cols:
        a_map = lambda b, j, t: (0, t, b)
    else:
        a_map = (lambda b, j, t: (b, t, 0)) if Ba > 1 else (lambda b, j, t: (0, t, 0))
    in_specs = [pl.BlockSpec((1, tt, K), a_map),
                pl.BlockSpec((1, tt, tn), (lambda b, j, t: (b, t, j)) if Bg > 1 else (lambda b, j, t: (0, t, j)))]
    args = [a, g]
    if into is None:
        out_shape, part, aliases = jax.ShapeDtypeStruct((B, K, N), F32), 0, {}
    else:
        buf, part = into
        out_shape, aliases = jax.ShapeDtypeStruct(buf.shape, F32), {2: 0}
        in_specs.append(ANY)
        args.append(buf)
    out_spec = pl.BlockSpec((1, K, tn), lambda b, j, t: (b, part, j))
    if ride is None:
        return pl.pallas_call(
            body, grid=(B, N // tn, nsteps), in_specs=in_specs, out_specs=out_spec,
            out_shape=out_shape, input_output_aliases=aliases,
            compiler_params=_cparams("parallel", "parallel", "arbitrary"), name=name)(*args)
    (res,), exchanged = _call(body, grid=(B, N // tn, nsteps), in_specs=in_specs, args=args, out_specs=[out_spec],
                              out_shape=[out_shape], aliases=aliases, sem=("parallel", "parallel", "arbitrary"),
                              name=name, ride=ride)
    return res, exchanged


def mix_in_fwd(h, g, w_gz, w_xbc, w_dtT, w_qkv, *, name, tm=256, ride=None):
    T, D = h.shape
    nt = T // tm
    CB = 1024

    def body(h_ref, g_ref, wgz_ref, wxbc_ref, wdtT_ref, wqkv_ref,
             u_ref, gates_ref, z_ref, xbc_ref, dtT_ref, q_ref, k_ref, v_ref):
        hh = h_ref[...]
        r = lax.rsqrt(jnp.mean(hh * hh, axis=-1, keepdims=True) + RMS_EPS)
        u = (hh * r * g_ref[...]).astype(BF16)
        u_ref[...] = u
        for cb in range(0, 2048, CB):
            gates_ref[:, cb:cb + CB] = _dot(u, wgz_ref[:, cb:cb + CB]).astype(BF16)
            z_ref[:, cb:cb + CB] = _dot(u, wgz_ref[:, 2048 + cb:2048 + cb + CB])
        for cb in range(0, SSM_CONV_DIM, CB):
            xbc_ref[:, cb:cb + CB] = _dot(u, wxbc_ref[:, cb:cb + CB])
        dtT_ref[...] = _dot_nt(wdtT_ref[...], u)
        q_ref[...] = (_dot(u, wqkv_ref[:, 0:ATTN_Q_DIM]) * ATTN_SCALE).astype(BF16)
        k_ref[...] = _dot(u, wqkv_ref[:, ATTN_Q_DIM:ATTN_Q_DIM + ATTN_KV_DIM]).astype(BF16)
        v_ref[...] = _dot(u, wqkv_ref[:, ATTN_Q_DIM + ATTN_KV_DIM:]).astype(BF16)

    sds = jax.ShapeDtypeStruct
    return _call(
        body, grid=(nt,),
        in_specs=[_rows(tm, D), _resident((1, D)), _resident(w_gz.shape), _resident(w_xbc.shape),
                  _resident(w_dtT.shape), _resident(w_qkv.shape)],
        args=[h, g, w_gz, w_xbc, w_dtT, w_qkv],
        out_specs=[_rows(tm, D), _rows(tm, 2048), _rows(tm, 2048), _rows(tm, SSM_CONV_DIM),
                   pl.BlockSpec((SSM_HEADS, tm), lambda i: (0, i)),
                   _rows(tm, ATTN_Q_DIM), _rows(tm, ATTN_KV_DIM), _rows(tm, ATTN_KV_DIM)],
        out_shape=[sds((T, D), BF16), sds((T, 2048), BF16), sds((T, 2048), F32), sds((T, SSM_CONV_DIM), F32),
                   sds((SSM_HEADS, T), F32),
                   sds((T, ATTN_Q_DIM), BF16), sds((T, ATTN_KV_DIM), BF16), sds((T, ATTN_KV_DIM), BF16)],
        sem=("parallel",), name=name, ride=ride)


def _softplus(x):
    return jnp.maximum(x, 0.0) + jnp.log(1.0 + jnp.exp(-jnp.abs(x)))


def _iota(shape, axis):
    return lax.broadcasted_iota(jnp.int32, shape, axis)


def _attn_specs(nb):
    BLK = ATTN_BLOCK

    def specs(last):
        def cur(b, n):
            return b * nb + (n if last is None else jnp.minimum(n, nb - 1))

        def prev(b, n):
            return b * nb + jnp.maximum((n if last is None else jnp.minimum(n, nb - 1)) - 1, 0)
        return cur, prev
    return specs


MASKED = -1e30
ATTN_SCALE = ATTN_HEAD_DIM ** -0.5


def attn_window():
    i = np.arange(ATTN_BLOCK)[:, None]
    j = np.arange(2 * ATTN_BLOCK)[None, :]
    return (j > i) & (j <= i + ATTN_BLOCK)


def _attn_group(kk, q_ref, bias_ref, sink_ref):
    BLK, HD = ATTN_BLOCK, ATTN_HEAD_DIM
    heads = range(ATTN_REP * kk, ATTN_REP * (kk + 1))
    qg = jnp.concatenate([q_ref[:, HD * hd:HD * (hd + 1)] for hd in heads], axis=0)
    bias_p = jnp.concatenate([bias_ref[hd, :, 0:BLK] for hd in heads], axis=0)
    bias_c = jnp.concatenate([bias_ref[hd, :, BLK:2 * BLK] for hd in heads], axis=0)
    sink = jnp.concatenate([jnp.broadcast_to(sink_ref[0:1, hd:hd + 1], (BLK, 1)) for hd in heads], axis=0)
    return qg, bias_p, bias_c, sink


def attn_bias(table_t, onehot, *, name):
    def body(t_ref, f_ref, o_ref):
        o_ref[...] = _dot_hi(t_ref[...], f_ref[...])
    return pl.pallas_call(body, out_shape=jax.ShapeDtypeStruct((ATTN_Q_HEADS, onehot.shape[1]), F32),
                          compiler_params=_cparams(), name=name)(table_t, onehot)


def attn_bias_bwd(dbias, onehot, *, name):
    def body(d_ref, f_ref, o_ref):
        o_ref[...] = lax.dot_general(d_ref[...], f_ref[...], (((1,), (1,)), ((), ())), preferred_element_type=F32,
                                     precision=lax.Precision.HIGHEST)
    return pl.pallas_call(body, out_shape=jax.ShapeDtypeStruct((ATTN_Q_HEADS, REL_BUCKETS), F32),
                          compiler_params=_cparams(), name=name)(dbias, onehot)


def attn_fwd(q, k, v, bias, sinks, *, batch, name):
    T = q.shape[0]
    BLK, HD = ATTN_BLOCK, ATTN_HEAD_DIM
    nb = T // batch // BLK
    cur, prev = _attn_specs(nb)(None)

    def body(q_ref, kc_ref, kp_ref, vc_ref, vp_ref, bias_ref, sink_ref, o_ref, lse_ref):
        n = pl.program_id(1)
        for kk in range(ATTN_KV_HEADS):
            ks = slice(HD * kk, HD * (kk + 1))
            kc, kp, vc, vp = kc_ref[:, ks], kp_ref[:, ks], vc_ref[:, ks], vp_ref[:, ks]
            qg, bias_p, bias_c, sink = _attn_group(kk, q_ref, bias_ref, sink_ref)
            lp = jnp.where(n > 0, _dot_nt(qg, kp) + bias_p, MASKED)
            lc = _dot_nt(qg, kc) + bias_c
            mx = jnp.maximum(jnp.max(jnp.maximum(lp, lc), axis=-1, keepdims=True), sink)
            pp = jnp.exp(lp - mx)
            pc = jnp.exp(lc - mx)
            den = jnp.sum(pp + pc, axis=-1, keepdims=True) + jnp.exp(sink - mx)
            o = ((_dot(pp.astype(BF16), vp) + _dot(pc.astype(BF16), vc)) * (1.0 / den)).astype(BF16)
            lse = mx + jnp.log(den)
            for r in range(ATTN_REP):
                hd = ATTN_REP * kk + r
                o_ref[:, HD * hd:HD * (hd + 1)] = o[BLK * r:BLK * (r + 1)]
                lse_ref[:, hd:hd + 1] = lse[BLK * r:BLK * (r + 1)]

    sds = jax.ShapeDtypeStruct
    return pl.pallas_call(
        body, grid=(batch, nb),
        in_specs=[pl.BlockSpec((BLK, ATTN_Q_DIM), lambda b, n: (cur(b, n), 0)),
                  pl.BlockSpec((BLK, ATTN_KV_DIM), lambda b, n: (cur(b, n), 0)),
                  pl.BlockSpec((BLK, ATTN_KV_DIM), lambda b, n: (prev(b, n), 0)),
                  pl.BlockSpec((BLK, ATTN_KV_DIM), lambda b, n: (cur(b, n), 0)),
                  pl.BlockSpec((BLK, ATTN_KV_DIM), lambda b, n: (prev(b, n), 0)),
                  pl.BlockSpec((ATTN_Q_HEADS, BLK, 2 * BLK), lambda b, n: (0, 0, 0)),
                  pl.BlockSpec((1, ATTN_Q_HEADS), lambda b, n: (0, 0))],
        out_specs=[pl.BlockSpec((BLK, ATTN_Q_DIM), lambda b, n: (cur(b, n), 0)),
                   pl.BlockSpec((BLK, ATTN_Q_HEADS), lambda b, n: (cur(b, n), 0))],
        out_shape=[sds((T, ATTN_Q_DIM), BF16), sds((T, ATTN_Q_HEADS), F32)],
        compiler_params=_cparams("parallel", "parallel"), name=name)(q, k, k, v, v, bias, sinks)


def _proj_specs(wmix):
    return [_part(wmix, 512, 0), _part(wmix, 256, 2), _part(wmix, 256, 3)]


def _natural(w_ref):
    return w_ref[...].reshape(-1, w_ref.shape[2])


def mix_out_fwd(ys, o, gates, h, wmix, g_post, *, name, tm=512):
    T, D = h.shape
    nt = T // tm

    def body(ys_ref, o_ref, gates_ref, h_ref, wssm_ref, wattn_ref, wout_ref, g_ref,
             hout_ref, yssm_ref, yattn_ref, mix_ref, merged_ref):
        y_ssm = _dot(ys_ref[...], _natural(wssm_ref))
        y_attn = _dot(o_ref[...], _natural(wattn_ref))
        yssm_ref[...] = y_ssm.astype(BF16)
        yattn_ref[...] = y_attn.astype(BF16)
        merged = (_sigmoid(gates_ref[:, 0:D].astype(F32)) * y_ssm
                  + _sigmoid(gates_ref[:, D:2 * D].astype(F32)) * y_attn).astype(BF16)
        merged_ref[...] = merged
        mix = _dot(merged, _natural(wout_ref))
        mix_ref[...] = mix.astype(BF16)
        r = lax.rsqrt(jnp.mean(mix * mix, axis=-1, keepdims=True) + RMS_EPS)
        hout_ref[...] = h_ref[...] + mix * r * g_ref[...]

    sds = jax.ShapeDtypeStruct
    return pl.pallas_call(
        body, grid=(nt,),
        in_specs=[_rows(tm, SSM_D_INNER), _rows(tm, ATTN_Q_DIM), _rows(tm, 2 * D), _rows(tm, D),
                  *_proj_specs(wmix), _resident((1, D))],
        out_specs=[_rows(tm, D)] * 5,
        out_shape=[sds((T, D), F32), sds((T, D), BF16), sds((T, D), BF16), sds((T, D), BF16), sds((T, D), BF16)],
        compiler_params=_cparams("parallel"), name=name)(ys, o, gates, h, wmix, wmix, wmix, g_post)


def mix_out_bwd(dh, mix, y_ssm, y_attn, gates, wmix, g_post, *, name, tm=256, ride=None):
    T, D = dh.shape
    nt = T // tm

    def body(dh_ref, mix_ref, yssm_ref, yattn_ref, gates_ref, wssm_ref, wattn_ref, wout_ref, g_ref,
             dmix_ref, dyssm_ref, dyattn_ref, dgates_ref, dys_ref, do_ref, dg_ref):
        @pl.when(pl.program_id(0) == 0)
        def _():
            dg_ref[...] = jnp.zeros_like(dg_ref)

        do = dh_ref[...]
        mix = mix_ref[...].astype(F32)
        r = lax.rsqrt(jnp.mean(mix * mix, axis=-1, keepdims=True) + RMS_EPS)
        dg_ref[...] += jnp.sum(do * mix * r, axis=0, keepdims=True)
        t = do * g_ref[...]
        dmix = (r * t - mix * (r * r * r * jnp.mean(t * mix, axis=-1, keepdims=True))).astype(BF16)
        dmix_ref[...] = dmix
        dmerged = _dot_nt(dmix, _natural(wout_ref))
        s1 = _sigmoid(gates_ref[:, 0:D].astype(F32))
        s2 = _sigmoid(gates_ref[:, D:2 * D].astype(F32))
        dyssm = (dmerged * s1).astype(BF16)
        dyattn = (dmerged * s2).astype(BF16)
        dyssm_ref[...] = dyssm
        dyattn_ref[...] = dyattn
        dgates_ref[:, 0:D] = (dmerged * yssm_ref[...].astype(F32) * (s1 * (1.0 - s1))).astype(BF16)
        dgates_ref[:, D:2 * D] = (dmerged * yattn_ref[...].astype(F32) * (s2 * (1.0 - s2))).astype(BF16)
        dys_ref[...] = _dot_nt(dyssm, _natural(wssm_ref))
        do_ref[...] = _dot_nt(dyattn, _natural(wattn_ref)).astype(BF16)

    sds = jax.ShapeDtypeStruct
    return _call(
        body, grid=(nt,),
        in_specs=[_rows(tm, D), _rows(tm, D), _rows(tm, D), _rows(tm, D), _rows(tm, 2 * D),
                  *_proj_specs(wmix), _resident((1, D))],
        args=[dh, mix, y_ssm, y_attn, gates, wmix, wmix, wmix, g_post],
        out_specs=[_rows(tm, D), _rows(tm, D), _rows(tm, D), _rows(tm, 2 * D), _rows(tm, SSM_D_INNER),
                   _rows(tm, ATTN_Q_DIM), pl.BlockSpec((1, D), lambda i: (0, 0))],
        out_shape=[sds((T, D), BF16), sds((T, D), BF16), sds((T, D), BF16), sds((T, 2 * D), BF16),
                   sds((T, SSM_D_INNER), F32), sds((T, ATTN_Q_DIM), BF16), sds((1, D), F32)],
        sem=("arbitrary",), name=name, ride=ride)


def attn_bwd(q, k, v, o, do, lse, bias, sinks, *, batch, name):
    T = q.shape[0]
    BLK, HD = ATTN_BLOCK, ATTN_HEAD_DIM
    nb = T // batch // BLK
    cur, prev = _attn_specs(nb)(nb)
    scale = HD ** -0.5

    def body(q_ref, kc_ref, kp_ref, vc_ref, vp_ref, o_ref, do_ref, lse_ref, bias_ref, sink_ref,
             dq_ref, dk_ref, dv_ref, dbias_ref, dsink_ref, ck, cv):
        b = pl.program_id(0)
        n = pl.program_id(1)

        @pl.when(jnp.logical_and(b == 0, n == 0))
        def _():
            dbias_ref[...] = jnp.zeros_like(dbias_ref)
            dsink_ref[...] = jnp.zeros_like(dsink_ref)

        @pl.when(n == 0)
        def _():
            ck[...] = jnp.zeros_like(ck)
            cv[...] = jnp.zeros_like(cv)

        @pl.when(n == nb)
        def _():
            dk_ref[...] = ck[...].astype(BF16)
            dv_ref[...] = cv[...].astype(BF16)

        @pl.when(n < nb)
        def _():
            lane16 = _iota((1, ATTN_Q_HEADS), 1)
            dsink = jnp.zeros((1, ATTN_Q_HEADS), F32)
            for kk in range(ATTN_KV_HEADS):
                ks = slice(HD * kk, HD * (kk + 1))
                kc, kp, vc, vp = kc_ref[:, ks], kp_ref[:, ks], vc_ref[:, ks], vp_ref[:, ks]
                heads = range(ATTN_REP * kk, ATTN_REP * (kk + 1))
                qg, bias_p, bias_c, sink = _attn_group(kk, q_ref, bias_ref, sink_ref)
                dog = jnp.concatenate([do_ref[:, HD * hd:HD * (hd + 1)] for hd in heads], axis=0)
                og = jnp.concatenate([o_ref[:, HD * hd:HD * (hd + 1)] for hd in heads], axis=0)
                lse = jnp.concatenate([lse_ref[:, hd:hd + 1] for hd in heads], axis=0)
                lp = jnp.where(n > 0, _dot_nt(qg, kp) + bias_p, MASKED)
                lc = _dot_nt(qg, kc) + bias_c
                pp = jnp.exp(lp - lse)
                pc = jnp.exp(lc - lse)
                delta = jnp.sum(dog.astype(F32) * og.astype(F32), axis=-1, keepdims=True)
                dlp = pp * (_dot_nt(dog, vp) - delta)
                dlc = pc * (_dot_nt(dog, vc) - delta)
                sd = jnp.exp(sink - lse) * delta
                dlpb = dlp.astype(BF16)
                dlcb = dlc.astype(BF16)
                dqg = ((_dot(dlpb, kp) + _dot(dlcb, kc)) * scale).astype(BF16)
                for r, hd in enumerate(heads):
                    rows = slice(BLK * r, BLK * (r + 1))
                    dsink = dsink + jnp.where(lane16 == hd, -jnp.sum(sd[rows], axis=0, keepdims=True), 0.0)
                    dbias_ref[hd, :, 0:BLK] += dlp[rows]
                    dbias_ref[hd, :, BLK:2 * BLK] += dlc[rows]
                    dq_ref[:, HD * hd:HD * (hd + 1)] = dqg[rows]
                dk_ref[:, ks] = (ck[:, ks] + _dot_tn(dlpb, qg)).astype(BF16)
                dv_ref[:, ks] = (cv[:, ks] + _dot_tn(pp.astype(BF16), dog)).astype(BF16)
                ck[:, ks] = _dot_tn(dlcb, qg)
                cv[:, ks] = _dot_tn(pc.astype(BF16), dog)
            dsink_ref[...] += dsink

    sds = jax.ShapeDtypeStruct
    qspec = pl.BlockSpec((BLK, ATTN_Q_DIM), lambda b, n: (cur(b, n), 0))
    cspec = pl.BlockSpec((BLK, ATTN_KV_DIM), lambda b, n: (cur(b, n), 0))
    pspec = pl.BlockSpec((BLK, ATTN_KV_DIM), lambda b, n: (prev(b, n), 0))
    late = pl.BlockSpec((BLK, ATTN_KV_DIM), lambda b, n: (b * nb + jnp.maximum(n - 1, 0), 0))
    return pl.pallas_call(
        body, grid=(batch, nb + 1),
        in_specs=[qspec, cspec, pspec, cspec, pspec, qspec, qspec,
                  pl.BlockSpec((BLK, ATTN_Q_HEADS), lambda b, n: (cur(b, n), 0)),
                  pl.BlockSpec((ATTN_Q_HEADS, BLK, 2 * BLK), lambda b, n: (0, 0, 0)),
                  pl.BlockSpec((1, ATTN_Q_HEADS), lambda b, n: (0, 0))],
        out_specs=[qspec, late, late,
                   pl.BlockSpec((ATTN_Q_HEADS, BLK, 2 * BLK), lambda b, n: (0, 0, 0)),
                   pl.BlockSpec((1, ATTN_Q_HEADS), lambda b, n: (0, 0))],
        out_shape=[sds((T, ATTN_Q_DIM), BF16), sds((T, ATTN_KV_DIM), BF16), sds((T, ATTN_KV_DIM), BF16),
                   sds((ATTN_Q_HEADS, BLK, 2 * BLK), F32), sds((1, ATTN_Q_HEADS), F32)],
        scratch_shapes=[pltpu.VMEM((BLK, ATTN_KV_DIM), F32), pltpu.VMEM((BLK, ATTN_KV_DIM), F32)],
        compiler_params=_cparams("arbitrary", "arbitrary"), name=name)(q, k, k, v, v, o, do, lse, bias, sinks)


def _conv_bwd(dxc, pre, xp_ref, w_ref, carry_ref, acc_ref, dp_ref, g, last):
    Q = SSM_CHUNK
    sg = _sigmoid(pre)
    dpre = dxc * (sg * (1.0 + pre * (1.0 - sg)))
    dp_ref[0:Q, :] = dpre
    dp_ref[Q:Q + HALO, :] = carry_ref[g]
    carry_ref[g] = dpre[0:HALO, :]
    rows = [jnp.sum(dpre * xp_ref[pl.ds(HALO - 3 + k, Q), :], axis=0, keepdims=True) for k in range(SSM_CONV)]
    rows.append(jnp.sum(dpre, axis=0, keepdims=True))
    rows.append(jnp.zeros((HALO - SSM_CONV - 1, dpre.shape[1]), F32))
    acc_ref[g] += jnp.concatenate(rows, axis=0)
    dx = w_ref[3:4, :] * dpre
    for k in range(SSM_CONV - 1):
        dx = dx + w_ref[k:k + 1, :] * dp_ref[pl.ds(3 - k, Q), :]
    return dx


def mix_in_bwd(dh, h, g, dgates, dz, dxs, dbm, dcm, ddtT, dq, dk, dv, w_gz, w_xbc, w_dtT, w_qkv, *, name, tm=512,
               ride=None):
    T, D = h.shape
    nt = T // tm
    GN = SSM_GROUPS * SSM_STATE

    def body(dh_ref, h_ref, g_ref, dgates_ref, dz_ref, dxs_ref, dbm_ref, dcm_ref, ddt_ref, dq_ref, dk_ref, dv_ref,
             wgz_ref, wxbc_ref, wdt_ref, wqkv_ref, dhin_ref, dg_ref):
        @pl.when(pl.program_id(0) == 0)
        def _():
            dg_ref[...] = jnp.zeros_like(dg_ref)

        du = _dot_nt(dgates_ref[...], wgz_ref[:, 0:2048])
        du = du + _dot_nt(dz_ref[...], wgz_ref[:, 2048:4096])
        du = du + _dot_nt(dxs_ref[...], wxbc_ref[:, 0:SSM_D_INNER])
        du = du + _dot_nt(dbm_ref[...], wxbc_ref[:, SSM_D_INNER:SSM_D_INNER + GN])
        du = du + _dot_nt(dcm_ref[...], wxbc_ref[:, SSM_D_INNER + GN:])
        du = du + _dot_tn(ddt_ref[...].astype(BF16), wdt_ref[...])
        du = du + _dot_nt(dq_ref[...], wqkv_ref[:, 0:ATTN_Q_DIM])
        du = du + _dot_nt(dk_ref[...], wqkv_ref[:, ATTN_Q_DIM:ATTN_Q_DIM + ATTN_KV_DIM])
        du = du + _dot_nt(dv_ref[...], wqkv_ref[:, ATTN_Q_DIM + ATTN_KV_DIM:])
        hh = h_ref[...]
        r = lax.rsqrt(jnp.mean(hh * hh, axis=-1, keepdims=True) + RMS_EPS)
        dg_ref[...] += jnp.sum(du * hh * r, axis=0, keepdims=True)
        t = du * g_ref[...]
        dhin_ref[...] = dh_ref[...] + r * t - hh * (r * r * r * jnp.mean(t * hh, axis=-1, keepdims=True))

    sds = jax.ShapeDtypeStruct
    return _call(
        body, grid=(nt,),
        in_specs=[_rows(tm, D), _rows(tm, D), _resident((1, D)), _rows(tm, 2048), _rows(tm, 2048), _rows(tm, SSM_D_INNER),
                  _rows(tm, GN), _rows(tm, GN), pl.BlockSpec((SSM_HEADS, tm), lambda i: (0, i)),
                  _rows(tm, ATTN_Q_DIM), _rows(tm, ATTN_KV_DIM),
                  _rows(tm, ATTN_KV_DIM), _resident(w_gz.shape), _resident(w_xbc.shape), _resident(w_dtT.shape),
                  _resident(w_qkv.shape)],
        args=[dh, h, g, dgates, dz, dxs, dbm, dcm, ddtT, dq, dk, dv, w_gz, w_xbc, w_dtT, w_qkv],
        out_specs=[_rows(tm, D), pl.BlockSpec((1, D), lambda i: (0, 0))],
        out_shape=[sds((T, D), F32), sds((1, D), F32)],
        sem=("arbitrary",), name=name, ride=ride)


PAIRS = SSM_HPG // 2
PW = 2 * SSM_HEAD_DIM


def ssd_scalars(dt_rawT, dt_bias, a_log, *, name, chunks=8):
    H, T = dt_rawT.shape
    Q, G, HPG = SSM_CHUNK, SSM_GROUPS, SSM_HPG
    chunks = math.gcd(chunks, T // Q)
    span = Q * chunks

    def body(dtT_ref, dtb_ref, alog_ref, cols_ref, acsT_ref, dec_ref):
        aT = -jnp.exp(alog_ref[...])
        triT = (_iota((Q, Q), 0) <= _iota((Q, Q), 1)).astype(F32)
        for j in range(chunks):
            at = slice(Q * j, Q * (j + 1))
            dtT = _softplus(dtT_ref[:, at] + dtb_ref[...])
            acsT = _dot_hi(dtT * aT, triT)
            lastT = acsT[:, Q - 1:Q]
            acsT_ref[:, at] = acsT
            dec_ref[j] = jnp.exp(lastT)
            parts = [dtT, acsT, jnp.exp(lastT - acsT), jnp.exp(acsT)]
            colsT = jnp.concatenate([q[HPG * g:HPG * (g + 1)] for g in range(G) for q in parts], axis=0).T
            for g in range(G):
                cols_ref[g, at, :] = colsT[:, 4 * HPG * g:4 * HPG * (g + 1)]

    sds = jax.ShapeDtypeStruct
    return pl.pallas_call(
        body, grid=(T // span,),
        in_specs=[pl.BlockSpec((H, span), lambda i: (0, i)), pl.BlockSpec((H, 1), lambda i: (0, 0)),
                  pl.BlockSpec((H, 1), lambda i: (0, 0))],
        out_specs=[pl.BlockSpec((G, span, 4 * HPG), lambda i: (0, i, 0)), pl.BlockSpec((H, span), lambda i: (0, i)),
                   pl.BlockSpec((chunks, H, 1), lambda i: (i, 0, 0))],
        out_shape=[sds((G, T, 4 * HPG), F32), sds((H, T), F32), sds((T // Q, H, 1), F32)],
        compiler_params=_cparams("parallel"), name=name)(dt_rawT, dt_bias, a_log)


CONV_XS, CONV_BM, CONV_CM = slice(0, SSM_GW), slice(SSM_GW, SSM_GW + SSM_STATE), slice(SSM_GW + SSM_STATE, SSM_GW + 2 * SSM_STATE)


def _pair_cols(cols, base, p, lo):
    k = base + 2 * p
    return jnp.where(lo, cols[:, k:k + 1], cols[:, k + 1:k + 2])


def _pair_row(colT, p, lo_row):
    return jnp.where(lo_row, colT[2 * p:2 * p + 1, :], colT[2 * p + 1:2 * p + 2, :])


def _pair_operands(pp, p, s, causal, lo, xb):
    zero = jnp.zeros_like(xb)
    rhs = jnp.concatenate([jnp.where(lo, xb, zero), jnp.where(lo, zero, xb)], axis=0)
    ls, ms = [], []
    for k in (2 * p, 2 * p + 1):
        seg = pp["cols"][:, 8 + k:9 + k] - pp["acsT"][k:k + 1, :]
        l = jnp.exp(jnp.where(causal, seg, -1e30))
        ls.append(l)
        ms.append(s * l)
    lhs = jnp.concatenate([m.astype(BF16) for m in ms], axis=1)
    return lhs, rhs, ls


def _group_cols(g):
    return (slice(SSM_GW * g, SSM_GW * (g + 1)),
            slice(SSM_D_INNER + SSM_STATE * g, SSM_D_INNER + SSM_STATE * (g + 1)),
            slice(SSM_D_INNER + SSM_STATE * (SSM_GROUPS + g), SSM_D_INNER + SSM_STATE * (SSM_GROUPS + g + 1)))


def _conv_pre(x, halo, w, b, xp_ref):
    Q = SSM_CHUNK
    xp_ref[0:HALO, :] = halo
    xp_ref[HALO:HALO + Q, :] = x
    pre = b + w[3:4, :] * x
    for k in range(SSM_CONV - 1):
        pre = pre + w[k:k + 1, :] * xp_ref[pl.ds(HALO - 3 + k, Q), :]
    return pre


def _ssd_prologue(g, first, xbc_ref, halo_ref, conv_ref, cols_ref, acsT_ref, dec_ref, xp_xs, xp_bm, xp_cm):
    cp = conv_ref[g]
    heads = slice(SSM_HPG * g, SSM_HPG * (g + 1))
    out, taps = {}, {}
    for n, at, pk, xp in zip(("xs", "bm", "cm"), _group_cols(g), (CONV_XS, CONV_BM, CONV_CM), (xp_xs, xp_bm, xp_cm)):
        taps[n] = (cp[:, pk], cp[SSM_CONV:SSM_CONV + 1, pk])
        halo = jnp.where(first, 0.0, halo_ref[:, at])
        pre = _conv_pre(xbc_ref[:, at], halo, *taps[n], xp.at[g])
        out["pre_" + n] = pre
        out[n] = pre * _sigmoid(pre)
    out.update(taps=taps, acsT=acsT_ref[heads, :], decayT=dec_ref[0, heads, :], cols=cols_ref[g])
    return out


def ssd_fwd(xbc, z, scalars, conv_pack, vec_pack, *, batch, name):
    T = xbc.shape[0]
    Q, GW, N, G = SSM_CHUNK, SSM_GW, SSM_STATE, SSM_GROUPS
    nc = T // batch // Q
    row = lambda b, c: b * nc + c
    hrow = lambda b, c: jnp.maximum(row(b, c) * (Q // HALO) - 1, 0)

    def body(xbc_ref, halo_ref, z_ref, conv_ref, cols_ref, acsT_ref, dec_ref, vec_ref, y_ref, ys_ref, st_ref,
             state, xp_xs, xp_bm, xp_cm):
        first = pl.program_id(1) == 0
        causal = _iota((Q, Q), 0) >= _iota((Q, Q), 1)
        lo = _iota((Q, PW), 1) < SSM_HEAD_DIM
        lo_row = _iota((1, PW), 1) < SSM_HEAD_DIM

        @pl.when(first)
        def _():
            state[...] = jnp.zeros_like(state)

        for g in range(G):
            cols = slice(GW * g, GW * (g + 1))
            pp = _ssd_prologue(g, first, xbc_ref, halo_ref, conv_ref, cols_ref, acsT_ref, dec_ref, xp_xs, xp_bm, xp_cm)
            xs = pp["xs"]
            bb = pp["bm"].astype(BF16)
            cb = pp["cm"].astype(BF16)
            s = _dot_nt(cb, bb)
            entering = state[g]
            st_ref[0, g] = entering
            wide = lambda base: jnp.concatenate([_pair_cols(pp["cols"], base, p, lo) for p in range(PAIRS)], axis=1)
            x = xs * wide(0)
            xb = x.astype(BF16)
            yd = []
            for p in range(PAIRS):
                lhs, rhs, _ = _pair_operands(pp, p, s, causal, lo, xb[:, PW * p:PW * (p + 1)])
                yd.append(_dot(lhs, rhs))
            y = jnp.concatenate(yd, axis=1) + _dot(cb, entering.astype(BF16)) * wide(24) + vec_ref[0:1, cols] * xs
            decay = jnp.concatenate([_pair_row(pp["decayT"], p, lo_row) for p in range(PAIRS)], axis=1)
            state[g] = entering * decay + _dot_tn(bb, (x * wide(16)).astype(BF16))
            y_ref[:, cols] = y
            zz = z_ref[:, cols]
            yg = y * (zz * _sigmoid(zz))
            rr = lax.rsqrt(jnp.mean(yg * yg, axis=-1, keepdims=True) + RMS_EPS)
            ys_ref[:, cols] = (yg * rr * vec_ref[1:2, cols]).astype(BF16)

    sds = jax.ShapeDtypeStruct
    cols3, acsT, decay = scalars
    return pl.pallas_call(
        body, grid=(batch, nc),
        in_specs=[pl.BlockSpec((Q, SSM_CONV_DIM), lambda b, c: (row(b, c), 0)),
                  pl.BlockSpec((HALO, SSM_CONV_DIM), lambda b, c: (hrow(b, c), 0)),
                  pl.BlockSpec((Q, SSM_D_INNER), lambda b, c: (row(b, c), 0)),
                  pl.BlockSpec((G, HALO, GW + 2 * N), lambda b, c: (0, 0, 0)),
                  pl.BlockSpec((G, Q, 4 * SSM_HPG), lambda b, c: (0, row(b, c), 0)),
                  pl.BlockSpec((SSM_HEADS, Q), lambda b, c: (0, row(b, c))),
                  pl.BlockSpec((1, SSM_HEADS, 1), lambda b, c: (row(b, c), 0, 0)),
                  pl.BlockSpec((2, SSM_D_INNER), lambda b, c: (0, 0))],
        out_specs=[pl.BlockSpec((Q, SSM_D_INNER), lambda b, c: (row(b, c), 0)),
                   pl.BlockSpec((Q, SSM_D_INNER), lambda b, c: (row(b, c), 0)),
                   pl.BlockSpec((1, G, N, GW), lambda b, c: (row(b, c), 0, 0, 0))],
        out_shape=[sds((T, SSM_D_INNER), F32), sds((T, SSM_D_INNER), BF16), sds((T // Q, G, N, GW), F32)],
        scratch_shapes=[pltpu.VMEM((G, N, GW), F32), pltpu.VMEM((G, HALO + Q, GW), F32),
                        pltpu.VMEM((G, HALO + Q, N), F32), pltpu.VMEM((G, HALO + Q, N), F32)],
        compiler_params=_cparams("arbitrary", "arbitrary"), name=name,
    )(xbc, xbc, z, conv_pack, cols3, acsT, decay, vec_pack)


def ssd_bwd(dys, y, xbc, z, dt_rawT, states, scalars, conv_pack, col_pack, vec_pack, *, batch, name, ride=None):
    T = xbc.shape[0]
    Q, GW, N, G, HPG = SSM_CHUNK, SSM_GW, SSM_STATE, SSM_GROUPS, SSM_HPG
    nc = T // batch // Q
    row = lambda b, c: b * nc + (nc - 1 - c)
    hrow = lambda b, c: jnp.maximum(row(b, c) * (Q // HALO) - 1, 0)

    def body(xbc_ref, halo_ref, z_ref, y_ref, dys_ref, dtT_ref, st_ref, conv_ref, cols_ref, acsT_ref, dec_ref, col_ref,
             vec_ref, dz_ref, dxs_ref, dbm_ref, dcm_ref, ddtT_ref, acc_xs, acc_bm, acc_cm, acc_head,
             dstate, xp_xs, xp_bm, xp_cm, dp_xs, dp_bm, dp_cm, cy_xs, cy_bm, cy_cm):
        b = pl.program_id(0)
        cr = pl.program_id(1)
        first = cr == nc - 1
        last = cr == 0

        @pl.when(jnp.logical_and(b == 0, cr == 0))
        def _():
            acc_xs[...] = jnp.zeros_like(acc_xs)
            acc_bm[...] = jnp.zeros_like(acc_bm)
            acc_cm[...] = jnp.zeros_like(acc_cm)
            acc_head[...] = jnp.zeros_like(acc_head)

        @pl.when(last)
        def _():
            dstate[...] = jnp.zeros_like(dstate)
            cy_xs[...] = jnp.zeros_like(cy_xs)
            cy_bm[...] = jnp.zeros_like(cy_bm)
            cy_cm[...] = jnp.zeros_like(cy_cm)

        causal = _iota((Q, Q), 0) >= _iota((Q, Q), 1)
        lo = _iota((Q, PW), 1) < SSM_HEAD_DIM
        lo_row = _iota((1, PW), 1) < SSM_HEAD_DIM
        tri = (_iota((Q, Q), 0) >= _iota((Q, Q), 1)).astype(F32)
        e8 = (_iota((HPG, GW), 0) == lax.shift_right_logical(_iota((HPG, GW), 1), 6)).astype(F32)
        seg_sum = lambda v: lax.dot_general(e8, v, (((1,), (1,)), ((), ())), preferred_element_type=F32,
                                            precision=lax.Precision.HIGHEST)
        lane = _iota((HPG, N), 1)
        for g in range(G):
            cols = slice(GW * g, GW * (g + 1))
            ncols = slice(N * g, N * (g + 1))
            heads = slice(HPG * g, HPG * (g + 1))
            pp = _ssd_prologue(g, first, xbc_ref, halo_ref, conv_ref, cols_ref, acsT_ref, dec_ref, xp_xs, xp_bm, xp_cm)
            xs, decayT = pp["xs"], pp["decayT"]
            dtrT = dtT_ref[heads, :] + col_ref[heads, 0:1]
            dtT = _softplus(dtrT)
            aT = -jnp.exp(col_ref[heads, 1:2])

            yv = y_ref[:, cols]
            zz = z_ref[:, cols]
            sz = _sigmoid(zz)
            silu_z = zz * sz
            yg = yv * silu_z
            rr = lax.rsqrt(jnp.mean(yg * yg, axis=-1, keepdims=True) + RMS_EPS)
            dys_v = dys_ref[:, cols]
            d_ng = jnp.sum(dys_v * yg * rr, axis=0, keepdims=True)
            t = dys_v * vec_ref[1:2, cols]
            dyg = rr * t - yg * (rr * rr * rr * jnp.mean(t * yg, axis=-1, keepdims=True))
            dy = dyg * silu_z
            dz_ref[:, cols] = (dyg * yv * (sz * (1.0 + zz * (1.0 - sz)))).astype(BF16)
            dsk = vec_ref[0:1, cols]
            d_dsk = jnp.sum(dy * xs, axis=0, keepdims=True)

            bb = pp["bm"].astype(BF16)
            cb = pp["cm"].astype(BF16)
            s = _dot_nt(cb, bb)
            wide = lambda base: jnp.concatenate([_pair_cols(pp["cols"], base, p, lo) for p in range(PAIRS)], axis=1)
            dt_x, w_x, e_x = wide(0), wide(16), wide(24)
            decay = jnp.concatenate([_pair_row(decayT, p, lo_row) for p in range(PAIRS)], axis=1)
            x = xs * dt_x
            xw = x * w_x
            xb = x.astype(BF16)
            dyb = dy.astype(BF16)
            dye = (dy * e_x).astype(BF16)
            hp = st_ref[0, g]
            hpb = hp.astype(BF16)
            dh = dstate[g]
            dhb = dh.astype(BF16)
            ds_acc = jnp.zeros((Q, Q), F32)
            yd, dxd = [], []
            for p in range(PAIRS):
                tile = slice(PW * p, PW * (p + 1))
                lhs, rhs, ls = _pair_operands(pp, p, s, causal, lo, xb[:, tile])
                dm = _dot_nt(dyb[:, tile], rhs)
                dxd2 = _dot_tn(lhs, dyb[:, tile])
                dxd.append(jnp.where(lo, dxd2[0:Q], dxd2[Q:2 * Q]))
                ds_acc = ds_acc + dm[:, 0:Q] * ls[0] + dm[:, Q:2 * Q] * ls[1]
                yd.append(_dot(lhs, rhs))
            yd = jnp.concatenate(yd, axis=1)
            dxd = jnp.concatenate(dxd, axis=1)
            dxw = _dot(bb, dhb)
            dx_full = dxd + dxw * w_x
            tw = dxw * xw
            q1 = dyb.astype(F32) * yd + dy * (_dot(cb, hpb) * e_x) - tw - xb.astype(F32) * dxd
            q2 = dx_full * xs
            dxs_v = dsk * dy + dx_full * dt_x
            dstate[g] = dh * decay + _dot_tn(cb, dye)
            dsb = ds_acc.astype(BF16)
            d_c = _dot_nt(dye, hpb) + _dot(dsb, bb)
            d_b = _dot_nt(xw.astype(BF16), dhb) + _dot_tn(dsb, cb)
            rowv = jnp.sum(dh * hp, axis=0, keepdims=True) * decay + jnp.sum(tw, axis=0, keepdims=True)
            last_terms = jnp.sum(e8 * rowv, axis=1, keepdims=True)
            dacsT = seg_sum(q1) + jnp.where(_iota((1, Q), 1) == Q - 1, 1.0, 0.0) * last_terms
            d_dtaT = _dot_hi(dacsT, tri)
            ddtT = d_dtaT * aT + seg_sum(q2)
            d_alog = jnp.sum(d_dtaT * dtT, axis=1, keepdims=True) * aT
            ddt_rawT = ddtT * _sigmoid(dtrT)
            ddtT_ref[heads, :] = ddt_rawT
            d_dtb = jnp.sum(ddt_rawT, axis=1, keepdims=True)
            acc_head[g] += jnp.where(lane == 0, d_dtb, 0.0) + jnp.where(lane == 1, d_alog, 0.0)
            taps = pp["taps"]
            dxs_ref[:, cols] = _conv_bwd(dxs_v, pp["pre_xs"], xp_xs.at[g], taps["xs"][0], cy_xs, acc_xs, dp_xs.at[g], g,
                                         last).astype(BF16)
            dbm_ref[:, ncols] = _conv_bwd(d_b, pp["pre_bm"], xp_bm.at[g], taps["bm"][0], cy_bm, acc_bm, dp_bm.at[g], g,
                                          last).astype(BF16)
            dcm_ref[:, ncols] = _conv_bwd(d_c, pp["pre_cm"], xp_cm.at[g], taps["cm"][0], cy_cm, acc_cm, dp_cm.at[g], g,
                                          last).astype(BF16)
            acc_xs[g, pl.ds(SSM_CONV + 1, 2), :] += jnp.concatenate([d_ng, d_dsk], axis=0)

    sds = jax.ShapeDtypeStruct
    cols3, acsT, decay3 = scalars
    full = lambda shape: pl.BlockSpec(shape, lambda b, c: (0,) * len(shape))
    wide_in = pl.BlockSpec((Q, SSM_D_INNER), lambda b, c: (row(b, c), 0))
    heads_in = pl.BlockSpec((SSM_HEADS, Q), lambda b, c: (0, row(b, c)))
    return _call(
        body, grid=(batch, nc),
        in_specs=[pl.BlockSpec((Q, SSM_CONV_DIM), lambda b, c: (row(b, c), 0)),
                  pl.BlockSpec((HALO, SSM_CONV_DIM), lambda b, c: (hrow(b, c), 0)),
                  wide_in, wide_in, wide_in, heads_in,
                  pl.BlockSpec((1, G, N, GW), lambda b, c: (row(b, c), 0, 0, 0)),
                  full((G, HALO, GW + 2 * N)),
                  pl.BlockSpec((G, Q, 4 * HPG), lambda b, c: (0, row(b, c), 0)),
                  heads_in,
                  pl.BlockSpec((1, SSM_HEADS, 1), lambda b, c: (row(b, c), 0, 0)),
                  full((SSM_HEADS, 2)), full((2, SSM_D_INNER))],
        args=[xbc, xbc, z, y, dys, dt_rawT, states, conv_pack, cols3, acsT, decay3, col_pack, vec_pack],
        out_specs=[wide_in, wide_in,
                   pl.BlockSpec((Q, G * N), lambda b, c: (row(b, c), 0)),
                   pl.BlockSpec((Q, G * N), lambda b, c: (row(b, c), 0)),
                   heads_in, full((G, HALO, GW)), full((G, HALO, N)), full((G, HALO, N)), full((G, HPG, N))],
        out_shape=[sds((T, SSM_D_INNER), BF16), sds((T, SSM_D_INNER), BF16), sds((T, G * N), BF16),
                   sds((T, G * N), BF16), sds((SSM_HEADS, T), F32),
                   sds((G, HALO, GW), F32), sds((G, HALO, N), F32), sds((G, HALO, N), F32), sds((G, HPG, N), F32)],
        scratch=[pltpu.VMEM((G, N, GW), F32),
                 pltpu.VMEM((G, HALO + Q, GW), F32), pltpu.VMEM((G, HALO + Q, N), F32), pltpu.VMEM((G, HALO + Q, N), F32),
                 pltpu.VMEM((G, Q + HALO, GW), F32), pltpu.VMEM((G, Q + HALO, N), F32), pltpu.VMEM((G, Q + HALO, N), F32),
                 pltpu.VMEM((G, HALO, GW), F32), pltpu.VMEM((G, HALO, N), F32), pltpu.VMEM((G, HALO, N), F32)],
        sem=("arbitrary", "arbitrary"), name=name, ride=ride)


def mm_rows(a, b, *, name, tt=2048):
    M, T = a.shape
    N = b.shape[1]
    tt = min(tt, T)

    def body(a_ref, b_ref, o_ref):
        @pl.when(pl.program_id(0) == 0)
        def _():
            o_ref[...] = jnp.zeros_like(o_ref)

        o_ref[...] += _dot(a_ref[...].astype(BF16), b_ref[...])

    return pl.pallas_call(
        body, grid=(T // tt,),
        in_specs=[pl.BlockSpec((M, tt), lambda t: (0, t)), pl.BlockSpec((tt, N), lambda t: (t, 0))],
        out_specs=pl.BlockSpec((M, N), lambda t: (0, 0)), out_shape=jax.ShapeDtypeStruct((M, N), F32),
        compiler_params=_cparams("arbitrary"), name=name)(a, b)


MESH = pl.DeviceIdType.MESH
ANY = pl.BlockSpec(memory_space=pl.ANY)
ROW_ALIGN = 16


def _me():
    return lax.axis_index("x"), lax.axis_index("y"), lax.axis_index("c")


def _other_chips(x, y):
    return [(1 - x, y), (x, 1 - y), (1 - x, 1 - y)]


def _remote(src, dst, send_sem, recv_sem, to):
    return pltpu.make_async_remote_copy(src_ref=src, dst_ref=dst, send_sem=send_sem, recv_sem=recv_sem,
                                        device_id=to, device_id_type=MESH)


def _half(c, rows):
    return pl.ds(pl.multiple_of(c * (rows // 2), ROW_ALIGN), rows // 2)


def ag_ride(bufs):
    n = len(bufs)

    def copies(outs, sems):
        ici_send, ici_recv, d2d_send, d2d_recv = sems
        x, y, c = _me()
        sib = (x, y, 1 - c)
        ici, d2d, d2d_in = [], [], []
        for i in range(n):
            rows = outs[i].shape[1]
            mine = outs[i].at[2 * x + y, _half(c, rows)]
            for j, chip in enumerate(_other_chips(x, y)):
                ici.append(_remote(mine, mine, ici_send.at[i, j], ici_recv.at[i, j], (*chip, c)))
                landed = outs[i].at[2 * chip[0] + chip[1], _half(c, rows)]
                d2d.append((_remote(landed, landed, ici_send.at[i, j], ici_recv.at[i, j], (*chip, c)),
                            _remote(landed, landed, d2d_send.at[i, j], d2d_recv.at[i, j], sib)))
                lands = outs[i].at[2 * chip[0] + chip[1], _half(1 - c, rows)]
                d2d_in.append(_remote(lands, lands, d2d_send.at[i, j], d2d_recv.at[i, j], sib))
        return ici, d2d, d2d_in

    def start(ins, outs, sems):
        for cp in copies(outs, sems)[0]:
            cp.start()

    def finish(ins, outs, sems):
        ici, d2d, d2d_in = copies(outs, sems)
        for arrived, forward in d2d:
            arrived.wait_recv()
            forward.start()
        for cp in d2d_in:
            cp.wait_recv()
        for cp in ici + [forward for _, forward in d2d]:
            cp.wait_send()

    return Ride(bufs, [jax.ShapeDtypeStruct(b.shape, b.dtype) for b in bufs], [(i, i) for i in range(n)],
                [pltpu.SemaphoreType.DMA((n, 3))] * 4, start, finish)


def pair_ride(grads):
    n = len(grads)

    def copies(ins, outs, sems):
        x, y, c = _me()
        return [_remote(ins[i].at[:, _half(1 - c, ins[i].shape[1]), :], outs[i], sems[0].at[i], sems[1].at[i], (x, y, 1 - c))
                for i in range(n)]

    def start(ins, outs, sems):
        for cp in copies(ins, outs, sems):
            cp.start()

    def finish(ins, outs, sems):
        for cp in copies(ins, outs, sems):
            cp.wait()

    return Ride(grads, [jax.ShapeDtypeStruct((N_SHARD, g.shape[1] // 2, g.shape[2]), g.dtype) for g in grads], [],
                [pltpu.SemaphoreType.DMA((n,))] * 2, start, finish)


def rs_add(grad, part, c, *, rt, name):
    _, rows, cols = grad.shape
    r2 = rows // 2
    nrb = r2 // rt

    def body(c_ref, g_ref, p_ref, o_ref):
        o_ref[...] = (g_ref[...] + p_ref[...]).astype(BF16)

    return pl.pallas_call(
        body,
        grid_spec=pltpu.PrefetchScalarGridSpec(
            num_scalar_prefetch=1, grid=(N_SHARD, nrb),
            in_specs=[pl.BlockSpec((1, rt, cols), lambda k, i, c_ref: (k, c_ref[1] * nrb + i, 0)),
                      pl.BlockSpec((1, rt, cols), lambda k, i, c_ref: (k, i, 0))],
            out_specs=pl.BlockSpec((1, rt, cols), lambda k, i, c_ref: (k, i, 0))),
        out_shape=jax.ShapeDtypeStruct((N_SHARD, r2, cols), BF16),
        compiler_params=_cparams("parallel", "parallel"), name=name)(c, grad, part)


def chips_ride(sums):
    n = len(sums)

    def copies(ins, outs, sems):
        send, recv = sems
        x, y, c = _me()
        return [_remote(ins[i].at[2 * chip[0] + chip[1]], outs[i].at[2 * x + y], send.at[i, j], recv.at[i, j], (*chip, c))
                for i in range(n) for j, chip in enumerate(_other_chips(x, y))]

    def start(ins, outs, sems):
        for cp in copies(ins, outs, sems):
            cp.start()

    def finish(ins, outs, sems):
        for cp in copies(ins, outs, sems):
            cp.wait()

    return Ride(sums, [jax.ShapeDtypeStruct(s.shape, s.dtype) for s in sums], [],
                [pltpu.SemaphoreType.DMA((n, 3))] * 2, start, finish)


def rs_total(parts, own, where, *, rt, name):
    _, r2, cols = parts.shape
    nrb = r2 // rt

    def body(w_ref, p0, p1, p2, p3, own_ref, o_ref):
        s_me = w_ref[0]
        acc = None
        for k, p in enumerate((p0, p1, p2, p3)):
            term = jnp.where(s_me == k, own_ref[0], p[0]).astype(F32)
            acc = term if acc is None else acc + term
        o_ref[...] = acc

    def slot(k):
        return pl.BlockSpec((1, rt, cols), lambda i, w: (jnp.where(w[0] == k, (k + 1) % N_SHARD, k), i, 0))

    return pl.pallas_call(
        body,
        grid_spec=pltpu.PrefetchScalarGridSpec(
            num_scalar_prefetch=1, grid=(nrb,),
            in_specs=[slot(0), slot(1), slot(2), slot(3), pl.BlockSpec((1, rt, cols), lambda i, w: (w[0], i, 0))],
            out_specs=pl.BlockSpec((rt, cols), lambda i, w: (w[1] * nrb + i, 0))),
        out_shape=jax.ShapeDtypeStruct((2 * r2, cols), F32),
        compiler_params=_cparams("parallel"), name=name)(where, parts, parts, parts, parts, own)


def share_ride(totals):
    n = len(totals)

    def halves(outs, sems):
        x, y, c = _me()
        mine = [outs[i].at[_half(c, outs[i].shape[0])] for i in range(n)]
        other = [outs[i].at[_half(1 - c, outs[i].shape[0])] for i in range(n)]
        return ([_remote(m, m, sems[0].at[i], sems[1].at[i], (x, y, 1 - c)) for i, m in enumerate(mine)],
                [_remote(o, o, sems[0].at[i], sems[1].at[i], (x, y, 1 - c)) for i, o in enumerate(other)])

    def start(ins, outs, sems):
        for cp in halves(outs, sems)[0]:
            cp.start()

    def finish(ins, outs, sems):
        sent, landing = halves(outs, sems)
        for cp in landing:
            cp.wait_recv()
        for cp in sent:
            cp.wait_send()

    return Ride(totals, [jax.ShapeDtypeStruct(t.shape, t.dtype) for t in totals], [(i, i) for i in range(n)],
                [pltpu.SemaphoreType.DMA((n,))] * 2, start, finish)


def small_allreduce(buf, *, name):
    rows = buf.shape[0]

    def body(x_ref, o_ref, slots, send, recv):
        x, y, c = _me()
        me = 4 * x + 2 * y + c
        slots[me] = x_ref[...]
        sent = []
        for d in range(1, 8):
            peer = (1 - x if d & 4 else x, 1 - y if d & 2 else y, 1 - c if d & 1 else c)
            sent.append(_remote(x_ref, slots.at[me], send.at[d - 1], recv.at[d - 1], peer))
            sent[-1].start()
        for cp in sent:
            cp.wait()
        acc = slots[0]
        for k in range(1, 8):
            acc = acc + slots[k]
        o_ref[...] = acc

    return pl.pallas_call(
        body, out_shape=jax.ShapeDtypeStruct(buf.shape, F32),
        in_specs=[pl.BlockSpec(memory_space=pltpu.VMEM)], out_specs=pl.BlockSpec(memory_space=pltpu.VMEM),
        scratch_shapes=[pltpu.VMEM((8, rows, 128), F32), pltpu.SemaphoreType.DMA((7,)), pltpu.SemaphoreType.DMA((7,))],
        name=name)(buf)


def adamw(w, g, m, v, *, name, rt=None):
    rows, cols = w.shape
    rt = rows if rt is None else rt
    c1 = 1.0 - ADAM_B1 ** ADAM_STEP
    c2 = 1.0 - ADAM_B2 ** ADAM_STEP

    def body(w_ref, g_ref, m_ref, v_ref, d_ref, nm_ref, nv_ref):
        gg = g_ref[...]
        nm = ADAM_B1 * m_ref[...] + (1.0 - ADAM_B1) * gg
        nv = ADAM_B2 * v_ref[...] + (1.0 - ADAM_B2) * (gg * gg)
        nm_ref[...] = nm
        nv_ref[...] = nv
        d_ref[...] = -ADAM_LR * ((nm / c1) / (jnp.sqrt(nv / c2) + ADAM_EPS) + ADAM_WD * w_ref[...])

    spec = pl.BlockSpec((rt, cols), lambda i: (i, 0))
    return pl.pallas_call(
        body, grid=(rows // rt,), in_specs=[spec] * 4, out_specs=[spec] * 3,
        out_shape=[jax.ShapeDtypeStruct((rows, cols), F32)] * 3,
        compiler_params=_cparams("parallel"), name=name)(w, g, m, v)


WEIGHTS = ['ffn1_pre_g', 'ffn1_w_gate', 'ffn1_w_up', 'ffn1_w_down', 'ffn1_post_g', 'mix_pre_g', 'w_in', 'conv_w',
           'conv_b', 'dt_bias', 'a_log', 'd_skip', 'ssm_norm_g', 'w_ssm_proj', 'attn_sinks', 'rel_bias_table',
           'w_attn_proj', 'w_out', 'mix_post_g', 'ffn2_pre_g', 'ffn2_w_gate', 'ffn2_w_up', 'ffn2_w_down', 'ffn2_post_g']
BIG = ['ffn1_w_gate', 'ffn1_w_up', 'ffn1_w_down', 'w_in', 'w_ssm_proj', 'w_attn_proj', 'w_out',
       'ffn2_w_gate', 'ffn2_w_up', 'ffn2_w_down']
SMALL = [w for w in WEIGHTS if w not in BIG]


def _bucket_onehot():
    blk = ATTN_BLOCK
    dist = np.maximum(np.arange(blk)[:, None] + blk - np.arange(2 * blk)[None, :], 0)
    max_exact = REL_BUCKETS // 2
    d = np.maximum(dist, 1).astype(np.float32)
    large = max_exact + (np.log(d / np.float32(max_exact)) / np.float32(math.log(REL_MAX_DISTANCE / max_exact))
                         * np.float32(REL_BUCKETS - max_exact)).astype(np.int32)
    bucket = np.where(dist < max_exact, dist, np.minimum(large, REL_BUCKETS - 1)).reshape(-1)
    return jnp.asarray((bucket[None, :] == np.arange(REL_BUCKETS)[:, None]).astype(np.float32))


def _pack_rows(parts, mult=8):
    flat = jnp.concatenate([p.reshape(-1).astype(F32) for p in parts])
    rows = -(-flat.shape[0] // (128 * mult)) * mult
    return jnp.pad(flat, (0, rows * 128 - flat.shape[0])).reshape(rows, 128)


def _unpack_rows(buf, shapes):
    flat = buf.reshape(-1)
    out, at = [], 0
    for shp in shapes:
        size = int(np.prod(shp))
        out.append(flat[at:at + size].reshape(shp))
        at += size
    return out


def kernel(x, ffn1_pre_g, ffn1_w_gate, ffn1_w_up, ffn1_w_down, ffn1_post_g, mix_pre_g, w_in, conv_w, conv_b, dt_bias, a_log, d_skip, ssm_norm_g, w_ssm_proj, attn_sinks, rel_bias_table, w_attn_proj, w_out, mix_post_g, ffn2_pre_g, ffn2_w_gate, ffn2_w_up, ffn2_w_down, ffn2_post_g, loss_target, m_ffn1_pre_g, m_ffn1_w_gate, m_ffn1_w_up, m_ffn1_w_down, m_ffn1_post_g, m_mix_pre_g, m_w_in, m_conv_w, m_conv_b, m_dt_bias, m_a_log, m_d_skip, m_ssm_norm_g, m_w_ssm_proj, m_attn_sinks, m_rel_bias_table, m_w_attn_proj, m_w_out, m_mix_post_g, m_ffn2_pre_g, m_ffn2_w_gate, m_ffn2_w_up, m_ffn2_w_down, m_ffn2_post_g, v_ffn1_pre_g, v_ffn1_w_gate, v_ffn1_w_up, v_ffn1_w_down, v_ffn1_post_g, v_mix_pre_g, v_w_in, v_conv_w, v_conv_b, v_dt_bias, v_a_log, v_d_skip, v_ssm_norm_g, v_w_ssm_proj, v_attn_sinks, v_rel_bias_table, v_w_attn_proj, v_w_out, v_mix_post_g, v_ffn2_pre_g, v_ffn2_w_gate, v_ffn2_w_up, v_ffn2_w_down, v_ffn2_post_g):
    args = locals()
    w = {n: args[n] for n in WEIGHTS}
    m = {n: args["m_" + n] for n in WEIGHTS}
    v = {n: args["v_" + n] for n in WEIGHTS}
    batch, seq, D = x.shape
    T = batch * seq
    xi, yi, ci = _me()
    s_me = 2 * xi + yi
    x2 = x.reshape(T, D)
    tgt = loss_target.reshape(T, D)

    def own_slot(parts):
        p = jnp.concatenate([t[0] for t in parts], axis=0).astype(BF16)
        return lax.dynamic_update_slice(lax.empty((N_SHARD,) + p.shape, BF16), p[None], (s_me, 0, 0))

    tr = lambda a: jnp.swapaxes(a, -1, -2)
    (wffn1,) = run_ride(ag_ride([own_slot([tr(ffn1_w_gate), tr(ffn1_w_up), ffn1_w_down])]), name="ag_ffn1")
    col = lambda v: v.reshape(SSM_HEADS, 1)
    d_skip_x = jnp.repeat(d_skip, SSM_HEAD_DIM, axis=1)
    cw_slot = lax.dynamic_update_slice(jnp.zeros((SSM_CONV, SSM_CONV_DIM), F32),
                                       conv_w[0] * (ci == 0).astype(F32), (0, s_me * (SSM_CONV_DIM // N_SHARD)))
    conv_w_full = small_allreduce(cw_slot.reshape(-1, 128), name="ag_conv_w").reshape(SSM_CONV, SSM_CONV_DIM)
    cwb = jnp.concatenate([conv_w_full, conv_b, jnp.zeros((HALO - SSM_CONV - 1, SSM_CONV_DIM), F32)], axis=0)
    conv_pack = jnp.stack([jnp.concatenate([cwb[:, SSM_GW * g:SSM_GW * (g + 1)],
                                            cwb[:, SSM_D_INNER + SSM_STATE * g:SSM_D_INNER + SSM_STATE * (g + 1)],
                                            cwb[:, SSM_D_INNER + SSM_STATE * (SSM_GROUPS + g):
                                                SSM_D_INNER + SSM_STATE * (SSM_GROUPS + g + 1)]], axis=1)
                           for g in range(SSM_GROUPS)])
    vec_pack = jnp.concatenate([d_skip_x, ssm_norm_g], axis=0)
    col_pack = jnp.concatenate([col(dt_bias), col(a_log)], axis=1)

    (h1, n1, gate1, up1, f1), (gin, gmix) = ffn_fwd(
        x2, ffn1_pre_g, wffn1, ffn1_post_g, name="ffn1_fwd",
        ride=ag_ride([own_slot([w_in]), own_slot([w_ssm_proj, w_attn_proj, w_out])]))
    w_in_full = gin.transpose(1, 0, 2).reshape(D, IN_COLS)
    w_gz = w_in_full[:, 0:4096]
    w_xbc = w_in_full[:, 4096:4096 + SSM_CONV_DIM]
    w_dtT = w_in_full[:, 7168:7200].T
    w_qkv = w_in_full[:, 7200:]
    (u, gates, z, xbc, dt_rawT, q, k, vv), (wffn2,) = mix_in_fwd(
        h1, mix_pre_g, w_gz, w_xbc, w_dtT, w_qkv, name="mix_in_fwd",
        ride=ag_ride([own_slot([tr(ffn2_w_gate), tr(ffn2_w_up), ffn2_w_down])]))
    scalars = ssd_scalars(dt_rawT, col(dt_bias), col(a_log), name="ssd_scalars")
    y, ys, states = ssd_fwd(xbc, z, scalars, conv_pack, vec_pack, batch=batch, name="ssd_fwd")
    onehot = _bucket_onehot()
    bias = attn_bias(rel_bias_table.T, onehot, name="attn_bias").reshape(ATTN_Q_HEADS, ATTN_BLOCK, 2 * ATTN_BLOCK)
    bias = jnp.where(attn_window()[None], bias, MASKED)
    o, lse = attn_fwd(q, k, vv, bias, attn_sinks, batch=batch, name="attn_fwd")
    h2, y_ssm, y_attn, mix, merged = mix_out_fwd(ys, o, gates, h1, gmix, mix_post_g, name="mix_out_fwd")
    h3, n3, gate2, up2, f2, dy, loss_parts = ffn_fwd(h2, ffn2_pre_g, wffn2, ffn2_post_g, tgt, name="ffn2_fwd")

    where = jnp.stack([s_me, ci]).astype(jnp.int32)

    def chip_sums(grads, pair, tiles, tag):
        return [rs_add(g, p, where, rt=rt, name=f"rs_add_{tag}{i}") for i, (g, p, rt) in enumerate(zip(grads, pair, tiles))]

    def totals(parts, sums, tiles, tag):
        return [rs_total(p, s, where, rt=rt, name=f"rs_total_{tag}{i}")
                for i, (p, s, rt) in enumerate(zip(parts, sums, tiles))]

    def ffn_grads(n, dgate, dup, a, df, tag, ride=None):
        d = mm_tn(dgate, n[None], into=(lax.empty(wffn1.shape, F32), 0), name="dw_gate" + tag, ride=ride)
        d, exchanged = d if ride is not None else (d, None)
        d = mm_tn(dup, n[None], into=(d, 1), name="dw_up" + tag)
        return [mm_tn(a, df[None], into=(d, 2), name="dw_down" + tag)], exchanged

    ffn_tiles, mix_tiles = [352], [256, 256]
    dh2, df2, a2, dgate2, dup2, dg_ffn2_pre, dg_ffn2_post = ffn_bwd(dy, h2, f2, gate2, up2, ffn2_pre_g, ffn2_post_g,
                                                                    wffn2, name="ffn2_bwd")
    d_f2, _ = ffn_grads(n3, dgate2, dup2, a2, df2, "2")
    (dmix, dyssm, dyattn, dgates, dys, do, dg_mix_post), pair_f2 = mix_out_bwd(
        dh2, mix, y_ssm, y_attn, gates, gmix, mix_post_g, name="mix_out_bwd", ride=pair_ride(d_f2))
    sums_f2 = chip_sums(d_f2, pair_f2, ffn_tiles, "f2")
    dq, dk, dv, dbias, dsinks = attn_bwd(q, k, vv, o, do, lse, bias, attn_sinks, batch=batch, name="attn_bwd")
    dtable = attn_bias_bwd(dbias.reshape(ATTN_Q_HEADS, -1), onehot, name="attn_bias_bwd").T
    (dz, dxs, dbm, dcm, ddtT, acc_xs, acc_bm, acc_cm, acc_head), parts_f2 = ssd_bwd(
        dys, y, xbc, z, dt_rawT, states, scalars, conv_pack, col_pack, vec_pack,
        batch=batch, name="ssd_bwd", ride=chips_ride(sums_f2))
    tot_f2 = totals(parts_f2, sums_f2, ffn_tiles, "f2")
    dmx = mm_tn(ys[None], dyssm[None], a_cols=(N_SHARD, 512), into=(lax.empty(gmix.shape, F32), 0), name="dw_ssm")
    dmx = mm_tn(o[None], dyattn[None], a_cols=(N_SHARD, 256), into=(dmx, 2), name="dw_attn")
    dmx = mm_tn(merged[None], dmix[None], a_cols=(N_SHARD, 256), into=(dmx, 3), name="dw_out")
    ub = u[None]
    din = jnp.concatenate([
        mm_tn(ub, dgates[None], name="dw_in_gates", tn=1024)[0], mm_tn(ub, dz[None], name="dw_in_z", tn=1024)[0],
        mm_tn(ub, dxs[None], name="dw_in_xs", tn=1024)[0], mm_tn(ub, dbm[None], name="dw_in_b")[0],
        mm_tn(ub, dcm[None], name="dw_in_c")[0], mm_rows(ddtT, u, name="dw_in_dt").T,
        mm_tn(ub, dq[None], name="dw_in_q")[0], mm_tn(ub, dk[None], name="dw_in_k")[0],
        mm_tn(ub, dv[None], name="dw_in_v")[0]], axis=1)
    din = din.reshape(D, N_SHARD, IN_COLS // N_SHARD).transpose(1, 0, 2)
    d_mx = [dmx, din]
    (dh1, dg_mix_pre), (pair_mx0, pair_mx1, rffn2) = mix_in_bwd(
        dh2, h1, mix_pre_g, dgates, dz, dxs, dbm, dcm, ddtT, dq, dk, dv, w_gz, w_xbc, w_dtT, w_qkv, name="mix_in_bwd",
        ride=join_rides(pair_ride(d_mx), share_ride(tot_f2)))
    sums_mx = chip_sums(d_mx, [pair_mx0, pair_mx1], mix_tiles, "mx")
    (dx, df1, a1, dgate1, dup1, dg_ffn1_pre, dg_ffn1_post), parts_mx = ffn_bwd(
        dh1, x2, f1, gate1, up1, ffn1_pre_g, ffn1_post_g, wffn1, name="ffn1_bwd", ride=chips_ride(sums_mx))
    d_f1, (rmx, rin) = ffn_grads(n1, dgate1, dup1, a1, df1, "1",
                                 ride=share_ride(totals(parts_mx, sums_mx, mix_tiles, "mx")))
    sums_f1 = chip_sums(d_f1, run_ride(pair_ride(d_f1), name="rs_pair_f1"), ffn_tiles, "f1")
    parts_f1 = run_ride(chips_ride(sums_f1), name="rs_chips_f1")
    (rffn1,) = run_ride(share_ride(totals(parts_f1, sums_f1, ffn_tiles, "f1")), name="rs_share_f1")
    FS = D_FF // N_SHARD
    gw = {
        'ffn1_w_gate': rffn1[0:FS], 'ffn1_w_up': rffn1[FS:2 * FS], 'ffn1_w_down': rffn1[2 * FS:],
        'ffn2_w_gate': rffn2[0:FS], 'ffn2_w_up': rffn2[FS:2 * FS], 'ffn2_w_down': rffn2[2 * FS:],
        'w_ssm_proj': rmx[0:512], 'w_attn_proj': rmx[512:768], 'w_out': rmx[768:1024], 'w_in': rin,
    }

    dconv_w = jnp.concatenate([acc[:, :SSM_CONV].transpose(1, 0, 2).reshape(SSM_CONV, -1)
                               for acc in (acc_xs, acc_bm, acc_cm)], axis=1)
    dconv_b = jnp.concatenate([acc[:, SSM_CONV].reshape(-1) for acc in (acc_xs, acc_bm, acc_cm)])
    small_local = {
        'ffn1_pre_g': dg_ffn1_pre, 'ffn1_post_g': dg_ffn1_post, 'mix_pre_g': dg_mix_pre, 'conv_w': dconv_w,
        'conv_b': dconv_b, 'dt_bias': acc_head[:, :, 0], 'a_log': acc_head[:, :, 1],
        'd_skip': acc_xs[:, SSM_CONV + 2].reshape(SSM_HEADS, SSM_HEAD_DIM).sum(axis=1),
        'ssm_norm_g': acc_xs[:, SSM_CONV + 1].reshape(-1), 'attn_sinks': dsinks, 'rel_bias_table': dtable,
        'mix_post_g': dg_mix_post, 'ffn2_pre_g': dg_ffn2_pre, 'ffn2_post_g': dg_ffn2_post,
    }
    full_shapes = [(SSM_CONV, SSM_CONV_DIM) if n == 'conv_w' else w[n].shape for n in SMALL]
    packed = _pack_rows([small_local[n] for n in SMALL] + [jnp.sum(loss_parts[:, 0, 0])])
    total = small_allreduce(packed, name="allreduce_small")
    *small_g, loss = _unpack_rows(total, full_shapes + [()])
    for n, g in zip(SMALL, small_g):
        gw[n] = g
    gw['conv_w'] = lax.dynamic_slice(gw['conv_w'], (0, s_me * (SSM_CONV_DIM // N_SHARD)),
                                     (SSM_CONV, SSM_CONV_DIM // N_SHARD))[None]

    delta, new_m, new_v = {}, {}, {}
    for n in BIG:
        lay = tr if n.endswith(('w_gate', 'w_up')) else (lambda a: a)
        d_, m_, v_ = adamw(lay(w[n][0]), gw[n], lay(m[n][0]), lay(v[n][0]), name="adamw_" + n, rt=gw[n].shape[0] // 4)
        gw[n] = lay(gw[n])[None]
        delta[n], new_m[n], new_v[n] = lay(d_)[None], lay(m_)[None], lay(v_)[None]
    shapes = [w[n].shape for n in SMALL]
    outs = adamw(_pack_rows([w[n] for n in SMALL]), _pack_rows([gw[n] for n in SMALL]),
                 _pack_rows([m[n] for n in SMALL]), _pack_rows([v[n] for n in SMALL]), name="adamw_small")
    for res, buf in zip((delta, new_m, new_v), outs):
        for n, val in zip(SMALL, _unpack_rows(buf, shapes)):
            res[n] = val
    return (loss, dx.reshape(batch, seq, D), *[gw[n].reshape(w[n].shape) for n in WEIGHTS],
            *[delta[n] for n in WEIGHTS], *[new_m[n] for n in WEIGHTS], *[new_v[n] for n in WEIGHTS])
```
